```python
import jax, jax.numpy as jnp
from jax import lax
import numpy as np

D_MODEL = 2048
BATCH = 1
SEQ = 8192
DEPTH = 1

MEM_LEN = 256
CHUNK = 64
RET_HEADS = 4
RET_DK = 256
RET_DV = 256
GLA_HEADS = 4
GLA_DK = 128
GLA_DV = 256
GLA_LOWRANK = 16
GLA_TAU = 16.0
MIX_WIDTH = RET_HEADS * RET_DV + GLA_HEADS * GLA_DV
ROPE_BASE = 10000.0
MEM_HEADS = 4
MEM_HEAD_DIM = D_MODEL // MEM_HEADS
N_GROUPS = 4
EXPERTS_PER_GROUP = 8
N_EXPERTS = N_GROUPS * EXPERTS_PER_GROUP
EXPERT_FF = 512
FINE_TOP_K = 2
MOE_BLOCK = 128
LN_EPS = 1e-5
DEEPNORM_ALPHA = (2 * DEPTH) ** 0.25
DEEPNORM_BETA = (8 * DEPTH) ** -0.25
IN_SIZES = (RET_HEADS * RET_DK, RET_HEADS * RET_DK, RET_HEADS * RET_DV, RET_HEADS * RET_DV,
            GLA_HEADS * GLA_DK, GLA_HEADS * GLA_DK, GLA_HEADS * GLA_DV, GLA_HEADS * GLA_DV,
            GLA_LOWRANK)
IN_WIDTH = sum(IN_SIZES)

kernel_name = "hymba_retnet_gla_hmoe_deepnorm"


def layer_norm(x, g, b):
    xf = x.astype(jnp.float32)
    mu = xf.mean(-1, keepdims=True)
    var = jnp.mean(jnp.square(xf - mu), -1, keepdims=True)
    y = (xf - mu) * lax.rsqrt(var + LN_EPS) * g.astype(jnp.float32) + b.astype(jnp.float32)
    return y.astype(x.dtype)


def head_group_norm(t):
    mu = t.mean(-1, keepdims=True)
    var = jnp.mean(jnp.square(t - mu), -1, keepdims=True)
    return (t - mu) * lax.rsqrt(var + LN_EPS)


def head_rms_norm(t, g):
    return t * lax.rsqrt(jnp.mean(jnp.square(t), -1, keepdims=True) + LN_EPS) * g.astype(jnp.float32)


def rotary(t, positions):
    half = t.shape[-1] // 2
    inv_freq = ROPE_BASE ** (-jnp.arange(half, dtype=jnp.float32) / half)
    ang = positions.astype(jnp.float32)[:, :, None, None] * inv_freq
    cos, sin = jnp.cos(ang), jnp.sin(ang)
    t1, t2 = t[..., :half], t[..., half:]
    return jnp.concatenate([t1 * cos - t2 * sin, t1 * sin + t2 * cos], axis=-1)


def to_chunks(t):
    B, S, H, d = t.shape
    return t.reshape(B, S // CHUNK, CHUNK, H, d).transpose(1, 0, 3, 2, 4)


def from_chunks(t):
    N, B, H, C, d = t.shape
    return t.transpose(1, 0, 3, 2, 4).reshape(B, N * C, H, d)


def retention_chunkwise(q, k, v):
    B, S, H, dk = q.shape
    dv = v.shape[-1]
    log_gamma = jnp.log1p(-jnp.exp2(-5.0 - jnp.arange(H, dtype=jnp.float32)))
    n = jnp.arange(CHUNK, dtype=jnp.float32)
    rel = n[:, None] - n[None, :]
    causal = rel >= 0
    decay_intra = jnp.where(causal, jnp.exp(log_gamma[:, None, None] * jnp.maximum(rel, 0.0)), 0.0)
    decay_q = jnp.exp(log_gamma[:, None] * (n + 1.0))[..., None]
    decay_k = jnp.exp(log_gamma[:, None] * (CHUNK - 1.0 - n))[..., None]
    decay_chunk = jnp.exp(log_gamma * CHUNK)[:, None, None]

    def step(R, inp):
        qc, kc, vc = inp
        scores = jnp.einsum('bhid,bhjd->bhij', qc, kc) * decay_intra
        o = (jnp.einsum('bhij,bhje->bhie', scores, vc)
             + jnp.einsum('bhid,bhde->bhie', qc * decay_q, R))
        R = decay_chunk * R + jnp.einsum('bhjd,bhje->bhde', kc * decay_k, vc)
        return R, o

    R0 = jnp.zeros((B, H, dk, dv), jnp.float32)
    _, o = lax.scan(step, R0, (to_chunks(q), to_chunks(k), to_chunks(v)))
    return from_chunks(o)


def gla_chunkwise(q, k, v, log_a):
    B, S, H, dk = q.shape
    dv = v.shape[-1]
    causal = jnp.tril(jnp.ones((CHUNK, CHUNK), bool))[:, :, None]

    def step(St, inp):
        qc, kc, vc, ac = inp
        b = jnp.cumsum(ac, axis=-2)
        diff = b[:, :, :, None, :] - b[:, :, None, :, :]
        gate = jnp.where(causal, jnp.exp(jnp.where(causal, diff, 0.0)), 0.0)
        scores = jnp.einsum('bhid,bhjd,bhijd->bhij', qc, kc, gate)
        o = (jnp.einsum('bhij,bhje->bhie', scores, vc)
             + jnp.einsum('bhid,bhde->bhie', qc * jnp.exp(b), St))
        b_last = b[:, :, -1:, :]
        St = (jnp.exp(b_last[:, :, 0, :])[..., None] * St
              + jnp.einsum('bhjd,bhje->bhde', kc * jnp.exp(b_last - b), vc))
        return St, o

    S0 = jnp.zeros((B, H, dk, dv), jnp.float32)
    _, o = lax.scan(step, S0, (to_chunks(q), to_chunks(k), to_chunks(v), to_chunks(log_a)))
    return from_chunks(o)


def hybrid_mixer(x, positions, w_in, w_gla_a2, b_gla_a, g_gla_norm, w_mix_out):
    B, S, _ = x.shape
    h = x @ w_in
    split_at = [int(i) for i in np.cumsum(IN_SIZES)[:-1]]
    rq, rk, rv, rg, gq, gk, gv, gg, glr = jnp.split(h, split_at, axis=-1)
    f32 = jnp.float32
    rq = rotary(rq.astype(f32).reshape(B, S, RET_HEADS, RET_DK), positions)
    rk = rotary(rk.astype(f32).reshape(B, S, RET_HEADS, RET_DK), positions) * RET_DK ** -0.5
    rv = rv.astype(f32).reshape(B, S, RET_HEADS, RET_DV)
    ret = head_group_norm(retention_chunkwise(rq, rk, rv)).reshape(B, S, RET_HEADS * RET_DV)
    ret = jax.nn.silu(rg.astype(f32)) * ret
    gq = gq.astype(f32).reshape(B, S, GLA_HEADS, GLA_DK) * GLA_DK ** -0.5
    gk = gk.astype(f32).reshape(B, S, GLA_HEADS, GLA_DK)
    gv = gv.astype(f32).reshape(B, S, GLA_HEADS, GLA_DV)
    log_a = (jax.nn.log_sigmoid((glr @ w_gla_a2 + b_gla_a).astype(f32)) / GLA_TAU
             ).reshape(B, S, GLA_HEADS, GLA_DK)
    gla = head_rms_norm(gla_chunkwise(gq, gk, gv, log_a), g_gla_norm).reshape(B, S, GLA_HEADS * GLA_DV)
    gla = jax.nn.silu(gg.astype(f32)) * gla
    mixed = jnp.concatenate([ret, gla], axis=-1).astype(x.dtype)
    return mixed @ w_mix_out


def memory_cross_attention(x, mem, w_mq, w_mk, w_mv, w_mo):
    B, S, _ = x.shape
    M = mem.shape[1]
    q = (x @ w_mq).reshape(B, S, MEM_HEADS, MEM_HEAD_DIM)
    k = (mem @ w_mk).reshape(B, M, MEM_HEADS, MEM_HEAD_DIM)
    v = (mem @ w_mv).reshape(B, M, MEM_HEADS, MEM_HEAD_DIM)
    s = jnp.einsum('bshd,bmhd->bhsm', q, k).astype(jnp.float32) * MEM_HEAD_DIM ** -0.5
    p = jax.nn.softmax(s, axis=-1).astype(v.dtype)
    o = jnp.einsum('bhsm,bmhd->bshd', p, v).reshape(B, S, D_MODEL)
    return o @ w_mo


def hierarchical_moe(x, w_route_group, b_route_group, w_route_expert, b_route_expert,
                     w_exp_gate, w_exp_up, w_exp_down):
    B, S, D = x.shape
    T = B * S
    xf = x.reshape(T, D)
    f32 = jnp.float32
    pg = jax.nn.softmax((xf @ w_route_group + b_route_group).astype(f32), axis=-1)
    grp = jnp.argmax(pg, axis=-1)
    pg_sel = jnp.take_along_axis(pg, grp[:, None], axis=-1)
    fine = (xf @ w_route_expert).reshape(T, N_GROUPS, EXPERTS_PER_GROUP) + b_route_expert
    fine_sel = jnp.take_along_axis(fine, grp[:, None, None], axis=1)[:, 0].astype(f32)
    top_p, top_i = lax.top_k(jax.nn.softmax(fine_sel, axis=-1), FINE_TOP_K)
    gate = pg_sel * top_p / jnp.sum(top_p, axis=-1, keepdims=True)
    expert = grp[:, None] * EXPERTS_PER_GROUP + top_i
    A = T * FINE_TOP_K
    flat_e = expert.reshape(A)
    flat_t = jnp.repeat(jnp.arange(T, dtype=jnp.int32), FINE_TOP_K)
    flat_w = gate.reshape(A)
    order = jnp.argsort(flat_e)
    se = flat_e[order]
    counts = jnp.bincount(flat_e, length=N_EXPERTS)
    padded = (counts + MOE_BLOCK - 1) // MOE_BLOCK * MOE_BLOCK
    ends = jnp.cumsum(padded)
    pstart = ends - padded
    start = jnp.cumsum(counts) - counts
    dest = pstart[se] + jnp.arange(A) - start[se]
    P = A + N_EXPERTS * MOE_BLOCK
    nb = P // MOE_BLOCK
    buf_t = jnp.full((P,), T, jnp.int32).at[dest].set(flat_t[order])
    buf_w = jnp.zeros((P,), xf.dtype).at[dest].set(flat_w[order].astype(xf.dtype))
    block_e = jnp.minimum(jnp.searchsorted(ends, jnp.arange(nb) * MOE_BLOCK, side='right'),
                          N_EXPERTS - 1)
    x_pad = jnp.concatenate([xf, jnp.zeros((1, D), xf.dtype)], axis=0)
    xb = x_pad[buf_t].reshape(nb, MOE_BLOCK, D)

    def expert_block(args):
        xblk, e = args
        hid = jax.nn.silu(xblk @ w_exp_gate[e]) * (xblk @ w_exp_up[e])
        return hid @ w_exp_down[e]

    yb = lax.map(expert_block, (xb, block_e)).reshape(P, D)
    out = jnp.zeros((T + 1, D), xf.dtype).at[buf_t].add(yb * buf_w[:, None])[:T]
    return out.reshape(B, S, D)


def setup_inputs(seed: int = 0) -> dict:
    key = jax.random.key(seed)
    ks = jax.random.split(key, 26)
    nrm = jax.random.normal
    L, D = DEPTH, D_MODEL
    offset = jax.random.randint(ks[2], (BATCH, 1), 0, 1024, dtype=jnp.int32)
    positions = offset + jnp.arange(SEQ, dtype=jnp.int32)[None, :]
    return {
        "x": nrm(ks[0], (BATCH, SEQ, D), jnp.float32),
        "mem": nrm(ks[1], (BATCH, MEM_LEN, D), jnp.float32),
        "positions": positions,
        "w_in": nrm(ks[3], (L, D, IN_WIDTH)) * D ** -0.5,
        "w_gla_a2": nrm(ks[4], (L, GLA_LOWRANK, GLA_HEADS * GLA_DK)) * GLA_LOWRANK ** -0.5,
        "b_gla_a": 0.1 * nrm(ks[5], (L, GLA_HEADS * GLA_DK)),
        "g_gla_norm": 1.0 + 0.02 * nrm(ks[6], (L, GLA_DV)),
        "w_mix_out": nrm(ks[7], (L, MIX_WIDTH, D)) * MIX_WIDTH ** -0.5 * DEEPNORM_BETA,
        "ln1_g": 1.0 + 0.02 * nrm(ks[8], (L, D)),
        "ln1_b": 0.02 * nrm(ks[9], (L, D)),
        "w_mq": nrm(ks[10], (L, D, D)) * D ** -0.5,
        "w_mk": nrm(ks[11], (L, D, D)) * D ** -0.5,
        "w_mv": nrm(ks[12], (L, D, D)) * D ** -0.5,
        "w_mo": nrm(ks[13], (L, D, D)) * D ** -0.5 * DEEPNORM_BETA,
        "ln2_g": 1.0 + 0.02 * nrm(ks[14], (L, D)),
        "ln2_b": 0.02 * nrm(ks[15], (L, D)),
        "w_route_group": nrm(ks[16], (L, D, N_GROUPS)) * D ** -0.5,
        "b_route_group": 0.01 * nrm(ks[17], (L, N_GROUPS)),
        "w_route_expert": nrm(ks[18], (L, D, N_EXPERTS)) * D ** -0.5,
        "b_route_expert": 0.01 * nrm(ks[19], (L, N_GROUPS, EXPERTS_PER_GROUP)),
        "w_exp_gate": nrm(ks[20], (L, N_EXPERTS, D, EXPERT_FF)) * D ** -0.5,
        "w_exp_up": nrm(ks[21], (L, N_EXPERTS, D, EXPERT_FF)) * D ** -0.5,
        "w_exp_down": nrm(ks[22], (L, N_EXPERTS, EXPERT_FF, D)) * EXPERT_FF ** -0.5 * DEEPNORM_BETA,
        "ln3_g": 1.0 + 0.02 * nrm(ks[23], (L, D)),
        "ln3_b": 0.02 * nrm(ks[24], (L, D)),
    }


def reference(x, mem, positions, w_in, w_gla_a2, b_gla_a, g_gla_norm, w_mix_out, ln1_g, ln1_b,
              w_mq, w_mk, w_mv, w_mo, ln2_g, ln2_b, w_route_group, b_route_group,
              w_route_expert, b_route_expert, w_exp_gate, w_exp_up, w_exp_down, ln3_g, ln3_b):
    h = x
    for l in range(DEPTH):
        mix = hybrid_mixer(h, positions, w_in[l], w_gla_a2[l], b_gla_a[l], g_gla_norm[l], w_mix_out[l])
        h = layer_norm(DEEPNORM_ALPHA * h + mix, ln1_g[l], ln1_b[l])
        cross = memory_cross_attention(h, mem, w_mq[l], w_mk[l], w_mv[l], w_mo[l])
        h = layer_norm(DEEPNORM_ALPHA * h + cross, ln2_g[l], ln2_b[l])
        ffn = hierarchical_moe(h, w_route_group[l], b_route_group[l], w_route_expert[l],
                               b_route_expert[l], w_exp_gate[l], w_exp_up[l], w_exp_down[l])
        h = layer_norm(DEEPNORM_ALPHA * h + ffn, ln3_g[l], ln3_b[l])
    return h
```

```python
import functools
import math

import jax
import jax.numpy as jnp
from jax import lax
from jax.experimental import pallas as pl
from jax.experimental.pallas import tpu as pltpu

F32 = jnp.float32
BF16 = jnp.bfloat16

D_MODEL = 2048
MEM_LEN = 256
RET_HEADS = 4
RET_DK = 256
RET_DV = 256
GLA_HEADS = 4
GLA_DK = 128
GLA_DV = 256
GLA_LOWRANK = 16
GLA_TAU = 16.0
ROPE_BASE = 10000.0
MEM_HEADS = 4
MEM_HEAD_DIM = D_MODEL // MEM_HEADS
N_GROUPS = 4
EXPERTS_PER_GROUP = 8
N_EXPERTS = N_GROUPS * EXPERTS_PER_GROUP
EXPERT_FF = 512
LN_EPS = 1e-5
DEPTH = 1
DEEPNORM_ALPHA = (2 * DEPTH) ** 0.25

RQ_OFF, RK_OFF, RV_OFF, RG_OFF = 0, 1024, 2048, 3072
GQ_OFF, GK_OFF, GV_OFF, GG_OFF, GLR_OFF = 4096, 4608, 5120, 6144, 7168
IN_MAIN = 7168

LANES = 128
ROW_CHUNKS = D_MODEL // LANES
RET_CHUNK = 256
GLA_CHUNK = 64
GLA_LEVELS = 6
MOE_BLK = 256
VMEM_LIMIT = 56 * 1024 * 1024


def _cparams(sem):
    return pltpu.CompilerParams(dimension_semantics=sem, vmem_limit_bytes=VMEM_LIMIT)


def _layer_norm(y, g, b):
    mu = jnp.mean(y, axis=-1, keepdims=True)
    d = y - mu
    var = jnp.mean(d * d, axis=-1, keepdims=True)
    return d * lax.rsqrt(var + LN_EPS) * g + b


def _silu(x):
    return x / (1.0 + jnp.exp(-x))


def _dot(a, b):
    return jnp.dot(a, b, preferred_element_type=F32)


def _dot_nt(a, b):
    return lax.dot_general(a, b, (((1,), (1,)), ((), ())), preferred_element_type=F32)


def _dot_tn(a, b):
    return lax.dot_general(a, b, (((0,), (0,)), ((), ())), preferred_element_type=F32)


def _rope_kernel(pos_ref, invf_ref, cos_ref, sin_ref):
    ang = pos_ref[...].astype(F32) * invf_ref[...]
    cos_ref[...] = jnp.cos(ang)
    sin_ref[...] = jnp.sin(ang)


def _rope_table(pos_col, inv_freq):
    s = pos_col.shape[0]
    tm = min(s, 1024)
    half = inv_freq.shape[1]
    return pl.pallas_call(
        _rope_kernel,
        out_shape=(jax.ShapeDtypeStruct((s, half), F32), jax.ShapeDtypeStruct((s, half), F32)),
        grid=(s // tm,),
        in_specs=[pl.BlockSpec((tm, 1), lambda i: (i, 0)),
                  pl.BlockSpec((1, half), lambda i: (0, 0))],
        out_specs=(pl.BlockSpec((tm, half), lambda i: (i, 0)),
                   pl.BlockSpec((tm, half), lambda i: (i, 0))),
        compiler_params=_cparams(("arbitrary",)),
        name="rope_table",
    )(pos_col, inv_freq)


def _mm_kernel(x_ref, w_ref, o_ref):
    o_ref[...] = _dot(x_ref[...], w_ref[...]).astype(o_ref.dtype)


def _proj_in(xb, w_in_b):
    s = xb.shape[0]
    tm = min(s, 1024)
    tn = 512
    return pl.pallas_call(
        _mm_kernel,
        out_shape=jax.ShapeDtypeStruct((s, IN_MAIN), F32),
        grid=(s // tm, IN_MAIN // tn),
        in_specs=[pl.BlockSpec((tm, D_MODEL), lambda i, j: (i, 0)),
                  pl.BlockSpec((D_MODEL, tn), lambda i, j: (0, j))],
        out_specs=pl.BlockSpec((tm, tn), lambda i, j: (i, j)),
        compiler_params=_cparams(("arbitrary", "arbitrary")),
        name="proj_in",
    )(xb, w_in_b)


def _loga_kernel(x_ref, wlr_ref, wa2_ref, ba_ref, o_ref):
    glr = _dot(x_ref[...], wlr_ref[...])
    z = _dot(glr.astype(BF16), wa2_ref[...]) + ba_ref[...]
    log_sig = jnp.minimum(z, 0.0) - jnp.log1p(jnp.exp(-jnp.abs(z)))
    o_ref[...] = log_sig / GLA_TAU


def _gla_log_decay(xb, w_lr, w_a2, b_a):
    s = xb.shape[0]
    tm = min(s, 1024)
    n = GLA_HEADS * GLA_DK
    return pl.pallas_call(
        _loga_kernel,
        out_shape=jax.ShapeDtypeStruct((s, n), F32),
        grid=(s // tm,),
        in_specs=[pl.BlockSpec((tm, D_MODEL), lambda i: (i, 0)),
                  pl.BlockSpec((D_MODEL, LANES), lambda i: (0, 0)),
                  pl.BlockSpec((LANES, n), lambda i: (0, 0)),
                  pl.BlockSpec((1, n), lambda i: (0, 0))],
        out_specs=pl.BlockSpec((tm, n), lambda i: (i, 0)),
        compiler_params=_cparams(("arbitrary",)),
        name="gla_log_decay",
    )(xb, w_lr, w_a2, b_a)


def _rotary(t, cos, sin):
    half = t.shape[-1] // 2
    t1, t2 = t[:, :half], t[:, half:]
    return jnp.concatenate([t1 * cos - t2 * sin, t1 * sin + t2 * cos], axis=-1)


def _retention_kernel(lg_ref, q_ref, k_ref, v_ref, g_ref, cos_ref, sin_ref, o_ref, state_ref):
    h = pl.program_id(0)
    c = pl.program_id(1)
    C = q_ref.shape[0]

    @pl.when(c == 0)
    def _():
        state_ref[...] = jnp.zeros_like(state_ref)

    lg = lg_ref[h]
    cos = cos_ref[...]
    sin = sin_ref[...]
    q = _rotary(q_ref[...], cos, sin)
    k = _rotary(k_ref[...], cos, sin) * (RET_DK ** -0.5)
    v = v_ref[...].astype(BF16)

    ri = lax.broadcasted_iota(jnp.int32, (C, C), 0)
    ci = lax.broadcasted_iota(jnp.int32, (C, C), 1)
    rel = (ri - ci).astype(F32)
    decay_intra = jnp.where(ri >= ci, jnp.exp(lg * jnp.maximum(rel, 0.0)), 0.0)
    n = lax.broadcasted_iota(jnp.int32, (C, 1), 0).astype(F32)
    decay_q = jnp.exp(lg * (n + 1.0))
    decay_k = jnp.exp(lg * (C - 1.0 - n))
    decay_chunk = jnp.exp(lg * C)

    state = state_ref[...]
    scores = _dot_nt(q.astype(BF16), k.astype(BF16)) * decay_intra
    o = _dot(scores.astype(BF16), v) + _dot((q * decay_q).astype(BF16), state.astype(BF16))
    state_ref[...] = decay_chunk * state + _dot_tn((k * decay_k).astype(BF16), v)

    mu = jnp.mean(o, axis=-1, keepdims=True)
    d = o - mu
    var = jnp.mean(d * d, axis=-1, keepdims=True)
    o = d * lax.rsqrt(var + LN_EPS)
    o_ref[...] = (_silu(g_ref[...]) * o).astype(o_ref.dtype)


def _retention(h_main, cos, sin, log_gamma):
    s = h_main.shape[0]
    C = min(RET_CHUNK, s)
    w = RET_DK
    col = lambda off: (lambda h, c, lg: (c, off // w + h))
    grid_spec = pltpu.PrefetchScalarGridSpec(
        num_scalar_prefetch=1,
        grid=(RET_HEADS, s // C),
        in_specs=[pl.BlockSpec((C, w), col(RQ_OFF)),
                  pl.BlockSpec((C, w), col(RK_OFF)),
                  pl.BlockSpec((C, w), col(RV_OFF)),
                  pl.BlockSpec((C, w), col(RG_OFF)),
                  pl.BlockSpec((C, w // 2), lambda h, c, lg: (c, 0)),
                  pl.BlockSpec((C, w // 2), lambda h, c, lg: (c, 0))],
        out_specs=pl.BlockSpec((C, RET_DV), lambda h, c, lg: (c, h)),
        scratch_shapes=[pltpu.VMEM((RET_DK, RET_DV), F32)],
    )
    return pl.pallas_call(
        _retention_kernel,
        out_shape=jax.ShapeDtypeStruct((s, RET_HEADS * RET_DV), BF16),
        grid_spec=grid_spec,
        compiler_params=_cparams(("arbitrary", "arbitrary")),
        name="retention",
    )(log_gamma, h_main, h_main, h_main, h_main, cos, sin)


def _gla_decay_matrix(C):
    import numpy as np
    levels = int(math.log2(C))
    r = np.arange(C)[:, None]
    t = np.arange(C)[None, :]
    mats = []
    for l in range(levels):
        blk = C >> l
        half = blk // 2
        m = (r // blk) * blk + half - 1
        qside = (r % blk) >= half
        mats.append(np.where(qside, (t > m) & (t <= r), (t > r) & (t <= m)))
    mats.append(t <= r)
    mats.append(t > r)
    return np.concatenate(mats, axis=0).astype(np.float32)


def _gla_kernel(m_ref, q_ref, k_ref, v_ref, g_ref, la_ref, gn_ref, o_ref, state_ref):
    c = pl.program_id(1)
    C = q_ref.shape[0]
    levels = GLA_LEVELS

    @pl.when(c == 0)
    def _():
        state_ref[...] = jnp.zeros_like(state_ref)

    q = q_ref[...] * (GLA_DK ** -0.5)
    k = k_ref[...]
    v = v_ref[...].astype(BF16)
    la = la_ref[...]
    la_hi = la.astype(BF16)
    la_lo = (la - la_hi.astype(F32)).astype(BF16)
    m = m_ref[...]
    expo = jnp.exp(_dot(m, la_hi) + _dot(m, la_lo))

    ri = lax.broadcasted_iota(jnp.int32, (C, C), 0)
    ci = lax.broadcasted_iota(jnp.int32, (C, C), 1)
    xor = jnp.where(ri > ci, ri ^ ci, 0)
    row = lax.broadcasted_iota(jnp.int32, (C, 1), 0)

    qb = q.astype(BF16)
    kb = k.astype(BF16)
    scores = jnp.where(ri == ci, _dot_nt(qb, kb), 0.0)
    for l in range(levels):
        half = C >> (l + 1)
        shift = int(math.log2(half))
        qside = (row & half) != 0
        x = (jnp.where(qside, q, k) * expo[l * C:(l + 1) * C]).astype(BF16)
        scores = scores + jnp.where((xor >> shift) == 1, _dot_nt(x, x), 0.0)

    e_b = expo[levels * C:(levels + 1) * C]
    e_rev = expo[(levels + 1) * C:(levels + 2) * C]
    e_last = e_b[C - 1:C, :]

    state = state_ref[...]
    o = _dot(scores.astype(BF16), v) + _dot_nt((q * e_b).astype(BF16), state.astype(BF16))
    state_ref[...] = state * e_last + _dot_tn(v, (k * e_rev).astype(BF16))

    o = o * lax.rsqrt(jnp.mean(o * o, axis=-1, keepdims=True) + LN_EPS) * gn_ref[...]
    o_ref[...] = (_silu(g_ref[...]) * o).astype(o_ref.dtype)


def _gla(h_main, log_a, g_norm):
    s = h_main.shape[0]
    C = GLA_CHUNK
    m = jnp.asarray(_gla_decay_matrix(C), dtype=BF16)
    nrow = m.shape[0]
    return pl.pallas_call(
        _gla_kernel,
        out_shape=jax.ShapeDtypeStruct((s, GLA_HEADS * GLA_DV), BF16),
        grid=(GLA_HEADS, s // C),
        in_specs=[pl.BlockSpec((nrow, C), lambda h, c: (0, 0)),
                  pl.BlockSpec((C, GLA_DK), lambda h, c: (c, GQ_OFF // GLA_DK + h)),
                  pl.BlockSpec((C, GLA_DK), lambda h, c: (c, GK_OFF // GLA_DK + h)),
                  pl.BlockSpec((C, GLA_DV), lambda h, c: (c, GV_OFF // GLA_DV + h)),
                  pl.BlockSpec((C, GLA_DV), lambda h, c: (c, GG_OFF // GLA_DV + h)),
                  pl.BlockSpec((C, GLA_DK), lambda h, c: (c, h)),
                  pl.BlockSpec((1, GLA_DV), lambda h, c: (0, 0))],
        out_specs=pl.BlockSpec((C, GLA_DV), lambda h, c: (c, h)),
        scratch_shapes=[pltpu.VMEM((GLA_DV, GLA_DK), F32)],
        compiler_params=_cparams(("arbitrary", "arbitrary")),
        name="gla",
    )(m, h_main, h_main, h_main, h_main, log_a, g_norm)


def _mixout_kernel(ret_ref, gla_ref, x_ref, w_ref, g_ref, b_ref, o_ref):
    nr = ret_ref.shape[1]
    mix = _dot(ret_ref[...], w_ref[:nr, :]) + _dot(gla_ref[...], w_ref[nr:, :])
    o_ref[...] = _layer_norm(DEEPNORM_ALPHA * x_ref[...] + mix, g_ref[...], b_ref[...])


def _mixout_ln(ret, gla, x2d, w_b, g, b):
    s = x2d.shape[0]
    tm = min(s, 512)
    nr, ng = ret.shape[1], gla.shape[1]
    return pl.pallas_call(
        _mixout_kernel,
        out_shape=jax.ShapeDtypeStruct((s, D_MODEL), F32),
        grid=(s // tm,),
        in_specs=[pl.BlockSpec((tm, nr), lambda i: (i, 0)),
                  pl.BlockSpec((tm, ng), lambda i: (i, 0)),
                  pl.BlockSpec((tm, D_MODEL), lambda i: (i, 0)),
                  pl.BlockSpec((nr + ng, D_MODEL), lambda i: (0, 0)),
                  pl.BlockSpec((1, D_MODEL), lambda i: (0, 0)),
                  pl.BlockSpec((1, D_MODEL), lambda i: (0, 0))],
        out_specs=pl.BlockSpec((tm, D_MODEL), lambda i: (i, 0)),
        compiler_params=_cparams(("arbitrary",)),
        name="mixout_ln1",
    )(ret, gla, x2d, w_b, g, b)


def _kv_kernel(mem_ref, wk_ref, wv_ref, k_ref, v_ref):
    m = mem_ref[...]
    k_ref[...] = _dot(m, wk_ref[...]).astype(k_ref.dtype)
    v_ref[...] = _dot(m, wv_ref[...]).astype(v_ref.dtype)


def _mem_kv(mem_b, wk_b, wv_b):
    tn = 512
    return pl.pallas_call(
        _kv_kernel,
        out_shape=(jax.ShapeDtypeStruct((MEM_LEN, D_MODEL), BF16),
                   jax.ShapeDtypeStruct((MEM_LEN, D_MODEL), BF16)),
        grid=(D_MODEL // tn,),
        in_specs=[pl.BlockSpec((MEM_LEN, D_MODEL), lambda j: (0, 0)),
                  pl.BlockSpec((D_MODEL, tn), lambda j: (0, j)),
                  pl.BlockSpec((D_MODEL, tn), lambda j: (0, j))],
        out_specs=(pl.BlockSpec((MEM_LEN, tn), lambda j: (0, j)),
                   pl.BlockSpec((MEM_LEN, tn), lambda j: (0, j))),
        compiler_params=_cparams(("arbitrary",)),
        name="mem_kv",
    )(mem_b, wk_b, wv_b)


def _store_row_chunks(ref, val):
    rows = val.shape[0]
    for j in range(ROW_CHUNKS):
        ref[pl.ds(j, rows, stride=ROW_CHUNKS), :] = val[:, j * LANES:(j + 1) * LANES]


def _load_row_chunks(ref, rows):
    return jnp.concatenate(
        [ref[pl.ds(j, rows, stride=ROW_CHUNKS), :] for j in range(ROW_CHUNKS)], axis=-1)


def _cross_kernel(h_ref, wq_ref, k_ref, v_ref, wo_ref, g_ref, b_ref, wr_ref, br_ref,
                  h2_ref, h2c_ref, lg_ref):
    h1 = h_ref[...]
    q = _dot(h1.astype(BF16), wq_ref[...]).astype(BF16)
    outs = []
    for hd in range(MEM_HEADS):
        sl = slice(hd * MEM_HEAD_DIM, (hd + 1) * MEM_HEAD_DIM)
        s = _dot_nt(q[:, sl], k_ref[:, sl]) * (MEM_HEAD_DIM ** -0.5)
        s = s - jnp.max(s, axis=-1, keepdims=True)
        p = jnp.exp(s)
        p = p / jnp.sum(p, axis=-1, keepdims=True)
        outs.append(_dot(p.astype(BF16), v_ref[:, sl]))
    o = jnp.concatenate(outs, axis=-1).astype(BF16)
    cross = _dot(o, wo_ref[...])
    h2 = _layer_norm(DEEPNORM_ALPHA * h1 + cross, g_ref[...], b_ref[...])
    h2_ref[...] = h2
    _store_row_chunks(h2c_ref, h2)
    lg_ref[...] = _dot(h2.astype(BF16), wr_ref[...]) + br_ref[...]


def _cross_attention(h1, wq_b, k, v, wo_b, g, b, w_route, b_route):
    s = h1.shape[0]
    tm = min(s, 256)
    const = lambda shape: pl.BlockSpec(shape, lambda i: (0, 0), pipeline_mode=pl.Buffered(1))
    return pl.pallas_call(
        _cross_kernel,
        out_shape=(jax.ShapeDtypeStruct((s, D_MODEL), F32),
                   jax.ShapeDtypeStruct((s * ROW_CHUNKS, LANES), F32),
                   jax.ShapeDtypeStruct((s, LANES), F32)),
        grid=(s // tm,),
        in_specs=[pl.BlockSpec((tm, D_MODEL), lambda i: (i, 0)),
                  const((D_MODEL, D_MODEL)),
                  const((MEM_LEN, D_MODEL)),
                  const((MEM_LEN, D_MODEL)),
                  const((D_MODEL, D_MODEL)),
                  const((1, D_MODEL)),
                  const((1, D_MODEL)),
                  const((D_MODEL, LANES)),
                  const((1, LANES))],
        out_specs=(pl.BlockSpec((tm, D_MODEL), lambda i: (i, 0)),
                   pl.BlockSpec((tm * ROW_CHUNKS, LANES), lambda i: (i, 0)),
                   pl.BlockSpec((tm, LANES), lambda i: (i, 0))),
        compiler_params=_cparams(("arbitrary",)),
        name="cross_attn_ln2",
    )(h1, wq_b, k, v, wo_b, g, b, w_route, b_route)


def _route_kernel(lg_ref, dest_ref, gate_ref, be_ref, cnt_row, cnt_col, pstart, run):
    phase = pl.program_id(0)
    i = pl.program_id(1)
    tm = lg_ref.shape[0]
    neg = -jnp.inf

    logits = lg_ref[...]
    lane = lax.broadcasted_iota(jnp.int32, (tm, LANES), 1)
    gmask = lane < N_GROUPS
    gl = jnp.where(gmask, logits, neg)
    ge = jnp.exp(gl - jnp.max(gl, axis=-1, keepdims=True))
    pg = ge / jnp.sum(ge, axis=-1, keepdims=True)
    pg_sel = jnp.max(pg, axis=-1, keepdims=True)
    grp = jnp.min(jnp.where((pg == pg_sel) & gmask, lane, LANES), axis=-1, keepdims=True)

    fl_lane = lane - N_GROUPS
    fmask = (fl_lane >= 0) & (fl_lane < N_EXPERTS) & ((fl_lane >> 3) == grp)
    fl = jnp.where(fmask, logits, neg)
    fe = jnp.exp(fl - jnp.max(fl, axis=-1, keepdims=True))
    fp = fe / jnp.sum(fe, axis=-1, keepdims=True)
    p1 = jnp.max(fp, axis=-1, keepdims=True)
    i1 = jnp.min(jnp.where((fp == p1) & fmask, lane, LANES), axis=-1, keepdims=True)
    rest = fmask & (lane != i1)
    fp2 = jnp.where(rest, fp, -1.0)
    p2 = jnp.max(fp2, axis=-1, keepdims=True)
    i2 = jnp.min(jnp.where((fp2 == p2) & rest, lane, LANES), axis=-1, keepdims=True)
    psum = p1 + p2
    gate1 = pg_sel * p1 / psum
    gate2 = pg_sel * p2 / psum

    oh1 = lane == (i1 - N_GROUPS)
    oh2 = lane == (i2 - N_GROUPS)
    oh = (jnp.where(oh1, 1.0, 0.0) + jnp.where(oh2, 1.0, 0.0)).astype(BF16)
    ones = jnp.ones((tm, LANES), BF16)

    @pl.when((phase == 0) & (i == 0))
    def _():
        cnt_row[...] = jnp.zeros_like(cnt_row)
        cnt_col[...] = jnp.zeros_like(cnt_col)

    @pl.when(phase == 0)
    def _():
        cnt_row[...] += _dot_tn(ones, oh)
        cnt_col[...] += _dot_tn(oh, ones)

    @pl.when((phase == 1) & (i == 0))
    def _():
        nblk_row = jnp.floor((cnt_row[...] + (MOE_BLK - 1.0)) * (1.0 / MOE_BLK))
        nblk_col = jnp.floor((cnt_col[...] + (MOE_BLK - 1.0)) * (1.0 / MOE_BLK))
        r = lax.broadcasted_iota(jnp.int32, (LANES, LANES), 0)
        c = lax.broadcasted_iota(jnp.int32, (LANES, LANES), 1)
        strict_upper = jnp.where(r < c, 1.0, 0.0).astype(BF16)
        lower_incl = jnp.where(c <= r, 1.0, 0.0).astype(BF16)
        pstart[...] = _dot(nblk_row[0:8, :].astype(BF16), strict_upper) * float(MOE_BLK)
        ends = _dot(lower_incl, nblk_col.astype(BF16))
        expert_rows = r < N_EXPERTS
        be = jnp.sum(jnp.where(expert_rows & (ends <= c.astype(F32)), 1.0, 0.0), axis=0, keepdims=True)
        be = jnp.minimum(be, N_EXPERTS - 1.0)
        total = jnp.sum(jnp.where(r < N_EXPERTS, nblk_col, 0.0), axis=0, keepdims=True)
        sub = lax.broadcasted_iota(jnp.int32, be_ref.shape, 0)
        be_ref[...] = jnp.where(sub == 0, jnp.broadcast_to(be, be_ref.shape),
                                jnp.broadcast_to(total, be_ref.shape)).astype(jnp.int32)
        run[...] = jnp.zeros_like(run)

    @pl.when(phase == 1)
    def _():
        tr = lax.broadcasted_iota(jnp.int32, (tm, tm), 0)
        tc = lax.broadcasted_iota(jnp.int32, (tm, tm), 1)
        strict_lower = jnp.where(tc < tr, 1.0, 0.0).astype(BF16)
        before = _dot(strict_lower, oh) + run[0:1, :] + pstart[0:1, :]
        d1 = jnp.sum(jnp.where(oh1, before, 0.0), axis=-1, keepdims=True)
        d2 = jnp.sum(jnp.where(oh2, before, 0.0), axis=-1, keepdims=True)
        dest_ref[...] = jnp.where(lane == 0, d1, jnp.where(lane == 1, d2, 0.0)).astype(jnp.int32)
        gate_ref[...] = jnp.where(lane == 0, gate1, jnp.where(lane == 1, gate2, 0.0))
        run[...] += _dot_tn(ones, oh)[0:8, :]


def _route(logits):
    t = logits.shape[0]
    tm = min(t, 512)
    nt = t // tm
    return pl.pallas_call(
        _route_kernel,
        out_shape=(jax.ShapeDtypeStruct((t, LANES), jnp.int32),
                   jax.ShapeDtypeStruct((t, LANES), F32),
                   jax.ShapeDtypeStruct((8, LANES), jnp.int32)),
        grid=(2, nt),
        in_specs=[pl.BlockSpec((tm, LANES), lambda p, i: (i, 0))],
        out_specs=(pl.BlockSpec((tm, LANES), lambda p, i: (i * p, 0)),
                   pl.BlockSpec((tm, LANES), lambda p, i: (i * p, 0)),
                   pl.BlockSpec((8, LANES), lambda p, i: (0, 0))),
        scratch_shapes=[pltpu.VMEM((LANES, LANES), F32),
                        pltpu.VMEM((LANES, LANES), F32),
                        pltpu.VMEM((8, LANES), F32),
                        pltpu.VMEM((8, LANES), F32)],
        compiler_params=_cparams(("arbitrary", "arbitrary")),
        name="moe_route",
    )(logits)


def _dispatch_kernel(dest_ref, x_ref, xb_in_ref, xb_ref, sem):
    del xb_in_ref
    i = pl.program_id(0)
    tm = x_ref.shape[0] // ROW_CHUNKS

    def row_copy(src_row, dst_row):
        return pltpu.make_async_copy(
            x_ref.at[pl.ds(pl.multiple_of(src_row * ROW_CHUNKS, ROW_CHUNKS), ROW_CHUNKS), :],
            xb_ref.at[pl.ds(pl.multiple_of(dst_row * ROW_CHUNKS, ROW_CHUNKS), ROW_CHUNKS), :],
            sem)

    def issue(r, carry):
        a = (i * tm + r) * 2
        row_copy(r, dest_ref[a]).start()
        row_copy(r, dest_ref[a + 1]).start()
        return carry

    lax.fori_loop(0, tm, issue, 0)

    def drain(r, carry):
        row_copy(0, 0).wait()
        row_copy(0, 0).wait()
        return carry

    lax.fori_loop(0, tm, drain, 0)


def _dispatch(dest_flat, h2c, xb_init):
    rows = h2c.shape[0]
    t = rows // ROW_CHUNKS
    tm = min(t, 256)
    grid_spec = pltpu.PrefetchScalarGridSpec(
        num_scalar_prefetch=1,
        grid=(t // tm,),
        in_specs=[pl.BlockSpec((tm * ROW_CHUNKS, LANES), lambda i, d: (i, 0)),
                  pl.BlockSpec(memory_space=pl.ANY)],
        out_specs=pl.BlockSpec(memory_space=pl.ANY),
        scratch_shapes=[pltpu.SemaphoreType.DMA(())],
    )
    return pl.pallas_call(
        _dispatch_kernel,
        out_shape=jax.ShapeDtypeStruct(xb_init.shape, xb_init.dtype),
        grid_spec=grid_spec,
        input_output_aliases={2: 0},
        compiler_params=pltpu.CompilerParams(dimension_semantics=("arbitrary",),
                                             vmem_limit_bytes=VMEM_LIMIT,
                                             has_side_effects=True),
        name="moe_dispatch",
    )(dest_flat, h2c, xb_init)


def _expert_kernel(be_ref, nu_ref, x_ref, wg_ref, wu_ref, wd_ref, y_ref, wg_b, wu_b, wd_b):
    b = pl.program_id(0)
    used = b < nu_ref[0]
    prev = be_ref[jnp.maximum(b - 1, 0)]
    fresh = used & ((b == 0) | (be_ref[b] != prev))

    @pl.when(fresh)
    def _():
        wg_b[...] = wg_ref[0].astype(BF16)
        wu_b[...] = wu_ref[0].astype(BF16)
        wd_b[...] = wd_ref[0].astype(BF16)

    @pl.when(used)
    def _():
        x = _load_row_chunks(x_ref, MOE_BLK).astype(BF16)
        hid = _silu(_dot(x, wg_b[...])) * _dot(x, wu_b[...])
        _store_row_chunks(y_ref, _dot(hid.astype(BF16), wd_b[...]))

    @pl.when(jnp.logical_not(used))
    def _():
        y_ref[...] = jnp.zeros_like(y_ref)


def _experts(block_e, n_used, xb, w_gate, w_up, w_down):
    nb = xb.shape[0] // (MOE_BLK * ROW_CHUNKS)
    blk = lambda b, be, nu: jnp.minimum(b, nu[0] - 1)
    wmap = lambda b, be, nu: (be[blk(b, be, nu)], 0, 0)
    grid_spec = pltpu.PrefetchScalarGridSpec(
        num_scalar_prefetch=2,
        grid=(nb,),
        in_specs=[pl.BlockSpec((MOE_BLK * ROW_CHUNKS, LANES), lambda b, be, nu: (blk(b, be, nu), 0)),
                  pl.BlockSpec((1, D_MODEL, EXPERT_FF), wmap),
                  pl.BlockSpec((1, D_MODEL, EXPERT_FF), wmap),
                  pl.BlockSpec((1, EXPERT_FF, D_MODEL), wmap)],
        out_specs=pl.BlockSpec((MOE_BLK * ROW_CHUNKS, LANES), lambda b, be, nu: (b, 0)),
        scratch_shapes=[pltpu.VMEM((D_MODEL, EXPERT_FF), BF16),
                        pltpu.VMEM((D_MODEL, EXPERT_FF), BF16),
                        pltpu.VMEM((EXPERT_FF, D_MODEL), BF16)],
    )
    return pl.pallas_call(
        _expert_kernel,
        out_shape=jax.ShapeDtypeStruct(xb.shape, F32),
        grid_spec=grid_spec,
        compiler_params=_cparams(("arbitrary",)),
        name="moe_experts",
    )(block_e, n_used, xb, w_gate, w_up, w_down)


def _combine_kernel(dest_ref, yb_ref, h2_ref, gate_ref, g_ref, b_ref, o_ref, buf0, buf1, sems):
    i = pl.program_id(0)
    n = pl.num_programs(0)
    tm = h2_ref.shape[0]
    bufs = (buf0, buf1)

    def row_copy(k, slot, dst_row, src_row):
        return pltpu.make_async_copy(
            yb_ref.at[pl.ds(pl.multiple_of(src_row * ROW_CHUNKS, ROW_CHUNKS), ROW_CHUNKS), :],
            bufs[k].at[slot, pl.ds(pl.multiple_of(dst_row * ROW_CHUNKS, ROW_CHUNKS), ROW_CHUNKS), :],
            sems.at[slot])

    def issue_tile(tile, slot):
        def body(r, carry):
            a = (tile * tm + r) * 2
            row_copy(0, slot, r, dest_ref[a]).start()
            row_copy(1, slot, r, dest_ref[a + 1]).start()
            return carry
        lax.fori_loop(0, tm, body, 0)

    def wait_tile(slot):
        def body(r, carry):
            row_copy(0, slot, 0, 0).wait()
            row_copy(1, slot, 0, 0).wait()
            return carry
        lax.fori_loop(0, tm, body, 0)

    slot = i % 2

    @pl.when(i == 0)
    def _():
        issue_tile(0, 0)

    @pl.when(i + 1 < n)
    def _():
        issue_tile(i + 1, 1 - slot)

    wait_tile(slot)
    y0 = _load_row_chunks(buf0.at[slot], tm)
    y1 = _load_row_chunks(buf1.at[slot], tm)
    gate = gate_ref[...]
    ffn = gate[:, 0:1] * y0 + gate[:, 1:2] * y1
    o_ref[...] = _layer_norm(DEEPNORM_ALPHA * h2_ref[...] + ffn, g_ref[...], b_ref[...])


def _combine_ln(dest_flat, yb, h2, gate, g, b):
    t = h2.shape[0]
    tm = min(t, 256)
    grid_spec = pltpu.PrefetchScalarGridSpec(
        num_scalar_prefetch=1,
        grid=(t // tm,),
        in_specs=[pl.BlockSpec(memory_space=pl.ANY),
                  pl.BlockSpec((tm, D_MODEL), lambda i, d: (i, 0)),
                  pl.BlockSpec((tm, LANES), lambda i, d: (i, 0)),
                  pl.BlockSpec((1, D_MODEL), lambda i, d: (0, 0)),
                  pl.BlockSpec((1, D_MODEL), lambda i, d: (0, 0))],
        out_specs=pl.BlockSpec((tm, D_MODEL), lambda i, d: (i, 0)),
        scratch_shapes=[pltpu.VMEM((2, tm * ROW_CHUNKS, LANES), F32),
                        pltpu.VMEM((2, tm * ROW_CHUNKS, LANES), F32),
                        pltpu.SemaphoreType.DMA((2,))],
    )
    return pl.pallas_call(
        _combine_kernel,
        out_shape=jax.ShapeDtypeStruct((t, D_MODEL), F32),
        grid_spec=grid_spec,
        compiler_params=_cparams(("arbitrary",)),
        name="moe_combine_ln3",
    )(dest_flat, yb, h2, gate, g, b)


def _mixer(x2d, positions, w_in, w_gla_a2, b_gla_a, g_gla_norm):
    s = x2d.shape[0]
    xb = x2d.astype(BF16)
    w_in_b = w_in.astype(BF16)
    half = RET_DK // 2
    inv_freq = (ROPE_BASE ** (-jnp.arange(half, dtype=F32) / half)).reshape(1, half)
    cos, sin = _rope_table(positions.reshape(s, 1), inv_freq)
    h_main = _proj_in(xb, w_in_b)
    w_lr = jnp.pad(w_in_b[:, GLR_OFF:], ((0, 0), (0, LANES - GLA_LOWRANK)))
    w_a2 = jnp.pad(w_gla_a2.astype(BF16), ((0, LANES - GLA_LOWRANK), (0, 0)))
    log_a = _gla_log_decay(xb, w_lr, w_a2, b_gla_a.reshape(1, -1))
    log_gamma = jnp.log1p(-jnp.exp2(-5.0 - jnp.arange(RET_HEADS, dtype=F32)))
    ret = _retention(h_main, cos, sin, log_gamma)
    gla = _gla(h_main, log_a, g_gla_norm.reshape(1, -1))
    return ret, gla


def _moe(h2, h2c, logits, w_gate, w_up, w_down, g, b):
    t = h2.shape[0]
    dest, gate, plan = _route(logits)
    dest_flat = dest[:, :2].reshape(-1)
    nb = (t * 2 + N_EXPERTS * MOE_BLK) // MOE_BLK
    block_e = plan[0, :nb]
    n_used = plan[1, :1]
    xb_init = jnp.zeros((nb * MOE_BLK * ROW_CHUNKS, LANES), F32)
    xb = _dispatch(dest_flat, h2c, xb_init)
    yb = _experts(block_e, n_used, xb, w_gate, w_up, w_down)
    return _combine_ln(dest_flat, yb, h2, gate, g, b)


def kernel(x, mem, positions, w_in, w_gla_a2, b_gla_a, g_gla_norm, w_mix_out, ln1_g, ln1_b, w_mq, w_mk, w_mv, w_mo, ln2_g, ln2_b, w_route_group, b_route_group, w_route_expert, b_route_expert, w_exp_gate, w_exp_up, w_exp_down, ln3_g, ln3_b):
    bsz, s, d = x.shape
    assert bsz == 1 and d == D_MODEL
    x2d = x.reshape(s, d)
    row = lambda v: v.reshape(1, -1)

    ret, gla = _mixer(x2d, positions, w_in[0], w_gla_a2[0], b_gla_a[0], g_gla_norm[0])
    h1 = _mixout_ln(ret, gla, x2d, w_mix_out[0].astype(BF16), row(ln1_g[0]), row(ln1_b[0]))

    k, v = _mem_kv(mem[0].astype(BF16), w_mk[0].astype(BF16), w_mv[0].astype(BF16))
    n_route = N_GROUPS + N_EXPERTS
    w_route = jnp.pad(jnp.concatenate([w_route_group[0], w_route_expert[0]], axis=1).astype(BF16),
                      ((0, 0), (0, LANES - n_route)))
    b_route = jnp.pad(jnp.concatenate([b_route_group[0], b_route_expert[0].reshape(-1)]),
                      (0, LANES - n_route)).reshape(1, LANES)
    h2, h2c, logits = _cross_attention(h1, w_mq[0].astype(BF16), k, v, w_mo[0].astype(BF16),
                                       row(ln2_g[0]), row(ln2_b[0]), w_route, b_route)

    out = _moe(h2, h2c, logits, w_exp_gate[0], w_exp_up[0], w_exp_down[0],
               row(ln3_g[0]), row(ln3_b[0]))
    return out.reshape(bsz, s, d)
```

```python
import functools
import math

import jax
import jax.numpy as jnp
from jax import lax
from jax.experimental import pallas as pl
from jax.experimental.pallas import tpu as pltpu

F32 = jnp.float32
BF16 = jnp.bfloat16

D_MODEL = 2048
MEM_LEN = 256
RET_HEADS = 4
RET_DK = 256
RET_DV = 256
GLA_HEADS = 4
GLA_DK = 128
GLA_DV = 256
GLA_LOWRANK = 16
GLA_TAU = 16.0
ROPE_BASE = 10000.0
MEM_HEADS = 4
MEM_HEAD_DIM = D_MODEL // MEM_HEADS
N_GROUPS = 4
EXPERTS_PER_GROUP = 8
N_EXPERTS = N_GROUPS * EXPERTS_PER_GROUP
EXPERT_FF = 512
LN_EPS = 1e-5
DEPTH = 1
DEEPNORM_ALPHA = (2 * DEPTH) ** 0.25

RQ_OFF, RK_OFF, RV_OFF, RG_OFF = 0, 1024, 2048, 3072
GQ_OFF, GK_OFF, GV_OFF, GG_OFF, GLR_OFF = 4096, 4608, 5120, 6144, 7168
IN_MAIN = 7168

LANES = 128
ROW_CHUNKS = D_MODEL // LANES
RET_CHUNK = 256
GLA_CHUNK = 128
GLA_LEVELS = 7
MOE_BLK = 256
VMEM_LIMIT = 56 * 1024 * 1024


def _cparams(sem):
    return pltpu.CompilerParams(dimension_semantics=sem, vmem_limit_bytes=VMEM_LIMIT)


def _layer_norm(y, g, b):
    mu = jnp.mean(y, axis=-1, keepdims=True)
    d = y - mu
    var = jnp.mean(d * d, axis=-1, keepdims=True)
    return d * lax.rsqrt(var + LN_EPS) * g + b


def _silu(x):
    return x / (1.0 + jnp.exp(-x))


def _dot(a, b):
    return jnp.dot(a, b, preferred_element_type=F32)


def _dot_nt(a, b):
    return lax.dot_general(a, b, (((1,), (1,)), ((), ())), preferred_element_type=F32)


def _dot_tn(a, b):
    return lax.dot_general(a, b, (((0,), (0,)), ((), ())), preferred_element_type=F32)


def _rope_kernel(pos_ref, invf_ref, cos_ref, sin_ref):
    ang = pos_ref[...].astype(F32) * invf_ref[...]
    cos_ref[...] = jnp.cos(ang)
    sin_ref[...] = jnp.sin(ang)


def _rope_table(pos_col, inv_freq):
    s = pos_col.shape[0]
    tm = min(s, 1024)
    half = inv_freq.shape[1]
    return pl.pallas_call(
        _rope_kernel,
        out_shape=(jax.ShapeDtypeStruct((s, half), F32), jax.ShapeDtypeStruct((s, half), F32)),
        grid=(s // tm,),
        in_specs=[pl.BlockSpec((tm, 1), lambda i: (i, 0)),
                  pl.BlockSpec((1, half), lambda i: (0, 0))],
        out_specs=(pl.BlockSpec((tm, half), lambda i: (i, 0)),
                   pl.BlockSpec((tm, half), lambda i: (i, 0))),
        compiler_params=_cparams(("arbitrary",)),
        name="rope_table",
    )(pos_col, inv_freq)


def _mm_kernel(x_ref, w_ref, o_ref):
    o_ref[...] = _dot(x_ref[...], w_ref[...]).astype(o_ref.dtype)


def _proj_in(xb, w_in_b):
    s = xb.shape[0]
    tm = min(s, 1024)
    tn = 512
    return pl.pallas_call(
        _mm_kernel,
        out_shape=jax.ShapeDtypeStruct((s, IN_MAIN), F32),
        grid=(s // tm, IN_MAIN // tn),
        in_specs=[pl.BlockSpec((tm, D_MODEL), lambda i, j: (i, 0)),
                  pl.BlockSpec((D_MODEL, tn), lambda i, j: (0, j))],
        out_specs=pl.BlockSpec((tm, tn), lambda i, j: (i, j)),
        compiler_params=_cparams(("arbitrary", "arbitrary")),
        name="proj_in",
    )(xb, w_in_b)


def _loga_kernel(x_ref, wlr_ref, wa2_ref, ba_ref, o_ref):
    glr = _dot(x_ref[...], wlr_ref[...])
    z = _dot(glr.astype(BF16), wa2_ref[...]) + ba_ref[...]
    log_sig = jnp.minimum(z, 0.0) - jnp.log1p(jnp.exp(-jnp.abs(z)))
    o_ref[...] = log_sig / GLA_TAU


def _gla_log_decay(xb, w_lr, w_a2, b_a):
    s = xb.shape[0]
    tm = min(s, 1024)
    n = GLA_HEADS * GLA_DK
    return pl.pallas_call(
        _loga_kernel,
        out_shape=jax.ShapeDtypeStruct((s, n), F32),
        grid=(s // tm,),
        in_specs=[pl.BlockSpec((tm, D_MODEL), lambda i: (i, 0)),
                  pl.BlockSpec((D_MODEL, LANES), lambda i: (0, 0)),
                  pl.BlockSpec((LANES, n), lambda i: (0, 0)),
                  pl.BlockSpec((1, n), lambda i: (0, 0))],
        out_specs=pl.BlockSpec((tm, n), lambda i: (i, 0)),
        compiler_params=_cparams(("arbitrary",)),
        name="gla_log_decay",
    )(xb, w_lr, w_a2, b_a)


def _rotary(t, cos, sin):
    half = t.shape[-1] // 2
    t1, t2 = t[:, :half], t[:, half:]
    return jnp.concatenate([t1 * cos - t2 * sin, t1 * sin + t2 * cos], axis=-1)


def _retention_kernel(lg_ref, q_ref, k_ref, v_ref, g_ref, cos_ref, sin_ref, o_ref, state_ref):
    c = pl.program_id(0)
    C = q_ref.shape[0]

    @pl.when(c == 0)
    def _():
        state_ref[...] = jnp.zeros_like(state_ref)

    cos = cos_ref[...]
    sin = sin_ref[...]
    ri = lax.broadcasted_iota(jnp.int32, (C, C), 0)
    ci = lax.broadcasted_iota(jnp.int32, (C, C), 1)
    rel = jnp.maximum(ri - ci, 0).astype(F32)
    n = lax.broadcasted_iota(jnp.int32, (C, 1), 0).astype(F32)

    for h in range(RET_HEADS):
        lg = lg_ref[h]
        sl = slice(h * RET_DK, (h + 1) * RET_DK)
        q = _rotary(q_ref[:, sl], cos, sin)
        k = _rotary(k_ref[:, sl], cos, sin) * (RET_DK ** -0.5)
        v = v_ref[:, sl].astype(BF16)
        decay_intra = jnp.where(ri >= ci, jnp.exp(lg * rel), 0.0)
        decay_q = jnp.exp(lg * (n + 1.0))
        decay_k = jnp.exp(lg * (C - 1.0 - n))
        decay_chunk = jnp.exp(lg * C)

        state = state_ref[h]
        scores = _dot_nt(q.astype(BF16), k.astype(BF16)) * decay_intra
        o = _dot(scores.astype(BF16), v) + _dot((q * decay_q).astype(BF16), state.astype(BF16))
        state_ref[h] = decay_chunk * state + _dot_tn((k * decay_k).astype(BF16), v)

        mu = jnp.mean(o, axis=-1, keepdims=True)
        d = o - mu
        var = jnp.mean(d * d, axis=-1, keepdims=True)
        o = d * lax.rsqrt(var + LN_EPS)
        o_ref[:, sl] = (_silu(g_ref[:, sl]) * o).astype(o_ref.dtype)


def _retention(h_main, cos, sin, log_gamma):
    s = h_main.shape[0]
    C = min(RET_CHUNK, s)
    w = RET_HEADS * RET_DK
    col = lambda off: (lambda c, lg: (c, off // w))
    grid_spec = pltpu.PrefetchScalarGridSpec(
        num_scalar_prefetch=1,
        grid=(s // C,),
        in_specs=[pl.BlockSpec((C, w), col(RQ_OFF)),
                  pl.BlockSpec((C, w), col(RK_OFF)),
                  pl.BlockSpec((C, w), col(RV_OFF)),
                  pl.BlockSpec((C, w), col(RG_OFF)),
                  pl.BlockSpec((C, RET_DK // 2), lambda c, lg: (c, 0)),
                  pl.BlockSpec((C, RET_DK // 2), lambda c, lg: (c, 0))],
        out_specs=pl.BlockSpec((C, RET_HEADS * RET_DV), lambda c, lg: (c, 0)),
        scratch_shapes=[pltpu.VMEM((RET_HEADS, RET_DK, RET_DV), F32)],
    )
    return pl.pallas_call(
        _retention_kernel,
        out_shape=jax.ShapeDtypeStruct((s, RET_HEADS * RET_DV), BF16),
        grid_spec=grid_spec,
        compiler_params=_cparams(("arbitrary",)),
        name="retention",
    )(log_gamma, h_main, h_main, h_main, h_main, cos, sin)


def _gla_decay_matrix(C):
    import numpy as np
    levels = int(math.log2(C))
    r = np.arange(C)[:, None]
    t = np.arange(C)[None, :]
    mats = []
    for l in range(levels):
        blk = C >> l
        half = blk // 2
        m = (r // blk) * blk + half - 1
        qside = (r % blk) >= half
        mats.append(np.where(qside, (t > m) & (t <= r), (t > r) & (t <= m)))
    mats.append(t <= r)
    mats.append(t > r)
    return np.concatenate(mats, axis=0).astype(np.float32)


def _gla_kernel(m_ref, q_ref, k_ref, v_ref, g_ref, la_ref, gn_ref, o_ref, state_ref):
    c = pl.program_id(0)
    C = q_ref.shape[0]
    levels = GLA_LEVELS

    @pl.when(c == 0)
    def _():
        state_ref[...] = jnp.zeros_like(state_ref)

    m = m_ref[...]
    ri = lax.broadcasted_iota(jnp.int32, (C, C), 0)
    ci = lax.broadcasted_iota(jnp.int32, (C, C), 1)
    xor = jnp.where(ri > ci, ri ^ ci, 0)
    row = lax.broadcasted_iota(jnp.int32, (C, 1), 0)

    for h in range(GLA_HEADS):
        ks = slice(h * GLA_DK, (h + 1) * GLA_DK)
        vs = slice(h * GLA_DV, (h + 1) * GLA_DV)
        q = q_ref[:, ks] * (GLA_DK ** -0.5)
        k = k_ref[:, ks]
        v = v_ref[:, vs].astype(BF16)
        la = la_ref[:, ks]
        la_hi = la.astype(BF16)
        la_lo = (la - la_hi.astype(F32)).astype(BF16)
        expo = jnp.exp(_dot(m, la_hi) + _dot(m, la_lo))

        scores = jnp.where(ri == ci, _dot_nt(q.astype(BF16), k.astype(BF16)), 0.0)
        for l in range(levels):
            half = C >> (l + 1)
            shift = int(math.log2(half))
            qside = (row & half) != 0
            x = (jnp.where(qside, q, k) * expo[l * C:(l + 1) * C]).astype(BF16)
            scores = scores + jnp.where((xor >> shift) == 1, _dot_nt(x, x), 0.0)

        e_b = expo[levels * C:(levels + 1) * C]
        e_rev = expo[(levels + 1) * C:(levels + 2) * C]
        e_last = e_b[C - 1:C, :]

        state = state_ref[h]
        o = _dot(scores.astype(BF16), v) + _dot_nt((q * e_b).astype(BF16), state.astype(BF16))
        state_ref[h] = state * e_last + _dot_tn(v, (k * e_rev).astype(BF16))

        o = o * lax.rsqrt(jnp.mean(o * o, axis=-1, keepdims=True) + LN_EPS) * gn_ref[...]
        o_ref[:, vs] = (_silu(g_ref[:, vs]) * o).astype(o_ref.dtype)


def _gla(h_main, log_a, g_norm):
    s = h_main.shape[0]
    C = GLA_CHUNK
    m = jnp.asarray(_gla_decay_matrix(C), dtype=BF16)
    nrow = m.shape[0]
    wk = GLA_HEADS * GLA_DK
    wv = GLA_HEADS * GLA_DV
    return pl.pallas_call(
        _gla_kernel,
        out_shape=jax.ShapeDtypeStruct((s, wv), BF16),
        grid=(s // C,),
        in_specs=[pl.BlockSpec((nrow, C), lambda c: (0, 0)),
                  pl.BlockSpec((C, wk), lambda c: (c, GQ_OFF // wk)),
                  pl.BlockSpec((C, wk), lambda c: (c, GK_OFF // wk)),
                  pl.BlockSpec((C, wv), lambda c: (c, GV_OFF // wv)),
                  pl.BlockSpec((C, wv), lambda c: (c, GG_OFF // wv)),
                  pl.BlockSpec((C, wk), lambda c: (c, 0)),
                  pl.BlockSpec((1, GLA_DV), lambda c: (0, 0))],
        out_specs=pl.BlockSpec((C, wv), lambda c: (c, 0)),
        scratch_shapes=[pltpu.VMEM((GLA_HEADS, GLA_DV, GLA_DK), F32)],
        compiler_params=_cparams(("arbitrary",)),
        name="gla",
    )(m, h_main, h_main, h_main, h_main, log_a, g_norm)


def _mixout_kernel(ret_ref, gla_ref, x_ref, w_ref, g_ref, b_ref, o_ref):
    nr = ret_ref.shape[1]
    mix = _dot(ret_ref[...], w_ref[:nr, :]) + _dot(gla_ref[...], w_ref[nr:, :])
    o_ref[...] = _layer_norm(DEEPNORM_ALPHA * x_ref[...] + mix, g_ref[...], b_ref[...])


def _mixout_ln(ret, gla, x2d, w_b, g, b):
    s = x2d.shape[0]
    tm = min(s, 512)
    nr, ng = ret.shape[1], gla.shape[1]
    return pl.pallas_call(
        _mixout_kernel,
        out_shape=jax.ShapeDtypeStruct((s, D_MODEL), F32),
        grid=(s // tm,),
        in_specs=[pl.BlockSpec((tm, nr), lambda i: (i, 0)),
                  pl.BlockSpec((tm, ng), lambda i: (i, 0)),
                  pl.BlockSpec((tm, D_MODEL), lambda i: (i, 0)),
                  pl.BlockSpec((nr + ng, D_MODEL), lambda i: (0, 0)),
                  pl.BlockSpec((1, D_MODEL), lambda i: (0, 0)),
                  pl.BlockSpec((1, D_MODEL), lambda i: (0, 0))],
        out_specs=pl.BlockSpec((tm, D_MODEL), lambda i: (i, 0)),
        compiler_params=_cparams(("arbitrary",)),
        name="mixout_ln1",
    )(ret, gla, x2d, w_b, g, b)


def _kv_kernel(mem_ref, wk_ref, wv_ref, k_ref, v_ref):
    m = mem_ref[...]
    k_ref[...] = _dot(m, wk_ref[...]).astype(k_ref.dtype)
    v_ref[...] = _dot(m, wv_ref[...]).astype(v_ref.dtype)


def _mem_kv(mem_b, wk_b, wv_b):
    tn = 512
    return pl.pallas_call(
        _kv_kernel,
        out_shape=(jax.ShapeDtypeStruct((MEM_LEN, D_MODEL), BF16),
                   jax.ShapeDtypeStruct((MEM_LEN, D_MODEL), BF16)),
        grid=(D_MODEL // tn,),
        in_specs=[pl.BlockSpec((MEM_LEN, D_MODEL), lambda j: (0, 0)),
                  pl.BlockSpec((D_MODEL, tn), lambda j: (0, j)),
                  pl.BlockSpec((D_MODEL, tn), lambda j: (0, j))],
        out_specs=(pl.BlockSpec((MEM_LEN, tn), lambda j: (0, j)),
                   pl.BlockSpec((MEM_LEN, tn), lambda j: (0, j))),
        compiler_params=_cparams(("arbitrary",)),
        name="mem_kv",
    )(mem_b, wk_b, wv_b)


def _store_row_chunks(ref, val):
    rows = val.shape[0]
    for j in range(ROW_CHUNKS):
        ref[pl.ds(j, rows, stride=ROW_CHUNKS), :] = val[:, j * LANES:(j + 1) * LANES]


def _load_row_chunks(ref, rows):
    return jnp.concatenate(
        [ref[pl.ds(j, rows, stride=ROW_CHUNKS), :] for j in range(ROW_CHUNKS)], axis=-1)


def _cross_kernel(h_ref, wq_ref, k_ref, v_ref, wo_ref, g_ref, b_ref, wr_ref, br_ref,
                  h2_ref, h2c_ref, lg_ref):
    h1 = h_ref[...]
    q = _dot(h1.astype(BF16), wq_ref[...]).astype(BF16)
    outs = []
    for hd in range(MEM_HEADS):
        sl = slice(hd * MEM_HEAD_DIM, (hd + 1) * MEM_HEAD_DIM)
        s = _dot_nt(q[:, sl], k_ref[:, sl]) * (MEM_HEAD_DIM ** -0.5)
        s = s - jnp.max(s, axis=-1, keepdims=True)
        p = jnp.exp(s)
        p = p / jnp.sum(p, axis=-1, keepdims=True)
        outs.append(_dot(p.astype(BF16), v_ref[:, sl]))
    o = jnp.concatenate(outs, axis=-1).astype(BF16)
    cross = _dot(o, wo_ref[...])
    h2 = _layer_norm(DEEPNORM_ALPHA * h1 + cross, g_ref[...], b_ref[...])
    h2_ref[...] = h2
    _store_row_chunks(h2c_ref, h2)
    lg_ref[...] = _dot(h2.astype(BF16), wr_ref[...]) + br_ref[...]


def _cross_attention(h1, wq_b, k, v, wo_b, g, b, w_route, b_route):
    s = h1.shape[0]
    tm = min(s, 256)
    const = lambda shape: pl.BlockSpec(shape, lambda i: (0, 0), pipeline_mode=pl.Buffered(1))
    return pl.pallas_call(
        _cross_kernel,
        out_shape=(jax.ShapeDtypeStruct((s, D_MODEL), F32),
                   jax.ShapeDtypeStruct((s * ROW_CHUNKS, LANES), F32),
                   jax.ShapeDtypeStruct((s, LANES), F32)),
        grid=(s // tm,),
        in_specs=[pl.BlockSpec((tm, D_MODEL), lambda i: (i, 0)),
                  const((D_MODEL, D_MODEL)),
                  const((MEM_LEN, D_MODEL)),
                  const((MEM_LEN, D_MODEL)),
                  const((D_MODEL, D_MODEL)),
                  const((1, D_MODEL)),
                  const((1, D_MODEL)),
                  const((D_MODEL, LANES)),
                  const((1, LANES))],
        out_specs=(pl.BlockSpec((tm, D_MODEL), lambda i: (i, 0)),
                   pl.BlockSpec((tm * ROW_CHUNKS, LANES), lambda i: (i, 0)),
                   pl.BlockSpec((tm, LANES), lambda i: (i, 0))),
        compiler_params=_cparams(("arbitrary",)),
        name="cross_attn_ln2",
    )(h1, wq_b, k, v, wo_b, g, b, w_route, b_route)


def _route_kernel(lg_ref, dest_ref, gate_ref, be_ref, cnt_row, cnt_col, pstart, run):
    phase = pl.program_id(0)
    i = pl.program_id(1)
    tm = lg_ref.shape[0]
    neg = -jnp.inf

    logits = lg_ref[...]
    lane = lax.broadcasted_iota(jnp.int32, (tm, LANES), 1)
    gmask = lane < N_GROUPS
    gl = jnp.where(gmask, logits, neg)
    ge = jnp.exp(gl - jnp.max(gl, axis=-1, keepdims=True))
    pg = ge / jnp.sum(ge, axis=-1, keepdims=True)
    pg_sel = jnp.max(pg, axis=-1, keepdims=True)
    grp = jnp.min(jnp.where((pg == pg_sel) & gmask, lane, LANES), axis=-1, keepdims=True)

    fl_lane = lane - N_GROUPS
    fmask = (fl_lane >= 0) & (fl_lane < N_EXPERTS) & ((fl_lane >> 3) == grp)
    fl = jnp.where(fmask, logits, neg)
    fe = jnp.exp(fl - jnp.max(fl, axis=-1, keepdims=True))
    fp = fe / jnp.sum(fe, axis=-1, keepdims=True)
    p1 = jnp.max(fp, axis=-1, keepdims=True)
    i1 = jnp.min(jnp.where((fp == p1) & fmask, lane, LANES), axis=-1, keepdims=True)
    rest = fmask & (lane != i1)
    fp2 = jnp.where(rest, fp, -1.0)
    p2 = jnp.max(fp2, axis=-1, keepdims=True)
    i2 = jnp.min(jnp.where((fp2 == p2) & rest, lane, LANES), axis=-1, keepdims=True)
    psum = p1 + p2
    gate1 = pg_sel * p1 / psum
    gate2 = pg_sel * p2 / psum

    oh1 = lane == (i1 - N_GROUPS)
    oh2 = lane == (i2 - N_GROUPS)
    oh = (jnp.where(oh1, 1.0, 0.0) + jnp.where(oh2, 1.0, 0.0)).astype(BF16)
    ones = jnp.ones((tm, LANES), BF16)

    @pl.when((phase == 0) & (i == 0))
    def _():
        cnt_row[...] = jnp.zeros_like(cnt_row)
        cnt_col[...] = jnp.zeros_like(cnt_col)

    @pl.when(phase == 0)
    def _():
        cnt_row[...] += _dot_tn(ones, oh)
        cnt_col[...] += _dot_tn(oh, ones)

    @pl.when((phase == 1) & (i == 0))
    def _():
        nblk_row = jnp.floor((cnt_row[...] + (MOE_BLK - 1.0)) * (1.0 / MOE_BLK))
        nblk_col = jnp.floor((cnt_col[...] + (MOE_BLK - 1.0)) * (1.0 / MOE_BLK))
        r = lax.broadcasted_iota(jnp.int32, (LANES, LANES), 0)
        c = lax.broadcasted_iota(jnp.int32, (LANES, LANES), 1)
        strict_upper = jnp.where(r < c, 1.0, 0.0).astype(BF16)
        lower_incl = jnp.where(c <= r, 1.0, 0.0).astype(BF16)
        pstart[...] = _dot(nblk_row[0:8, :].astype(BF16), strict_upper) * float(MOE_BLK)
        ends = _dot(lower_incl, nblk_col.astype(BF16))
        expert_rows = r < N_EXPERTS
        be = jnp.sum(jnp.where(expert_rows & (ends <= c.astype(F32)), 1.0, 0.0), axis=0, keepdims=True)
        be = jnp.minimum(be, N_EXPERTS - 1.0)
        total = jnp.sum(jnp.where(r < N_EXPERTS, nblk_col, 0.0), axis=0, keepdims=True)
        sub = lax.broadcasted_iota(jnp.int32, be_ref.shape, 0)
        be_ref[...] = jnp.where(sub == 0, jnp.broadcast_to(be, be_ref.shape),
                                jnp.broadcast_to(total, be_ref.shape)).astype(jnp.int32)
        run[...] = jnp.zeros_like(run)

    @pl.when(phase == 1)
    def _():
        tr = lax.broadcasted_iota(jnp.int32, (tm, tm), 0)
        tc = lax.broadcasted_iota(jnp.int32, (tm, tm), 1)
        strict_lower = jnp.where(tc < tr, 1.0, 0.0).astype(BF16)
        before = _dot(strict_lower, oh) + run[0:1, :] + pstart[0:1, :]
        d1 = jnp.sum(jnp.where(oh1, before, 0.0), axis=-1, keepdims=True)
        d2 = jnp.sum(jnp.where(oh2, before, 0.0), axis=-1, keepdims=True)
        dest_ref[...] = jnp.where(lane == 0, d1, jnp.where(lane == 1, d2, 0.0)).astype(jnp.int32)
        gate_ref[...] = jnp.where(lane == 0, gate1, jnp.where(lane == 1, gate2, 0.0))
        run[...] += _dot_tn(ones, oh)[0:8, :]


def _route(logits):
    t = logits.shape[0]
    tm = min(t, 512)
    nt = t // tm
    return pl.pallas_call(
        _route_kernel,
        out_shape=(jax.ShapeDtypeStruct((t, LANES), jnp.int32),
                   jax.ShapeDtypeStruct((t, LANES), F32),
                   jax.ShapeDtypeStruct((8, LANES), jnp.int32)),
        grid=(2, nt),
        in_specs=[pl.BlockSpec((tm, LANES), lambda p, i: (i, 0))],
        out_specs=(pl.BlockSpec((tm, LANES), lambda p, i: (i * p, 0)),
                   pl.BlockSpec((tm, LANES), lambda p, i: (i * p, 0)),
                   pl.BlockSpec((8, LANES), lambda p, i: (0, 0))),
        scratch_shapes=[pltpu.VMEM((LANES, LANES), F32),
                        pltpu.VMEM((LANES, LANES), F32),
                        pltpu.VMEM((8, LANES), F32),
                        pltpu.VMEM((8, LANES), F32)],
        compiler_params=_cparams(("arbitrary", "arbitrary")),
        name="moe_route",
    )(logits)


def _dispatch_kernel(dest_ref, x_ref, xb_in_ref, xb_ref, sem):
    del xb_in_ref
    i = pl.program_id(0)
    tm = x_ref.shape[0] // ROW_CHUNKS

    def row_copy(src_row, dst_row):
        return pltpu.make_async_copy(
            x_ref.at[pl.ds(pl.multiple_of(src_row * ROW_CHUNKS, ROW_CHUNKS), ROW_CHUNKS), :],
            xb_ref.at[pl.ds(pl.multiple_of(dst_row * ROW_CHUNKS, ROW_CHUNKS), ROW_CHUNKS), :],
            sem)

    def issue(r, carry):
        a = (i * tm + r) * 2
        row_copy(r, dest_ref[a]).start()
        row_copy(r, dest_ref[a + 1]).start()
        return carry

    lax.fori_loop(0, tm, issue, 0)

    def drain(r, carry):
        row_copy(0, 0).wait()
        row_copy(0, 0).wait()
        return carry

    lax.fori_loop(0, tm, drain, 0)


def _dispatch(dest_flat, h2c, xb_init):
    rows = h2c.shape[0]
    t = rows // ROW_CHUNKS
    tm = min(t, 256)
    grid_spec = pltpu.PrefetchScalarGridSpec(
        num_scalar_prefetch=1,
        grid=(t // tm,),
        in_specs=[pl.BlockSpec((tm * ROW_CHUNKS, LANES), lambda i, d: (i, 0)),
                  pl.BlockSpec(memory_space=pl.ANY)],
        out_specs=pl.BlockSpec(memory_space=pl.ANY),
        scratch_shapes=[pltpu.SemaphoreType.DMA(())],
    )
    return pl.pallas_call(
        _dispatch_kernel,
        out_shape=jax.ShapeDtypeStruct(xb_init.shape, xb_init.dtype),
        grid_spec=grid_spec,
        input_output_aliases={2: 0},
        compiler_params=pltpu.CompilerParams(dimension_semantics=("arbitrary",),
                                             vmem_limit_bytes=VMEM_LIMIT,
                                             has_side_effects=True),
        name="moe_dispatch",
    )(dest_flat, h2c, xb_init)


def _expert_kernel(be_ref, nu_ref, x_ref, wg_ref, wu_ref, wd_ref, y_ref, wg_b, wu_b, wd_b):
    b = pl.program_id(0)
    used = b < nu_ref[0]
    prev = be_ref[jnp.maximum(b - 1, 0)]
    fresh = used & ((b == 0) | (be_ref[b] != prev))

    @pl.when(fresh)
    def _():
        wg_b[...] = wg_ref[0].astype(BF16)
        wu_b[...] = wu_ref[0].astype(BF16)
        wd_b[...] = wd_ref[0].astype(BF16)

    @pl.when(used)
    def _():
        x = _load_row_chunks(x_ref, MOE_BLK).astype(BF16)
        hid = _silu(_dot(x, wg_b[...])) * _dot(x, wu_b[...])
        _store_row_chunks(y_ref, _dot(hid.astype(BF16), wd_b[...]))

    @pl.when(jnp.logical_not(used))
    def _():
        y_ref[...] = jnp.zeros_like(y_ref)


def _experts(block_e, n_used, xb, w_gate, w_up, w_down):
    nb = xb.shape[0] // (MOE_BLK * ROW_CHUNKS)
    blk = lambda b, be, nu: jnp.minimum(b, nu[0] - 1)
    wmap = lambda b, be, nu: (be[blk(b, be, nu)], 0, 0)
    grid_spec = pltpu.PrefetchScalarGridSpec(
        num_scalar_prefetch=2,
        grid=(nb,),
        in_specs=[pl.BlockSpec((MOE_BLK * ROW_CHUNKS, LANES), lambda b, be, nu: (blk(b, be, nu), 0)),
                  pl.BlockSpec((1, D_MODEL, EXPERT_FF), wmap),
                  pl.BlockSpec((1, D_MODEL, EXPERT_FF), wmap),
                  pl.BlockSpec((1, EXPERT_FF, D_MODEL), wmap)],
        out_specs=pl.BlockSpec((MOE_BLK * ROW_CHUNKS, LANES), lambda b, be, nu: (b, 0)),
        scratch_shapes=[pltpu.VMEM((D_MODEL, EXPERT_FF), BF16),
                        pltpu.VMEM((D_MODEL, EXPERT_FF), BF16),
                        pltpu.VMEM((EXPERT_FF, D_MODEL), BF16)],
    )
    return pl.pallas_call(
        _expert_kernel,
        out_shape=jax.ShapeDtypeStruct(xb.shape, F32),
        grid_spec=grid_spec,
        compiler_params=_cparams(("arbitrary",)),
        name="moe_experts",
    )(block_e, n_used, xb, w_gate, w_up, w_down)


def _combine_kernel(dest_ref, yb_ref, h2_ref, gate_ref, g_ref, b_ref, o_ref, buf0, buf1, sems):
    i = pl.program_id(0)
    n = pl.num_programs(0)
    tm = h2_ref.shape[0]
    bufs = (buf0, buf1)

    def row_copy(k, slot, dst_row, src_row):
        return pltpu.make_async_copy(
            yb_ref.at[pl.ds(pl.multiple_of(src_row * ROW_CHUNKS, ROW_CHUNKS), ROW_CHUNKS), :],
            bufs[k].at[slot, pl.ds(pl.multiple_of(dst_row * ROW_CHUNKS, ROW_CHUNKS), ROW_CHUNKS), :],
            sems.at[slot])

    def issue_tile(tile, slot):
        def body(r, carry):
            a = (tile * tm + r) * 2
            row_copy(0, slot, r, dest_ref[a]).start()
            row_copy(1, slot, r, dest_ref[a + 1]).start()
            return carry
        lax.fori_loop(0, tm, body, 0)

    def wait_tile(slot):
        def body(r, carry):
            row_copy(0, slot, 0, 0).wait()
            row_copy(1, slot, 0, 0).wait()
            return carry
        lax.fori_loop(0, tm, body, 0)

    slot = i % 2

    @pl.when(i == 0)
    def _():
        issue_tile(0, 0)

    @pl.when(i + 1 < n)
    def _():
        issue_tile(i + 1, 1 - slot)

    wait_tile(slot)
    y0 = _load_row_chunks(buf0.at[slot], tm)
    y1 = _load_row_chunks(buf1.at[slot], tm)
    gate = gate_ref[...]
    ffn = gate[:, 0:1] * y0 + gate[:, 1:2] * y1
    o_ref[...] = _layer_norm(DEEPNORM_ALPHA * h2_ref[...] + ffn, g_ref[...], b_ref[...])


def _combine_ln(dest_flat, yb, h2, gate, g, b):
    t = h2.shape[0]
    tm = min(t, 256)
    grid_spec = pltpu.PrefetchScalarGridSpec(
        num_scalar_prefetch=1,
        grid=(t // tm,),
        in_specs=[pl.BlockSpec(memory_space=pl.ANY),
                  pl.BlockSpec((tm, D_MODEL), lambda i, d: (i, 0)),
                  pl.BlockSpec((tm, LANES), lambda i, d: (i, 0)),
                  pl.BlockSpec((1, D_MODEL), lambda i, d: (0, 0)),
                  pl.BlockSpec((1, D_MODEL), lambda i, d: (0, 0))],
        out_specs=pl.BlockSpec((tm, D_MODEL), lambda i, d: (i, 0)),
        scratch_shapes=[pltpu.VMEM((2, tm * ROW_CHUNKS, LANES), F32),
                        pltpu.VMEM((2, tm * ROW_CHUNKS, LANES), F32),
                        pltpu.SemaphoreType.DMA((2,))],
    )
    return pl.pallas_call(
        _combine_kernel,
        out_shape=jax.ShapeDtypeStruct((t, D_MODEL), F32),
        grid_spec=grid_spec,
        compiler_params=_cparams(("arbitrary",)),
        name="moe_combine_ln3",
    )(dest_flat, yb, h2, gate, g, b)


def _mixer(x2d, positions, w_in, w_gla_a2, b_gla_a, g_gla_norm):
    s = x2d.shape[0]
    xb = x2d.astype(BF16)
    w_in_b = w_in.astype(BF16)
    half = RET_DK // 2
    inv_freq = (ROPE_BASE ** (-jnp.arange(half, dtype=F32) / half)).reshape(1, half)
    cos, sin = _rope_table(positions.reshape(s, 1), inv_freq)
    h_main = _proj_in(xb, w_in_b)
    w_lr = jnp.pad(w_in_b[:, GLR_OFF:], ((0, 0), (0, LANES - GLA_LOWRANK)))
    w_a2 = jnp.pad(w_gla_a2.astype(BF16), ((0, LANES - GLA_LOWRANK), (0, 0)))
    log_a = _gla_log_decay(xb, w_lr, w_a2, b_gla_a.reshape(1, -1))
    log_gamma = jnp.log1p(-jnp.exp2(-5.0 - jnp.arange(RET_HEADS, dtype=F32)))
    ret = _retention(h_main, cos, sin, log_gamma)
    gla = _gla(h_main, log_a, g_gla_norm.reshape(1, -1))
    return ret, gla


def _moe(h2, h2c, logits, w_gate, w_up, w_down, g, b):
    t = h2.shape[0]
    dest, gate, plan = _route(logits)
    dest_flat = dest[:, :2].reshape(-1)
    nb = (t * 2 + N_EXPERTS * MOE_BLK) // MOE_BLK
    block_e = plan[0, :nb]
    n_used = plan[1, :1]
    xb_init = jnp.zeros((nb * MOE_BLK * ROW_CHUNKS, LANES), F32)
    xb = _dispatch(dest_flat, h2c, xb_init)
    yb = _experts(block_e, n_used, xb, w_gate, w_up, w_down)
    return _combine_ln(dest_flat, yb, h2, gate, g, b)


def kernel(x, mem, positions, w_in, w_gla_a2, b_gla_a, g_gla_norm, w_mix_out, ln1_g, ln1_b, w_mq, w_mk, w_mv, w_mo, ln2_g, ln2_b, w_route_group, b_route_group, w_route_expert, b_route_expert, w_exp_gate, w_exp_up, w_exp_down, ln3_g, ln3_b):
    bsz, s, d = x.shape
    assert bsz == 1 and d == D_MODEL
    x2d = x.reshape(s, d)
    row = lambda v: v.reshape(1, -1)

    ret, gla = _mixer(x2d, positions, w_in[0], w_gla_a2[0], b_gla_a[0], g_gla_norm[0])
    h1 = _mixout_ln(ret, gla, x2d, w_mix_out[0].astype(BF16), row(ln1_g[0]), row(ln1_b[0]))

    k, v = _mem_kv(mem[0].astype(BF16), w_mk[0].astype(BF16), w_mv[0].astype(BF16))
    n_route = N_GROUPS + N_EXPERTS
    w_route = jnp.pad(jnp.concatenate([w_route_group[0], w_route_expert[0]], axis=1).astype(BF16),
                      ((0, 0), (0, LANES - n_route)))
    b_route = jnp.pad(jnp.concatenate([b_route_group[0], b_route_expert[0].reshape(-1)]),
                      (0, LANES - n_route)).reshape(1, LANES)
    h2, h2c, logits = _cross_attention(h1, w_mq[0].astype(BF16), k, v, w_mo[0].astype(BF16),
                                       row(ln2_g[0]), row(ln2_b[0]), w_route, b_route)

    out = _moe(h2, h2c, logits, w_exp_gate[0], w_exp_up[0], w_exp_down[0],
               row(ln3_g[0]), row(ln3_b[0]))
    return out.reshape(bsz, s, d)
```

```python
import functools
import math

import jax
import jax.numpy as jnp
from jax import lax
from jax.experimental import pallas as pl
from jax.experimental.pallas import tpu as pltpu

F32 = jnp.float32
BF16 = jnp.bfloat16

D_MODEL = 2048
MEM_LEN = 256
RET_HEADS = 4
RET_DK = 256
RET_DV = 256
GLA_HEADS = 4
GLA_DK = 128
GLA_DV = 256
GLA_LOWRANK = 16
GLA_TAU = 16.0
ROPE_BASE = 10000.0
MEM_HEADS = 4
MEM_HEAD_DIM = D_MODEL // MEM_HEADS
N_GROUPS = 4
EXPERTS_PER_GROUP = 8
N_EXPERTS = N_GROUPS * EXPERTS_PER_GROUP
EXPERT_FF = 512
LN_EPS = 1e-5
DEPTH = 1
DEEPNORM_ALPHA = (2 * DEPTH) ** 0.25

RQ_OFF, RK_OFF, RV_OFF, RG_OFF = 0, 1024, 2048, 3072
GQ_OFF, GK_OFF, GV_OFF, GG_OFF, GLR_OFF = 4096, 4608, 5120, 6144, 7168
IN_MAIN = 7168

LANES = 128
RET_CHUNK = 256
GLA_CHUNK = 128
GLA_LEVELS = 7
MOE_BLK = 256
VMEM_LIMIT = 56 * 1024 * 1024


def _cparams(sem):
    return pltpu.CompilerParams(dimension_semantics=sem, vmem_limit_bytes=VMEM_LIMIT)


def _layer_norm(y, g, b):
    mu = jnp.mean(y, axis=-1, keepdims=True)
    d = y - mu
    var = jnp.mean(d * d, axis=-1, keepdims=True)
    return d * lax.rsqrt(var + LN_EPS) * g + b


def _silu(x):
    return x / (1.0 + jnp.exp(-x))


def _dot(a, b):
    return jnp.dot(a, b, preferred_element_type=F32)


def _dot_nt(a, b):
    return lax.dot_general(a, b, (((1,), (1,)), ((), ())), preferred_element_type=F32)


def _dot_tn(a, b):
    return lax.dot_general(a, b, (((0,), (0,)), ((), ())), preferred_element_type=F32)


def _rope_kernel(pos_ref, invf_ref, cos_ref, sin_ref):
    ang = pos_ref[...].astype(F32) * invf_ref[...]
    cos_ref[...] = jnp.cos(ang)
    sin_ref[...] = jnp.sin(ang)


def _rope_table(pos_col, inv_freq):
    s = pos_col.shape[0]
    tm = min(s, 1024)
    half = inv_freq.shape[1]
    return pl.pallas_call(
        _rope_kernel,
        out_shape=(jax.ShapeDtypeStruct((s, half), F32), jax.ShapeDtypeStruct((s, half), F32)),
        grid=(s // tm,),
        in_specs=[pl.BlockSpec((tm, 1), lambda i: (i, 0)),
                  pl.BlockSpec((1, half), lambda i: (0, 0))],
        out_specs=(pl.BlockSpec((tm, half), lambda i: (i, 0)),
                   pl.BlockSpec((tm, half), lambda i: (i, 0))),
        compiler_params=_cparams(("arbitrary",)),
        name="rope_table",
    )(pos_col, inv_freq)


def _mm_kernel(x_ref, w_ref, o_ref):
    o_ref[...] = _dot(x_ref[...], w_ref[...]).astype(o_ref.dtype)


def _proj_in(xb, w_in_b):
    s = xb.shape[0]
    tm = min(s, 1024)
    tn = 512
    return pl.pallas_call(
        _mm_kernel,
        out_shape=jax.ShapeDtypeStruct((s, IN_MAIN), F32),
        grid=(s // tm, IN_MAIN // tn),
        in_specs=[pl.BlockSpec((tm, D_MODEL), lambda i, j: (i, 0)),
                  pl.BlockSpec((D_MODEL, tn), lambda i, j: (0, j))],
        out_specs=pl.BlockSpec((tm, tn), lambda i, j: (i, j)),
        compiler_params=_cparams(("arbitrary", "arbitrary")),
        name="proj_in",
    )(xb, w_in_b)


def _loga_kernel(x_ref, wlr_ref, wa2_ref, ba_ref, o_ref):
    glr = _dot(x_ref[...], wlr_ref[...])
    z = _dot(glr.astype(BF16), wa2_ref[...]) + ba_ref[...]
    log_sig = jnp.minimum(z, 0.0) - jnp.log1p(jnp.exp(-jnp.abs(z)))
    o_ref[...] = log_sig / GLA_TAU


def _gla_log_decay(xb, w_lr, w_a2, b_a):
    s = xb.shape[0]
    tm = min(s, 1024)
    n = GLA_HEADS * GLA_DK
    return pl.pallas_call(
        _loga_kernel,
        out_shape=jax.ShapeDtypeStruct((s, n), F32),
        grid=(s // tm,),
        in_specs=[pl.BlockSpec((tm, D_MODEL), lambda i: (i, 0)),
                  pl.BlockSpec((D_MODEL, LANES), lambda i: (0, 0)),
                  pl.BlockSpec((LANES, n), lambda i: (0, 0)),
                  pl.BlockSpec((1, n), lambda i: (0, 0))],
        out_specs=pl.BlockSpec((tm, n), lambda i: (i, 0)),
        compiler_params=_cparams(("arbitrary",)),
        name="gla_log_decay",
    )(xb, w_lr, w_a2, b_a)


def _rotary(t, cos, sin):
    half = t.shape[-1] // 2
    t1, t2 = t[:, :half], t[:, half:]
    return jnp.concatenate([t1 * cos - t2 * sin, t1 * sin + t2 * cos], axis=-1)


def _retention_kernel(lg_ref, q_ref, k_ref, v_ref, g_ref, cos_ref, sin_ref, o_ref, state_ref):
    c = pl.program_id(0)
    C = q_ref.shape[0]

    @pl.when(c == 0)
    def _():
        state_ref[...] = jnp.zeros_like(state_ref)

    cos = cos_ref[...]
    sin = sin_ref[...]
    ri = lax.broadcasted_iota(jnp.int32, (C, C), 0)
    ci = lax.broadcasted_iota(jnp.int32, (C, C), 1)
    rel = jnp.maximum(ri - ci, 0).astype(F32)
    n = lax.broadcasted_iota(jnp.int32, (C, 1), 0).astype(F32)

    for h in range(RET_HEADS):
        lg = lg_ref[h]
        sl = slice(h * RET_DK, (h + 1) * RET_DK)
        q = _rotary(q_ref[:, sl], cos, sin)
        k = _rotary(k_ref[:, sl], cos, sin) * (RET_DK ** -0.5)
        v = v_ref[:, sl].astype(BF16)
        decay_intra = jnp.where(ri >= ci, jnp.exp(lg * rel), 0.0)
        decay_q = jnp.exp(lg * (n + 1.0))
        decay_k = jnp.exp(lg * (C - 1.0 - n))
        decay_chunk = jnp.exp(lg * C)

        state = state_ref[h]
        scores = _dot_nt(q.astype(BF16), k.astype(BF16)) * decay_intra
        o = _dot(scores.astype(BF16), v) + _dot((q * decay_q).astype(BF16), state.astype(BF16))
        state_ref[h] = decay_chunk * state + _dot_tn((k * decay_k).astype(BF16), v)

        mu = jnp.mean(o, axis=-1, keepdims=True)
        d = o - mu
        var = jnp.mean(d * d, axis=-1, keepdims=True)
        o = d * lax.rsqrt(var + LN_EPS)
        o_ref[:, sl] = (_silu(g_ref[:, sl]) * o).astype(o_ref.dtype)


def _retention(h_main, cos, sin, log_gamma):
    s = h_main.shape[0]
    C = min(RET_CHUNK, s)
    w = RET_HEADS * RET_DK
    col = lambda off: (lambda c, lg: (c, off // w))
    grid_spec = pltpu.PrefetchScalarGridSpec(
        num_scalar_prefetch=1,
        grid=(s // C,),
        in_specs=[pl.BlockSpec((C, w), col(RQ_OFF)),
                  pl.BlockSpec((C, w), col(RK_OFF)),
                  pl.BlockSpec((C, w), col(RV_OFF)),
                  pl.BlockSpec((C, w), col(RG_OFF)),
                  pl.BlockSpec((C, RET_DK // 2), lambda c, lg: (c, 0)),
                  pl.BlockSpec((C, RET_DK // 2), lambda c, lg: (c, 0))],
        out_specs=pl.BlockSpec((C, RET_HEADS * RET_DV), lambda c, lg: (c, 0)),
        scratch_shapes=[pltpu.VMEM((RET_HEADS, RET_DK, RET_DV), F32)],
    )
    return pl.pallas_call(
        _retention_kernel,
        out_shape=jax.ShapeDtypeStruct((s, RET_HEADS * RET_DV), BF16),
        grid_spec=grid_spec,
        compiler_params=_cparams(("arbitrary",)),
        name="retention",
    )(log_gamma, h_main, h_main, h_main, h_main, cos, sin)


def _gla_decay_matrix(C):
    import numpy as np
    levels = int(math.log2(C))
    r = np.arange(C)[:, None]
    t = np.arange(C)[None, :]
    mats = []
    for l in range(levels):
        blk = C >> l
        half = blk // 2
        m = (r // blk) * blk + half - 1
        qside = (r % blk) >= half
        mats.append(np.where(qside, (t > m) & (t <= r), (t > r) & (t <= m)))
    mats.append(t <= r)
    mats.append(t > r)
    return np.concatenate(mats, axis=0).astype(np.float32)


def _gla_kernel(m_ref, q_ref, k_ref, v_ref, g_ref, la_ref, gn_ref, o_ref, state_ref):
    c = pl.program_id(0)
    C = q_ref.shape[0]
    levels = GLA_LEVELS

    @pl.when(c == 0)
    def _():
        state_ref[...] = jnp.zeros_like(state_ref)

    m = m_ref[...]
    ri = lax.broadcasted_iota(jnp.int32, (C, C), 0)
    ci = lax.broadcasted_iota(jnp.int32, (C, C), 1)
    xor = jnp.where(ri > ci, ri ^ ci, 0)
    row = lax.broadcasted_iota(jnp.int32, (C, 1), 0)

    for h in range(GLA_HEADS):
        ks = slice(h * GLA_DK, (h + 1) * GLA_DK)
        vs = slice(h * GLA_DV, (h + 1) * GLA_DV)
        q = q_ref[:, ks] * (GLA_DK ** -0.5)
        k = k_ref[:, ks]
        v = v_ref[:, vs].astype(BF16)
        la = la_ref[:, ks]
        la_hi = la.astype(BF16)
        la_lo = (la - la_hi.astype(F32)).astype(BF16)
        expo = jnp.exp(_dot(m, la_hi) + _dot(m, la_lo))

        scores = jnp.where(ri == ci, _dot_nt(q.astype(BF16), k.astype(BF16)), 0.0)
        for l in range(levels):
            half = C >> (l + 1)
            shift = int(math.log2(half))
            qside = (row & half) != 0
            x = (jnp.where(qside, q, k) * expo[l * C:(l + 1) * C]).astype(BF16)
            scores = scores + jnp.where((xor >> shift) == 1, _dot_nt(x, x), 0.0)

        e_b = expo[levels * C:(levels + 1) * C]
        e_rev = expo[(levels + 1) * C:(levels + 2) * C]
        e_last = e_b[C - 1:C, :]

        state = state_ref[h]
        o = _dot(scores.astype(BF16), v) + _dot_nt((q * e_b).astype(BF16), state.astype(BF16))
        state_ref[h] = state * e_last + _dot_tn(v, (k * e_rev).astype(BF16))

        o = o * lax.rsqrt(jnp.mean(o * o, axis=-1, keepdims=True) + LN_EPS) * gn_ref[...]
        o_ref[:, vs] = (_silu(g_ref[:, vs]) * o).astype(o_ref.dtype)


def _gla(h_main, log_a, g_norm):
    s = h_main.shape[0]
    C = GLA_CHUNK
    m = jnp.asarray(_gla_decay_matrix(C), dtype=BF16)
    nrow = m.shape[0]
    wk = GLA_HEADS * GLA_DK
    wv = GLA_HEADS * GLA_DV
    return pl.pallas_call(
        _gla_kernel,
        out_shape=jax.ShapeDtypeStruct((s, wv), BF16),
        grid=(s // C,),
        in_specs=[pl.BlockSpec((nrow, C), lambda c: (0, 0)),
                  pl.BlockSpec((C, wk), lambda c: (c, GQ_OFF // wk)),
                  pl.BlockSpec((C, wk), lambda c: (c, GK_OFF // wk)),
                  pl.BlockSpec((C, wv), lambda c: (c, GV_OFF // wv)),
                  pl.BlockSpec((C, wv), lambda c: (c, GG_OFF // wv)),
                  pl.BlockSpec((C, wk), lambda c: (c, 0)),
                  pl.BlockSpec((1, GLA_DV), lambda c: (0, 0))],
        out_specs=pl.BlockSpec((C, wv), lambda c: (c, 0)),
        scratch_shapes=[pltpu.VMEM((GLA_HEADS, GLA_DV, GLA_DK), F32)],
        compiler_params=_cparams(("arbitrary",)),
        name="gla",
    )(m, h_main, h_main, h_main, h_main, log_a, g_norm)


def _mixout_kernel(ret_ref, gla_ref, x_ref, w_ref, g_ref, b_ref, o_ref):
    nr = ret_ref.shape[1]
    mix = _dot(ret_ref[...], w_ref[:nr, :]) + _dot(gla_ref[...], w_ref[nr:, :])
    o_ref[...] = _layer_norm(DEEPNORM_ALPHA * x_ref[...] + mix, g_ref[...], b_ref[...])


def _mixout_ln(ret, gla, x2d, w_b, g, b):
    s = x2d.shape[0]
    tm = min(s, 512)
    nr, ng = ret.shape[1], gla.shape[1]
    return pl.pallas_call(
        _mixout_kernel,
        out_shape=jax.ShapeDtypeStruct((s, D_MODEL), F32),
        grid=(s // tm,),
        in_specs=[pl.BlockSpec((tm, nr), lambda i: (i, 0)),
                  pl.BlockSpec((tm, ng), lambda i: (i, 0)),
                  pl.BlockSpec((tm, D_MODEL), lambda i: (i, 0)),
                  pl.BlockSpec((nr + ng, D_MODEL), lambda i: (0, 0)),
                  pl.BlockSpec((1, D_MODEL), lambda i: (0, 0)),
                  pl.BlockSpec((1, D_MODEL), lambda i: (0, 0))],
        out_specs=pl.BlockSpec((tm, D_MODEL), lambda i: (i, 0)),
        compiler_params=_cparams(("arbitrary",)),
        name="mixout_ln1",
    )(ret, gla, x2d, w_b, g, b)


def _kv_kernel(mem_ref, wk_ref, wv_ref, k_ref, v_ref):
    m = mem_ref[...]
    k_ref[...] = _dot(m, wk_ref[...]).astype(k_ref.dtype)
    v_ref[...] = _dot(m, wv_ref[...]).astype(v_ref.dtype)


def _mem_kv(mem_b, wk_b, wv_b):
    tn = 512
    return pl.pallas_call(
        _kv_kernel,
        out_shape=(jax.ShapeDtypeStruct((MEM_LEN, D_MODEL), BF16),
                   jax.ShapeDtypeStruct((MEM_LEN, D_MODEL), BF16)),
        grid=(D_MODEL // tn,),
        in_specs=[pl.BlockSpec((MEM_LEN, D_MODEL), lambda j: (0, 0)),
                  pl.BlockSpec((D_MODEL, tn), lambda j: (0, j)),
                  pl.BlockSpec((D_MODEL, tn), lambda j: (0, j))],
        out_specs=(pl.BlockSpec((MEM_LEN, tn), lambda j: (0, j)),
                   pl.BlockSpec((MEM_LEN, tn), lambda j: (0, j))),
        compiler_params=_cparams(("arbitrary",)),
        name="mem_kv",
    )(mem_b, wk_b, wv_b)


def _cross_kernel(h_ref, wq_ref, k_ref, v_ref, wo_ref, g_ref, b_ref, wr_ref, br_ref,
                  h2_ref, lg_ref):
    h1 = h_ref[...]
    q = _dot(h1.astype(BF16), wq_ref[...]).astype(BF16)
    outs = []
    for hd in range(MEM_HEADS):
        sl = slice(hd * MEM_HEAD_DIM, (hd + 1) * MEM_HEAD_DIM)
        s = _dot_nt(q[:, sl], k_ref[:, sl]) * (MEM_HEAD_DIM ** -0.5)
        s = s - jnp.max(s, axis=-1, keepdims=True)
        p = jnp.exp(s)
        p = p / jnp.sum(p, axis=-1, keepdims=True)
        outs.append(_dot(p.astype(BF16), v_ref[:, sl]))
    o = jnp.concatenate(outs, axis=-1).astype(BF16)
    cross = _dot(o, wo_ref[...])
    h2 = _layer_norm(DEEPNORM_ALPHA * h1 + cross, g_ref[...], b_ref[...])
    h2_ref[...] = h2
    lg_ref[...] = _dot(h2.astype(BF16), wr_ref[...]) + br_ref[...]


def _cross_attention(h1, wq_b, k, v, wo_b, g, b, w_route, b_route):
    s = h1.shape[0]
    tm = min(s, 256)
    const = lambda shape: pl.BlockSpec(shape, lambda i: (0, 0), pipeline_mode=pl.Buffered(1))
    return pl.pallas_call(
        _cross_kernel,
        out_shape=(jax.ShapeDtypeStruct((s, D_MODEL), F32),
                   jax.ShapeDtypeStruct((s, LANES), F32)),
        grid=(s // tm,),
        in_specs=[pl.BlockSpec((tm, D_MODEL), lambda i: (i, 0)),
                  const((D_MODEL, D_MODEL)),
                  const((MEM_LEN, D_MODEL)),
                  const((MEM_LEN, D_MODEL)),
                  const((D_MODEL, D_MODEL)),
                  const((1, D_MODEL)),
                  const((1, D_MODEL)),
                  const((D_MODEL, LANES)),
                  const((1, LANES))],
        out_specs=(pl.BlockSpec((tm, D_MODEL), lambda i: (i, 0)),
                   pl.BlockSpec((tm, LANES), lambda i: (i, 0))),
        compiler_params=_cparams(("arbitrary",)),
        name="cross_attn_ln2",
    )(h1, wq_b, k, v, wo_b, g, b, w_route, b_route)


def _route_kernel(lg_ref, dest_ref, gate_ref, be_ref, cnt_row, cnt_col, pstart, run):
    phase = pl.program_id(0)
    i = pl.program_id(1)
    tm = lg_ref.shape[0]
    neg = -jnp.inf

    logits = lg_ref[...]
    lane = lax.broadcasted_iota(jnp.int32, (tm, LANES), 1)
    gmask = lane < N_GROUPS
    gl = jnp.where(gmask, logits, neg)
    ge = jnp.exp(gl - jnp.max(gl, axis=-1, keepdims=True))
    pg = ge / jnp.sum(ge, axis=-1, keepdims=True)
    pg_sel = jnp.max(pg, axis=-1, keepdims=True)
    grp = jnp.min(jnp.where((pg == pg_sel) & gmask, lane, LANES), axis=-1, keepdims=True)

    fl_lane = lane - N_GROUPS
    fmask = (fl_lane >= 0) & (fl_lane < N_EXPERTS) & ((fl_lane >> 3) == grp)
    fl = jnp.where(fmask, logits, neg)
    fe = jnp.exp(fl - jnp.max(fl, axis=-1, keepdims=True))
    fp = fe / jnp.sum(fe, axis=-1, keepdims=True)
    p1 = jnp.max(fp, axis=-1, keepdims=True)
    i1 = jnp.min(jnp.where((fp == p1) & fmask, lane, LANES), axis=-1, keepdims=True)
    rest = fmask & (lane != i1)
    fp2 = jnp.where(rest, fp, -1.0)
    p2 = jnp.max(fp2, axis=-1, keepdims=True)
    i2 = jnp.min(jnp.where((fp2 == p2) & rest, lane, LANES), axis=-1, keepdims=True)
    psum = p1 + p2
    gate1 = pg_sel * p1 / psum
    gate2 = pg_sel * p2 / psum

    oh1 = lane == (i1 - N_GROUPS)
    oh2 = lane == (i2 - N_GROUPS)
    oh = (jnp.where(oh1, 1.0, 0.0) + jnp.where(oh2, 1.0, 0.0)).astype(BF16)
    ones = jnp.ones((tm, LANES), BF16)

    @pl.when((phase == 0) & (i == 0))
    def _():
        cnt_row[...] = jnp.zeros_like(cnt_row)
        cnt_col[...] = jnp.zeros_like(cnt_col)

    @pl.when(phase == 0)
    def _():
        cnt_row[...] += _dot_tn(ones, oh)
        cnt_col[...] += _dot_tn(oh, ones)

    @pl.when((phase == 1) & (i == 0))
    def _():
        nblk_row = jnp.floor((cnt_row[...] + (MOE_BLK - 1.0)) * (1.0 / MOE_BLK))
        nblk_col = jnp.floor((cnt_col[...] + (MOE_BLK - 1.0)) * (1.0 / MOE_BLK))
        r = lax.broadcasted_iota(jnp.int32, (LANES, LANES), 0)
        c = lax.broadcasted_iota(jnp.int32, (LANES, LANES), 1)
        strict_upper = jnp.where(r < c, 1.0, 0.0).astype(BF16)
        lower_incl = jnp.where(c <= r, 1.0, 0.0).astype(BF16)
        pstart[...] = _dot(nblk_row[0:8, :].astype(BF16), strict_upper) * float(MOE_BLK)
        ends = _dot(lower_incl, nblk_col.astype(BF16))
        expert_rows = r < N_EXPERTS
        be = jnp.sum(jnp.where(expert_rows & (ends <= c.astype(F32)), 1.0, 0.0), axis=0, keepdims=True)
        be = jnp.minimum(be, N_EXPERTS - 1.0)
        total = jnp.sum(jnp.where(r < N_EXPERTS, nblk_col, 0.0), axis=0, keepdims=True)
        nblk8 = nblk_row[0:8, :]
        last_blk_row = pstart[...] + (nblk8 - 1.0) * float(MOE_BLK)
        sub = lax.broadcasted_iota(jnp.int32, be_ref.shape, 0)
        plan = jnp.where(sub == 0, jnp.broadcast_to(be, be_ref.shape),
                         jnp.where(sub == 1, jnp.broadcast_to(total, be_ref.shape),
                                   jnp.where(sub == 2, last_blk_row, nblk8)))
        be_ref[...] = plan.astype(jnp.int32)
        run[...] = jnp.zeros_like(run)

    @pl.when(phase == 1)
    def _():
        tr = lax.broadcasted_iota(jnp.int32, (tm, tm), 0)
        tc = lax.broadcasted_iota(jnp.int32, (tm, tm), 1)
        strict_lower = jnp.where(tc < tr, 1.0, 0.0).astype(BF16)
        before = _dot(strict_lower, oh) + run[0:1, :] + pstart[0:1, :]
        d1 = jnp.sum(jnp.where(oh1, before, 0.0), axis=-1, keepdims=True)
        d2 = jnp.sum(jnp.where(oh2, before, 0.0), axis=-1, keepdims=True)
        dest_ref[...] = jnp.where(lane == 0, d1, jnp.where(lane == 1, d2, 0.0)).astype(jnp.int32)
        gate_ref[...] = jnp.where(lane == 0, gate1, jnp.where(lane == 1, gate2, 0.0))
        run[...] += _dot_tn(ones, oh)[0:8, :]


def _route(logits):
    t = logits.shape[0]
    tm = min(t, 512)
    nt = t // tm
    return pl.pallas_call(
        _route_kernel,
        out_shape=(jax.ShapeDtypeStruct((t, LANES), jnp.int32),
                   jax.ShapeDtypeStruct((t, LANES), F32),
                   jax.ShapeDtypeStruct((8, LANES), jnp.int32)),
        grid=(2, nt),
        in_specs=[pl.BlockSpec((tm, LANES), lambda p, i: (i, 0))],
        out_specs=(pl.BlockSpec((tm, LANES), lambda p, i: (i * p, 0)),
                   pl.BlockSpec((tm, LANES), lambda p, i: (i * p, 0)),
                   pl.BlockSpec((8, LANES), lambda p, i: (0, 0))),
        scratch_shapes=[pltpu.VMEM((LANES, LANES), F32),
                        pltpu.VMEM((LANES, LANES), F32),
                        pltpu.VMEM((8, LANES), F32),
                        pltpu.VMEM((8, LANES), F32)],
        compiler_params=_cparams(("arbitrary", "arbitrary")),
        name="moe_route",
    )(logits)


def _dispatch_kernel(dest_ref, zrow_ref, nblk_ref, nu_ref, x_ref, xb_ref, zero_ref, sem, zsem):
    i = pl.program_id(0)
    tm = x_ref.shape[0]
    nb = xb_ref.shape[0] // MOE_BLK

    def zero_copy(row):
        return pltpu.make_async_copy(
            zero_ref, xb_ref.at[pl.ds(pl.multiple_of(row, MOE_BLK), MOE_BLK), :], zsem)

    @pl.when(i == 0)
    def _():
        zero_ref[...] = jnp.zeros_like(zero_ref)

        def issue_zero(e, carry):
            @pl.when(nblk_ref[e] > 0)
            def _():
                zero_copy(zrow_ref[e]).start()
            return carry
        lax.fori_loop(0, N_EXPERTS, issue_zero, 0)
        lax.fori_loop(nu_ref[0], nb, lambda b, c: (zero_copy(b * MOE_BLK).start(), c)[1], 0)

        def wait_zero(e, carry):
            @pl.when(nblk_ref[e] > 0)
            def _():
                zero_copy(0).wait()
            return carry
        lax.fori_loop(0, N_EXPERTS, wait_zero, 0)
        lax.fori_loop(nu_ref[0], nb, lambda b, c: (zero_copy(0).wait(), c)[1], 0)

    def row_copy(src_row, dst_row):
        return pltpu.make_async_copy(
            x_ref.at[pl.ds(src_row, 1), :], xb_ref.at[pl.ds(dst_row, 1), :], sem)

    def issue(r, carry):
        a = (i * tm + r) * 2
        row_copy(r, dest_ref[a]).start()
        row_copy(r, dest_ref[a + 1]).start()
        return carry

    lax.fori_loop(0, tm, issue, 0)

    def drain(r, carry):
        row_copy(0, 0).wait()
        row_copy(0, 0).wait()
        return carry

    lax.fori_loop(0, tm, drain, 0)


def _dispatch(dest_flat, zrow, nblk, n_used, h2, n_rows):
    t = h2.shape[0]
    tm = min(t, 256)
    grid_spec = pltpu.PrefetchScalarGridSpec(
        num_scalar_prefetch=4,
        grid=(t // tm,),
        in_specs=[pl.BlockSpec((tm, D_MODEL), lambda i, d, z, n, u: (i, 0))],
        out_specs=pl.BlockSpec(memory_space=pl.ANY),
        scratch_shapes=[pltpu.VMEM((MOE_BLK, D_MODEL), F32),
                        pltpu.SemaphoreType.DMA(()),
                        pltpu.SemaphoreType.DMA(())],
    )
    return pl.pallas_call(
        _dispatch_kernel,
        out_shape=jax.ShapeDtypeStruct((n_rows, D_MODEL), F32),
        grid_spec=grid_spec,
        compiler_params=_cparams(("arbitrary",)),
        name="moe_dispatch",
    )(dest_flat, zrow, nblk, n_used, h2)


def _expert_kernel(be_ref, nu_ref, x_ref, wg_ref, wu_ref, wd_ref, y_ref, wg_b, wu_b, wd_b):
    b = pl.program_id(0)
    used = b < nu_ref[0]
    prev = be_ref[jnp.maximum(b - 1, 0)]
    fresh = used & ((b == 0) | (be_ref[b] != prev))

    @pl.when(fresh)
    def _():
        wg_b[...] = wg_ref[0].astype(BF16)
        wu_b[...] = wu_ref[0].astype(BF16)
        wd_b[...] = wd_ref[0].astype(BF16)

    @pl.when(used)
    def _():
        x = x_ref[...].astype(BF16)
        hid = _silu(_dot(x, wg_b[...])) * _dot(x, wu_b[...])
        y_ref[...] = _dot(hid.astype(BF16), wd_b[...])

    @pl.when(jnp.logical_not(used))
    def _():
        y_ref[...] = jnp.zeros_like(y_ref)


def _experts(block_e, n_used, xb, w_gate, w_up, w_down):
    nb = xb.shape[0] // MOE_BLK
    blk = lambda b, be, nu: jnp.minimum(b, nu[0] - 1)
    wmap = lambda b, be, nu: (be[blk(b, be, nu)], 0, 0)
    grid_spec = pltpu.PrefetchScalarGridSpec(
        num_scalar_prefetch=2,
        grid=(nb,),
        in_specs=[pl.BlockSpec((MOE_BLK, D_MODEL), lambda b, be, nu: (blk(b, be, nu), 0)),
                  pl.BlockSpec((1, D_MODEL, EXPERT_FF), wmap),
                  pl.BlockSpec((1, D_MODEL, EXPERT_FF), wmap),
                  pl.BlockSpec((1, EXPERT_FF, D_MODEL), wmap)],
        out_specs=pl.BlockSpec((MOE_BLK, D_MODEL), lambda b, be, nu: (b, 0)),
        scratch_shapes=[pltpu.VMEM((D_MODEL, EXPERT_FF), BF16),
                        pltpu.VMEM((D_MODEL, EXPERT_FF), BF16),
                        pltpu.VMEM((EXPERT_FF, D_MODEL), BF16)],
    )
    return pl.pallas_call(
        _expert_kernel,
        out_shape=jax.ShapeDtypeStruct(xb.shape, F32),
        grid_spec=grid_spec,
        compiler_params=_cparams(("arbitrary",)),
        name="moe_experts",
    )(block_e, n_used, xb, w_gate, w_up, w_down)


def _combine_kernel(dest_ref, yb_ref, h2_ref, gate_ref, g_ref, b_ref, o_ref, buf0, buf1, sems):
    i = pl.program_id(0)
    n = pl.num_programs(0)
    tm = h2_ref.shape[0]
    bufs = (buf0, buf1)

    def row_copy(k, slot, dst_row, src_row):
        return pltpu.make_async_copy(
            yb_ref.at[pl.ds(src_row, 1), :], bufs[k].at[slot, pl.ds(dst_row, 1), :], sems.at[slot])

    def issue_tile(tile, slot):
        def body(r, carry):
            a = (tile * tm + r) * 2
            row_copy(0, slot, r, dest_ref[a]).start()
            row_copy(1, slot, r, dest_ref[a + 1]).start()
            return carry
        lax.fori_loop(0, tm, body, 0)

    def wait_tile(slot):
        def body(r, carry):
            row_copy(0, slot, 0, 0).wait()
            row_copy(1, slot, 0, 0).wait()
            return carry
        lax.fori_loop(0, tm, body, 0)

    slot = i % 2

    @pl.when(i == 0)
    def _():
        issue_tile(0, 0)

    @pl.when(i + 1 < n)
    def _():
        issue_tile(i + 1, 1 - slot)

    wait_tile(slot)
    gate = gate_ref[...]
    ffn = gate[:, 0:1] * buf0[slot] + gate[:, 1:2] * buf1[slot]
    o_ref[...] = _layer_norm(DEEPNORM_ALPHA * h2_ref[...] + ffn, g_ref[...], b_ref[...])


def _combine_ln(dest_flat, yb, h2, gate, g, b):
    t = h2.shape[0]
    tm = min(t, 256)
    grid_spec = pltpu.PrefetchScalarGridSpec(
        num_scalar_prefetch=1,
        grid=(t // tm,),
        in_specs=[pl.BlockSpec(memory_space=pl.ANY),
                  pl.BlockSpec((tm, D_MODEL), lambda i, d: (i, 0)),
                  pl.BlockSpec((tm, LANES), lambda i, d: (i, 0)),
                  pl.BlockSpec((1, D_MODEL), lambda i, d: (0, 0)),
                  pl.BlockSpec((1, D_MODEL), lambda i, d: (0, 0))],
        out_specs=pl.BlockSpec((tm, D_MODEL), lambda i, d: (i, 0)),
        scratch_shapes=[pltpu.VMEM((2, tm, D_MODEL), F32),
                        pltpu.VMEM((2, tm, D_MODEL), F32),
                        pltpu.SemaphoreType.DMA((2,))],
    )
    return pl.pallas_call(
        _combine_kernel,
        out_shape=jax.ShapeDtypeStruct((t, D_MODEL), F32),
        grid_spec=grid_spec,
        compiler_params=_cparams(("arbitrary",)),
        name="moe_combine_ln3",
    )(dest_flat, yb, h2, gate, g, b)


def _mixer(x2d, positions, w_in, w_gla_a2, b_gla_a, g_gla_norm):
    s = x2d.shape[0]
    xb = x2d.astype(BF16)
    w_in_b = w_in.astype(BF16)
    half = RET_DK // 2
    inv_freq = (ROPE_BASE ** (-jnp.arange(half, dtype=F32) / half)).reshape(1, half)
    cos, sin = _rope_table(positions.reshape(s, 1), inv_freq)
    h_main = _proj_in(xb, w_in_b)
    w_lr = jnp.pad(w_in_b[:, GLR_OFF:], ((0, 0), (0, LANES - GLA_LOWRANK)))
    w_a2 = jnp.pad(w_gla_a2.astype(BF16), ((0, LANES - GLA_LOWRANK), (0, 0)))
    log_a = _gla_log_decay(xb, w_lr, w_a2, b_gla_a.reshape(1, -1))
    log_gamma = jnp.log1p(-jnp.exp2(-5.0 - jnp.arange(RET_HEADS, dtype=F32)))
    ret = _retention(h_main, cos, sin, log_gamma)
    gla = _gla(h_main, log_a, g_gla_norm.reshape(1, -1))
    return ret, gla


def _moe(h2, logits, w_gate, w_up, w_down, g, b):
    t = h2.shape[0]
    dest, gate, plan = _route(logits)
    dest_flat = dest[:, :2].reshape(-1)
    nb = (t * 2 + N_EXPERTS * MOE_BLK) // MOE_BLK
    block_e = plan[0, :nb]
    n_used = plan[1, :1]
    xb = _dispatch(dest_flat, plan[2, :N_EXPERTS], plan[3, :N_EXPERTS], n_used, h2, nb * MOE_BLK)
    yb = _experts(block_e, n_used, xb, w_gate, w_up, w_down)
    return _combine_ln(dest_flat, yb, h2, gate, g, b)


def kernel(x, mem, positions, w_in, w_gla_a2, b_gla_a, g_gla_norm, w_mix_out, ln1_g, ln1_b, w_mq, w_mk, w_mv, w_mo, ln2_g, ln2_b, w_route_group, b_route_group, w_route_expert, b_route_expert, w_exp_gate, w_exp_up, w_exp_down, ln3_g, ln3_b):
    bsz, s, d = x.shape
    assert bsz == 1 and d == D_MODEL
    x2d = x.reshape(s, d)
    row = lambda v: v.reshape(1, -1)

    ret, gla = _mixer(x2d, positions, w_in[0], w_gla_a2[0], b_gla_a[0], g_gla_norm[0])
    h1 = _mixout_ln(ret, gla, x2d, w_mix_out[0].astype(BF16), row(ln1_g[0]), row(ln1_b[0]))

    k, v = _mem_kv(mem[0].astype(BF16), w_mk[0].astype(BF16), w_mv[0].astype(BF16))
    n_route = N_GROUPS + N_EXPERTS
    w_route = jnp.pad(jnp.concatenate([w_route_group[0], w_route_expert[0]], axis=1).astype(BF16),
                      ((0, 0), (0, LANES - n_route)))
    b_route = jnp.pad(jnp.concatenate([b_route_group[0], b_route_expert[0].reshape(-1)]),
                      (0, LANES - n_route)).reshape(1, LANES)
    h2, logits = _cross_attention(h1, w_mq[0].astype(BF16), k, v, w_mo[0].astype(BF16),
                                       row(ln2_g[0]), row(ln2_b[0]), w_route, b_route)

    out = _moe(h2, logits, w_exp_gate[0], w_exp_up[0], w_exp_down[0],
               row(ln3_g[0]), row(ln3_b[0]))
    return out.reshape(bsz, s, d)
```

```python
import functools
import math

import jax
import jax.numpy as jnp
from jax import lax
from jax.experimental import pallas as pl
from jax.experimental.pallas import tpu as pltpu

F32 = jnp.float32
BF16 = jnp.bfloat16

D_MODEL = 2048
MEM_LEN = 256
RET_HEADS = 4
RET_DK = 256
RET_DV = 256
GLA_HEADS = 4
GLA_DK = 128
GLA_DV = 256
GLA_LOWRANK = 16
GLA_TAU = 16.0
ROPE_BASE = 10000.0
MEM_HEADS = 4
MEM_HEAD_DIM = D_MODEL // MEM_HEADS
N_GROUPS = 4
EXPERTS_PER_GROUP = 8
N_EXPERTS = N_GROUPS * EXPERTS_PER_GROUP
EXPERT_FF = 512
LN_EPS = 1e-5
DEPTH = 1
DEEPNORM_ALPHA = (2 * DEPTH) ** 0.25

RQ_OFF, RK_OFF, RV_OFF, RG_OFF = 0, 1024, 2048, 3072
GQ_OFF, GK_OFF, GV_OFF, GG_OFF, GLR_OFF = 4096, 4608, 5120, 6144, 7168
IN_MAIN = 7168

LANES = 128
RET_CHUNK = 256
GLA_CHUNK = 128
GLA_LEVELS = 7
MOE_BLK = 256
VMEM_LIMIT = 56 * 1024 * 1024


def _cparams(sem):
    return pltpu.CompilerParams(dimension_semantics=sem, vmem_limit_bytes=VMEM_LIMIT)


def _layer_norm(y, g, b):
    mu = jnp.mean(y, axis=-1, keepdims=True)
    d = y - mu
    var = jnp.mean(d * d, axis=-1, keepdims=True)
    return d * lax.rsqrt(var + LN_EPS) * g + b


def _silu(x):
    return x / (1.0 + jnp.exp(-x))


def _dot(a, b):
    return jnp.dot(a, b, preferred_element_type=F32)


def _dot_nt(a, b):
    return lax.dot_general(a, b, (((1,), (1,)), ((), ())), preferred_element_type=F32)


def _dot_tn(a, b):
    return lax.dot_general(a, b, (((0,), (0,)), ((), ())), preferred_element_type=F32)


def _rope_kernel(pos_ref, invf_ref, cos_ref, sin_ref):
    ang = pos_ref[...].astype(F32) * invf_ref[...]
    cos_ref[...] = jnp.cos(ang)
    sin_ref[...] = jnp.sin(ang)


def _rope_table(pos_col, inv_freq):
    s = pos_col.shape[0]
    tm = min(s, 1024)
    half = inv_freq.shape[1]
    return pl.pallas_call(
        _rope_kernel,
        out_shape=(jax.ShapeDtypeStruct((s, half), F32), jax.ShapeDtypeStruct((s, half), F32)),
        grid=(s // tm,),
        in_specs=[pl.BlockSpec((tm, 1), lambda i: (i, 0)),
                  pl.BlockSpec((1, half), lambda i: (0, 0))],
        out_specs=(pl.BlockSpec((tm, half), lambda i: (i, 0)),
                   pl.BlockSpec((tm, half), lambda i: (i, 0))),
        compiler_params=_cparams(("arbitrary",)),
        name="rope_table",
    )(pos_col, inv_freq)


def _mm_kernel(x_ref, w_ref, o_ref):
    o_ref[...] = _dot(x_ref[...], w_ref[...]).astype(o_ref.dtype)


def _proj_in(xb, w_in_b):
    s = xb.shape[0]
    tm = min(s, 1024)
    tn = 512
    return pl.pallas_call(
        _mm_kernel,
        out_shape=jax.ShapeDtypeStruct((s, IN_MAIN), F32),
        grid=(s // tm, IN_MAIN // tn),
        in_specs=[pl.BlockSpec((tm, D_MODEL), lambda i, j: (i, 0)),
                  pl.BlockSpec((D_MODEL, tn), lambda i, j: (0, j))],
        out_specs=pl.BlockSpec((tm, tn), lambda i, j: (i, j)),
        compiler_params=_cparams(("arbitrary", "arbitrary")),
        name="proj_in",
    )(xb, w_in_b)


def _loga_kernel(x_ref, wlr_ref, wa2_ref, ba_ref, o_ref):
    glr = _dot(x_ref[...], wlr_ref[...])
    z = _dot(glr.astype(BF16), wa2_ref[...]) + ba_ref[...]
    log_sig = jnp.minimum(z, 0.0) - jnp.log1p(jnp.exp(-jnp.abs(z)))
    o_ref[...] = log_sig / GLA_TAU


def _gla_log_decay(xb, w_lr, w_a2, b_a):
    s = xb.shape[0]
    tm = min(s, 1024)
    n = GLA_HEADS * GLA_DK
    return pl.pallas_call(
        _loga_kernel,
        out_shape=jax.ShapeDtypeStruct((s, n), F32),
        grid=(s // tm,),
        in_specs=[pl.BlockSpec((tm, D_MODEL), lambda i: (i, 0)),
                  pl.BlockSpec((D_MODEL, LANES), lambda i: (0, 0)),
                  pl.BlockSpec((LANES, n), lambda i: (0, 0)),
                  pl.BlockSpec((1, n), lambda i: (0, 0))],
        out_specs=pl.BlockSpec((tm, n), lambda i: (i, 0)),
        compiler_params=_cparams(("arbitrary",)),
        name="gla_log_decay",
    )(xb, w_lr, w_a2, b_a)


def _rotary(t, cos, sin):
    half = t.shape[-1] // 2
    t1, t2 = t[:, :half], t[:, half:]
    return jnp.concatenate([t1 * cos - t2 * sin, t1 * sin + t2 * cos], axis=-1)


def _retention_kernel(lg_ref, q_ref, k_ref, v_ref, g_ref, cos_ref, sin_ref, o_ref, state_ref):
    c = pl.program_id(0)
    C = q_ref.shape[0]

    @pl.when(c == 0)
    def _():
        state_ref[...] = jnp.zeros_like(state_ref)

    cos = cos_ref[...]
    sin = sin_ref[...]
    ri = lax.broadcasted_iota(jnp.int32, (C, C), 0)
    ci = lax.broadcasted_iota(jnp.int32, (C, C), 1)
    rel = jnp.maximum(ri - ci, 0).astype(F32)
    n = lax.broadcasted_iota(jnp.int32, (C, 1), 0).astype(F32)

    for h in range(RET_HEADS):
        lg = lg_ref[h]
        sl = slice(h * RET_DK, (h + 1) * RET_DK)
        q = _rotary(q_ref[:, sl], cos, sin)
        k = _rotary(k_ref[:, sl], cos, sin) * (RET_DK ** -0.5)
        v = v_ref[:, sl].astype(BF16)
        decay_intra = jnp.where(ri >= ci, jnp.exp(lg * rel), 0.0)
        decay_q = jnp.exp(lg * (n + 1.0))
        decay_k = jnp.exp(lg * (C - 1.0 - n))
        decay_chunk = jnp.exp(lg * C)

        state = state_ref[h]
        scores = _dot_nt(q.astype(BF16), k.astype(BF16)) * decay_intra
        o = _dot(scores.astype(BF16), v) + _dot((q * decay_q).astype(BF16), state.astype(BF16))
        state_ref[h] = decay_chunk * state + _dot_tn((k * decay_k).astype(BF16), v)

        mu = jnp.mean(o, axis=-1, keepdims=True)
        d = o - mu
        var = jnp.mean(d * d, axis=-1, keepdims=True)
        o = d * lax.rsqrt(var + LN_EPS)
        o_ref[:, sl] = (_silu(g_ref[:, sl]) * o).astype(o_ref.dtype)


def _retention(h_main, cos, sin, log_gamma):
    s = h_main.shape[0]
    C = min(RET_CHUNK, s)
    w = RET_HEADS * RET_DK
    col = lambda off: (lambda c, lg: (c, off // w))
    grid_spec = pltpu.PrefetchScalarGridSpec(
        num_scalar_prefetch=1,
        grid=(s // C,),
        in_specs=[pl.BlockSpec((C, w), col(RQ_OFF)),
                  pl.BlockSpec((C, w), col(RK_OFF)),
                  pl.BlockSpec((C, w), col(RV_OFF)),
                  pl.BlockSpec((C, w), col(RG_OFF)),
                  pl.BlockSpec((C, RET_DK // 2), lambda c, lg: (c, 0)),
                  pl.BlockSpec((C, RET_DK // 2), lambda c, lg: (c, 0))],
        out_specs=pl.BlockSpec((C, RET_HEADS * RET_DV), lambda c, lg: (c, 0)),
        scratch_shapes=[pltpu.VMEM((RET_HEADS, RET_DK, RET_DV), F32)],
    )
    return pl.pallas_call(
        _retention_kernel,
        out_shape=jax.ShapeDtypeStruct((s, RET_HEADS * RET_DV), BF16),
        grid_spec=grid_spec,
        compiler_params=_cparams(("arbitrary",)),
        name="retention",
    )(log_gamma, h_main, h_main, h_main, h_main, cos, sin)


def _gla_decay_matrix(C):
    import numpy as np
    levels = int(math.log2(C))
    r = np.arange(C)[:, None]
    t = np.arange(C)[None, :]
    mats = []
    for l in range(levels):
        blk = C >> l
        half = blk // 2
        m = (r // blk) * blk + half - 1
        qside = (r % blk) >= half
        mats.append(np.where(qside, (t > m) & (t <= r), (t > r) & (t <= m)))
    mats.append(t <= r)
    mats.append(t > r)
    return np.concatenate(mats, axis=0).astype(np.float32)


def _gla_kernel(m_ref, q_ref, k_ref, v_ref, g_ref, la_ref, gn_ref, o_ref, state_ref):
    c = pl.program_id(0)
    C = q_ref.shape[0]
    levels = GLA_LEVELS

    @pl.when(c == 0)
    def _():
        state_ref[...] = jnp.zeros_like(state_ref)

    m = m_ref[...]
    ri = lax.broadcasted_iota(jnp.int32, (C, C), 0)
    ci = lax.broadcasted_iota(jnp.int32, (C, C), 1)
    xor = jnp.where(ri > ci, ri ^ ci, 0)
    row = lax.broadcasted_iota(jnp.int32, (C, 1), 0)

    for h in range(GLA_HEADS):
        ks = slice(h * GLA_DK, (h + 1) * GLA_DK)
        vs = slice(h * GLA_DV, (h + 1) * GLA_DV)
        q = q_ref[:, ks] * (GLA_DK ** -0.5)
        k = k_ref[:, ks]
        v = v_ref[:, vs].astype(BF16)
        la = la_ref[:, ks]
        la_hi = la.astype(BF16)
        la_lo = (la - la_hi.astype(F32)).astype(BF16)
        expo = jnp.exp(_dot(m, la_hi) + _dot(m, la_lo))

        scores = jnp.where(ri == ci, _dot_nt(q.astype(BF16), k.astype(BF16)), 0.0)
        for l in range(levels):
            half = C >> (l + 1)
            shift = int(math.log2(half))
            qside = (row & half) != 0
            x = (jnp.where(qside, q, k) * expo[l * C:(l + 1) * C]).astype(BF16)
            scores = scores + jnp.where((xor >> shift) == 1, _dot_nt(x, x), 0.0)

        e_b = expo[levels * C:(levels + 1) * C]
        e_rev = expo[(levels + 1) * C:(levels + 2) * C]
        e_last = e_b[C - 1:C, :]

        state = state_ref[h]
        o = _dot(scores.astype(BF16), v) + _dot_nt((q * e_b).astype(BF16), state.astype(BF16))
        state_ref[h] = state * e_last + _dot_tn(v, (k * e_rev).astype(BF16))

        o = o * lax.rsqrt(jnp.mean(o * o, axis=-1, keepdims=True) + LN_EPS) * gn_ref[...]
        o_ref[:, vs] = (_silu(g_ref[:, vs]) * o).astype(o_ref.dtype)


def _gla(h_main, log_a, g_norm):
    s = h_main.shape[0]
    C = GLA_CHUNK
    m = jnp.asarray(_gla_decay_matrix(C), dtype=BF16)
    nrow = m.shape[0]
    wk = GLA_HEADS * GLA_DK
    wv = GLA_HEADS * GLA_DV
    return pl.pallas_call(
        _gla_kernel,
        out_shape=jax.ShapeDtypeStruct((s, wv), BF16),
        grid=(s // C,),
        in_specs=[pl.BlockSpec((nrow, C), lambda c: (0, 0)),
                  pl.BlockSpec((C, wk), lambda c: (c, GQ_OFF // wk)),
                  pl.BlockSpec((C, wk), lambda c: (c, GK_OFF // wk)),
                  pl.BlockSpec((C, wv), lambda c: (c, GV_OFF // wv)),
                  pl.BlockSpec((C, wv), lambda c: (c, GG_OFF // wv)),
                  pl.BlockSpec((C, wk), lambda c: (c, 0)),
                  pl.BlockSpec((1, GLA_DV), lambda c: (0, 0))],
        out_specs=pl.BlockSpec((C, wv), lambda c: (c, 0)),
        scratch_shapes=[pltpu.VMEM((GLA_HEADS, GLA_DV, GLA_DK), F32)],
        compiler_params=_cparams(("arbitrary",)),
        name="gla",
    )(m, h_main, h_main, h_main, h_main, log_a, g_norm)


def _mixout_kernel(ret_ref, gla_ref, x_ref, w_ref, g_ref, b_ref, o_ref):
    nr = ret_ref.shape[1]
    mix = _dot(ret_ref[...], w_ref[:nr, :]) + _dot(gla_ref[...], w_ref[nr:, :])
    o_ref[...] = _layer_norm(DEEPNORM_ALPHA * x_ref[...] + mix, g_ref[...], b_ref[...])


def _mixout_ln(ret, gla, x2d, w_b, g, b):
    s = x2d.shape[0]
    tm = min(s, 512)
    nr, ng = ret.shape[1], gla.shape[1]
    return pl.pallas_call(
        _mixout_kernel,
        out_shape=jax.ShapeDtypeStruct((s, D_MODEL), F32),
        grid=(s // tm,),
        in_specs=[pl.BlockSpec((tm, nr), lambda i: (i, 0)),
                  pl.BlockSpec((tm, ng), lambda i: (i, 0)),
                  pl.BlockSpec((tm, D_MODEL), lambda i: (i, 0)),
                  pl.BlockSpec((nr + ng, D_MODEL), lambda i: (0, 0)),
                  pl.BlockSpec((1, D_MODEL), lambda i: (0, 0)),
                  pl.BlockSpec((1, D_MODEL), lambda i: (0, 0))],
        out_specs=pl.BlockSpec((tm, D_MODEL), lambda i: (i, 0)),
        compiler_params=_cparams(("arbitrary",)),
        name="mixout_ln1",
    )(ret, gla, x2d, w_b, g, b)


def _kv_kernel(mem_ref, wk_ref, wv_ref, k_ref, v_ref):
    m = mem_ref[...]
    k_ref[...] = _dot(m, wk_ref[...]).astype(k_ref.dtype)
    v_ref[...] = _dot(m, wv_ref[...]).astype(v_ref.dtype)


def _mem_kv(mem_b, wk_b, wv_b):
    tn = 512
    return pl.pallas_call(
        _kv_kernel,
        out_shape=(jax.ShapeDtypeStruct((MEM_LEN, D_MODEL), BF16),
                   jax.ShapeDtypeStruct((MEM_LEN, D_MODEL), BF16)),
        grid=(D_MODEL // tn,),
        in_specs=[pl.BlockSpec((MEM_LEN, D_MODEL), lambda j: (0, 0)),
                  pl.BlockSpec((D_MODEL, tn), lambda j: (0, j)),
                  pl.BlockSpec((D_MODEL, tn), lambda j: (0, j))],
        out_specs=(pl.BlockSpec((MEM_LEN, tn), lambda j: (0, j)),
                   pl.BlockSpec((MEM_LEN, tn), lambda j: (0, j))),
        compiler_params=_cparams(("arbitrary",)),
        name="mem_kv",
    )(mem_b, wk_b, wv_b)


def _cross_kernel(h_ref, wq_ref, k_ref, v_ref, wo_ref, g_ref, b_ref, wr_ref, br_ref,
                  h2_ref, lg_ref):
    h1 = h_ref[...]
    q = _dot(h1.astype(BF16), wq_ref[...]).astype(BF16)
    outs = []
    for hd in range(MEM_HEADS):
        sl = slice(hd * MEM_HEAD_DIM, (hd + 1) * MEM_HEAD_DIM)
        s = _dot_nt(q[:, sl], k_ref[:, sl]) * (MEM_HEAD_DIM ** -0.5)
        s = s - jnp.max(s, axis=-1, keepdims=True)
        p = jnp.exp(s)
        p = p / jnp.sum(p, axis=-1, keepdims=True)
        outs.append(_dot(p.astype(BF16), v_ref[:, sl]))
    o = jnp.concatenate(outs, axis=-1).astype(BF16)
    cross = _dot(o, wo_ref[...])
    h2 = _layer_norm(DEEPNORM_ALPHA * h1 + cross, g_ref[...], b_ref[...])
    h2_ref[...] = h2
    lg_ref[...] = _dot(h2.astype(BF16), wr_ref[...]) + br_ref[...]


def _cross_attention(h1, wq_b, k, v, wo_b, g, b, w_route, b_route):
    s = h1.shape[0]
    tm = min(s, 256)
    const = lambda shape: pl.BlockSpec(shape, lambda i: (0, 0), pipeline_mode=pl.Buffered(1))
    return pl.pallas_call(
        _cross_kernel,
        out_shape=(jax.ShapeDtypeStruct((s, D_MODEL), F32),
                   jax.ShapeDtypeStruct((s, LANES), F32)),
        grid=(s // tm,),
        in_specs=[pl.BlockSpec((tm, D_MODEL), lambda i: (i, 0)),
                  const((D_MODEL, D_MODEL)),
                  const((MEM_LEN, D_MODEL)),
                  const((MEM_LEN, D_MODEL)),
                  const((D_MODEL, D_MODEL)),
                  const((1, D_MODEL)),
                  const((1, D_MODEL)),
                  const((D_MODEL, LANES)),
                  const((1, LANES))],
        out_specs=(pl.BlockSpec((tm, D_MODEL), lambda i: (i, 0)),
                   pl.BlockSpec((tm, LANES), lambda i: (i, 0))),
        compiler_params=_cparams(("arbitrary",)),
        name="cross_attn_ln2",
    )(h1, wq_b, k, v, wo_b, g, b, w_route, b_route)


def _route_kernel(lg_ref, dest_ref, gate_ref, be_ref, cnt_row, cnt_col, pstart, run):
    phase = pl.program_id(0)
    i = pl.program_id(1)
    tm = lg_ref.shape[0]
    neg = -jnp.inf

    logits = lg_ref[...]
    lane = lax.broadcasted_iota(jnp.int32, (tm, LANES), 1)
    gmask = lane < N_GROUPS
    gl = jnp.where(gmask, logits, neg)
    ge = jnp.exp(gl - jnp.max(gl, axis=-1, keepdims=True))
    pg = ge / jnp.sum(ge, axis=-1, keepdims=True)
    pg_sel = jnp.max(pg, axis=-1, keepdims=True)
    grp = jnp.min(jnp.where((pg == pg_sel) & gmask, lane, LANES), axis=-1, keepdims=True)

    fl_lane = lane - N_GROUPS
    fmask = (fl_lane >= 0) & (fl_lane < N_EXPERTS) & ((fl_lane >> 3) == grp)
    fl = jnp.where(fmask, logits, neg)
    fe = jnp.exp(fl - jnp.max(fl, axis=-1, keepdims=True))
    fp = fe / jnp.sum(fe, axis=-1, keepdims=True)
    p1 = jnp.max(fp, axis=-1, keepdims=True)
    i1 = jnp.min(jnp.where((fp == p1) & fmask, lane, LANES), axis=-1, keepdims=True)
    rest = fmask & (lane != i1)
    fp2 = jnp.where(rest, fp, -1.0)
    p2 = jnp.max(fp2, axis=-1, keepdims=True)
    i2 = jnp.min(jnp.where((fp2 == p2) & rest, lane, LANES), axis=-1, keepdims=True)
    psum = p1 + p2
    gate1 = pg_sel * p1 / psum
    gate2 = pg_sel * p2 / psum

    oh1 = lane == (i1 - N_GROUPS)
    oh2 = lane == (i2 - N_GROUPS)
    oh = (jnp.where(oh1, 1.0, 0.0) + jnp.where(oh2, 1.0, 0.0)).astype(BF16)
    ones = jnp.ones((tm, LANES), BF16)

    @pl.when((phase == 0) & (i == 0))
    def _():
        cnt_row[...] = jnp.zeros_like(cnt_row)
        cnt_col[...] = jnp.zeros_like(cnt_col)

    @pl.when(phase == 0)
    def _():
        cnt_row[...] += _dot_tn(ones, oh)
        cnt_col[...] += _dot_tn(oh, ones)

    @pl.when((phase == 1) & (i == 0))
    def _():
        nblk_row = jnp.floor((cnt_row[...] + (MOE_BLK - 1.0)) * (1.0 / MOE_BLK))
        nblk_col = jnp.floor((cnt_col[...] + (MOE_BLK - 1.0)) * (1.0 / MOE_BLK))
        r = lax.broadcasted_iota(jnp.int32, (LANES, LANES), 0)
        c = lax.broadcasted_iota(jnp.int32, (LANES, LANES), 1)
        strict_upper = jnp.where(r < c, 1.0, 0.0).astype(BF16)
        lower_incl = jnp.where(c <= r, 1.0, 0.0).astype(BF16)
        pstart[...] = _dot(nblk_row[0:8, :].astype(BF16), strict_upper) * float(MOE_BLK)
        ends = _dot(lower_incl, nblk_col.astype(BF16))
        expert_rows = r < N_EXPERTS
        be = jnp.sum(jnp.where(expert_rows & (ends <= c.astype(F32)), 1.0, 0.0), axis=0, keepdims=True)
        be = jnp.minimum(be, N_EXPERTS - 1.0)
        total = jnp.sum(jnp.where(r < N_EXPERTS, nblk_col, 0.0), axis=0, keepdims=True)
        sub = lax.broadcasted_iota(jnp.int32, be_ref.shape, 0)
        plan = jnp.where(sub == 0, jnp.broadcast_to(be, be_ref.shape),
                         jnp.where(sub == 1, jnp.broadcast_to(total, be_ref.shape),
                                   jnp.where(sub == 2, pstart[...], cnt_row[0:8, :])))
        be_ref[...] = plan.astype(jnp.int32)
        run[...] = jnp.zeros_like(run)

    @pl.when(phase == 1)
    def _():
        tr = lax.broadcasted_iota(jnp.int32, (tm, tm), 0)
        tc = lax.broadcasted_iota(jnp.int32, (tm, tm), 1)
        strict_lower = jnp.where(tc < tr, 1.0, 0.0).astype(BF16)
        before = _dot(strict_lower, oh) + run[0:1, :] + pstart[0:1, :]
        d1 = jnp.sum(jnp.where(oh1, before, 0.0), axis=-1, keepdims=True)
        d2 = jnp.sum(jnp.where(oh2, before, 0.0), axis=-1, keepdims=True)
        dest_ref[...] = jnp.where(lane == 0, d1, jnp.where(lane == 1, d2, 0.0)).astype(jnp.int32)
        gate_ref[...] = jnp.where(lane == 0, gate1, jnp.where(lane == 1, gate2, 0.0))
        run[...] += _dot_tn(ones, oh)[0:8, :]


def _route(logits):
    t = logits.shape[0]
    tm = min(t, 512)
    nt = t // tm
    return pl.pallas_call(
        _route_kernel,
        out_shape=(jax.ShapeDtypeStruct((t, LANES), jnp.int32),
                   jax.ShapeDtypeStruct((t, LANES), F32),
                   jax.ShapeDtypeStruct((8, LANES), jnp.int32)),
        grid=(2, nt),
        in_specs=[pl.BlockSpec((tm, LANES), lambda p, i: (i, 0))],
        out_specs=(pl.BlockSpec((tm, LANES), lambda p, i: (i * p, 0)),
                   pl.BlockSpec((tm, LANES), lambda p, i: (i * p, 0)),
                   pl.BlockSpec((8, LANES), lambda p, i: (0, 0))),
        scratch_shapes=[pltpu.VMEM((LANES, LANES), F32),
                        pltpu.VMEM((LANES, LANES), F32),
                        pltpu.VMEM((8, LANES), F32),
                        pltpu.VMEM((8, LANES), F32)],
        compiler_params=_cparams(("arbitrary", "arbitrary")),
        name="moe_route",
    )(logits)


SLOT_UNROLL = 8


def _expert_kernel(be_ref, nu_ref, dest_ref, pstart_ref, cnt_ref, h2_ref, wg_ref, wu_ref, wd_ref, y_ref,
                   inv_ref, xbuf, sems, wg_b, wu_b, wd_b):
    b = pl.program_id(0)
    n_used = nu_ref[0]
    used = b < n_used
    n_assign = dest_ref.shape[0]

    def valid_rows(blk):
        e = be_ref[blk]
        return jnp.clip(cnt_ref[e] - (blk * MOE_BLK - pstart_ref[e]), 0, MOE_BLK)

    def row_copy(slot, r, tok):
        return pltpu.make_async_copy(
            h2_ref.at[pl.ds(tok, 1), :], xbuf.at[slot, pl.ds(r, 1), :], sems.at[slot])

    def issue_block(blk, slot):
        def body(r, carry):
            row_copy(slot, r, inv_ref[blk * MOE_BLK + r]).start()
            return carry
        lax.fori_loop(0, valid_rows(blk), body, 0)

    def wait_block(blk, slot):
        def body(r, carry):
            row_copy(slot, 0, 0).wait()
            return carry
        lax.fori_loop(0, valid_rows(blk), body, 0)

    @pl.when(b == 0)
    def _():
        def invert(j, carry):
            for u in range(SLOT_UNROLL):
                a = j * SLOT_UNROLL + u
                inv_ref[dest_ref[a]] = a >> 1
            return carry
        lax.fori_loop(0, n_assign // SLOT_UNROLL, invert, 0)
        xbuf[...] = jnp.zeros_like(xbuf)
        issue_block(0, 0)

    slot = b % 2

    @pl.when(b + 1 < n_used)
    def _():
        issue_block(b + 1, 1 - slot)

    prev = be_ref[jnp.maximum(b - 1, 0)]
    fresh = used & ((b == 0) | (be_ref[b] != prev))

    @pl.when(fresh)
    def _():
        wg_b[...] = wg_ref[0].astype(BF16)
        wu_b[...] = wu_ref[0].astype(BF16)
        wd_b[...] = wd_ref[0].astype(BF16)

    @pl.when(used)
    def _():
        wait_block(b, slot)
        x = xbuf[slot].astype(BF16)
        hid = _silu(_dot(x, wg_b[...])) * _dot(x, wu_b[...])
        y_ref[...] = _dot(hid.astype(BF16), wd_b[...])

    @pl.when(jnp.logical_not(used))
    def _():
        y_ref[...] = jnp.zeros_like(y_ref)


def _experts(block_e, n_used, dest_flat, pstart, cnt, h2, w_gate, w_up, w_down):
    nb = block_e.shape[0]
    blk = lambda b, be, nu: jnp.minimum(b, nu[0] - 1)
    wmap = lambda b, be, nu, d, p, c: (be[blk(b, be, nu)], 0, 0)
    grid_spec = pltpu.PrefetchScalarGridSpec(
        num_scalar_prefetch=5,
        grid=(nb,),
        in_specs=[pl.BlockSpec(memory_space=pl.ANY),
                  pl.BlockSpec((1, D_MODEL, EXPERT_FF), wmap),
                  pl.BlockSpec((1, D_MODEL, EXPERT_FF), wmap),
                  pl.BlockSpec((1, EXPERT_FF, D_MODEL), wmap)],
        out_specs=pl.BlockSpec((MOE_BLK, D_MODEL), lambda b, be, nu, d, p, c: (b, 0)),
        scratch_shapes=[pltpu.SMEM((nb * MOE_BLK,), jnp.int32),
                        pltpu.VMEM((2, MOE_BLK, D_MODEL), F32),
                        pltpu.SemaphoreType.DMA((2,)),
                        pltpu.VMEM((D_MODEL, EXPERT_FF), BF16),
                        pltpu.VMEM((D_MODEL, EXPERT_FF), BF16),
                        pltpu.VMEM((EXPERT_FF, D_MODEL), BF16)],
    )
    return pl.pallas_call(
        _expert_kernel,
        out_shape=jax.ShapeDtypeStruct((nb * MOE_BLK, D_MODEL), F32),
        grid_spec=grid_spec,
        compiler_params=_cparams(("arbitrary",)),
        name="moe_experts",
    )(block_e, n_used, dest_flat, pstart, cnt, h2, w_gate, w_up, w_down)


def _combine_kernel(dest_ref, yb_ref, h2_ref, gate_ref, g_ref, b_ref, o_ref, buf0, buf1, sems):
    i = pl.program_id(0)
    n = pl.num_programs(0)
    tm = h2_ref.shape[0]
    bufs = (buf0, buf1)

    def row_copy(k, slot, dst_row, src_row):
        return pltpu.make_async_copy(
            yb_ref.at[pl.ds(src_row, 1), :], bufs[k].at[slot, pl.ds(dst_row, 1), :], sems.at[slot])

    def issue_tile(tile, slot):
        def body(r, carry):
            a = (tile * tm + r) * 2
            row_copy(0, slot, r, dest_ref[a]).start()
            row_copy(1, slot, r, dest_ref[a + 1]).start(priority=1)
            return carry
        lax.fori_loop(0, tm, body, 0)

    def wait_tile(slot):
        def body(r, carry):
            row_copy(0, slot, 0, 0).wait()
            row_copy(1, slot, 0, 0).wait()
            return carry
        lax.fori_loop(0, tm, body, 0)

    slot = i % 2

    @pl.when(i == 0)
    def _():
        issue_tile(0, 0)

    @pl.when(i + 1 < n)
    def _():
        issue_tile(i + 1, 1 - slot)

    wait_tile(slot)
    gate = gate_ref[...]
    ffn = gate[:, 0:1] * buf0[slot] + gate[:, 1:2] * buf1[slot]
    o_ref[...] = _layer_norm(DEEPNORM_ALPHA * h2_ref[...] + ffn, g_ref[...], b_ref[...])


def _combine_ln(dest_flat, yb, h2, gate, g, b):
    t = h2.shape[0]
    tm = min(t, 256)
    grid_spec = pltpu.PrefetchScalarGridSpec(
        num_scalar_prefetch=1,
        grid=(t // tm,),
        in_specs=[pl.BlockSpec(memory_space=pl.ANY),
                  pl.BlockSpec((tm, D_MODEL), lambda i, d: (i, 0)),
                  pl.BlockSpec((tm, LANES), lambda i, d: (i, 0)),
                  pl.BlockSpec((1, D_MODEL), lambda i, d: (0, 0)),
                  pl.BlockSpec((1, D_MODEL), lambda i, d: (0, 0))],
        out_specs=pl.BlockSpec((tm, D_MODEL), lambda i, d: (i, 0)),
        scratch_shapes=[pltpu.VMEM((2, tm, D_MODEL), F32),
                        pltpu.VMEM((2, tm, D_MODEL), F32),
                        pltpu.SemaphoreType.DMA((2,))],
    )
    return pl.pallas_call(
        _combine_kernel,
        out_shape=jax.ShapeDtypeStruct((t, D_MODEL), F32),
        grid_spec=grid_spec,
        compiler_params=_cparams(("arbitrary",)),
        name="moe_combine_ln3",
    )(dest_flat, yb, h2, gate, g, b)


def _mixer(x2d, positions, w_in, w_gla_a2, b_gla_a, g_gla_norm):
    s = x2d.shape[0]
    xb = x2d.astype(BF16)
    w_in_b = w_in.astype(BF16)
    half = RET_DK // 2
    inv_freq = (ROPE_BASE ** (-jnp.arange(half, dtype=F32) / half)).reshape(1, half)
    cos, sin = _rope_table(positions.reshape(s, 1), inv_freq)
    h_main = _proj_in(xb, w_in_b)
    w_lr = jnp.pad(w_in_b[:, GLR_OFF:], ((0, 0), (0, LANES - GLA_LOWRANK)))
    w_a2 = jnp.pad(w_gla_a2.astype(BF16), ((0, LANES - GLA_LOWRANK), (0, 0)))
    log_a = _gla_log_decay(xb, w_lr, w_a2, b_gla_a.reshape(1, -1))
    log_gamma = jnp.log1p(-jnp.exp2(-5.0 - jnp.arange(RET_HEADS, dtype=F32)))
    ret = _retention(h_main, cos, sin, log_gamma)
    gla = _gla(h_main, log_a, g_gla_norm.reshape(1, -1))
    return ret, gla


def _moe(h2, logits, w_gate, w_up, w_down, g, b):
    t = h2.shape[0]
    dest, gate, plan = _route(logits)
    dest_flat = dest[:, :2].reshape(-1)
    nb = (t * 2 + N_EXPERTS * MOE_BLK) // MOE_BLK
    yb = _experts(plan[0, :nb], plan[1, :1], dest_flat, plan[2, :N_EXPERTS], plan[3, :N_EXPERTS],
                  h2, w_gate, w_up, w_down)
    return _combine_ln(dest_flat, yb, h2, gate, g, b)


def kernel(x, mem, positions, w_in, w_gla_a2, b_gla_a, g_gla_norm, w_mix_out, ln1_g, ln1_b, w_mq, w_mk, w_mv, w_mo, ln2_g, ln2_b, w_route_group, b_route_group, w_route_expert, b_route_expert, w_exp_gate, w_exp_up, w_exp_down, ln3_g, ln3_b):
    bsz, s, d = x.shape
    assert bsz == 1 and d == D_MODEL
    x2d = x.reshape(s, d)
    row = lambda v: v.reshape(1, -1)

    ret, gla = _mixer(x2d, positions, w_in[0], w_gla_a2[0], b_gla_a[0], g_gla_norm[0])
    h1 = _mixout_ln(ret, gla, x2d, w_mix_out[0].astype(BF16), row(ln1_g[0]), row(ln1_b[0]))

    k, v = _mem_kv(mem[0].astype(BF16), w_mk[0].astype(BF16), w_mv[0].astype(BF16))
    n_route = N_GROUPS + N_EXPERTS
    w_route = jnp.pad(jnp.concatenate([w_route_group[0], w_route_expert[0]], axis=1).astype(BF16),
                      ((0, 0), (0, LANES - n_route)))
    b_route = jnp.pad(jnp.concatenate([b_route_group[0], b_route_expert[0].reshape(-1)]),
                      (0, LANES - n_route)).reshape(1, LANES)
    h2, logits = _cross_attention(h1, w_mq[0].astype(BF16), k, v, w_mo[0].astype(BF16),
                                       row(ln2_g[0]), row(ln2_b[0]), w_route, b_route)

    out = _moe(h2, logits, w_exp_gate[0], w_exp_up[0], w_exp_down[0],
               row(ln3_g[0]), row(ln3_b[0]))
    return out.reshape(bsz, s, d)
```

```python
import math

import jax
import jax.numpy as jnp
from jax import lax
from jax.experimental import pallas as pl
from jax.experimental.pallas import tpu as pltpu

F32 = jnp.float32
BF16 = jnp.bfloat16

D_MODEL = 2048
MEM_LEN = 256
RET_HEADS = 4
RET_DK = 256
RET_DV = 256
GLA_HEADS = 4
GLA_DK = 128
GLA_DV = 256
GLA_LOWRANK = 16
GLA_TAU = 16.0
ROPE_BASE = 10000.0
MEM_HEADS = 4
MEM_HEAD_DIM = D_MODEL // MEM_HEADS
N_GROUPS = 4
EXPERTS_PER_GROUP = 8
N_EXPERTS = N_GROUPS * EXPERTS_PER_GROUP
EXPERT_FF = 512
LN_EPS = 1e-5
DEPTH = 1
DEEPNORM_ALPHA = (2 * DEPTH) ** 0.25

RQ_OFF, RK_OFF, RV_OFF, RG_OFF = 0, 1024, 2048, 3072
GQ_OFF, GK_OFF, GV_OFF, GG_OFF, GLR_OFF = 4096, 4608, 5120, 6144, 7168
IN_MAIN = 7168

LANES = 128
RET_CHUNK = 256
GLA_CHUNK = 128
GLA_LEVELS = 7
MOE_BLK = 256
TILE_TOK = 512
RUN_ALIGN = 16
TILE_SLOTS = 2 * TILE_TOK + N_EXPERTS * RUN_ALIGN
SORT_ROWS = 256
VMEM_LIMIT = 56 * 1024 * 1024


def _cparams(sem):
    return pltpu.CompilerParams(dimension_semantics=sem, vmem_limit_bytes=VMEM_LIMIT)


def _layer_norm(y, g, b):
    mu = jnp.mean(y, axis=-1, keepdims=True)
    d = y - mu
    var = jnp.mean(d * d, axis=-1, keepdims=True)
    return d * lax.rsqrt(var + LN_EPS) * g + b


def _silu(x):
    return x / (1.0 + jnp.exp(-x))


def _dot(a, b):
    return jnp.dot(a, b, preferred_element_type=F32)


def _dot_nt(a, b):
    return lax.dot_general(a, b, (((1,), (1,)), ((), ())), preferred_element_type=F32)


def _dot_tn(a, b):
    return lax.dot_general(a, b, (((0,), (0,)), ((), ())), preferred_element_type=F32)


def _rope_kernel(pos_ref, invf_ref, cos_ref, sin_ref):
    ang = pos_ref[...].astype(F32) * invf_ref[...]
    cos_ref[...] = jnp.cos(ang)
    sin_ref[...] = jnp.sin(ang)


def _rope_table(pos_col, inv_freq):
    s = pos_col.shape[0]
    tm = min(s, 1024)
    half = inv_freq.shape[1]
    return pl.pallas_call(
        _rope_kernel,
        out_shape=(jax.ShapeDtypeStruct((s, half), F32), jax.ShapeDtypeStruct((s, half), F32)),
        grid=(s // tm,),
        in_specs=[pl.BlockSpec((tm, 1), lambda i: (i, 0)),
                  pl.BlockSpec((1, half), lambda i: (0, 0))],
        out_specs=(pl.BlockSpec((tm, half), lambda i: (i, 0)),
                   pl.BlockSpec((tm, half), lambda i: (i, 0))),
        compiler_params=_cparams(("arbitrary",)),
        name="rope_table",
    )(pos_col, inv_freq)


def _mm_kernel(x_ref, w_ref, o_ref):
    o_ref[...] = _dot(x_ref[...], w_ref[...]).astype(o_ref.dtype)


def _proj_in(xb, w_in_b):
    s = xb.shape[0]
    tm = min(s, 1024)
    tn = 512
    return pl.pallas_call(
        _mm_kernel,
        out_shape=jax.ShapeDtypeStruct((s, IN_MAIN), F32),
        grid=(s // tm, IN_MAIN // tn),
        in_specs=[pl.BlockSpec((tm, D_MODEL), lambda i, j: (i, 0)),
                  pl.BlockSpec((D_MODEL, tn), lambda i, j: (0, j))],
        out_specs=pl.BlockSpec((tm, tn), lambda i, j: (i, j)),
        compiler_params=_cparams(("arbitrary", "arbitrary")),
        name="proj_in",
    )(xb, w_in_b)


def _loga_kernel(x_ref, wlr_ref, wa2_ref, ba_ref, o_ref):
    glr = _dot(x_ref[...], wlr_ref[...])
    z = _dot(glr.astype(BF16), wa2_ref[...]) + ba_ref[...]
    log_sig = jnp.minimum(z, 0.0) - jnp.log1p(jnp.exp(-jnp.abs(z)))
    o_ref[...] = log_sig / GLA_TAU


def _gla_log_decay(xb, w_lr, w_a2, b_a):
    s = xb.shape[0]
    tm = min(s, 1024)
    n = GLA_HEADS * GLA_DK
    return pl.pallas_call(
        _loga_kernel,
        out_shape=jax.ShapeDtypeStruct((s, n), F32),
        grid=(s // tm,),
        in_specs=[pl.BlockSpec((tm, D_MODEL), lambda i: (i, 0)),
                  pl.BlockSpec((D_MODEL, LANES), lambda i: (0, 0)),
                  pl.BlockSpec((LANES, n), lambda i: (0, 0)),
                  pl.BlockSpec((1, n), lambda i: (0, 0))],
        out_specs=pl.BlockSpec((tm, n), lambda i: (i, 0)),
        compiler_params=_cparams(("arbitrary",)),
        name="gla_log_decay",
    )(xb, w_lr, w_a2, b_a)


def _rotary(t, cos, sin):
    half = t.shape[-1] // 2
    t1, t2 = t[:, :half], t[:, half:]
    return jnp.concatenate([t1 * cos - t2 * sin, t1 * sin + t2 * cos], axis=-1)


def _retention_kernel(lg_ref, q_ref, k_ref, v_ref, g_ref, cos_ref, sin_ref, o_ref, state_ref):
    c = pl.program_id(0)
    C = q_ref.shape[0]

    @pl.when(c == 0)
    def _():
        state_ref[...] = jnp.zeros_like(state_ref)

    cos = cos_ref[...]
    sin = sin_ref[...]
    ri = lax.broadcasted_iota(jnp.int32, (C, C), 0)
    ci = lax.broadcasted_iota(jnp.int32, (C, C), 1)
    rel = jnp.maximum(ri - ci, 0).astype(F32)
    n = lax.broadcasted_iota(jnp.int32, (C, 1), 0).astype(F32)

    for h in range(RET_HEADS):
        lg = lg_ref[h]
        sl = slice(h * RET_DK, (h + 1) * RET_DK)
        q = _rotary(q_ref[:, sl], cos, sin)
        k = _rotary(k_ref[:, sl], cos, sin) * (RET_DK ** -0.5)
        v = v_ref[:, sl].astype(BF16)
        decay_intra = jnp.where(ri >= ci, jnp.exp(lg * rel), 0.0)
        decay_q = jnp.exp(lg * (n + 1.0))
        decay_k = jnp.exp(lg * (C - 1.0 - n))
        decay_chunk = jnp.exp(lg * C)

        state = state_ref[h]
        scores = _dot_nt(q.astype(BF16), k.astype(BF16)) * decay_intra
        o = _dot(scores.astype(BF16), v) + _dot((q * decay_q).astype(BF16), state.astype(BF16))
        state_ref[h] = decay_chunk * state + _dot_tn((k * decay_k).astype(BF16), v)

        mu = jnp.mean(o, axis=-1, keepdims=True)
        d = o - mu
        var = jnp.mean(d * d, axis=-1, keepdims=True)
        o = d * lax.rsqrt(var + LN_EPS)
        o_ref[:, sl] = (_silu(g_ref[:, sl]) * o).astype(o_ref.dtype)


def _retention(h_main, cos, sin, log_gamma):
    s = h_main.shape[0]
    C = min(RET_CHUNK, s)
    w = RET_HEADS * RET_DK
    col = lambda off: (lambda c, lg: (c, off // w))
    grid_spec = pltpu.PrefetchScalarGridSpec(
        num_scalar_prefetch=1,
        grid=(s // C,),
        in_specs=[pl.BlockSpec((C, w), col(RQ_OFF)),
                  pl.BlockSpec((C, w), col(RK_OFF)),
                  pl.BlockSpec((C, w), col(RV_OFF)),
                  pl.BlockSpec((C, w), col(RG_OFF)),
                  pl.BlockSpec((C, RET_DK // 2), lambda c, lg: (c, 0)),
                  pl.BlockSpec((C, RET_DK // 2), lambda c, lg: (c, 0))],
        out_specs=pl.BlockSpec((C, RET_HEADS * RET_DV), lambda c, lg: (c, 0)),
        scratch_shapes=[pltpu.VMEM((RET_HEADS, RET_DK, RET_DV), F32)],
    )
    return pl.pallas_call(
        _retention_kernel,
        out_shape=jax.ShapeDtypeStruct((s, RET_HEADS * RET_DV), BF16),
        grid_spec=grid_spec,
        compiler_params=_cparams(("arbitrary",)),
        name="retention",
    )(log_gamma, h_main, h_main, h_main, h_main, cos, sin)


def _gla_decay_matrix(C):
    import numpy as np
    levels = int(math.log2(C))
    r = np.arange(C)[:, None]
    t = np.arange(C)[None, :]
    mats = []
    for l in range(levels):
        blk = C >> l
        half = blk // 2
        m = (r // blk) * blk + half - 1
        qside = (r % blk) >= half
        mats.append(np.where(qside, (t > m) & (t <= r), (t > r) & (t <= m)))
    mats.append(t <= r)
    mats.append(t > r)
    return np.concatenate(mats, axis=0).astype(np.float32)


def _gla_kernel(m_ref, q_ref, k_ref, v_ref, g_ref, la_ref, gn_ref, o_ref, state_ref):
    c = pl.program_id(0)
    C = q_ref.shape[0]
    levels = GLA_LEVELS

    @pl.when(c == 0)
    def _():
        state_ref[...] = jnp.zeros_like(state_ref)

    m = m_ref[...]
    ri = lax.broadcasted_iota(jnp.int32, (C, C), 0)
    ci = lax.broadcasted_iota(jnp.int32, (C, C), 1)
    xor = jnp.where(ri > ci, ri ^ ci, 0)
    row = lax.broadcasted_iota(jnp.int32, (C, 1), 0)

    for h in range(GLA_HEADS):
        ks = slice(h * GLA_DK, (h + 1) * GLA_DK)
        vs = slice(h * GLA_DV, (h + 1) * GLA_DV)
        q = q_ref[:, ks] * (GLA_DK ** -0.5)
        k = k_ref[:, ks]
        v = v_ref[:, vs].astype(BF16)
        la = la_ref[:, ks]
        la_hi = la.astype(BF16)
        la_lo = (la - la_hi.astype(F32)).astype(BF16)
        expo = jnp.exp(_dot(m, la_hi) + _dot(m, la_lo))

        scores = jnp.where(ri == ci, _dot_nt(q.astype(BF16), k.astype(BF16)), 0.0)
        for l in range(levels):
            half = C >> (l + 1)
            shift = int(math.log2(half))
            qside = (row & half) != 0
            x = (jnp.where(qside, q, k) * expo[l * C:(l + 1) * C]).astype(BF16)
            scores = scores + jnp.where((xor >> shift) == 1, _dot_nt(x, x), 0.0)

        e_b = expo[levels * C:(levels + 1) * C]
        e_rev = expo[(levels + 1) * C:(levels + 2) * C]
        e_last = e_b[C - 1:C, :]

        state = state_ref[h]
        o = _dot(scores.astype(BF16), v) + _dot_nt((q * e_b).astype(BF16), state.astype(BF16))
        state_ref[h] = state * e_last + _dot_tn(v, (k * e_rev).astype(BF16))

        o = o * lax.rsqrt(jnp.mean(o * o, axis=-1, keepdims=True) + LN_EPS) * gn_ref[...]
        o_ref[:, vs] = (_silu(g_ref[:, vs]) * o).astype(o_ref.dtype)


def _gla(h_main, log_a, g_norm):
    s = h_main.shape[0]
    C = GLA_CHUNK
    m = jnp.asarray(_gla_decay_matrix(C), dtype=BF16)
    nrow = m.shape[0]
    wk = GLA_HEADS * GLA_DK
    wv = GLA_HEADS * GLA_DV
    return pl.pallas_call(
        _gla_kernel,
        out_shape=jax.ShapeDtypeStruct((s, wv), BF16),
        grid=(s // C,),
        in_specs=[pl.BlockSpec((nrow, C), lambda c: (0, 0)),
                  pl.BlockSpec((C, wk), lambda c: (c, GQ_OFF // wk)),
                  pl.BlockSpec((C, wk), lambda c: (c, GK_OFF // wk)),
                  pl.BlockSpec((C, wv), lambda c: (c, GV_OFF // wv)),
                  pl.BlockSpec((C, wv), lambda c: (c, GG_OFF // wv)),
                  pl.BlockSpec((C, wk), lambda c: (c, 0)),
                  pl.BlockSpec((1, GLA_DV), lambda c: (0, 0))],
        out_specs=pl.BlockSpec((C, wv), lambda c: (c, 0)),
        scratch_shapes=[pltpu.VMEM((GLA_HEADS, GLA_DV, GLA_DK), F32)],
        compiler_params=_cparams(("arbitrary",)),
        name="gla",
    )(m, h_main, h_main, h_main, h_main, log_a, g_norm)


def _mixout_kernel(ret_ref, gla_ref, x_ref, w_ref, g_ref, b_ref, o_ref):
    nr = ret_ref.shape[1]
    mix = _dot(ret_ref[...], w_ref[:nr, :]) + _dot(gla_ref[...], w_ref[nr:, :])
    o_ref[...] = _layer_norm(DEEPNORM_ALPHA * x_ref[...] + mix, g_ref[...], b_ref[...])


def _mixout_ln(ret, gla, x2d, w_b, g, b):
    s = x2d.shape[0]
    tm = min(s, 512)
    nr, ng = ret.shape[1], gla.shape[1]
    return pl.pallas_call(
        _mixout_kernel,
        out_shape=jax.ShapeDtypeStruct((s, D_MODEL), F32),
        grid=(s // tm,),
        in_specs=[pl.BlockSpec((tm, nr), lambda i: (i, 0)),
                  pl.BlockSpec((tm, ng), lambda i: (i, 0)),
                  pl.BlockSpec((tm, D_MODEL), lambda i: (i, 0)),
                  pl.BlockSpec((nr + ng, D_MODEL), lambda i: (0, 0)),
                  pl.BlockSpec((1, D_MODEL), lambda i: (0, 0)),
                  pl.BlockSpec((1, D_MODEL), lambda i: (0, 0))],
        out_specs=pl.BlockSpec((tm, D_MODEL), lambda i: (i, 0)),
        compiler_params=_cparams(("arbitrary",)),
        name="mixout_ln1",
    )(ret, gla, x2d, w_b, g, b)


def _kv_kernel(mem_ref, wk_ref, wv_ref, k_ref, v_ref):
    m = mem_ref[...]
    k_ref[...] = _dot(m, wk_ref[...]).astype(k_ref.dtype)
    v_ref[...] = _dot(m, wv_ref[...]).astype(v_ref.dtype)


def _mem_kv(mem_b, wk_b, wv_b):
    tn = 512
    return pl.pallas_call(
        _kv_kernel,
        out_shape=(jax.ShapeDtypeStruct((MEM_LEN, D_MODEL), BF16),
                   jax.ShapeDtypeStruct((MEM_LEN, D_MODEL), BF16)),
        grid=(D_MODEL // tn,),
        in_specs=[pl.BlockSpec((MEM_LEN, D_MODEL), lambda j: (0, 0)),
                  pl.BlockSpec((D_MODEL, tn), lambda j: (0, j)),
                  pl.BlockSpec((D_MODEL, tn), lambda j: (0, j))],
        out_specs=(pl.BlockSpec((MEM_LEN, tn), lambda j: (0, j)),
                   pl.BlockSpec((MEM_LEN, tn), lambda j: (0, j))),
        compiler_params=_cparams(("arbitrary",)),
        name="mem_kv",
    )(mem_b, wk_b, wv_b)


def _cross_kernel(h_ref, wq_ref, k_ref, v_ref, wo_ref, g_ref, b_ref, wr_ref, br_ref,
                  h2_ref, lg_ref):
    h1 = h_ref[...]
    q = _dot(h1.astype(BF16), wq_ref[...]).astype(BF16)
    outs = []
    for hd in range(MEM_HEADS):
        sl = slice(hd * MEM_HEAD_DIM, (hd + 1) * MEM_HEAD_DIM)
        s = _dot_nt(q[:, sl], k_ref[:, sl]) * (MEM_HEAD_DIM ** -0.5)
        s = s - jnp.max(s, axis=-1, keepdims=True)
        p = jnp.exp(s)
        p = p / jnp.sum(p, axis=-1, keepdims=True)
        outs.append(_dot(p.astype(BF16), v_ref[:, sl]))
    o = jnp.concatenate(outs, axis=-1).astype(BF16)
    cross = _dot(o, wo_ref[...])
    h2 = _layer_norm(DEEPNORM_ALPHA * h1 + cross, g_ref[...], b_ref[...])
    h2_ref[...] = h2
    lg_ref[...] = _dot(h2.astype(BF16), wr_ref[...]) + br_ref[...]


def _cross_attention(h1, wq_b, k, v, wo_b, g, b, w_route, b_route):
    s = h1.shape[0]
    tm = min(s, 256)
    const = lambda shape: pl.BlockSpec(shape, lambda i: (0, 0), pipeline_mode=pl.Buffered(1))
    return pl.pallas_call(
        _cross_kernel,
        out_shape=(jax.ShapeDtypeStruct((s, D_MODEL), F32),
                   jax.ShapeDtypeStruct((s, LANES), F32)),
        grid=(s // tm,),
        in_specs=[pl.BlockSpec((tm, D_MODEL), lambda i: (i, 0)),
                  const((D_MODEL, D_MODEL)),
                  const((MEM_LEN, D_MODEL)),
                  const((MEM_LEN, D_MODEL)),
                  const((D_MODEL, D_MODEL)),
                  const((1, D_MODEL)),
                  const((1, D_MODEL)),
                  const((D_MODEL, LANES)),
                  const((1, LANES))],
        out_specs=(pl.BlockSpec((tm, D_MODEL), lambda i: (i, 0)),
                   pl.BlockSpec((tm, LANES), lambda i: (i, 0))),
        compiler_params=_cparams(("arbitrary",)),
        name="cross_attn_ln2",
    )(h1, wq_b, k, v, wo_b, g, b, w_route, b_route)


def _route_kernel(lg_ref, slot_ref, slot_t_ref, gate_ref, runs_ref, plan_ref, tot_row, tot_col, gstart):
    phase = pl.program_id(0)
    i = pl.program_id(1)
    tm = lg_ref.shape[0]
    neg = -jnp.inf

    logits = lg_ref[...]
    lane = lax.broadcasted_iota(jnp.int32, (tm, LANES), 1)
    gmask = lane < N_GROUPS
    gl = jnp.where(gmask, logits, neg)
    ge = jnp.exp(gl - jnp.max(gl, axis=-1, keepdims=True))
    pg = ge / jnp.sum(ge, axis=-1, keepdims=True)
    pg_sel = jnp.max(pg, axis=-1, keepdims=True)
    grp = jnp.min(jnp.where((pg == pg_sel) & gmask, lane, LANES), axis=-1, keepdims=True)

    fl_lane = lane - N_GROUPS
    fmask = (fl_lane >= 0) & (fl_lane < N_EXPERTS) & ((fl_lane >> 3) == grp)
    fl = jnp.where(fmask, logits, neg)
    fe = jnp.exp(fl - jnp.max(fl, axis=-1, keepdims=True))
    fp = fe / jnp.sum(fe, axis=-1, keepdims=True)
    p1 = jnp.max(fp, axis=-1, keepdims=True)
    i1 = jnp.min(jnp.where((fp == p1) & fmask, lane, LANES), axis=-1, keepdims=True)
    rest = fmask & (lane != i1)
    fp2 = jnp.where(rest, fp, -1.0)
    p2 = jnp.max(fp2, axis=-1, keepdims=True)
    i2 = jnp.min(jnp.where((fp2 == p2) & rest, lane, LANES), axis=-1, keepdims=True)
    psum = p1 + p2
    gate1 = pg_sel * p1 / psum
    gate2 = pg_sel * p2 / psum

    oh1 = lane == (i1 - N_GROUPS)
    oh2 = lane == (i2 - N_GROUPS)
    oh = (jnp.where(oh1, 1.0, 0.0) + jnp.where(oh2, 1.0, 0.0)).astype(BF16)
    ones = jnp.ones((tm, LANES), BF16)

    def align_up(v, a):
        return jnp.floor((v + (a - 1.0)) * (1.0 / a)) * a

    run_row = align_up(_dot_tn(ones, oh)[0:8, :], RUN_ALIGN)
    rr = lax.broadcasted_iota(jnp.int32, (LANES, LANES), 0)
    cc = lax.broadcasted_iota(jnp.int32, (LANES, LANES), 1)
    strict_upper = jnp.where(rr < cc, 1.0, 0.0).astype(BF16)

    @pl.when((phase == 0) & (i == 0))
    def _():
        tot_row[...] = jnp.zeros_like(tot_row)
        tot_col[...] = jnp.zeros_like(tot_col)

    @pl.when(phase == 0)
    def _():
        tot_row[...] += run_row
        tot_col[...] += align_up(_dot_tn(oh, ones), RUN_ALIGN)

    @pl.when((phase == 1) & (i == 0))
    def _():
        nblk_row = align_up(tot_row[...], MOE_BLK) * (1.0 / MOE_BLK)
        nblk_col = align_up(tot_col[...], MOE_BLK) * (1.0 / MOE_BLK)
        lower_incl = jnp.where(cc <= rr, 1.0, 0.0).astype(BF16)
        base = _dot(nblk_row.astype(BF16), strict_upper) * float(MOE_BLK)
        ends = _dot(lower_incl, nblk_col.astype(BF16))
        expert_rows = rr < N_EXPERTS
        be = jnp.sum(jnp.where(expert_rows & (ends <= cc.astype(F32)), 1.0, 0.0), axis=0, keepdims=True)
        be = jnp.minimum(be, N_EXPERTS - 1.0)
        total = jnp.sum(jnp.where(expert_rows, nblk_col, 0.0), axis=0, keepdims=True)
        sub = lax.broadcasted_iota(jnp.int32, plan_ref.shape, 0)
        plan = jnp.where(sub == 0, jnp.broadcast_to(be, plan_ref.shape),
                         jnp.where(sub == 1, jnp.broadcast_to(total, plan_ref.shape),
                                   jnp.where(sub == 2, base + (nblk_row - 1.0) * float(MOE_BLK), nblk_row)))
        plan_ref[...] = plan.astype(jnp.int32)
        gstart[...] = base

    @pl.when(phase == 1)
    def _():
        off = _dot((run_row * (1.0 / RUN_ALIGN)).astype(BF16), strict_upper) * float(RUN_ALIGN)
        tr = lax.broadcasted_iota(jnp.int32, (tm, tm), 0)
        tc = lax.broadcasted_iota(jnp.int32, (tm, tm), 1)
        strict_lower = jnp.where(tc < tr, 1.0, 0.0).astype(BF16)
        slot = _dot(strict_lower, oh) + off[0:1, :]
        s1 = jnp.sum(jnp.where(oh1, slot, 0.0), axis=-1, keepdims=True)
        s2 = jnp.sum(jnp.where(oh2, slot, 0.0), axis=-1, keepdims=True)
        slots = jnp.where(lane == 0, s1, jnp.where(lane == 1, s2, 0.0))
        slot_ref[...] = slots.astype(jnp.int32)
        slot_t_ref[...] = jnp.transpose(slots)[0:8, :].astype(jnp.int32)
        gate_ref[...] = jnp.where(lane == 0, gate1, jnp.where(lane == 1, gate2, 0.0))
        sub = lax.broadcasted_iota(jnp.int32, runs_ref.shape, 0)
        runs = jnp.where(sub == 0, gstart[...], jnp.where(sub == 1, off, run_row))
        runs_ref[...] = runs.astype(jnp.int32)
        gstart[...] += run_row


def _route(logits):
    t = logits.shape[0]
    tm = min(t, TILE_TOK)
    nt = t // tm
    step = lambda p, i: (i * p, 0)
    return pl.pallas_call(
        _route_kernel,
        out_shape=(jax.ShapeDtypeStruct((t, LANES), jnp.int32),
                   jax.ShapeDtypeStruct((nt * 8, tm), jnp.int32),
                   jax.ShapeDtypeStruct((t, LANES), F32),
                   jax.ShapeDtypeStruct((nt * 8, LANES), jnp.int32),
                   jax.ShapeDtypeStruct((8, LANES), jnp.int32)),
        grid=(2, nt),
        in_specs=[pl.BlockSpec((tm, LANES), lambda p, i: (i, 0))],
        out_specs=(pl.BlockSpec((tm, LANES), step),
                   pl.BlockSpec((8, tm), step),
                   pl.BlockSpec((tm, LANES), step),
                   pl.BlockSpec((8, LANES), step),
                   pl.BlockSpec((8, LANES), lambda p, i: (0, 0))),
        scratch_shapes=[pltpu.VMEM((8, LANES), F32),
                        pltpu.VMEM((LANES, LANES), F32),
                        pltpu.VMEM((8, LANES), F32)],
        compiler_params=_cparams(("arbitrary", "arbitrary")),
        name="moe_route",
    )(logits)


def _run_copies(rg_ref, ro_ref, rn_ref, tile, make_copy):
    def each(action):
        def body(e, carry):
            j = tile * N_EXPERTS + e
            n = rn_ref[j]

            @pl.when(n > 0)
            def _():
                action(make_copy(pl.multiple_of(rg_ref[j], RUN_ALIGN), pl.multiple_of(ro_ref[j], RUN_ALIGN),
                                 pl.multiple_of(n, RUN_ALIGN)))
            return carry
        lax.fori_loop(0, N_EXPERTS, body, 0)
    return each


def _dispatch_kernel(rg_ref, ro_ref, rn_ref, zrow_ref, nblk_ref, nu_ref, x_ref, slot_t_ref, xb_ref,
                     sorted_ref, zero_ref, sems, zsem):
    i = pl.program_id(0)
    n = pl.num_programs(0)
    tm = x_ref.shape[0]
    nb = xb_ref.shape[0] // MOE_BLK
    buf = i % 2

    def zero_copy(row):
        return pltpu.make_async_copy(
            zero_ref, xb_ref.at[pl.ds(pl.multiple_of(row, MOE_BLK), MOE_BLK), :], zsem)

    def runs_of(tile, slot):
        def make_copy(g, o, size):
            return pltpu.make_async_copy(sorted_ref.at[slot, pl.ds(o, size), :],
                                         xb_ref.at[pl.ds(g, size), :], sems.at[slot])
        return _run_copies(rg_ref, ro_ref, rn_ref, tile, make_copy)

    @pl.when(i == 0)
    def _():
        zero_ref[...] = jnp.zeros_like(zero_ref)

        def issue_zero(e, carry):
            @pl.when(nblk_ref[e] > 0)
            def _():
                zero_copy(zrow_ref[e]).start()
            return carry
        lax.fori_loop(0, N_EXPERTS, issue_zero, 0)
        lax.fori_loop(nu_ref[0], nb, lambda b, c: (zero_copy(b * MOE_BLK).start(), c)[1], 0)

        def wait_zero(e, carry):
            @pl.when(nblk_ref[e] > 0)
            def _():
                zero_copy(0).wait()
            return carry
        lax.fori_loop(0, N_EXPERTS, wait_zero, 0)
        lax.fori_loop(nu_ref[0], nb, lambda b, c: (zero_copy(0).wait(), c)[1], 0)

    @pl.when(i >= 2)
    def _():
        runs_of(i - 2, buf)(lambda c: c.wait())

    x = x_ref[...].astype(BF16)
    s1 = slot_t_ref[0:1, :]
    s2 = slot_t_ref[1:2, :]
    for rc in range(TILE_SLOTS // SORT_ROWS):
        r = lax.broadcasted_iota(jnp.int32, (SORT_ROWS, tm), 0) + rc * SORT_ROWS
        perm = jnp.where((r == s1) | (r == s2), 1.0, 0.0).astype(BF16)
        sorted_ref[buf, rc * SORT_ROWS:(rc + 1) * SORT_ROWS, :] = _dot(perm, x).astype(BF16)

    runs_of(i, buf)(lambda c: c.start())

    @pl.when(i == n - 1)
    def _():
        @pl.when(i >= 1)
        def _():
            runs_of(i - 1, 1 - buf)(lambda c: c.wait())
        runs_of(i, buf)(lambda c: c.wait())


def _dispatch(rg, ro, rn, zrow, nblk, n_used, h2, slot_t, n_rows):
    t = h2.shape[0]
    tm = min(t, TILE_TOK)
    grid_spec = pltpu.PrefetchScalarGridSpec(
        num_scalar_prefetch=6,
        grid=(t // tm,),
        in_specs=[pl.BlockSpec((tm, D_MODEL), lambda i, *_: (i, 0)),
                  pl.BlockSpec((8, tm), lambda i, *_: (i, 0))],
        out_specs=pl.BlockSpec(memory_space=pl.ANY),
        scratch_shapes=[pltpu.VMEM((2, TILE_SLOTS, D_MODEL), BF16),
                        pltpu.VMEM((MOE_BLK, D_MODEL), BF16),
                        pltpu.SemaphoreType.DMA((2,)),
                        pltpu.SemaphoreType.DMA(())],
    )
    return pl.pallas_call(
        _dispatch_kernel,
        out_shape=jax.ShapeDtypeStruct((n_rows, D_MODEL), BF16),
        grid_spec=grid_spec,
        compiler_params=_cparams(("arbitrary",)),
        name="moe_dispatch",
    )(rg, ro, rn, zrow, nblk, n_used, h2, slot_t)


def _expert_kernel(be_ref, nu_ref, x_ref, wg_ref, wu_ref, wd_ref, y_ref, wg_b, wu_b, wd_b):
    b = pl.program_id(0)
    used = b < nu_ref[0]
    prev = be_ref[jnp.maximum(b - 1, 0)]
    fresh = used & ((b == 0) | (be_ref[b] != prev))

    @pl.when(fresh)
    def _():
        wg_b[...] = wg_ref[0].astype(BF16)
        wu_b[...] = wu_ref[0].astype(BF16)
        wd_b[...] = wd_ref[0].astype(BF16)

    @pl.when(used)
    def _():
        x = x_ref[...]
        hid = _silu(_dot(x, wg_b[...])) * _dot(x, wu_b[...])
        y_ref[...] = _dot(hid.astype(BF16), wd_b[...]).astype(y_ref.dtype)

    @pl.when(jnp.logical_not(used))
    def _():
        y_ref[...] = jnp.zeros_like(y_ref)


def _experts(block_e, n_used, xb, w_gate, w_up, w_down):
    nb = xb.shape[0] // MOE_BLK
    blk = lambda b, be, nu: jnp.minimum(b, nu[0] - 1)
    wmap = lambda b, be, nu: (be[blk(b, be, nu)], 0, 0)
    grid_spec = pltpu.PrefetchScalarGridSpec(
        num_scalar_prefetch=2,
        grid=(nb,),
        in_specs=[pl.BlockSpec((MOE_BLK, D_MODEL), lambda b, be, nu: (blk(b, be, nu), 0)),
                  pl.BlockSpec((1, D_MODEL, EXPERT_FF), wmap),
                  pl.BlockSpec((1, D_MODEL, EXPERT_FF), wmap),
                  pl.BlockSpec((1, EXPERT_FF, D_MODEL), wmap)],
        out_specs=pl.BlockSpec((MOE_BLK, D_MODEL), lambda b, be, nu: (b, 0)),
        scratch_shapes=[pltpu.VMEM((D_MODEL, EXPERT_FF), BF16),
                        pltpu.VMEM((D_MODEL, EXPERT_FF), BF16),
                        pltpu.VMEM((EXPERT_FF, D_MODEL), BF16)],
    )
    return pl.pallas_call(
        _expert_kernel,
        out_shape=jax.ShapeDtypeStruct(xb.shape, BF16),
        grid_spec=grid_spec,
        compiler_params=_cparams(("arbitrary",)),
        name="moe_experts",
    )(block_e, n_used, xb, w_gate, w_up, w_down)


def _combine_kernel(rg_ref, ro_ref, rn_ref, yb_ref, h2_ref, slot_ref, gate_ref, g_ref, b_ref, o_ref,
                    ybuf, sems):
    i = pl.program_id(0)
    n = pl.num_programs(0)
    tm = h2_ref.shape[0]
    buf = i % 2

    def runs_of(tile, slot):
        def make_copy(g, o, size):
            return pltpu.make_async_copy(yb_ref.at[pl.ds(g, size), :],
                                         ybuf.at[slot, pl.ds(o, size), :], sems.at[slot])
        return _run_copies(rg_ref, ro_ref, rn_ref, tile, make_copy)

    @pl.when(i == 0)
    def _():
        ybuf[...] = jnp.zeros_like(ybuf)
        runs_of(0, 0)(lambda c: c.start())

    @pl.when(i + 1 < n)
    def _():
        runs_of(i + 1, 1 - buf)(lambda c: c.start())

    runs_of(i, buf)(lambda c: c.wait())

    lane = lax.broadcasted_iota(jnp.int32, (tm, TILE_SLOTS), 1)
    slots = slot_ref[...]
    gate = gate_ref[...]
    y = ybuf[buf]
    sel1 = jnp.where(lane == slots[:, 0:1], 1.0, 0.0).astype(BF16)
    sel2 = jnp.where(lane == slots[:, 1:2], 1.0, 0.0).astype(BF16)
    ffn = gate[:, 0:1] * _dot(sel1, y) + gate[:, 1:2] * _dot(sel2, y)
    o_ref[...] = _layer_norm(DEEPNORM_ALPHA * h2_ref[...] + ffn, g_ref[...], b_ref[...])


def _combine_ln(rg, ro, rn, yb, h2, slots, gate, g, b):
    t = h2.shape[0]
    tm = min(t, TILE_TOK)
    grid_spec = pltpu.PrefetchScalarGridSpec(
        num_scalar_prefetch=3,
        grid=(t // tm,),
        in_specs=[pl.BlockSpec(memory_space=pl.ANY),
                  pl.BlockSpec((tm, D_MODEL), lambda i, *_: (i, 0)),
                  pl.BlockSpec((tm, LANES), lambda i, *_: (i, 0)),
                  pl.BlockSpec((tm, LANES), lambda i, *_: (i, 0)),
                  pl.BlockSpec((1, D_MODEL), lambda i, *_: (0, 0)),
                  pl.BlockSpec((1, D_MODEL), lambda i, *_: (0, 0))],
        out_specs=pl.BlockSpec((tm, D_MODEL), lambda i, *_: (i, 0)),
        scratch_shapes=[pltpu.VMEM((2, TILE_SLOTS, D_MODEL), BF16),
                        pltpu.SemaphoreType.DMA((2,))],
    )
    return pl.pallas_call(
        _combine_kernel,
        out_shape=jax.ShapeDtypeStruct((t, D_MODEL), F32),
        grid_spec=grid_spec,
        compiler_params=_cparams(("arbitrary",)),
        name="moe_combine_ln3",
    )(rg, ro, rn, yb, h2, slots, gate, g, b)


def _mixer(x2d, positions, w_in, w_gla_a2, b_gla_a, g_gla_norm):
    s = x2d.shape[0]
    xb = x2d.astype(BF16)
    w_in_b = w_in.astype(BF16)
    half = RET_DK // 2
    inv_freq = (ROPE_BASE ** (-jnp.arange(half, dtype=F32) / half)).reshape(1, half)
    cos, sin = _rope_table(positions.reshape(s, 1), inv_freq)
    h_main = _proj_in(xb, w_in_b)
    w_lr = jnp.pad(w_in_b[:, GLR_OFF:], ((0, 0), (0, LANES - GLA_LOWRANK)))
    w_a2 = jnp.pad(w_gla_a2.astype(BF16), ((0, LANES - GLA_LOWRANK), (0, 0)))
    log_a = _gla_log_decay(xb, w_lr, w_a2, b_gla_a.reshape(1, -1))
    log_gamma = jnp.log1p(-jnp.exp2(-5.0 - jnp.arange(RET_HEADS, dtype=F32)))
    ret = _retention(h_main, cos, sin, log_gamma)
    gla = _gla(h_main, log_a, g_gla_norm.reshape(1, -1))
    return ret, gla


def _moe(h2, logits, w_gate, w_up, w_down, g, b):
    t = h2.shape[0]
    nt = t // min(t, TILE_TOK)
    slots, slot_t, gate, runs, plan = _route(logits)
    runs = runs.reshape(nt, 8, LANES)[:, :3, :N_EXPERTS]
    rg, ro, rn = (runs[:, j, :].reshape(-1) for j in range(3))
    max_rows = 2 * t + nt * N_EXPERTS * (RUN_ALIGN - 1) + N_EXPERTS * (MOE_BLK - 1)
    nb = -(-max_rows // MOE_BLK)
    block_e, n_used = plan[0, :nb], plan[1, :1]
    xb = _dispatch(rg, ro, rn, plan[2, :N_EXPERTS], plan[3, :N_EXPERTS], n_used, h2, slot_t, nb * MOE_BLK)
    yb = _experts(block_e, n_used, xb, w_gate, w_up, w_down)
    return _combine_ln(rg, ro, rn, yb, h2, slots, gate, g, b)


def kernel(x, mem, positions, w_in, w_gla_a2, b_gla_a, g_gla_norm, w_mix_out, ln1_g, ln1_b, w_mq, w_mk, w_mv, w_mo, ln2_g, ln2_b, w_route_group, b_route_group, w_route_expert, b_route_expert, w_exp_gate, w_exp_up, w_exp_down, ln3_g, ln3_b):
    bsz, s, d = x.shape
    assert bsz == 1 and d == D_MODEL
    x2d = x.reshape(s, d)
    row = lambda v: v.reshape(1, -1)

    ret, gla = _mixer(x2d, positions, w_in[0], w_gla_a2[0], b_gla_a[0], g_gla_norm[0])
    h1 = _mixout_ln(ret, gla, x2d, w_mix_out[0].astype(BF16), row(ln1_g[0]), row(ln1_b[0]))

    k, v = _mem_kv(mem[0].astype(BF16), w_mk[0].astype(BF16), w_mv[0].astype(BF16))
    n_route = N_GROUPS + N_EXPERTS
    w_route = jnp.pad(jnp.concatenate([w_route_group[0], w_route_expert[0]], axis=1).astype(BF16),
                      ((0, 0), (0, LANES - n_route)))
    b_route = jnp.pad(jnp.concatenate([b_route_group[0], b_route_expert[0].reshape(-1)]),
                      (0, LANES - n_route)).reshape(1, LANES)
    h2, logits = _cross_attention(h1, w_mq[0].astype(BF16), k, v, w_mo[0].astype(BF16),
                                  row(ln2_g[0]), row(ln2_b[0]), w_route, b_route)

    out = _moe(h2, logits, w_exp_gate[0], w_exp_up[0], w_exp_down[0],
               row(ln3_g[0]), row(ln3_b[0]))
    return out.reshape(bsz, s, d)
```

```python
import math

import jax
import jax.numpy as jnp
from jax import lax
from jax.experimental import pallas as pl
from jax.experimental.pallas import tpu as pltpu

F32 = jnp.float32
BF16 = jnp.bfloat16

D_MODEL = 2048
MEM_LEN = 256
RET_HEADS = 4
RET_DK = 256
RET_DV = 256
GLA_HEADS = 4
GLA_DK = 128
GLA_DV = 256
GLA_LOWRANK = 16
GLA_TAU = 16.0
ROPE_BASE = 10000.0
MEM_HEADS = 4
MEM_HEAD_DIM = D_MODEL // MEM_HEADS
N_GROUPS = 4
EXPERTS_PER_GROUP = 8
N_EXPERTS = N_GROUPS * EXPERTS_PER_GROUP
EXPERT_FF = 512
LN_EPS = 1e-5
DEPTH = 1
DEEPNORM_ALPHA = (2 * DEPTH) ** 0.25

RQ_OFF, RK_OFF, RV_OFF, RG_OFF = 0, 1024, 2048, 3072
GQ_OFF, GK_OFF, GV_OFF, GG_OFF, GLR_OFF = 4096, 4608, 5120, 6144, 7168
IN_MAIN = 7168

LANES = 128
RET_CHUNK = 256
GLA_CHUNK = 128
GLA_LEVELS = 7
MOE_BLK = 256
TILE_TOK = 512
RUN_ALIGN = 16
TILE_SLOTS = 2 * TILE_TOK + N_EXPERTS * RUN_ALIGN
SORT_ROWS = 256
VMEM_LIMIT = 56 * 1024 * 1024


def _cparams(sem):
    return pltpu.CompilerParams(dimension_semantics=sem, vmem_limit_bytes=VMEM_LIMIT)


def _layer_norm(y, g, b):
    mu = jnp.mean(y, axis=-1, keepdims=True)
    d = y - mu
    var = jnp.mean(d * d, axis=-1, keepdims=True)
    return d * lax.rsqrt(var + LN_EPS) * g + b


def _silu(x):
    return x / (1.0 + jnp.exp(-x))


def _dot(a, b):
    return jnp.dot(a, b, preferred_element_type=F32)


def _dot_nt(a, b):
    return lax.dot_general(a, b, (((1,), (1,)), ((), ())), preferred_element_type=F32)


def _dot_tn(a, b):
    return lax.dot_general(a, b, (((0,), (0,)), ((), ())), preferred_element_type=F32)


def _rope_kernel(pos_ref, invf_ref, cos_ref, sin_ref):
    ang = pos_ref[...].astype(F32) * invf_ref[...]
    cos_ref[...] = jnp.cos(ang)
    sin_ref[...] = jnp.sin(ang)


def _rope_table(pos_col, inv_freq):
    s = pos_col.shape[0]
    tm = min(s, 1024)
    half = inv_freq.shape[1]
    return pl.pallas_call(
        _rope_kernel,
        out_shape=(jax.ShapeDtypeStruct((s, half), F32), jax.ShapeDtypeStruct((s, half), F32)),
        grid=(s // tm,),
        in_specs=[pl.BlockSpec((tm, 1), lambda i: (i, 0)),
                  pl.BlockSpec((1, half), lambda i: (0, 0))],
        out_specs=(pl.BlockSpec((tm, half), lambda i: (i, 0)),
                   pl.BlockSpec((tm, half), lambda i: (i, 0))),
        compiler_params=_cparams(("arbitrary",)),
        name="rope_table",
    )(pos_col, inv_freq)


def _mm_kernel(x_ref, w_ref, o_ref):
    o_ref[...] = _dot(x_ref[...], w_ref[...]).astype(o_ref.dtype)


def _proj_in(xb, w_in_b):
    s = xb.shape[0]
    tm = min(s, 1024)
    tn = 512
    return pl.pallas_call(
        _mm_kernel,
        out_shape=jax.ShapeDtypeStruct((s, IN_MAIN), F32),
        grid=(s // tm, IN_MAIN // tn),
        in_specs=[pl.BlockSpec((tm, D_MODEL), lambda i, j: (i, 0)),
                  pl.BlockSpec((D_MODEL, tn), lambda i, j: (0, j))],
        out_specs=pl.BlockSpec((tm, tn), lambda i, j: (i, j)),
        compiler_params=_cparams(("arbitrary", "arbitrary")),
        name="proj_in",
    )(xb, w_in_b)


def _loga_kernel(x_ref, wlr_ref, wa2_ref, ba_ref, o_ref):
    glr = _dot(x_ref[...], wlr_ref[...])
    z = _dot(glr.astype(BF16), wa2_ref[...]) + ba_ref[...]
    log_sig = jnp.minimum(z, 0.0) - jnp.log1p(jnp.exp(-jnp.abs(z)))
    o_ref[...] = log_sig / GLA_TAU


def _gla_log_decay(xb, w_lr, w_a2, b_a):
    s = xb.shape[0]
    tm = min(s, 1024)
    n = GLA_HEADS * GLA_DK
    return pl.pallas_call(
        _loga_kernel,
        out_shape=jax.ShapeDtypeStruct((s, n), F32),
        grid=(s // tm,),
        in_specs=[pl.BlockSpec((tm, D_MODEL), lambda i: (i, 0)),
                  pl.BlockSpec((D_MODEL, LANES), lambda i: (0, 0)),
                  pl.BlockSpec((LANES, n), lambda i: (0, 0)),
                  pl.BlockSpec((1, n), lambda i: (0, 0))],
        out_specs=pl.BlockSpec((tm, n), lambda i: (i, 0)),
        compiler_params=_cparams(("arbitrary",)),
        name="gla_log_decay",
    )(xb, w_lr, w_a2, b_a)


def _rotary(t, cos, sin):
    half = t.shape[-1] // 2
    t1, t2 = t[:, :half], t[:, half:]
    return jnp.concatenate([t1 * cos - t2 * sin, t1 * sin + t2 * cos], axis=-1)


def _retention_kernel(lg_ref, q_ref, k_ref, v_ref, g_ref, cos_ref, sin_ref, o_ref, state_ref):
    c = pl.program_id(0)
    C = q_ref.shape[0]

    @pl.when(c == 0)
    def _():
        state_ref[...] = jnp.zeros_like(state_ref)

    cos = cos_ref[...]
    sin = sin_ref[...]
    ri = lax.broadcasted_iota(jnp.int32, (C, C), 0)
    ci = lax.broadcasted_iota(jnp.int32, (C, C), 1)
    rel = jnp.maximum(ri - ci, 0).astype(F32)
    n = lax.broadcasted_iota(jnp.int32, (C, 1), 0).astype(F32)

    for h in range(RET_HEADS):
        lg = lg_ref[h]
        sl = slice(h * RET_DK, (h + 1) * RET_DK)
        q = _rotary(q_ref[:, sl], cos, sin)
        k = _rotary(k_ref[:, sl], cos, sin) * (RET_DK ** -0.5)
        v = v_ref[:, sl].astype(BF16)
        decay_intra = jnp.where(ri >= ci, jnp.exp(lg * rel), 0.0)
        decay_q = jnp.exp(lg * (n + 1.0))
        decay_k = jnp.exp(lg * (C - 1.0 - n))
        decay_chunk = jnp.exp(lg * C)

        state = state_ref[h]
        scores = _dot_nt(q.astype(BF16), k.astype(BF16)) * decay_intra
        o = _dot(scores.astype(BF16), v) + _dot((q * decay_q).astype(BF16), state.astype(BF16))
        state_ref[h] = decay_chunk * state + _dot_tn((k * decay_k).astype(BF16), v)

        mu = jnp.mean(o, axis=-1, keepdims=True)
        d = o - mu
        var = jnp.mean(d * d, axis=-1, keepdims=True)
        o = d * lax.rsqrt(var + LN_EPS)
        o_ref[:, sl] = (_silu(g_ref[:, sl]) * o).astype(o_ref.dtype)


def _retention(h_main, cos, sin, log_gamma):
    s = h_main.shape[0]
    C = min(RET_CHUNK, s)
    w = RET_HEADS * RET_DK
    col = lambda off: (lambda c, lg: (c, off // w))
    grid_spec = pltpu.PrefetchScalarGridSpec(
        num_scalar_prefetch=1,
        grid=(s // C,),
        in_specs=[pl.BlockSpec((C, w), col(RQ_OFF)),
                  pl.BlockSpec((C, w), col(RK_OFF)),
                  pl.BlockSpec((C, w), col(RV_OFF)),
                  pl.BlockSpec((C, w), col(RG_OFF)),
                  pl.BlockSpec((C, RET_DK // 2), lambda c, lg: (c, 0)),
                  pl.BlockSpec((C, RET_DK // 2), lambda c, lg: (c, 0))],
        out_specs=pl.BlockSpec((C, RET_HEADS * RET_DV), lambda c, lg: (c, 0)),
        scratch_shapes=[pltpu.VMEM((RET_HEADS, RET_DK, RET_DV), F32)],
    )
    return pl.pallas_call(
        _retention_kernel,
        out_shape=jax.ShapeDtypeStruct((s, RET_HEADS * RET_DV), BF16),
        grid_spec=grid_spec,
        compiler_params=_cparams(("arbitrary",)),
        name="retention",
    )(log_gamma, h_main, h_main, h_main, h_main, cos, sin)


def _gla_decay_matrix(C):
    import numpy as np
    levels = int(math.log2(C))
    r = np.arange(C)[:, None]
    t = np.arange(C)[None, :]
    mats = []
    for l in range(levels):
        blk = C >> l
        half = blk // 2
        m = (r // blk) * blk + half - 1
        qside = (r % blk) >= half
        mats.append(np.where(qside, (t > m) & (t <= r), (t > r) & (t <= m)))
    mats.append(t <= r)
    mats.append(t > r)
    return np.concatenate(mats, axis=0).astype(np.float32)


def _gla_kernel(m_ref, q_ref, k_ref, v_ref, g_ref, la_ref, gn_ref, o_ref, state_ref):
    c = pl.program_id(0)
    C = q_ref.shape[0]
    levels = GLA_LEVELS

    @pl.when(c == 0)
    def _():
        state_ref[...] = jnp.zeros_like(state_ref)

    m = m_ref[...]
    ri = lax.broadcasted_iota(jnp.int32, (C, C), 0)
    ci = lax.broadcasted_iota(jnp.int32, (C, C), 1)
    xor = jnp.where(ri > ci, ri ^ ci, 0)
    row = lax.broadcasted_iota(jnp.int32, (C, 1), 0)

    for h in range(GLA_HEADS):
        ks = slice(h * GLA_DK, (h + 1) * GLA_DK)
        vs = slice(h * GLA_DV, (h + 1) * GLA_DV)
        q = q_ref[:, ks] * (GLA_DK ** -0.5)
        k = k_ref[:, ks]
        v = v_ref[:, vs].astype(BF16)
        la = la_ref[:, ks]
        la_hi = la.astype(BF16)
        la_lo = (la - la_hi.astype(F32)).astype(BF16)
        expo = jnp.exp(_dot(m, la_hi) + _dot(m, la_lo))

        scores = jnp.where(ri == ci, _dot_nt(q.astype(BF16), k.astype(BF16)), 0.0)
        for l in range(levels):
            half = C >> (l + 1)
            shift = int(math.log2(half))
            qside = (row & half) != 0
            x = (jnp.where(qside, q, k) * expo[l * C:(l + 1) * C]).astype(BF16)
            scores = scores + jnp.where((xor >> shift) == 1, _dot_nt(x, x), 0.0)

        e_b = expo[levels * C:(levels + 1) * C]
        e_rev = expo[(levels + 1) * C:(levels + 2) * C]
        e_last = e_b[C - 1:C, :]

        state = state_ref[h]
        o = _dot(scores.astype(BF16), v) + _dot_nt((q * e_b).astype(BF16), state.astype(BF16))
        state_ref[h] = state * e_last + _dot_tn(v, (k * e_rev).astype(BF16))

        o = o * lax.rsqrt(jnp.mean(o * o, axis=-1, keepdims=True) + LN_EPS) * gn_ref[...]
        o_ref[:, vs] = (_silu(g_ref[:, vs]) * o).astype(o_ref.dtype)


def _gla(h_main, log_a, g_norm):
    s = h_main.shape[0]
    C = GLA_CHUNK
    m = jnp.asarray(_gla_decay_matrix(C), dtype=BF16)
    nrow = m.shape[0]
    wk = GLA_HEADS * GLA_DK
    wv = GLA_HEADS * GLA_DV
    return pl.pallas_call(
        _gla_kernel,
        out_shape=jax.ShapeDtypeStruct((s, wv), BF16),
        grid=(s // C,),
        in_specs=[pl.BlockSpec((nrow, C), lambda c: (0, 0)),
                  pl.BlockSpec((C, wk), lambda c: (c, GQ_OFF // wk)),
                  pl.BlockSpec((C, wk), lambda c: (c, GK_OFF // wk)),
                  pl.BlockSpec((C, wv), lambda c: (c, GV_OFF // wv)),
                  pl.BlockSpec((C, wv), lambda c: (c, GG_OFF // wv)),
                  pl.BlockSpec((C, wk), lambda c: (c, 0)),
                  pl.BlockSpec((1, GLA_DV), lambda c: (0, 0))],
        out_specs=pl.BlockSpec((C, wv), lambda c: (c, 0)),
        scratch_shapes=[pltpu.VMEM((GLA_HEADS, GLA_DV, GLA_DK), F32)],
        compiler_params=_cparams(("arbitrary",)),
        name="gla",
    )(m, h_main, h_main, h_main, h_main, log_a, g_norm)


def _mixout_kernel(ret_ref, gla_ref, x_ref, w_ref, g_ref, b_ref, o_ref, wb_ref):
    nr = ret_ref.shape[1]

    @pl.when(pl.program_id(0) == 0)
    def _():
        wb_ref[...] = w_ref[...].astype(BF16)

    mix = _dot(ret_ref[...], wb_ref[:nr, :]) + _dot(gla_ref[...], wb_ref[nr:, :])
    o_ref[...] = _layer_norm(DEEPNORM_ALPHA * x_ref[...] + mix, g_ref[...], b_ref[...])


def _mixout_ln(ret, gla, x2d, w_b, g, b):
    s = x2d.shape[0]
    tm = min(s, 512)
    nr, ng = ret.shape[1], gla.shape[1]
    return pl.pallas_call(
        _mixout_kernel,
        out_shape=jax.ShapeDtypeStruct((s, D_MODEL), F32),
        grid=(s // tm,),
        in_specs=[pl.BlockSpec((tm, nr), lambda i: (i, 0)),
                  pl.BlockSpec((tm, ng), lambda i: (i, 0)),
                  pl.BlockSpec((tm, D_MODEL), lambda i: (i, 0)),
                  pl.BlockSpec((nr + ng, D_MODEL), lambda i: (0, 0), pipeline_mode=pl.Buffered(1)),
                  pl.BlockSpec((1, D_MODEL), lambda i: (0, 0)),
                  pl.BlockSpec((1, D_MODEL), lambda i: (0, 0))],
        out_specs=pl.BlockSpec((tm, D_MODEL), lambda i: (i, 0)),
        scratch_shapes=[pltpu.VMEM((nr + ng, D_MODEL), BF16)],
        compiler_params=_cparams(("arbitrary",)),
        name="mixout_ln1",
    )(ret, gla, x2d, w_b, g, b)


def _kv_kernel(mem_ref, wk_ref, wv_ref, k_ref, v_ref):
    m = mem_ref[...].astype(BF16)
    k_ref[...] = _dot(m, wk_ref[...].astype(BF16)).astype(k_ref.dtype)
    v_ref[...] = _dot(m, wv_ref[...].astype(BF16)).astype(v_ref.dtype)


def _mem_kv(mem_b, wk_b, wv_b):
    tn = 512
    return pl.pallas_call(
        _kv_kernel,
        out_shape=(jax.ShapeDtypeStruct((MEM_LEN, D_MODEL), BF16),
                   jax.ShapeDtypeStruct((MEM_LEN, D_MODEL), BF16)),
        grid=(D_MODEL // tn,),
        in_specs=[pl.BlockSpec((MEM_LEN, D_MODEL), lambda j: (0, 0)),
                  pl.BlockSpec((D_MODEL, tn), lambda j: (0, j)),
                  pl.BlockSpec((D_MODEL, tn), lambda j: (0, j))],
        out_specs=(pl.BlockSpec((MEM_LEN, tn), lambda j: (0, j)),
                   pl.BlockSpec((MEM_LEN, tn), lambda j: (0, j))),
        compiler_params=_cparams(("arbitrary",)),
        name="mem_kv",
    )(mem_b, wk_b, wv_b)


def _cross_kernel(h_ref, wq_ref, k_ref, v_ref, wo_ref, g_ref, b_ref, wr_ref, br_ref,
                  h2_ref, lg_ref):
    h1 = h_ref[...]
    q = _dot(h1.astype(BF16), wq_ref[...]).astype(BF16)
    outs = []
    for hd in range(MEM_HEADS):
        sl = slice(hd * MEM_HEAD_DIM, (hd + 1) * MEM_HEAD_DIM)
        s = _dot_nt(q[:, sl], k_ref[:, sl]) * (MEM_HEAD_DIM ** -0.5)
        s = s - jnp.max(s, axis=-1, keepdims=True)
        p = jnp.exp(s)
        p = p / jnp.sum(p, axis=-1, keepdims=True)
        outs.append(_dot(p.astype(BF16), v_ref[:, sl]))
    o = jnp.concatenate(outs, axis=-1).astype(BF16)
    cross = _dot(o, wo_ref[...])
    h2 = _layer_norm(DEEPNORM_ALPHA * h1 + cross, g_ref[...], b_ref[...])
    h2_ref[...] = h2
    lg_ref[...] = _dot(h2.astype(BF16), wr_ref[...]) + br_ref[...]


def _cross_attention(h1, wq_b, k, v, wo_b, g, b, w_route, b_route):
    s = h1.shape[0]
    tm = min(s, 256)
    const = lambda shape: pl.BlockSpec(shape, lambda i: (0, 0), pipeline_mode=pl.Buffered(1))
    return pl.pallas_call(
        _cross_kernel,
        out_shape=(jax.ShapeDtypeStruct((s, D_MODEL), F32),
                   jax.ShapeDtypeStruct((s, LANES), F32)),
        grid=(s // tm,),
        in_specs=[pl.BlockSpec((tm, D_MODEL), lambda i: (i, 0)),
                  const((D_MODEL, D_MODEL)),
                  const((MEM_LEN, D_MODEL)),
                  const((MEM_LEN, D_MODEL)),
                  const((D_MODEL, D_MODEL)),
                  const((1, D_MODEL)),
                  const((1, D_MODEL)),
                  const((D_MODEL, LANES)),
                  const((1, LANES))],
        out_specs=(pl.BlockSpec((tm, D_MODEL), lambda i: (i, 0)),
                   pl.BlockSpec((tm, LANES), lambda i: (i, 0))),
        compiler_params=_cparams(("arbitrary",)),
        name="cross_attn_ln2",
    )(h1, wq_b, k, v, wo_b, g, b, w_route, b_route)


def _route_kernel(lg_ref, slot_ref, slot_t_ref, gate_ref, runs_ref, plan_ref, tot_row, tot_col, gstart):
    phase = pl.program_id(0)
    i = pl.program_id(1)
    tm = lg_ref.shape[0]
    neg = -jnp.inf

    logits = lg_ref[...]
    lane = lax.broadcasted_iota(jnp.int32, (tm, LANES), 1)
    gmask = lane < N_GROUPS
    gl = jnp.where(gmask, logits, neg)
    ge = jnp.exp(gl - jnp.max(gl, axis=-1, keepdims=True))
    pg = ge / jnp.sum(ge, axis=-1, keepdims=True)
    pg_sel = jnp.max(pg, axis=-1, keepdims=True)
    grp = jnp.min(jnp.where((pg == pg_sel) & gmask, lane, LANES), axis=-1, keepdims=True)

    fl_lane = lane - N_GROUPS
    fmask = (fl_lane >= 0) & (fl_lane < N_EXPERTS) & ((fl_lane >> 3) == grp)
    fl = jnp.where(fmask, logits, neg)
    fe = jnp.exp(fl - jnp.max(fl, axis=-1, keepdims=True))
    fp = fe / jnp.sum(fe, axis=-1, keepdims=True)
    p1 = jnp.max(fp, axis=-1, keepdims=True)
    i1 = jnp.min(jnp.where((fp == p1) & fmask, lane, LANES), axis=-1, keepdims=True)
    rest = fmask & (lane != i1)
    fp2 = jnp.where(rest, fp, -1.0)
    p2 = jnp.max(fp2, axis=-1, keepdims=True)
    i2 = jnp.min(jnp.where((fp2 == p2) & rest, lane, LANES), axis=-1, keepdims=True)
    psum = p1 + p2
    gate1 = pg_sel * p1 / psum
    gate2 = pg_sel * p2 / psum

    oh1 = lane == (i1 - N_GROUPS)
    oh2 = lane == (i2 - N_GROUPS)
    oh = (jnp.where(oh1, 1.0, 0.0) + jnp.where(oh2, 1.0, 0.0)).astype(BF16)
    ones = jnp.ones((tm, LANES), BF16)

    def align_up(v, a):
        return jnp.floor((v + (a - 1.0)) * (1.0 / a)) * a

    run_row = align_up(_dot_tn(ones, oh)[0:8, :], RUN_ALIGN)
    rr = lax.broadcasted_iota(jnp.int32, (LANES, LANES), 0)
    cc = lax.broadcasted_iota(jnp.int32, (LANES, LANES), 1)
    strict_upper = jnp.where(rr < cc, 1.0, 0.0).astype(BF16)

    @pl.when((phase == 0) & (i == 0))
    def _():
        tot_row[...] = jnp.zeros_like(tot_row)
        tot_col[...] = jnp.zeros_like(tot_col)

    @pl.when(phase == 0)
    def _():
        tot_row[...] += run_row
        tot_col[...] += align_up(_dot_tn(oh, ones), RUN_ALIGN)

    @pl.when((phase == 1) & (i == 0))
    def _():
        nblk_row = align_up(tot_row[...], MOE_BLK) * (1.0 / MOE_BLK)
        nblk_col = align_up(tot_col[...], MOE_BLK) * (1.0 / MOE_BLK)
        lower_incl = jnp.where(cc <= rr, 1.0, 0.0).astype(BF16)
        base = _dot(nblk_row.astype(BF16), strict_upper) * float(MOE_BLK)
        ends = _dot(lower_incl, nblk_col.astype(BF16))
        expert_rows = rr < N_EXPERTS
        be = jnp.sum(jnp.where(expert_rows & (ends <= cc.astype(F32)), 1.0, 0.0), axis=0, keepdims=True)
        be = jnp.minimum(be, N_EXPERTS - 1.0)
        total = jnp.sum(jnp.where(expert_rows, nblk_col, 0.0), axis=0, keepdims=True)
        sub = lax.broadcasted_iota(jnp.int32, plan_ref.shape, 0)
        plan = jnp.where(sub == 0, jnp.broadcast_to(be, plan_ref.shape),
                         jnp.where(sub == 1, jnp.broadcast_to(total, plan_ref.shape),
                                   jnp.where(sub == 2, base + (nblk_row - 1.0) * float(MOE_BLK), nblk_row)))
        plan_ref[...] = plan.astype(jnp.int32)
        gstart[...] = base

    @pl.when(phase == 1)
    def _():
        off = _dot((run_row * (1.0 / RUN_ALIGN)).astype(BF16), strict_upper) * float(RUN_ALIGN)
        tr = lax.broadcasted_iota(jnp.int32, (tm, tm), 0)
        tc = lax.broadcasted_iota(jnp.int32, (tm, tm), 1)
        strict_lower = jnp.where(tc < tr, 1.0, 0.0).astype(BF16)
        slot = _dot(strict_lower, oh) + off[0:1, :]
        s1 = jnp.sum(jnp.where(oh1, slot, 0.0), axis=-1, keepdims=True)
        s2 = jnp.sum(jnp.where(oh2, slot, 0.0), axis=-1, keepdims=True)
        slots = jnp.where(lane == 0, s1, jnp.where(lane == 1, s2, 0.0))
        slot_ref[...] = slots.astype(jnp.int32)
        slot_t_ref[...] = jnp.transpose(slots)[0:8, :].astype(jnp.int32)
        gate_ref[...] = jnp.where(lane == 0, gate1, jnp.where(lane == 1, gate2, 0.0))
        sub = lax.broadcasted_iota(jnp.int32, runs_ref.shape, 0)
        runs = jnp.where(sub == 0, gstart[...], jnp.where(sub == 1, off, run_row))
        runs_ref[...] = runs.astype(jnp.int32)
        gstart[...] += run_row


def _route(logits):
    t = logits.shape[0]
    tm = min(t, TILE_TOK)
    nt = t // tm
    step = lambda p, i: (i * p, 0)
    return pl.pallas_call(
        _route_kernel,
        out_shape=(jax.ShapeDtypeStruct((t, LANES), jnp.int32),
                   jax.ShapeDtypeStruct((nt * 8, tm), jnp.int32),
                   jax.ShapeDtypeStruct((t, LANES), F32),
                   jax.ShapeDtypeStruct((nt * 8, LANES), jnp.int32),
                   jax.ShapeDtypeStruct((8, LANES), jnp.int32)),
        grid=(2, nt),
        in_specs=[pl.BlockSpec((tm, LANES), lambda p, i: (i, 0))],
        out_specs=(pl.BlockSpec((tm, LANES), step),
                   pl.BlockSpec((8, tm), step),
                   pl.BlockSpec((tm, LANES), step),
                   pl.BlockSpec((8, LANES), step),
                   pl.BlockSpec((8, LANES), lambda p, i: (0, 0))),
        scratch_shapes=[pltpu.VMEM((8, LANES), F32),
                        pltpu.VMEM((LANES, LANES), F32),
                        pltpu.VMEM((8, LANES), F32)],
        compiler_params=_cparams(("arbitrary", "arbitrary")),
        name="moe_route",
    )(logits)


def _run_copies(rg_ref, ro_ref, rn_ref, tile, make_copy):
    def each(action):
        def body(e, carry):
            j = tile * N_EXPERTS + e
            n = rn_ref[j]

            @pl.when(n > 0)
            def _():
                action(make_copy(pl.multiple_of(rg_ref[j], RUN_ALIGN), pl.multiple_of(ro_ref[j], RUN_ALIGN),
                                 pl.multiple_of(n, RUN_ALIGN)))
            return carry
        lax.fori_loop(0, N_EXPERTS, body, 0)
    return each


def _dispatch_kernel(rg_ref, ro_ref, rn_ref, zrow_ref, nblk_ref, nu_ref, x_ref, slot_t_ref, xb_ref,
                     sorted_ref, zero_ref, sems, zsem):
    i = pl.program_id(0)
    n = pl.num_programs(0)
    tm = x_ref.shape[0]
    nb = xb_ref.shape[0] // MOE_BLK
    buf = i % 2

    def zero_copy(row):
        return pltpu.make_async_copy(
            zero_ref, xb_ref.at[pl.ds(pl.multiple_of(row, MOE_BLK), MOE_BLK), :], zsem)

    def runs_of(tile, slot):
        def make_copy(g, o, size):
            return pltpu.make_async_copy(sorted_ref.at[slot, pl.ds(o, size), :],
                                         xb_ref.at[pl.ds(g, size), :], sems.at[slot])
        return _run_copies(rg_ref, ro_ref, rn_ref, tile, make_copy)

    @pl.when(i == 0)
    def _():
        zero_ref[...] = jnp.zeros_like(zero_ref)

        def issue_zero(e, carry):
            @pl.when(nblk_ref[e] > 0)
            def _():
                zero_copy(zrow_ref[e]).start()
            return carry
        lax.fori_loop(0, N_EXPERTS, issue_zero, 0)
        lax.fori_loop(nu_ref[0], nb, lambda b, c: (zero_copy(b * MOE_BLK).start(), c)[1], 0)

        def wait_zero(e, carry):
            @pl.when(nblk_ref[e] > 0)
            def _():
                zero_copy(0).wait()
            return carry
        lax.fori_loop(0, N_EXPERTS, wait_zero, 0)
        lax.fori_loop(nu_ref[0], nb, lambda b, c: (zero_copy(0).wait(), c)[1], 0)

    @pl.when(i >= 2)
    def _():
        runs_of(i - 2, buf)(lambda c: c.wait())

    x = x_ref[...].astype(BF16)
    s1 = slot_t_ref[0:1, :]
    s2 = slot_t_ref[1:2, :]
    for rc in range(TILE_SLOTS // SORT_ROWS):
        r = lax.broadcasted_iota(jnp.int32, (SORT_ROWS, tm), 0) + rc * SORT_ROWS
        perm = jnp.where((r == s1) | (r == s2), 1.0, 0.0).astype(BF16)
        sorted_ref[buf, rc * SORT_ROWS:(rc + 1) * SORT_ROWS, :] = _dot(perm, x).astype(BF16)

    runs_of(i, buf)(lambda c: c.start())

    @pl.when(i == n - 1)
    def _():
        @pl.when(i >= 1)
        def _():
            runs_of(i - 1, 1 - buf)(lambda c: c.wait())
        runs_of(i, buf)(lambda c: c.wait())


def _dispatch(rg, ro, rn, zrow, nblk, n_used, h2, slot_t, n_rows):
    t = h2.shape[0]
    tm = min(t, TILE_TOK)
    grid_spec = pltpu.PrefetchScalarGridSpec(
        num_scalar_prefetch=6,
        grid=(t // tm,),
        in_specs=[pl.BlockSpec((tm, D_MODEL), lambda i, *_: (i, 0)),
                  pl.BlockSpec((8, tm), lambda i, *_: (i, 0))],
        out_specs=pl.BlockSpec(memory_space=pl.ANY),
        scratch_shapes=[pltpu.VMEM((2, TILE_SLOTS, D_MODEL), BF16),
                        pltpu.VMEM((MOE_BLK, D_MODEL), BF16),
                        pltpu.SemaphoreType.DMA((2,)),
                        pltpu.SemaphoreType.DMA(())],
    )
    return pl.pallas_call(
        _dispatch_kernel,
        out_shape=jax.ShapeDtypeStruct((n_rows, D_MODEL), BF16),
        grid_spec=grid_spec,
        compiler_params=_cparams(("arbitrary",)),
        name="moe_dispatch",
    )(rg, ro, rn, zrow, nblk, n_used, h2, slot_t)


def _expert_kernel(be_ref, nu_ref, nblk_ref, x_ref, wg_ref, wu_ref, wd_ref, y_ref,
                   wg_f, wu_f, wd_f, wg_b, wu_b, wd_b, sems, cur_ref):
    b = pl.program_id(0)
    n_used = nu_ref[0]
    used = b < n_used
    e = be_ref[b]
    prev = be_ref[jnp.maximum(b - 1, 0)]
    fresh = used & ((b == 0) | (e != prev))

    def weight_copies(expert, slot):
        return [pltpu.make_async_copy(src.at[expert], dst.at[slot], sems.at[slot])
                for src, dst in ((wg_ref, wg_f), (wu_ref, wu_f), (wd_ref, wd_f))]

    @pl.when(b == 0)
    def _():
        cur_ref[0] = 0
        for c in weight_copies(e, 0):
            c.start()

    @pl.when(fresh)
    def _():
        slot = cur_ref[0]
        for c in weight_copies(e, slot):
            c.wait()
        wg_b[...] = wg_f[slot].astype(BF16)
        wu_b[...] = wu_f[slot].astype(BF16)
        wd_b[...] = wd_f[slot].astype(BF16)
        nxt = b + nblk_ref[e]

        @pl.when(nxt < n_used)
        def _():
            for c in weight_copies(be_ref[nxt], 1 - slot):
                c.start()
        cur_ref[0] = 1 - slot

    @pl.when(used)
    def _():
        x = x_ref[...]
        hid = _silu(_dot(x, wg_b[...])) * _dot(x, wu_b[...])
        y_ref[...] = _dot(hid.astype(BF16), wd_b[...]).astype(y_ref.dtype)

    @pl.when(jnp.logical_not(used))
    def _():
        y_ref[...] = jnp.zeros_like(y_ref)


def _experts(block_e, n_used, nblk, xb, w_gate, w_up, w_down):
    nb = xb.shape[0] // MOE_BLK
    any_space = pl.BlockSpec(memory_space=pl.ANY)
    grid_spec = pltpu.PrefetchScalarGridSpec(
        num_scalar_prefetch=3,
        grid=(nb,),
        in_specs=[pl.BlockSpec((MOE_BLK, D_MODEL), lambda b, be, nu, nk: (jnp.minimum(b, nu[0] - 1), 0)),
                  any_space, any_space, any_space],
        out_specs=pl.BlockSpec((MOE_BLK, D_MODEL), lambda b, be, nu, nk: (b, 0)),
        scratch_shapes=[pltpu.VMEM((2, D_MODEL, EXPERT_FF), F32),
                        pltpu.VMEM((2, D_MODEL, EXPERT_FF), F32),
                        pltpu.VMEM((2, EXPERT_FF, D_MODEL), F32),
                        pltpu.VMEM((D_MODEL, EXPERT_FF), BF16),
                        pltpu.VMEM((D_MODEL, EXPERT_FF), BF16),
                        pltpu.VMEM((EXPERT_FF, D_MODEL), BF16),
                        pltpu.SemaphoreType.DMA((2,)),
                        pltpu.SMEM((1,), jnp.int32)],
    )
    return pl.pallas_call(
        _expert_kernel,
        out_shape=jax.ShapeDtypeStruct(xb.shape, BF16),
        grid_spec=grid_spec,
        compiler_params=_cparams(("arbitrary",)),
        name="moe_experts",
    )(block_e, n_used, nblk, xb, w_gate, w_up, w_down)


def _combine_kernel(rg_ref, ro_ref, rn_ref, yb_ref, h2_ref, slot_ref, gate_ref, g_ref, b_ref, o_ref,
                    ybuf, sems):
    i = pl.program_id(0)
    n = pl.num_programs(0)
    tm = h2_ref.shape[0]
    buf = i % 2

    def runs_of(tile, slot):
        def make_copy(g, o, size):
            return pltpu.make_async_copy(yb_ref.at[pl.ds(g, size), :],
                                         ybuf.at[slot, pl.ds(o, size), :], sems.at[slot])
        return _run_copies(rg_ref, ro_ref, rn_ref, tile, make_copy)

    @pl.when(i == 0)
    def _():
        ybuf[...] = jnp.zeros_like(ybuf)
        runs_of(0, 0)(lambda c: c.start())

    @pl.when(i + 1 < n)
    def _():
        runs_of(i + 1, 1 - buf)(lambda c: c.start())

    runs_of(i, buf)(lambda c: c.wait())

    lane = lax.broadcasted_iota(jnp.int32, (tm, TILE_SLOTS), 1)
    slots = slot_ref[...]
    gate = gate_ref[...]
    y = ybuf[buf]
    sel1 = jnp.where(lane == slots[:, 0:1], 1.0, 0.0).astype(BF16)
    sel2 = jnp.where(lane == slots[:, 1:2], 1.0, 0.0).astype(BF16)
    ffn = gate[:, 0:1] * _dot(sel1, y) + gate[:, 1:2] * _dot(sel2, y)
    o_ref[...] = _layer_norm(DEEPNORM_ALPHA * h2_ref[...] + ffn, g_ref[...], b_ref[...])


def _combine_ln(rg, ro, rn, yb, h2, slots, gate, g, b):
    t = h2.shape[0]
    tm = min(t, TILE_TOK)
    grid_spec = pltpu.PrefetchScalarGridSpec(
        num_scalar_prefetch=3,
        grid=(t // tm,),
        in_specs=[pl.BlockSpec(memory_space=pl.ANY),
                  pl.BlockSpec((tm, D_MODEL), lambda i, *_: (i, 0)),
                  pl.BlockSpec((tm, LANES), lambda i, *_: (i, 0)),
                  pl.BlockSpec((tm, LANES), lambda i, *_: (i, 0)),
                  pl.BlockSpec((1, D_MODEL), lambda i, *_: (0, 0)),
                  pl.BlockSpec((1, D_MODEL), lambda i, *_: (0, 0))],
        out_specs=pl.BlockSpec((tm, D_MODEL), lambda i, *_: (i, 0)),
        scratch_shapes=[pltpu.VMEM((2, TILE_SLOTS, D_MODEL), BF16),
                        pltpu.SemaphoreType.DMA((2,))],
    )
    return pl.pallas_call(
        _combine_kernel,
        out_shape=jax.ShapeDtypeStruct((t, D_MODEL), F32),
        grid_spec=grid_spec,
        compiler_params=_cparams(("arbitrary",)),
        name="moe_combine_ln3",
    )(rg, ro, rn, yb, h2, slots, gate, g, b)


def _mixer(x2d, positions, w_in, w_gla_a2, b_gla_a, g_gla_norm):
    s = x2d.shape[0]
    xb = x2d.astype(BF16)
    w_in_b = w_in.astype(BF16)
    half = RET_DK // 2
    inv_freq = (ROPE_BASE ** (-jnp.arange(half, dtype=F32) / half)).reshape(1, half)
    cos, sin = _rope_table(positions.reshape(s, 1), inv_freq)
    h_main = _proj_in(xb, w_in_b)
    w_lr = jnp.pad(w_in_b[:, GLR_OFF:], ((0, 0), (0, LANES - GLA_LOWRANK)))
    w_a2 = jnp.pad(w_gla_a2.astype(BF16), ((0, LANES - GLA_LOWRANK), (0, 0)))
    log_a = _gla_log_decay(xb, w_lr, w_a2, b_gla_a.reshape(1, -1))
    log_gamma = jnp.log1p(-jnp.exp2(-5.0 - jnp.arange(RET_HEADS, dtype=F32)))
    ret = _retention(h_main, cos, sin, log_gamma)
    gla = _gla(h_main, log_a, g_gla_norm.reshape(1, -1))
    return ret, gla


def _moe(h2, logits, w_gate, w_up, w_down, g, b):
    t = h2.shape[0]
    nt = t // min(t, TILE_TOK)
    slots, slot_t, gate, runs, plan = _route(logits)
    runs = runs.reshape(nt, 8, LANES)[:, :3, :N_EXPERTS]
    rg, ro, rn = (runs[:, j, :].reshape(-1) for j in range(3))
    max_rows = 2 * t + nt * N_EXPERTS * (RUN_ALIGN - 1) + N_EXPERTS * (MOE_BLK - 1)
    nb = -(-max_rows // MOE_BLK)
    block_e, n_used = plan[0, :nb], plan[1, :1]
    last_blk_row, nblk = plan[2, :N_EXPERTS], plan[3, :N_EXPERTS]
    xb = _dispatch(rg, ro, rn, last_blk_row, nblk, n_used, h2, slot_t, nb * MOE_BLK)
    yb = _experts(block_e, n_used, nblk, xb, w_gate, w_up, w_down)
    return _combine_ln(rg, ro, rn, yb, h2, slots, gate, g, b)


def kernel(x, mem, positions, w_in, w_gla_a2, b_gla_a, g_gla_norm, w_mix_out, ln1_g, ln1_b, w_mq, w_mk, w_mv, w_mo, ln2_g, ln2_b, w_route_group, b_route_group, w_route_expert, b_route_expert, w_exp_gate, w_exp_up, w_exp_down, ln3_g, ln3_b):
    bsz, s, d = x.shape
    assert bsz == 1 and d == D_MODEL
    x2d = x.reshape(s, d)
    row = lambda v: v.reshape(1, -1)

    ret, gla = _mixer(x2d, positions, w_in[0], w_gla_a2[0], b_gla_a[0], g_gla_norm[0])
    h1 = _mixout_ln(ret, gla, x2d, w_mix_out[0], row(ln1_g[0]), row(ln1_b[0]))

    k, v = _mem_kv(mem[0], w_mk[0], w_mv[0])
    n_route = N_GROUPS + N_EXPERTS
    w_route = jnp.pad(jnp.concatenate([w_route_group[0], w_route_expert[0]], axis=1).astype(BF16),
                      ((0, 0), (0, LANES - n_route)))
    b_route = jnp.pad(jnp.concatenate([b_route_group[0], b_route_expert[0].reshape(-1)]),
                      (0, LANES - n_route)).reshape(1, LANES)
    h2, logits = _cross_attention(h1, w_mq[0].astype(BF16), k, v, w_mo[0].astype(BF16),
                                  row(ln2_g[0]), row(ln2_b[0]), w_route, b_route)

    out = _moe(h2, logits, w_exp_gate[0], w_exp_up[0], w_exp_down[0],
               row(ln3_g[0]), row(ln3_b[0]))
    return out.reshape(bsz, s, d)
```

```python
import math

import jax
import jax.numpy as jnp
from jax import lax
from jax.experimental import pallas as pl
from jax.experimental.pallas import tpu as pltpu

F32 = jnp.float32
BF16 = jnp.bfloat16

D_MODEL = 2048
MEM_LEN = 256
RET_HEADS = 4
RET_DK = 256
RET_DV = 256
GLA_HEADS = 4
GLA_DK = 128
GLA_DV = 256
GLA_LOWRANK = 16
GLA_TAU = 16.0
ROPE_BASE = 10000.0
MEM_HEADS = 4
MEM_HEAD_DIM = D_MODEL // MEM_HEADS
N_GROUPS = 4
EXPERTS_PER_GROUP = 8
N_EXPERTS = N_GROUPS * EXPERTS_PER_GROUP
EXPERT_FF = 512
LN_EPS = 1e-5
DEPTH = 1
DEEPNORM_ALPHA = (2 * DEPTH) ** 0.25

RQ_OFF, RK_OFF, RV_OFF, RG_OFF = 0, 1024, 2048, 3072
GQ_OFF, GK_OFF, GV_OFF, GG_OFF, GLR_OFF = 4096, 4608, 5120, 6144, 7168
IN_MAIN = 7168

LANES = 128
RET_CHUNK = 256
GLA_CHUNK = 128
GLA_LEVELS = 7
MOE_BLK = 256
TILE_TOK = 512
RUN_ALIGN = 16
TILE_SLOTS = 2 * TILE_TOK + N_EXPERTS * RUN_ALIGN
SORT_ROWS = 256
VMEM_LIMIT = 56 * 1024 * 1024


def _cparams(sem):
    return pltpu.CompilerParams(dimension_semantics=sem, vmem_limit_bytes=VMEM_LIMIT)


def _layer_norm(y, g, b):
    mu = jnp.mean(y, axis=-1, keepdims=True)
    d = y - mu
    var = jnp.mean(d * d, axis=-1, keepdims=True)
    return d * lax.rsqrt(var + LN_EPS) * g + b


def _silu(x):
    return x / (1.0 + jnp.exp(-x))


def _dot(a, b):
    return jnp.dot(a, b, preferred_element_type=F32)


def _dot_nt(a, b):
    return lax.dot_general(a, b, (((1,), (1,)), ((), ())), preferred_element_type=F32)


def _dot_tn(a, b):
    return lax.dot_general(a, b, (((0,), (0,)), ((), ())), preferred_element_type=F32)


def _rope_kernel(pos_ref, invf_ref, cos_ref, sin_ref):
    ang = pos_ref[...].astype(F32) * invf_ref[...]
    cos_ref[...] = jnp.cos(ang)
    sin_ref[...] = jnp.sin(ang)


def _rope_table(pos_col, inv_freq):
    s = pos_col.shape[0]
    tm = min(s, 1024)
    half = inv_freq.shape[1]
    return pl.pallas_call(
        _rope_kernel,
        out_shape=(jax.ShapeDtypeStruct((s, half), F32), jax.ShapeDtypeStruct((s, half), F32)),
        grid=(s // tm,),
        in_specs=[pl.BlockSpec((tm, 1), lambda i: (i, 0)),
                  pl.BlockSpec((1, half), lambda i: (0, 0))],
        out_specs=(pl.BlockSpec((tm, half), lambda i: (i, 0)),
                   pl.BlockSpec((tm, half), lambda i: (i, 0))),
        compiler_params=_cparams(("arbitrary",)),
        name="rope_table",
    )(pos_col, inv_freq)


def _proj_in_kernel(x_ref, w_ref, wlr_ref, wa2_ref, ba_ref, o_ref, la_ref, xb_ref):
    @pl.when(pl.program_id(1) == 0)
    def _():
        xb_ref[...] = x_ref[...].astype(BF16)
        glr = _dot(xb_ref[...], wlr_ref[...])
        z = _dot(glr.astype(BF16), wa2_ref[...]) + ba_ref[...]
        la_ref[...] = (jnp.minimum(z, 0.0) - jnp.log1p(jnp.exp(-jnp.abs(z)))) / GLA_TAU

    o_ref[...] = _dot(xb_ref[...], w_ref[...])


def _proj_in(x2d, w_in_b, w_lr, w_a2, b_a):
    s = x2d.shape[0]
    tm = min(s, 1024)
    tn = 512
    n = GLA_HEADS * GLA_DK
    return pl.pallas_call(
        _proj_in_kernel,
        out_shape=(jax.ShapeDtypeStruct((s, IN_MAIN), F32), jax.ShapeDtypeStruct((s, n), F32)),
        grid=(s // tm, IN_MAIN // tn),
        in_specs=[pl.BlockSpec((tm, D_MODEL), lambda i, j: (i, 0)),
                  pl.BlockSpec((D_MODEL, tn), lambda i, j: (0, j)),
                  pl.BlockSpec((D_MODEL, LANES), lambda i, j: (0, 0)),
                  pl.BlockSpec((LANES, n), lambda i, j: (0, 0)),
                  pl.BlockSpec((1, n), lambda i, j: (0, 0))],
        out_specs=(pl.BlockSpec((tm, tn), lambda i, j: (i, j)),
                   pl.BlockSpec((tm, n), lambda i, j: (i, 0))),
        scratch_shapes=[pltpu.VMEM((tm, D_MODEL), BF16)],
        compiler_params=_cparams(("arbitrary", "arbitrary")),
        name="proj_in",
    )(x2d, w_in_b, w_lr, w_a2, b_a)


def _rotary(t, cos, sin):
    half = t.shape[-1] // 2
    t1, t2 = t[:, :half], t[:, half:]
    return jnp.concatenate([t1 * cos - t2 * sin, t1 * sin + t2 * cos], axis=-1)


def _retention_kernel(lg_ref, q_ref, k_ref, v_ref, g_ref, cos_ref, sin_ref, o_ref, state_ref):
    c = pl.program_id(0)
    C = q_ref.shape[0]

    @pl.when(c == 0)
    def _():
        state_ref[...] = jnp.zeros_like(state_ref)

    cos = cos_ref[...]
    sin = sin_ref[...]
    ri = lax.broadcasted_iota(jnp.int32, (C, C), 0)
    ci = lax.broadcasted_iota(jnp.int32, (C, C), 1)
    rel = jnp.maximum(ri - ci, 0).astype(F32)
    n = lax.broadcasted_iota(jnp.int32, (C, 1), 0).astype(F32)

    for h in range(RET_HEADS):
        lg = lg_ref[h]
        sl = slice(h * RET_DK, (h + 1) * RET_DK)
        q = _rotary(q_ref[:, sl], cos, sin)
        k = _rotary(k_ref[:, sl], cos, sin) * (RET_DK ** -0.5)
        v = v_ref[:, sl].astype(BF16)
        decay_intra = jnp.where(ri >= ci, jnp.exp(lg * rel), 0.0)
        decay_q = jnp.exp(lg * (n + 1.0))
        decay_k = jnp.exp(lg * (C - 1.0 - n))
        decay_chunk = jnp.exp(lg * C)

        state = state_ref[h]
        scores = _dot_nt(q.astype(BF16), k.astype(BF16)) * decay_intra
        o = _dot(scores.astype(BF16), v) + _dot((q * decay_q).astype(BF16), state.astype(BF16))
        state_ref[h] = decay_chunk * state + _dot_tn((k * decay_k).astype(BF16), v)

        mu = jnp.mean(o, axis=-1, keepdims=True)
        d = o - mu
        var = jnp.mean(d * d, axis=-1, keepdims=True)
        o = d * lax.rsqrt(var + LN_EPS)
        o_ref[:, sl] = (_silu(g_ref[:, sl]) * o).astype(o_ref.dtype)


def _retention(h_main, cos, sin, log_gamma):
    s = h_main.shape[0]
    C = min(RET_CHUNK, s)
    w = RET_HEADS * RET_DK
    col = lambda off: (lambda c, lg: (c, off // w))
    grid_spec = pltpu.PrefetchScalarGridSpec(
        num_scalar_prefetch=1,
        grid=(s // C,),
        in_specs=[pl.BlockSpec((C, w), col(RQ_OFF)),
                  pl.BlockSpec((C, w), col(RK_OFF)),
                  pl.BlockSpec((C, w), col(RV_OFF)),
                  pl.BlockSpec((C, w), col(RG_OFF)),
                  pl.BlockSpec((C, RET_DK // 2), lambda c, lg: (c, 0)),
                  pl.BlockSpec((C, RET_DK // 2), lambda c, lg: (c, 0))],
        out_specs=pl.BlockSpec((C, RET_HEADS * RET_DV), lambda c, lg: (c, 0)),
        scratch_shapes=[pltpu.VMEM((RET_HEADS, RET_DK, RET_DV), F32)],
    )
    return pl.pallas_call(
        _retention_kernel,
        out_shape=jax.ShapeDtypeStruct((s, RET_HEADS * RET_DV), BF16),
        grid_spec=grid_spec,
        compiler_params=_cparams(("arbitrary",)),
        name="retention",
    )(log_gamma, h_main, h_main, h_main, h_main, cos, sin)


def _gla_decay_matrix(C):
    import numpy as np
    levels = int(math.log2(C))
    r = np.arange(C)[:, None]
    t = np.arange(C)[None, :]
    mats = []
    for l in range(levels):
        blk = C >> l
        half = blk // 2
        m = (r // blk) * blk + half - 1
        qside = (r % blk) >= half
        mats.append(np.where(qside, (t > m) & (t <= r), (t > r) & (t <= m)))
    mats.append(t <= r)
    mats.append(t > r)
    return np.concatenate(mats, axis=0).astype(np.float32)


def _gla_kernel(m_ref, q_ref, k_ref, v_ref, g_ref, la_ref, gn_ref, o_ref, state_ref):
    c = pl.program_id(0)
    C = q_ref.shape[0]
    levels = GLA_LEVELS

    @pl.when(c == 0)
    def _():
        state_ref[...] = jnp.zeros_like(state_ref)

    m = m_ref[...]
    ri = lax.broadcasted_iota(jnp.int32, (C, C), 0)
    ci = lax.broadcasted_iota(jnp.int32, (C, C), 1)
    xor = jnp.where(ri > ci, ri ^ ci, 0)
    row = lax.broadcasted_iota(jnp.int32, (C, 1), 0)

    for h in range(GLA_HEADS):
        ks = slice(h * GLA_DK, (h + 1) * GLA_DK)
        vs = slice(h * GLA_DV, (h + 1) * GLA_DV)
        q = q_ref[:, ks] * (GLA_DK ** -0.5)
        k = k_ref[:, ks]
        v = v_ref[:, vs].astype(BF16)
        la = la_ref[:, ks]
        la_hi = la.astype(BF16)
        la_lo = (la - la_hi.astype(F32)).astype(BF16)
        expo = jnp.exp(_dot(m, la_hi) + _dot(m, la_lo))

        scores = jnp.where(ri == ci, _dot_nt(q.astype(BF16), k.astype(BF16)), 0.0)
        for l in range(levels):
            half = C >> (l + 1)
            shift = int(math.log2(half))
            qside = (row & half) != 0
            x = (jnp.where(qside, q, k) * expo[l * C:(l + 1) * C]).astype(BF16)
            scores = scores + jnp.where((xor >> shift) == 1, _dot_nt(x, x), 0.0)

        e_b = expo[levels * C:(levels + 1) * C]
        e_rev = expo[(levels + 1) * C:(levels + 2) * C]
        e_last = e_b[C - 1:C, :]

        state = state_ref[h]
        o = _dot(scores.astype(BF16), v) + _dot_nt((q * e_b).astype(BF16), state.astype(BF16))
        state_ref[h] = state * e_last + _dot_tn(v, (k * e_rev).astype(BF16))

        o = o * lax.rsqrt(jnp.mean(o * o, axis=-1, keepdims=True) + LN_EPS) * gn_ref[...]
        o_ref[:, vs] = (_silu(g_ref[:, vs]) * o).astype(o_ref.dtype)


def _gla(h_main, log_a, g_norm):
    s = h_main.shape[0]
    C = GLA_CHUNK
    m = jnp.asarray(_gla_decay_matrix(C), dtype=BF16)
    nrow = m.shape[0]
    wk = GLA_HEADS * GLA_DK
    wv = GLA_HEADS * GLA_DV
    return pl.pallas_call(
        _gla_kernel,
        out_shape=jax.ShapeDtypeStruct((s, wv), BF16),
        grid=(s // C,),
        in_specs=[pl.BlockSpec((nrow, C), lambda c: (0, 0)),
                  pl.BlockSpec((C, wk), lambda c: (c, GQ_OFF // wk)),
                  pl.BlockSpec((C, wk), lambda c: (c, GK_OFF // wk)),
                  pl.BlockSpec((C, wv), lambda c: (c, GV_OFF // wv)),
                  pl.BlockSpec((C, wv), lambda c: (c, GG_OFF // wv)),
                  pl.BlockSpec((C, wk), lambda c: (c, 0)),
                  pl.BlockSpec((1, GLA_DV), lambda c: (0, 0))],
        out_specs=pl.BlockSpec((C, wv), lambda c: (c, 0)),
        scratch_shapes=[pltpu.VMEM((GLA_HEADS, GLA_DV, GLA_DK), F32)],
        compiler_params=_cparams(("arbitrary",)),
        name="gla",
    )(m, h_main, h_main, h_main, h_main, log_a, g_norm)


def _mixout_kernel(ret_ref, gla_ref, x_ref, w_ref, g_ref, b_ref, o_ref, wb_ref):
    nr = ret_ref.shape[1]

    @pl.when(pl.program_id(0) == 0)
    def _():
        wb_ref[...] = w_ref[...].astype(BF16)

    mix = _dot(ret_ref[...], wb_ref[:nr, :]) + _dot(gla_ref[...], wb_ref[nr:, :])
    o_ref[...] = _layer_norm(DEEPNORM_ALPHA * x_ref[...] + mix, g_ref[...], b_ref[...])


def _mixout_ln(ret, gla, x2d, w_b, g, b):
    s = x2d.shape[0]
    tm = min(s, 512)
    nr, ng = ret.shape[1], gla.shape[1]
    return pl.pallas_call(
        _mixout_kernel,
        out_shape=jax.ShapeDtypeStruct((s, D_MODEL), F32),
        grid=(s // tm,),
        in_specs=[pl.BlockSpec((tm, nr), lambda i: (i, 0)),
                  pl.BlockSpec((tm, ng), lambda i: (i, 0)),
                  pl.BlockSpec((tm, D_MODEL), lambda i: (i, 0)),
                  pl.BlockSpec((nr + ng, D_MODEL), lambda i: (0, 0), pipeline_mode=pl.Buffered(1)),
                  pl.BlockSpec((1, D_MODEL), lambda i: (0, 0)),
                  pl.BlockSpec((1, D_MODEL), lambda i: (0, 0))],
        out_specs=pl.BlockSpec((tm, D_MODEL), lambda i: (i, 0)),
        scratch_shapes=[pltpu.VMEM((nr + ng, D_MODEL), BF16)],
        compiler_params=_cparams(("arbitrary",)),
        name="mixout_ln1",
    )(ret, gla, x2d, w_b, g, b)


def _kv_kernel(mem_ref, wk_ref, wv_ref, k_ref, v_ref):
    m = mem_ref[...].astype(BF16)
    k_ref[...] = _dot(m, wk_ref[...].astype(BF16)).astype(k_ref.dtype)
    v_ref[...] = _dot(m, wv_ref[...].astype(BF16)).astype(v_ref.dtype)


def _mem_kv(mem_b, wk_b, wv_b):
    tn = 512
    return pl.pallas_call(
        _kv_kernel,
        out_shape=(jax.ShapeDtypeStruct((MEM_LEN, D_MODEL), BF16),
                   jax.ShapeDtypeStruct((MEM_LEN, D_MODEL), BF16)),
        grid=(D_MODEL // tn,),
        in_specs=[pl.BlockSpec((MEM_LEN, D_MODEL), lambda j: (0, 0)),
                  pl.BlockSpec((D_MODEL, tn), lambda j: (0, j)),
                  pl.BlockSpec((D_MODEL, tn), lambda j: (0, j))],
        out_specs=(pl.BlockSpec((MEM_LEN, tn), lambda j: (0, j)),
                   pl.BlockSpec((MEM_LEN, tn), lambda j: (0, j))),
        compiler_params=_cparams(("arbitrary",)),
        name="mem_kv",
    )(mem_b, wk_b, wv_b)


def _cross_kernel(h_ref, wq_ref, k_ref, v_ref, wo_ref, g_ref, b_ref, wr_ref, br_ref,
                  h2_ref, lg_ref):
    h1 = h_ref[...]
    q = _dot(h1.astype(BF16), wq_ref[...]).astype(BF16)
    outs = []
    for hd in range(MEM_HEADS):
        sl = slice(hd * MEM_HEAD_DIM, (hd + 1) * MEM_HEAD_DIM)
        s = _dot_nt(q[:, sl], k_ref[:, sl]) * (MEM_HEAD_DIM ** -0.5)
        s = s - jnp.max(s, axis=-1, keepdims=True)
        p = jnp.exp(s)
        p = p / jnp.sum(p, axis=-1, keepdims=True)
        outs.append(_dot(p.astype(BF16), v_ref[:, sl]))
    o = jnp.concatenate(outs, axis=-1).astype(BF16)
    cross = _dot(o, wo_ref[...])
    h2 = _layer_norm(DEEPNORM_ALPHA * h1 + cross, g_ref[...], b_ref[...])
    h2_ref[...] = h2
    lg_ref[...] = _dot(h2.astype(BF16), wr_ref[...]) + br_ref[...]


def _cross_attention(h1, wq_b, k, v, wo_b, g, b, w_route, b_route):
    s = h1.shape[0]
    tm = min(s, 256)
    const = lambda shape: pl.BlockSpec(shape, lambda i: (0, 0), pipeline_mode=pl.Buffered(1))
    return pl.pallas_call(
        _cross_kernel,
        out_shape=(jax.ShapeDtypeStruct((s, D_MODEL), F32),
                   jax.ShapeDtypeStruct((s, LANES), F32)),
        grid=(s // tm,),
        in_specs=[pl.BlockSpec((tm, D_MODEL), lambda i: (i, 0)),
                  const((D_MODEL, D_MODEL)),
                  const((MEM_LEN, D_MODEL)),
                  const((MEM_LEN, D_MODEL)),
                  const((D_MODEL, D_MODEL)),
                  const((1, D_MODEL)),
                  const((1, D_MODEL)),
                  const((D_MODEL, LANES)),
                  const((1, LANES))],
        out_specs=(pl.BlockSpec((tm, D_MODEL), lambda i: (i, 0)),
                   pl.BlockSpec((tm, LANES), lambda i: (i, 0))),
        compiler_params=_cparams(("arbitrary",)),
        name="cross_attn_ln2",
    )(h1, wq_b, k, v, wo_b, g, b, w_route, b_route)


def _route_kernel(lg_ref, slot_ref, slot_t_ref, gate_ref, runs_ref, plan_ref, tot_row, tot_col, gstart):
    phase = pl.program_id(0)
    i = pl.program_id(1)
    tm = lg_ref.shape[0]
    neg = -jnp.inf

    logits = lg_ref[...]
    lane = lax.broadcasted_iota(jnp.int32, (tm, LANES), 1)
    gmask = lane < N_GROUPS
    gl = jnp.where(gmask, logits, neg)
    ge = jnp.exp(gl - jnp.max(gl, axis=-1, keepdims=True))
    pg = ge / jnp.sum(ge, axis=-1, keepdims=True)
    pg_sel = jnp.max(pg, axis=-1, keepdims=True)
    grp = jnp.min(jnp.where((pg == pg_sel) & gmask, lane, LANES), axis=-1, keepdims=True)

    fl_lane = lane - N_GROUPS
    fmask = (fl_lane >= 0) & (fl_lane < N_EXPERTS) & ((fl_lane >> 3) == grp)
    fl = jnp.where(fmask, logits, neg)
    fe = jnp.exp(fl - jnp.max(fl, axis=-1, keepdims=True))
    fp = fe / jnp.sum(fe, axis=-1, keepdims=True)
    p1 = jnp.max(fp, axis=-1, keepdims=True)
    i1 = jnp.min(jnp.where((fp == p1) & fmask, lane, LANES), axis=-1, keepdims=True)
    rest = fmask & (lane != i1)
    fp2 = jnp.where(rest, fp, -1.0)
    p2 = jnp.max(fp2, axis=-1, keepdims=True)
    i2 = jnp.min(jnp.where((fp2 == p2) & rest, lane, LANES), axis=-1, keepdims=True)
    psum = p1 + p2
    gate1 = pg_sel * p1 / psum
    gate2 = pg_sel * p2 / psum

    oh1 = lane == (i1 - N_GROUPS)
    oh2 = lane == (i2 - N_GROUPS)
    oh = (jnp.where(oh1, 1.0, 0.0) + jnp.where(oh2, 1.0, 0.0)).astype(BF16)
    ones = jnp.ones((tm, LANES), BF16)

    def align_up(v, a):
        return jnp.floor((v + (a - 1.0)) * (1.0 / a)) * a

    run_row = align_up(_dot_tn(ones, oh)[0:8, :], RUN_ALIGN)
    rr = lax.broadcasted_iota(jnp.int32, (LANES, LANES), 0)
    cc = lax.broadcasted_iota(jnp.int32, (LANES, LANES), 1)
    strict_upper = jnp.where(rr < cc, 1.0, 0.0).astype(BF16)

    @pl.when((phase == 0) & (i == 0))
    def _():
        tot_row[...] = jnp.zeros_like(tot_row)
        tot_col[...] = jnp.zeros_like(tot_col)

    @pl.when(phase == 0)
    def _():
        tot_row[...] += run_row
        tot_col[...] += align_up(_dot_tn(oh, ones), RUN_ALIGN)

    @pl.when((phase == 1) & (i == 0))
    def _():
        nblk_row = align_up(tot_row[...], MOE_BLK) * (1.0 / MOE_BLK)
        nblk_col = align_up(tot_col[...], MOE_BLK) * (1.0 / MOE_BLK)
        lower_incl = jnp.where(cc <= rr, 1.0, 0.0).astype(BF16)
        base = _dot(nblk_row.astype(BF16), strict_upper) * float(MOE_BLK)
        ends = _dot(lower_incl, nblk_col.astype(BF16))
        expert_rows = rr < N_EXPERTS
        be = jnp.sum(jnp.where(expert_rows & (ends <= cc.astype(F32)), 1.0, 0.0), axis=0, keepdims=True)
        be = jnp.minimum(be, N_EXPERTS - 1.0)
        total = jnp.sum(jnp.where(expert_rows, nblk_col, 0.0), axis=0, keepdims=True)
        sub = lax.broadcasted_iota(jnp.int32, plan_ref.shape, 0)
        plan = jnp.where(sub == 0, jnp.broadcast_to(be, plan_ref.shape),
                         jnp.where(sub == 1, jnp.broadcast_to(total, plan_ref.shape),
                                   jnp.where(sub == 2, base + (nblk_row - 1.0) * float(MOE_BLK), nblk_row)))
        plan_ref[...] = plan.astype(jnp.int32)
        gstart[...] = base

    @pl.when(phase == 1)
    def _():
        off = _dot((run_row * (1.0 / RUN_ALIGN)).astype(BF16), strict_upper) * float(RUN_ALIGN)
        tr = lax.broadcasted_iota(jnp.int32, (tm, tm), 0)
        tc = lax.broadcasted_iota(jnp.int32, (tm, tm), 1)
        strict_lower = jnp.where(tc < tr, 1.0, 0.0).astype(BF16)
        slot = _dot(strict_lower, oh) + off[0:1, :]
        s1 = jnp.sum(jnp.where(oh1, slot, 0.0), axis=-1, keepdims=True)
        s2 = jnp.sum(jnp.where(oh2, slot, 0.0), axis=-1, keepdims=True)
        slots = jnp.where(lane == 0, s1, jnp.where(lane == 1, s2, 0.0))
        slot_ref[...] = slots.astype(jnp.int32)
        slot_t_ref[...] = jnp.transpose(slots)[0:8, :].astype(jnp.int32)
        gate_ref[...] = jnp.where(lane == 0, gate1, jnp.where(lane == 1, gate2, 0.0))
        sub = lax.broadcasted_iota(jnp.int32, runs_ref.shape, 0)
        runs = jnp.where(sub == 0, gstart[...], jnp.where(sub == 1, off, run_row))
        runs_ref[...] = runs.astype(jnp.int32)
        gstart[...] += run_row


def _route(logits):
    t = logits.shape[0]
    tm = min(t, TILE_TOK)
    nt = t // tm
    step = lambda p, i: (i * p, 0)
    return pl.pallas_call(
        _route_kernel,
        out_shape=(jax.ShapeDtypeStruct((t, LANES), jnp.int32),
                   jax.ShapeDtypeStruct((nt * 8, tm), jnp.int32),
                   jax.ShapeDtypeStruct((t, LANES), F32),
                   jax.ShapeDtypeStruct((nt * 8, LANES), jnp.int32),
                   jax.ShapeDtypeStruct((8, LANES), jnp.int32)),
        grid=(2, nt),
        in_specs=[pl.BlockSpec((tm, LANES), lambda p, i: (i, 0))],
        out_specs=(pl.BlockSpec((tm, LANES), step),
                   pl.BlockSpec((8, tm), step),
                   pl.BlockSpec((tm, LANES), step),
                   pl.BlockSpec((8, LANES), step),
                   pl.BlockSpec((8, LANES), lambda p, i: (0, 0))),
        scratch_shapes=[pltpu.VMEM((8, LANES), F32),
                        pltpu.VMEM((LANES, LANES), F32),
                        pltpu.VMEM((8, LANES), F32)],
        compiler_params=_cparams(("arbitrary", "arbitrary")),
        name="moe_route",
    )(logits)


def _run_copies(rg_ref, ro_ref, rn_ref, tile, make_copy):
    def each(action):
        def body(e, carry):
            j = tile * N_EXPERTS + e
            n = rn_ref[j]

            @pl.when(n > 0)
            def _():
                action(make_copy(pl.multiple_of(rg_ref[j], RUN_ALIGN), pl.multiple_of(ro_ref[j], RUN_ALIGN),
                                 pl.multiple_of(n, RUN_ALIGN)))
            return carry
        lax.fori_loop(0, N_EXPERTS, body, 0)
    return each


def _dispatch_kernel(rg_ref, ro_ref, rn_ref, zrow_ref, nblk_ref, nu_ref, x_ref, slot_t_ref, xb_ref,
                     sorted_ref, zero_ref, sems, zsem):
    i = pl.program_id(0)
    n = pl.num_programs(0)
    tm = x_ref.shape[0]
    nb = xb_ref.shape[0] // MOE_BLK
    buf = i % 2

    def zero_copy(row):
        return pltpu.make_async_copy(
            zero_ref, xb_ref.at[pl.ds(pl.multiple_of(row, MOE_BLK), MOE_BLK), :], zsem)

    def runs_of(tile, slot):
        def make_copy(g, o, size):
            return pltpu.make_async_copy(sorted_ref.at[slot, pl.ds(o, size), :],
                                         xb_ref.at[pl.ds(g, size), :], sems.at[slot])
        return _run_copies(rg_ref, ro_ref, rn_ref, tile, make_copy)

    @pl.when(i == 0)
    def _():
        zero_ref[...] = jnp.zeros_like(zero_ref)

        def issue_zero(e, carry):
            @pl.when(nblk_ref[e] > 0)
            def _():
                zero_copy(zrow_ref[e]).start()
            return carry
        lax.fori_loop(0, N_EXPERTS, issue_zero, 0)
        lax.fori_loop(nu_ref[0], nb, lambda b, c: (zero_copy(b * MOE_BLK).start(), c)[1], 0)

        def wait_zero(e, carry):
            @pl.when(nblk_ref[e] > 0)
            def _():
                zero_copy(0).wait()
            return carry
        lax.fori_loop(0, N_EXPERTS, wait_zero, 0)
        lax.fori_loop(nu_ref[0], nb, lambda b, c: (zero_copy(0).wait(), c)[1], 0)

    @pl.when(i >= 2)
    def _():
        runs_of(i - 2, buf)(lambda c: c.wait())

    x = x_ref[...].astype(BF16)
    s1 = slot_t_ref[0:1, :]
    s2 = slot_t_ref[1:2, :]
    for rc in range(TILE_SLOTS // SORT_ROWS):
        r = lax.broadcasted_iota(jnp.int32, (SORT_ROWS, tm), 0) + rc * SORT_ROWS
        perm = jnp.where((r == s1) | (r == s2), 1.0, 0.0).astype(BF16)
        sorted_ref[buf, rc * SORT_ROWS:(rc + 1) * SORT_ROWS, :] = _dot(perm, x).astype(BF16)

    runs_of(i, buf)(lambda c: c.start())

    @pl.when(i == n - 1)
    def _():
        @pl.when(i >= 1)
        def _():
            runs_of(i - 1, 1 - buf)(lambda c: c.wait())
        runs_of(i, buf)(lambda c: c.wait())


def _dispatch(rg, ro, rn, zrow, nblk, n_used, h2, slot_t, n_rows):
    t = h2.shape[0]
    tm = min(t, TILE_TOK)
    grid_spec = pltpu.PrefetchScalarGridSpec(
        num_scalar_prefetch=6,
        grid=(t // tm,),
        in_specs=[pl.BlockSpec((tm, D_MODEL), lambda i, *_: (i, 0)),
                  pl.BlockSpec((8, tm), lambda i, *_: (i, 0))],
        out_specs=pl.BlockSpec(memory_space=pl.ANY),
        scratch_shapes=[pltpu.VMEM((2, TILE_SLOTS, D_MODEL), BF16),
                        pltpu.VMEM((MOE_BLK, D_MODEL), BF16),
                        pltpu.SemaphoreType.DMA((2,)),
                        pltpu.SemaphoreType.DMA(())],
    )
    return pl.pallas_call(
        _dispatch_kernel,
        out_shape=jax.ShapeDtypeStruct((n_rows, D_MODEL), BF16),
        grid_spec=grid_spec,
        compiler_params=_cparams(("arbitrary",)),
        name="moe_dispatch",
    )(rg, ro, rn, zrow, nblk, n_used, h2, slot_t)


def _expert_kernel(be_ref, nu_ref, nblk_ref, x_ref, wg_ref, wu_ref, wd_ref, y_ref,
                   wg_f, wu_f, wd_f, wg_b, wu_b, wd_b, sems, cur_ref):
    b = pl.program_id(0)
    n_used = nu_ref[0]
    used = b < n_used
    e = be_ref[b]
    prev = be_ref[jnp.maximum(b - 1, 0)]
    fresh = used & ((b == 0) | (e != prev))

    def weight_copies(expert, slot):
        return [pltpu.make_async_copy(src.at[expert], dst.at[slot], sems.at[slot])
                for src, dst in ((wg_ref, wg_f), (wu_ref, wu_f), (wd_ref, wd_f))]

    @pl.when(b == 0)
    def _():
        cur_ref[0] = 0
        for c in weight_copies(e, 0):
            c.start()

    @pl.when(fresh)
    def _():
        slot = cur_ref[0]
        for c in weight_copies(e, slot):
            c.wait()
        wg_b[...] = wg_f[slot].astype(BF16)
        wu_b[...] = wu_f[slot].astype(BF16)
        wd_b[...] = wd_f[slot].astype(BF16)
        nxt = b + nblk_ref[e]

        @pl.when(nxt < n_used)
        def _():
            for c in weight_copies(be_ref[nxt], 1 - slot):
                c.start()
        cur_ref[0] = 1 - slot

    @pl.when(used)
    def _():
        x = x_ref[...]
        hid = _silu(_dot(x, wg_b[...])) * _dot(x, wu_b[...])
        y_ref[...] = _dot(hid.astype(BF16), wd_b[...]).astype(y_ref.dtype)

    @pl.when(jnp.logical_not(used))
    def _():
        y_ref[...] = jnp.zeros_like(y_ref)


def _experts(block_e, n_used, nblk, xb, w_gate, w_up, w_down):
    nb = xb.shape[0] // MOE_BLK
    any_space = pl.BlockSpec(memory_space=pl.ANY)
    grid_spec = pltpu.PrefetchScalarGridSpec(
        num_scalar_prefetch=3,
        grid=(nb,),
        in_specs=[pl.BlockSpec((MOE_BLK, D_MODEL), lambda b, be, nu, nk: (jnp.minimum(b, nu[0] - 1), 0)),
                  any_space, any_space, any_space],
        out_specs=pl.BlockSpec((MOE_BLK, D_MODEL), lambda b, be, nu, nk: (b, 0)),
        scratch_shapes=[pltpu.VMEM((2, D_MODEL, EXPERT_FF), F32),
                        pltpu.VMEM((2, D_MODEL, EXPERT_FF), F32),
                        pltpu.VMEM((2, EXPERT_FF, D_MODEL), F32),
                        pltpu.VMEM((D_MODEL, EXPERT_FF), BF16),
                        pltpu.VMEM((D_MODEL, EXPERT_FF), BF16),
                        pltpu.VMEM((EXPERT_FF, D_MODEL), BF16),
                        pltpu.SemaphoreType.DMA((2,)),
                        pltpu.SMEM((1,), jnp.int32)],
    )
    return pl.pallas_call(
        _expert_kernel,
        out_shape=jax.ShapeDtypeStruct(xb.shape, BF16),
        grid_spec=grid_spec,
        compiler_params=_cparams(("arbitrary",)),
        name="moe_experts",
    )(block_e, n_used, nblk, xb, w_gate, w_up, w_down)


def _combine_kernel(rg_ref, ro_ref, rn_ref, yb_ref, h2_ref, slot_ref, gate_ref, g_ref, b_ref, o_ref,
                    ybuf, sems):
    i = pl.program_id(0)
    n = pl.num_programs(0)
    tm = h2_ref.shape[0]
    buf = i % 2

    def runs_of(tile, slot):
        def make_copy(g, o, size):
            return pltpu.make_async_copy(yb_ref.at[pl.ds(g, size), :],
                                         ybuf.at[slot, pl.ds(o, size), :], sems.at[slot])
        return _run_copies(rg_ref, ro_ref, rn_ref, tile, make_copy)

    @pl.when(i == 0)
    def _():
        ybuf[...] = jnp.zeros_like(ybuf)
        runs_of(0, 0)(lambda c: c.start())

    @pl.when(i + 1 < n)
    def _():
        runs_of(i + 1, 1 - buf)(lambda c: c.start())

    runs_of(i, buf)(lambda c: c.wait())

    lane = lax.broadcasted_iota(jnp.int32, (tm, TILE_SLOTS), 1)
    slots = slot_ref[...]
    gate = gate_ref[...]
    y = ybuf[buf]
    sel = jnp.where(lane == slots[:, 0:1], gate[:, 0:1],
                    jnp.where(lane == slots[:, 1:2], gate[:, 1:2], 0.0)).astype(BF16)
    ffn = _dot(sel, y)
    o_ref[...] = _layer_norm(DEEPNORM_ALPHA * h2_ref[...] + ffn, g_ref[...], b_ref[...])


def _combine_ln(rg, ro, rn, yb, h2, slots, gate, g, b):
    t = h2.shape[0]
    tm = min(t, TILE_TOK)
    grid_spec = pltpu.PrefetchScalarGridSpec(
        num_scalar_prefetch=3,
        grid=(t // tm,),
        in_specs=[pl.BlockSpec(memory_space=pl.ANY),
                  pl.BlockSpec((tm, D_MODEL), lambda i, *_: (i, 0)),
                  pl.BlockSpec((tm, LANES), lambda i, *_: (i, 0)),
                  pl.BlockSpec((tm, LANES), lambda i, *_: (i, 0)),
                  pl.BlockSpec((1, D_MODEL), lambda i, *_: (0, 0)),
                  pl.BlockSpec((1, D_MODEL), lambda i, *_: (0, 0))],
        out_specs=pl.BlockSpec((tm, D_MODEL), lambda i, *_: (i, 0)),
        scratch_shapes=[pltpu.VMEM((2, TILE_SLOTS, D_MODEL), BF16),
                        pltpu.SemaphoreType.DMA((2,))],
    )
    return pl.pallas_call(
        _combine_kernel,
        out_shape=jax.ShapeDtypeStruct((t, D_MODEL), F32),
        grid_spec=grid_spec,
        compiler_params=_cparams(("arbitrary",)),
        name="moe_combine_ln3",
    )(rg, ro, rn, yb, h2, slots, gate, g, b)


def _mixer(x2d, positions, w_in, w_gla_a2, b_gla_a, g_gla_norm):
    s = x2d.shape[0]
    w_in_b = w_in.astype(BF16)
    half = RET_DK // 2
    inv_freq = (ROPE_BASE ** (-jnp.arange(half, dtype=F32) / half)).reshape(1, half)
    cos, sin = _rope_table(positions.reshape(s, 1), inv_freq)
    w_lr = jnp.pad(w_in_b[:, GLR_OFF:], ((0, 0), (0, LANES - GLA_LOWRANK)))
    w_a2 = jnp.pad(w_gla_a2.astype(BF16), ((0, LANES - GLA_LOWRANK), (0, 0)))
    h_main, log_a = _proj_in(x2d, w_in_b, w_lr, w_a2, b_gla_a.reshape(1, -1))
    log_gamma = jnp.log1p(-jnp.exp2(-5.0 - jnp.arange(RET_HEADS, dtype=F32)))
    ret = _retention(h_main, cos, sin, log_gamma)
    gla = _gla(h_main, log_a, g_gla_norm.reshape(1, -1))
    return ret, gla


def _moe(h2, logits, w_gate, w_up, w_down, g, b):
    t = h2.shape[0]
    nt = t // min(t, TILE_TOK)
    slots, slot_t, gate, runs, plan = _route(logits)
    runs = runs.reshape(nt, 8, LANES)[:, :3, :N_EXPERTS]
    rg, ro, rn = (runs[:, j, :].reshape(-1) for j in range(3))
    max_rows = 2 * t + nt * N_EXPERTS * (RUN_ALIGN - 1) + N_EXPERTS * (MOE_BLK - 1)
    nb = -(-max_rows // MOE_BLK)
    block_e, n_used = plan[0, :nb], plan[1, :1]
    last_blk_row, nblk = plan[2, :N_EXPERTS], plan[3, :N_EXPERTS]
    xb = _dispatch(rg, ro, rn, last_blk_row, nblk, n_used, h2, slot_t, nb * MOE_BLK)
    yb = _experts(block_e, n_used, nblk, xb, w_gate, w_up, w_down)
    return _combine_ln(rg, ro, rn, yb, h2, slots, gate, g, b)


def kernel(x, mem, positions, w_in, w_gla_a2, b_gla_a, g_gla_norm, w_mix_out, ln1_g, ln1_b, w_mq, w_mk, w_mv, w_mo, ln2_g, ln2_b, w_route_group, b_route_group, w_route_expert, b_route_expert, w_exp_gate, w_exp_up, w_exp_down, ln3_g, ln3_b):
    bsz, s, d = x.shape
    assert bsz == 1 and d == D_MODEL
    x2d = x.reshape(s, d)
    row = lambda v: v.reshape(1, -1)

    ret, gla = _mixer(x2d, positions, w_in[0], w_gla_a2[0], b_gla_a[0], g_gla_norm[0])
    h1 = _mixout_ln(ret, gla, x2d, w_mix_out[0], row(ln1_g[0]), row(ln1_b[0]))

    k, v = _mem_kv(mem[0], w_mk[0], w_mv[0])
    n_route = N_GROUPS + N_EXPERTS
    w_route = jnp.pad(jnp.concatenate([w_route_group[0], w_route_expert[0]], axis=1).astype(BF16),
                      ((0, 0), (0, LANES - n_route)))
    b_route = jnp.pad(jnp.concatenate([b_route_group[0], b_route_expert[0].reshape(-1)]),
                      (0, LANES - n_route)).reshape(1, LANES)
    h2, logits = _cross_attention(h1, w_mq[0].astype(BF16), k, v, w_mo[0].astype(BF16),
                                  row(ln2_g[0]), row(ln2_b[0]), w_route, b_route)

    out = _moe(h2, logits, w_exp_gate[0], w_exp_up[0], w_exp_down[0],
               row(ln3_g[0]), row(ln3_b[0]))
    return out.reshape(bsz, s, d)
```

```python
import math

import jax
import jax.numpy as jnp
from jax import lax
from jax.experimental import pallas as pl
from jax.experimental.pallas import tpu as pltpu

F32 = jnp.float32
BF16 = jnp.bfloat16

D_MODEL = 2048
MEM_LEN = 256
RET_HEADS = 4
RET_DK = 256
RET_DV = 256
GLA_HEADS = 4
GLA_DK = 128
GLA_DV = 256
GLA_LOWRANK = 16
GLA_TAU = 16.0
ROPE_BASE = 10000.0
MEM_HEADS = 4
MEM_HEAD_DIM = D_MODEL // MEM_HEADS
N_GROUPS = 4
EXPERTS_PER_GROUP = 8
N_EXPERTS = N_GROUPS * EXPERTS_PER_GROUP
EXPERT_FF = 512
LN_EPS = 1e-5
DEPTH = 1
DEEPNORM_ALPHA = (2 * DEPTH) ** 0.25

RQ_OFF, RK_OFF, RV_OFF, RG_OFF = 0, 1024, 2048, 3072
GQ_OFF, GK_OFF, GV_OFF, GG_OFF, GLR_OFF = 4096, 4608, 5120, 6144, 7168
IN_MAIN = 7168

LANES = 128
RET_CHUNK = 256
GLA_CHUNK = 128
GLA_LEVELS = 7
MOE_BLK = 256
TILE_TOK = 512
RUN_ALIGN = 16
TILE_SLOTS = 2 * TILE_TOK + N_EXPERTS * RUN_ALIGN
SORT_ROWS = 256
VMEM_LIMIT = 56 * 1024 * 1024


def _cparams(sem):
    return pltpu.CompilerParams(dimension_semantics=sem, vmem_limit_bytes=VMEM_LIMIT)


def _layer_norm(y, g, b):
    mu = jnp.mean(y, axis=-1, keepdims=True)
    d = y - mu
    var = jnp.mean(d * d, axis=-1, keepdims=True)
    return d * lax.rsqrt(var + LN_EPS) * g + b


def _silu(x):
    return x / (1.0 + jnp.exp(-x))


def _dot(a, b):
    return jnp.dot(a, b, preferred_element_type=F32)


def _dot_nt(a, b):
    return lax.dot_general(a, b, (((1,), (1,)), ((), ())), preferred_element_type=F32)


def _dot_tn(a, b):
    return lax.dot_general(a, b, (((0,), (0,)), ((), ())), preferred_element_type=F32)


def _rope_kernel(pos_ref, invf_ref, cos_ref, sin_ref):
    ang = pos_ref[...].astype(F32) * invf_ref[...]
    cos_ref[...] = jnp.cos(ang)
    sin_ref[...] = jnp.sin(ang)


def _rope_table(pos_col, inv_freq):
    s = pos_col.shape[0]
    tm = min(s, 1024)
    half = inv_freq.shape[1]
    return pl.pallas_call(
        _rope_kernel,
        out_shape=(jax.ShapeDtypeStruct((s, half), F32), jax.ShapeDtypeStruct((s, half), F32)),
        grid=(s // tm,),
        in_specs=[pl.BlockSpec((tm, 1), lambda i: (i, 0)),
                  pl.BlockSpec((1, half), lambda i: (0, 0))],
        out_specs=(pl.BlockSpec((tm, half), lambda i: (i, 0)),
                   pl.BlockSpec((tm, half), lambda i: (i, 0))),
        compiler_params=_cparams(("arbitrary",)),
        name="rope_table",
    )(pos_col, inv_freq)


def _proj_in_kernel(x_ref, w_ref, wlr_ref, wa2_ref, ba_ref, o_ref, la_ref, xb_ref):
    @pl.when(pl.program_id(1) == 0)
    def _():
        xb_ref[...] = x_ref[...].astype(BF16)
        glr = _dot(xb_ref[...], wlr_ref[...])
        z = _dot(glr.astype(BF16), wa2_ref[...]) + ba_ref[...]
        la_ref[...] = (jnp.minimum(z, 0.0) - jnp.log1p(jnp.exp(-jnp.abs(z)))) / GLA_TAU

    o_ref[...] = _dot(xb_ref[...], w_ref[...])


def _proj_in(x2d, w_in_b, w_lr, w_a2, b_a):
    s = x2d.shape[0]
    tm = min(s, 1024)
    tn = 512
    n = GLA_HEADS * GLA_DK
    return pl.pallas_call(
        _proj_in_kernel,
        out_shape=(jax.ShapeDtypeStruct((s, IN_MAIN), F32), jax.ShapeDtypeStruct((s, n), F32)),
        grid=(s // tm, IN_MAIN // tn),
        in_specs=[pl.BlockSpec((tm, D_MODEL), lambda i, j: (i, 0)),
                  pl.BlockSpec((D_MODEL, tn), lambda i, j: (0, j)),
                  pl.BlockSpec((D_MODEL, LANES), lambda i, j: (0, 0)),
                  pl.BlockSpec((LANES, n), lambda i, j: (0, 0)),
                  pl.BlockSpec((1, n), lambda i, j: (0, 0))],
        out_specs=(pl.BlockSpec((tm, tn), lambda i, j: (i, j)),
                   pl.BlockSpec((tm, n), lambda i, j: (i, 0))),
        scratch_shapes=[pltpu.VMEM((tm, D_MODEL), BF16)],
        compiler_params=_cparams(("arbitrary", "arbitrary")),
        name="proj_in",
    )(x2d, w_in_b, w_lr, w_a2, b_a)


def _rotary(t, cos, sin):
    half = t.shape[-1] // 2
    t1, t2 = t[:, :half], t[:, half:]
    return jnp.concatenate([t1 * cos - t2 * sin, t1 * sin + t2 * cos], axis=-1)


def _retention_kernel(lg_ref, q_ref, k_ref, v_ref, g_ref, cos_ref, sin_ref, o_ref, state_ref, intra_ref, dq_ref, dk_ref):
    c = pl.program_id(0)
    C = q_ref.shape[0]

    @pl.when(c == 0)
    def _():
        state_ref[...] = jnp.zeros_like(state_ref)
        ri = lax.broadcasted_iota(jnp.int32, (C, C), 0)
        ci = lax.broadcasted_iota(jnp.int32, (C, C), 1)
        rel = jnp.maximum(ri - ci, 0).astype(F32)
        n = lax.broadcasted_iota(jnp.int32, (C, RET_DK), 0).astype(F32)
        for h in range(RET_HEADS):
            lg = lg_ref[h]
            intra_ref[h] = jnp.where(ri >= ci, jnp.exp(lg * rel), 0.0)
            dq_ref[h] = jnp.exp(lg * (n + 1.0))
            dk_ref[h] = jnp.exp(lg * (C - 1.0 - n))

    cos = cos_ref[...]
    sin = sin_ref[...]

    for h in range(RET_HEADS):
        sl = slice(h * RET_DK, (h + 1) * RET_DK)
        q = _rotary(q_ref[:, sl], cos, sin)
        k = _rotary(k_ref[:, sl], cos, sin) * (RET_DK ** -0.5)
        v = v_ref[:, sl].astype(BF16)
        decay_q = dq_ref[h]
        decay_k = dk_ref[h]
        decay_chunk = jnp.exp(lg_ref[h] * C)

        state = state_ref[h]
        scores = _dot_nt(q.astype(BF16), k.astype(BF16)) * intra_ref[h]
        o = _dot(scores.astype(BF16), v) + _dot((q * decay_q).astype(BF16), state.astype(BF16))
        state_ref[h] = decay_chunk * state + _dot_tn((k * decay_k).astype(BF16), v)

        mu = jnp.mean(o, axis=-1, keepdims=True)
        d = o - mu
        var = jnp.mean(d * d, axis=-1, keepdims=True)
        o = d * lax.rsqrt(var + LN_EPS)
        o_ref[:, sl] = (_silu(g_ref[:, sl]) * o).astype(o_ref.dtype)


def _retention(h_main, cos, sin, log_gamma):
    s = h_main.shape[0]
    C = min(RET_CHUNK, s)
    w = RET_HEADS * RET_DK
    col = lambda off: (lambda c, lg: (c, off // w))
    grid_spec = pltpu.PrefetchScalarGridSpec(
        num_scalar_prefetch=1,
        grid=(s // C,),
        in_specs=[pl.BlockSpec((C, w), col(RQ_OFF)),
                  pl.BlockSpec((C, w), col(RK_OFF)),
                  pl.BlockSpec((C, w), col(RV_OFF)),
                  pl.BlockSpec((C, w), col(RG_OFF)),
                  pl.BlockSpec((C, RET_DK // 2), lambda c, lg: (c, 0)),
                  pl.BlockSpec((C, RET_DK // 2), lambda c, lg: (c, 0))],
        out_specs=pl.BlockSpec((C, RET_HEADS * RET_DV), lambda c, lg: (c, 0)),
        scratch_shapes=[pltpu.VMEM((RET_HEADS, RET_DK, RET_DV), F32),
                        pltpu.VMEM((RET_HEADS, C, C), F32),
                        pltpu.VMEM((RET_HEADS, C, RET_DK), F32),
                        pltpu.VMEM((RET_HEADS, C, RET_DK), F32)],
    )
    return pl.pallas_call(
        _retention_kernel,
        out_shape=jax.ShapeDtypeStruct((s, RET_HEADS * RET_DV), BF16),
        grid_spec=grid_spec,
        compiler_params=_cparams(("arbitrary",)),
        name="retention",
    )(log_gamma, h_main, h_main, h_main, h_main, cos, sin)


def _gla_decay_matrix(C):
    import numpy as np
    levels = int(math.log2(C))
    r = np.arange(C)[:, None]
    t = np.arange(C)[None, :]
    mats = []
    for l in range(levels):
        blk = C >> l
        half = blk // 2
        m = (r // blk) * blk + half - 1
        qside = (r % blk) >= half
        mats.append(np.where(qside, (t > m) & (t <= r), (t > r) & (t <= m)))
    mats.append(t <= r)
    mats.append(t > r)
    return np.concatenate(mats, axis=0).astype(np.float32)


def _gla_kernel(m_ref, q_ref, k_ref, v_ref, g_ref, la_ref, gn_ref, o_ref, state_ref):
    c = pl.program_id(0)
    C = q_ref.shape[0]
    levels = GLA_LEVELS

    @pl.when(c == 0)
    def _():
        state_ref[...] = jnp.zeros_like(state_ref)

    m = m_ref[...]
    ri = lax.broadcasted_iota(jnp.int32, (C, C), 0)
    ci = lax.broadcasted_iota(jnp.int32, (C, C), 1)
    xor = jnp.where(ri > ci, ri ^ ci, 0)
    row = lax.broadcasted_iota(jnp.int32, (C, 1), 0)

    for h in range(GLA_HEADS):
        ks = slice(h * GLA_DK, (h + 1) * GLA_DK)
        vs = slice(h * GLA_DV, (h + 1) * GLA_DV)
        q = q_ref[:, ks] * (GLA_DK ** -0.5)
        k = k_ref[:, ks]
        v = v_ref[:, vs].astype(BF16)
        la = la_ref[:, ks]
        la_hi = la.astype(BF16)
        la_lo = (la - la_hi.astype(F32)).astype(BF16)
        expo = jnp.exp(_dot(m, la_hi) + _dot(m, la_lo))

        scores = jnp.where(ri == ci, _dot_nt(q.astype(BF16), k.astype(BF16)), 0.0)
        for l in range(levels):
            half = C >> (l + 1)
            shift = int(math.log2(half))
            qside = (row & half) != 0
            x = (jnp.where(qside, q, k) * expo[l * C:(l + 1) * C]).astype(BF16)
            scores = scores + jnp.where((xor >> shift) == 1, _dot_nt(x, x), 0.0)

        e_b = expo[levels * C:(levels + 1) * C]
        e_rev = expo[(levels + 1) * C:(levels + 2) * C]
        e_last = e_b[C - 1:C, :]

        state = state_ref[h]
        o = _dot(scores.astype(BF16), v) + _dot_nt((q * e_b).astype(BF16), state.astype(BF16))
        state_ref[h] = state * e_last + _dot_tn(v, (k * e_rev).astype(BF16))

        o = o * lax.rsqrt(jnp.mean(o * o, axis=-1, keepdims=True) + LN_EPS) * gn_ref[...]
        o_ref[:, vs] = (_silu(g_ref[:, vs]) * o).astype(o_ref.dtype)


def _gla(h_main, log_a, g_norm):
    s = h_main.shape[0]
    C = GLA_CHUNK
    m = jnp.asarray(_gla_decay_matrix(C), dtype=BF16)
    nrow = m.shape[0]
    wk = GLA_HEADS * GLA_DK
    wv = GLA_HEADS * GLA_DV
    return pl.pallas_call(
        _gla_kernel,
        out_shape=jax.ShapeDtypeStruct((s, wv), BF16),
        grid=(s // C,),
        in_specs=[pl.BlockSpec((nrow, C), lambda c: (0, 0)),
                  pl.BlockSpec((C, wk), lambda c: (c, GQ_OFF // wk)),
                  pl.BlockSpec((C, wk), lambda c: (c, GK_OFF // wk)),
                  pl.BlockSpec((C, wv), lambda c: (c, GV_OFF // wv)),
                  pl.BlockSpec((C, wv), lambda c: (c, GG_OFF // wv)),
                  pl.BlockSpec((C, wk), lambda c: (c, 0)),
                  pl.BlockSpec((1, GLA_DV), lambda c: (0, 0))],
        out_specs=pl.BlockSpec((C, wv), lambda c: (c, 0)),
        scratch_shapes=[pltpu.VMEM((GLA_HEADS, GLA_DV, GLA_DK), F32)],
        compiler_params=_cparams(("arbitrary",)),
        name="gla",
    )(m, h_main, h_main, h_main, h_main, log_a, g_norm)


def _mixout_kernel(ret_ref, gla_ref, x_ref, w_ref, g_ref, b_ref, o_ref, wb_ref):
    nr = ret_ref.shape[1]

    @pl.when(pl.program_id(0) == 0)
    def _():
        wb_ref[...] = w_ref[...].astype(BF16)

    mix = _dot(ret_ref[...], wb_ref[:nr, :]) + _dot(gla_ref[...], wb_ref[nr:, :])
    o_ref[...] = _layer_norm(DEEPNORM_ALPHA * x_ref[...] + mix, g_ref[...], b_ref[...])


def _mixout_ln(ret, gla, x2d, w_b, g, b):
    s = x2d.shape[0]
    tm = min(s, 512)
    nr, ng = ret.shape[1], gla.shape[1]
    return pl.pallas_call(
        _mixout_kernel,
        out_shape=jax.ShapeDtypeStruct((s, D_MODEL), F32),
        grid=(s // tm,),
        in_specs=[pl.BlockSpec((tm, nr), lambda i: (i, 0)),
                  pl.BlockSpec((tm, ng), lambda i: (i, 0)),
                  pl.BlockSpec((tm, D_MODEL), lambda i: (i, 0)),
                  pl.BlockSpec((nr + ng, D_MODEL), lambda i: (0, 0), pipeline_mode=pl.Buffered(1)),
                  pl.BlockSpec((1, D_MODEL), lambda i: (0, 0)),
                  pl.BlockSpec((1, D_MODEL), lambda i: (0, 0))],
        out_specs=pl.BlockSpec((tm, D_MODEL), lambda i: (i, 0)),
        scratch_shapes=[pltpu.VMEM((nr + ng, D_MODEL), BF16)],
        compiler_params=_cparams(("arbitrary",)),
        name="mixout_ln1",
    )(ret, gla, x2d, w_b, g, b)


def _kv_kernel(mem_ref, wk_ref, wv_ref, k_ref, v_ref):
    m = mem_ref[...].astype(BF16)
    k_ref[...] = _dot(m, wk_ref[...].astype(BF16)).astype(k_ref.dtype)
    v_ref[...] = _dot(m, wv_ref[...].astype(BF16)).astype(v_ref.dtype)


def _mem_kv(mem_b, wk_b, wv_b):
    tn = 512
    return pl.pallas_call(
        _kv_kernel,
        out_shape=(jax.ShapeDtypeStruct((MEM_LEN, D_MODEL), BF16),
                   jax.ShapeDtypeStruct((MEM_LEN, D_MODEL), BF16)),
        grid=(D_MODEL // tn,),
        in_specs=[pl.BlockSpec((MEM_LEN, D_MODEL), lambda j: (0, 0)),
                  pl.BlockSpec((D_MODEL, tn), lambda j: (0, j)),
                  pl.BlockSpec((D_MODEL, tn), lambda j: (0, j))],
        out_specs=(pl.BlockSpec((MEM_LEN, tn), lambda j: (0, j)),
                   pl.BlockSpec((MEM_LEN, tn), lambda j: (0, j))),
        compiler_params=_cparams(("arbitrary",)),
        name="mem_kv",
    )(mem_b, wk_b, wv_b)


def _cross_kernel(h_ref, wq_ref, k_ref, v_ref, wo_ref, g_ref, b_ref, wr_ref, br_ref,
                  h2_ref, lg_ref):
    h1 = h_ref[...]
    q = _dot(h1.astype(BF16), wq_ref[...]).astype(BF16)
    outs = []
    for hd in range(MEM_HEADS):
        sl = slice(hd * MEM_HEAD_DIM, (hd + 1) * MEM_HEAD_DIM)
        s = _dot_nt(q[:, sl], k_ref[:, sl]) * (MEM_HEAD_DIM ** -0.5)
        s = s - jnp.max(s, axis=-1, keepdims=True)
        p = jnp.exp(s)
        p = p / jnp.sum(p, axis=-1, keepdims=True)
        outs.append(_dot(p.astype(BF16), v_ref[:, sl]))
    o = jnp.concatenate(outs, axis=-1).astype(BF16)
    cross = _dot(o, wo_ref[...])
    h2 = _layer_norm(DEEPNORM_ALPHA * h1 + cross, g_ref[...], b_ref[...])
    h2_ref[...] = h2
    lg_ref[...] = _dot(h2.astype(BF16), wr_ref[...]) + br_ref[...]


def _cross_attention(h1, wq_b, k, v, wo_b, g, b, w_route, b_route):
    s = h1.shape[0]
    tm = min(s, 512)
    const = lambda shape: pl.BlockSpec(shape, lambda i: (0, 0), pipeline_mode=pl.Buffered(1))
    return pl.pallas_call(
        _cross_kernel,
        out_shape=(jax.ShapeDtypeStruct((s, D_MODEL), F32),
                   jax.ShapeDtypeStruct((s, LANES), F32)),
        grid=(s // tm,),
        in_specs=[pl.BlockSpec((tm, D_MODEL), lambda i: (i, 0)),
                  const((D_MODEL, D_MODEL)),
                  const((MEM_LEN, D_MODEL)),
                  const((MEM_LEN, D_MODEL)),
                  const((D_MODEL, D_MODEL)),
                  const((1, D_MODEL)),
                  const((1, D_MODEL)),
                  const((D_MODEL, LANES)),
                  const((1, LANES))],
        out_specs=(pl.BlockSpec((tm, D_MODEL), lambda i: (i, 0)),
                   pl.BlockSpec((tm, LANES), lambda i: (i, 0))),
        compiler_params=_cparams(("arbitrary",)),
        name="cross_attn_ln2",
    )(h1, wq_b, k, v, wo_b, g, b, w_route, b_route)


def _route_kernel(lg_ref, slot_ref, slot_t_ref, gate_ref, runs_ref, plan_ref, tot_row, tot_col, gstart):
    phase = pl.program_id(0)
    i = pl.program_id(1)
    tm = lg_ref.shape[0]
    neg = -jnp.inf

    logits = lg_ref[...]
    lane = lax.broadcasted_iota(jnp.int32, (tm, LANES), 1)
    gmask = lane < N_GROUPS
    gl = jnp.where(gmask, logits, neg)
    ge = jnp.exp(gl - jnp.max(gl, axis=-1, keepdims=True))
    pg = ge / jnp.sum(ge, axis=-1, keepdims=True)
    pg_sel = jnp.max(pg, axis=-1, keepdims=True)
    grp = jnp.min(jnp.where((pg == pg_sel) & gmask, lane, LANES), axis=-1, keepdims=True)

    fl_lane = lane - N_GROUPS
    fmask = (fl_lane >= 0) & (fl_lane < N_EXPERTS) & ((fl_lane >> 3) == grp)
    fl = jnp.where(fmask, logits, neg)
    fe = jnp.exp(fl - jnp.max(fl, axis=-1, keepdims=True))
    fp = fe / jnp.sum(fe, axis=-1, keepdims=True)
    p1 = jnp.max(fp, axis=-1, keepdims=True)
    i1 = jnp.min(jnp.where((fp == p1) & fmask, lane, LANES), axis=-1, keepdims=True)
    rest = fmask & (lane != i1)
    fp2 = jnp.where(rest, fp, -1.0)
    p2 = jnp.max(fp2, axis=-1, keepdims=True)
    i2 = jnp.min(jnp.where((fp2 == p2) & rest, lane, LANES), axis=-1, keepdims=True)
    psum = p1 + p2
    gate1 = pg_sel * p1 / psum
    gate2 = pg_sel * p2 / psum

    oh1 = lane == (i1 - N_GROUPS)
    oh2 = lane == (i2 - N_GROUPS)
    oh = (jnp.where(oh1, 1.0, 0.0) + jnp.where(oh2, 1.0, 0.0)).astype(BF16)
    ones = jnp.ones((tm, LANES), BF16)

    def align_up(v, a):
        return jnp.floor((v + (a - 1.0)) * (1.0 / a)) * a

    run_row = align_up(_dot_tn(ones, oh)[0:8, :], RUN_ALIGN)
    rr = lax.broadcasted_iota(jnp.int32, (LANES, LANES), 0)
    cc = lax.broadcasted_iota(jnp.int32, (LANES, LANES), 1)
    strict_upper = jnp.where(rr < cc, 1.0, 0.0).astype(BF16)

    @pl.when((phase == 0) & (i == 0))
    def _():
        tot_row[...] = jnp.zeros_like(tot_row)
        tot_col[...] = jnp.zeros_like(tot_col)

    @pl.when(phase == 0)
    def _():
        tot_row[...] += run_row
        tot_col[...] += align_up(_dot_tn(oh, ones), RUN_ALIGN)

    @pl.when((phase == 1) & (i == 0))
    def _():
        nblk_row = align_up(tot_row[...], MOE_BLK) * (1.0 / MOE_BLK)
        nblk_col = align_up(tot_col[...], MOE_BLK) * (1.0 / MOE_BLK)
        lower_incl = jnp.where(cc <= rr, 1.0, 0.0).astype(BF16)
        base = _dot(nblk_row.astype(BF16), strict_upper) * float(MOE_BLK)
        ends = _dot(lower_incl, nblk_col.astype(BF16))
        expert_rows = rr < N_EXPERTS
        be = jnp.sum(jnp.where(expert_rows & (ends <= cc.astype(F32)), 1.0, 0.0), axis=0, keepdims=True)
        be = jnp.minimum(be, N_EXPERTS - 1.0)
        total = jnp.sum(jnp.where(expert_rows, nblk_col, 0.0), axis=0, keepdims=True)
        sub = lax.broadcasted_iota(jnp.int32, plan_ref.shape, 0)
        plan = jnp.where(sub == 0, jnp.broadcast_to(be, plan_ref.shape),
                         jnp.where(sub == 1, jnp.broadcast_to(total, plan_ref.shape),
                                   jnp.where(sub == 2, base + (nblk_row - 1.0) * float(MOE_BLK), nblk_row)))
        plan_ref[...] = plan.astype(jnp.int32)
        gstart[...] = base

    @pl.when(phase == 1)
    def _():
        off = _dot((run_row * (1.0 / RUN_ALIGN)).astype(BF16), strict_upper) * float(RUN_ALIGN)
        tr = lax.broadcasted_iota(jnp.int32, (tm, tm), 0)
        tc = lax.broadcasted_iota(jnp.int32, (tm, tm), 1)
        strict_lower = jnp.where(tc < tr, 1.0, 0.0).astype(BF16)
        slot = _dot(strict_lower, oh) + off[0:1, :]
        s1 = jnp.sum(jnp.where(oh1, slot, 0.0), axis=-1, keepdims=True)
        s2 = jnp.sum(jnp.where(oh2, slot, 0.0), axis=-1, keepdims=True)
        slots = jnp.where(lane == 0, s1, jnp.where(lane == 1, s2, 0.0))
        slot_ref[...] = slots.astype(jnp.int32)
        slot_t_ref[...] = jnp.transpose(slots)[0:8, :].astype(jnp.int32)
        gate_ref[...] = jnp.where(lane == 0, gate1, jnp.where(lane == 1, gate2, 0.0))
        sub = lax.broadcasted_iota(jnp.int32, runs_ref.shape, 0)
        runs = jnp.where(sub == 0, gstart[...], jnp.where(sub == 1, off, run_row))
        runs_ref[...] = runs.astype(jnp.int32)
        gstart[...] += run_row


def _route(logits):
    t = logits.shape[0]
    tm = min(t, TILE_TOK)
    nt = t // tm
    step = lambda p, i: (i * p, 0)
    return pl.pallas_call(
        _route_kernel,
        out_shape=(jax.ShapeDtypeStruct((t, LANES), jnp.int32),
                   jax.ShapeDtypeStruct((nt * 8, tm), jnp.int32),
                   jax.ShapeDtypeStruct((t, LANES), F32),
                   jax.ShapeDtypeStruct((nt * 8, LANES), jnp.int32),
                   jax.ShapeDtypeStruct((8, LANES), jnp.int32)),
        grid=(2, nt),
        in_specs=[pl.BlockSpec((tm, LANES), lambda p, i: (i, 0))],
        out_specs=(pl.BlockSpec((tm, LANES), step),
                   pl.BlockSpec((8, tm), step),
                   pl.BlockSpec((tm, LANES), step),
                   pl.BlockSpec((8, LANES), step),
                   pl.BlockSpec((8, LANES), lambda p, i: (0, 0))),
        scratch_shapes=[pltpu.VMEM((8, LANES), F32),
                        pltpu.VMEM((LANES, LANES), F32),
                        pltpu.VMEM((8, LANES), F32)],
        compiler_params=_cparams(("arbitrary", "arbitrary")),
        name="moe_route",
    )(logits)


def _run_copies(rg_ref, ro_ref, rn_ref, tile, make_copy):
    def each(action):
        def body(e, carry):
            j = tile * N_EXPERTS + e
            n = rn_ref[j]

            @pl.when(n > 0)
            def _():
                action(make_copy(pl.multiple_of(rg_ref[j], RUN_ALIGN), pl.multiple_of(ro_ref[j], RUN_ALIGN),
                                 pl.multiple_of(n, RUN_ALIGN)))
            return carry
        lax.fori_loop(0, N_EXPERTS, body, 0)
    return each


def _dispatch_kernel(rg_ref, ro_ref, rn_ref, zrow_ref, nblk_ref, nu_ref, x_ref, slot_t_ref, xb_ref,
                     sorted_ref, zero_ref, sems, zsem):
    i = pl.program_id(0)
    n = pl.num_programs(0)
    tm = x_ref.shape[0]
    nb = xb_ref.shape[0] // MOE_BLK
    buf = i % 2

    def zero_copy(row):
        return pltpu.make_async_copy(
            zero_ref, xb_ref.at[pl.ds(pl.multiple_of(row, MOE_BLK), MOE_BLK), :], zsem)

    def runs_of(tile, slot):
        def make_copy(g, o, size):
            return pltpu.make_async_copy(sorted_ref.at[slot, pl.ds(o, size), :],
                                         xb_ref.at[pl.ds(g, size), :], sems.at[slot])
        return _run_copies(rg_ref, ro_ref, rn_ref, tile, make_copy)

    @pl.when(i == 0)
    def _():
        zero_ref[...] = jnp.zeros_like(zero_ref)

        def issue_zero(e, carry):
            @pl.when(nblk_ref[e] > 0)
            def _():
                zero_copy(zrow_ref[e]).start()
            return carry
        lax.fori_loop(0, N_EXPERTS, issue_zero, 0)
        lax.fori_loop(nu_ref[0], nb, lambda b, c: (zero_copy(b * MOE_BLK).start(), c)[1], 0)

        def wait_zero(e, carry):
            @pl.when(nblk_ref[e] > 0)
            def _():
                zero_copy(0).wait()
            return carry
        lax.fori_loop(0, N_EXPERTS, wait_zero, 0)
        lax.fori_loop(nu_ref[0], nb, lambda b, c: (zero_copy(0).wait(), c)[1], 0)

    @pl.when(i >= 2)
    def _():
        runs_of(i - 2, buf)(lambda c: c.wait())

    x = x_ref[...].astype(BF16)
    s1 = slot_t_ref[0:1, :]
    s2 = slot_t_ref[1:2, :]
    for rc in range(TILE_SLOTS // SORT_ROWS):
        r = lax.broadcasted_iota(jnp.int32, (SORT_ROWS, tm), 0) + rc * SORT_ROWS
        perm = jnp.where((r == s1) | (r == s2), 1.0, 0.0).astype(BF16)
        sorted_ref[buf, rc * SORT_ROWS:(rc + 1) * SORT_ROWS, :] = _dot(perm, x).astype(BF16)

    runs_of(i, buf)(lambda c: c.start())

    @pl.when(i == n - 1)
    def _():
        @pl.when(i >= 1)
        def _():
            runs_of(i - 1, 1 - buf)(lambda c: c.wait())
        runs_of(i, buf)(lambda c: c.wait())


def _dispatch(rg, ro, rn, zrow, nblk, n_used, h2, slot_t, n_rows):
    t = h2.shape[0]
    tm = min(t, TILE_TOK)
    grid_spec = pltpu.PrefetchScalarGridSpec(
        num_scalar_prefetch=6,
        grid=(t // tm,),
        in_specs=[pl.BlockSpec((tm, D_MODEL), lambda i, *_: (i, 0)),
                  pl.BlockSpec((8, tm), lambda i, *_: (i, 0))],
        out_specs=pl.BlockSpec(memory_space=pl.ANY),
        scratch_shapes=[pltpu.VMEM((2, TILE_SLOTS, D_MODEL), BF16),
                        pltpu.VMEM((MOE_BLK, D_MODEL), BF16),
                        pltpu.SemaphoreType.DMA((2,)),
                        pltpu.SemaphoreType.DMA(())],
    )
    return pl.pallas_call(
        _dispatch_kernel,
        out_shape=jax.ShapeDtypeStruct((n_rows, D_MODEL), BF16),
        grid_spec=grid_spec,
        compiler_params=_cparams(("arbitrary",)),
        name="moe_dispatch",
    )(rg, ro, rn, zrow, nblk, n_used, h2, slot_t)


def _expert_kernel(be_ref, nu_ref, nblk_ref, x_ref, wg_ref, wu_ref, wd_ref, y_ref,
                   wg_f, wu_f, wd_f, wg_b, wu_b, wd_b, sems, cur_ref):
    b = pl.program_id(0)
    n_used = nu_ref[0]
    used = b < n_used
    e = be_ref[b]
    prev = be_ref[jnp.maximum(b - 1, 0)]
    fresh = used & ((b == 0) | (e != prev))

    def weight_copies(expert, slot):
        return [pltpu.make_async_copy(src.at[expert], dst.at[slot], sems.at[slot])
                for src, dst in ((wg_ref, wg_f), (wu_ref, wu_f), (wd_ref, wd_f))]

    @pl.when(b == 0)
    def _():
        cur_ref[0] = 0
        for c in weight_copies(e, 0):
            c.start()

    @pl.when(fresh)
    def _():
        slot = cur_ref[0]
        for c in weight_copies(e, slot):
            c.wait()
        wg_b[...] = wg_f[slot].astype(BF16)
        wu_b[...] = wu_f[slot].astype(BF16)
        wd_b[...] = wd_f[slot].astype(BF16)
        nxt = b + nblk_ref[e]

        @pl.when(nxt < n_used)
        def _():
            for c in weight_copies(be_ref[nxt], 1 - slot):
                c.start()
        cur_ref[0] = 1 - slot

    @pl.when(used)
    def _():
        x = x_ref[...]
        hid = _silu(_dot(x, wg_b[...])) * _dot(x, wu_b[...])
        y_ref[...] = _dot(hid.astype(BF16), wd_b[...]).astype(y_ref.dtype)

    @pl.when(jnp.logical_not(used))
    def _():
        y_ref[...] = jnp.zeros_like(y_ref)


def _experts(block_e, n_used, nblk, xb, w_gate, w_up, w_down):
    nb = xb.shape[0] // MOE_BLK
    any_space = pl.BlockSpec(memory_space=pl.ANY)
    grid_spec = pltpu.PrefetchScalarGridSpec(
        num_scalar_prefetch=3,
        grid=(nb,),
        in_specs=[pl.BlockSpec((MOE_BLK, D_MODEL), lambda b, be, nu, nk: (jnp.minimum(b, nu[0] - 1), 0)),
                  any_space, any_space, any_space],
        out_specs=pl.BlockSpec((MOE_BLK, D_MODEL), lambda b, be, nu, nk: (b, 0)),
        scratch_shapes=[pltpu.VMEM((2, D_MODEL, EXPERT_FF), F32),
                        pltpu.VMEM((2, D_MODEL, EXPERT_FF), F32),
                        pltpu.VMEM((2, EXPERT_FF, D_MODEL), F32),
                        pltpu.VMEM((D_MODEL, EXPERT_FF), BF16),
                        pltpu.VMEM((D_MODEL, EXPERT_FF), BF16),
                        pltpu.VMEM((EXPERT_FF, D_MODEL), BF16),
                        pltpu.SemaphoreType.DMA((2,)),
                        pltpu.SMEM((1,), jnp.int32)],
    )
    return pl.pallas_call(
        _expert_kernel,
        out_shape=jax.ShapeDtypeStruct(xb.shape, BF16),
        grid_spec=grid_spec,
        compiler_params=_cparams(("arbitrary",)),
        name="moe_experts",
    )(block_e, n_used, nblk, xb, w_gate, w_up, w_down)


def _combine_kernel(rg_ref, ro_ref, rn_ref, yb_ref, h2_ref, slot_ref, gate_ref, g_ref, b_ref, o_ref,
                    ybuf, sems):
    i = pl.program_id(0)
    n = pl.num_programs(0)
    tm = h2_ref.shape[0]
    buf = i % 2

    def runs_of(tile, slot):
        def make_copy(g, o, size):
            return pltpu.make_async_copy(yb_ref.at[pl.ds(g, size), :],
                                         ybuf.at[slot, pl.ds(o, size), :], sems.at[slot])
        return _run_copies(rg_ref, ro_ref, rn_ref, tile, make_copy)

    @pl.when(i == 0)
    def _():
        ybuf[...] = jnp.zeros_like(ybuf)
        runs_of(0, 0)(lambda c: c.start())

    @pl.when(i + 1 < n)
    def _():
        runs_of(i + 1, 1 - buf)(lambda c: c.start())

    runs_of(i, buf)(lambda c: c.wait())

    lane = lax.broadcasted_iota(jnp.int32, (tm, TILE_SLOTS), 1)
    slots = slot_ref[...]
    gate = gate_ref[...]
    y = ybuf[buf]
    sel = jnp.where(lane == slots[:, 0:1], gate[:, 0:1],
                    jnp.where(lane == slots[:, 1:2], gate[:, 1:2], 0.0)).astype(BF16)
    ffn = _dot(sel, y)
    o_ref[...] = _layer_norm(DEEPNORM_ALPHA * h2_ref[...] + ffn, g_ref[...], b_ref[...])


def _combine_ln(rg, ro, rn, yb, h2, slots, gate, g, b):
    t = h2.shape[0]
    tm = min(t, TILE_TOK)
    grid_spec = pltpu.PrefetchScalarGridSpec(
        num_scalar_prefetch=3,
        grid=(t // tm,),
        in_specs=[pl.BlockSpec(memory_space=pl.ANY),
                  pl.BlockSpec((tm, D_MODEL), lambda i, *_: (i, 0)),
                  pl.BlockSpec((tm, LANES), lambda i, *_: (i, 0)),
                  pl.BlockSpec((tm, LANES), lambda i, *_: (i, 0)),
                  pl.BlockSpec((1, D_MODEL), lambda i, *_: (0, 0)),
                  pl.BlockSpec((1, D_MODEL), lambda i, *_: (0, 0))],
        out_specs=pl.BlockSpec((tm, D_MODEL), lambda i, *_: (i, 0)),
        scratch_shapes=[pltpu.VMEM((2, TILE_SLOTS, D_MODEL), BF16),
                        pltpu.SemaphoreType.DMA((2,))],
    )
    return pl.pallas_call(
        _combine_kernel,
        out_shape=jax.ShapeDtypeStruct((t, D_MODEL), F32),
        grid_spec=grid_spec,
        compiler_params=_cparams(("arbitrary",)),
        name="moe_combine_ln3",
    )(rg, ro, rn, yb, h2, slots, gate, g, b)


def _mixer(x2d, positions, w_in, w_gla_a2, b_gla_a, g_gla_norm):
    s = x2d.shape[0]
    w_in_b = w_in.astype(BF16)
    half = RET_DK // 2
    inv_freq = (ROPE_BASE ** (-jnp.arange(half, dtype=F32) / half)).reshape(1, half)
    cos, sin = _rope_table(positions.reshape(s, 1), inv_freq)
    w_lr = jnp.pad(w_in_b[:, GLR_OFF:], ((0, 0), (0, LANES - GLA_LOWRANK)))
    w_a2 = jnp.pad(w_gla_a2.astype(BF16), ((0, LANES - GLA_LOWRANK), (0, 0)))
    h_main, log_a = _proj_in(x2d, w_in_b, w_lr, w_a2, b_gla_a.reshape(1, -1))
    log_gamma = jnp.log1p(-jnp.exp2(-5.0 - jnp.arange(RET_HEADS, dtype=F32)))
    ret = _retention(h_main, cos, sin, log_gamma)
    gla = _gla(h_main, log_a, g_gla_norm.reshape(1, -1))
    return ret, gla


def _moe(h2, logits, w_gate, w_up, w_down, g, b):
    t = h2.shape[0]
    nt = t // min(t, TILE_TOK)
    slots, slot_t, gate, runs, plan = _route(logits)
    runs = runs.reshape(nt, 8, LANES)[:, :3, :N_EXPERTS]
    rg, ro, rn = (runs[:, j, :].reshape(-1) for j in range(3))
    max_rows = 2 * t + nt * N_EXPERTS * (RUN_ALIGN - 1) + N_EXPERTS * (MOE_BLK - 1)
    nb = -(-max_rows // MOE_BLK)
    block_e, n_used = plan[0, :nb], plan[1, :1]
    last_blk_row, nblk = plan[2, :N_EXPERTS], plan[3, :N_EXPERTS]
    xb = _dispatch(rg, ro, rn, last_blk_row, nblk, n_used, h2, slot_t, nb * MOE_BLK)
    yb = _experts(block_e, n_used, nblk, xb, w_gate, w_up, w_down)
    return _combine_ln(rg, ro, rn, yb, h2, slots, gate, g, b)


def kernel(x, mem, positions, w_in, w_gla_a2, b_gla_a, g_gla_norm, w_mix_out, ln1_g, ln1_b, w_mq, w_mk, w_mv, w_mo, ln2_g, ln2_b, w_route_group, b_route_group, w_route_expert, b_route_expert, w_exp_gate, w_exp_up, w_exp_down, ln3_g, ln3_b):
    bsz, s, d = x.shape
    assert bsz == 1 and d == D_MODEL
    x2d = x.reshape(s, d)
    row = lambda v: v.reshape(1, -1)

    ret, gla = _mixer(x2d, positions, w_in[0], w_gla_a2[0], b_gla_a[0], g_gla_norm[0])
    h1 = _mixout_ln(ret, gla, x2d, w_mix_out[0], row(ln1_g[0]), row(ln1_b[0]))

    k, v = _mem_kv(mem[0], w_mk[0], w_mv[0])
    n_route = N_GROUPS + N_EXPERTS
    w_route = jnp.pad(jnp.concatenate([w_route_group[0], w_route_expert[0]], axis=1).astype(BF16),
                      ((0, 0), (0, LANES - n_route)))
    b_route = jnp.pad(jnp.concatenate([b_route_group[0], b_route_expert[0].reshape(-1)]),
                      (0, LANES - n_route)).reshape(1, LANES)
    h2, logits = _cross_attention(h1, w_mq[0].astype(BF16), k, v, w_mo[0].astype(BF16),
                                  row(ln2_g[0]), row(ln2_b[0]), w_route, b_route)

    out = _moe(h2, logits, w_exp_gate[0], w_exp_up[0], w_exp_down[0],
               row(ln3_g[0]), row(ln3_b[0]))
    return out.reshape(bsz, s, d)
```

```python
import math

import jax
import jax.numpy as jnp
from jax import lax
from jax.experimental import pallas as pl
from jax.experimental.pallas import tpu as pltpu

F32 = jnp.float32
BF16 = jnp.bfloat16

D_MODEL = 2048
MEM_LEN = 256
RET_HEADS = 4
RET_DK = 256
RET_DV = 256
GLA_HEADS = 4
GLA_DK = 128
GLA_DV = 256
GLA_LOWRANK = 16
GLA_TAU = 16.0
ROPE_BASE = 10000.0
MEM_HEADS = 4
MEM_HEAD_DIM = D_MODEL // MEM_HEADS
N_GROUPS = 4
EXPERTS_PER_GROUP = 8
N_EXPERTS = N_GROUPS * EXPERTS_PER_GROUP
EXPERT_FF = 512
LN_EPS = 1e-5
DEPTH = 1
DEEPNORM_ALPHA = (2 * DEPTH) ** 0.25

RQ_OFF, RK_OFF, RV_OFF, RG_OFF = 0, 1024, 2048, 3072
GQ_OFF, GK_OFF, GV_OFF, GG_OFF, GLR_OFF = 4096, 4608, 5120, 6144, 7168
IN_MAIN = 7168

LANES = 128
RET_CHUNK = 256
GLA_CHUNK = 128
GLA_LEVELS = 7
MOE_BLK = 256
TILE_TOK = 512
RUN_ALIGN = 16
TILE_SLOTS = 2 * TILE_TOK + N_EXPERTS * RUN_ALIGN
SORT_ROWS = 256
VMEM_LIMIT = 56 * 1024 * 1024


def _cparams(sem):
    return pltpu.CompilerParams(dimension_semantics=sem, vmem_limit_bytes=VMEM_LIMIT)


def _layer_norm(y, g, b):
    mu = jnp.mean(y, axis=-1, keepdims=True)
    d = y - mu
    var = jnp.mean(d * d, axis=-1, keepdims=True)
    return d * lax.rsqrt(var + LN_EPS) * g + b


def _silu(x):
    return x / (1.0 + jnp.exp(-x))


def _dot(a, b):
    return jnp.dot(a, b, preferred_element_type=F32)


def _dot_nt(a, b):
    return lax.dot_general(a, b, (((1,), (1,)), ((), ())), preferred_element_type=F32)


def _dot_tn(a, b):
    return lax.dot_general(a, b, (((0,), (0,)), ((), ())), preferred_element_type=F32)


def _rope_kernel(pos_ref, invf_ref, cos_ref, sin_ref):
    ang = pos_ref[...].astype(F32) * invf_ref[...]
    cos_ref[...] = jnp.cos(ang)
    sin_ref[...] = jnp.sin(ang)


def _rope_table(pos_col, inv_freq):
    s = pos_col.shape[0]
    tm = min(s, 1024)
    half = inv_freq.shape[1]
    return pl.pallas_call(
        _rope_kernel,
        out_shape=(jax.ShapeDtypeStruct((s, half), F32), jax.ShapeDtypeStruct((s, half), F32)),
        grid=(s // tm,),
        in_specs=[pl.BlockSpec((tm, 1), lambda i: (i, 0)),
                  pl.BlockSpec((1, half), lambda i: (0, 0))],
        out_specs=(pl.BlockSpec((tm, half), lambda i: (i, 0)),
                   pl.BlockSpec((tm, half), lambda i: (i, 0))),
        compiler_params=_cparams(("arbitrary",)),
        name="rope_table",
    )(pos_col, inv_freq)


def _proj_in_kernel(x_ref, w_ref, wlr_ref, wa2_ref, ba_ref, o_ref, la_ref, xb_ref):
    @pl.when(pl.program_id(1) == 0)
    def _():
        xb_ref[...] = x_ref[...].astype(BF16)
        glr = _dot(xb_ref[...], wlr_ref[...])
        z = _dot(glr.astype(BF16), wa2_ref[...]) + ba_ref[...]
        la_ref[...] = (jnp.minimum(z, 0.0) - jnp.log1p(jnp.exp(-jnp.abs(z)))) / GLA_TAU

    o_ref[...] = _dot(xb_ref[...], w_ref[...])


def _proj_in(x2d, w_in_b, w_lr, w_a2, b_a):
    s = x2d.shape[0]
    tm = min(s, 1024)
    tn = 512
    n = GLA_HEADS * GLA_DK
    return pl.pallas_call(
        _proj_in_kernel,
        out_shape=(jax.ShapeDtypeStruct((s, IN_MAIN), F32), jax.ShapeDtypeStruct((s, n), F32)),
        grid=(s // tm, IN_MAIN // tn),
        in_specs=[pl.BlockSpec((tm, D_MODEL), lambda i, j: (i, 0)),
                  pl.BlockSpec((D_MODEL, tn), lambda i, j: (0, j)),
                  pl.BlockSpec((D_MODEL, LANES), lambda i, j: (0, 0)),
                  pl.BlockSpec((LANES, n), lambda i, j: (0, 0)),
                  pl.BlockSpec((1, n), lambda i, j: (0, 0))],
        out_specs=(pl.BlockSpec((tm, tn), lambda i, j: (i, j)),
                   pl.BlockSpec((tm, n), lambda i, j: (i, 0))),
        scratch_shapes=[pltpu.VMEM((tm, D_MODEL), BF16)],
        compiler_params=_cparams(("arbitrary", "arbitrary")),
        name="proj_in",
    )(x2d, w_in_b, w_lr, w_a2, b_a)


def _rotary(t, cos, sin):
    half = t.shape[-1] // 2
    t1, t2 = t[:, :half], t[:, half:]
    return jnp.concatenate([t1 * cos - t2 * sin, t1 * sin + t2 * cos], axis=-1)


def _retention_kernel(lg_ref, q_ref, k_ref, v_ref, g_ref, cos_ref, sin_ref, o_ref, state_ref, intra_ref, dq_ref, dk_ref):
    c = pl.program_id(0)
    C = q_ref.shape[0]

    @pl.when(c == 0)
    def _():
        state_ref[...] = jnp.zeros_like(state_ref)
        ri = lax.broadcasted_iota(jnp.int32, (C, C), 0)
        ci = lax.broadcasted_iota(jnp.int32, (C, C), 1)
        rel = jnp.maximum(ri - ci, 0).astype(F32)
        n = lax.broadcasted_iota(jnp.int32, (C, RET_DK), 0).astype(F32)
        for h in range(RET_HEADS):
            lg = lg_ref[h]
            intra_ref[h] = jnp.where(ri >= ci, jnp.exp(lg * rel), 0.0)
            dq_ref[h] = jnp.exp(lg * (n + 1.0))
            dk_ref[h] = jnp.exp(lg * (C - 1.0 - n))

    cos = cos_ref[...]
    sin = sin_ref[...]

    for h in range(RET_HEADS):
        sl = slice(h * RET_DK, (h + 1) * RET_DK)
        q = _rotary(q_ref[:, sl], cos, sin)
        k = _rotary(k_ref[:, sl], cos, sin) * (RET_DK ** -0.5)
        v = v_ref[:, sl].astype(BF16)
        decay_q = dq_ref[h]
        decay_k = dk_ref[h]
        decay_chunk = jnp.exp(lg_ref[h] * C)

        state = state_ref[h]
        scores = _dot_nt(q.astype(BF16), k.astype(BF16)) * intra_ref[h]
        o = _dot(scores.astype(BF16), v) + _dot((q * decay_q).astype(BF16), state.astype(BF16))
        state_ref[h] = decay_chunk * state + _dot_tn((k * decay_k).astype(BF16), v)

        mu = jnp.mean(o, axis=-1, keepdims=True)
        d = o - mu
        var = jnp.mean(d * d, axis=-1, keepdims=True)
        o = d * lax.rsqrt(var + LN_EPS)
        o_ref[:, sl] = (_silu(g_ref[:, sl]) * o).astype(o_ref.dtype)


def _retention(h_main, cos, sin, log_gamma):
    s = h_main.shape[0]
    C = min(RET_CHUNK, s)
    w = RET_HEADS * RET_DK
    col = lambda off: (lambda c, lg: (c, off // w))
    grid_spec = pltpu.PrefetchScalarGridSpec(
        num_scalar_prefetch=1,
        grid=(s // C,),
        in_specs=[pl.BlockSpec((C, w), col(RQ_OFF)),
                  pl.BlockSpec((C, w), col(RK_OFF)),
                  pl.BlockSpec((C, w), col(RV_OFF)),
                  pl.BlockSpec((C, w), col(RG_OFF)),
                  pl.BlockSpec((C, RET_DK // 2), lambda c, lg: (c, 0)),
                  pl.BlockSpec((C, RET_DK // 2), lambda c, lg: (c, 0))],
        out_specs=pl.BlockSpec((C, RET_HEADS * RET_DV), lambda c, lg: (c, 0)),
        scratch_shapes=[pltpu.VMEM((RET_HEADS, RET_DK, RET_DV), F32),
                        pltpu.VMEM((RET_HEADS, C, C), F32),
                        pltpu.VMEM((RET_HEADS, C, RET_DK), F32),
                        pltpu.VMEM((RET_HEADS, C, RET_DK), F32)],
    )
    return pl.pallas_call(
        _retention_kernel,
        out_shape=jax.ShapeDtypeStruct((s, RET_HEADS * RET_DV), BF16),
        grid_spec=grid_spec,
        compiler_params=_cparams(("arbitrary",)),
        name="retention",
    )(log_gamma, h_main, h_main, h_main, h_main, cos, sin)


def _gla_decay_matrix(C):
    import numpy as np
    levels = int(math.log2(C))
    r = np.arange(C)[:, None]
    t = np.arange(C)[None, :]
    mats = []
    for l in range(levels):
        blk = C >> l
        half = blk // 2
        m = (r // blk) * blk + half - 1
        qside = (r % blk) >= half
        mats.append(np.where(qside, (t > m) & (t <= r), (t > r) & (t <= m)))
    mats.append(t <= r)
    mats.append(t > r)
    return np.concatenate(mats, axis=0).astype(np.float32)


def _gla_kernel(m_ref, q_ref, k_ref, v_ref, g_ref, la_ref, gn_ref, o_ref, state_ref):
    c = pl.program_id(0)
    C = q_ref.shape[0]
    levels = GLA_LEVELS

    @pl.when(c == 0)
    def _():
        state_ref[...] = jnp.zeros_like(state_ref)

    m = m_ref[...]
    ri = lax.broadcasted_iota(jnp.int32, (C, C), 0)
    ci = lax.broadcasted_iota(jnp.int32, (C, C), 1)
    xor = jnp.where(ri > ci, ri ^ ci, 0)
    row = lax.broadcasted_iota(jnp.int32, (C, 1), 0)

    heads = range(GLA_HEADS)
    ks = [slice(h * GLA_DK, (h + 1) * GLA_DK) for h in heads]
    vs = [slice(h * GLA_DV, (h + 1) * GLA_DV) for h in heads]

    expo = []
    for h in heads:
        la = la_ref[:, ks[h]]
        la_hi = la.astype(BF16)
        la_lo = (la - la_hi.astype(F32)).astype(BF16)
        expo.append(jnp.exp(_dot(m, la_hi) + _dot(m, la_lo)))

    q = [q_ref[:, ks[h]] * (GLA_DK ** -0.5) for h in heads]
    k = [k_ref[:, ks[h]] for h in heads]
    v = [v_ref[:, vs[h]].astype(BF16) for h in heads]

    scores = [jnp.where(ri == ci, _dot_nt(q[h].astype(BF16), k[h].astype(BF16)), 0.0) for h in heads]
    for l in range(levels):
        half = C >> (l + 1)
        shift = int(math.log2(half))
        qside = (row & half) != 0
        keep = (xor >> shift) == 1
        for h in heads:
            x = (jnp.where(qside, q[h], k[h]) * expo[h][l * C:(l + 1) * C]).astype(BF16)
            scores[h] = scores[h] + jnp.where(keep, _dot_nt(x, x), 0.0)

    outs = []
    for h in heads:
        e_b = expo[h][levels * C:(levels + 1) * C]
        e_rev = expo[h][(levels + 1) * C:(levels + 2) * C]
        e_last = e_b[C - 1:C, :]
        state = state_ref[h]
        outs.append(_dot(scores[h].astype(BF16), v[h])
                    + _dot_nt((q[h] * e_b).astype(BF16), state.astype(BF16)))
        state_ref[h] = state * e_last + _dot_tn(v[h], (k[h] * e_rev).astype(BF16))

    for h in heads:
        o = outs[h]
        o = o * lax.rsqrt(jnp.mean(o * o, axis=-1, keepdims=True) + LN_EPS) * gn_ref[...]
        o_ref[:, vs[h]] = (_silu(g_ref[:, vs[h]]) * o).astype(o_ref.dtype)


def _gla(h_main, log_a, g_norm):
    s = h_main.shape[0]
    C = GLA_CHUNK
    m = jnp.asarray(_gla_decay_matrix(C), dtype=BF16)
    nrow = m.shape[0]
    wk = GLA_HEADS * GLA_DK
    wv = GLA_HEADS * GLA_DV
    return pl.pallas_call(
        _gla_kernel,
        out_shape=jax.ShapeDtypeStruct((s, wv), BF16),
        grid=(s // C,),
        in_specs=[pl.BlockSpec((nrow, C), lambda c: (0, 0)),
                  pl.BlockSpec((C, wk), lambda c: (c, GQ_OFF // wk)),
                  pl.BlockSpec((C, wk), lambda c: (c, GK_OFF // wk)),
                  pl.BlockSpec((C, wv), lambda c: (c, GV_OFF // wv)),
                  pl.BlockSpec((C, wv), lambda c: (c, GG_OFF // wv)),
                  pl.BlockSpec((C, wk), lambda c: (c, 0)),
                  pl.BlockSpec((1, GLA_DV), lambda c: (0, 0))],
        out_specs=pl.BlockSpec((C, wv), lambda c: (c, 0)),
        scratch_shapes=[pltpu.VMEM((GLA_HEADS, GLA_DV, GLA_DK), F32)],
        compiler_params=_cparams(("arbitrary",)),
        name="gla",
    )(m, h_main, h_main, h_main, h_main, log_a, g_norm)


def _mixout_kernel(ret_ref, gla_ref, x_ref, w_ref, g_ref, b_ref, o_ref, wb_ref):
    nr = ret_ref.shape[1]

    @pl.when(pl.program_id(0) == 0)
    def _():
        wb_ref[...] = w_ref[...].astype(BF16)

    mix = _dot(ret_ref[...], wb_ref[:nr, :]) + _dot(gla_ref[...], wb_ref[nr:, :])
    o_ref[...] = _layer_norm(DEEPNORM_ALPHA * x_ref[...] + mix, g_ref[...], b_ref[...])


def _mixout_ln(ret, gla, x2d, w_b, g, b):
    s = x2d.shape[0]
    tm = min(s, 512)
    nr, ng = ret.shape[1], gla.shape[1]
    return pl.pallas_call(
        _mixout_kernel,
        out_shape=jax.ShapeDtypeStruct((s, D_MODEL), F32),
        grid=(s // tm,),
        in_specs=[pl.BlockSpec((tm, nr), lambda i: (i, 0)),
                  pl.BlockSpec((tm, ng), lambda i: (i, 0)),
                  pl.BlockSpec((tm, D_MODEL), lambda i: (i, 0)),
                  pl.BlockSpec((nr + ng, D_MODEL), lambda i: (0, 0), pipeline_mode=pl.Buffered(1)),
                  pl.BlockSpec((1, D_MODEL), lambda i: (0, 0)),
                  pl.BlockSpec((1, D_MODEL), lambda i: (0, 0))],
        out_specs=pl.BlockSpec((tm, D_MODEL), lambda i: (i, 0)),
        scratch_shapes=[pltpu.VMEM((nr + ng, D_MODEL), BF16)],
        compiler_params=_cparams(("arbitrary",)),
        name="mixout_ln1",
    )(ret, gla, x2d, w_b, g, b)


def _kv_kernel(mem_ref, wk_ref, wv_ref, k_ref, v_ref):
    m = mem_ref[...].astype(BF16)
    k_ref[...] = _dot(m, wk_ref[...].astype(BF16)).astype(k_ref.dtype)
    v_ref[...] = _dot(m, wv_ref[...].astype(BF16)).astype(v_ref.dtype)


def _mem_kv(mem_b, wk_b, wv_b):
    tn = 512
    return pl.pallas_call(
        _kv_kernel,
        out_shape=(jax.ShapeDtypeStruct((MEM_LEN, D_MODEL), BF16),
                   jax.ShapeDtypeStruct((MEM_LEN, D_MODEL), BF16)),
        grid=(D_MODEL // tn,),
        in_specs=[pl.BlockSpec((MEM_LEN, D_MODEL), lambda j: (0, 0)),
                  pl.BlockSpec((D_MODEL, tn), lambda j: (0, j)),
                  pl.BlockSpec((D_MODEL, tn), lambda j: (0, j))],
        out_specs=(pl.BlockSpec((MEM_LEN, tn), lambda j: (0, j)),
                   pl.BlockSpec((MEM_LEN, tn), lambda j: (0, j))),
        compiler_params=_cparams(("arbitrary",)),
        name="mem_kv",
    )(mem_b, wk_b, wv_b)


def _cross_kernel(h_ref, wq_ref, k_ref, v_ref, wo_ref, g_ref, b_ref, wr_ref, br_ref,
                  h2_ref, lg_ref):
    h1 = h_ref[...]
    q = _dot(h1.astype(BF16), wq_ref[...]).astype(BF16)
    outs = []
    for hd in range(MEM_HEADS):
        sl = slice(hd * MEM_HEAD_DIM, (hd + 1) * MEM_HEAD_DIM)
        s = _dot_nt(q[:, sl], k_ref[:, sl]) * (MEM_HEAD_DIM ** -0.5)
        s = s - jnp.max(s, axis=-1, keepdims=True)
        p = jnp.exp(s)
        p = p / jnp.sum(p, axis=-1, keepdims=True)
        outs.append(_dot(p.astype(BF16), v_ref[:, sl]))
    o = jnp.concatenate(outs, axis=-1).astype(BF16)
    cross = _dot(o, wo_ref[...])
    h2 = _layer_norm(DEEPNORM_ALPHA * h1 + cross, g_ref[...], b_ref[...])
    h2_ref[...] = h2
    lg_ref[...] = _dot(h2.astype(BF16), wr_ref[...]) + br_ref[...]


def _cross_attention(h1, wq_b, k, v, wo_b, g, b, w_route, b_route):
    s = h1.shape[0]
    tm = min(s, 512)
    const = lambda shape: pl.BlockSpec(shape, lambda i: (0, 0), pipeline_mode=pl.Buffered(1))
    return pl.pallas_call(
        _cross_kernel,
        out_shape=(jax.ShapeDtypeStruct((s, D_MODEL), F32),
                   jax.ShapeDtypeStruct((s, LANES), F32)),
        grid=(s // tm,),
        in_specs=[pl.BlockSpec((tm, D_MODEL), lambda i: (i, 0)),
                  const((D_MODEL, D_MODEL)),
                  const((MEM_LEN, D_MODEL)),
                  const((MEM_LEN, D_MODEL)),
                  const((D_MODEL, D_MODEL)),
                  const((1, D_MODEL)),
                  const((1, D_MODEL)),
                  const((D_MODEL, LANES)),
                  const((1, LANES))],
        out_specs=(pl.BlockSpec((tm, D_MODEL), lambda i: (i, 0)),
                   pl.BlockSpec((tm, LANES), lambda i: (i, 0))),
        compiler_params=_cparams(("arbitrary",)),
        name="cross_attn_ln2",
    )(h1, wq_b, k, v, wo_b, g, b, w_route, b_route)


def _route_kernel(lg_ref, slot_ref, slot_t_ref, gate_ref, runs_ref, plan_ref, tot_row, tot_col, gstart):
    phase = pl.program_id(0)
    i = pl.program_id(1)
    tm = lg_ref.shape[0]
    neg = -jnp.inf

    logits = lg_ref[...]
    lane = lax.broadcasted_iota(jnp.int32, (tm, LANES), 1)
    gmask = lane < N_GROUPS
    gl = jnp.where(gmask, logits, neg)
    ge = jnp.exp(gl - jnp.max(gl, axis=-1, keepdims=True))
    pg = ge / jnp.sum(ge, axis=-1, keepdims=True)
    pg_sel = jnp.max(pg, axis=-1, keepdims=True)
    grp = jnp.min(jnp.where((pg == pg_sel) & gmask, lane, LANES), axis=-1, keepdims=True)

    fl_lane = lane - N_GROUPS
    fmask = (fl_lane >= 0) & (fl_lane < N_EXPERTS) & ((fl_lane >> 3) == grp)
    fl = jnp.where(fmask, logits, neg)
    fe = jnp.exp(fl - jnp.max(fl, axis=-1, keepdims=True))
    fp = fe / jnp.sum(fe, axis=-1, keepdims=True)
    p1 = jnp.max(fp, axis=-1, keepdims=True)
    i1 = jnp.min(jnp.where((fp == p1) & fmask, lane, LANES), axis=-1, keepdims=True)
    rest = fmask & (lane != i1)
    fp2 = jnp.where(rest, fp, -1.0)
    p2 = jnp.max(fp2, axis=-1, keepdims=True)
    i2 = jnp.min(jnp.where((fp2 == p2) & rest, lane, LANES), axis=-1, keepdims=True)
    psum = p1 + p2
    gate1 = pg_sel * p1 / psum
    gate2 = pg_sel * p2 / psum

    oh1 = lane == (i1 - N_GROUPS)
    oh2 = lane == (i2 - N_GROUPS)
    oh = (jnp.where(oh1, 1.0, 0.0) + jnp.where(oh2, 1.0, 0.0)).astype(BF16)
    ones = jnp.ones((tm, LANES), BF16)

    def align_up(v, a):
        return jnp.floor((v + (a - 1.0)) * (1.0 / a)) * a

    run_row = align_up(_dot_tn(ones, oh)[0:8, :], RUN_ALIGN)
    rr = lax.broadcasted_iota(jnp.int32, (LANES, LANES), 0)
    cc = lax.broadcasted_iota(jnp.int32, (LANES, LANES), 1)
    strict_upper = jnp.where(rr < cc, 1.0, 0.0).astype(BF16)

    @pl.when((phase == 0) & (i == 0))
    def _():
        tot_row[...] = jnp.zeros_like(tot_row)
        tot_col[...] = jnp.zeros_like(tot_col)

    @pl.when(phase == 0)
    def _():
        tot_row[...] += run_row
        tot_col[...] += align_up(_dot_tn(oh, ones), RUN_ALIGN)

    @pl.when((phase == 1) & (i == 0))
    def _():
        nblk_row = align_up(tot_row[...], MOE_BLK) * (1.0 / MOE_BLK)
        nblk_col = align_up(tot_col[...], MOE_BLK) * (1.0 / MOE_BLK)
        lower_incl = jnp.where(cc <= rr, 1.0, 0.0).astype(BF16)
        base = _dot(nblk_row.astype(BF16), strict_upper) * float(MOE_BLK)
        ends = _dot(lower_incl, nblk_col.astype(BF16))
        expert_rows = rr < N_EXPERTS
        be = jnp.sum(jnp.where(expert_rows & (ends <= cc.astype(F32)), 1.0, 0.0), axis=0, keepdims=True)
        be = jnp.minimum(be, N_EXPERTS - 1.0)
        total = jnp.sum(jnp.where(expert_rows, nblk_col, 0.0), axis=0, keepdims=True)
        sub = lax.broadcasted_iota(jnp.int32, plan_ref.shape, 0)
        plan = jnp.where(sub == 0, jnp.broadcast_to(be, plan_ref.shape),
                         jnp.where(sub == 1, jnp.broadcast_to(total, plan_ref.shape),
                                   jnp.where(sub == 2, base + (nblk_row - 1.0) * float(MOE_BLK), nblk_row)))
        plan_ref[...] = plan.astype(jnp.int32)
        gstart[...] = base

    @pl.when(phase == 1)
    def _():
        off = _dot((run_row * (1.0 / RUN_ALIGN)).astype(BF16), strict_upper) * float(RUN_ALIGN)
        tr = lax.broadcasted_iota(jnp.int32, (tm, tm), 0)
        tc = lax.broadcasted_iota(jnp.int32, (tm, tm), 1)
        strict_lower = jnp.where(tc < tr, 1.0, 0.0).astype(BF16)
        slot = _dot(strict_lower, oh) + off[0:1, :]
        s1 = jnp.sum(jnp.where(oh1, slot, 0.0), axis=-1, keepdims=True)
        s2 = jnp.sum(jnp.where(oh2, slot, 0.0), axis=-1, keepdims=True)
        slots = jnp.where(lane == 0, s1, jnp.where(lane == 1, s2, 0.0))
        slot_ref[...] = slots.astype(jnp.int32)
        slot_t_ref[...] = jnp.transpose(slots)[0:8, :].astype(jnp.int32)
        gate_ref[...] = jnp.where(lane == 0, gate1, jnp.where(lane == 1, gate2, 0.0))
        sub = lax.broadcasted_iota(jnp.int32, runs_ref.shape, 0)
        runs = jnp.where(sub == 0, gstart[...], jnp.where(sub == 1, off, run_row))
        runs_ref[...] = runs.astype(jnp.int32)
        gstart[...] += run_row


def _route(logits):
    t = logits.shape[0]
    tm = min(t, TILE_TOK)
    nt = t // tm
    step = lambda p, i: (i * p, 0)
    return pl.pallas_call(
        _route_kernel,
        out_shape=(jax.ShapeDtypeStruct((t, LANES), jnp.int32),
                   jax.ShapeDtypeStruct((nt * 8, tm), jnp.int32),
                   jax.ShapeDtypeStruct((t, LANES), F32),
                   jax.ShapeDtypeStruct((nt * 8, LANES), jnp.int32),
                   jax.ShapeDtypeStruct((8, LANES), jnp.int32)),
        grid=(2, nt),
        in_specs=[pl.BlockSpec((tm, LANES), lambda p, i: (i, 0))],
        out_specs=(pl.BlockSpec((tm, LANES), step),
                   pl.BlockSpec((8, tm), step),
                   pl.BlockSpec((tm, LANES), step),
                   pl.BlockSpec((8, LANES), step),
                   pl.BlockSpec((8, LANES), lambda p, i: (0, 0))),
        scratch_shapes=[pltpu.VMEM((8, LANES), F32),
                        pltpu.VMEM((LANES, LANES), F32),
                        pltpu.VMEM((8, LANES), F32)],
        compiler_params=_cparams(("arbitrary", "arbitrary")),
        name="moe_route",
    )(logits)


def _run_copies(rg_ref, ro_ref, rn_ref, tile, make_copy):
    def each(action):
        def body(e, carry):
            j = tile * N_EXPERTS + e
            n = rn_ref[j]

            @pl.when(n > 0)
            def _():
                action(make_copy(pl.multiple_of(rg_ref[j], RUN_ALIGN), pl.multiple_of(ro_ref[j], RUN_ALIGN),
                                 pl.multiple_of(n, RUN_ALIGN)))
            return carry
        lax.fori_loop(0, N_EXPERTS, body, 0)
    return each


def _dispatch_kernel(rg_ref, ro_ref, rn_ref, zrow_ref, nblk_ref, nu_ref, x_ref, slot_t_ref, xb_ref,
                     sorted_ref, zero_ref, sems, zsem):
    i = pl.program_id(0)
    n = pl.num_programs(0)
    tm = x_ref.shape[0]
    nb = xb_ref.shape[0] // MOE_BLK
    buf = i % 2

    def zero_copy(row):
        return pltpu.make_async_copy(
            zero_ref, xb_ref.at[pl.ds(pl.multiple_of(row, MOE_BLK), MOE_BLK), :], zsem)

    def runs_of(tile, slot):
        def make_copy(g, o, size):
            return pltpu.make_async_copy(sorted_ref.at[slot, pl.ds(o, size), :],
                                         xb_ref.at[pl.ds(g, size), :], sems.at[slot])
        return _run_copies(rg_ref, ro_ref, rn_ref, tile, make_copy)

    @pl.when(i == 0)
    def _():
        zero_ref[...] = jnp.zeros_like(zero_ref)

        def issue_zero(e, carry):
            @pl.when(nblk_ref[e] > 0)
            def _():
                zero_copy(zrow_ref[e]).start()
            return carry
        lax.fori_loop(0, N_EXPERTS, issue_zero, 0)
        lax.fori_loop(nu_ref[0], nb, lambda b, c: (zero_copy(b * MOE_BLK).start(), c)[1], 0)

        def wait_zero(e, carry):
            @pl.when(nblk_ref[e] > 0)
            def _():
                zero_copy(0).wait()
            return carry
        lax.fori_loop(0, N_EXPERTS, wait_zero, 0)
        lax.fori_loop(nu_ref[0], nb, lambda b, c: (zero_copy(0).wait(), c)[1], 0)

    @pl.when(i >= 2)
    def _():
        runs_of(i - 2, buf)(lambda c: c.wait())

    x = x_ref[...].astype(BF16)
    s1 = slot_t_ref[0:1, :]
    s2 = slot_t_ref[1:2, :]
    for rc in range(TILE_SLOTS // SORT_ROWS):
        r = lax.broadcasted_iota(jnp.int32, (SORT_ROWS, tm), 0) + rc * SORT_ROWS
        perm = jnp.where((r == s1) | (r == s2), 1.0, 0.0).astype(BF16)
        sorted_ref[buf, rc * SORT_ROWS:(rc + 1) * SORT_ROWS, :] = _dot(perm, x).astype(BF16)

    runs_of(i, buf)(lambda c: c.start())

    @pl.when(i == n - 1)
    def _():
        @pl.when(i >= 1)
        def _():
            runs_of(i - 1, 1 - buf)(lambda c: c.wait())
        runs_of(i, buf)(lambda c: c.wait())


def _dispatch(rg, ro, rn, zrow, nblk, n_used, h2, slot_t, n_rows):
    t = h2.shape[0]
    tm = min(t, TILE_TOK)
    grid_spec = pltpu.PrefetchScalarGridSpec(
        num_scalar_prefetch=6,
        grid=(t // tm,),
        in_specs=[pl.BlockSpec((tm, D_MODEL), lambda i, *_: (i, 0)),
                  pl.BlockSpec((8, tm), lambda i, *_: (i, 0))],
        out_specs=pl.BlockSpec(memory_space=pl.ANY),
        scratch_shapes=[pltpu.VMEM((2, TILE_SLOTS, D_MODEL), BF16),
                        pltpu.VMEM((MOE_BLK, D_MODEL), BF16),
                        pltpu.SemaphoreType.DMA((2,)),
                        pltpu.SemaphoreType.DMA(())],
    )
    return pl.pallas_call(
        _dispatch_kernel,
        out_shape=jax.ShapeDtypeStruct((n_rows, D_MODEL), BF16),
        grid_spec=grid_spec,
        compiler_params=_cparams(("arbitrary",)),
        name="moe_dispatch",
    )(rg, ro, rn, zrow, nblk, n_used, h2, slot_t)


def _expert_kernel(be_ref, nu_ref, nblk_ref, x_ref, wg_ref, wu_ref, wd_ref, y_ref,
                   wg_f, wu_f, wd_f, wg_b, wu_b, wd_b, sems, cur_ref):
    b = pl.program_id(0)
    n_used = nu_ref[0]
    used = b < n_used
    e = be_ref[b]
    prev = be_ref[jnp.maximum(b - 1, 0)]
    fresh = used & ((b == 0) | (e != prev))

    def weight_copies(expert, slot):
        half = EXPERT_FF // 2
        return [pltpu.make_async_copy(wg_ref.at[expert], wg_f.at[slot], sems.at[slot]),
                pltpu.make_async_copy(wu_ref.at[expert], wu_f.at[slot], sems.at[slot]),
                pltpu.make_async_copy(wd_ref.at[expert, :half], wd_f.at[slot, :half], sems.at[slot]),
                pltpu.make_async_copy(wd_ref.at[expert, half:], wd_f.at[slot, half:], sems.at[slot])]

    def start_weights(expert, slot):
        for j, c in enumerate(weight_copies(expert, slot)):
            c.start(priority=j % 2)

    @pl.when(b == 0)
    def _():
        cur_ref[0] = 0
        start_weights(e, 0)

    @pl.when(fresh)
    def _():
        slot = cur_ref[0]
        for c in weight_copies(e, slot):
            c.wait()
        wg_b[...] = wg_f[slot].astype(BF16)
        wu_b[...] = wu_f[slot].astype(BF16)
        wd_b[...] = wd_f[slot].astype(BF16)
        nxt = b + nblk_ref[e]

        @pl.when(nxt < n_used)
        def _():
            start_weights(be_ref[nxt], 1 - slot)
        cur_ref[0] = 1 - slot

    @pl.when(used)
    def _():
        x = x_ref[...]
        hid = _silu(_dot(x, wg_b[...])) * _dot(x, wu_b[...])
        y_ref[...] = _dot(hid.astype(BF16), wd_b[...]).astype(y_ref.dtype)

    @pl.when(jnp.logical_not(used))
    def _():
        y_ref[...] = jnp.zeros_like(y_ref)


def _experts(block_e, n_used, nblk, xb, w_gate, w_up, w_down):
    nb = xb.shape[0] // MOE_BLK
    any_space = pl.BlockSpec(memory_space=pl.ANY)
    grid_spec = pltpu.PrefetchScalarGridSpec(
        num_scalar_prefetch=3,
        grid=(nb,),
        in_specs=[pl.BlockSpec((MOE_BLK, D_MODEL), lambda b, be, nu, nk: (jnp.minimum(b, nu[0] - 1), 0)),
                  any_space, any_space, any_space],
        out_specs=pl.BlockSpec((MOE_BLK, D_MODEL), lambda b, be, nu, nk: (b, 0)),
        scratch_shapes=[pltpu.VMEM((2, D_MODEL, EXPERT_FF), F32),
                        pltpu.VMEM((2, D_MODEL, EXPERT_FF), F32),
                        pltpu.VMEM((2, EXPERT_FF, D_MODEL), F32),
                        pltpu.VMEM((D_MODEL, EXPERT_FF), BF16),
                        pltpu.VMEM((D_MODEL, EXPERT_FF), BF16),
                        pltpu.VMEM((EXPERT_FF, D_MODEL), BF16),
                        pltpu.SemaphoreType.DMA((2,)),
                        pltpu.SMEM((1,), jnp.int32)],
    )
    return pl.pallas_call(
        _expert_kernel,
        out_shape=jax.ShapeDtypeStruct(xb.shape, BF16),
        grid_spec=grid_spec,
        compiler_params=_cparams(("arbitrary",)),
        name="moe_experts",
    )(block_e, n_used, nblk, xb, w_gate, w_up, w_down)


def _combine_kernel(rg_ref, ro_ref, rn_ref, yb_ref, h2_ref, slot_ref, gate_ref, g_ref, b_ref, o_ref,
                    ybuf, sems):
    i = pl.program_id(0)
    n = pl.num_programs(0)
    tm = h2_ref.shape[0]
    buf = i % 2

    def runs_of(tile, slot):
        def make_copy(g, o, size):
            return pltpu.make_async_copy(yb_ref.at[pl.ds(g, size), :],
                                         ybuf.at[slot, pl.ds(o, size), :], sems.at[slot])
        return _run_copies(rg_ref, ro_ref, rn_ref, tile, make_copy)

    @pl.when(i == 0)
    def _():
        ybuf[...] = jnp.zeros_like(ybuf)
        runs_of(0, 0)(lambda c: c.start())

    @pl.when(i + 1 < n)
    def _():
        runs_of(i + 1, 1 - buf)(lambda c: c.start())

    runs_of(i, buf)(lambda c: c.wait())

    lane = lax.broadcasted_iota(jnp.int32, (tm, TILE_SLOTS), 1)
    slots = slot_ref[...]
    gate = gate_ref[...]
    y = ybuf[buf]
    sel = jnp.where(lane == slots[:, 0:1], gate[:, 0:1],
                    jnp.where(lane == slots[:, 1:2], gate[:, 1:2], 0.0)).astype(BF16)
    ffn = _dot(sel, y)
    o_ref[...] = _layer_norm(DEEPNORM_ALPHA * h2_ref[...] + ffn, g_ref[...], b_ref[...])


def _combine_ln(rg, ro, rn, yb, h2, slots, gate, g, b):
    t = h2.shape[0]
    tm = min(t, TILE_TOK)
    grid_spec = pltpu.PrefetchScalarGridSpec(
        num_scalar_prefetch=3,
        grid=(t // tm,),
        in_specs=[pl.BlockSpec(memory_space=pl.ANY),
                  pl.BlockSpec((tm, D_MODEL), lambda i, *_: (i, 0)),
                  pl.BlockSpec((tm, LANES), lambda i, *_: (i, 0)),
                  pl.BlockSpec((tm, LANES), lambda i, *_: (i, 0)),
                  pl.BlockSpec((1, D_MODEL), lambda i, *_: (0, 0)),
                  pl.BlockSpec((1, D_MODEL), lambda i, *_: (0, 0))],
        out_specs=pl.BlockSpec((tm, D_MODEL), lambda i, *_: (i, 0)),
        scratch_shapes=[pltpu.VMEM((2, TILE_SLOTS, D_MODEL), BF16),
                        pltpu.SemaphoreType.DMA((2,))],
    )
    return pl.pallas_call(
        _combine_kernel,
        out_shape=jax.ShapeDtypeStruct((t, D_MODEL), F32),
        grid_spec=grid_spec,
        compiler_params=_cparams(("arbitrary",)),
        name="moe_combine_ln3",
    )(rg, ro, rn, yb, h2, slots, gate, g, b)


def _mixer(x2d, positions, w_in, w_gla_a2, b_gla_a, g_gla_norm):
    s = x2d.shape[0]
    w_in_b = w_in.astype(BF16)
    half = RET_DK // 2
    inv_freq = (ROPE_BASE ** (-jnp.arange(half, dtype=F32) / half)).reshape(1, half)
    cos, sin = _rope_table(positions.reshape(s, 1), inv_freq)
    w_lr = jnp.pad(w_in_b[:, GLR_OFF:], ((0, 0), (0, LANES - GLA_LOWRANK)))
    w_a2 = jnp.pad(w_gla_a2.astype(BF16), ((0, LANES - GLA_LOWRANK), (0, 0)))
    h_main, log_a = _proj_in(x2d, w_in_b, w_lr, w_a2, b_gla_a.reshape(1, -1))
    log_gamma = jnp.log1p(-jnp.exp2(-5.0 - jnp.arange(RET_HEADS, dtype=F32)))
    ret = _retention(h_main, cos, sin, log_gamma)
    gla = _gla(h_main, log_a, g_gla_norm.reshape(1, -1))
    return ret, gla


def _moe(h2, logits, w_gate, w_up, w_down, g, b):
    t = h2.shape[0]
    nt = t // min(t, TILE_TOK)
    slots, slot_t, gate, runs, plan = _route(logits)
    runs = runs.reshape(nt, 8, LANES)[:, :3, :N_EXPERTS]
    rg, ro, rn = (runs[:, j, :].reshape(-1) for j in range(3))
    max_rows = 2 * t + nt * N_EXPERTS * (RUN_ALIGN - 1) + N_EXPERTS * (MOE_BLK - 1)
    nb = -(-max_rows // MOE_BLK)
    block_e, n_used = plan[0, :nb], plan[1, :1]
    last_blk_row, nblk = plan[2, :N_EXPERTS], plan[3, :N_EXPERTS]
    xb = _dispatch(rg, ro, rn, last_blk_row, nblk, n_used, h2, slot_t, nb * MOE_BLK)
    yb = _experts(block_e, n_used, nblk, xb, w_gate, w_up, w_down)
    return _combine_ln(rg, ro, rn, yb, h2, slots, gate, g, b)


def kernel(x, mem, positions, w_in, w_gla_a2, b_gla_a, g_gla_norm, w_mix_out, ln1_g, ln1_b, w_mq, w_mk, w_mv, w_mo, ln2_g, ln2_b, w_route_group, b_route_group, w_route_expert, b_route_expert, w_exp_gate, w_exp_up, w_exp_down, ln3_g, ln3_b):
    bsz, s, d = x.shape
    assert bsz == 1 and d == D_MODEL
    x2d = x.reshape(s, d)
    row = lambda v: v.reshape(1, -1)

    ret, gla = _mixer(x2d, positions, w_in[0], w_gla_a2[0], b_gla_a[0], g_gla_norm[0])
    h1 = _mixout_ln(ret, gla, x2d, w_mix_out[0], row(ln1_g[0]), row(ln1_b[0]))

    k, v = _mem_kv(mem[0], w_mk[0], w_mv[0])
    n_route = N_GROUPS + N_EXPERTS
    w_route = jnp.pad(jnp.concatenate([w_route_group[0], w_route_expert[0]], axis=1).astype(BF16),
                      ((0, 0), (0, LANES - n_route)))
    b_route = jnp.pad(jnp.concatenate([b_route_group[0], b_route_expert[0].reshape(-1)]),
                      (0, LANES - n_route)).reshape(1, LANES)
    h2, logits = _cross_attention(h1, w_mq[0].astype(BF16), k, v, w_mo[0].astype(BF16),
                                  row(ln2_g[0]), row(ln2_b[0]), w_route, b_route)

    out = _moe(h2, logits, w_exp_gate[0], w_exp_up[0], w_exp_down[0],
               row(ln3_g[0]), row(ln3_b[0]))
    return out.reshape(bsz, s, d)
```

```python
import math

import jax
import jax.numpy as jnp
from jax import lax
from jax.experimental import pallas as pl
from jax.experimental.pallas import tpu as pltpu

F32 = jnp.float32
BF16 = jnp.bfloat16

D_MODEL = 2048
MEM_LEN = 256
RET_HEADS = 4
RET_DK = 256
RET_DV = 256
GLA_HEADS = 4
GLA_DK = 128
GLA_DV = 256
GLA_LOWRANK = 16
GLA_TAU = 16.0
ROPE_BASE = 10000.0
MEM_HEADS = 4
MEM_HEAD_DIM = D_MODEL // MEM_HEADS
N_GROUPS = 4
EXPERTS_PER_GROUP = 8
N_EXPERTS = N_GROUPS * EXPERTS_PER_GROUP
EXPERT_FF = 512
LN_EPS = 1e-5
DEPTH = 1
DEEPNORM_ALPHA = (2 * DEPTH) ** 0.25

RQ_OFF, RK_OFF, RV_OFF, RG_OFF = 0, 1024, 2048, 3072
GQ_OFF, GK_OFF, GV_OFF, GG_OFF, GLR_OFF = 4096, 4608, 5120, 6144, 7168
IN_MAIN = 7168

LANES = 128
RET_CHUNK = 256
GLA_CHUNK = 128
GLA_LEVELS = 7
MOE_BLK = 512
TILE_TOK = 512
RUN_ALIGN = 16
TILE_SLOTS = 2 * TILE_TOK + N_EXPERTS * RUN_ALIGN
SORT_ROWS = 256
VMEM_LIMIT = 56 * 1024 * 1024


def _cparams(sem):
    return pltpu.CompilerParams(dimension_semantics=sem, vmem_limit_bytes=VMEM_LIMIT)


def _layer_norm(y, g, b):
    mu = jnp.mean(y, axis=-1, keepdims=True)
    d = y - mu
    var = jnp.mean(d * d, axis=-1, keepdims=True)
    return d * lax.rsqrt(var + LN_EPS) * g + b


def _silu(x):
    return x / (1.0 + jnp.exp(-x))


def _dot(a, b):
    return jnp.dot(a, b, preferred_element_type=F32)


def _dot_nt(a, b):
    return lax.dot_general(a, b, (((1,), (1,)), ((), ())), preferred_element_type=F32)


def _dot_tn(a, b):
    return lax.dot_general(a, b, (((0,), (0,)), ((), ())), preferred_element_type=F32)


def _rope_kernel(pos_ref, invf_ref, cos_ref, sin_ref):
    ang = pos_ref[...].astype(F32) * invf_ref[...]
    cos_ref[...] = jnp.cos(ang)
    sin_ref[...] = jnp.sin(ang)


def _rope_table(pos_col, inv_freq):
    s = pos_col.shape[0]
    tm = min(s, 1024)
    half = inv_freq.shape[1]
    return pl.pallas_call(
        _rope_kernel,
        out_shape=(jax.ShapeDtypeStruct((s, half), F32), jax.ShapeDtypeStruct((s, half), F32)),
        grid=(s // tm,),
        in_specs=[pl.BlockSpec((tm, 1), lambda i: (i, 0)),
                  pl.BlockSpec((1, half), lambda i: (0, 0))],
        out_specs=(pl.BlockSpec((tm, half), lambda i: (i, 0)),
                   pl.BlockSpec((tm, half), lambda i: (i, 0))),
        compiler_params=_cparams(("arbitrary",)),
        name="rope_table",
    )(pos_col, inv_freq)


def _proj_in_kernel(x_ref, w_ref, wlr_ref, wa2_ref, ba_ref, o_ref, la_ref, xb_ref):
    @pl.when(pl.program_id(1) == 0)
    def _():
        xb_ref[...] = x_ref[...].astype(BF16)
        glr = _dot(xb_ref[...], wlr_ref[...])
        z = _dot(glr.astype(BF16), wa2_ref[...]) + ba_ref[...]
        la_ref[...] = (jnp.minimum(z, 0.0) - jnp.log1p(jnp.exp(-jnp.abs(z)))) / GLA_TAU

    o_ref[...] = _dot(xb_ref[...], w_ref[...])


def _proj_in(x2d, w_in_b, w_lr, w_a2, b_a):
    s = x2d.shape[0]
    tm = min(s, 1024)
    tn = 512
    n = GLA_HEADS * GLA_DK
    return pl.pallas_call(
        _proj_in_kernel,
        out_shape=(jax.ShapeDtypeStruct((s, IN_MAIN), F32), jax.ShapeDtypeStruct((s, n), F32)),
        grid=(s // tm, IN_MAIN // tn),
        in_specs=[pl.BlockSpec((tm, D_MODEL), lambda i, j: (i, 0)),
                  pl.BlockSpec((D_MODEL, tn), lambda i, j: (0, j)),
                  pl.BlockSpec((D_MODEL, LANES), lambda i, j: (0, 0)),
                  pl.BlockSpec((LANES, n), lambda i, j: (0, 0)),
                  pl.BlockSpec((1, n), lambda i, j: (0, 0))],
        out_specs=(pl.BlockSpec((tm, tn), lambda i, j: (i, j)),
                   pl.BlockSpec((tm, n), lambda i, j: (i, 0))),
        scratch_shapes=[pltpu.VMEM((tm, D_MODEL), BF16)],
        compiler_params=_cparams(("arbitrary", "arbitrary")),
        name="proj_in",
    )(x2d, w_in_b, w_lr, w_a2, b_a)


def _rotary(t, cos, sin):
    half = t.shape[-1] // 2
    t1, t2 = t[:, :half], t[:, half:]
    return jnp.concatenate([t1 * cos - t2 * sin, t1 * sin + t2 * cos], axis=-1)


def _retention_kernel(lg_ref, q_ref, k_ref, v_ref, g_ref, cos_ref, sin_ref, o_ref, state_ref, intra_ref, dq_ref, dk_ref):
    c = pl.program_id(0)
    C = q_ref.shape[0]

    @pl.when(c == 0)
    def _():
        state_ref[...] = jnp.zeros_like(state_ref)
        ri = lax.broadcasted_iota(jnp.int32, (C, C), 0)
        ci = lax.broadcasted_iota(jnp.int32, (C, C), 1)
        rel = jnp.maximum(ri - ci, 0).astype(F32)
        n = lax.broadcasted_iota(jnp.int32, (C, RET_DK), 0).astype(F32)
        for h in range(RET_HEADS):
            lg = lg_ref[h]
            intra_ref[h] = jnp.where(ri >= ci, jnp.exp(lg * rel), 0.0)
            dq_ref[h] = jnp.exp(lg * (n + 1.0))
            dk_ref[h] = jnp.exp(lg * (C - 1.0 - n))

    cos = cos_ref[...]
    sin = sin_ref[...]

    for h in range(RET_HEADS):
        sl = slice(h * RET_DK, (h + 1) * RET_DK)
        q = _rotary(q_ref[:, sl], cos, sin)
        k = _rotary(k_ref[:, sl], cos, sin) * (RET_DK ** -0.5)
        v = v_ref[:, sl].astype(BF16)
        decay_q = dq_ref[h]
        decay_k = dk_ref[h]
        decay_chunk = jnp.exp(lg_ref[h] * C)

        state = state_ref[h]
        scores = _dot_nt(q.astype(BF16), k.astype(BF16)) * intra_ref[h]
        o = _dot(scores.astype(BF16), v) + _dot((q * decay_q).astype(BF16), state.astype(BF16))
        state_ref[h] = decay_chunk * state + _dot_tn((k * decay_k).astype(BF16), v)

        mu = jnp.mean(o, axis=-1, keepdims=True)
        d = o - mu
        var = jnp.mean(d * d, axis=-1, keepdims=True)
        o = d * lax.rsqrt(var + LN_EPS)
        o_ref[:, sl] = (_silu(g_ref[:, sl]) * o).astype(o_ref.dtype)


def _retention(h_main, cos, sin, log_gamma):
    s = h_main.shape[0]
    C = min(RET_CHUNK, s)
    w = RET_HEADS * RET_DK
    col = lambda off: (lambda c, lg: (c, off // w))
    grid_spec = pltpu.PrefetchScalarGridSpec(
        num_scalar_prefetch=1,
        grid=(s // C,),
        in_specs=[pl.BlockSpec((C, w), col(RQ_OFF)),
                  pl.BlockSpec((C, w), col(RK_OFF)),
                  pl.BlockSpec((C, w), col(RV_OFF)),
                  pl.BlockSpec((C, w), col(RG_OFF)),
                  pl.BlockSpec((C, RET_DK // 2), lambda c, lg: (c, 0)),
                  pl.BlockSpec((C, RET_DK // 2), lambda c, lg: (c, 0))],
        out_specs=pl.BlockSpec((C, RET_HEADS * RET_DV), lambda c, lg: (c, 0)),
        scratch_shapes=[pltpu.VMEM((RET_HEADS, RET_DK, RET_DV), F32),
                        pltpu.VMEM((RET_HEADS, C, C), F32),
                        pltpu.VMEM((RET_HEADS, C, RET_DK), F32),
                        pltpu.VMEM((RET_HEADS, C, RET_DK), F32)],
    )
    return pl.pallas_call(
        _retention_kernel,
        out_shape=jax.ShapeDtypeStruct((s, RET_HEADS * RET_DV), BF16),
        grid_spec=grid_spec,
        compiler_params=_cparams(("arbitrary",)),
        name="retention",
    )(log_gamma, h_main, h_main, h_main, h_main, cos, sin)


def _gla_decay_matrix(C):
    import numpy as np
    levels = int(math.log2(C))
    r = np.arange(C)[:, None]
    t = np.arange(C)[None, :]
    mats = []
    for l in range(levels):
        blk = C >> l
        half = blk // 2
        m = (r // blk) * blk + half - 1
        qside = (r % blk) >= half
        mats.append(np.where(qside, (t > m) & (t <= r), (t > r) & (t <= m)))
    mats.append(t <= r)
    mats.append(t > r)
    return np.concatenate(mats, axis=0).astype(np.float32)


def _gla_kernel(m_ref, q_ref, k_ref, v_ref, g_ref, la_ref, gn_ref, o_ref, state_ref):
    c = pl.program_id(0)
    C = q_ref.shape[0]
    levels = GLA_LEVELS

    @pl.when(c == 0)
    def _():
        state_ref[...] = jnp.zeros_like(state_ref)

    m = m_ref[...]
    ri = lax.broadcasted_iota(jnp.int32, (C, C), 0)
    ci = lax.broadcasted_iota(jnp.int32, (C, C), 1)
    xor = jnp.where(ri > ci, ri ^ ci, 0)
    row = lax.broadcasted_iota(jnp.int32, (C, 1), 0)

    heads = range(GLA_HEADS)
    ks = [slice(h * GLA_DK, (h + 1) * GLA_DK) for h in heads]
    vs = [slice(h * GLA_DV, (h + 1) * GLA_DV) for h in heads]

    expo = []
    for h in heads:
        la = la_ref[:, ks[h]]
        la_hi = la.astype(BF16)
        la_lo = (la - la_hi.astype(F32)).astype(BF16)
        expo.append(jnp.exp(_dot(m, la_hi) + _dot(m, la_lo)))

    q = [q_ref[:, ks[h]] * (GLA_DK ** -0.5) for h in heads]
    k = [k_ref[:, ks[h]] for h in heads]
    v = [v_ref[:, vs[h]].astype(BF16) for h in heads]

    scores = [jnp.where(ri == ci, _dot_nt(q[h].astype(BF16), k[h].astype(BF16)), 0.0) for h in heads]
    for l in range(levels):
        half = C >> (l + 1)
        shift = int(math.log2(half))
        qside = (row & half) != 0
        keep = (xor >> shift) == 1
        for h in heads:
            x = (jnp.where(qside, q[h], k[h]) * expo[h][l * C:(l + 1) * C]).astype(BF16)
            scores[h] = scores[h] + jnp.where(keep, _dot_nt(x, x), 0.0)

    outs = []
    for h in heads:
        e_b = expo[h][levels * C:(levels + 1) * C]
        e_rev = expo[h][(levels + 1) * C:(levels + 2) * C]
        e_last = e_b[C - 1:C, :]
        state = state_ref[h]
        outs.append(_dot(scores[h].astype(BF16), v[h])
                    + _dot_nt((q[h] * e_b).astype(BF16), state.astype(BF16)))
        state_ref[h] = state * e_last + _dot_tn(v[h], (k[h] * e_rev).astype(BF16))

    for h in heads:
        o = outs[h]
        o = o * lax.rsqrt(jnp.mean(o * o, axis=-1, keepdims=True) + LN_EPS) * gn_ref[...]
        o_ref[:, vs[h]] = (_silu(g_ref[:, vs[h]]) * o).astype(o_ref.dtype)


def _gla(h_main, log_a, g_norm):
    s = h_main.shape[0]
    C = GLA_CHUNK
    m = jnp.asarray(_gla_decay_matrix(C), dtype=BF16)
    nrow = m.shape[0]
    wk = GLA_HEADS * GLA_DK
    wv = GLA_HEADS * GLA_DV
    return pl.pallas_call(
        _gla_kernel,
        out_shape=jax.ShapeDtypeStruct((s, wv), BF16),
        grid=(s // C,),
        in_specs=[pl.BlockSpec((nrow, C), lambda c: (0, 0)),
                  pl.BlockSpec((C, wk), lambda c: (c, GQ_OFF // wk)),
                  pl.BlockSpec((C, wk), lambda c: (c, GK_OFF // wk)),
                  pl.BlockSpec((C, wv), lambda c: (c, GV_OFF // wv)),
                  pl.BlockSpec((C, wv), lambda c: (c, GG_OFF // wv)),
                  pl.BlockSpec((C, wk), lambda c: (c, 0)),
                  pl.BlockSpec((1, GLA_DV), lambda c: (0, 0))],
        out_specs=pl.BlockSpec((C, wv), lambda c: (c, 0)),
        scratch_shapes=[pltpu.VMEM((GLA_HEADS, GLA_DV, GLA_DK), F32)],
        compiler_params=_cparams(("arbitrary",)),
        name="gla",
    )(m, h_main, h_main, h_main, h_main, log_a, g_norm)


def _mixout_kernel(ret_ref, gla_ref, x_ref, w_ref, g_ref, b_ref, o_ref, wb_ref):
    nr = ret_ref.shape[1]

    @pl.when(pl.program_id(0) == 0)
    def _():
        wb_ref[...] = w_ref[...].astype(BF16)

    mix = _dot(ret_ref[...], wb_ref[:nr, :]) + _dot(gla_ref[...], wb_ref[nr:, :])
    o_ref[...] = _layer_norm(DEEPNORM_ALPHA * x_ref[...] + mix, g_ref[...], b_ref[...])


def _mixout_ln(ret, gla, x2d, w_b, g, b):
    s = x2d.shape[0]
    tm = min(s, 512)
    nr, ng = ret.shape[1], gla.shape[1]
    return pl.pallas_call(
        _mixout_kernel,
        out_shape=jax.ShapeDtypeStruct((s, D_MODEL), F32),
        grid=(s // tm,),
        in_specs=[pl.BlockSpec((tm, nr), lambda i: (i, 0)),
                  pl.BlockSpec((tm, ng), lambda i: (i, 0)),
                  pl.BlockSpec((tm, D_MODEL), lambda i: (i, 0)),
                  pl.BlockSpec((nr + ng, D_MODEL), lambda i: (0, 0), pipeline_mode=pl.Buffered(1)),
                  pl.BlockSpec((1, D_MODEL), lambda i: (0, 0)),
                  pl.BlockSpec((1, D_MODEL), lambda i: (0, 0))],
        out_specs=pl.BlockSpec((tm, D_MODEL), lambda i: (i, 0)),
        scratch_shapes=[pltpu.VMEM((nr + ng, D_MODEL), BF16)],
        compiler_params=_cparams(("arbitrary",)),
        name="mixout_ln1",
    )(ret, gla, x2d, w_b, g, b)


def _kv_kernel(mem_ref, wk_ref, wv_ref, k_ref, v_ref):
    m = mem_ref[...].astype(BF16)
    k_ref[...] = _dot(m, wk_ref[...].astype(BF16)).astype(k_ref.dtype)
    v_ref[...] = _dot(m, wv_ref[...].astype(BF16)).astype(v_ref.dtype)


def _mem_kv(mem_b, wk_b, wv_b):
    tn = 512
    return pl.pallas_call(
        _kv_kernel,
        out_shape=(jax.ShapeDtypeStruct((MEM_LEN, D_MODEL), BF16),
                   jax.ShapeDtypeStruct((MEM_LEN, D_MODEL), BF16)),
        grid=(D_MODEL // tn,),
        in_specs=[pl.BlockSpec((MEM_LEN, D_MODEL), lambda j: (0, 0)),
                  pl.BlockSpec((D_MODEL, tn), lambda j: (0, j)),
                  pl.BlockSpec((D_MODEL, tn), lambda j: (0, j))],
        out_specs=(pl.BlockSpec((MEM_LEN, tn), lambda j: (0, j)),
                   pl.BlockSpec((MEM_LEN, tn), lambda j: (0, j))),
        compiler_params=_cparams(("arbitrary",)),
        name="mem_kv",
    )(mem_b, wk_b, wv_b)


def _cross_kernel(h_ref, wq_ref, k_ref, v_ref, wo_ref, g_ref, b_ref, wr_ref, br_ref,
                  h2_ref, lg_ref):
    h1 = h_ref[...]
    q = _dot(h1.astype(BF16), wq_ref[...]).astype(BF16)
    outs = []
    for hd in range(MEM_HEADS):
        sl = slice(hd * MEM_HEAD_DIM, (hd + 1) * MEM_HEAD_DIM)
        s = _dot_nt(q[:, sl], k_ref[:, sl]) * (MEM_HEAD_DIM ** -0.5)
        s = s - jnp.max(s, axis=-1, keepdims=True)
        p = jnp.exp(s)
        p = p / jnp.sum(p, axis=-1, keepdims=True)
        outs.append(_dot(p.astype(BF16), v_ref[:, sl]))
    o = jnp.concatenate(outs, axis=-1).astype(BF16)
    cross = _dot(o, wo_ref[...])
    h2 = _layer_norm(DEEPNORM_ALPHA * h1 + cross, g_ref[...], b_ref[...])
    h2_ref[...] = h2
    lg_ref[...] = _dot(h2.astype(BF16), wr_ref[...]) + br_ref[...]


def _cross_attention(h1, wq_b, k, v, wo_b, g, b, w_route, b_route):
    s = h1.shape[0]
    tm = min(s, 512)
    const = lambda shape: pl.BlockSpec(shape, lambda i: (0, 0), pipeline_mode=pl.Buffered(1))
    return pl.pallas_call(
        _cross_kernel,
        out_shape=(jax.ShapeDtypeStruct((s, D_MODEL), F32),
                   jax.ShapeDtypeStruct((s, LANES), F32)),
        grid=(s // tm,),
        in_specs=[pl.BlockSpec((tm, D_MODEL), lambda i: (i, 0)),
                  const((D_MODEL, D_MODEL)),
                  const((MEM_LEN, D_MODEL)),
                  const((MEM_LEN, D_MODEL)),
                  const((D_MODEL, D_MODEL)),
                  const((1, D_MODEL)),
                  const((1, D_MODEL)),
                  const((D_MODEL, LANES)),
                  const((1, LANES))],
        out_specs=(pl.BlockSpec((tm, D_MODEL), lambda i: (i, 0)),
                   pl.BlockSpec((tm, LANES), lambda i: (i, 0))),
        compiler_params=_cparams(("arbitrary",)),
        name="cross_attn_ln2",
    )(h1, wq_b, k, v, wo_b, g, b, w_route, b_route)


def _route_kernel(lg_ref, slot_ref, slot_t_ref, gate_ref, runs_ref, plan_ref, tot_row, tot_col, gstart):
    phase = pl.program_id(0)
    i = pl.program_id(1)
    tm = lg_ref.shape[0]
    neg = -jnp.inf

    logits = lg_ref[...]
    lane = lax.broadcasted_iota(jnp.int32, (tm, LANES), 1)
    gmask = lane < N_GROUPS
    gl = jnp.where(gmask, logits, neg)
    ge = jnp.exp(gl - jnp.max(gl, axis=-1, keepdims=True))
    pg = ge / jnp.sum(ge, axis=-1, keepdims=True)
    pg_sel = jnp.max(pg, axis=-1, keepdims=True)
    grp = jnp.min(jnp.where((pg == pg_sel) & gmask, lane, LANES), axis=-1, keepdims=True)

    fl_lane = lane - N_GROUPS
    fmask = (fl_lane >= 0) & (fl_lane < N_EXPERTS) & ((fl_lane >> 3) == grp)
    fl = jnp.where(fmask, logits, neg)
    fe = jnp.exp(fl - jnp.max(fl, axis=-1, keepdims=True))
    fp = fe / jnp.sum(fe, axis=-1, keepdims=True)
    p1 = jnp.max(fp, axis=-1, keepdims=True)
    i1 = jnp.min(jnp.where((fp == p1) & fmask, lane, LANES), axis=-1, keepdims=True)
    rest = fmask & (lane != i1)
    fp2 = jnp.where(rest, fp, -1.0)
    p2 = jnp.max(fp2, axis=-1, keepdims=True)
    i2 = jnp.min(jnp.where((fp2 == p2) & rest, lane, LANES), axis=-1, keepdims=True)
    psum = p1 + p2
    gate1 = pg_sel * p1 / psum
    gate2 = pg_sel * p2 / psum

    oh1 = lane == (i1 - N_GROUPS)
    oh2 = lane == (i2 - N_GROUPS)
    oh = (jnp.where(oh1, 1.0, 0.0) + jnp.where(oh2, 1.0, 0.0)).astype(BF16)
    ones = jnp.ones((tm, LANES), BF16)

    def align_up(v, a):
        return jnp.floor((v + (a - 1.0)) * (1.0 / a)) * a

    run_row = align_up(_dot_tn(ones, oh)[0:8, :], RUN_ALIGN)
    rr = lax.broadcasted_iota(jnp.int32, (LANES, LANES), 0)
    cc = lax.broadcasted_iota(jnp.int32, (LANES, LANES), 1)
    strict_upper = jnp.where(rr < cc, 1.0, 0.0).astype(BF16)

    @pl.when((phase == 0) & (i == 0))
    def _():
        tot_row[...] = jnp.zeros_like(tot_row)
        tot_col[...] = jnp.zeros_like(tot_col)

    @pl.when(phase == 0)
    def _():
        tot_row[...] += run_row
        tot_col[...] += align_up(_dot_tn(oh, ones), RUN_ALIGN)

    @pl.when((phase == 1) & (i == 0))
    def _():
        nblk_row = align_up(tot_row[...], MOE_BLK) * (1.0 / MOE_BLK)
        nblk_col = align_up(tot_col[...], MOE_BLK) * (1.0 / MOE_BLK)
        lower_incl = jnp.where(cc <= rr, 1.0, 0.0).astype(BF16)
        base = _dot(nblk_row.astype(BF16), strict_upper) * float(MOE_BLK)
        ends = _dot(lower_incl, nblk_col.astype(BF16))
        expert_rows = rr < N_EXPERTS
        be = jnp.sum(jnp.where(expert_rows & (ends <= cc.astype(F32)), 1.0, 0.0), axis=0, keepdims=True)
        be = jnp.minimum(be, N_EXPERTS - 1.0)
        total = jnp.sum(jnp.where(expert_rows, nblk_col, 0.0), axis=0, keepdims=True)
        sub = lax.broadcasted_iota(jnp.int32, plan_ref.shape, 0)
        plan = jnp.where(sub == 0, jnp.broadcast_to(be, plan_ref.shape),
                         jnp.where(sub == 1, jnp.broadcast_to(total, plan_ref.shape),
                                   jnp.where(sub == 2, base + tot_row[...],
                                             jnp.where(sub == 3, nblk_row,
                                                       nblk_row * float(MOE_BLK) - tot_row[...]))))
        plan_ref[...] = plan.astype(jnp.int32)
        gstart[...] = base

    @pl.when(phase == 1)
    def _():
        off = _dot((run_row * (1.0 / RUN_ALIGN)).astype(BF16), strict_upper) * float(RUN_ALIGN)
        tr = lax.broadcasted_iota(jnp.int32, (tm, tm), 0)
        tc = lax.broadcasted_iota(jnp.int32, (tm, tm), 1)
        strict_lower = jnp.where(tc < tr, 1.0, 0.0).astype(BF16)
        slot = _dot(strict_lower, oh) + off[0:1, :]
        s1 = jnp.sum(jnp.where(oh1, slot, 0.0), axis=-1, keepdims=True)
        s2 = jnp.sum(jnp.where(oh2, slot, 0.0), axis=-1, keepdims=True)
        slots = jnp.where(lane == 0, s1, jnp.where(lane == 1, s2, 0.0))
        slot_ref[...] = slots.astype(jnp.int32)
        slot_t_ref[...] = jnp.transpose(slots)[0:8, :].astype(jnp.int32)
        gate_ref[...] = jnp.where(lane == 0, gate1, jnp.where(lane == 1, gate2, 0.0))
        sub = lax.broadcasted_iota(jnp.int32, runs_ref.shape, 0)
        runs = jnp.where(sub == 0, gstart[...], jnp.where(sub == 1, off, run_row))
        runs_ref[...] = runs.astype(jnp.int32)
        gstart[...] += run_row


def _route(logits):
    t = logits.shape[0]
    tm = min(t, TILE_TOK)
    nt = t // tm
    step = lambda p, i: (i * p, 0)
    return pl.pallas_call(
        _route_kernel,
        out_shape=(jax.ShapeDtypeStruct((t, LANES), jnp.int32),
                   jax.ShapeDtypeStruct((nt * 8, tm), jnp.int32),
                   jax.ShapeDtypeStruct((t, LANES), F32),
                   jax.ShapeDtypeStruct((nt * 8, LANES), jnp.int32),
                   jax.ShapeDtypeStruct((8, LANES), jnp.int32)),
        grid=(2, nt),
        in_specs=[pl.BlockSpec((tm, LANES), lambda p, i: (i, 0))],
        out_specs=(pl.BlockSpec((tm, LANES), step),
                   pl.BlockSpec((8, tm), step),
                   pl.BlockSpec((tm, LANES), step),
                   pl.BlockSpec((8, LANES), step),
                   pl.BlockSpec((8, LANES), lambda p, i: (0, 0))),
        scratch_shapes=[pltpu.VMEM((8, LANES), F32),
                        pltpu.VMEM((LANES, LANES), F32),
                        pltpu.VMEM((8, LANES), F32)],
        compiler_params=_cparams(("arbitrary", "arbitrary")),
        name="moe_route",
    )(logits)


def _run_copies(rg_ref, ro_ref, rn_ref, tile, make_copy):
    def each(action):
        def body(e, carry):
            j = tile * N_EXPERTS + e
            n = rn_ref[j]

            @pl.when(n > 0)
            def _():
                action(make_copy(pl.multiple_of(rg_ref[j], RUN_ALIGN), pl.multiple_of(ro_ref[j], RUN_ALIGN),
                                 pl.multiple_of(n, RUN_ALIGN)))
            return carry
        lax.fori_loop(0, N_EXPERTS, body, 0)
    return each


def _dispatch_kernel(rg_ref, ro_ref, rn_ref, zrow_ref, zlen_ref, nu_ref, x_ref, slot_t_ref, xb_ref,
                     sorted_ref, zero_ref, sems, zsem):
    i = pl.program_id(0)
    n = pl.num_programs(0)
    tm = x_ref.shape[0]
    nb = xb_ref.shape[0] // MOE_BLK
    buf = i % 2

    def zero_copy(row, size):
        return pltpu.make_async_copy(
            zero_ref.at[pl.ds(0, size), :], xb_ref.at[pl.ds(pl.multiple_of(row, RUN_ALIGN), size), :], zsem)

    def pad_copy(e):
        return zero_copy(zrow_ref[e], pl.multiple_of(zlen_ref[e], RUN_ALIGN))

    def runs_of(tile, slot):
        def make_copy(g, o, size):
            return pltpu.make_async_copy(sorted_ref.at[slot, pl.ds(o, size), :],
                                         xb_ref.at[pl.ds(g, size), :], sems.at[slot])
        return _run_copies(rg_ref, ro_ref, rn_ref, tile, make_copy)

    @pl.when(i == 0)
    def _():
        zero_ref[...] = jnp.zeros_like(zero_ref)

        def issue_zero(e, carry):
            @pl.when(zlen_ref[e] > 0)
            def _():
                pad_copy(e).start()
            return carry
        lax.fori_loop(0, N_EXPERTS, issue_zero, 0)
        lax.fori_loop(nu_ref[0], nb, lambda b, c: (zero_copy(b * MOE_BLK, MOE_BLK).start(), c)[1], 0)

        def wait_zero(e, carry):
            @pl.when(zlen_ref[e] > 0)
            def _():
                pad_copy(e).wait()
            return carry
        lax.fori_loop(0, N_EXPERTS, wait_zero, 0)
        lax.fori_loop(nu_ref[0], nb, lambda b, c: (zero_copy(0, MOE_BLK).wait(), c)[1], 0)

    @pl.when(i >= 2)
    def _():
        runs_of(i - 2, buf)(lambda c: c.wait())

    x = x_ref[...].astype(BF16)
    s1 = slot_t_ref[0:1, :]
    s2 = slot_t_ref[1:2, :]
    for rc in range(TILE_SLOTS // SORT_ROWS):
        r = lax.broadcasted_iota(jnp.int32, (SORT_ROWS, tm), 0) + rc * SORT_ROWS
        perm = jnp.where((r == s1) | (r == s2), 1.0, 0.0).astype(BF16)
        sorted_ref[buf, rc * SORT_ROWS:(rc + 1) * SORT_ROWS, :] = _dot(perm, x).astype(BF16)

    runs_of(i, buf)(lambda c: c.start())

    @pl.when(i == n - 1)
    def _():
        @pl.when(i >= 1)
        def _():
            runs_of(i - 1, 1 - buf)(lambda c: c.wait())
        runs_of(i, buf)(lambda c: c.wait())


def _dispatch(rg, ro, rn, zrow, zlen, n_used, h2, slot_t, n_rows):
    t = h2.shape[0]
    tm = min(t, TILE_TOK)
    grid_spec = pltpu.PrefetchScalarGridSpec(
        num_scalar_prefetch=6,
        grid=(t // tm,),
        in_specs=[pl.BlockSpec((tm, D_MODEL), lambda i, *_: (i, 0)),
                  pl.BlockSpec((8, tm), lambda i, *_: (i, 0))],
        out_specs=pl.BlockSpec(memory_space=pl.ANY),
        scratch_shapes=[pltpu.VMEM((2, TILE_SLOTS, D_MODEL), BF16),
                        pltpu.VMEM((MOE_BLK, D_MODEL), BF16),
                        pltpu.SemaphoreType.DMA((2,)),
                        pltpu.SemaphoreType.DMA(())],
    )
    return pl.pallas_call(
        _dispatch_kernel,
        out_shape=jax.ShapeDtypeStruct((n_rows, D_MODEL), BF16),
        grid_spec=grid_spec,
        compiler_params=_cparams(("arbitrary",)),
        name="moe_dispatch",
    )(rg, ro, rn, zrow, zlen, n_used, h2, slot_t)


def _expert_kernel(be_ref, nu_ref, nblk_ref, x_ref, wg_ref, wu_ref, wd_ref, y_ref,
                   wg_f, wu_f, wd_f, wg_b, wu_b, wd_b, sems, cur_ref):
    b = pl.program_id(0)
    n_used = nu_ref[0]
    used = b < n_used
    e = be_ref[b]
    prev = be_ref[jnp.maximum(b - 1, 0)]
    fresh = used & ((b == 0) | (e != prev))

    def weight_copies(expert, slot):
        half = EXPERT_FF // 2
        return [pltpu.make_async_copy(wg_ref.at[expert], wg_f.at[slot], sems.at[slot]),
                pltpu.make_async_copy(wu_ref.at[expert], wu_f.at[slot], sems.at[slot]),
                pltpu.make_async_copy(wd_ref.at[expert, :half], wd_f.at[slot, :half], sems.at[slot]),
                pltpu.make_async_copy(wd_ref.at[expert, half:], wd_f.at[slot, half:], sems.at[slot])]

    def start_weights(expert, slot):
        for j, c in enumerate(weight_copies(expert, slot)):
            c.start(priority=j % 2)

    @pl.when(b == 0)
    def _():
        cur_ref[0] = 0
        start_weights(e, 0)

    @pl.when(fresh)
    def _():
        slot = cur_ref[0]
        for c in weight_copies(e, slot):
            c.wait()
        wg_b[...] = wg_f[slot].astype(BF16)
        wu_b[...] = wu_f[slot].astype(BF16)
        wd_b[...] = wd_f[slot].astype(BF16)
        nxt = b + nblk_ref[e]

        @pl.when(nxt < n_used)
        def _():
            start_weights(be_ref[nxt], 1 - slot)
        cur_ref[0] = 1 - slot

    @pl.when(used)
    def _():
        x = x_ref[...]
        hid = _silu(_dot(x, wg_b[...])) * _dot(x, wu_b[...])
        y_ref[...] = _dot(hid.astype(BF16), wd_b[...]).astype(y_ref.dtype)

    @pl.when(jnp.logical_not(used))
    def _():
        y_ref[...] = jnp.zeros_like(y_ref)


def _experts(block_e, n_used, nblk, xb, w_gate, w_up, w_down):
    nb = xb.shape[0] // MOE_BLK
    any_space = pl.BlockSpec(memory_space=pl.ANY)
    grid_spec = pltpu.PrefetchScalarGridSpec(
        num_scalar_prefetch=3,
        grid=(nb,),
        in_specs=[pl.BlockSpec((MOE_BLK, D_MODEL), lambda b, be, nu, nk: (jnp.minimum(b, nu[0] - 1), 0)),
                  any_space, any_space, any_space],
        out_specs=pl.BlockSpec((MOE_BLK, D_MODEL), lambda b, be, nu, nk: (b, 0)),
        scratch_shapes=[pltpu.VMEM((2, D_MODEL, EXPERT_FF), F32),
                        pltpu.VMEM((2, D_MODEL, EXPERT_FF), F32),
                        pltpu.VMEM((2, EXPERT_FF, D_MODEL), F32),
                        pltpu.VMEM((D_MODEL, EXPERT_FF), BF16),
                        pltpu.VMEM((D_MODEL, EXPERT_FF), BF16),
                        pltpu.VMEM((EXPERT_FF, D_MODEL), BF16),
                        pltpu.SemaphoreType.DMA((2,)),
                        pltpu.SMEM((1,), jnp.int32)],
    )
    return pl.pallas_call(
        _expert_kernel,
        out_shape=jax.ShapeDtypeStruct(xb.shape, BF16),
        grid_spec=grid_spec,
        compiler_params=_cparams(("arbitrary",)),
        name="moe_experts",
    )(block_e, n_used, nblk, xb, w_gate, w_up, w_down)


def _combine_kernel(rg_ref, ro_ref, rn_ref, yb_ref, h2_ref, slot_ref, gate_ref, g_ref, b_ref, o_ref,
                    ybuf, sems):
    i = pl.program_id(0)
    n = pl.num_programs(0)
    tm = h2_ref.shape[0]
    buf = i % 2

    def runs_of(tile, slot):
        def make_copy(g, o, size):
            return pltpu.make_async_copy(yb_ref.at[pl.ds(g, size), :],
                                         ybuf.at[slot, pl.ds(o, size), :], sems.at[slot])
        return _run_copies(rg_ref, ro_ref, rn_ref, tile, make_copy)

    @pl.when(i == 0)
    def _():
        ybuf[...] = jnp.zeros_like(ybuf)
        runs_of(0, 0)(lambda c: c.start())

    @pl.when(i + 1 < n)
    def _():
        runs_of(i + 1, 1 - buf)(lambda c: c.start())

    runs_of(i, buf)(lambda c: c.wait())

    lane = lax.broadcasted_iota(jnp.int32, (tm, TILE_SLOTS), 1)
    slots = slot_ref[...]
    gate = gate_ref[...]
    y = ybuf[buf]
    sel = jnp.where(lane == slots[:, 0:1], gate[:, 0:1],
                    jnp.where(lane == slots[:, 1:2], gate[:, 1:2], 0.0)).astype(BF16)
    ffn = _dot(sel, y)
    o_ref[...] = _layer_norm(DEEPNORM_ALPHA * h2_ref[...] + ffn, g_ref[...], b_ref[...])


def _combine_ln(rg, ro, rn, yb, h2, slots, gate, g, b):
    t = h2.shape[0]
    tm = min(t, TILE_TOK)
    grid_spec = pltpu.PrefetchScalarGridSpec(
        num_scalar_prefetch=3,
        grid=(t // tm,),
        in_specs=[pl.BlockSpec(memory_space=pl.ANY),
                  pl.BlockSpec((tm, D_MODEL), lambda i, *_: (i, 0)),
                  pl.BlockSpec((tm, LANES), lambda i, *_: (i, 0)),
                  pl.BlockSpec((tm, LANES), lambda i, *_: (i, 0)),
                  pl.BlockSpec((1, D_MODEL), lambda i, *_: (0, 0)),
                  pl.BlockSpec((1, D_MODEL), lambda i, *_: (0, 0))],
        out_specs=pl.BlockSpec((tm, D_MODEL), lambda i, *_: (i, 0)),
        scratch_shapes=[pltpu.VMEM((2, TILE_SLOTS, D_MODEL), BF16),
                        pltpu.SemaphoreType.DMA((2,))],
    )
    return pl.pallas_call(
        _combine_kernel,
        out_shape=jax.ShapeDtypeStruct((t, D_MODEL), F32),
        grid_spec=grid_spec,
        compiler_params=_cparams(("arbitrary",)),
        name="moe_combine_ln3",
    )(rg, ro, rn, yb, h2, slots, gate, g, b)


def _mixer(x2d, positions, w_in, w_gla_a2, b_gla_a, g_gla_norm):
    s = x2d.shape[0]
    w_in_b = w_in.astype(BF16)
    half = RET_DK // 2
    inv_freq = (ROPE_BASE ** (-jnp.arange(half, dtype=F32) / half)).reshape(1, half)
    cos, sin = _rope_table(positions.reshape(s, 1), inv_freq)
    w_lr = jnp.pad(w_in_b[:, GLR_OFF:], ((0, 0), (0, LANES - GLA_LOWRANK)))
    w_a2 = jnp.pad(w_gla_a2.astype(BF16), ((0, LANES - GLA_LOWRANK), (0, 0)))
    h_main, log_a = _proj_in(x2d, w_in_b, w_lr, w_a2, b_gla_a.reshape(1, -1))
    log_gamma = jnp.log1p(-jnp.exp2(-5.0 - jnp.arange(RET_HEADS, dtype=F32)))
    ret = _retention(h_main, cos, sin, log_gamma)
    gla = _gla(h_main, log_a, g_gla_norm.reshape(1, -1))
    return ret, gla


def _moe(h2, logits, w_gate, w_up, w_down, g, b):
    t = h2.shape[0]
    nt = t // min(t, TILE_TOK)
    slots, slot_t, gate, runs, plan = _route(logits)
    runs = runs.reshape(nt, 8, LANES)[:, :3, :N_EXPERTS]
    rg, ro, rn = (runs[:, j, :].reshape(-1) for j in range(3))
    max_rows = 2 * t + nt * N_EXPERTS * (RUN_ALIGN - 1) + N_EXPERTS * (MOE_BLK - 1)
    nb = -(-max_rows // MOE_BLK)
    block_e, n_used = plan[0, :nb], plan[1, :1]
    pad_row, nblk, pad_len = (plan[j, :N_EXPERTS] for j in (2, 3, 4))
    xb = _dispatch(rg, ro, rn, pad_row, pad_len, n_used, h2, slot_t, nb * MOE_BLK)
    yb = _experts(block_e, n_used, nblk, xb, w_gate, w_up, w_down)
    return _combine_ln(rg, ro, rn, yb, h2, slots, gate, g, b)


def kernel(x, mem, positions, w_in, w_gla_a2, b_gla_a, g_gla_norm, w_mix_out, ln1_g, ln1_b, w_mq, w_mk, w_mv, w_mo, ln2_g, ln2_b, w_route_group, b_route_group, w_route_expert, b_route_expert, w_exp_gate, w_exp_up, w_exp_down, ln3_g, ln3_b):
    bsz, s, d = x.shape
    assert bsz == 1 and d == D_MODEL
    x2d = x.reshape(s, d)
    row = lambda v: v.reshape(1, -1)

    ret, gla = _mixer(x2d, positions, w_in[0], w_gla_a2[0], b_gla_a[0], g_gla_norm[0])
    h1 = _mixout_ln(ret, gla, x2d, w_mix_out[0], row(ln1_g[0]), row(ln1_b[0]))

    k, v = _mem_kv(mem[0], w_mk[0], w_mv[0])
    n_route = N_GROUPS + N_EXPERTS
    w_route = jnp.pad(jnp.concatenate([w_route_group[0], w_route_expert[0]], axis=1).astype(BF16),
                      ((0, 0), (0, LANES - n_route)))
    b_route = jnp.pad(jnp.concatenate([b_route_group[0], b_route_expert[0].reshape(-1)]),
                      (0, LANES - n_route)).reshape(1, LANES)
    h2, logits = _cross_attention(h1, w_mq[0].astype(BF16), k, v, w_mo[0].astype(BF16),
                                  row(ln2_g[0]), row(ln2_b[0]), w_route, b_route)

    out = _moe(h2, logits, w_exp_gate[0], w_exp_up[0], w_exp_down[0],
               row(ln3_g[0]), row(ln3_b[0]))
    return out.reshape(bsz, s, d)
```

```python
import math

import jax
import jax.numpy as jnp
from jax import lax
from jax.experimental import pallas as pl
from jax.experimental.pallas import tpu as pltpu

F32 = jnp.float32
BF16 = jnp.bfloat16

D_MODEL = 2048
MEM_LEN = 256
RET_HEADS = 4
RET_DK = 256
RET_DV = 256
GLA_HEADS = 4
GLA_DK = 128
GLA_DV = 256
GLA_LOWRANK = 16
GLA_TAU = 16.0
ROPE_BASE = 10000.0
MEM_HEADS = 4
MEM_HEAD_DIM = D_MODEL // MEM_HEADS
N_GROUPS = 4
EXPERTS_PER_GROUP = 8
N_EXPERTS = N_GROUPS * EXPERTS_PER_GROUP
EXPERT_FF = 512
LN_EPS = 1e-5
DEPTH = 1
DEEPNORM_ALPHA = (2 * DEPTH) ** 0.25

RQ_OFF, RK_OFF, RV_OFF, RG_OFF = 0, 1024, 2048, 3072
GQ_OFF, GK_OFF, GV_OFF, GG_OFF, GLR_OFF = 4096, 4608, 5120, 6144, 7168
IN_MAIN = 7168

LANES = 128
RET_CHUNK = 256
GLA_CHUNK = 128
GLA_LEVELS = 7
MOE_BLK = 256
EXPERT_STEP_BLKS = 4
TILE_TOK = 512
RUN_ALIGN = 16
TILE_SLOTS = 2 * TILE_TOK + N_EXPERTS * RUN_ALIGN
SORT_ROWS = 256
VMEM_LIMIT = 56 * 1024 * 1024


def _cparams(sem):
    return pltpu.CompilerParams(dimension_semantics=sem, vmem_limit_bytes=VMEM_LIMIT)


def _layer_norm(y, g, b):
    mu = jnp.mean(y, axis=-1, keepdims=True)
    d = y - mu
    var = jnp.mean(d * d, axis=-1, keepdims=True)
    return d * lax.rsqrt(var + LN_EPS) * g + b


def _silu(x):
    return x / (1.0 + jnp.exp(-x))


def _dot(a, b):
    return jnp.dot(a, b, preferred_element_type=F32)


def _dot_nt(a, b):
    return lax.dot_general(a, b, (((1,), (1,)), ((), ())), preferred_element_type=F32)


def _dot_tn(a, b):
    return lax.dot_general(a, b, (((0,), (0,)), ((), ())), preferred_element_type=F32)


def _rope_kernel(pos_ref, invf_ref, cos_ref, sin_ref):
    ang = pos_ref[...].astype(F32) * invf_ref[...]
    cos_ref[...] = jnp.cos(ang)
    sin_ref[...] = jnp.sin(ang)


def _rope_table(pos_col, inv_freq):
    s = pos_col.shape[0]
    tm = min(s, 1024)
    half = inv_freq.shape[1]
    return pl.pallas_call(
        _rope_kernel,
        out_shape=(jax.ShapeDtypeStruct((s, half), F32), jax.ShapeDtypeStruct((s, half), F32)),
        grid=(s // tm,),
        in_specs=[pl.BlockSpec((tm, 1), lambda i: (i, 0)),
                  pl.BlockSpec((1, half), lambda i: (0, 0))],
        out_specs=(pl.BlockSpec((tm, half), lambda i: (i, 0)),
                   pl.BlockSpec((tm, half), lambda i: (i, 0))),
        compiler_params=_cparams(("arbitrary",)),
        name="rope_table",
    )(pos_col, inv_freq)


def _proj_in_kernel(x_ref, w_ref, wlr_ref, wa2_ref, ba_ref, o_ref, la_ref, xb_ref):
    @pl.when(pl.program_id(1) == 0)
    def _():
        xb_ref[...] = x_ref[...].astype(BF16)
        glr = _dot(xb_ref[...], wlr_ref[...])
        z = _dot(glr.astype(BF16), wa2_ref[...]) + ba_ref[...]
        la_ref[...] = (jnp.minimum(z, 0.0) - jnp.log(1.0 + jnp.exp(-jnp.abs(z)))) / GLA_TAU

    o_ref[...] = _dot(xb_ref[...], w_ref[...])


def _proj_in(x2d, w_in_b, w_lr, w_a2, b_a):
    s = x2d.shape[0]
    tm = min(s, 1024)
    tn = 1024
    n = GLA_HEADS * GLA_DK
    return pl.pallas_call(
        _proj_in_kernel,
        out_shape=(jax.ShapeDtypeStruct((s, IN_MAIN), F32), jax.ShapeDtypeStruct((s, n), F32)),
        grid=(s // tm, IN_MAIN // tn),
        in_specs=[pl.BlockSpec((tm, D_MODEL), lambda i, j: (i, 0)),
                  pl.BlockSpec((D_MODEL, tn), lambda i, j: (0, j)),
                  pl.BlockSpec((D_MODEL, LANES), lambda i, j: (0, 0)),
                  pl.BlockSpec((LANES, n), lambda i, j: (0, 0)),
                  pl.BlockSpec((1, n), lambda i, j: (0, 0))],
        out_specs=(pl.BlockSpec((tm, tn), lambda i, j: (i, j)),
                   pl.BlockSpec((tm, n), lambda i, j: (i, 0))),
        scratch_shapes=[pltpu.VMEM((tm, D_MODEL), BF16)],
        compiler_params=_cparams(("arbitrary", "arbitrary")),
        name="proj_in",
    )(x2d, w_in_b, w_lr, w_a2, b_a)


def _rotary(t, cos, sin):
    half = t.shape[-1] // 2
    t1, t2 = t[:, :half], t[:, half:]
    return jnp.concatenate([t1 * cos - t2 * sin, t1 * sin + t2 * cos], axis=-1)


def _retention_kernel(lg_ref, q_ref, k_ref, v_ref, g_ref, cos_ref, sin_ref, o_ref, state_ref, intra_ref, dq_ref, dk_ref):
    c = pl.program_id(0)
    C = q_ref.shape[0]

    @pl.when(c == 0)
    def _():
        state_ref[...] = jnp.zeros_like(state_ref)
        ri = lax.broadcasted_iota(jnp.int32, (C, C), 0)
        ci = lax.broadcasted_iota(jnp.int32, (C, C), 1)
        rel = jnp.maximum(ri - ci, 0).astype(F32)
        n = lax.broadcasted_iota(jnp.int32, (C, RET_DK), 0).astype(F32)
        for h in range(RET_HEADS):
            lg = lg_ref[h]
            intra_ref[h] = jnp.where(ri >= ci, jnp.exp(lg * rel), 0.0)
            dq_ref[h] = jnp.exp(lg * (n + 1.0))
            dk_ref[h] = jnp.exp(lg * (C - 1.0 - n))

    cos = cos_ref[...]
    sin = sin_ref[...]

    for h in range(RET_HEADS):
        sl = slice(h * RET_DK, (h + 1) * RET_DK)
        q = _rotary(q_ref[:, sl], cos, sin)
        k = _rotary(k_ref[:, sl], cos, sin) * (RET_DK ** -0.5)
        v = v_ref[:, sl].astype(BF16)
        decay_q = dq_ref[h]
        decay_k = dk_ref[h]
        decay_chunk = jnp.exp(lg_ref[h] * C)

        state = state_ref[h]
        scores = _dot_nt(q.astype(BF16), k.astype(BF16)) * intra_ref[h]
        o = _dot(scores.astype(BF16), v) + _dot((q * decay_q).astype(BF16), state.astype(BF16))
        state_ref[h] = decay_chunk * state + _dot_tn((k * decay_k).astype(BF16), v)

        mu = jnp.mean(o, axis=-1, keepdims=True)
        d = o - mu
        var = jnp.mean(d * d, axis=-1, keepdims=True)
        o = d * lax.rsqrt(var + LN_EPS)
        o_ref[:, sl] = (_silu(g_ref[:, sl]) * o).astype(o_ref.dtype)


def _retention(h_main, cos, sin, log_gamma):
    s = h_main.shape[0]
    C = min(RET_CHUNK, s)
    w = RET_HEADS * RET_DK
    col = lambda off: (lambda c, lg: (c, off // w))
    grid_spec = pltpu.PrefetchScalarGridSpec(
        num_scalar_prefetch=1,
        grid=(s // C,),
        in_specs=[pl.BlockSpec((C, w), col(RQ_OFF)),
                  pl.BlockSpec((C, w), col(RK_OFF)),
                  pl.BlockSpec((C, w), col(RV_OFF)),
                  pl.BlockSpec((C, w), col(RG_OFF)),
                  pl.BlockSpec((C, RET_DK // 2), lambda c, lg: (c, 0)),
                  pl.BlockSpec((C, RET_DK // 2), lambda c, lg: (c, 0))],
        out_specs=pl.BlockSpec((C, RET_HEADS * RET_DV), lambda c, lg: (c, 0)),
        scratch_shapes=[pltpu.VMEM((RET_HEADS, RET_DK, RET_DV), F32),
                        pltpu.VMEM((RET_HEADS, C, C), F32),
                        pltpu.VMEM((RET_HEADS, C, RET_DK), F32),
                        pltpu.VMEM((RET_HEADS, C, RET_DK), F32)],
    )
    return pl.pallas_call(
        _retention_kernel,
        out_shape=jax.ShapeDtypeStruct((s, RET_HEADS * RET_DV), BF16),
        grid_spec=grid_spec,
        compiler_params=_cparams(("arbitrary",)),
        name="retention",
    )(log_gamma, h_main, h_main, h_main, h_main, cos, sin)


def _gla_decay_matrix(C):
    import numpy as np
    levels = int(math.log2(C))
    r = np.arange(C)[:, None]
    t = np.arange(C)[None, :]
    mats = []
    for l in range(levels):
        blk = C >> l
        half = blk // 2
        m = (r // blk) * blk + half - 1
        qside = (r % blk) >= half
        mats.append(np.where(qside, (t > m) & (t <= r), (t > r) & (t <= m)))
    mats.append(t <= r)
    mats.append(t > r)
    return np.concatenate(mats, axis=0).astype(np.float32)


def _gla_kernel(m_ref, q_ref, k_ref, v_ref, g_ref, la_ref, gn_ref, o_ref, state_ref):
    c = pl.program_id(0)
    C = q_ref.shape[0]
    levels = GLA_LEVELS

    @pl.when(c == 0)
    def _():
        state_ref[...] = jnp.zeros_like(state_ref)

    m = m_ref[...]
    ri = lax.broadcasted_iota(jnp.int32, (C, C), 0)
    ci = lax.broadcasted_iota(jnp.int32, (C, C), 1)
    xor = jnp.where(ri > ci, ri ^ ci, 0)
    row = lax.broadcasted_iota(jnp.int32, (C, 1), 0)

    heads = range(GLA_HEADS)
    ks = [slice(h * GLA_DK, (h + 1) * GLA_DK) for h in heads]
    vs = [slice(h * GLA_DV, (h + 1) * GLA_DV) for h in heads]

    expo = []
    for h in heads:
        la = la_ref[:, ks[h]]
        la_hi = la.astype(BF16)
        la_lo = (la - la_hi.astype(F32)).astype(BF16)
        expo.append(jnp.exp(_dot(m, la_hi) + _dot(m, la_lo)))

    q = [q_ref[:, ks[h]] * (GLA_DK ** -0.5) for h in heads]
    k = [k_ref[:, ks[h]] for h in heads]
    v = [v_ref[:, vs[h]].astype(BF16) for h in heads]

    scores = [jnp.where(ri == ci, _dot_nt(q[h].astype(BF16), k[h].astype(BF16)), 0.0) for h in heads]
    for l in range(levels):
        half = C >> (l + 1)
        shift = int(math.log2(half))
        qside = (row & half) != 0
        keep = (xor >> shift) == 1
        for h in heads:
            x = (jnp.where(qside, q[h], k[h]) * expo[h][l * C:(l + 1) * C]).astype(BF16)
            scores[h] = scores[h] + jnp.where(keep, _dot_nt(x, x), 0.0)

    outs = []
    for h in heads:
        e_b = expo[h][levels * C:(levels + 1) * C]
        e_rev = expo[h][(levels + 1) * C:(levels + 2) * C]
        e_last = e_b[C - 1:C, :]
        state = state_ref[h]
        outs.append(_dot(scores[h].astype(BF16), v[h])
                    + _dot_nt((q[h] * e_b).astype(BF16), state.astype(BF16)))
        state_ref[h] = state * e_last + _dot_tn(v[h], (k[h] * e_rev).astype(BF16))

    for h in heads:
        o = outs[h]
        o = o * lax.rsqrt(jnp.mean(o * o, axis=-1, keepdims=True) + LN_EPS) * gn_ref[...]
        o_ref[:, vs[h]] = (_silu(g_ref[:, vs[h]]) * o).astype(o_ref.dtype)


def _gla(h_main, log_a, g_norm):
    s = h_main.shape[0]
    C = GLA_CHUNK
    m = jnp.asarray(_gla_decay_matrix(C), dtype=BF16)
    nrow = m.shape[0]
    wk = GLA_HEADS * GLA_DK
    wv = GLA_HEADS * GLA_DV
    return pl.pallas_call(
        _gla_kernel,
        out_shape=jax.ShapeDtypeStruct((s, wv), BF16),
        grid=(s // C,),
        in_specs=[pl.BlockSpec((nrow, C), lambda c: (0, 0)),
                  pl.BlockSpec((C, wk), lambda c: (c, GQ_OFF // wk)),
                  pl.BlockSpec((C, wk), lambda c: (c, GK_OFF // wk)),
                  pl.BlockSpec((C, wv), lambda c: (c, GV_OFF // wv)),
                  pl.BlockSpec((C, wv), lambda c: (c, GG_OFF // wv)),
                  pl.BlockSpec((C, wk), lambda c: (c, 0)),
                  pl.BlockSpec((1, GLA_DV), lambda c: (0, 0))],
        out_specs=pl.BlockSpec((C, wv), lambda c: (c, 0)),
        scratch_shapes=[pltpu.VMEM((GLA_HEADS, GLA_DV, GLA_DK), F32)],
        compiler_params=_cparams(("arbitrary",)),
        name="gla",
    )(m, h_main, h_main, h_main, h_main, log_a, g_norm)


def _mixout_kernel(ret_ref, gla_ref, x_ref, w_ref, g_ref, b_ref, o_ref, wb_ref):
    nr = ret_ref.shape[1]

    @pl.when(pl.program_id(0) == 0)
    def _():
        wb_ref[...] = w_ref[...].astype(BF16)

    mix = _dot(ret_ref[...], wb_ref[:nr, :]) + _dot(gla_ref[...], wb_ref[nr:, :])
    o_ref[...] = _layer_norm(DEEPNORM_ALPHA * x_ref[...] + mix, g_ref[...], b_ref[...])


def _mixout_ln(ret, gla, x2d, w_b, g, b):
    s = x2d.shape[0]
    tm = min(s, 512)
    nr, ng = ret.shape[1], gla.shape[1]
    return pl.pallas_call(
        _mixout_kernel,
        out_shape=jax.ShapeDtypeStruct((s, D_MODEL), F32),
        grid=(s // tm,),
        in_specs=[pl.BlockSpec((tm, nr), lambda i: (i, 0)),
                  pl.BlockSpec((tm, ng), lambda i: (i, 0)),
                  pl.BlockSpec((tm, D_MODEL), lambda i: (i, 0)),
                  pl.BlockSpec((nr + ng, D_MODEL), lambda i: (0, 0), pipeline_mode=pl.Buffered(1)),
                  pl.BlockSpec((1, D_MODEL), lambda i: (0, 0)),
                  pl.BlockSpec((1, D_MODEL), lambda i: (0, 0))],
        out_specs=pl.BlockSpec((tm, D_MODEL), lambda i: (i, 0)),
        scratch_shapes=[pltpu.VMEM((nr + ng, D_MODEL), BF16)],
        compiler_params=_cparams(("arbitrary",)),
        name="mixout_ln1",
    )(ret, gla, x2d, w_b, g, b)


def _kv_kernel(mem_ref, wk_ref, wv_ref, k_ref, v_ref):
    m = mem_ref[...].astype(BF16)
    k_ref[...] = _dot(m, wk_ref[...].astype(BF16)).astype(k_ref.dtype)
    v_ref[...] = _dot(m, wv_ref[...].astype(BF16)).astype(v_ref.dtype)


def _mem_kv(mem_b, wk_b, wv_b):
    tn = 512
    return pl.pallas_call(
        _kv_kernel,
        out_shape=(jax.ShapeDtypeStruct((MEM_LEN, D_MODEL), BF16),
                   jax.ShapeDtypeStruct((MEM_LEN, D_MODEL), BF16)),
        grid=(D_MODEL // tn,),
        in_specs=[pl.BlockSpec((MEM_LEN, D_MODEL), lambda j: (0, 0)),
                  pl.BlockSpec((D_MODEL, tn), lambda j: (0, j)),
                  pl.BlockSpec((D_MODEL, tn), lambda j: (0, j))],
        out_specs=(pl.BlockSpec((MEM_LEN, tn), lambda j: (0, j)),
                   pl.BlockSpec((MEM_LEN, tn), lambda j: (0, j))),
        compiler_params=_cparams(("arbitrary",)),
        name="mem_kv",
    )(mem_b, wk_b, wv_b)


def _cross_kernel(h_ref, wq_ref, k_ref, v_ref, wo_ref, g_ref, b_ref, wr_ref, br_ref,
                  h2_ref, lg_ref):
    h1 = h_ref[...]
    q = _dot(h1.astype(BF16), wq_ref[...]).astype(BF16)
    outs = []
    for hd in range(MEM_HEADS):
        sl = slice(hd * MEM_HEAD_DIM, (hd + 1) * MEM_HEAD_DIM)
        s = _dot_nt(q[:, sl], k_ref[:, sl]) * (MEM_HEAD_DIM ** -0.5)
        s = s - jnp.max(s, axis=-1, keepdims=True)
        p = jnp.exp(s)
        p = p / jnp.sum(p, axis=-1, keepdims=True)
        outs.append(_dot(p.astype(BF16), v_ref[:, sl]))
    o = jnp.concatenate(outs, axis=-1).astype(BF16)
    cross = _dot(o, wo_ref[...])
    h2 = _layer_norm(DEEPNORM_ALPHA * h1 + cross, g_ref[...], b_ref[...])
    h2_ref[...] = h2
    lg_ref[...] = _dot(h2.astype(BF16), wr_ref[...]) + br_ref[...]


def _cross_attention(h1, wq_b, k, v, wo_b, g, b, w_route, b_route):
    s = h1.shape[0]
    tm = min(s, 512)
    const = lambda shape: pl.BlockSpec(shape, lambda i: (0, 0), pipeline_mode=pl.Buffered(1))
    return pl.pallas_call(
        _cross_kernel,
        out_shape=(jax.ShapeDtypeStruct((s, D_MODEL), F32),
                   jax.ShapeDtypeStruct((s, LANES), F32)),
        grid=(s // tm,),
        in_specs=[pl.BlockSpec((tm, D_MODEL), lambda i: (i, 0)),
                  const((D_MODEL, D_MODEL)),
                  const((MEM_LEN, D_MODEL)),
                  const((MEM_LEN, D_MODEL)),
                  const((D_MODEL, D_MODEL)),
                  const((1, D_MODEL)),
                  const((1, D_MODEL)),
                  const((D_MODEL, LANES)),
                  const((1, LANES))],
        out_specs=(pl.BlockSpec((tm, D_MODEL), lambda i: (i, 0)),
                   pl.BlockSpec((tm, LANES), lambda i: (i, 0))),
        compiler_params=_cparams(("arbitrary",)),
        name="cross_attn_ln2",
    )(h1, wq_b, k, v, wo_b, g, b, w_route, b_route)


def _route_kernel(lg_ref, slot_ref, slot_t_ref, gate_ref, runs_ref, plan_ref, tot_row, tot_col, gstart):
    phase = pl.program_id(0)
    i = pl.program_id(1)
    tm = lg_ref.shape[0]
    neg = -jnp.inf

    logits = lg_ref[...]
    lane = lax.broadcasted_iota(jnp.int32, (tm, LANES), 1)
    gmask = lane < N_GROUPS
    gl = jnp.where(gmask, logits, neg)
    ge = jnp.exp(gl - jnp.max(gl, axis=-1, keepdims=True))
    pg = ge / jnp.sum(ge, axis=-1, keepdims=True)
    pg_sel = jnp.max(pg, axis=-1, keepdims=True)
    grp = jnp.min(jnp.where((pg == pg_sel) & gmask, lane, LANES), axis=-1, keepdims=True)

    fl_lane = lane - N_GROUPS
    fmask = (fl_lane >= 0) & (fl_lane < N_EXPERTS) & ((fl_lane >> 3) == grp)
    fl = jnp.where(fmask, logits, neg)
    fe = jnp.exp(fl - jnp.max(fl, axis=-1, keepdims=True))
    fp = fe / jnp.sum(fe, axis=-1, keepdims=True)
    p1 = jnp.max(fp, axis=-1, keepdims=True)
    i1 = jnp.min(jnp.where((fp == p1) & fmask, lane, LANES), axis=-1, keepdims=True)
    rest = fmask & (lane != i1)
    fp2 = jnp.where(rest, fp, -1.0)
    p2 = jnp.max(fp2, axis=-1, keepdims=True)
    i2 = jnp.min(jnp.where((fp2 == p2) & rest, lane, LANES), axis=-1, keepdims=True)
    psum = p1 + p2
    gate1 = pg_sel * p1 / psum
    gate2 = pg_sel * p2 / psum

    oh1 = lane == (i1 - N_GROUPS)
    oh2 = lane == (i2 - N_GROUPS)
    oh = (jnp.where(oh1, 1.0, 0.0) + jnp.where(oh2, 1.0, 0.0)).astype(BF16)
    ones = jnp.ones((tm, LANES), BF16)

    def align_up(v, a):
        return jnp.floor((v + (a - 1.0)) * (1.0 / a)) * a

    run_row = align_up(_dot_tn(ones, oh)[0:8, :], RUN_ALIGN)
    rr = lax.broadcasted_iota(jnp.int32, (LANES, LANES), 0)
    cc = lax.broadcasted_iota(jnp.int32, (LANES, LANES), 1)
    strict_upper = jnp.where(rr < cc, 1.0, 0.0).astype(BF16)

    @pl.when((phase == 0) & (i == 0))
    def _():
        tot_row[...] = jnp.zeros_like(tot_row)
        tot_col[...] = jnp.zeros_like(tot_col)

    @pl.when(phase == 0)
    def _():
        tot_row[...] += run_row
        tot_col[...] += align_up(_dot_tn(oh, ones), RUN_ALIGN)

    @pl.when((phase == 1) & (i == 0))
    def _():
        nblk_row = align_up(tot_row[...], MOE_BLK) * (1.0 / MOE_BLK)
        nblk_col = align_up(tot_col[...], MOE_BLK) * (1.0 / MOE_BLK)
        lower_incl = jnp.where(cc <= rr, 1.0, 0.0).astype(BF16)
        base = _dot(nblk_row.astype(BF16), strict_upper) * float(MOE_BLK)
        ends = _dot(lower_incl, nblk_col.astype(BF16))
        expert_rows = rr < N_EXPERTS
        be = jnp.sum(jnp.where(expert_rows & (ends <= cc.astype(F32)), 1.0, 0.0), axis=0, keepdims=True)
        be = jnp.minimum(be, N_EXPERTS - 1.0)
        total = jnp.sum(jnp.where(expert_rows, nblk_col, 0.0), axis=0, keepdims=True)
        sub = lax.broadcasted_iota(jnp.int32, plan_ref.shape, 0)
        plan = jnp.where(sub == 0, jnp.broadcast_to(be, plan_ref.shape),
                         jnp.where(sub == 1, jnp.broadcast_to(total, plan_ref.shape),
                                   jnp.where(sub == 2, base + tot_row[...],
                                             jnp.where(sub == 3, nblk_row,
                                                       nblk_row * float(MOE_BLK) - tot_row[...]))))
        plan_ref[...] = plan.astype(jnp.int32)
        gstart[...] = base

    @pl.when(phase == 1)
    def _():
        off = _dot((run_row * (1.0 / RUN_ALIGN)).astype(BF16), strict_upper) * float(RUN_ALIGN)
        tr = lax.broadcasted_iota(jnp.int32, (tm, tm), 0)
        tc = lax.broadcasted_iota(jnp.int32, (tm, tm), 1)
        strict_lower = jnp.where(tc < tr, 1.0, 0.0).astype(BF16)
        slot = _dot(strict_lower, oh) + off[0:1, :]
        s1 = jnp.sum(jnp.where(oh1, slot, 0.0), axis=-1, keepdims=True)
        s2 = jnp.sum(jnp.where(oh2, slot, 0.0), axis=-1, keepdims=True)
        slots = jnp.where(lane == 0, s1, jnp.where(lane == 1, s2, 0.0))
        slot_ref[...] = slots.astype(jnp.int32)
        slot_t_ref[...] = jnp.transpose(slots)[0:8, :].astype(jnp.int32)
        gate_ref[...] = jnp.where(lane == 0, gate1, jnp.where(lane == 1, gate2, 0.0))
        sub = lax.broadcasted_iota(jnp.int32, runs_ref.shape, 0)
        runs = jnp.where(sub == 0, gstart[...], jnp.where(sub == 1, off, run_row))
        runs_ref[...] = runs.astype(jnp.int32)
        gstart[...] += run_row


def _route(logits):
    t = logits.shape[0]
    tm = min(t, TILE_TOK)
    nt = t // tm
    step = lambda p, i: (i * p, 0)
    return pl.pallas_call(
        _route_kernel,
        out_shape=(jax.ShapeDtypeStruct((t, LANES), jnp.int32),
                   jax.ShapeDtypeStruct((nt * 8, tm), jnp.int32),
                   jax.ShapeDtypeStruct((t, LANES), F32),
                   jax.ShapeDtypeStruct((nt * 8, LANES), jnp.int32),
                   jax.ShapeDtypeStruct((8, LANES), jnp.int32)),
        grid=(2, nt),
        in_specs=[pl.BlockSpec((tm, LANES), lambda p, i: (i, 0))],
        out_specs=(pl.BlockSpec((tm, LANES), step),
                   pl.BlockSpec((8, tm), step),
                   pl.BlockSpec((tm, LANES), step),
                   pl.BlockSpec((8, LANES), step),
                   pl.BlockSpec((8, LANES), lambda p, i: (0, 0))),
        scratch_shapes=[pltpu.VMEM((8, LANES), F32),
                        pltpu.VMEM((LANES, LANES), F32),
                        pltpu.VMEM((8, LANES), F32)],
        compiler_params=_cparams(("arbitrary", "arbitrary")),
        name="moe_route",
    )(logits)


def _run_copies(rg_ref, ro_ref, rn_ref, tile, make_copy):
    def each(action):
        def body(e, carry):
            j = tile * N_EXPERTS + e
            n = rn_ref[j]

            @pl.when(n > 0)
            def _():
                action(make_copy(pl.multiple_of(rg_ref[j], RUN_ALIGN), pl.multiple_of(ro_ref[j], RUN_ALIGN),
                                 pl.multiple_of(n, RUN_ALIGN)))
            return carry
        lax.fori_loop(0, N_EXPERTS, body, 0)
    return each


def _dispatch_kernel(rg_ref, ro_ref, rn_ref, zrow_ref, zlen_ref, nu_ref, x_ref, slot_t_ref, xb_ref,
                     sorted_ref, zero_ref, sems, zsem):
    i = pl.program_id(0)
    n = pl.num_programs(0)
    tm = x_ref.shape[0]
    nb = xb_ref.shape[0] // MOE_BLK
    buf = i % 2

    def zero_copy(row, size):
        return pltpu.make_async_copy(
            zero_ref.at[pl.ds(0, size), :], xb_ref.at[pl.ds(pl.multiple_of(row, RUN_ALIGN), size), :], zsem)

    def pad_copy(e):
        return zero_copy(zrow_ref[e], pl.multiple_of(zlen_ref[e], RUN_ALIGN))

    def runs_of(tile, slot):
        def make_copy(g, o, size):
            return pltpu.make_async_copy(sorted_ref.at[slot, pl.ds(o, size), :],
                                         xb_ref.at[pl.ds(g, size), :], sems.at[slot])
        return _run_copies(rg_ref, ro_ref, rn_ref, tile, make_copy)

    @pl.when(i == 0)
    def _():
        zero_ref[...] = jnp.zeros_like(zero_ref)

        def issue_zero(e, carry):
            @pl.when(zlen_ref[e] > 0)
            def _():
                pad_copy(e).start()
            return carry
        lax.fori_loop(0, N_EXPERTS, issue_zero, 0)
        lax.fori_loop(nu_ref[0], nb, lambda b, c: (zero_copy(b * MOE_BLK, MOE_BLK).start(), c)[1], 0)

        def wait_zero(e, carry):
            @pl.when(zlen_ref[e] > 0)
            def _():
                pad_copy(e).wait()
            return carry
        lax.fori_loop(0, N_EXPERTS, wait_zero, 0)
        lax.fori_loop(nu_ref[0], nb, lambda b, c: (zero_copy(0, MOE_BLK).wait(), c)[1], 0)

    @pl.when(i >= 2)
    def _():
        runs_of(i - 2, buf)(lambda c: c.wait())

    x = x_ref[...].astype(BF16)
    s1 = slot_t_ref[0:1, :]
    s2 = slot_t_ref[1:2, :]
    for rc in range(TILE_SLOTS // SORT_ROWS):
        r = lax.broadcasted_iota(jnp.int32, (SORT_ROWS, tm), 0) + rc * SORT_ROWS
        perm = jnp.where((r == s1) | (r == s2), 1.0, 0.0).astype(BF16)
        sorted_ref[buf, rc * SORT_ROWS:(rc + 1) * SORT_ROWS, :] = _dot(perm, x).astype(BF16)

    runs_of(i, buf)(lambda c: c.start())

    @pl.when(i == n - 1)
    def _():
        @pl.when(i >= 1)
        def _():
            runs_of(i - 1, 1 - buf)(lambda c: c.wait())
        runs_of(i, buf)(lambda c: c.wait())


def _dispatch(rg, ro, rn, zrow, zlen, n_used, h2, slot_t, n_rows):
    t = h2.shape[0]
    tm = min(t, TILE_TOK)
    grid_spec = pltpu.PrefetchScalarGridSpec(
        num_scalar_prefetch=6,
        grid=(t // tm,),
        in_specs=[pl.BlockSpec((tm, D_MODEL), lambda i, *_: (i, 0)),
                  pl.BlockSpec((8, tm), lambda i, *_: (i, 0))],
        out_specs=pl.BlockSpec(memory_space=pl.ANY),
        scratch_shapes=[pltpu.VMEM((2, TILE_SLOTS, D_MODEL), BF16),
                        pltpu.VMEM((MOE_BLK, D_MODEL), BF16),
                        pltpu.SemaphoreType.DMA((2,)),
                        pltpu.SemaphoreType.DMA(())],
    )
    return pl.pallas_call(
        _dispatch_kernel,
        out_shape=jax.ShapeDtypeStruct((n_rows, D_MODEL), BF16),
        grid_spec=grid_spec,
        compiler_params=_cparams(("arbitrary",)),
        name="moe_dispatch",
    )(rg, ro, rn, zrow, zlen, n_used, h2, slot_t)


def _expert_kernel(be_ref, nu_ref, nblk_ref, x_ref, wg_ref, wu_ref, wd_ref, y_ref,
                   wg_f, wu_f, wd_f, wg_b, wu_b, wd_b, sems, cur_ref):
    step = pl.program_id(0)
    n_used = nu_ref[0]

    def weight_copies(expert, slot):
        half = EXPERT_FF // 2
        return [pltpu.make_async_copy(wg_ref.at[expert], wg_f.at[slot], sems.at[slot]),
                pltpu.make_async_copy(wu_ref.at[expert], wu_f.at[slot], sems.at[slot]),
                pltpu.make_async_copy(wd_ref.at[expert, :half], wd_f.at[slot, :half], sems.at[slot]),
                pltpu.make_async_copy(wd_ref.at[expert, half:], wd_f.at[slot, half:], sems.at[slot])]

    def start_weights(expert, slot):
        for j, c in enumerate(weight_copies(expert, slot)):
            c.start(priority=j % 2)

    @pl.when(step == 0)
    def _():
        cur_ref[0] = 0
        start_weights(be_ref[0], 0)

    for sub in range(EXPERT_STEP_BLKS):
        b = step * EXPERT_STEP_BLKS + sub
        used = b < n_used
        e = be_ref[b]
        prev = be_ref[jnp.maximum(b - 1, 0)]
        fresh = used & ((b == 0) | (e != prev))
        rows = slice(sub * MOE_BLK, (sub + 1) * MOE_BLK)

        @pl.when(fresh)
        def _():
            slot = cur_ref[0]
            for c in weight_copies(e, slot):
                c.wait()
            wg_b[...] = wg_f[slot].astype(BF16)
            wu_b[...] = wu_f[slot].astype(BF16)
            wd_b[...] = wd_f[slot].astype(BF16)
            nxt = b + nblk_ref[e]

            @pl.when(nxt < n_used)
            def _():
                start_weights(be_ref[nxt], 1 - slot)
            cur_ref[0] = 1 - slot

        @pl.when(used)
        def _():
            x = x_ref[rows, :]
            hid = _silu(_dot(x, wg_b[...])) * _dot(x, wu_b[...])
            y_ref[rows, :] = _dot(hid.astype(BF16), wd_b[...]).astype(y_ref.dtype)

        @pl.when(jnp.logical_not(used))
        def _():
            y_ref[rows, :] = jnp.zeros((MOE_BLK, D_MODEL), y_ref.dtype)


def _experts(block_e, n_used, nblk, xb, w_gate, w_up, w_down):
    step_rows = EXPERT_STEP_BLKS * MOE_BLK
    any_space = pl.BlockSpec(memory_space=pl.ANY)
    last_step = lambda nu: (nu[0] - 1) // EXPERT_STEP_BLKS
    grid_spec = pltpu.PrefetchScalarGridSpec(
        num_scalar_prefetch=3,
        grid=(xb.shape[0] // step_rows,),
        in_specs=[pl.BlockSpec((step_rows, D_MODEL), lambda s, be, nu, nk: (jnp.minimum(s, last_step(nu)), 0)),
                  any_space, any_space, any_space],
        out_specs=pl.BlockSpec((step_rows, D_MODEL), lambda s, be, nu, nk: (s, 0)),
        scratch_shapes=[pltpu.VMEM((2, D_MODEL, EXPERT_FF), F32),
                        pltpu.VMEM((2, D_MODEL, EXPERT_FF), F32),
                        pltpu.VMEM((2, EXPERT_FF, D_MODEL), F32),
                        pltpu.VMEM((D_MODEL, EXPERT_FF), BF16),
                        pltpu.VMEM((D_MODEL, EXPERT_FF), BF16),
                        pltpu.VMEM((EXPERT_FF, D_MODEL), BF16),
                        pltpu.SemaphoreType.DMA((2,)),
                        pltpu.SMEM((1,), jnp.int32)],
    )
    return pl.pallas_call(
        _expert_kernel,
        out_shape=jax.ShapeDtypeStruct(xb.shape, BF16),
        grid_spec=grid_spec,
        compiler_params=_cparams(("arbitrary",)),
        name="moe_experts",
    )(block_e, n_used, nblk, xb, w_gate, w_up, w_down)


def _combine_kernel(rg_ref, ro_ref, rn_ref, yb_ref, h2_ref, slot_ref, gate_ref, g_ref, b_ref, o_ref,
                    ybuf, sems):
    i = pl.program_id(0)
    n = pl.num_programs(0)
    tm = h2_ref.shape[0]
    buf = i % 2

    def runs_of(tile, slot):
        def make_copy(g, o, size):
            return pltpu.make_async_copy(yb_ref.at[pl.ds(g, size), :],
                                         ybuf.at[slot, pl.ds(o, size), :], sems.at[slot])
        return _run_copies(rg_ref, ro_ref, rn_ref, tile, make_copy)

    @pl.when(i == 0)
    def _():
        ybuf[...] = jnp.zeros_like(ybuf)
        runs_of(0, 0)(lambda c: c.start())

    @pl.when(i + 1 < n)
    def _():
        runs_of(i + 1, 1 - buf)(lambda c: c.start())

    runs_of(i, buf)(lambda c: c.wait())

    lane = lax.broadcasted_iota(jnp.int32, (tm, TILE_SLOTS), 1)
    slots = slot_ref[...]
    gate = gate_ref[...]
    y = ybuf[buf]
    sel = jnp.where(lane == slots[:, 0:1], gate[:, 0:1],
                    jnp.where(lane == slots[:, 1:2], gate[:, 1:2], 0.0)).astype(BF16)
    ffn = _dot(sel, y)
    o_ref[...] = _layer_norm(DEEPNORM_ALPHA * h2_ref[...] + ffn, g_ref[...], b_ref[...])


def _combine_ln(rg, ro, rn, yb, h2, slots, gate, g, b):
    t = h2.shape[0]
    tm = min(t, TILE_TOK)
    grid_spec = pltpu.PrefetchScalarGridSpec(
        num_scalar_prefetch=3,
        grid=(t // tm,),
        in_specs=[pl.BlockSpec(memory_space=pl.ANY),
                  pl.BlockSpec((tm, D_MODEL), lambda i, *_: (i, 0)),
                  pl.BlockSpec((tm, LANES), lambda i, *_: (i, 0)),
                  pl.BlockSpec((tm, LANES), lambda i, *_: (i, 0)),
                  pl.BlockSpec((1, D_MODEL), lambda i, *_: (0, 0)),
                  pl.BlockSpec((1, D_MODEL), lambda i, *_: (0, 0))],
        out_specs=pl.BlockSpec((tm, D_MODEL), lambda i, *_: (i, 0)),
        scratch_shapes=[pltpu.VMEM((2, TILE_SLOTS, D_MODEL), BF16),
                        pltpu.SemaphoreType.DMA((2,))],
    )
    return pl.pallas_call(
        _combine_kernel,
        out_shape=jax.ShapeDtypeStruct((t, D_MODEL), F32),
        grid_spec=grid_spec,
        compiler_params=_cparams(("arbitrary",)),
        name="moe_combine_ln3",
    )(rg, ro, rn, yb, h2, slots, gate, g, b)


def _mixer(x2d, positions, w_in, w_gla_a2, b_gla_a, g_gla_norm):
    s = x2d.shape[0]
    w_in_b = w_in.astype(BF16)
    half = RET_DK // 2
    inv_freq = (ROPE_BASE ** (-jnp.arange(half, dtype=F32) / half)).reshape(1, half)
    cos, sin = _rope_table(positions.reshape(s, 1), inv_freq)
    w_lr = jnp.pad(w_in_b[:, GLR_OFF:], ((0, 0), (0, LANES - GLA_LOWRANK)))
    w_a2 = jnp.pad(w_gla_a2.astype(BF16), ((0, LANES - GLA_LOWRANK), (0, 0)))
    h_main, log_a = _proj_in(x2d, w_in_b, w_lr, w_a2, b_gla_a.reshape(1, -1))
    log_gamma = jnp.log1p(-jnp.exp2(-5.0 - jnp.arange(RET_HEADS, dtype=F32)))
    ret = _retention(h_main, cos, sin, log_gamma)
    gla = _gla(h_main, log_a, g_gla_norm.reshape(1, -1))
    return ret, gla


def _moe(h2, logits, w_gate, w_up, w_down, g, b):
    t = h2.shape[0]
    nt = t // min(t, TILE_TOK)
    slots, slot_t, gate, runs, plan = _route(logits)
    runs = runs.reshape(nt, 8, LANES)[:, :3, :N_EXPERTS]
    rg, ro, rn = (runs[:, j, :].reshape(-1) for j in range(3))
    max_rows = 2 * t + nt * N_EXPERTS * (RUN_ALIGN - 1) + N_EXPERTS * (MOE_BLK - 1)
    nb = -(-max_rows // (MOE_BLK * EXPERT_STEP_BLKS)) * EXPERT_STEP_BLKS
    block_e, n_used = plan[0, :nb], plan[1, :1]
    pad_row, nblk, pad_len = (plan[j, :N_EXPERTS] for j in (2, 3, 4))
    xb = _dispatch(rg, ro, rn, pad_row, pad_len, n_used, h2, slot_t, nb * MOE_BLK)
    yb = _experts(block_e, n_used, nblk, xb, w_gate, w_up, w_down)
    return _combine_ln(rg, ro, rn, yb, h2, slots, gate, g, b)


def kernel(x, mem, positions, w_in, w_gla_a2, b_gla_a, g_gla_norm, w_mix_out, ln1_g, ln1_b, w_mq, w_mk, w_mv, w_mo, ln2_g, ln2_b, w_route_group, b_route_group, w_route_expert, b_route_expert, w_exp_gate, w_exp_up, w_exp_down, ln3_g, ln3_b):
    bsz, s, d = x.shape
    assert bsz == 1 and d == D_MODEL
    x2d = x.reshape(s, d)
    row = lambda v: v.reshape(1, -1)

    ret, gla = _mixer(x2d, positions, w_in[0], w_gla_a2[0], b_gla_a[0], g_gla_norm[0])
    h1 = _mixout_ln(ret, gla, x2d, w_mix_out[0], row(ln1_g[0]), row(ln1_b[0]))

    k, v = _mem_kv(mem[0], w_mk[0], w_mv[0])
    n_route = N_GROUPS + N_EXPERTS
    w_route = jnp.pad(jnp.concatenate([w_route_group[0], w_route_expert[0]], axis=1).astype(BF16),
                      ((0, 0), (0, LANES - n_route)))
    b_route = jnp.pad(jnp.concatenate([b_route_group[0], b_route_expert[0].reshape(-1)]),
                      (0, LANES - n_route)).reshape(1, LANES)
    h2, logits = _cross_attention(h1, w_mq[0].astype(BF16), k, v, w_mo[0].astype(BF16),
                                  row(ln2_g[0]), row(ln2_b[0]), w_route, b_route)

    out = _moe(h2, logits, w_exp_gate[0], w_exp_up[0], w_exp_down[0],
               row(ln3_g[0]), row(ln3_b[0]))
    return out.reshape(bsz, s, d)
```

```python
import math

import jax
import jax.numpy as jnp
from jax import lax
from jax.experimental import pallas as pl
from jax.experimental.pallas import tpu as pltpu

F32 = jnp.float32
BF16 = jnp.bfloat16

D_MODEL = 2048
MEM_LEN = 256
RET_HEADS = 4
RET_DK = 256
RET_DV = 256
GLA_HEADS = 4
GLA_DK = 128
GLA_DV = 256
GLA_LOWRANK = 16
GLA_TAU = 16.0
ROPE_BASE = 10000.0
MEM_HEADS = 4
MEM_HEAD_DIM = D_MODEL // MEM_HEADS
N_GROUPS = 4
EXPERTS_PER_GROUP = 8
N_EXPERTS = N_GROUPS * EXPERTS_PER_GROUP
EXPERT_FF = 512
LN_EPS = 1e-5
DEPTH = 1
DEEPNORM_ALPHA = (2 * DEPTH) ** 0.25

RQ_OFF, RK_OFF, RV_OFF, RG_OFF = 0, 1024, 2048, 3072
GQ_OFF, GK_OFF, GV_OFF, GG_OFF, GLR_OFF = 4096, 4608, 5120, 6144, 7168
IN_MAIN = 7168

LANES = 128
RET_CHUNK = 256
GLA_CHUNK = 128
GLA_LEVELS = 7
MOE_BLK = 256
EXPERT_STEP_BLKS = 4
TILE_TOK = 512
RUN_ALIGN = 16
TILE_SLOTS = 2 * TILE_TOK + N_EXPERTS * RUN_ALIGN
SORT_ROWS = 256
VMEM_LIMIT = 56 * 1024 * 1024


def _cparams(sem):
    return pltpu.CompilerParams(dimension_semantics=sem, vmem_limit_bytes=VMEM_LIMIT)


def _layer_norm(y, g, b):
    mu = jnp.mean(y, axis=-1, keepdims=True)
    d = y - mu
    var = jnp.mean(d * d, axis=-1, keepdims=True)
    return d * lax.rsqrt(var + LN_EPS) * g + b


def _silu(x):
    return x / (1.0 + jnp.exp(-x))


def _dot(a, b):
    return jnp.dot(a, b, preferred_element_type=F32)


def _dot_nt(a, b):
    return lax.dot_general(a, b, (((1,), (1,)), ((), ())), preferred_element_type=F32)


def _dot_tn(a, b):
    return lax.dot_general(a, b, (((0,), (0,)), ((), ())), preferred_element_type=F32)


def _rope_kernel(pos_ref, invf_ref, cos_ref, sin_ref):
    ang = pos_ref[...].astype(F32) * invf_ref[...]
    cos_ref[...] = jnp.cos(ang)
    sin_ref[...] = jnp.sin(ang)


def _rope_table(pos_col, inv_freq):
    s = pos_col.shape[0]
    tm = min(s, 1024)
    half = inv_freq.shape[1]
    return pl.pallas_call(
        _rope_kernel,
        out_shape=(jax.ShapeDtypeStruct((s, half), F32), jax.ShapeDtypeStruct((s, half), F32)),
        grid=(s // tm,),
        in_specs=[pl.BlockSpec((tm, 1), lambda i: (i, 0)),
                  pl.BlockSpec((1, half), lambda i: (0, 0))],
        out_specs=(pl.BlockSpec((tm, half), lambda i: (i, 0)),
                   pl.BlockSpec((tm, half), lambda i: (i, 0))),
        compiler_params=_cparams(("arbitrary",)),
        name="rope_table",
    )(pos_col, inv_freq)


def _proj_in_kernel(x_ref, w_ref, wlr_ref, wa2_ref, ba_ref, o_ref, la_ref, xb_ref):
    @pl.when(pl.program_id(1) == 0)
    def _():
        xb_ref[...] = x_ref[...].astype(BF16)
        glr = _dot(xb_ref[...], wlr_ref[...])
        z = _dot(glr.astype(BF16), wa2_ref[...]) + ba_ref[...]
        la_ref[...] = (jnp.minimum(z, 0.0) - jnp.log(1.0 + jnp.exp(-jnp.abs(z)))) / GLA_TAU

    o_ref[...] = _dot(xb_ref[...], w_ref[...])


def _proj_in(x2d, w_in_b, w_lr, w_a2, b_a):
    s = x2d.shape[0]
    tm = min(s, 1024)
    tn = 1024
    n = GLA_HEADS * GLA_DK
    return pl.pallas_call(
        _proj_in_kernel,
        out_shape=(jax.ShapeDtypeStruct((s, IN_MAIN), F32), jax.ShapeDtypeStruct((s, n), F32)),
        grid=(s // tm, IN_MAIN // tn),
        in_specs=[pl.BlockSpec((tm, D_MODEL), lambda i, j: (i, 0)),
                  pl.BlockSpec((D_MODEL, tn), lambda i, j: (0, j)),
                  pl.BlockSpec((D_MODEL, LANES), lambda i, j: (0, 0)),
                  pl.BlockSpec((LANES, n), lambda i, j: (0, 0)),
                  pl.BlockSpec((1, n), lambda i, j: (0, 0))],
        out_specs=(pl.BlockSpec((tm, tn), lambda i, j: (i, j)),
                   pl.BlockSpec((tm, n), lambda i, j: (i, 0))),
        scratch_shapes=[pltpu.VMEM((tm, D_MODEL), BF16)],
        compiler_params=_cparams(("arbitrary", "arbitrary")),
        name="proj_in",
    )(x2d, w_in_b, w_lr, w_a2, b_a)


def _rotary(t, cos, sin):
    half = t.shape[-1] // 2
    t1, t2 = t[:, :half], t[:, half:]
    return jnp.concatenate([t1 * cos - t2 * sin, t1 * sin + t2 * cos], axis=-1)


def _retention_kernel(lg_ref, q_ref, k_ref, v_ref, g_ref, cos_ref, sin_ref, o_ref, state_ref, intra_ref, dq_ref, dk_ref):
    c = pl.program_id(0)
    C = q_ref.shape[0]

    @pl.when(c == 0)
    def _():
        state_ref[...] = jnp.zeros_like(state_ref)
        ri = lax.broadcasted_iota(jnp.int32, (C, C), 0)
        ci = lax.broadcasted_iota(jnp.int32, (C, C), 1)
        rel = jnp.maximum(ri - ci, 0).astype(F32)
        n = lax.broadcasted_iota(jnp.int32, (C, RET_DK), 0).astype(F32)
        for h in range(RET_HEADS):
            lg = lg_ref[h]
            intra_ref[h] = jnp.where(ri >= ci, jnp.exp(lg * rel), 0.0)
            dq_ref[h] = jnp.exp(lg * (n + 1.0))
            dk_ref[h] = jnp.exp(lg * (C - 1.0 - n))

    cos = cos_ref[...]
    sin = sin_ref[...]

    for h in range(RET_HEADS):
        sl = slice(h * RET_DK, (h + 1) * RET_DK)
        q = _rotary(q_ref[:, sl], cos, sin)
        k = _rotary(k_ref[:, sl], cos, sin) * (RET_DK ** -0.5)
        v = v_ref[:, sl].astype(BF16)
        decay_q = dq_ref[h]
        decay_k = dk_ref[h]
        decay_chunk = jnp.exp(lg_ref[h] * C)

        state = state_ref[h]
        scores = _dot_nt(q.astype(BF16), k.astype(BF16)) * intra_ref[h]
        o = _dot(scores.astype(BF16), v) + _dot((q * decay_q).astype(BF16), state.astype(BF16))
        state_ref[h] = decay_chunk * state + _dot_tn((k * decay_k).astype(BF16), v)

        mu = jnp.mean(o, axis=-1, keepdims=True)
        d = o - mu
        var = jnp.mean(d * d, axis=-1, keepdims=True)
        o = d * lax.rsqrt(var + LN_EPS)
        o_ref[:, sl] = (_silu(g_ref[:, sl]) * o).astype(o_ref.dtype)


def _retention(h_main, cos, sin, log_gamma):
    s = h_main.shape[0]
    C = min(RET_CHUNK, s)
    w = RET_HEADS * RET_DK
    col = lambda off: (lambda c, lg: (c, off // w))
    grid_spec = pltpu.PrefetchScalarGridSpec(
        num_scalar_prefetch=1,
        grid=(s // C,),
        in_specs=[pl.BlockSpec((C, w), col(RQ_OFF)),
                  pl.BlockSpec((C, w), col(RK_OFF)),
                  pl.BlockSpec((C, w), col(RV_OFF)),
                  pl.BlockSpec((C, w), col(RG_OFF)),
                  pl.BlockSpec((C, RET_DK // 2), lambda c, lg: (c, 0)),
                  pl.BlockSpec((C, RET_DK // 2), lambda c, lg: (c, 0))],
        out_specs=pl.BlockSpec((C, RET_HEADS * RET_DV), lambda c, lg: (c, 0)),
        scratch_shapes=[pltpu.VMEM((RET_HEADS, RET_DK, RET_DV), F32),
                        pltpu.VMEM((RET_HEADS, C, C), F32),
                        pltpu.VMEM((RET_HEADS, C, RET_DK), F32),
                        pltpu.VMEM((RET_HEADS, C, RET_DK), F32)],
    )
    return pl.pallas_call(
        _retention_kernel,
        out_shape=jax.ShapeDtypeStruct((s, RET_HEADS * RET_DV), BF16),
        grid_spec=grid_spec,
        compiler_params=_cparams(("arbitrary",)),
        name="retention",
    )(log_gamma, h_main, h_main, h_main, h_main, cos, sin)


def _gla_decay_matrix(C):
    import numpy as np
    levels = int(math.log2(C))
    r = np.arange(C)[:, None]
    t = np.arange(C)[None, :]
    mats = []
    for l in range(levels):
        blk = C >> l
        half = blk // 2
        m = (r // blk) * blk + half - 1
        qside = (r % blk) >= half
        mats.append(np.where(qside, (t > m) & (t <= r), (t > r) & (t <= m)))
    mats.append(t <= r)
    mats.append(t > r)
    return np.concatenate(mats, axis=0).astype(np.float32)


def _gla_kernel(m_ref, q_ref, k_ref, v_ref, g_ref, la_ref, gn_ref, o_ref, state_ref):
    c = pl.program_id(0)
    C = q_ref.shape[0]
    levels = GLA_LEVELS

    @pl.when(c == 0)
    def _():
        state_ref[...] = jnp.zeros_like(state_ref)

    m = m_ref[...]
    ri = lax.broadcasted_iota(jnp.int32, (C, C), 0)
    ci = lax.broadcasted_iota(jnp.int32, (C, C), 1)
    xor = jnp.where(ri > ci, ri ^ ci, 0)
    row = lax.broadcasted_iota(jnp.int32, (C, 1), 0)

    heads = range(GLA_HEADS)
    ks = [slice(h * GLA_DK, (h + 1) * GLA_DK) for h in heads]
    vs = [slice(h * GLA_DV, (h + 1) * GLA_DV) for h in heads]

    expo = []
    for h in heads:
        la = la_ref[:, ks[h]]
        la_hi = la.astype(BF16)
        la_lo = (la - la_hi.astype(F32)).astype(BF16)
        expo.append(jnp.exp(_dot(m, la_hi) + _dot(m, la_lo)))

    q = [q_ref[:, ks[h]] * (GLA_DK ** -0.5) for h in heads]
    k = [k_ref[:, ks[h]] for h in heads]
    v = [v_ref[:, vs[h]].astype(BF16) for h in heads]

    scores = [jnp.where(ri == ci, _dot_nt(q[h].astype(BF16), k[h].astype(BF16)), 0.0) for h in heads]
    for l in range(levels):
        half = C >> (l + 1)
        shift = int(math.log2(half))
        qside = (row & half) != 0
        keep = (xor >> shift) == 1
        for h in heads:
            x = (jnp.where(qside, q[h], k[h]) * expo[h][l * C:(l + 1) * C]).astype(BF16)
            scores[h] = scores[h] + jnp.where(keep, _dot_nt(x, x), 0.0)

    outs = []
    for h in heads:
        e_b = expo[h][levels * C:(levels + 1) * C]
        e_rev = expo[h][(levels + 1) * C:(levels + 2) * C]
        e_last = e_b[C - 1:C, :]
        state = state_ref[h]
        outs.append(_dot(scores[h].astype(BF16), v[h])
                    + _dot_nt((q[h] * e_b).astype(BF16), state.astype(BF16)))
        state_ref[h] = state * e_last + _dot_tn(v[h], (k[h] * e_rev).astype(BF16))

    for h in heads:
        o = outs[h]
        o = o * lax.rsqrt(jnp.mean(o * o, axis=-1, keepdims=True) + LN_EPS) * gn_ref[...]
        o_ref[:, vs[h]] = (_silu(g_ref[:, vs[h]]) * o).astype(o_ref.dtype)


def _gla(h_main, log_a, g_norm):
    s = h_main.shape[0]
    C = GLA_CHUNK
    m = jnp.asarray(_gla_decay_matrix(C), dtype=BF16)
    nrow = m.shape[0]
    wk = GLA_HEADS * GLA_DK
    wv = GLA_HEADS * GLA_DV
    return pl.pallas_call(
        _gla_kernel,
        out_shape=jax.ShapeDtypeStruct((s, wv), BF16),
        grid=(s // C,),
        in_specs=[pl.BlockSpec((nrow, C), lambda c: (0, 0)),
                  pl.BlockSpec((C, wk), lambda c: (c, GQ_OFF // wk)),
                  pl.BlockSpec((C, wk), lambda c: (c, GK_OFF // wk)),
                  pl.BlockSpec((C, wv), lambda c: (c, GV_OFF // wv)),
                  pl.BlockSpec((C, wv), lambda c: (c, GG_OFF // wv)),
                  pl.BlockSpec((C, wk), lambda c: (c, 0)),
                  pl.BlockSpec((1, GLA_DV), lambda c: (0, 0))],
        out_specs=pl.BlockSpec((C, wv), lambda c: (c, 0)),
        scratch_shapes=[pltpu.VMEM((GLA_HEADS, GLA_DV, GLA_DK), F32)],
        compiler_params=_cparams(("arbitrary",)),
        name="gla",
    )(m, h_main, h_main, h_main, h_main, log_a, g_norm)


def _mixout_kernel(ret_ref, gla_ref, x_ref, w_ref, g_ref, b_ref, o_ref, wb_ref):
    nr = ret_ref.shape[1]

    @pl.when(pl.program_id(0) == 0)
    def _():
        wb_ref[...] = w_ref[...].astype(BF16)

    mix = _dot(ret_ref[...], wb_ref[:nr, :]) + _dot(gla_ref[...], wb_ref[nr:, :])
    o_ref[...] = _layer_norm(DEEPNORM_ALPHA * x_ref[...] + mix, g_ref[...], b_ref[...])


def _mixout_ln(ret, gla, x2d, w_b, g, b):
    s = x2d.shape[0]
    tm = min(s, 512)
    nr, ng = ret.shape[1], gla.shape[1]
    return pl.pallas_call(
        _mixout_kernel,
        out_shape=jax.ShapeDtypeStruct((s, D_MODEL), F32),
        grid=(s // tm,),
        in_specs=[pl.BlockSpec((tm, nr), lambda i: (i, 0)),
                  pl.BlockSpec((tm, ng), lambda i: (i, 0)),
                  pl.BlockSpec((tm, D_MODEL), lambda i: (i, 0)),
                  pl.BlockSpec((nr + ng, D_MODEL), lambda i: (0, 0), pipeline_mode=pl.Buffered(1)),
                  pl.BlockSpec((1, D_MODEL), lambda i: (0, 0)),
                  pl.BlockSpec((1, D_MODEL), lambda i: (0, 0))],
        out_specs=pl.BlockSpec((tm, D_MODEL), lambda i: (i, 0)),
        scratch_shapes=[pltpu.VMEM((nr + ng, D_MODEL), BF16)],
        compiler_params=_cparams(("arbitrary",)),
        name="mixout_ln1",
    )(ret, gla, x2d, w_b, g, b)


def _kv_kernel(mem_ref, wk_ref, wv_ref, k_ref, v_ref):
    m = mem_ref[...].astype(BF16)
    k_ref[...] = _dot(m, wk_ref[...].astype(BF16)).astype(k_ref.dtype)
    v_ref[...] = _dot(m, wv_ref[...].astype(BF16)).astype(v_ref.dtype)


def _mem_kv(mem_b, wk_b, wv_b):
    tn = 512
    return pl.pallas_call(
        _kv_kernel,
        out_shape=(jax.ShapeDtypeStruct((MEM_LEN, D_MODEL), BF16),
                   jax.ShapeDtypeStruct((MEM_LEN, D_MODEL), BF16)),
        grid=(D_MODEL // tn,),
        in_specs=[pl.BlockSpec((MEM_LEN, D_MODEL), lambda j: (0, 0)),
                  pl.BlockSpec((D_MODEL, tn), lambda j: (0, j)),
                  pl.BlockSpec((D_MODEL, tn), lambda j: (0, j))],
        out_specs=(pl.BlockSpec((MEM_LEN, tn), lambda j: (0, j)),
                   pl.BlockSpec((MEM_LEN, tn), lambda j: (0, j))),
        compiler_params=_cparams(("arbitrary",)),
        name="mem_kv",
    )(mem_b, wk_b, wv_b)


def _cross_kernel(h_ref, wq_ref, k_ref, v_ref, wo_ref, g_ref, b_ref, wr_ref, br_ref,
                  h2_ref, lg_ref):
    h1 = h_ref[...]
    q = _dot(h1.astype(BF16), wq_ref[...]).astype(BF16)
    outs = []
    for hd in range(MEM_HEADS):
        sl = slice(hd * MEM_HEAD_DIM, (hd + 1) * MEM_HEAD_DIM)
        s = _dot_nt(q[:, sl], k_ref[:, sl]) * (MEM_HEAD_DIM ** -0.5)
        s = s - jnp.max(s, axis=-1, keepdims=True)
        p = jnp.exp(s)
        p = p / jnp.sum(p, axis=-1, keepdims=True)
        outs.append(_dot(p.astype(BF16), v_ref[:, sl]))
    o = jnp.concatenate(outs, axis=-1).astype(BF16)
    cross = _dot(o, wo_ref[...])
    h2 = _layer_norm(DEEPNORM_ALPHA * h1 + cross, g_ref[...], b_ref[...])
    h2_ref[...] = h2
    lg_ref[...] = _dot(h2.astype(BF16), wr_ref[...]) + br_ref[...]


def _cross_attention(h1, wq_b, k, v, wo_b, g, b, w_route, b_route):
    s = h1.shape[0]
    tm = min(s, 512)
    const = lambda shape: pl.BlockSpec(shape, lambda i: (0, 0), pipeline_mode=pl.Buffered(1))
    return pl.pallas_call(
        _cross_kernel,
        out_shape=(jax.ShapeDtypeStruct((s, D_MODEL), F32),
                   jax.ShapeDtypeStruct((s, LANES), F32)),
        grid=(s // tm,),
        in_specs=[pl.BlockSpec((tm, D_MODEL), lambda i: (i, 0)),
                  const((D_MODEL, D_MODEL)),
                  const((MEM_LEN, D_MODEL)),
                  const((MEM_LEN, D_MODEL)),
                  const((D_MODEL, D_MODEL)),
                  const((1, D_MODEL)),
                  const((1, D_MODEL)),
                  const((D_MODEL, LANES)),
                  const((1, LANES))],
        out_specs=(pl.BlockSpec((tm, D_MODEL), lambda i: (i, 0)),
                   pl.BlockSpec((tm, LANES), lambda i: (i, 0))),
        compiler_params=_cparams(("arbitrary",)),
        name="cross_attn_ln2",
    )(h1, wq_b, k, v, wo_b, g, b, w_route, b_route)


def _route_kernel(lg_ref, slot_ref, slot_t_ref, gate_ref, runs_ref, plan_ref, tot_row, tot_col, gstart, rinfo):
    phase = pl.program_id(0)
    i = pl.program_id(1)
    tm = lg_ref.shape[0]
    lane = lax.broadcasted_iota(jnp.int32, (tm, LANES), 1)
    rows = pl.ds(pl.multiple_of(i * tm, tm), tm)

    @pl.when(phase == 0)
    def _():
        neg = -jnp.inf
        logits = lg_ref[...]
        gmask = lane < N_GROUPS
        gl = jnp.where(gmask, logits, neg)
        ge = jnp.exp(gl - jnp.max(gl, axis=-1, keepdims=True))
        pg = ge / jnp.sum(ge, axis=-1, keepdims=True)
        pg_sel = jnp.max(pg, axis=-1, keepdims=True)
        grp = jnp.min(jnp.where((pg == pg_sel) & gmask, lane, LANES), axis=-1, keepdims=True)

        fl_lane = lane - N_GROUPS
        fmask = (fl_lane >= 0) & (fl_lane < N_EXPERTS) & ((fl_lane >> 3) == grp)
        fl = jnp.where(fmask, logits, neg)
        fe = jnp.exp(fl - jnp.max(fl, axis=-1, keepdims=True))
        fp = fe / jnp.sum(fe, axis=-1, keepdims=True)
        p1 = jnp.max(fp, axis=-1, keepdims=True)
        i1 = jnp.min(jnp.where((fp == p1) & fmask, lane, LANES), axis=-1, keepdims=True)
        rest = fmask & (lane != i1)
        fp2 = jnp.where(rest, fp, -1.0)
        p2 = jnp.max(fp2, axis=-1, keepdims=True)
        i2 = jnp.min(jnp.where((fp2 == p2) & rest, lane, LANES), axis=-1, keepdims=True)
        psum = p1 + p2
        picks = ((i1 - N_GROUPS).astype(F32), (i2 - N_GROUPS).astype(F32),
                 pg_sel * p1 / psum, pg_sel * p2 / psum)
        info = jnp.zeros((tm, LANES), F32)
        for j, val in enumerate(picks):
            info = jnp.where(lane == j, val, info)
        rinfo[rows, :] = info

    info = rinfo[rows, :]
    oh1 = lane == info[:, 0:1].astype(jnp.int32)
    oh2 = lane == info[:, 1:2].astype(jnp.int32)
    gate1 = info[:, 2:3]
    gate2 = info[:, 3:4]
    oh = (jnp.where(oh1, 1.0, 0.0) + jnp.where(oh2, 1.0, 0.0)).astype(BF16)
    ones = jnp.ones((tm, LANES), BF16)

    def align_up(v, a):
        return jnp.floor((v + (a - 1.0)) * (1.0 / a)) * a

    run_row = align_up(_dot_tn(ones, oh)[0:8, :], RUN_ALIGN)
    rr = lax.broadcasted_iota(jnp.int32, (LANES, LANES), 0)
    cc = lax.broadcasted_iota(jnp.int32, (LANES, LANES), 1)
    strict_upper = jnp.where(rr < cc, 1.0, 0.0).astype(BF16)

    @pl.when((phase == 0) & (i == 0))
    def _():
        tot_row[...] = jnp.zeros_like(tot_row)
        tot_col[...] = jnp.zeros_like(tot_col)

    @pl.when(phase == 0)
    def _():
        tot_row[...] += run_row
        tot_col[...] += align_up(_dot_tn(oh, ones), RUN_ALIGN)

    @pl.when((phase == 1) & (i == 0))
    def _():
        nblk_row = align_up(tot_row[...], MOE_BLK) * (1.0 / MOE_BLK)
        nblk_col = align_up(tot_col[...], MOE_BLK) * (1.0 / MOE_BLK)
        lower_incl = jnp.where(cc <= rr, 1.0, 0.0).astype(BF16)
        base = _dot(nblk_row.astype(BF16), strict_upper) * float(MOE_BLK)
        ends = _dot(lower_incl, nblk_col.astype(BF16))
        expert_rows = rr < N_EXPERTS
        be = jnp.sum(jnp.where(expert_rows & (ends <= cc.astype(F32)), 1.0, 0.0), axis=0, keepdims=True)
        be = jnp.minimum(be, N_EXPERTS - 1.0)
        total = jnp.sum(jnp.where(expert_rows, nblk_col, 0.0), axis=0, keepdims=True)
        sub = lax.broadcasted_iota(jnp.int32, plan_ref.shape, 0)
        plan = jnp.where(sub == 0, jnp.broadcast_to(be, plan_ref.shape),
                         jnp.where(sub == 1, jnp.broadcast_to(total, plan_ref.shape),
                                   jnp.where(sub == 2, base + tot_row[...],
                                             jnp.where(sub == 3, nblk_row,
                                                       nblk_row * float(MOE_BLK) - tot_row[...]))))
        plan_ref[...] = plan.astype(jnp.int32)
        gstart[...] = base

    @pl.when(phase == 1)
    def _():
        off = _dot((run_row * (1.0 / RUN_ALIGN)).astype(BF16), strict_upper) * float(RUN_ALIGN)
        tr = lax.broadcasted_iota(jnp.int32, (tm, tm), 0)
        tc = lax.broadcasted_iota(jnp.int32, (tm, tm), 1)
        strict_lower = jnp.where(tc < tr, 1.0, 0.0).astype(BF16)
        slot = _dot(strict_lower, oh) + off[0:1, :]
        s1 = jnp.sum(jnp.where(oh1, slot, 0.0), axis=-1, keepdims=True)
        s2 = jnp.sum(jnp.where(oh2, slot, 0.0), axis=-1, keepdims=True)
        slots = jnp.where(lane == 0, s1, jnp.where(lane == 1, s2, 0.0))
        slot_ref[...] = slots.astype(jnp.int32)
        slot_t_ref[...] = jnp.transpose(slots)[0:8, :].astype(jnp.int32)
        gate_ref[...] = jnp.where(lane == 0, gate1, jnp.where(lane == 1, gate2, 0.0))
        sub = lax.broadcasted_iota(jnp.int32, runs_ref.shape, 0)
        runs = jnp.where(sub == 0, gstart[...], jnp.where(sub == 1, off, run_row))
        runs_ref[...] = runs.astype(jnp.int32)
        gstart[...] += run_row


def _route(logits):
    t = logits.shape[0]
    tm = min(t, TILE_TOK)
    nt = t // tm
    step = lambda p, i: (i * p, 0)
    return pl.pallas_call(
        _route_kernel,
        out_shape=(jax.ShapeDtypeStruct((t, LANES), jnp.int32),
                   jax.ShapeDtypeStruct((nt * 8, tm), jnp.int32),
                   jax.ShapeDtypeStruct((t, LANES), F32),
                   jax.ShapeDtypeStruct((nt * 8, LANES), jnp.int32),
                   jax.ShapeDtypeStruct((8, LANES), jnp.int32)),
        grid=(2, nt),
        in_specs=[pl.BlockSpec((tm, LANES), lambda p, i: (i, 0))],
        out_specs=(pl.BlockSpec((tm, LANES), step),
                   pl.BlockSpec((8, tm), step),
                   pl.BlockSpec((tm, LANES), step),
                   pl.BlockSpec((8, LANES), step),
                   pl.BlockSpec((8, LANES), lambda p, i: (0, 0))),
        scratch_shapes=[pltpu.VMEM((8, LANES), F32),
                        pltpu.VMEM((LANES, LANES), F32),
                        pltpu.VMEM((8, LANES), F32),
                        pltpu.VMEM((t, LANES), F32)],
        compiler_params=_cparams(("arbitrary", "arbitrary")),
        name="moe_route",
    )(logits)


def _run_copies(rg_ref, ro_ref, rn_ref, tile, make_copy):
    def each(action):
        def body(e, carry):
            j = tile * N_EXPERTS + e
            n = rn_ref[j]

            @pl.when(n > 0)
            def _():
                action(make_copy(pl.multiple_of(rg_ref[j], RUN_ALIGN), pl.multiple_of(ro_ref[j], RUN_ALIGN),
                                 pl.multiple_of(n, RUN_ALIGN)))
            return carry
        lax.fori_loop(0, N_EXPERTS, body, 0)
    return each


def _dispatch_kernel(rg_ref, ro_ref, rn_ref, zrow_ref, zlen_ref, nu_ref, x_ref, slot_t_ref, xb_ref,
                     sorted_ref, zero_ref, sems, zsem):
    i = pl.program_id(0)
    n = pl.num_programs(0)
    tm = x_ref.shape[0]
    nb = xb_ref.shape[0] // MOE_BLK
    buf = i % 2

    def zero_copy(row, size):
        return pltpu.make_async_copy(
            zero_ref.at[pl.ds(0, size), :], xb_ref.at[pl.ds(pl.multiple_of(row, RUN_ALIGN), size), :], zsem)

    def pad_copy(e):
        return zero_copy(zrow_ref[e], pl.multiple_of(zlen_ref[e], RUN_ALIGN))

    def runs_of(tile, slot):
        def make_copy(g, o, size):
            return pltpu.make_async_copy(sorted_ref.at[slot, pl.ds(o, size), :],
                                         xb_ref.at[pl.ds(g, size), :], sems.at[slot])
        return _run_copies(rg_ref, ro_ref, rn_ref, tile, make_copy)

    @pl.when(i == 0)
    def _():
        zero_ref[...] = jnp.zeros_like(zero_ref)

        def issue_zero(e, carry):
            @pl.when(zlen_ref[e] > 0)
            def _():
                pad_copy(e).start()
            return carry
        lax.fori_loop(0, N_EXPERTS, issue_zero, 0)
        lax.fori_loop(nu_ref[0], nb, lambda b, c: (zero_copy(b * MOE_BLK, MOE_BLK).start(), c)[1], 0)

        def wait_zero(e, carry):
            @pl.when(zlen_ref[e] > 0)
            def _():
                pad_copy(e).wait()
            return carry
        lax.fori_loop(0, N_EXPERTS, wait_zero, 0)
        lax.fori_loop(nu_ref[0], nb, lambda b, c: (zero_copy(0, MOE_BLK).wait(), c)[1], 0)

    @pl.when(i >= 2)
    def _():
        runs_of(i - 2, buf)(lambda c: c.wait())

    x = x_ref[...].astype(BF16)
    s1 = slot_t_ref[0:1, :]
    s2 = slot_t_ref[1:2, :]
    for rc in range(TILE_SLOTS // SORT_ROWS):
        r = lax.broadcasted_iota(jnp.int32, (SORT_ROWS, tm), 0) + rc * SORT_ROWS
        perm = jnp.where((r == s1) | (r == s2), 1.0, 0.0).astype(BF16)
        sorted_ref[buf, rc * SORT_ROWS:(rc + 1) * SORT_ROWS, :] = _dot(perm, x).astype(BF16)

    runs_of(i, buf)(lambda c: c.start())

    @pl.when(i == n - 1)
    def _():
        @pl.when(i >= 1)
        def _():
            runs_of(i - 1, 1 - buf)(lambda c: c.wait())
        runs_of(i, buf)(lambda c: c.wait())


def _dispatch(rg, ro, rn, zrow, zlen, n_used, h2, slot_t, n_rows):
    t = h2.shape[0]
    tm = min(t, TILE_TOK)
    grid_spec = pltpu.PrefetchScalarGridSpec(
        num_scalar_prefetch=6,
        grid=(t // tm,),
        in_specs=[pl.BlockSpec((tm, D_MODEL), lambda i, *_: (i, 0)),
                  pl.BlockSpec((8, tm), lambda i, *_: (i, 0))],
        out_specs=pl.BlockSpec(memory_space=pl.ANY),
        scratch_shapes=[pltpu.VMEM((2, TILE_SLOTS, D_MODEL), BF16),
                        pltpu.VMEM((MOE_BLK, D_MODEL), BF16),
                        pltpu.SemaphoreType.DMA((2,)),
                        pltpu.SemaphoreType.DMA(())],
    )
    return pl.pallas_call(
        _dispatch_kernel,
        out_shape=jax.ShapeDtypeStruct((n_rows, D_MODEL), BF16),
        grid_spec=grid_spec,
        compiler_params=_cparams(("arbitrary",)),
        name="moe_dispatch",
    )(rg, ro, rn, zrow, zlen, n_used, h2, slot_t)


def _expert_kernel(be_ref, nu_ref, nblk_ref, x_ref, wg_ref, wu_ref, wd_ref, y_ref,
                   wg_f, wu_f, wd_f, wg_b, wu_b, wd_b, sems, cur_ref):
    step = pl.program_id(0)
    n_used = nu_ref[0]

    def weight_copies(expert, slot):
        half = EXPERT_FF // 2
        return [pltpu.make_async_copy(wg_ref.at[expert], wg_f.at[slot], sems.at[slot]),
                pltpu.make_async_copy(wu_ref.at[expert], wu_f.at[slot], sems.at[slot]),
                pltpu.make_async_copy(wd_ref.at[expert, :half], wd_f.at[slot, :half], sems.at[slot]),
                pltpu.make_async_copy(wd_ref.at[expert, half:], wd_f.at[slot, half:], sems.at[slot])]

    def start_weights(expert, slot):
        for j, c in enumerate(weight_copies(expert, slot)):
            c.start(priority=j % 2)

    def after(blk):
        return blk + nblk_ref[be_ref[blk]]

    @pl.when(step == 0)
    def _():
        cur_ref[0] = 0
        start_weights(be_ref[0], 0)

        @pl.when(after(0) < n_used)
        def _():
            start_weights(be_ref[after(0)], 1)

    for sub in range(EXPERT_STEP_BLKS):
        b = step * EXPERT_STEP_BLKS + sub
        used = b < n_used
        e = be_ref[b]
        prev = be_ref[jnp.maximum(b - 1, 0)]
        fresh = used & ((b == 0) | (e != prev))
        rows = slice(sub * MOE_BLK, (sub + 1) * MOE_BLK)

        @pl.when(fresh)
        def _():
            slot = cur_ref[0]
            for c in weight_copies(e, slot):
                c.wait()
            wg_b[...] = wg_f[slot].astype(BF16)
            wu_b[...] = wu_f[slot].astype(BF16)
            wd_b[...] = wd_f[slot].astype(BF16)
            nxt = after(b)

            @pl.when(nxt < n_used)
            def _():
                nxt2 = after(nxt)

                @pl.when(nxt2 < n_used)
                def _():
                    start_weights(be_ref[nxt2], slot)
            cur_ref[0] = 1 - slot

        @pl.when(used)
        def _():
            x = x_ref[rows, :]
            hid = _silu(_dot(x, wg_b[...])) * _dot(x, wu_b[...])
            y_ref[rows, :] = _dot(hid.astype(BF16), wd_b[...]).astype(y_ref.dtype)

        @pl.when(jnp.logical_not(used))
        def _():
            y_ref[rows, :] = jnp.zeros((MOE_BLK, D_MODEL), y_ref.dtype)


def _experts(block_e, n_used, nblk, xb, w_gate, w_up, w_down):
    step_rows = EXPERT_STEP_BLKS * MOE_BLK
    any_space = pl.BlockSpec(memory_space=pl.ANY)
    last_step = lambda nu: (nu[0] - 1) // EXPERT_STEP_BLKS
    grid_spec = pltpu.PrefetchScalarGridSpec(
        num_scalar_prefetch=3,
        grid=(xb.shape[0] // step_rows,),
        in_specs=[pl.BlockSpec((step_rows, D_MODEL), lambda s, be, nu, nk: (jnp.minimum(s, last_step(nu)), 0)),
                  any_space, any_space, any_space],
        out_specs=pl.BlockSpec((step_rows, D_MODEL), lambda s, be, nu, nk: (s, 0)),
        scratch_shapes=[pltpu.VMEM((2, D_MODEL, EXPERT_FF), F32),
                        pltpu.VMEM((2, D_MODEL, EXPERT_FF), F32),
                        pltpu.VMEM((2, EXPERT_FF, D_MODEL), F32),
                        pltpu.VMEM((D_MODEL, EXPERT_FF), BF16),
                        pltpu.VMEM((D_MODEL, EXPERT_FF), BF16),
                        pltpu.VMEM((EXPERT_FF, D_MODEL), BF16),
                        pltpu.SemaphoreType.DMA((2,)),
                        pltpu.SMEM((1,), jnp.int32)],
    )
    return pl.pallas_call(
        _expert_kernel,
        out_shape=jax.ShapeDtypeStruct(xb.shape, BF16),
        grid_spec=grid_spec,
        compiler_params=_cparams(("arbitrary",)),
        name="moe_experts",
    )(block_e, n_used, nblk, xb, w_gate, w_up, w_down)


def _combine_kernel(rg_ref, ro_ref, rn_ref, yb_ref, h2_ref, slot_ref, gate_ref, g_ref, b_ref, o_ref,
                    ybuf, sems):
    i = pl.program_id(0)
    n = pl.num_programs(0)
    tm = h2_ref.shape[0]
    buf = i % 2

    def runs_of(tile, slot):
        def make_copy(g, o, size):
            return pltpu.make_async_copy(yb_ref.at[pl.ds(g, size), :],
                                         ybuf.at[slot, pl.ds(o, size), :], sems.at[slot])
        return _run_copies(rg_ref, ro_ref, rn_ref, tile, make_copy)

    @pl.when(i == 0)
    def _():
        ybuf[...] = jnp.zeros_like(ybuf)
        runs_of(0, 0)(lambda c: c.start())

    @pl.when(i + 1 < n)
    def _():
        runs_of(i + 1, 1 - buf)(lambda c: c.start())

    runs_of(i, buf)(lambda c: c.wait())

    lane = lax.broadcasted_iota(jnp.int32, (tm, TILE_SLOTS), 1)
    slots = slot_ref[...]
    gate = gate_ref[...]
    y = ybuf[buf]
    sel = jnp.where(lane == slots[:, 0:1], gate[:, 0:1],
                    jnp.where(lane == slots[:, 1:2], gate[:, 1:2], 0.0)).astype(BF16)
    ffn = _dot(sel, y)
    o_ref[...] = _layer_norm(DEEPNORM_ALPHA * h2_ref[...] + ffn, g_ref[...], b_ref[...])


def _combine_ln(rg, ro, rn, yb, h2, slots, gate, g, b):
    t = h2.shape[0]
    tm = min(t, TILE_TOK)
    grid_spec = pltpu.PrefetchScalarGridSpec(
        num_scalar_prefetch=3,
        grid=(t // tm,),
        in_specs=[pl.BlockSpec(memory_space=pl.ANY),
                  pl.BlockSpec((tm, D_MODEL), lambda i, *_: (i, 0)),
                  pl.BlockSpec((tm, LANES), lambda i, *_: (i, 0)),
                  pl.BlockSpec((tm, LANES), lambda i, *_: (i, 0)),
                  pl.BlockSpec((1, D_MODEL), lambda i, *_: (0, 0)),
                  pl.BlockSpec((1, D_MODEL), lambda i, *_: (0, 0))],
        out_specs=pl.BlockSpec((tm, D_MODEL), lambda i, *_: (i, 0)),
        scratch_shapes=[pltpu.VMEM((2, TILE_SLOTS, D_MODEL), BF16),
                        pltpu.SemaphoreType.DMA((2,))],
    )
    return pl.pallas_call(
        _combine_kernel,
        out_shape=jax.ShapeDtypeStruct((t, D_MODEL), F32),
        grid_spec=grid_spec,
        compiler_params=_cparams(("arbitrary",)),
        name="moe_combine_ln3",
    )(rg, ro, rn, yb, h2, slots, gate, g, b)


def _mixer(x2d, positions, w_in, w_gla_a2, b_gla_a, g_gla_norm):
    s = x2d.shape[0]
    w_in_b = w_in.astype(BF16)
    half = RET_DK // 2
    inv_freq = (ROPE_BASE ** (-jnp.arange(half, dtype=F32) / half)).reshape(1, half)
    cos, sin = _rope_table(positions.reshape(s, 1), inv_freq)
    w_lr = jnp.pad(w_in_b[:, GLR_OFF:], ((0, 0), (0, LANES - GLA_LOWRANK)))
    w_a2 = jnp.pad(w_gla_a2.astype(BF16), ((0, LANES - GLA_LOWRANK), (0, 0)))
    h_main, log_a = _proj_in(x2d, w_in_b, w_lr, w_a2, b_gla_a.reshape(1, -1))
    log_gamma = jnp.log1p(-jnp.exp2(-5.0 - jnp.arange(RET_HEADS, dtype=F32)))
    ret = _retention(h_main, cos, sin, log_gamma)
    gla = _gla(h_main, log_a, g_gla_norm.reshape(1, -1))
    return ret, gla


def _moe(h2, logits, w_gate, w_up, w_down, g, b):
    t = h2.shape[0]
    nt = t // min(t, TILE_TOK)
    slots, slot_t, gate, runs, plan = _route(logits)
    runs = runs.reshape(nt, 8, LANES)[:, :3, :N_EXPERTS]
    rg, ro, rn = (runs[:, j, :].reshape(-1) for j in range(3))
    max_rows = 2 * t + nt * N_EXPERTS * (RUN_ALIGN - 1) + N_EXPERTS * (MOE_BLK - 1)
    nb = -(-max_rows // (MOE_BLK * EXPERT_STEP_BLKS)) * EXPERT_STEP_BLKS
    block_e, n_used = plan[0, :nb], plan[1, :1]
    pad_row, nblk, pad_len = (plan[j, :N_EXPERTS] for j in (2, 3, 4))
    xb = _dispatch(rg, ro, rn, pad_row, pad_len, n_used, h2, slot_t, nb * MOE_BLK)
    yb = _experts(block_e, n_used, nblk, xb, w_gate, w_up, w_down)
    return _combine_ln(rg, ro, rn, yb, h2, slots, gate, g, b)


def kernel(x, mem, positions, w_in, w_gla_a2, b_gla_a, g_gla_norm, w_mix_out, ln1_g, ln1_b, w_mq, w_mk, w_mv, w_mo, ln2_g, ln2_b, w_route_group, b_route_group, w_route_expert, b_route_expert, w_exp_gate, w_exp_up, w_exp_down, ln3_g, ln3_b):
    bsz, s, d = x.shape
    assert bsz == 1 and d == D_MODEL
    x2d = x.reshape(s, d)
    row = lambda v: v.reshape(1, -1)

    ret, gla = _mixer(x2d, positions, w_in[0], w_gla_a2[0], b_gla_a[0], g_gla_norm[0])
    h1 = _mixout_ln(ret, gla, x2d, w_mix_out[0], row(ln1_g[0]), row(ln1_b[0]))

    k, v = _mem_kv(mem[0], w_mk[0], w_mv[0])
    n_route = N_GROUPS + N_EXPERTS
    w_route = jnp.pad(jnp.concatenate([w_route_group[0], w_route_expert[0]], axis=1).astype(BF16),
                      ((0, 0), (0, LANES - n_route)))
    b_route = jnp.pad(jnp.concatenate([b_route_group[0], b_route_expert[0].reshape(-1)]),
                      (0, LANES - n_route)).reshape(1, LANES)
    h2, logits = _cross_attention(h1, w_mq[0].astype(BF16), k, v, w_mo[0].astype(BF16),
                                  row(ln2_g[0]), row(ln2_b[0]), w_route, b_route)

    out = _moe(h2, logits, w_exp_gate[0], w_exp_up[0], w_exp_down[0],
               row(ln3_g[0]), row(ln3_b[0]))
    return out.reshape(bsz, s, d)
```

```python
import math

import jax
import jax.numpy as jnp
from jax import lax
from jax.experimental import pallas as pl
from jax.experimental.pallas import tpu as pltpu

F32 = jnp.float32
BF16 = jnp.bfloat16

D_MODEL = 2048
MEM_LEN = 256
RET_HEADS = 4
RET_DK = 256
RET_DV = 256
GLA_HEADS = 4
GLA_DK = 128
GLA_DV = 256
GLA_LOWRANK = 16
GLA_TAU = 16.0
ROPE_BASE = 10000.0
MEM_HEADS = 4
MEM_HEAD_DIM = D_MODEL // MEM_HEADS
N_GROUPS = 4
EXPERTS_PER_GROUP = 8
N_EXPERTS = N_GROUPS * EXPERTS_PER_GROUP
EXPERT_FF = 512
LN_EPS = 1e-5
DEPTH = 1
DEEPNORM_ALPHA = (2 * DEPTH) ** 0.25

RQ_OFF, RK_OFF, RV_OFF, RG_OFF = 0, 1024, 2048, 3072
GQ_OFF, GK_OFF, GV_OFF, GG_OFF, GLR_OFF = 4096, 4608, 5120, 6144, 7168
IN_MAIN = 7168

LANES = 128
RET_CHUNK = 256
GLA_CHUNK = 128
GLA_LEVELS = 7
GLA_STEP_CHUNKS = 4
MOE_BLK = 256
EXPERT_STEP_BLKS = 4
TILE_TOK = 512
RUN_ALIGN = 16
TILE_SLOTS = 2 * TILE_TOK + N_EXPERTS * RUN_ALIGN
SORT_ROWS = 256
VMEM_LIMIT = 56 * 1024 * 1024


def _cparams(sem):
    return pltpu.CompilerParams(dimension_semantics=sem, vmem_limit_bytes=VMEM_LIMIT)


def _layer_norm(y, g, b):
    mu = jnp.mean(y, axis=-1, keepdims=True)
    d = y - mu
    var = jnp.mean(d * d, axis=-1, keepdims=True)
    return d * lax.rsqrt(var + LN_EPS) * g + b


def _silu(x):
    return x / (1.0 + jnp.exp(-x))


def _dot(a, b):
    return jnp.dot(a, b, preferred_element_type=F32)


def _dot_nt(a, b):
    return lax.dot_general(a, b, (((1,), (1,)), ((), ())), preferred_element_type=F32)


def _dot_tn(a, b):
    return lax.dot_general(a, b, (((0,), (0,)), ((), ())), preferred_element_type=F32)


def _rope_kernel(pos_ref, invf_ref, cos_ref, sin_ref):
    ang = pos_ref[...].astype(F32) * invf_ref[...]
    cos_ref[...] = jnp.cos(ang)
    sin_ref[...] = jnp.sin(ang)


def _rope_table(pos_col, inv_freq):
    s = pos_col.shape[0]
    tm = min(s, 1024)
    half = inv_freq.shape[1]
    return pl.pallas_call(
        _rope_kernel,
        out_shape=(jax.ShapeDtypeStruct((s, half), F32), jax.ShapeDtypeStruct((s, half), F32)),
        grid=(s // tm,),
        in_specs=[pl.BlockSpec((tm, 1), lambda i: (i, 0)),
                  pl.BlockSpec((1, half), lambda i: (0, 0))],
        out_specs=(pl.BlockSpec((tm, half), lambda i: (i, 0)),
                   pl.BlockSpec((tm, half), lambda i: (i, 0))),
        compiler_params=_cparams(("arbitrary",)),
        name="rope_table",
    )(pos_col, inv_freq)


def _proj_in_kernel(x_ref, w_ref, wlr_ref, wa2_ref, ba_ref, o_ref, la_ref, xb_ref):
    @pl.when(pl.program_id(1) == 0)
    def _():
        xb_ref[...] = x_ref[...].astype(BF16)
        glr = _dot(xb_ref[...], wlr_ref[...])
        z = _dot(glr.astype(BF16), wa2_ref[...]) + ba_ref[...]
        la_ref[...] = (jnp.minimum(z, 0.0) - jnp.log(1.0 + jnp.exp(-jnp.abs(z)))) / GLA_TAU

    o_ref[...] = _dot(xb_ref[...], w_ref[...])


def _proj_in(x2d, w_in_b, w_lr, w_a2, b_a):
    s = x2d.shape[0]
    tm = min(s, 1024)
    tn = 1024
    n = GLA_HEADS * GLA_DK
    return pl.pallas_call(
        _proj_in_kernel,
        out_shape=(jax.ShapeDtypeStruct((s, IN_MAIN), F32), jax.ShapeDtypeStruct((s, n), F32)),
        grid=(s // tm, IN_MAIN // tn),
        in_specs=[pl.BlockSpec((tm, D_MODEL), lambda i, j: (i, 0)),
                  pl.BlockSpec((D_MODEL, tn), lambda i, j: (0, j)),
                  pl.BlockSpec((D_MODEL, LANES), lambda i, j: (0, 0)),
                  pl.BlockSpec((LANES, n), lambda i, j: (0, 0)),
                  pl.BlockSpec((1, n), lambda i, j: (0, 0))],
        out_specs=(pl.BlockSpec((tm, tn), lambda i, j: (i, j)),
                   pl.BlockSpec((tm, n), lambda i, j: (i, 0))),
        scratch_shapes=[pltpu.VMEM((tm, D_MODEL), BF16)],
        compiler_params=_cparams(("arbitrary", "arbitrary")),
        name="proj_in",
    )(x2d, w_in_b, w_lr, w_a2, b_a)


def _rotary(t, cos, sin):
    half = t.shape[-1] // 2
    t1, t2 = t[:, :half], t[:, half:]
    return jnp.concatenate([t1 * cos - t2 * sin, t1 * sin + t2 * cos], axis=-1)


def _retention_kernel(lg_ref, q_ref, k_ref, v_ref, g_ref, cos_ref, sin_ref, o_ref, state_ref, intra_ref, dq_ref, dk_ref):
    c = pl.program_id(0)
    C = q_ref.shape[0]

    @pl.when(c == 0)
    def _():
        state_ref[...] = jnp.zeros_like(state_ref)
        ri = lax.broadcasted_iota(jnp.int32, (C, C), 0)
        ci = lax.broadcasted_iota(jnp.int32, (C, C), 1)
        rel = jnp.maximum(ri - ci, 0).astype(F32)
        n = lax.broadcasted_iota(jnp.int32, (C, RET_DK), 0).astype(F32)
        for h in range(RET_HEADS):
            lg = lg_ref[h]
            intra_ref[h] = jnp.where(ri >= ci, jnp.exp(lg * rel), 0.0)
            dq_ref[h] = jnp.exp(lg * (n + 1.0))
            dk_ref[h] = jnp.exp(lg * (C - 1.0 - n))

    cos = cos_ref[...]
    sin = sin_ref[...]

    for h in range(RET_HEADS):
        sl = slice(h * RET_DK, (h + 1) * RET_DK)
        q = _rotary(q_ref[:, sl], cos, sin)
        k = _rotary(k_ref[:, sl], cos, sin) * (RET_DK ** -0.5)
        v = v_ref[:, sl].astype(BF16)
        decay_q = dq_ref[h]
        decay_k = dk_ref[h]
        decay_chunk = jnp.exp(lg_ref[h] * C)

        state = state_ref[h]
        scores = _dot_nt(q.astype(BF16), k.astype(BF16)) * intra_ref[h]
        o = _dot(scores.astype(BF16), v) + _dot((q * decay_q).astype(BF16), state.astype(BF16))
        state_ref[h] = decay_chunk * state + _dot_tn((k * decay_k).astype(BF16), v)

        mu = jnp.mean(o, axis=-1, keepdims=True)
        d = o - mu
        var = jnp.mean(d * d, axis=-1, keepdims=True)
        o = d * lax.rsqrt(var + LN_EPS)
        o_ref[:, sl] = (_silu(g_ref[:, sl]) * o).astype(o_ref.dtype)


def _retention(h_main, cos, sin, log_gamma):
    s = h_main.shape[0]
    C = min(RET_CHUNK, s)
    w = RET_HEADS * RET_DK
    col = lambda off: (lambda c, lg: (c, off // w))
    grid_spec = pltpu.PrefetchScalarGridSpec(
        num_scalar_prefetch=1,
        grid=(s // C,),
        in_specs=[pl.BlockSpec((C, w), col(RQ_OFF)),
                  pl.BlockSpec((C, w), col(RK_OFF)),
                  pl.BlockSpec((C, w), col(RV_OFF)),
                  pl.BlockSpec((C, w), col(RG_OFF)),
                  pl.BlockSpec((C, RET_DK // 2), lambda c, lg: (c, 0)),
                  pl.BlockSpec((C, RET_DK // 2), lambda c, lg: (c, 0))],
        out_specs=pl.BlockSpec((C, RET_HEADS * RET_DV), lambda c, lg: (c, 0)),
        scratch_shapes=[pltpu.VMEM((RET_HEADS, RET_DK, RET_DV), F32),
                        pltpu.VMEM((RET_HEADS, C, C), F32),
                        pltpu.VMEM((RET_HEADS, C, RET_DK), F32),
                        pltpu.VMEM((RET_HEADS, C, RET_DK), F32)],
    )
    return pl.pallas_call(
        _retention_kernel,
        out_shape=jax.ShapeDtypeStruct((s, RET_HEADS * RET_DV), BF16),
        grid_spec=grid_spec,
        compiler_params=_cparams(("arbitrary",)),
        name="retention",
    )(log_gamma, h_main, h_main, h_main, h_main, cos, sin)


def _gla_decay_matrix(C):
    import numpy as np
    levels = int(math.log2(C))
    r = np.arange(C)[:, None]
    t = np.arange(C)[None, :]
    mats = []
    for l in range(levels):
        blk = C >> l
        half = blk // 2
        m = (r // blk) * blk + half - 1
        qside = (r % blk) >= half
        mats.append(np.where(qside, (t > m) & (t <= r), (t > r) & (t <= m)))
    mats.append(t <= r)
    mats.append(t > r)
    return np.concatenate(mats, axis=0).astype(np.float32)


def _gla_kernel(m_ref, q_ref, k_ref, v_ref, g_ref, la_ref, gn_ref, o_ref, state_ref):
    c = pl.program_id(0)
    C = GLA_CHUNK
    levels = GLA_LEVELS
    nsub = q_ref.shape[0] // C

    @pl.when(c == 0)
    def _():
        state_ref[...] = jnp.zeros_like(state_ref)

    m = m_ref[...]
    ri = lax.broadcasted_iota(jnp.int32, (C, C), 0)
    ci = lax.broadcasted_iota(jnp.int32, (C, C), 1)
    xor = jnp.where(ri > ci, ri ^ ci, 0)
    row = lax.broadcasted_iota(jnp.int32, (C, 1), 0)

    heads = range(GLA_HEADS)
    ks = [slice(h * GLA_DK, (h + 1) * GLA_DK) for h in heads]
    vs = [slice(h * GLA_DV, (h + 1) * GLA_DV) for h in heads]
    rs = [slice(u * C, (u + 1) * C) for u in range(nsub)]
    items = [(u, h) for u in range(nsub) for h in heads]

    expo, q, k, v, scores = {}, {}, {}, {}, {}
    for u, h in items:
        la = la_ref[rs[u], ks[h]]
        la_hi = la.astype(BF16)
        la_lo = (la - la_hi.astype(F32)).astype(BF16)
        expo[u, h] = jnp.exp(_dot(m, la_hi) + _dot(m, la_lo))
    for u, h in items:
        q[u, h] = q_ref[rs[u], ks[h]] * (GLA_DK ** -0.5)
        k[u, h] = k_ref[rs[u], ks[h]]
        v[u, h] = v_ref[rs[u], vs[h]].astype(BF16)
        scores[u, h] = jnp.where(ri == ci, _dot_nt(q[u, h].astype(BF16), k[u, h].astype(BF16)), 0.0)
    for l in range(levels):
        half = C >> (l + 1)
        shift = int(math.log2(half))
        qside = (row & half) != 0
        keep = (xor >> shift) == 1
        for it in items:
            x = (jnp.where(qside, q[it], k[it]) * expo[it][l * C:(l + 1) * C]).astype(BF16)
            scores[it] = scores[it] + jnp.where(keep, _dot_nt(x, x), 0.0)

    state = [state_ref[h] for h in heads]
    outs = {}
    for u, h in items:
        e_b = expo[u, h][levels * C:(levels + 1) * C]
        e_rev = expo[u, h][(levels + 1) * C:(levels + 2) * C]
        e_last = e_b[C - 1:C, :]
        outs[u, h] = (_dot(scores[u, h].astype(BF16), v[u, h])
                      + _dot_nt((q[u, h] * e_b).astype(BF16), state[h].astype(BF16)))
        state[h] = state[h] * e_last + _dot_tn(v[u, h], (k[u, h] * e_rev).astype(BF16))
    for h in heads:
        state_ref[h] = state[h]

    for u, h in items:
        o = outs[u, h]
        o = o * lax.rsqrt(jnp.mean(o * o, axis=-1, keepdims=True) + LN_EPS) * gn_ref[...]
        o_ref[rs[u], vs[h]] = (_silu(g_ref[rs[u], vs[h]]) * o).astype(o_ref.dtype)


def _gla(h_main, log_a, g_norm):
    s = h_main.shape[0]
    C = GLA_CHUNK
    m = jnp.asarray(_gla_decay_matrix(C), dtype=BF16)
    nrow = m.shape[0]
    rows = min(s, GLA_STEP_CHUNKS * C)
    wk = GLA_HEADS * GLA_DK
    wv = GLA_HEADS * GLA_DV
    return pl.pallas_call(
        _gla_kernel,
        out_shape=jax.ShapeDtypeStruct((s, wv), BF16),
        grid=(s // rows,),
        in_specs=[pl.BlockSpec((nrow, C), lambda c: (0, 0)),
                  pl.BlockSpec((rows, wk), lambda c: (c, GQ_OFF // wk)),
                  pl.BlockSpec((rows, wk), lambda c: (c, GK_OFF // wk)),
                  pl.BlockSpec((rows, wv), lambda c: (c, GV_OFF // wv)),
                  pl.BlockSpec((rows, wv), lambda c: (c, GG_OFF // wv)),
                  pl.BlockSpec((rows, wk), lambda c: (c, 0)),
                  pl.BlockSpec((1, GLA_DV), lambda c: (0, 0))],
        out_specs=pl.BlockSpec((rows, wv), lambda c: (c, 0)),
        scratch_shapes=[pltpu.VMEM((GLA_HEADS, GLA_DV, GLA_DK), F32)],
        compiler_params=_cparams(("arbitrary",)),
        name="gla",
    )(m, h_main, h_main, h_main, h_main, log_a, g_norm)


def _mixout_kernel(ret_ref, gla_ref, x_ref, w_ref, g_ref, b_ref, o_ref, wb_ref):
    nr = ret_ref.shape[1]

    @pl.when(pl.program_id(0) == 0)
    def _():
        wb_ref[...] = w_ref[...].astype(BF16)

    tm = x_ref.shape[0]
    parts = 2 if tm % 512 == 0 else 1
    rows = [slice(p * (tm // parts), (p + 1) * (tm // parts)) for p in range(parts)]
    mix = [_dot(ret_ref[r, :], wb_ref[:nr, :]) + _dot(gla_ref[r, :], wb_ref[nr:, :]) for r in rows]
    for r, m in zip(rows, mix):
        o_ref[r, :] = _layer_norm(DEEPNORM_ALPHA * x_ref[r, :] + m, g_ref[...], b_ref[...])


def _mixout_ln(ret, gla, x2d, w_b, g, b):
    s = x2d.shape[0]
    tm = min(s, 512)
    nr, ng = ret.shape[1], gla.shape[1]
    return pl.pallas_call(
        _mixout_kernel,
        out_shape=jax.ShapeDtypeStruct((s, D_MODEL), F32),
        grid=(s // tm,),
        in_specs=[pl.BlockSpec((tm, nr), lambda i: (i, 0)),
                  pl.BlockSpec((tm, ng), lambda i: (i, 0)),
                  pl.BlockSpec((tm, D_MODEL), lambda i: (i, 0)),
                  pl.BlockSpec((nr + ng, D_MODEL), lambda i: (0, 0), pipeline_mode=pl.Buffered(1)),
                  pl.BlockSpec((1, D_MODEL), lambda i: (0, 0)),
                  pl.BlockSpec((1, D_MODEL), lambda i: (0, 0))],
        out_specs=pl.BlockSpec((tm, D_MODEL), lambda i: (i, 0)),
        scratch_shapes=[pltpu.VMEM((nr + ng, D_MODEL), BF16)],
        compiler_params=_cparams(("arbitrary",)),
        name="mixout_ln1",
    )(ret, gla, x2d, w_b, g, b)


def _kv_kernel(mem_ref, wk_ref, wv_ref, k_ref, v_ref):
    m = mem_ref[...].astype(BF16)
    k_ref[...] = _dot(m, wk_ref[...].astype(BF16)).astype(k_ref.dtype)
    v_ref[...] = _dot(m, wv_ref[...].astype(BF16)).astype(v_ref.dtype)


def _mem_kv(mem_b, wk_b, wv_b):
    tn = 512
    return pl.pallas_call(
        _kv_kernel,
        out_shape=(jax.ShapeDtypeStruct((MEM_LEN, D_MODEL), BF16),
                   jax.ShapeDtypeStruct((MEM_LEN, D_MODEL), BF16)),
        grid=(D_MODEL // tn,),
        in_specs=[pl.BlockSpec((MEM_LEN, D_MODEL), lambda j: (0, 0)),
                  pl.BlockSpec((D_MODEL, tn), lambda j: (0, j)),
                  pl.BlockSpec((D_MODEL, tn), lambda j: (0, j))],
        out_specs=(pl.BlockSpec((MEM_LEN, tn), lambda j: (0, j)),
                   pl.BlockSpec((MEM_LEN, tn), lambda j: (0, j))),
        compiler_params=_cparams(("arbitrary",)),
        name="mem_kv",
    )(mem_b, wk_b, wv_b)


def _cross_kernel(h_ref, wq_ref, k_ref, v_ref, wo_ref, g_ref, b_ref, wr_ref, br_ref,
                  h2_ref, lg_ref):
    tm = h_ref.shape[0]
    parts = 2 if tm % 512 == 0 else 1
    pr = tm // parts
    rows = [slice(p * pr, (p + 1) * pr) for p in range(parts)]
    sls = [slice(hd * MEM_HEAD_DIM, (hd + 1) * MEM_HEAD_DIM) for hd in range(MEM_HEADS)]
    h1 = [h_ref[r, :] for r in rows]
    q = [_dot(h.astype(BF16), wq_ref[...]).astype(BF16) for h in h1]
    o = []
    for p in range(parts):
        outs = []
        for sl in sls:
            s = _dot_nt(q[p][:, sl], k_ref[:, sl]) * (MEM_HEAD_DIM ** -0.5)
            s = s - jnp.max(s, axis=-1, keepdims=True)
            e = jnp.exp(s)
            e = e / jnp.sum(e, axis=-1, keepdims=True)
            outs.append(_dot(e.astype(BF16), v_ref[:, sl]))
        o.append(jnp.concatenate(outs, axis=-1).astype(BF16))
    cross = [_dot(x, wo_ref[...]) for x in o]
    for p in range(parts):
        h2 = _layer_norm(DEEPNORM_ALPHA * h1[p] + cross[p], g_ref[...], b_ref[...])
        h2_ref[rows[p], :] = h2
        lg_ref[rows[p], :] = _dot(h2.astype(BF16), wr_ref[...]) + br_ref[...]


def _cross_attention(h1, wq_b, k, v, wo_b, g, b, w_route, b_route):
    s = h1.shape[0]
    tm = min(s, 512)
    const = lambda shape: pl.BlockSpec(shape, lambda i: (0, 0), pipeline_mode=pl.Buffered(1))
    return pl.pallas_call(
        _cross_kernel,
        out_shape=(jax.ShapeDtypeStruct((s, D_MODEL), F32),
                   jax.ShapeDtypeStruct((s, LANES), F32)),
        grid=(s // tm,),
        in_specs=[pl.BlockSpec((tm, D_MODEL), lambda i: (i, 0)),
                  const((D_MODEL, D_MODEL)),
                  const((MEM_LEN, D_MODEL)),
                  const((MEM_LEN, D_MODEL)),
                  const((D_MODEL, D_MODEL)),
                  const((1, D_MODEL)),
                  const((1, D_MODEL)),
                  const((D_MODEL, LANES)),
                  const((1, LANES))],
        out_specs=(pl.BlockSpec((tm, D_MODEL), lambda i: (i, 0)),
                   pl.BlockSpec((tm, LANES), lambda i: (i, 0))),
        compiler_params=_cparams(("arbitrary",)),
        name="cross_attn_ln2",
    )(h1, wq_b, k, v, wo_b, g, b, w_route, b_route)


def _route_kernel(lg_ref, slot_ref, slot_t_ref, gate_ref, runs_ref, plan_ref, tot_row, tot_col, gstart, rinfo):
    phase = pl.program_id(0)
    i = pl.program_id(1)
    tm = lg_ref.shape[0]
    lane = lax.broadcasted_iota(jnp.int32, (tm, LANES), 1)
    rows = pl.ds(pl.multiple_of(i * tm, tm), tm)

    @pl.when(phase == 0)
    def _():
        neg = -jnp.inf
        logits = lg_ref[...]
        gmask = lane < N_GROUPS
        gl = jnp.where(gmask, logits, neg)
        ge = jnp.exp(gl - jnp.max(gl, axis=-1, keepdims=True))
        pg = ge / jnp.sum(ge, axis=-1, keepdims=True)
        pg_sel = jnp.max(pg, axis=-1, keepdims=True)
        grp = jnp.min(jnp.where((pg == pg_sel) & gmask, lane, LANES), axis=-1, keepdims=True)

        fl_lane = lane - N_GROUPS
        fmask = (fl_lane >= 0) & (fl_lane < N_EXPERTS) & ((fl_lane >> 3) == grp)
        fl = jnp.where(fmask, logits, neg)
        fe = jnp.exp(fl - jnp.max(fl, axis=-1, keepdims=True))
        fp = fe / jnp.sum(fe, axis=-1, keepdims=True)
        p1 = jnp.max(fp, axis=-1, keepdims=True)
        i1 = jnp.min(jnp.where((fp == p1) & fmask, lane, LANES), axis=-1, keepdims=True)
        rest = fmask & (lane != i1)
        fp2 = jnp.where(rest, fp, -1.0)
        p2 = jnp.max(fp2, axis=-1, keepdims=True)
        i2 = jnp.min(jnp.where((fp2 == p2) & rest, lane, LANES), axis=-1, keepdims=True)
        psum = p1 + p2
        picks = ((i1 - N_GROUPS).astype(F32), (i2 - N_GROUPS).astype(F32),
                 pg_sel * p1 / psum, pg_sel * p2 / psum)
        info = jnp.zeros((tm, LANES), F32)
        for j, val in enumerate(picks):
            info = jnp.where(lane == j, val, info)
        rinfo[rows, :] = info

    info = rinfo[rows, :]
    oh1 = lane == info[:, 0:1].astype(jnp.int32)
    oh2 = lane == info[:, 1:2].astype(jnp.int32)
    gate1 = info[:, 2:3]
    gate2 = info[:, 3:4]
    oh = (jnp.where(oh1, 1.0, 0.0) + jnp.where(oh2, 1.0, 0.0)).astype(BF16)
    ones = jnp.ones((tm, LANES), BF16)

    def align_up(v, a):
        return jnp.floor((v + (a - 1.0)) * (1.0 / a)) * a

    run_row = align_up(_dot_tn(ones, oh)[0:8, :], RUN_ALIGN)
    rr = lax.broadcasted_iota(jnp.int32, (LANES, LANES), 0)
    cc = lax.broadcasted_iota(jnp.int32, (LANES, LANES), 1)
    strict_upper = jnp.where(rr < cc, 1.0, 0.0).astype(BF16)

    @pl.when((phase == 0) & (i == 0))
    def _():
        tot_row[...] = jnp.zeros_like(tot_row)
        tot_col[...] = jnp.zeros_like(tot_col)

    @pl.when(phase == 0)
    def _():
        tot_row[...] += run_row
        tot_col[...] += align_up(_dot_tn(oh, ones), RUN_ALIGN)

    @pl.when((phase == 1) & (i == 0))
    def _():
        nblk_row = align_up(tot_row[...], MOE_BLK) * (1.0 / MOE_BLK)
        nblk_col = align_up(tot_col[...], MOE_BLK) * (1.0 / MOE_BLK)
        lower_incl = jnp.where(cc <= rr, 1.0, 0.0).astype(BF16)
        base = _dot(nblk_row.astype(BF16), strict_upper) * float(MOE_BLK)
        ends = _dot(lower_incl, nblk_col.astype(BF16))
        expert_rows = rr < N_EXPERTS
        be = jnp.sum(jnp.where(expert_rows & (ends <= cc.astype(F32)), 1.0, 0.0), axis=0, keepdims=True)
        be = jnp.minimum(be, N_EXPERTS - 1.0)
        total = jnp.sum(jnp.where(expert_rows, nblk_col, 0.0), axis=0, keepdims=True)
        sub = lax.broadcasted_iota(jnp.int32, plan_ref.shape, 0)
        plan = jnp.where(sub == 0, jnp.broadcast_to(be, plan_ref.shape),
                         jnp.where(sub == 1, jnp.broadcast_to(total, plan_ref.shape),
                                   jnp.where(sub == 2, base + tot_row[...],
                                             jnp.where(sub == 3, nblk_row,
                                                       nblk_row * float(MOE_BLK) - tot_row[...]))))
        plan_ref[...] = plan.astype(jnp.int32)
        gstart[...] = base

    @pl.when(phase == 1)
    def _():
        off = _dot((run_row * (1.0 / RUN_ALIGN)).astype(BF16), strict_upper) * float(RUN_ALIGN)
        tr = lax.broadcasted_iota(jnp.int32, (tm, tm), 0)
        tc = lax.broadcasted_iota(jnp.int32, (tm, tm), 1)
        strict_lower = jnp.where(tc < tr, 1.0, 0.0).astype(BF16)
        slot = _dot(strict_lower, oh) + off[0:1, :]
        s1 = jnp.sum(jnp.where(oh1, slot, 0.0), axis=-1, keepdims=True)
        s2 = jnp.sum(jnp.where(oh2, slot, 0.0), axis=-1, keepdims=True)
        slots = jnp.where(lane == 0, s1, jnp.where(lane == 1, s2, 0.0))
        slot_ref[...] = slots.astype(jnp.int32)
        slot_t_ref[...] = jnp.transpose(slots)[0:8, :].astype(jnp.int32)
        gate_ref[...] = jnp.where(lane == 0, gate1, jnp.where(lane == 1, gate2, 0.0))
        sub = lax.broadcasted_iota(jnp.int32, runs_ref.shape, 0)
        runs = jnp.where(sub == 0, gstart[...], jnp.where(sub == 1, off, run_row))
        runs_ref[...] = runs.astype(jnp.int32)
        gstart[...] += run_row


def _route(logits):
    t = logits.shape[0]
    tm = min(t, TILE_TOK)
    nt = t // tm
    step = lambda p, i: (i * p, 0)
    return pl.pallas_call(
        _route_kernel,
        out_shape=(jax.ShapeDtypeStruct((t, LANES), jnp.int32),
                   jax.ShapeDtypeStruct((nt * 8, tm), jnp.int32),
                   jax.ShapeDtypeStruct((t, LANES), F32),
                   jax.ShapeDtypeStruct((nt * 8, LANES), jnp.int32),
                   jax.ShapeDtypeStruct((8, LANES), jnp.int32)),
        grid=(2, nt),
        in_specs=[pl.BlockSpec((tm, LANES), lambda p, i: (i, 0))],
        out_specs=(pl.BlockSpec((tm, LANES), step),
                   pl.BlockSpec((8, tm), step),
                   pl.BlockSpec((tm, LANES), step),
                   pl.BlockSpec((8, LANES), step),
                   pl.BlockSpec((8, LANES), lambda p, i: (0, 0))),
        scratch_shapes=[pltpu.VMEM((8, LANES), F32),
                        pltpu.VMEM((LANES, LANES), F32),
                        pltpu.VMEM((8, LANES), F32),
                        pltpu.VMEM((t, LANES), F32)],
        compiler_params=_cparams(("arbitrary", "arbitrary")),
        name="moe_route",
    )(logits)


def _run_copies(rg_ref, ro_ref, rn_ref, tile, make_copy):
    def each(action):
        def body(e, carry):
            j = tile * N_EXPERTS + e
            n = rn_ref[j]

            @pl.when(n > 0)
            def _():
                action(make_copy(pl.multiple_of(rg_ref[j], RUN_ALIGN), pl.multiple_of(ro_ref[j], RUN_ALIGN),
                                 pl.multiple_of(n, RUN_ALIGN)))
            return carry
        lax.fori_loop(0, N_EXPERTS, body, 0)
    return each


def _dispatch_kernel(rg_ref, ro_ref, rn_ref, zrow_ref, zlen_ref, nu_ref, x_ref, slot_t_ref, xb_ref,
                     sorted_ref, zero_ref, sems, zsem):
    i = pl.program_id(0)
    n = pl.num_programs(0)
    tm = x_ref.shape[0]
    nb = xb_ref.shape[0] // MOE_BLK
    buf = i % 2

    def zero_copy(row, size):
        return pltpu.make_async_copy(
            zero_ref.at[pl.ds(0, size), :], xb_ref.at[pl.ds(pl.multiple_of(row, RUN_ALIGN), size), :], zsem)

    def pad_copy(e):
        return zero_copy(zrow_ref[e], pl.multiple_of(zlen_ref[e], RUN_ALIGN))

    def runs_of(tile, slot):
        def make_copy(g, o, size):
            return pltpu.make_async_copy(sorted_ref.at[slot, pl.ds(o, size), :],
                                         xb_ref.at[pl.ds(g, size), :], sems.at[slot])
        return _run_copies(rg_ref, ro_ref, rn_ref, tile, make_copy)

    @pl.when(i == 0)
    def _():
        zero_ref[...] = jnp.zeros_like(zero_ref)

        def issue_zero(e, carry):
            @pl.when(zlen_ref[e] > 0)
            def _():
                pad_copy(e).start()
            return carry
        lax.fori_loop(0, N_EXPERTS, issue_zero, 0)
        lax.fori_loop(nu_ref[0], nb, lambda b, c: (zero_copy(b * MOE_BLK, MOE_BLK).start(), c)[1], 0)

        def wait_zero(e, carry):
            @pl.when(zlen_ref[e] > 0)
            def _():
                pad_copy(e).wait()
            return carry
        lax.fori_loop(0, N_EXPERTS, wait_zero, 0)
        lax.fori_loop(nu_ref[0], nb, lambda b, c: (zero_copy(0, MOE_BLK).wait(), c)[1], 0)

    @pl.when(i >= 2)
    def _():
        runs_of(i - 2, buf)(lambda c: c.wait())

    x = x_ref[...].astype(BF16)
    s1 = slot_t_ref[0:1, :]
    s2 = slot_t_ref[1:2, :]
    for rc in range(TILE_SLOTS // SORT_ROWS):
        r = lax.broadcasted_iota(jnp.int32, (SORT_ROWS, tm), 0) + rc * SORT_ROWS
        perm = jnp.where((r == s1) | (r == s2), 1.0, 0.0).astype(BF16)
        sorted_ref[buf, rc * SORT_ROWS:(rc + 1) * SORT_ROWS, :] = _dot(perm, x).astype(BF16)

    runs_of(i, buf)(lambda c: c.start())

    @pl.when(i == n - 1)
    def _():
        @pl.when(i >= 1)
        def _():
            runs_of(i - 1, 1 - buf)(lambda c: c.wait())
        runs_of(i, buf)(lambda c: c.wait())


def _dispatch(rg, ro, rn, zrow, zlen, n_used, h2, slot_t, n_rows):
    t = h2.shape[0]
    tm = min(t, TILE_TOK)
    grid_spec = pltpu.PrefetchScalarGridSpec(
        num_scalar_prefetch=6,
        grid=(t // tm,),
        in_specs=[pl.BlockSpec((tm, D_MODEL), lambda i, *_: (i, 0)),
                  pl.BlockSpec((8, tm), lambda i, *_: (i, 0))],
        out_specs=pl.BlockSpec(memory_space=pl.ANY),
        scratch_shapes=[pltpu.VMEM((2, TILE_SLOTS, D_MODEL), BF16),
                        pltpu.VMEM((MOE_BLK, D_MODEL), BF16),
                        pltpu.SemaphoreType.DMA((2,)),
                        pltpu.SemaphoreType.DMA(())],
    )
    return pl.pallas_call(
        _dispatch_kernel,
        out_shape=jax.ShapeDtypeStruct((n_rows, D_MODEL), BF16),
        grid_spec=grid_spec,
        compiler_params=_cparams(("arbitrary",)),
        name="moe_dispatch",
    )(rg, ro, rn, zrow, zlen, n_used, h2, slot_t)


def _expert_kernel(be_ref, nu_ref, nblk_ref, x_ref, wg_ref, wu_ref, wd_ref, y_ref,
                   wg_f, wu_f, wd_f, wg_b, wu_b, wd_b, sems, cur_ref):
    step = pl.program_id(0)
    n_used = nu_ref[0]

    def weight_copies(expert, slot):
        half = EXPERT_FF // 2
        return [pltpu.make_async_copy(wg_ref.at[expert], wg_f.at[slot], sems.at[slot]),
                pltpu.make_async_copy(wu_ref.at[expert], wu_f.at[slot], sems.at[slot]),
                pltpu.make_async_copy(wd_ref.at[expert, :half], wd_f.at[slot, :half], sems.at[slot]),
                pltpu.make_async_copy(wd_ref.at[expert, half:], wd_f.at[slot, half:], sems.at[slot])]

    def start_weights(expert, slot):
        for j, c in enumerate(weight_copies(expert, slot)):
            c.start(priority=j % 2)

    def after(blk):
        return blk + nblk_ref[be_ref[blk]]

    @pl.when(step == 0)
    def _():
        cur_ref[0] = 0
        start_weights(be_ref[0], 0)

        @pl.when(after(0) < n_used)
        def _():
            start_weights(be_ref[after(0)], 1)

    for sub in range(EXPERT_STEP_BLKS):
        b = step * EXPERT_STEP_BLKS + sub
        used = b < n_used
        e = be_ref[b]
        prev = be_ref[jnp.maximum(b - 1, 0)]
        fresh = used & ((b == 0) | (e != prev))
        rows = slice(sub * MOE_BLK, (sub + 1) * MOE_BLK)

        @pl.when(fresh)
        def _():
            slot = cur_ref[0]
            for c in weight_copies(e, slot):
                c.wait()
            wg_b[...] = wg_f[slot].astype(BF16)
            wu_b[...] = wu_f[slot].astype(BF16)
            wd_b[...] = wd_f[slot].astype(BF16)
            nxt = after(b)

            @pl.when(nxt < n_used)
            def _():
                nxt2 = after(nxt)

                @pl.when(nxt2 < n_used)
                def _():
                    start_weights(be_ref[nxt2], slot)
            cur_ref[0] = 1 - slot

        @pl.when(used)
        def _():
            x = x_ref[rows, :]
            hid = _silu(_dot(x, wg_b[...])) * _dot(x, wu_b[...])
            y_ref[rows, :] = _dot(hid.astype(BF16), wd_b[...]).astype(y_ref.dtype)

        @pl.when(jnp.logical_not(used))
        def _():
            y_ref[rows, :] = jnp.zeros((MOE_BLK, D_MODEL), y_ref.dtype)


def _experts(block_e, n_used, nblk, xb, w_gate, w_up, w_down):
    step_rows = EXPERT_STEP_BLKS * MOE_BLK
    any_space = pl.BlockSpec(memory_space=pl.ANY)
    last_step = lambda nu: (nu[0] - 1) // EXPERT_STEP_BLKS
    grid_spec = pltpu.PrefetchScalarGridSpec(
        num_scalar_prefetch=3,
        grid=(xb.shape[0] // step_rows,),
        in_specs=[pl.BlockSpec((step_rows, D_MODEL), lambda s, be, nu, nk: (jnp.minimum(s, last_step(nu)), 0)),
                  any_space, any_space, any_space],
        out_specs=pl.BlockSpec((step_rows, D_MODEL), lambda s, be, nu, nk: (s, 0)),
        scratch_shapes=[pltpu.VMEM((2, D_MODEL, EXPERT_FF), F32),
                        pltpu.VMEM((2, D_MODEL, EXPERT_FF), F32),
                        pltpu.VMEM((2, EXPERT_FF, D_MODEL), F32),
                        pltpu.VMEM((D_MODEL, EXPERT_FF), BF16),
                        pltpu.VMEM((D_MODEL, EXPERT_FF), BF16),
                        pltpu.VMEM((EXPERT_FF, D_MODEL), BF16),
                        pltpu.SemaphoreType.DMA((2,)),
                        pltpu.SMEM((1,), jnp.int32)],
    )
    return pl.pallas_call(
        _expert_kernel,
        out_shape=jax.ShapeDtypeStruct(xb.shape, BF16),
        grid_spec=grid_spec,
        compiler_params=_cparams(("arbitrary",)),
        name="moe_experts",
    )(block_e, n_used, nblk, xb, w_gate, w_up, w_down)


def _combine_kernel(rg_ref, ro_ref, rn_ref, yb_ref, h2_ref, slot_ref, gate_ref, g_ref, b_ref, o_ref,
                    ybuf, sems):
    i = pl.program_id(0)
    n = pl.num_programs(0)
    tm = h2_ref.shape[0]
    buf = i % 2

    def runs_of(tile, slot):
        def make_copy(g, o, size):
            return pltpu.make_async_copy(yb_ref.at[pl.ds(g, size), :],
                                         ybuf.at[slot, pl.ds(o, size), :], sems.at[slot])
        return _run_copies(rg_ref, ro_ref, rn_ref, tile, make_copy)

    @pl.when(i == 0)
    def _():
        ybuf[...] = jnp.zeros_like(ybuf)
        runs_of(0, 0)(lambda c: c.start())

    @pl.when(i + 1 < n)
    def _():
        runs_of(i + 1, 1 - buf)(lambda c: c.start())

    runs_of(i, buf)(lambda c: c.wait())

    lane = lax.broadcasted_iota(jnp.int32, (tm, TILE_SLOTS), 1)
    slots = slot_ref[...]
    gate = gate_ref[...]
    y = ybuf[buf]
    sel = jnp.where(lane == slots[:, 0:1], gate[:, 0:1],
                    jnp.where(lane == slots[:, 1:2], gate[:, 1:2], 0.0)).astype(BF16)
    parts = 2 if tm % 512 == 0 else 1
    rows = [slice(p * (tm // parts), (p + 1) * (tm // parts)) for p in range(parts)]
    ffn = [_dot(sel[r, :], y) for r in rows]
    for r, f in zip(rows, ffn):
        o_ref[r, :] = _layer_norm(DEEPNORM_ALPHA * h2_ref[r, :] + f, g_ref[...], b_ref[...])


def _combine_ln(rg, ro, rn, yb, h2, slots, gate, g, b):
    t = h2.shape[0]
    tm = min(t, TILE_TOK)
    grid_spec = pltpu.PrefetchScalarGridSpec(
        num_scalar_prefetch=3,
        grid=(t // tm,),
        in_specs=[pl.BlockSpec(memory_space=pl.ANY),
                  pl.BlockSpec((tm, D_MODEL), lambda i, *_: (i, 0)),
                  pl.BlockSpec((tm, LANES), lambda i, *_: (i, 0)),
                  pl.BlockSpec((tm, LANES), lambda i, *_: (i, 0)),
                  pl.BlockSpec((1, D_MODEL), lambda i, *_: (0, 0)),
                  pl.BlockSpec((1, D_MODEL), lambda i, *_: (0, 0))],
        out_specs=pl.BlockSpec((tm, D_MODEL), lambda i, *_: (i, 0)),
        scratch_shapes=[pltpu.VMEM((2, TILE_SLOTS, D_MODEL), BF16),
                        pltpu.SemaphoreType.DMA((2,))],
    )
    return pl.pallas_call(
        _combine_kernel,
        out_shape=jax.ShapeDtypeStruct((t, D_MODEL), F32),
        grid_spec=grid_spec,
        compiler_params=_cparams(("arbitrary",)),
        name="moe_combine_ln3",
    )(rg, ro, rn, yb, h2, slots, gate, g, b)


def _mixer(x2d, positions, w_in, w_gla_a2, b_gla_a, g_gla_norm):
    s = x2d.shape[0]
    w_in_b = w_in.astype(BF16)
    half = RET_DK // 2
    inv_freq = (ROPE_BASE ** (-jnp.arange(half, dtype=F32) / half)).reshape(1, half)
    cos, sin = _rope_table(positions.reshape(s, 1), inv_freq)
    w_lr = jnp.pad(w_in_b[:, GLR_OFF:], ((0, 0), (0, LANES - GLA_LOWRANK)))
    w_a2 = jnp.pad(w_gla_a2.astype(BF16), ((0, LANES - GLA_LOWRANK), (0, 0)))
    h_main, log_a = _proj_in(x2d, w_in_b, w_lr, w_a2, b_gla_a.reshape(1, -1))
    log_gamma = jnp.log1p(-jnp.exp2(-5.0 - jnp.arange(RET_HEADS, dtype=F32)))
    ret = _retention(h_main, cos, sin, log_gamma)
    gla = _gla(h_main, log_a, g_gla_norm.reshape(1, -1))
    return ret, gla


def _moe(h2, logits, w_gate, w_up, w_down, g, b):
    t = h2.shape[0]
    nt = t // min(t, TILE_TOK)
    slots, slot_t, gate, runs, plan = _route(logits)
    runs = runs.reshape(nt, 8, LANES)[:, :3, :N_EXPERTS]
    rg, ro, rn = (runs[:, j, :].reshape(-1) for j in range(3))
    max_rows = 2 * t + nt * N_EXPERTS * (RUN_ALIGN - 1) + N_EXPERTS * (MOE_BLK - 1)
    nb = -(-max_rows // (MOE_BLK * EXPERT_STEP_BLKS)) * EXPERT_STEP_BLKS
    block_e, n_used = plan[0, :nb], plan[1, :1]
    pad_row, nblk, pad_len = (plan[j, :N_EXPERTS] for j in (2, 3, 4))
    xb = _dispatch(rg, ro, rn, pad_row, pad_len, n_used, h2, slot_t, nb * MOE_BLK)
    yb = _experts(block_e, n_used, nblk, xb, w_gate, w_up, w_down)
    return _combine_ln(rg, ro, rn, yb, h2, slots, gate, g, b)


def kernel(x, mem, positions, w_in, w_gla_a2, b_gla_a, g_gla_norm, w_mix_out, ln1_g, ln1_b, w_mq, w_mk, w_mv, w_mo, ln2_g, ln2_b, w_route_group, b_route_group, w_route_expert, b_route_expert, w_exp_gate, w_exp_up, w_exp_down, ln3_g, ln3_b):
    bsz, s, d = x.shape
    assert bsz == 1 and d == D_MODEL
    x2d = x.reshape(s, d)
    row = lambda v: v.reshape(1, -1)

    ret, gla = _mixer(x2d, positions, w_in[0], w_gla_a2[0], b_gla_a[0], g_gla_norm[0])
    h1 = _mixout_ln(ret, gla, x2d, w_mix_out[0], row(ln1_g[0]), row(ln1_b[0]))

    k, v = _mem_kv(mem[0], w_mk[0], w_mv[0])
    n_route = N_GROUPS + N_EXPERTS
    w_route = jnp.pad(jnp.concatenate([w_route_group[0], w_route_expert[0]], axis=1).astype(BF16),
                      ((0, 0), (0, LANES - n_route)))
    b_route = jnp.pad(jnp.concatenate([b_route_group[0], b_route_expert[0].reshape(-1)]),
                      (0, LANES - n_route)).reshape(1, LANES)
    h2, logits = _cross_attention(h1, w_mq[0].astype(BF16), k, v, w_mo[0].astype(BF16),
                                  row(ln2_g[0]), row(ln2_b[0]), w_route, b_route)

    out = _moe(h2, logits, w_exp_gate[0], w_exp_up[0], w_exp_down[0],
               row(ln3_g[0]), row(ln3_b[0]))
    return out.reshape(bsz, s, d)
```

```python
import math

import jax
import jax.numpy as jnp
from jax import lax
from jax.experimental import pallas as pl
from jax.experimental.pallas import tpu as pltpu

F32 = jnp.float32
BF16 = jnp.bfloat16

D_MODEL = 2048
MEM_LEN = 256
RET_HEADS = 4
RET_DK = 256
RET_DV = 256
GLA_HEADS = 4
GLA_DK = 128
GLA_DV = 256
GLA_LOWRANK = 16
GLA_TAU = 16.0
ROPE_BASE = 10000.0
MEM_HEADS = 4
MEM_HEAD_DIM = D_MODEL // MEM_HEADS
N_GROUPS = 4
EXPERTS_PER_GROUP = 8
N_EXPERTS = N_GROUPS * EXPERTS_PER_GROUP
EXPERT_FF = 512
LN_EPS = 1e-5
DEPTH = 1
DEEPNORM_ALPHA = (2 * DEPTH) ** 0.25

RQ_OFF, RK_OFF, RV_OFF, RG_OFF = 0, 1024, 2048, 3072
GQ_OFF, GK_OFF, GV_OFF, GG_OFF, GLR_OFF = 4096, 4608, 5120, 6144, 7168
IN_MAIN = 7168

LANES = 128
RET_CHUNK = 256
GLA_CHUNK = 128
GLA_LEVELS = 7
GLA_STEP_CHUNKS = 4
MOE_BLK = 256
EXPERT_STEP_BLKS = 4
TILE_TOK = 512
RUN_ALIGN = 16
TILE_SLOTS = 2 * TILE_TOK + N_EXPERTS * RUN_ALIGN
SORT_ROWS = 256
PROJ_K_CHUNK = 512
VMEM_LIMIT = 56 * 1024 * 1024


def _cparams(sem):
    return pltpu.CompilerParams(dimension_semantics=sem, vmem_limit_bytes=VMEM_LIMIT)


def _layer_norm(y, g, b):
    mu = jnp.mean(y, axis=-1, keepdims=True)
    d = y - mu
    var = jnp.mean(d * d, axis=-1, keepdims=True)
    return d * lax.rsqrt(var + LN_EPS) * g + b


def _silu(x):
    return x / (1.0 + jnp.exp(-x))


def _dot(a, b):
    return jnp.dot(a, b, preferred_element_type=F32)


def _dot_nt(a, b):
    return lax.dot_general(a, b, (((1,), (1,)), ((), ())), preferred_element_type=F32)


def _dot_tn(a, b):
    return lax.dot_general(a, b, (((0,), (0,)), ((), ())), preferred_element_type=F32)


def _rope_kernel(pos_ref, invf_ref, cos_ref, sin_ref):
    ang = pos_ref[...].astype(F32) * invf_ref[...]
    cos_ref[...] = jnp.cos(ang)
    sin_ref[...] = jnp.sin(ang)


def _rope_table(pos_col, inv_freq):
    s = pos_col.shape[0]
    tm = min(s, 1024)
    half = inv_freq.shape[1]
    return pl.pallas_call(
        _rope_kernel,
        out_shape=(jax.ShapeDtypeStruct((s, half), F32), jax.ShapeDtypeStruct((s, half), F32)),
        grid=(s // tm,),
        in_specs=[pl.BlockSpec((tm, 1), lambda i: (i, 0)),
                  pl.BlockSpec((1, half), lambda i: (0, 0))],
        out_specs=(pl.BlockSpec((tm, half), lambda i: (i, 0)),
                   pl.BlockSpec((tm, half), lambda i: (i, 0))),
        compiler_params=_cparams(("arbitrary",)),
        name="rope_table",
    )(pos_col, inv_freq)


def _proj_in_kernel(x_ref, w_ref, wlr_ref, wa2_ref, ba_ref, o_ref, la_ref, xb_ref):
    @pl.when(pl.program_id(1) == 0)
    def _():
        xb_ref[...] = x_ref[...].astype(BF16)
        glr = _dot(xb_ref[...], wlr_ref[...])
        z = _dot(glr.astype(BF16), wa2_ref[...]) + ba_ref[...]
        la_ref[...] = (jnp.minimum(z, 0.0) - jnp.log(1.0 + jnp.exp(-jnp.abs(z)))) / GLA_TAU

    acc = None
    for c in range(D_MODEL // PROJ_K_CHUNK):
        ck = slice(c * PROJ_K_CHUNK, (c + 1) * PROJ_K_CHUNK)
        part = _dot(xb_ref[:, ck], w_ref[ck, :].astype(BF16))
        acc = part if acc is None else acc + part
    o_ref[...] = acc


def _proj_in(x2d, w_in, w_lr, w_a2, b_a):
    s = x2d.shape[0]
    tm = min(s, 1024)
    tn = 1024
    n = GLA_HEADS * GLA_DK
    return pl.pallas_call(
        _proj_in_kernel,
        out_shape=(jax.ShapeDtypeStruct((s, IN_MAIN), F32), jax.ShapeDtypeStruct((s, n), F32)),
        grid=(s // tm, IN_MAIN // tn),
        in_specs=[pl.BlockSpec((tm, D_MODEL), lambda i, j: (i, 0)),
                  pl.BlockSpec((D_MODEL, tn), lambda i, j: (0, j)),
                  pl.BlockSpec((D_MODEL, LANES), lambda i, j: (0, 0)),
                  pl.BlockSpec((LANES, n), lambda i, j: (0, 0)),
                  pl.BlockSpec((1, n), lambda i, j: (0, 0))],
        out_specs=(pl.BlockSpec((tm, tn), lambda i, j: (i, j)),
                   pl.BlockSpec((tm, n), lambda i, j: (i, 0))),
        scratch_shapes=[pltpu.VMEM((tm, D_MODEL), BF16)],
        compiler_params=_cparams(("arbitrary", "arbitrary")),
        name="proj_in",
    )(x2d, w_in, w_lr, w_a2, b_a)


def _rotary(t, cos, sin):
    half = t.shape[-1] // 2
    t1, t2 = t[:, :half], t[:, half:]
    return jnp.concatenate([t1 * cos - t2 * sin, t1 * sin + t2 * cos], axis=-1)


def _retention_kernel(lg_ref, q_ref, k_ref, v_ref, g_ref, cos_ref, sin_ref, o_ref, state_ref, intra_ref, dq_ref, dk_ref):
    c = pl.program_id(0)
    C = q_ref.shape[0]

    @pl.when(c == 0)
    def _():
        state_ref[...] = jnp.zeros_like(state_ref)
        ri = lax.broadcasted_iota(jnp.int32, (C, C), 0)
        ci = lax.broadcasted_iota(jnp.int32, (C, C), 1)
        rel = jnp.maximum(ri - ci, 0).astype(F32)
        n = lax.broadcasted_iota(jnp.int32, (C, RET_DK), 0).astype(F32)
        for h in range(RET_HEADS):
            lg = lg_ref[h]
            intra_ref[h] = jnp.where(ri >= ci, jnp.exp(lg * rel), 0.0)
            dq_ref[h] = jnp.exp(lg * (n + 1.0))
            dk_ref[h] = jnp.exp(lg * (C - 1.0 - n))

    cos = cos_ref[...]
    sin = sin_ref[...]

    for h in range(RET_HEADS):
        sl = slice(h * RET_DK, (h + 1) * RET_DK)
        q = _rotary(q_ref[:, sl], cos, sin)
        k = _rotary(k_ref[:, sl], cos, sin) * (RET_DK ** -0.5)
        v = v_ref[:, sl].astype(BF16)
        decay_q = dq_ref[h]
        decay_k = dk_ref[h]
        decay_chunk = jnp.exp(lg_ref[h] * C)

        state = state_ref[h]
        scores = _dot_nt(q.astype(BF16), k.astype(BF16)) * intra_ref[h]
        o = _dot(scores.astype(BF16), v) + _dot((q * decay_q).astype(BF16), state.astype(BF16))
        state_ref[h] = decay_chunk * state + _dot_tn((k * decay_k).astype(BF16), v)

        mu = jnp.mean(o, axis=-1, keepdims=True)
        d = o - mu
        var = jnp.mean(d * d, axis=-1, keepdims=True)
        o = d * lax.rsqrt(var + LN_EPS)
        o_ref[:, sl] = (_silu(g_ref[:, sl]) * o).astype(o_ref.dtype)


def _retention(h_main, cos, sin, log_gamma):
    s = h_main.shape[0]
    C = min(RET_CHUNK, s)
    w = RET_HEADS * RET_DK
    col = lambda off: (lambda c, lg: (c, off // w))
    grid_spec = pltpu.PrefetchScalarGridSpec(
        num_scalar_prefetch=1,
        grid=(s // C,),
        in_specs=[pl.BlockSpec((C, w), col(RQ_OFF)),
                  pl.BlockSpec((C, w), col(RK_OFF)),
                  pl.BlockSpec((C, w), col(RV_OFF)),
                  pl.BlockSpec((C, w), col(RG_OFF)),
                  pl.BlockSpec((C, RET_DK // 2), lambda c, lg: (c, 0)),
                  pl.BlockSpec((C, RET_DK // 2), lambda c, lg: (c, 0))],
        out_specs=pl.BlockSpec((C, RET_HEADS * RET_DV), lambda c, lg: (c, 0)),
        scratch_shapes=[pltpu.VMEM((RET_HEADS, RET_DK, RET_DV), F32),
                        pltpu.VMEM((RET_HEADS, C, C), F32),
                        pltpu.VMEM((RET_HEADS, C, RET_DK), F32),
                        pltpu.VMEM((RET_HEADS, C, RET_DK), F32)],
    )
    return pl.pallas_call(
        _retention_kernel,
        out_shape=jax.ShapeDtypeStruct((s, RET_HEADS * RET_DV), BF16),
        grid_spec=grid_spec,
        compiler_params=_cparams(("arbitrary",)),
        name="retention",
    )(log_gamma, h_main, h_main, h_main, h_main, cos, sin)


def _gla_decay_matrix(C):
    import numpy as np
    levels = int(math.log2(C))
    r = np.arange(C)[:, None]
    t = np.arange(C)[None, :]
    mats = []
    for l in range(levels):
        blk = C >> l
        half = blk // 2
        m = (r // blk) * blk + half - 1
        qside = (r % blk) >= half
        mats.append(np.where(qside, (t > m) & (t <= r), (t > r) & (t <= m)))
    mats.append(t <= r)
    mats.append(t > r)
    return np.concatenate(mats, axis=0).astype(np.float32)


def _gla_kernel(m_ref, q_ref, k_ref, v_ref, g_ref, la_ref, gn_ref, o_ref, state_ref):
    c = pl.program_id(0)
    C = GLA_CHUNK
    levels = GLA_LEVELS
    nsub = q_ref.shape[0] // C

    @pl.when(c == 0)
    def _():
        state_ref[...] = jnp.zeros_like(state_ref)

    m = m_ref[...]
    ri = lax.broadcasted_iota(jnp.int32, (C, C), 0)
    ci = lax.broadcasted_iota(jnp.int32, (C, C), 1)
    xor = jnp.where(ri > ci, ri ^ ci, 0)
    row = lax.broadcasted_iota(jnp.int32, (C, 1), 0)

    heads = range(GLA_HEADS)
    ks = [slice(h * GLA_DK, (h + 1) * GLA_DK) for h in heads]
    vs = [slice(h * GLA_DV, (h + 1) * GLA_DV) for h in heads]
    rs = [slice(u * C, (u + 1) * C) for u in range(nsub)]
    items = [(u, h) for u in range(nsub) for h in heads]

    expo, q, k, v, scores = {}, {}, {}, {}, {}
    for u, h in items:
        la = la_ref[rs[u], ks[h]]
        la_hi = la.astype(BF16)
        la_lo = (la - la_hi.astype(F32)).astype(BF16)
        expo[u, h] = jnp.exp(_dot(m, la_hi) + _dot(m, la_lo))
    for u, h in items:
        q[u, h] = q_ref[rs[u], ks[h]] * (GLA_DK ** -0.5)
        k[u, h] = k_ref[rs[u], ks[h]]
        v[u, h] = v_ref[rs[u], vs[h]].astype(BF16)
        scores[u, h] = jnp.where(ri == ci, _dot_nt(q[u, h].astype(BF16), k[u, h].astype(BF16)), 0.0)
    for l in range(levels):
        half = C >> (l + 1)
        shift = int(math.log2(half))
        qside = (row & half) != 0
        keep = (xor >> shift) == 1
        for it in items:
            x = (jnp.where(qside, q[it], k[it]) * expo[it][l * C:(l + 1) * C]).astype(BF16)
            scores[it] = scores[it] + jnp.where(keep, _dot_nt(x, x), 0.0)

    state = [state_ref[h] for h in heads]
    outs = {}
    for u, h in items:
        e_b = expo[u, h][levels * C:(levels + 1) * C]
        e_rev = expo[u, h][(levels + 1) * C:(levels + 2) * C]
        e_last = e_b[C - 1:C, :]
        outs[u, h] = (_dot(scores[u, h].astype(BF16), v[u, h])
                      + _dot_nt((q[u, h] * e_b).astype(BF16), state[h].astype(BF16)))
        state[h] = state[h] * e_last + _dot_tn(v[u, h], (k[u, h] * e_rev).astype(BF16))
    for h in heads:
        state_ref[h] = state[h]

    for u, h in items:
        o = outs[u, h]
        o = o * lax.rsqrt(jnp.mean(o * o, axis=-1, keepdims=True) + LN_EPS) * gn_ref[...]
        o_ref[rs[u], vs[h]] = (_silu(g_ref[rs[u], vs[h]]) * o).astype(o_ref.dtype)


def _gla(h_main, log_a, g_norm):
    s = h_main.shape[0]
    C = GLA_CHUNK
    m = jnp.asarray(_gla_decay_matrix(C), dtype=BF16)
    nrow = m.shape[0]
    rows = min(s, GLA_STEP_CHUNKS * C)
    wk = GLA_HEADS * GLA_DK
    wv = GLA_HEADS * GLA_DV
    return pl.pallas_call(
        _gla_kernel,
        out_shape=jax.ShapeDtypeStruct((s, wv), BF16),
        grid=(s // rows,),
        in_specs=[pl.BlockSpec((nrow, C), lambda c: (0, 0)),
                  pl.BlockSpec((rows, wk), lambda c: (c, GQ_OFF // wk)),
                  pl.BlockSpec((rows, wk), lambda c: (c, GK_OFF // wk)),
                  pl.BlockSpec((rows, wv), lambda c: (c, GV_OFF // wv)),
                  pl.BlockSpec((rows, wv), lambda c: (c, GG_OFF // wv)),
                  pl.BlockSpec((rows, wk), lambda c: (c, 0)),
                  pl.BlockSpec((1, GLA_DV), lambda c: (0, 0))],
        out_specs=pl.BlockSpec((rows, wv), lambda c: (c, 0)),
        scratch_shapes=[pltpu.VMEM((GLA_HEADS, GLA_DV, GLA_DK), F32)],
        compiler_params=_cparams(("arbitrary",)),
        name="gla",
    )(m, h_main, h_main, h_main, h_main, log_a, g_norm)


def _mixout_kernel(ret_ref, gla_ref, x_ref, w_ref, g_ref, b_ref, o_ref, wb_ref):
    nr = ret_ref.shape[1]

    @pl.when(pl.program_id(0) == 0)
    def _():
        wb_ref[...] = w_ref[...].astype(BF16)

    tm = x_ref.shape[0]
    parts = 2 if tm % 512 == 0 else 1
    rows = [slice(p * (tm // parts), (p + 1) * (tm // parts)) for p in range(parts)]
    mix = [_dot(ret_ref[r, :], wb_ref[:nr, :]) + _dot(gla_ref[r, :], wb_ref[nr:, :]) for r in rows]
    for r, m in zip(rows, mix):
        o_ref[r, :] = _layer_norm(DEEPNORM_ALPHA * x_ref[r, :] + m, g_ref[...], b_ref[...])


def _mixout_ln(ret, gla, x2d, w_b, g, b):
    s = x2d.shape[0]
    tm = min(s, 512)
    nr, ng = ret.shape[1], gla.shape[1]
    return pl.pallas_call(
        _mixout_kernel,
        out_shape=jax.ShapeDtypeStruct((s, D_MODEL), F32),
        grid=(s // tm,),
        in_specs=[pl.BlockSpec((tm, nr), lambda i: (i, 0)),
                  pl.BlockSpec((tm, ng), lambda i: (i, 0)),
                  pl.BlockSpec((tm, D_MODEL), lambda i: (i, 0)),
                  pl.BlockSpec((nr + ng, D_MODEL), lambda i: (0, 0), pipeline_mode=pl.Buffered(1)),
                  pl.BlockSpec((1, D_MODEL), lambda i: (0, 0)),
                  pl.BlockSpec((1, D_MODEL), lambda i: (0, 0))],
        out_specs=pl.BlockSpec((tm, D_MODEL), lambda i: (i, 0)),
        scratch_shapes=[pltpu.VMEM((nr + ng, D_MODEL), BF16)],
        compiler_params=_cparams(("arbitrary",)),
        name="mixout_ln1",
    )(ret, gla, x2d, w_b, g, b)


def _kv_kernel(mem_ref, wk_ref, wv_ref, k_ref, v_ref):
    m = mem_ref[...].astype(BF16)
    k_ref[...] = _dot(m, wk_ref[...].astype(BF16)).astype(k_ref.dtype)
    v_ref[...] = _dot(m, wv_ref[...].astype(BF16)).astype(v_ref.dtype)


def _mem_kv(mem_b, wk_b, wv_b):
    tn = 512
    return pl.pallas_call(
        _kv_kernel,
        out_shape=(jax.ShapeDtypeStruct((MEM_LEN, D_MODEL), BF16),
                   jax.ShapeDtypeStruct((MEM_LEN, D_MODEL), BF16)),
        grid=(D_MODEL // tn,),
        in_specs=[pl.BlockSpec((MEM_LEN, D_MODEL), lambda j: (0, 0)),
                  pl.BlockSpec((D_MODEL, tn), lambda j: (0, j)),
                  pl.BlockSpec((D_MODEL, tn), lambda j: (0, j))],
        out_specs=(pl.BlockSpec((MEM_LEN, tn), lambda j: (0, j)),
                   pl.BlockSpec((MEM_LEN, tn), lambda j: (0, j))),
        compiler_params=_cparams(("arbitrary",)),
        name="mem_kv",
    )(mem_b, wk_b, wv_b)


def _cross_kernel(h_ref, wq_ref, k_ref, v_ref, wo_ref, g_ref, b_ref, wr_ref, br_ref,
                  h2_ref, lg_ref):
    tm = h_ref.shape[0]
    parts = 2 if tm % 512 == 0 else 1
    pr = tm // parts
    rows = [slice(p * pr, (p + 1) * pr) for p in range(parts)]
    sls = [slice(hd * MEM_HEAD_DIM, (hd + 1) * MEM_HEAD_DIM) for hd in range(MEM_HEADS)]
    h1 = [h_ref[r, :] for r in rows]
    q = [_dot(h.astype(BF16), wq_ref[...]).astype(BF16) for h in h1]
    o = []
    for p in range(parts):
        outs = []
        for sl in sls:
            s = _dot_nt(q[p][:, sl], k_ref[:, sl]) * (MEM_HEAD_DIM ** -0.5)
            s = s - jnp.max(s, axis=-1, keepdims=True)
            e = jnp.exp(s)
            e = e / jnp.sum(e, axis=-1, keepdims=True)
            outs.append(_dot(e.astype(BF16), v_ref[:, sl]))
        o.append(jnp.concatenate(outs, axis=-1).astype(BF16))
    cross = [_dot(x, wo_ref[...]) for x in o]
    for p in range(parts):
        h2 = _layer_norm(DEEPNORM_ALPHA * h1[p] + cross[p], g_ref[...], b_ref[...])
        h2_ref[rows[p], :] = h2
        lg_ref[rows[p], :] = _dot(h2.astype(BF16), wr_ref[...]) + br_ref[...]


def _cross_attention(h1, wq_b, k, v, wo_b, g, b, w_route, b_route):
    s = h1.shape[0]
    tm = min(s, 512)
    const = lambda shape: pl.BlockSpec(shape, lambda i: (0, 0), pipeline_mode=pl.Buffered(1))
    return pl.pallas_call(
        _cross_kernel,
        out_shape=(jax.ShapeDtypeStruct((s, D_MODEL), F32),
                   jax.ShapeDtypeStruct((s, LANES), F32)),
        grid=(s // tm,),
        in_specs=[pl.BlockSpec((tm, D_MODEL), lambda i: (i, 0)),
                  const((D_MODEL, D_MODEL)),
                  const((MEM_LEN, D_MODEL)),
                  const((MEM_LEN, D_MODEL)),
                  const((D_MODEL, D_MODEL)),
                  const((1, D_MODEL)),
                  const((1, D_MODEL)),
                  const((D_MODEL, LANES)),
                  const((1, LANES))],
        out_specs=(pl.BlockSpec((tm, D_MODEL), lambda i: (i, 0)),
                   pl.BlockSpec((tm, LANES), lambda i: (i, 0))),
        compiler_params=_cparams(("arbitrary",)),
        name="cross_attn_ln2",
    )(h1, wq_b, k, v, wo_b, g, b, w_route, b_route)


def _route_kernel(lg_ref, slot_ref, slot_t_ref, gate_ref, runs_ref, plan_ref, tot_row, tot_col, gstart, rinfo):
    phase = pl.program_id(0)
    i = pl.program_id(1)
    tm = lg_ref.shape[0]
    lane = lax.broadcasted_iota(jnp.int32, (tm, LANES), 1)
    rows = pl.ds(pl.multiple_of(i * tm, tm), tm)

    @pl.when(phase == 0)
    def _():
        neg = -jnp.inf
        logits = lg_ref[...]
        gmask = lane < N_GROUPS
        gl = jnp.where(gmask, logits, neg)
        ge = jnp.exp(gl - jnp.max(gl, axis=-1, keepdims=True))
        pg = ge / jnp.sum(ge, axis=-1, keepdims=True)
        pg_sel = jnp.max(pg, axis=-1, keepdims=True)
        grp = jnp.min(jnp.where((pg == pg_sel) & gmask, lane, LANES), axis=-1, keepdims=True)

        fl_lane = lane - N_GROUPS
        fmask = (fl_lane >= 0) & (fl_lane < N_EXPERTS) & ((fl_lane >> 3) == grp)
        fl = jnp.where(fmask, logits, neg)
        fe = jnp.exp(fl - jnp.max(fl, axis=-1, keepdims=True))
        fp = fe / jnp.sum(fe, axis=-1, keepdims=True)
        p1 = jnp.max(fp, axis=-1, keepdims=True)
        i1 = jnp.min(jnp.where((fp == p1) & fmask, lane, LANES), axis=-1, keepdims=True)
        rest = fmask & (lane != i1)
        fp2 = jnp.where(rest, fp, -1.0)
        p2 = jnp.max(fp2, axis=-1, keepdims=True)
        i2 = jnp.min(jnp.where((fp2 == p2) & rest, lane, LANES), axis=-1, keepdims=True)
        psum = p1 + p2
        picks = ((i1 - N_GROUPS).astype(F32), (i2 - N_GROUPS).astype(F32),
                 pg_sel * p1 / psum, pg_sel * p2 / psum)
        info = jnp.zeros((tm, LANES), F32)
        for j, val in enumerate(picks):
            info = jnp.where(lane == j, val, info)
        rinfo[rows, :] = info

    info = rinfo[rows, :]
    oh1 = lane == info[:, 0:1].astype(jnp.int32)
    oh2 = lane == info[:, 1:2].astype(jnp.int32)
    gate1 = info[:, 2:3]
    gate2 = info[:, 3:4]
    oh = (jnp.where(oh1, 1.0, 0.0) + jnp.where(oh2, 1.0, 0.0)).astype(BF16)
    ones = jnp.ones((tm, LANES), BF16)

    def align_up(v, a):
        return jnp.floor((v + (a - 1.0)) * (1.0 / a)) * a

    run_row = align_up(_dot_tn(ones, oh)[0:8, :], RUN_ALIGN)
    rr = lax.broadcasted_iota(jnp.int32, (LANES, LANES), 0)
    cc = lax.broadcasted_iota(jnp.int32, (LANES, LANES), 1)
    strict_upper = jnp.where(rr < cc, 1.0, 0.0).astype(BF16)

    @pl.when((phase == 0) & (i == 0))
    def _():
        tot_row[...] = jnp.zeros_like(tot_row)
        tot_col[...] = jnp.zeros_like(tot_col)

    @pl.when(phase == 0)
    def _():
        tot_row[...] += run_row
        tot_col[...] += align_up(_dot_tn(oh, ones), RUN_ALIGN)

    @pl.when((phase == 1) & (i == 0))
    def _():
        nblk_row = align_up(tot_row[...], MOE_BLK) * (1.0 / MOE_BLK)
        nblk_col = align_up(tot_col[...], MOE_BLK) * (1.0 / MOE_BLK)
        lower_incl = jnp.where(cc <= rr, 1.0, 0.0).astype(BF16)
        base = _dot(nblk_row.astype(BF16), strict_upper) * float(MOE_BLK)
        ends = _dot(lower_incl, nblk_col.astype(BF16))
        expert_rows = rr < N_EXPERTS
        be = jnp.sum(jnp.where(expert_rows & (ends <= cc.astype(F32)), 1.0, 0.0), axis=0, keepdims=True)
        be = jnp.minimum(be, N_EXPERTS - 1.0)
        total = jnp.sum(jnp.where(expert_rows, nblk_col, 0.0), axis=0, keepdims=True)
        sub = lax.broadcasted_iota(jnp.int32, plan_ref.shape, 0)
        plan = jnp.where(sub == 0, jnp.broadcast_to(be, plan_ref.shape),
                         jnp.where(sub == 1, jnp.broadcast_to(total, plan_ref.shape),
                                   jnp.where(sub == 2, base + tot_row[...],
                                             jnp.where(sub == 3, nblk_row,
                                                       nblk_row * float(MOE_BLK) - tot_row[...]))))
        plan_ref[...] = plan.astype(jnp.int32)
        gstart[...] = base

    @pl.when(phase == 1)
    def _():
        off = _dot((run_row * (1.0 / RUN_ALIGN)).astype(BF16), strict_upper) * float(RUN_ALIGN)
        tr = lax.broadcasted_iota(jnp.int32, (tm, tm), 0)
        tc = lax.broadcasted_iota(jnp.int32, (tm, tm), 1)
        strict_lower = jnp.where(tc < tr, 1.0, 0.0).astype(BF16)
        slot = _dot(strict_lower, oh) + off[0:1, :]
        s1 = jnp.sum(jnp.where(oh1, slot, 0.0), axis=-1, keepdims=True)
        s2 = jnp.sum(jnp.where(oh2, slot, 0.0), axis=-1, keepdims=True)
        slots = jnp.where(lane == 0, s1, jnp.where(lane == 1, s2, 0.0))
        slot_ref[...] = slots.astype(jnp.int32)
        slot_t_ref[...] = jnp.transpose(slots)[0:8, :].astype(jnp.int32)
        gate_ref[...] = jnp.where(lane == 0, gate1, jnp.where(lane == 1, gate2, 0.0))
        sub = lax.broadcasted_iota(jnp.int32, runs_ref.shape, 0)
        runs = jnp.where(sub == 0, gstart[...], jnp.where(sub == 1, off, run_row))
        runs_ref[...] = runs.astype(jnp.int32)
        gstart[...] += run_row


def _route(logits):
    t = logits.shape[0]
    tm = min(t, TILE_TOK)
    nt = t // tm
    step = lambda p, i: (i * p, 0)
    return pl.pallas_call(
        _route_kernel,
        out_shape=(jax.ShapeDtypeStruct((t, LANES), jnp.int32),
                   jax.ShapeDtypeStruct((nt * 8, tm), jnp.int32),
                   jax.ShapeDtypeStruct((t, LANES), F32),
                   jax.ShapeDtypeStruct((nt * 8, LANES), jnp.int32),
                   jax.ShapeDtypeStruct((8, LANES), jnp.int32)),
        grid=(2, nt),
        in_specs=[pl.BlockSpec((tm, LANES), lambda p, i: (i, 0))],
        out_specs=(pl.BlockSpec((tm, LANES), step),
                   pl.BlockSpec((8, tm), step),
                   pl.BlockSpec((tm, LANES), step),
                   pl.BlockSpec((8, LANES), step),
                   pl.BlockSpec((8, LANES), lambda p, i: (0, 0))),
        scratch_shapes=[pltpu.VMEM((8, LANES), F32),
                        pltpu.VMEM((LANES, LANES), F32),
                        pltpu.VMEM((8, LANES), F32),
                        pltpu.VMEM((t, LANES), F32)],
        compiler_params=_cparams(("arbitrary", "arbitrary")),
        name="moe_route",
    )(logits)


def _run_copies(rg_ref, ro_ref, rn_ref, tile, make_copy):
    def each(action):
        def body(e, carry):
            j = tile * N_EXPERTS + e
            n = rn_ref[j]

            @pl.when(n > 0)
            def _():
                action(make_copy(pl.multiple_of(rg_ref[j], RUN_ALIGN), pl.multiple_of(ro_ref[j], RUN_ALIGN),
                                 pl.multiple_of(n, RUN_ALIGN)))
            return carry
        lax.fori_loop(0, N_EXPERTS, body, 0)
    return each


def _dispatch_kernel(rg_ref, ro_ref, rn_ref, zrow_ref, zlen_ref, nu_ref, x_ref, slot_t_ref, xb_ref,
                     sorted_ref, zero_ref, sems, zsem):
    i = pl.program_id(0)
    n = pl.num_programs(0)
    tm = x_ref.shape[0]
    nb = xb_ref.shape[0] // MOE_BLK
    buf = i % 2

    def zero_copy(row, size):
        return pltpu.make_async_copy(
            zero_ref.at[pl.ds(0, size), :], xb_ref.at[pl.ds(pl.multiple_of(row, RUN_ALIGN), size), :], zsem)

    def pad_copy(e):
        return zero_copy(zrow_ref[e], pl.multiple_of(zlen_ref[e], RUN_ALIGN))

    def runs_of(tile, slot):
        def make_copy(g, o, size):
            return pltpu.make_async_copy(sorted_ref.at[slot, pl.ds(o, size), :],
                                         xb_ref.at[pl.ds(g, size), :], sems.at[slot])
        return _run_copies(rg_ref, ro_ref, rn_ref, tile, make_copy)

    @pl.when(i == 0)
    def _():
        zero_ref[...] = jnp.zeros_like(zero_ref)

        def issue_zero(e, carry):
            @pl.when(zlen_ref[e] > 0)
            def _():
                pad_copy(e).start()
            return carry
        lax.fori_loop(0, N_EXPERTS, issue_zero, 0)
        lax.fori_loop(nu_ref[0], nb, lambda b, c: (zero_copy(b * MOE_BLK, MOE_BLK).start(), c)[1], 0)

        def wait_zero(e, carry):
            @pl.when(zlen_ref[e] > 0)
            def _():
                pad_copy(e).wait()
            return carry
        lax.fori_loop(0, N_EXPERTS, wait_zero, 0)
        lax.fori_loop(nu_ref[0], nb, lambda b, c: (zero_copy(0, MOE_BLK).wait(), c)[1], 0)

    @pl.when(i >= 2)
    def _():
        runs_of(i - 2, buf)(lambda c: c.wait())

    x = x_ref[...].astype(BF16)
    s1 = slot_t_ref[0:1, :]
    s2 = slot_t_ref[1:2, :]
    for rc in range(TILE_SLOTS // SORT_ROWS):
        r = lax.broadcasted_iota(jnp.int32, (SORT_ROWS, tm), 0) + rc * SORT_ROWS
        perm = jnp.where((r == s1) | (r == s2), 1.0, 0.0).astype(BF16)
        sorted_ref[buf, rc * SORT_ROWS:(rc + 1) * SORT_ROWS, :] = _dot(perm, x).astype(BF16)

    runs_of(i, buf)(lambda c: c.start())

    @pl.when(i == n - 1)
    def _():
        @pl.when(i >= 1)
        def _():
            runs_of(i - 1, 1 - buf)(lambda c: c.wait())
        runs_of(i, buf)(lambda c: c.wait())


def _dispatch(rg, ro, rn, zrow, zlen, n_used, h2, slot_t, n_rows):
    t = h2.shape[0]
    tm = min(t, TILE_TOK)
    grid_spec = pltpu.PrefetchScalarGridSpec(
        num_scalar_prefetch=6,
        grid=(t // tm,),
        in_specs=[pl.BlockSpec((tm, D_MODEL), lambda i, *_: (i, 0)),
                  pl.BlockSpec((8, tm), lambda i, *_: (i, 0))],
        out_specs=pl.BlockSpec(memory_space=pl.ANY),
        scratch_shapes=[pltpu.VMEM((2, TILE_SLOTS, D_MODEL), BF16),
                        pltpu.VMEM((MOE_BLK, D_MODEL), BF16),
                        pltpu.SemaphoreType.DMA((2,)),
                        pltpu.SemaphoreType.DMA(())],
    )
    return pl.pallas_call(
        _dispatch_kernel,
        out_shape=jax.ShapeDtypeStruct((n_rows, D_MODEL), BF16),
        grid_spec=grid_spec,
        compiler_params=_cparams(("arbitrary",)),
        name="moe_dispatch",
    )(rg, ro, rn, zrow, zlen, n_used, h2, slot_t)


def _expert_kernel(be_ref, nu_ref, nblk_ref, x_ref, wg_ref, wu_ref, wd_ref, y_ref,
                   wg_f, wu_f, wd_f, wg_b, wu_b, wd_b, sems, cur_ref):
    step = pl.program_id(0)
    n_used = nu_ref[0]

    def weight_copies(expert, slot):
        half = EXPERT_FF // 2
        return [pltpu.make_async_copy(wg_ref.at[expert], wg_f.at[slot], sems.at[slot]),
                pltpu.make_async_copy(wu_ref.at[expert], wu_f.at[slot], sems.at[slot]),
                pltpu.make_async_copy(wd_ref.at[expert, :half], wd_f.at[slot, :half], sems.at[slot]),
                pltpu.make_async_copy(wd_ref.at[expert, half:], wd_f.at[slot, half:], sems.at[slot])]

    def start_weights(expert, slot):
        for c in weight_copies(expert, slot):
            c.start(priority=1)

    def after(blk):
        return blk + nblk_ref[be_ref[blk]]

    @pl.when(step == 0)
    def _():
        cur_ref[0] = 0
        start_weights(be_ref[0], 0)

        @pl.when(after(0) < n_used)
        def _():
            start_weights(be_ref[after(0)], 1)

    for sub in range(EXPERT_STEP_BLKS):
        b = step * EXPERT_STEP_BLKS + sub
        used = b < n_used
        e = be_ref[b]
        prev = be_ref[jnp.maximum(b - 1, 0)]
        fresh = used & ((b == 0) | (e != prev))
        rows = slice(sub * MOE_BLK, (sub + 1) * MOE_BLK)

        @pl.when(fresh)
        def _():
            slot = cur_ref[0]
            for c in weight_copies(e, slot):
                c.wait()
            wg_b[...] = wg_f[slot].astype(BF16)
            wu_b[...] = wu_f[slot].astype(BF16)
            wd_b[...] = wd_f[slot].astype(BF16)
            nxt = after(b)

            @pl.when(nxt < n_used)
            def _():
                nxt2 = after(nxt)

                @pl.when(nxt2 < n_used)
                def _():
                    start_weights(be_ref[nxt2], slot)
            cur_ref[0] = 1 - slot

        @pl.when(used)
        def _():
            x = x_ref[rows, :]
            hid = _silu(_dot(x, wg_b[...])) * _dot(x, wu_b[...])
            y_ref[rows, :] = _dot(hid.astype(BF16), wd_b[...]).astype(y_ref.dtype)

        @pl.when(jnp.logical_not(used))
        def _():
            y_ref[rows, :] = jnp.zeros((MOE_BLK, D_MODEL), y_ref.dtype)


def _experts(block_e, n_used, nblk, xb, w_gate, w_up, w_down):
    step_rows = EXPERT_STEP_BLKS * MOE_BLK
    any_space = pl.BlockSpec(memory_space=pl.ANY)
    last_step = lambda nu: (nu[0] - 1) // EXPERT_STEP_BLKS
    grid_spec = pltpu.PrefetchScalarGridSpec(
        num_scalar_prefetch=3,
        grid=(xb.shape[0] // step_rows,),
        in_specs=[pl.BlockSpec((step_rows, D_MODEL), lambda s, be, nu, nk: (jnp.minimum(s, last_step(nu)), 0)),
                  any_space, any_space, any_space],
        out_specs=pl.BlockSpec((step_rows, D_MODEL), lambda s, be, nu, nk: (s, 0)),
        scratch_shapes=[pltpu.VMEM((2, D_MODEL, EXPERT_FF), F32),
                        pltpu.VMEM((2, D_MODEL, EXPERT_FF), F32),
                        pltpu.VMEM((2, EXPERT_FF, D_MODEL), F32),
                        pltpu.VMEM((D_MODEL, EXPERT_FF), BF16),
                        pltpu.VMEM((D_MODEL, EXPERT_FF), BF16),
                        pltpu.VMEM((EXPERT_FF, D_MODEL), BF16),
                        pltpu.SemaphoreType.DMA((2,)),
                        pltpu.SMEM((1,), jnp.int32)],
    )
    return pl.pallas_call(
        _expert_kernel,
        out_shape=jax.ShapeDtypeStruct(xb.shape, BF16),
        grid_spec=grid_spec,
        compiler_params=_cparams(("arbitrary",)),
        name="moe_experts",
    )(block_e, n_used, nblk, xb, w_gate, w_up, w_down)


def _combine_kernel(rg_ref, ro_ref, rn_ref, yb_ref, h2_ref, slot_ref, gate_ref, g_ref, b_ref, o_ref,
                    ybuf, sems):
    i = pl.program_id(0)
    n = pl.num_programs(0)
    tm = h2_ref.shape[0]
    buf = i % 2

    def runs_of(tile, slot):
        def make_copy(g, o, size):
            return pltpu.make_async_copy(yb_ref.at[pl.ds(g, size), :],
                                         ybuf.at[slot, pl.ds(o, size), :], sems.at[slot])
        return _run_copies(rg_ref, ro_ref, rn_ref, tile, make_copy)

    @pl.when(i == 0)
    def _():
        ybuf[...] = jnp.zeros_like(ybuf)
        runs_of(0, 0)(lambda c: c.start())

    @pl.when(i + 1 < n)
    def _():
        runs_of(i + 1, 1 - buf)(lambda c: c.start())

    runs_of(i, buf)(lambda c: c.wait())

    lane = lax.broadcasted_iota(jnp.int32, (tm, TILE_SLOTS), 1)
    slots = slot_ref[...]
    gate = gate_ref[...]
    y = ybuf[buf]
    sel = jnp.where(lane == slots[:, 0:1], gate[:, 0:1],
                    jnp.where(lane == slots[:, 1:2], gate[:, 1:2], 0.0)).astype(BF16)
    parts = 2 if tm % 512 == 0 else 1
    rows = [slice(p * (tm // parts), (p + 1) * (tm // parts)) for p in range(parts)]
    ffn = [_dot(sel[r, :], y) for r in rows]
    for r, f in zip(rows, ffn):
        o_ref[r, :] = _layer_norm(DEEPNORM_ALPHA * h2_ref[r, :] + f, g_ref[...], b_ref[...])


def _combine_ln(rg, ro, rn, yb, h2, slots, gate, g, b):
    t = h2.shape[0]
    tm = min(t, TILE_TOK)
    grid_spec = pltpu.PrefetchScalarGridSpec(
        num_scalar_prefetch=3,
        grid=(t // tm,),
        in_specs=[pl.BlockSpec(memory_space=pl.ANY),
                  pl.BlockSpec((tm, D_MODEL), lambda i, *_: (i, 0)),
                  pl.BlockSpec((tm, LANES), lambda i, *_: (i, 0)),
                  pl.BlockSpec((tm, LANES), lambda i, *_: (i, 0)),
                  pl.BlockSpec((1, D_MODEL), lambda i, *_: (0, 0)),
                  pl.BlockSpec((1, D_MODEL), lambda i, *_: (0, 0))],
        out_specs=pl.BlockSpec((tm, D_MODEL), lambda i, *_: (i, 0)),
        scratch_shapes=[pltpu.VMEM((2, TILE_SLOTS, D_MODEL), BF16),
                        pltpu.SemaphoreType.DMA((2,))],
    )
    return pl.pallas_call(
        _combine_kernel,
        out_shape=jax.ShapeDtypeStruct((t, D_MODEL), F32),
        grid_spec=grid_spec,
        compiler_params=_cparams(("arbitrary",)),
        name="moe_combine_ln3",
    )(rg, ro, rn, yb, h2, slots, gate, g, b)


def _mixer(x2d, positions, w_in, w_gla_a2, b_gla_a, g_gla_norm):
    s = x2d.shape[0]
    half = RET_DK // 2
    inv_freq = (ROPE_BASE ** (-jnp.arange(half, dtype=F32) / half)).reshape(1, half)
    cos, sin = _rope_table(positions.reshape(s, 1), inv_freq)
    w_lr = jnp.pad(w_in[:, GLR_OFF:].astype(BF16), ((0, 0), (0, LANES - GLA_LOWRANK)))
    w_a2 = jnp.pad(w_gla_a2.astype(BF16), ((0, LANES - GLA_LOWRANK), (0, 0)))
    h_main, log_a = _proj_in(x2d, w_in, w_lr, w_a2, b_gla_a.reshape(1, -1))
    log_gamma = jnp.log1p(-jnp.exp2(-5.0 - jnp.arange(RET_HEADS, dtype=F32)))
    ret = _retention(h_main, cos, sin, log_gamma)
    gla = _gla(h_main, log_a, g_gla_norm.reshape(1, -1))
    return ret, gla


def _moe(h2, logits, w_gate, w_up, w_down, g, b):
    t = h2.shape[0]
    nt = t // min(t, TILE_TOK)
    slots, slot_t, gate, runs, plan = _route(logits)
    runs = runs.reshape(nt, 8, LANES)[:, :3, :N_EXPERTS]
    rg, ro, rn = (runs[:, j, :].reshape(-1) for j in range(3))
    max_rows = 2 * t + nt * N_EXPERTS * (RUN_ALIGN - 1) + N_EXPERTS * (MOE_BLK - 1)
    nb = -(-max_rows // (MOE_BLK * EXPERT_STEP_BLKS)) * EXPERT_STEP_BLKS
    block_e, n_used = plan[0, :nb], plan[1, :1]
    pad_row, nblk, pad_len = (plan[j, :N_EXPERTS] for j in (2, 3, 4))
    xb = _dispatch(rg, ro, rn, pad_row, pad_len, n_used, h2, slot_t, nb * MOE_BLK)
    yb = _experts(block_e, n_used, nblk, xb, w_gate, w_up, w_down)
    return _combine_ln(rg, ro, rn, yb, h2, slots, gate, g, b)


def kernel(x, mem, positions, w_in, w_gla_a2, b_gla_a, g_gla_norm, w_mix_out, ln1_g, ln1_b, w_mq, w_mk, w_mv, w_mo, ln2_g, ln2_b, w_route_group, b_route_group, w_route_expert, b_route_expert, w_exp_gate, w_exp_up, w_exp_down, ln3_g, ln3_b):
    bsz, s, d = x.shape
    assert bsz == 1 and d == D_MODEL
    x2d = x.reshape(s, d)
    row = lambda v: v.reshape(1, -1)

    ret, gla = _mixer(x2d, positions, w_in[0], w_gla_a2[0], b_gla_a[0], g_gla_norm[0])
    h1 = _mixout_ln(ret, gla, x2d, w_mix_out[0], row(ln1_g[0]), row(ln1_b[0]))

    k, v = _mem_kv(mem[0], w_mk[0], w_mv[0])
    n_route = N_GROUPS + N_EXPERTS
    w_route = jnp.pad(jnp.concatenate([w_route_group[0], w_route_expert[0]], axis=1).astype(BF16),
                      ((0, 0), (0, LANES - n_route)))
    b_route = jnp.pad(jnp.concatenate([b_route_group[0], b_route_expert[0].reshape(-1)]),
                      (0, LANES - n_route)).reshape(1, LANES)
    h2, logits = _cross_attention(h1, w_mq[0].astype(BF16), k, v, w_mo[0].astype(BF16),
                                  row(ln2_g[0]), row(ln2_b[0]), w_route, b_route)

    out = _moe(h2, logits, w_exp_gate[0], w_exp_up[0], w_exp_down[0],
               row(ln3_g[0]), row(ln3_b[0]))
    return out.reshape(bsz, s, d)
```

```python
import math

import jax
import jax.numpy as jnp
from jax import lax
from jax.experimental import pallas as pl
from jax.experimental.pallas import tpu as pltpu

F32 = jnp.float32
BF16 = jnp.bfloat16

D_MODEL = 2048
MEM_LEN = 256
RET_HEADS = 4
RET_DK = 256
RET_DV = 256
GLA_HEADS = 4
GLA_DK = 128
GLA_DV = 256
GLA_LOWRANK = 16
GLA_TAU = 16.0
ROPE_BASE = 10000.0
MEM_HEADS = 4
MEM_HEAD_DIM = D_MODEL // MEM_HEADS
N_GROUPS = 4
EXPERTS_PER_GROUP = 8
N_EXPERTS = N_GROUPS * EXPERTS_PER_GROUP
EXPERT_FF = 512
LN_EPS = 1e-5
DEPTH = 1
DEEPNORM_ALPHA = (2 * DEPTH) ** 0.25

RQ_OFF, RK_OFF, RV_OFF, RG_OFF = 0, 1024, 2048, 3072
GQ_OFF, GK_OFF, GV_OFF, GG_OFF, GLR_OFF = 4096, 4608, 5120, 6144, 7168
IN_MAIN = 7168

LANES = 128
RET_CHUNK = 256
GLA_CHUNK = 128
GLA_LEVELS = 7
GLA_STEP_CHUNKS = 4
MOE_BLK = 256
EXPERT_STEP_BLKS = 4
TILE_TOK = 512
RUN_ALIGN = 16
TILE_SLOTS = 2 * TILE_TOK + N_EXPERTS * RUN_ALIGN
SORT_ROWS = 256
PROJ_K_CHUNK = 512
VMEM_LIMIT = 56 * 1024 * 1024


def _cparams(sem):
    return pltpu.CompilerParams(dimension_semantics=sem, vmem_limit_bytes=VMEM_LIMIT)


def _layer_norm(y, g, b):
    mu = jnp.mean(y, axis=-1, keepdims=True)
    d = y - mu
    var = jnp.mean(d * d, axis=-1, keepdims=True)
    return d * lax.rsqrt(var + LN_EPS) * g + b


def _silu(x):
    return x / (1.0 + jnp.exp(-x))


def _dot(a, b):
    return jnp.dot(a, b, preferred_element_type=F32)


def _dot_nt(a, b):
    return lax.dot_general(a, b, (((1,), (1,)), ((), ())), preferred_element_type=F32)


def _dot_tn(a, b):
    return lax.dot_general(a, b, (((0,), (0,)), ((), ())), preferred_element_type=F32)


def _rope_kernel(pos_ref, invf_ref, cos_ref, sin_ref):
    ang = pos_ref[...].astype(F32) * invf_ref[...]
    cos_ref[...] = jnp.cos(ang)
    sin_ref[...] = jnp.sin(ang)


def _rope_table(pos_col, inv_freq):
    s = pos_col.shape[0]
    tm = min(s, 1024)
    half = inv_freq.shape[1]
    return pl.pallas_call(
        _rope_kernel,
        out_shape=(jax.ShapeDtypeStruct((s, half), F32), jax.ShapeDtypeStruct((s, half), F32)),
        grid=(s // tm,),
        in_specs=[pl.BlockSpec((tm, 1), lambda i: (i, 0)),
                  pl.BlockSpec((1, half), lambda i: (0, 0))],
        out_specs=(pl.BlockSpec((tm, half), lambda i: (i, 0)),
                   pl.BlockSpec((tm, half), lambda i: (i, 0))),
        compiler_params=_cparams(("arbitrary",)),
        name="rope_table",
    )(pos_col, inv_freq)


def _proj_in_kernel(x_ref, wt_ref, wlr_ref, wa2_ref, ba_ref, o_ref, la_ref, xb_ref):
    @pl.when(pl.program_id(1) == 0)
    def _():
        xb_ref[...] = x_ref[...].astype(BF16)
        glr = _dot_nt(xb_ref[...], wlr_ref[...])
        z = _dot(glr.astype(BF16), wa2_ref[...]) + ba_ref[...]
        la_ref[...] = (jnp.minimum(z, 0.0) - jnp.log(1.0 + jnp.exp(-jnp.abs(z)))) / GLA_TAU

    acc = None
    for c in range(D_MODEL // PROJ_K_CHUNK):
        ck = slice(c * PROJ_K_CHUNK, (c + 1) * PROJ_K_CHUNK)
        part = _dot_nt(xb_ref[:, ck], wt_ref[:, ck].astype(BF16))
        acc = part if acc is None else acc + part
    o_ref[...] = acc


def _proj_in(x2d, w_in_t, w_lr_t, w_a2, b_a):
    s = x2d.shape[0]
    tm = min(s, 1024)
    tn = 1024
    n = GLA_HEADS * GLA_DK
    return pl.pallas_call(
        _proj_in_kernel,
        out_shape=(jax.ShapeDtypeStruct((s, IN_MAIN), F32), jax.ShapeDtypeStruct((s, n), F32)),
        grid=(s // tm, IN_MAIN // tn),
        in_specs=[pl.BlockSpec((tm, D_MODEL), lambda i, j: (i, 0)),
                  pl.BlockSpec((tn, D_MODEL), lambda i, j: (j, 0)),
                  pl.BlockSpec((LANES, D_MODEL), lambda i, j: (0, 0)),
                  pl.BlockSpec((LANES, n), lambda i, j: (0, 0)),
                  pl.BlockSpec((1, n), lambda i, j: (0, 0))],
        out_specs=(pl.BlockSpec((tm, tn), lambda i, j: (i, j)),
                   pl.BlockSpec((tm, n), lambda i, j: (i, 0))),
        scratch_shapes=[pltpu.VMEM((tm, D_MODEL), BF16)],
        compiler_params=_cparams(("arbitrary", "arbitrary")),
        name="proj_in",
    )(x2d, w_in_t, w_lr_t, w_a2, b_a)


def _rotary(t, cos, sin):
    half = t.shape[-1] // 2
    t1, t2 = t[:, :half], t[:, half:]
    return jnp.concatenate([t1 * cos - t2 * sin, t1 * sin + t2 * cos], axis=-1)


def _retention_kernel(lg_ref, q_ref, k_ref, v_ref, g_ref, cos_ref, sin_ref, o_ref, state_ref, intra_ref, dq_ref, dk_ref):
    c = pl.program_id(0)
    C = q_ref.shape[0]

    @pl.when(c == 0)
    def _():
        state_ref[...] = jnp.zeros_like(state_ref)
        ri = lax.broadcasted_iota(jnp.int32, (C, C), 0)
        ci = lax.broadcasted_iota(jnp.int32, (C, C), 1)
        rel = jnp.maximum(ri - ci, 0).astype(F32)
        n = lax.broadcasted_iota(jnp.int32, (C, RET_DK), 0).astype(F32)
        for h in range(RET_HEADS):
            lg = lg_ref[h]
            intra_ref[h] = jnp.where(ri >= ci, jnp.exp(lg * rel), 0.0)
            dq_ref[h] = jnp.exp(lg * (n + 1.0))
            dk_ref[h] = jnp.exp(lg * (C - 1.0 - n))

    cos = cos_ref[...]
    sin = sin_ref[...]

    for h in range(RET_HEADS):
        sl = slice(h * RET_DK, (h + 1) * RET_DK)
        q = _rotary(q_ref[:, sl], cos, sin)
        k = _rotary(k_ref[:, sl], cos, sin) * (RET_DK ** -0.5)
        v = v_ref[:, sl].astype(BF16)
        decay_q = dq_ref[h]
        decay_k = dk_ref[h]
        decay_chunk = jnp.exp(lg_ref[h] * C)

        state = state_ref[h]
        scores = _dot_nt(q.astype(BF16), k.astype(BF16)) * intra_ref[h]
        o = _dot(scores.astype(BF16), v) + _dot((q * decay_q).astype(BF16), state.astype(BF16))
        state_ref[h] = decay_chunk * state + _dot_tn((k * decay_k).astype(BF16), v)

        mu = jnp.mean(o, axis=-1, keepdims=True)
        d = o - mu
        var = jnp.mean(d * d, axis=-1, keepdims=True)
        o = d * lax.rsqrt(var + LN_EPS)
        o_ref[:, sl] = (_silu(g_ref[:, sl]) * o).astype(o_ref.dtype)


def _retention(h_main, cos, sin, log_gamma):
    s = h_main.shape[0]
    C = min(RET_CHUNK, s)
    w = RET_HEADS * RET_DK
    col = lambda off: (lambda c, lg: (c, off // w))
    grid_spec = pltpu.PrefetchScalarGridSpec(
        num_scalar_prefetch=1,
        grid=(s // C,),
        in_specs=[pl.BlockSpec((C, w), col(RQ_OFF)),
                  pl.BlockSpec((C, w), col(RK_OFF)),
                  pl.BlockSpec((C, w), col(RV_OFF)),
                  pl.BlockSpec((C, w), col(RG_OFF)),
                  pl.BlockSpec((C, RET_DK // 2), lambda c, lg: (c, 0)),
                  pl.BlockSpec((C, RET_DK // 2), lambda c, lg: (c, 0))],
        out_specs=pl.BlockSpec((C, RET_HEADS * RET_DV), lambda c, lg: (c, 0)),
        scratch_shapes=[pltpu.VMEM((RET_HEADS, RET_DK, RET_DV), F32),
                        pltpu.VMEM((RET_HEADS, C, C), F32),
                        pltpu.VMEM((RET_HEADS, C, RET_DK), F32),
                        pltpu.VMEM((RET_HEADS, C, RET_DK), F32)],
    )
    return pl.pallas_call(
        _retention_kernel,
        out_shape=jax.ShapeDtypeStruct((s, RET_HEADS * RET_DV), BF16),
        grid_spec=grid_spec,
        compiler_params=_cparams(("arbitrary",)),
        name="retention",
    )(log_gamma, h_main, h_main, h_main, h_main, cos, sin)


def _gla_decay_matrix(C):
    import numpy as np
    levels = int(math.log2(C))
    r = np.arange(C)[:, None]
    t = np.arange(C)[None, :]
    mats = []
    for l in range(levels):
        blk = C >> l
        half = blk // 2
        m = (r // blk) * blk + half - 1
        qside = (r % blk) >= half
        mats.append(np.where(qside, (t > m) & (t <= r), (t > r) & (t <= m)))
    mats.append(t <= r)
    mats.append(t > r)
    return np.concatenate(mats, axis=0).astype(np.float32)


def _gla_kernel(m_ref, q_ref, k_ref, v_ref, g_ref, la_ref, gn_ref, o_ref, state_ref):
    c = pl.program_id(0)
    C = GLA_CHUNK
    levels = GLA_LEVELS
    nsub = q_ref.shape[0] // C

    @pl.when(c == 0)
    def _():
        state_ref[...] = jnp.zeros_like(state_ref)

    m = m_ref[...]
    ri = lax.broadcasted_iota(jnp.int32, (C, C), 0)
    ci = lax.broadcasted_iota(jnp.int32, (C, C), 1)
    xor = jnp.where(ri > ci, ri ^ ci, 0)
    row = lax.broadcasted_iota(jnp.int32, (C, 1), 0)

    heads = range(GLA_HEADS)
    ks = [slice(h * GLA_DK, (h + 1) * GLA_DK) for h in heads]
    vs = [slice(h * GLA_DV, (h + 1) * GLA_DV) for h in heads]
    rs = [slice(u * C, (u + 1) * C) for u in range(nsub)]
    items = [(u, h) for u in range(nsub) for h in heads]

    expo, q, k, v, scores = {}, {}, {}, {}, {}
    for u, h in items:
        la = la_ref[rs[u], ks[h]]
        la_hi = la.astype(BF16)
        la_lo = (la - la_hi.astype(F32)).astype(BF16)
        expo[u, h] = jnp.exp(_dot(m, la_hi) + _dot(m, la_lo))
    for u, h in items:
        q[u, h] = q_ref[rs[u], ks[h]] * (GLA_DK ** -0.5)
        k[u, h] = k_ref[rs[u], ks[h]]
        v[u, h] = v_ref[rs[u], vs[h]].astype(BF16)
        scores[u, h] = jnp.where(ri == ci, _dot_nt(q[u, h].astype(BF16), k[u, h].astype(BF16)), 0.0)
    for l in range(levels):
        half = C >> (l + 1)
        shift = int(math.log2(half))
        qside = (row & half) != 0
        keep = (xor >> shift) == 1
        for it in items:
            x = (jnp.where(qside, q[it], k[it]) * expo[it][l * C:(l + 1) * C]).astype(BF16)
            scores[it] = scores[it] + jnp.where(keep, _dot_nt(x, x), 0.0)

    state = [state_ref[h] for h in heads]
    outs = {}
    for u, h in items:
        e_b = expo[u, h][levels * C:(levels + 1) * C]
        e_rev = expo[u, h][(levels + 1) * C:(levels + 2) * C]
        e_last = e_b[C - 1:C, :]
        outs[u, h] = (_dot(scores[u, h].astype(BF16), v[u, h])
                      + _dot_nt((q[u, h] * e_b).astype(BF16), state[h].astype(BF16)))
        state[h] = state[h] * e_last + _dot_tn(v[u, h], (k[u, h] * e_rev).astype(BF16))
    for h in heads:
        state_ref[h] = state[h]

    for u, h in items:
        o = outs[u, h]
        o = o * lax.rsqrt(jnp.mean(o * o, axis=-1, keepdims=True) + LN_EPS) * gn_ref[...]
        o_ref[rs[u], vs[h]] = (_silu(g_ref[rs[u], vs[h]]) * o).astype(o_ref.dtype)


def _gla(h_main, log_a, g_norm):
    s = h_main.shape[0]
    C = GLA_CHUNK
    m = jnp.asarray(_gla_decay_matrix(C), dtype=BF16)
    nrow = m.shape[0]
    rows = min(s, GLA_STEP_CHUNKS * C)
    wk = GLA_HEADS * GLA_DK
    wv = GLA_HEADS * GLA_DV
    return pl.pallas_call(
        _gla_kernel,
        out_shape=jax.ShapeDtypeStruct((s, wv), BF16),
        grid=(s // rows,),
        in_specs=[pl.BlockSpec((nrow, C), lambda c: (0, 0)),
                  pl.BlockSpec((rows, wk), lambda c: (c, GQ_OFF // wk)),
                  pl.BlockSpec((rows, wk), lambda c: (c, GK_OFF // wk)),
                  pl.BlockSpec((rows, wv), lambda c: (c, GV_OFF // wv)),
                  pl.BlockSpec((rows, wv), lambda c: (c, GG_OFF // wv)),
                  pl.BlockSpec((rows, wk), lambda c: (c, 0)),
                  pl.BlockSpec((1, GLA_DV), lambda c: (0, 0))],
        out_specs=pl.BlockSpec((rows, wv), lambda c: (c, 0)),
        scratch_shapes=[pltpu.VMEM((GLA_HEADS, GLA_DV, GLA_DK), F32)],
        compiler_params=_cparams(("arbitrary",)),
        name="gla",
    )(m, h_main, h_main, h_main, h_main, log_a, g_norm)


def _mixout_kernel(ret_ref, gla_ref, x_ref, w_ref, g_ref, b_ref, o_ref, wb_ref):
    nr = ret_ref.shape[1]

    @pl.when(pl.program_id(0) == 0)
    def _():
        wb_ref[...] = w_ref[...].astype(BF16)

    tm = x_ref.shape[0]
    parts = 2 if tm % 512 == 0 else 1
    rows = [slice(p * (tm // parts), (p + 1) * (tm // parts)) for p in range(parts)]
    mix = [_dot(ret_ref[r, :], wb_ref[:nr, :]) + _dot(gla_ref[r, :], wb_ref[nr:, :]) for r in rows]
    for r, m in zip(rows, mix):
        o_ref[r, :] = _layer_norm(DEEPNORM_ALPHA * x_ref[r, :] + m, g_ref[...], b_ref[...])


def _mixout_ln(ret, gla, x2d, w_b, g, b):
    s = x2d.shape[0]
    tm = min(s, 512)
    nr, ng = ret.shape[1], gla.shape[1]
    return pl.pallas_call(
        _mixout_kernel,
        out_shape=jax.ShapeDtypeStruct((s, D_MODEL), F32),
        grid=(s // tm,),
        in_specs=[pl.BlockSpec((tm, nr), lambda i: (i, 0)),
                  pl.BlockSpec((tm, ng), lambda i: (i, 0)),
                  pl.BlockSpec((tm, D_MODEL), lambda i: (i, 0)),
                  pl.BlockSpec((nr + ng, D_MODEL), lambda i: (0, 0), pipeline_mode=pl.Buffered(1)),
                  pl.BlockSpec((1, D_MODEL), lambda i: (0, 0)),
                  pl.BlockSpec((1, D_MODEL), lambda i: (0, 0))],
        out_specs=pl.BlockSpec((tm, D_MODEL), lambda i: (i, 0)),
        scratch_shapes=[pltpu.VMEM((nr + ng, D_MODEL), BF16)],
        compiler_params=_cparams(("arbitrary",)),
        name="mixout_ln1",
    )(ret, gla, x2d, w_b, g, b)


def _kv_kernel(mem_ref, wk_ref, wv_ref, k_ref, v_ref):
    m = mem_ref[...].astype(BF16)
    k_ref[...] = _dot(m, wk_ref[...].astype(BF16)).astype(k_ref.dtype)
    v_ref[...] = _dot(m, wv_ref[...].astype(BF16)).astype(v_ref.dtype)


def _mem_kv(mem_b, wk_b, wv_b):
    tn = 512
    return pl.pallas_call(
        _kv_kernel,
        out_shape=(jax.ShapeDtypeStruct((MEM_LEN, D_MODEL), BF16),
                   jax.ShapeDtypeStruct((MEM_LEN, D_MODEL), BF16)),
        grid=(D_MODEL // tn,),
        in_specs=[pl.BlockSpec((MEM_LEN, D_MODEL), lambda j: (0, 0)),
                  pl.BlockSpec((D_MODEL, tn), lambda j: (0, j)),
                  pl.BlockSpec((D_MODEL, tn), lambda j: (0, j))],
        out_specs=(pl.BlockSpec((MEM_LEN, tn), lambda j: (0, j)),
                   pl.BlockSpec((MEM_LEN, tn), lambda j: (0, j))),
        compiler_params=_cparams(("arbitrary",)),
        name="mem_kv",
    )(mem_b, wk_b, wv_b)


def _cross_kernel(h_ref, wq_ref, k_ref, v_ref, wo_ref, g_ref, b_ref, wr_ref, br_ref,
                  h2_ref, lg_ref):
    tm = h_ref.shape[0]
    parts = 2 if tm % 512 == 0 else 1
    pr = tm // parts
    rows = [slice(p * pr, (p + 1) * pr) for p in range(parts)]
    sls = [slice(hd * MEM_HEAD_DIM, (hd + 1) * MEM_HEAD_DIM) for hd in range(MEM_HEADS)]
    h1 = [h_ref[r, :] for r in rows]
    q = [_dot(h.astype(BF16), wq_ref[...]).astype(BF16) for h in h1]
    o = []
    for p in range(parts):
        outs = []
        for sl in sls:
            s = _dot_nt(q[p][:, sl], k_ref[:, sl]) * (MEM_HEAD_DIM ** -0.5)
            s = s - jnp.max(s, axis=-1, keepdims=True)
            e = jnp.exp(s)
            e = e / jnp.sum(e, axis=-1, keepdims=True)
            outs.append(_dot(e.astype(BF16), v_ref[:, sl]))
        o.append(jnp.concatenate(outs, axis=-1).astype(BF16))
    cross = [_dot(x, wo_ref[...]) for x in o]
    for p in range(parts):
        h2 = _layer_norm(DEEPNORM_ALPHA * h1[p] + cross[p], g_ref[...], b_ref[...])
        h2_ref[rows[p], :] = h2
        lg_ref[rows[p], :] = _dot(h2.astype(BF16), wr_ref[...]) + br_ref[...]


def _cross_attention(h1, wq_b, k, v, wo_b, g, b, w_route, b_route):
    s = h1.shape[0]
    tm = min(s, 512)
    const = lambda shape: pl.BlockSpec(shape, lambda i: (0, 0), pipeline_mode=pl.Buffered(1))
    return pl.pallas_call(
        _cross_kernel,
        out_shape=(jax.ShapeDtypeStruct((s, D_MODEL), F32),
                   jax.ShapeDtypeStruct((s, LANES), F32)),
        grid=(s // tm,),
        in_specs=[pl.BlockSpec((tm, D_MODEL), lambda i: (i, 0)),
                  const((D_MODEL, D_MODEL)),
                  const((MEM_LEN, D_MODEL)),
                  const((MEM_LEN, D_MODEL)),
                  const((D_MODEL, D_MODEL)),
                  const((1, D_MODEL)),
                  const((1, D_MODEL)),
                  const((D_MODEL, LANES)),
                  const((1, LANES))],
        out_specs=(pl.BlockSpec((tm, D_MODEL), lambda i: (i, 0)),
                   pl.BlockSpec((tm, LANES), lambda i: (i, 0))),
        compiler_params=_cparams(("arbitrary",)),
        name="cross_attn_ln2",
    )(h1, wq_b, k, v, wo_b, g, b, w_route, b_route)


def _route_kernel(lg_ref, slot_ref, slot_t_ref, gate_ref, runs_ref, plan_ref, tot_row, tot_col, gstart, rinfo):
    phase = pl.program_id(0)
    i = pl.program_id(1)
    tm = lg_ref.shape[0]
    lane = lax.broadcasted_iota(jnp.int32, (tm, LANES), 1)
    rows = pl.ds(pl.multiple_of(i * tm, tm), tm)

    @pl.when(phase == 0)
    def _():
        neg = -jnp.inf
        logits = lg_ref[...]
        gmask = lane < N_GROUPS
        gl = jnp.where(gmask, logits, neg)
        ge = jnp.exp(gl - jnp.max(gl, axis=-1, keepdims=True))
        pg = ge / jnp.sum(ge, axis=-1, keepdims=True)
        pg_sel = jnp.max(pg, axis=-1, keepdims=True)
        grp = jnp.min(jnp.where((pg == pg_sel) & gmask, lane, LANES), axis=-1, keepdims=True)

        fl_lane = lane - N_GROUPS
        fmask = (fl_lane >= 0) & (fl_lane < N_EXPERTS) & ((fl_lane >> 3) == grp)
        fl = jnp.where(fmask, logits, neg)
        fe = jnp.exp(fl - jnp.max(fl, axis=-1, keepdims=True))
        fp = fe / jnp.sum(fe, axis=-1, keepdims=True)
        p1 = jnp.max(fp, axis=-1, keepdims=True)
        i1 = jnp.min(jnp.where((fp == p1) & fmask, lane, LANES), axis=-1, keepdims=True)
        rest = fmask & (lane != i1)
        fp2 = jnp.where(rest, fp, -1.0)
        p2 = jnp.max(fp2, axis=-1, keepdims=True)
        i2 = jnp.min(jnp.where((fp2 == p2) & rest, lane, LANES), axis=-1, keepdims=True)
        psum = p1 + p2
        picks = ((i1 - N_GROUPS).astype(F32), (i2 - N_GROUPS).astype(F32),
                 pg_sel * p1 / psum, pg_sel * p2 / psum)
        info = jnp.zeros((tm, LANES), F32)
        for j, val in enumerate(picks):
            info = jnp.where(lane == j, val, info)
        rinfo[rows, :] = info

    info = rinfo[rows, :]
    oh1 = lane == info[:, 0:1].astype(jnp.int32)
    oh2 = lane == info[:, 1:2].astype(jnp.int32)
    gate1 = info[:, 2:3]
    gate2 = info[:, 3:4]
    oh = (jnp.where(oh1, 1.0, 0.0) + jnp.where(oh2, 1.0, 0.0)).astype(BF16)
    ones = jnp.ones((tm, LANES), BF16)

    def align_up(v, a):
        return jnp.floor((v + (a - 1.0)) * (1.0 / a)) * a

    run_row = align_up(_dot_tn(ones, oh)[0:8, :], RUN_ALIGN)
    rr = lax.broadcasted_iota(jnp.int32, (LANES, LANES), 0)
    cc = lax.broadcasted_iota(jnp.int32, (LANES, LANES), 1)
    strict_upper = jnp.where(rr < cc, 1.0, 0.0).astype(BF16)

    @pl.when((phase == 0) & (i == 0))
    def _():
        tot_row[...] = jnp.zeros_like(tot_row)
        tot_col[...] = jnp.zeros_like(tot_col)

    @pl.when(phase == 0)
    def _():
        tot_row[...] += run_row
        tot_col[...] += align_up(_dot_tn(oh, ones), RUN_ALIGN)

    @pl.when((phase == 1) & (i == 0))
    def _():
        nblk_row = align_up(tot_row[...], MOE_BLK) * (1.0 / MOE_BLK)
        nblk_col = align_up(tot_col[...], MOE_BLK) * (1.0 / MOE_BLK)
        lower_incl = jnp.where(cc <= rr, 1.0, 0.0).astype(BF16)
        base = _dot(nblk_row.astype(BF16), strict_upper) * float(MOE_BLK)
        ends = _dot(lower_incl, nblk_col.astype(BF16))
        expert_rows = rr < N_EXPERTS
        be = jnp.sum(jnp.where(expert_rows & (ends <= cc.astype(F32)), 1.0, 0.0), axis=0, keepdims=True)
        be = jnp.minimum(be, N_EXPERTS - 1.0)
        total = jnp.sum(jnp.where(expert_rows, nblk_col, 0.0), axis=0, keepdims=True)
        sub = lax.broadcasted_iota(jnp.int32, plan_ref.shape, 0)
        plan = jnp.where(sub == 0, jnp.broadcast_to(be, plan_ref.shape),
                         jnp.where(sub == 1, jnp.broadcast_to(total, plan_ref.shape),
                                   jnp.where(sub == 2, base + tot_row[...],
                                             jnp.where(sub == 3, nblk_row,
                                                       nblk_row * float(MOE_BLK) - tot_row[...]))))
        plan_ref[...] = plan.astype(jnp.int32)
        gstart[...] = base

    @pl.when(phase == 1)
    def _():
        off = _dot((run_row * (1.0 / RUN_ALIGN)).astype(BF16), strict_upper) * float(RUN_ALIGN)
        tr = lax.broadcasted_iota(jnp.int32, (tm, tm), 0)
        tc = lax.broadcasted_iota(jnp.int32, (tm, tm), 1)
        strict_lower = jnp.where(tc < tr, 1.0, 0.0).astype(BF16)
        slot = _dot(strict_lower, oh) + off[0:1, :]
        s1 = jnp.sum(jnp.where(oh1, slot, 0.0), axis=-1, keepdims=True)
        s2 = jnp.sum(jnp.where(oh2, slot, 0.0), axis=-1, keepdims=True)
        slots = jnp.where(lane == 0, s1, jnp.where(lane == 1, s2, 0.0))
        slot_ref[...] = slots.astype(jnp.int32)
        slot_t_ref[...] = jnp.transpose(slots)[0:8, :].astype(jnp.int32)
        gate_ref[...] = jnp.where(lane == 0, gate1, jnp.where(lane == 1, gate2, 0.0))
        sub = lax.broadcasted_iota(jnp.int32, runs_ref.shape, 0)
        runs = jnp.where(sub == 0, gstart[...], jnp.where(sub == 1, off, run_row))
        runs_ref[...] = runs.astype(jnp.int32)
        gstart[...] += run_row


def _route(logits):
    t = logits.shape[0]
    tm = min(t, TILE_TOK)
    nt = t // tm
    step = lambda p, i: (i * p, 0)
    return pl.pallas_call(
        _route_kernel,
        out_shape=(jax.ShapeDtypeStruct((t, LANES), jnp.int32),
                   jax.ShapeDtypeStruct((nt * 8, tm), jnp.int32),
                   jax.ShapeDtypeStruct((t, LANES), F32),
                   jax.ShapeDtypeStruct((nt * 8, LANES), jnp.int32),
                   jax.ShapeDtypeStruct((8, LANES), jnp.int32)),
        grid=(2, nt),
        in_specs=[pl.BlockSpec((tm, LANES), lambda p, i: (i, 0))],
        out_specs=(pl.BlockSpec((tm, LANES), step),
                   pl.BlockSpec((8, tm), step),
                   pl.BlockSpec((tm, LANES), step),
                   pl.BlockSpec((8, LANES), step),
                   pl.BlockSpec((8, LANES), lambda p, i: (0, 0))),
        scratch_shapes=[pltpu.VMEM((8, LANES), F32),
                        pltpu.VMEM((LANES, LANES), F32),
                        pltpu.VMEM((8, LANES), F32),
                        pltpu.VMEM((t, LANES), F32)],
        compiler_params=_cparams(("arbitrary", "arbitrary")),
        name="moe_route",
    )(logits)


def _run_copies(rg_ref, ro_ref, rn_ref, tile, make_copy):
    def each(action):
        def body(e, carry):
            j = tile * N_EXPERTS + e
            n = rn_ref[j]

            @pl.when(n > 0)
            def _():
                action(make_copy(pl.multiple_of(rg_ref[j], RUN_ALIGN), pl.multiple_of(ro_ref[j], RUN_ALIGN),
                                 pl.multiple_of(n, RUN_ALIGN)))
            return carry
        lax.fori_loop(0, N_EXPERTS, body, 0)
    return each


def _dispatch_kernel(rg_ref, ro_ref, rn_ref, zrow_ref, zlen_ref, nu_ref, x_ref, slot_t_ref, xb_ref,
                     sorted_ref, zero_ref, sems, zsem):
    i = pl.program_id(0)
    n = pl.num_programs(0)
    tm = x_ref.shape[0]
    nb = xb_ref.shape[0] // MOE_BLK
    buf = i % 2

    def zero_copy(row, size):
        return pltpu.make_async_copy(
            zero_ref.at[pl.ds(0, size), :], xb_ref.at[pl.ds(pl.multiple_of(row, RUN_ALIGN), size), :], zsem)

    def pad_copy(e):
        return zero_copy(zrow_ref[e], pl.multiple_of(zlen_ref[e], RUN_ALIGN))

    def runs_of(tile, slot):
        def make_copy(g, o, size):
            return pltpu.make_async_copy(sorted_ref.at[slot, pl.ds(o, size), :],
                                         xb_ref.at[pl.ds(g, size), :], sems.at[slot])
        return _run_copies(rg_ref, ro_ref, rn_ref, tile, make_copy)

    @pl.when(i == 0)
    def _():
        zero_ref[...] = jnp.zeros_like(zero_ref)

        def issue_zero(e, carry):
            @pl.when(zlen_ref[e] > 0)
            def _():
                pad_copy(e).start()
            return carry
        lax.fori_loop(0, N_EXPERTS, issue_zero, 0)
        lax.fori_loop(nu_ref[0], nb, lambda b, c: (zero_copy(b * MOE_BLK, MOE_BLK).start(), c)[1], 0)

        def wait_zero(e, carry):
            @pl.when(zlen_ref[e] > 0)
            def _():
                pad_copy(e).wait()
            return carry
        lax.fori_loop(0, N_EXPERTS, wait_zero, 0)
        lax.fori_loop(nu_ref[0], nb, lambda b, c: (zero_copy(0, MOE_BLK).wait(), c)[1], 0)

    @pl.when(i >= 2)
    def _():
        runs_of(i - 2, buf)(lambda c: c.wait())

    x = x_ref[...].astype(BF16)
    s1 = slot_t_ref[0:1, :]
    s2 = slot_t_ref[1:2, :]
    for rc in range(TILE_SLOTS // SORT_ROWS):
        r = lax.broadcasted_iota(jnp.int32, (SORT_ROWS, tm), 0) + rc * SORT_ROWS
        perm = jnp.where((r == s1) | (r == s2), 1.0, 0.0).astype(BF16)
        sorted_ref[buf, rc * SORT_ROWS:(rc + 1) * SORT_ROWS, :] = _dot(perm, x).astype(BF16)

    runs_of(i, buf)(lambda c: c.start())

    @pl.when(i == n - 1)
    def _():
        @pl.when(i >= 1)
        def _():
            runs_of(i - 1, 1 - buf)(lambda c: c.wait())
        runs_of(i, buf)(lambda c: c.wait())


def _dispatch(rg, ro, rn, zrow, zlen, n_used, h2, slot_t, n_rows):
    t = h2.shape[0]
    tm = min(t, TILE_TOK)
    grid_spec = pltpu.PrefetchScalarGridSpec(
        num_scalar_prefetch=6,
        grid=(t // tm,),
        in_specs=[pl.BlockSpec((tm, D_MODEL), lambda i, *_: (i, 0)),
                  pl.BlockSpec((8, tm), lambda i, *_: (i, 0))],
        out_specs=pl.BlockSpec(memory_space=pl.ANY),
        scratch_shapes=[pltpu.VMEM((2, TILE_SLOTS, D_MODEL), BF16),
                        pltpu.VMEM((MOE_BLK, D_MODEL), BF16),
                        pltpu.SemaphoreType.DMA((2,)),
                        pltpu.SemaphoreType.DMA(())],
    )
    return pl.pallas_call(
        _dispatch_kernel,
        out_shape=jax.ShapeDtypeStruct((n_rows, D_MODEL), BF16),
        grid_spec=grid_spec,
        compiler_params=_cparams(("arbitrary",)),
        name="moe_dispatch",
    )(rg, ro, rn, zrow, zlen, n_used, h2, slot_t)


def _expert_kernel(be_ref, nu_ref, nblk_ref, x_ref, wg_ref, wu_ref, wd_ref, y_ref,
                   wg_f, wu_f, wd_f, wg_b, wu_b, wd_b, sems, cur_ref):
    step = pl.program_id(0)
    n_used = nu_ref[0]

    def weight_copies(expert, slot):
        half = EXPERT_FF // 2
        return [pltpu.make_async_copy(wg_ref.at[expert], wg_f.at[slot], sems.at[slot]),
                pltpu.make_async_copy(wu_ref.at[expert], wu_f.at[slot], sems.at[slot]),
                pltpu.make_async_copy(wd_ref.at[expert, :half], wd_f.at[slot, :half], sems.at[slot]),
                pltpu.make_async_copy(wd_ref.at[expert, half:], wd_f.at[slot, half:], sems.at[slot])]

    def start_weights(expert, slot):
        for c in weight_copies(expert, slot):
            c.start(priority=1)

    def after(blk):
        return blk + nblk_ref[be_ref[blk]]

    @pl.when(step == 0)
    def _():
        cur_ref[0] = 0
        start_weights(be_ref[0], 0)

        @pl.when(after(0) < n_used)
        def _():
            start_weights(be_ref[after(0)], 1)

    for sub in range(EXPERT_STEP_BLKS):
        b = step * EXPERT_STEP_BLKS + sub
        used = b < n_used
        e = be_ref[b]
        prev = be_ref[jnp.maximum(b - 1, 0)]
        fresh = used & ((b == 0) | (e != prev))
        rows = slice(sub * MOE_BLK, (sub + 1) * MOE_BLK)

        @pl.when(fresh)
        def _():
            slot = cur_ref[0]
            for c in weight_copies(e, slot):
                c.wait()
            wg_b[...] = wg_f[slot].astype(BF16)
            wu_b[...] = wu_f[slot].astype(BF16)
            wd_b[...] = wd_f[slot].astype(BF16)
            nxt = after(b)

            @pl.when(nxt < n_used)
            def _():
                nxt2 = after(nxt)

                @pl.when(nxt2 < n_used)
                def _():
                    start_weights(be_ref[nxt2], slot)
            cur_ref[0] = 1 - slot

        @pl.when(used)
        def _():
            x = x_ref[rows, :]
            hid = _silu(_dot(x, wg_b[...])) * _dot(x, wu_b[...])
            y_ref[rows, :] = _dot(hid.astype(BF16), wd_b[...]).astype(y_ref.dtype)

        @pl.when(jnp.logical_not(used))
        def _():
            y_ref[rows, :] = jnp.zeros((MOE_BLK, D_MODEL), y_ref.dtype)


def _experts(block_e, n_used, nblk, xb, w_gate, w_up, w_down):
    step_rows = EXPERT_STEP_BLKS * MOE_BLK
    any_space = pl.BlockSpec(memory_space=pl.ANY)
    last_step = lambda nu: (nu[0] - 1) // EXPERT_STEP_BLKS
    grid_spec = pltpu.PrefetchScalarGridSpec(
        num_scalar_prefetch=3,
        grid=(xb.shape[0] // step_rows,),
        in_specs=[pl.BlockSpec((step_rows, D_MODEL), lambda s, be, nu, nk: (jnp.minimum(s, last_step(nu)), 0)),
                  any_space, any_space, any_space],
        out_specs=pl.BlockSpec((step_rows, D_MODEL), lambda s, be, nu, nk: (s, 0)),
        scratch_shapes=[pltpu.VMEM((2, D_MODEL, EXPERT_FF), F32),
                        pltpu.VMEM((2, D_MODEL, EXPERT_FF), F32),
                        pltpu.VMEM((2, EXPERT_FF, D_MODEL), F32),
                        pltpu.VMEM((D_MODEL, EXPERT_FF), BF16),
                        pltpu.VMEM((D_MODEL, EXPERT_FF), BF16),
                        pltpu.VMEM((EXPERT_FF, D_MODEL), BF16),
                        pltpu.SemaphoreType.DMA((2,)),
                        pltpu.SMEM((1,), jnp.int32)],
    )
    return pl.pallas_call(
        _expert_kernel,
        out_shape=jax.ShapeDtypeStruct(xb.shape, BF16),
        grid_spec=grid_spec,
        compiler_params=_cparams(("arbitrary",)),
        name="moe_experts",
    )(block_e, n_used, nblk, xb, w_gate, w_up, w_down)


def _combine_kernel(rg_ref, ro_ref, rn_ref, yb_ref, h2_ref, slot_ref, gate_ref, g_ref, b_ref, o_ref,
                    ybuf, sems):
    i = pl.program_id(0)
    n = pl.num_programs(0)
    tm = h2_ref.shape[0]
    buf = i % 2

    def runs_of(tile, slot):
        def make_copy(g, o, size):
            return pltpu.make_async_copy(yb_ref.at[pl.ds(g, size), :],
                                         ybuf.at[slot, pl.ds(o, size), :], sems.at[slot])
        return _run_copies(rg_ref, ro_ref, rn_ref, tile, make_copy)

    @pl.when(i == 0)
    def _():
        ybuf[...] = jnp.zeros_like(ybuf)
        runs_of(0, 0)(lambda c: c.start())

    @pl.when(i + 1 < n)
    def _():
        runs_of(i + 1, 1 - buf)(lambda c: c.start())

    runs_of(i, buf)(lambda c: c.wait())

    lane = lax.broadcasted_iota(jnp.int32, (tm, TILE_SLOTS), 1)
    slots = slot_ref[...]
    gate = gate_ref[...]
    y = ybuf[buf]
    sel = jnp.where(lane == slots[:, 0:1], gate[:, 0:1],
                    jnp.where(lane == slots[:, 1:2], gate[:, 1:2], 0.0)).astype(BF16)
    parts = 2 if tm % 512 == 0 else 1
    rows = [slice(p * (tm // parts), (p + 1) * (tm // parts)) for p in range(parts)]
    ffn = [_dot(sel[r, :], y) for r in rows]
    for r, f in zip(rows, ffn):
        o_ref[r, :] = _layer_norm(DEEPNORM_ALPHA * h2_ref[r, :] + f, g_ref[...], b_ref[...])


def _combine_ln(rg, ro, rn, yb, h2, slots, gate, g, b):
    t = h2.shape[0]
    tm = min(t, TILE_TOK)
    grid_spec = pltpu.PrefetchScalarGridSpec(
        num_scalar_prefetch=3,
        grid=(t // tm,),
        in_specs=[pl.BlockSpec(memory_space=pl.ANY),
                  pl.BlockSpec((tm, D_MODEL), lambda i, *_: (i, 0)),
                  pl.BlockSpec((tm, LANES), lambda i, *_: (i, 0)),
                  pl.BlockSpec((tm, LANES), lambda i, *_: (i, 0)),
                  pl.BlockSpec((1, D_MODEL), lambda i, *_: (0, 0)),
                  pl.BlockSpec((1, D_MODEL), lambda i, *_: (0, 0))],
        out_specs=pl.BlockSpec((tm, D_MODEL), lambda i, *_: (i, 0)),
        scratch_shapes=[pltpu.VMEM((2, TILE_SLOTS, D_MODEL), BF16),
                        pltpu.SemaphoreType.DMA((2,))],
    )
    return pl.pallas_call(
        _combine_kernel,
        out_shape=jax.ShapeDtypeStruct((t, D_MODEL), F32),
        grid_spec=grid_spec,
        compiler_params=_cparams(("arbitrary",)),
        name="moe_combine_ln3",
    )(rg, ro, rn, yb, h2, slots, gate, g, b)


def _mixer(x2d, positions, w_in, w_gla_a2, b_gla_a, g_gla_norm):
    s = x2d.shape[0]
    half = RET_DK // 2
    inv_freq = (ROPE_BASE ** (-jnp.arange(half, dtype=F32) / half)).reshape(1, half)
    cos, sin = _rope_table(positions.reshape(s, 1), inv_freq)
    w_in_t = jnp.swapaxes(w_in, 0, 1)
    w_lr_t = jnp.pad(w_in_t[GLR_OFF:].astype(BF16), ((0, LANES - GLA_LOWRANK), (0, 0)))
    w_a2 = jnp.pad(w_gla_a2.astype(BF16), ((0, LANES - GLA_LOWRANK), (0, 0)))
    h_main, log_a = _proj_in(x2d, w_in_t, w_lr_t, w_a2, b_gla_a.reshape(1, -1))
    log_gamma = jnp.log1p(-jnp.exp2(-5.0 - jnp.arange(RET_HEADS, dtype=F32)))
    ret = _retention(h_main, cos, sin, log_gamma)
    gla = _gla(h_main, log_a, g_gla_norm.reshape(1, -1))
    return ret, gla


def _moe(h2, logits, w_gate, w_up, w_down, g, b):
    t = h2.shape[0]
    nt = t // min(t, TILE_TOK)
    slots, slot_t, gate, runs, plan = _route(logits)
    runs = runs.reshape(nt, 8, LANES)[:, :3, :N_EXPERTS]
    rg, ro, rn = (runs[:, j, :].reshape(-1) for j in range(3))
    max_rows = 2 * t + nt * N_EXPERTS * (RUN_ALIGN - 1) + N_EXPERTS * (MOE_BLK - 1)
    nb = -(-max_rows // (MOE_BLK * EXPERT_STEP_BLKS)) * EXPERT_STEP_BLKS
    block_e, n_used = plan[0, :nb], plan[1, :1]
    pad_row, nblk, pad_len = (plan[j, :N_EXPERTS] for j in (2, 3, 4))
    xb = _dispatch(rg, ro, rn, pad_row, pad_len, n_used, h2, slot_t, nb * MOE_BLK)
    yb = _experts(block_e, n_used, nblk, xb, w_gate, w_up, w_down)
    return _combine_ln(rg, ro, rn, yb, h2, slots, gate, g, b)


def kernel(x, mem, positions, w_in, w_gla_a2, b_gla_a, g_gla_norm, w_mix_out, ln1_g, ln1_b, w_mq, w_mk, w_mv, w_mo, ln2_g, ln2_b, w_route_group, b_route_group, w_route_expert, b_route_expert, w_exp_gate, w_exp_up, w_exp_down, ln3_g, ln3_b):
    bsz, s, d = x.shape
    assert bsz == 1 and d == D_MODEL
    x2d = x.reshape(s, d)
    row = lambda v: v.reshape(1, -1)

    ret, gla = _mixer(x2d, positions, w_in[0], w_gla_a2[0], b_gla_a[0], g_gla_norm[0])
    h1 = _mixout_ln(ret, gla, x2d, w_mix_out[0], row(ln1_g[0]), row(ln1_b[0]))

    k, v = _mem_kv(mem[0], w_mk[0], w_mv[0])
    n_route = N_GROUPS + N_EXPERTS
    w_route = jnp.pad(jnp.concatenate([w_route_group[0], w_route_expert[0]], axis=1).astype(BF16),
                      ((0, 0), (0, LANES - n_route)))
    b_route = jnp.pad(jnp.concatenate([b_route_group[0], b_route_expert[0].reshape(-1)]),
                      (0, LANES - n_route)).reshape(1, LANES)
    h2, logits = _cross_attention(h1, w_mq[0].astype(BF16), k, v, w_mo[0].astype(BF16),
                                  row(ln2_g[0]), row(ln2_b[0]), w_route, b_route)

    out = _moe(h2, logits, w_exp_gate[0], w_exp_up[0], w_exp_down[0],
               row(ln3_g[0]), row(ln3_b[0]))
    return out.reshape(bsz, s, d)
```

```python
import math

import jax
import jax.numpy as jnp
from jax import lax
from jax.experimental import pallas as pl
from jax.experimental.pallas import tpu as pltpu

F32 = jnp.float32
BF16 = jnp.bfloat16

D_MODEL = 2048
MEM_LEN = 256
RET_HEADS = 4
RET_DK = 256
RET_DV = 256
GLA_HEADS = 4
GLA_DK = 128
GLA_DV = 256
GLA_LOWRANK = 16
GLA_TAU = 16.0
ROPE_BASE = 10000.0
MEM_HEADS = 4
MEM_HEAD_DIM = D_MODEL // MEM_HEADS
N_GROUPS = 4
EXPERTS_PER_GROUP = 8
N_EXPERTS = N_GROUPS * EXPERTS_PER_GROUP
EXPERT_FF = 512
LN_EPS = 1e-5
DEPTH = 1
DEEPNORM_ALPHA = (2 * DEPTH) ** 0.25

RQ_OFF, RK_OFF, RV_OFF, RG_OFF = 0, 1024, 2048, 3072
GQ_OFF, GK_OFF, GV_OFF, GG_OFF, GLR_OFF = 4096, 4608, 5120, 6144, 7168
IN_MAIN = 7168

LANES = 128
RET_CHUNK = 256
GLA_CHUNK = 128
GLA_LEVELS = 7
GLA_STEP_CHUNKS = 4
MOE_BLK = 256
EXPERT_STEP_BLKS = 4
TILE_TOK = 512
RUN_ALIGN = 16
TILE_SLOTS = 2 * TILE_TOK + N_EXPERTS * RUN_ALIGN
SORT_ROWS = 256
PROJ_K_CHUNK = 512
VMEM_LIMIT = 56 * 1024 * 1024


def _cparams(sem):
    return pltpu.CompilerParams(dimension_semantics=sem, vmem_limit_bytes=VMEM_LIMIT)


def _layer_norm(y, g, b):
    mu = jnp.mean(y, axis=-1, keepdims=True)
    d = y - mu
    var = jnp.mean(d * d, axis=-1, keepdims=True)
    return d * lax.rsqrt(var + LN_EPS) * g + b


def _silu(x):
    return x / (1.0 + jnp.exp(-x))


def _dot(a, b):
    return jnp.dot(a, b, preferred_element_type=F32)


def _dot_nt(a, b):
    return lax.dot_general(a, b, (((1,), (1,)), ((), ())), preferred_element_type=F32)


def _dot_tn(a, b):
    return lax.dot_general(a, b, (((0,), (0,)), ((), ())), preferred_element_type=F32)


def _rope_kernel(pos_ref, invf_ref, cos_ref, sin_ref):
    ang = pos_ref[...].astype(F32) * invf_ref[...]
    cos_ref[...] = jnp.cos(ang)
    sin_ref[...] = jnp.sin(ang)


def _rope_table(pos_col, inv_freq):
    s = pos_col.shape[0]
    tm = min(s, 1024)
    half = inv_freq.shape[1]
    return pl.pallas_call(
        _rope_kernel,
        out_shape=(jax.ShapeDtypeStruct((s, half), F32), jax.ShapeDtypeStruct((s, half), F32)),
        grid=(s // tm,),
        in_specs=[pl.BlockSpec((tm, 1), lambda i: (i, 0)),
                  pl.BlockSpec((1, half), lambda i: (0, 0))],
        out_specs=(pl.BlockSpec((tm, half), lambda i: (i, 0)),
                   pl.BlockSpec((tm, half), lambda i: (i, 0))),
        compiler_params=_cparams(("arbitrary",)),
        name="rope_table",
    )(pos_col, inv_freq)


def _proj_in_kernel(x_ref, wt_ref, wlr_ref, wa2_ref, ba_ref, o_ref, la_ref, xb_ref):
    @pl.when(pl.program_id(1) == 0)
    def _():
        xb_ref[...] = x_ref[...].astype(BF16)
        glr = _dot_nt(xb_ref[...], wlr_ref[...])
        z = _dot(glr.astype(BF16), wa2_ref[...]) + ba_ref[...]
        la_ref[...] = (jnp.minimum(z, 0.0) - jnp.log(1.0 + jnp.exp(-jnp.abs(z)))) / GLA_TAU

    acc = None
    for c in range(D_MODEL // PROJ_K_CHUNK):
        ck = slice(c * PROJ_K_CHUNK, (c + 1) * PROJ_K_CHUNK)
        part = _dot_nt(xb_ref[:, ck], wt_ref[:, ck].astype(BF16))
        acc = part if acc is None else acc + part
    o_ref[...] = acc.astype(o_ref.dtype)


def _proj_in(x2d, w_in_t, w_lr_t, w_a2, b_a):
    s = x2d.shape[0]
    tm = min(s, 1024)
    tn = 1024
    n = GLA_HEADS * GLA_DK
    return pl.pallas_call(
        _proj_in_kernel,
        out_shape=(jax.ShapeDtypeStruct((s, IN_MAIN), BF16), jax.ShapeDtypeStruct((s, n), F32)),
        grid=(s // tm, IN_MAIN // tn),
        in_specs=[pl.BlockSpec((tm, D_MODEL), lambda i, j: (i, 0)),
                  pl.BlockSpec((tn, D_MODEL), lambda i, j: (j, 0)),
                  pl.BlockSpec((LANES, D_MODEL), lambda i, j: (0, 0)),
                  pl.BlockSpec((LANES, n), lambda i, j: (0, 0)),
                  pl.BlockSpec((1, n), lambda i, j: (0, 0))],
        out_specs=(pl.BlockSpec((tm, tn), lambda i, j: (i, j)),
                   pl.BlockSpec((tm, n), lambda i, j: (i, 0))),
        scratch_shapes=[pltpu.VMEM((tm, D_MODEL), BF16)],
        compiler_params=_cparams(("arbitrary", "arbitrary")),
        name="proj_in",
    )(x2d, w_in_t, w_lr_t, w_a2, b_a)


def _rotary(t, cos, sin):
    half = t.shape[-1] // 2
    t1, t2 = t[:, :half], t[:, half:]
    return jnp.concatenate([t1 * cos - t2 * sin, t1 * sin + t2 * cos], axis=-1)


def _retention_kernel(lg_ref, q_ref, k_ref, v_ref, g_ref, cos_ref, sin_ref, o_ref, state_ref, intra_ref, dq_ref, dk_ref):
    c = pl.program_id(0)
    C = q_ref.shape[0]

    @pl.when(c == 0)
    def _():
        state_ref[...] = jnp.zeros_like(state_ref)
        ri = lax.broadcasted_iota(jnp.int32, (C, C), 0)
        ci = lax.broadcasted_iota(jnp.int32, (C, C), 1)
        rel = jnp.maximum(ri - ci, 0).astype(F32)
        n = lax.broadcasted_iota(jnp.int32, (C, RET_DK), 0).astype(F32)
        for h in range(RET_HEADS):
            lg = lg_ref[h]
            intra_ref[h] = jnp.where(ri >= ci, jnp.exp(lg * rel), 0.0)
            dq_ref[h] = jnp.exp(lg * (n + 1.0))
            dk_ref[h] = jnp.exp(lg * (C - 1.0 - n))

    cos = cos_ref[...]
    sin = sin_ref[...]

    for h in range(RET_HEADS):
        sl = slice(h * RET_DK, (h + 1) * RET_DK)
        q = _rotary(q_ref[:, sl].astype(F32), cos, sin)
        k = _rotary(k_ref[:, sl].astype(F32), cos, sin) * (RET_DK ** -0.5)
        v = v_ref[:, sl].astype(BF16)
        decay_q = dq_ref[h]
        decay_k = dk_ref[h]
        decay_chunk = jnp.exp(lg_ref[h] * C)

        state = state_ref[h]
        scores = _dot_nt(q.astype(BF16), k.astype(BF16)) * intra_ref[h]
        o = _dot(scores.astype(BF16), v) + _dot((q * decay_q).astype(BF16), state.astype(BF16))
        state_ref[h] = decay_chunk * state + _dot_tn((k * decay_k).astype(BF16), v)

        mu = jnp.mean(o, axis=-1, keepdims=True)
        d = o - mu
        var = jnp.mean(d * d, axis=-1, keepdims=True)
        o = d * lax.rsqrt(var + LN_EPS)
        o_ref[:, sl] = (_silu(g_ref[:, sl].astype(F32)) * o).astype(o_ref.dtype)


def _retention(h_main, cos, sin, log_gamma):
    s = h_main.shape[0]
    C = min(RET_CHUNK, s)
    w = RET_HEADS * RET_DK
    col = lambda off: (lambda c, lg: (c, off // w))
    grid_spec = pltpu.PrefetchScalarGridSpec(
        num_scalar_prefetch=1,
        grid=(s // C,),
        in_specs=[pl.BlockSpec((C, w), col(RQ_OFF)),
                  pl.BlockSpec((C, w), col(RK_OFF)),
                  pl.BlockSpec((C, w), col(RV_OFF)),
                  pl.BlockSpec((C, w), col(RG_OFF)),
                  pl.BlockSpec((C, RET_DK // 2), lambda c, lg: (c, 0)),
                  pl.BlockSpec((C, RET_DK // 2), lambda c, lg: (c, 0))],
        out_specs=pl.BlockSpec((C, RET_HEADS * RET_DV), lambda c, lg: (c, 0)),
        scratch_shapes=[pltpu.VMEM((RET_HEADS, RET_DK, RET_DV), F32),
                        pltpu.VMEM((RET_HEADS, C, C), F32),
                        pltpu.VMEM((RET_HEADS, C, RET_DK), F32),
                        pltpu.VMEM((RET_HEADS, C, RET_DK), F32)],
    )
    return pl.pallas_call(
        _retention_kernel,
        out_shape=jax.ShapeDtypeStruct((s, RET_HEADS * RET_DV), BF16),
        grid_spec=grid_spec,
        compiler_params=_cparams(("arbitrary",)),
        name="retention",
    )(log_gamma, h_main, h_main, h_main, h_main, cos, sin)


def _gla_decay_matrix(C):
    import numpy as np
    levels = int(math.log2(C))
    r = np.arange(C)[:, None]
    t = np.arange(C)[None, :]
    mats = []
    for l in range(levels):
        blk = C >> l
        half = blk // 2
        m = (r // blk) * blk + half - 1
        qside = (r % blk) >= half
        mats.append(np.where(qside, (t > m) & (t <= r), (t > r) & (t <= m)))
    mats.append(t <= r)
    mats.append(t > r)
    return np.concatenate(mats, axis=0).astype(np.float32)


def _gla_kernel(m_ref, q_ref, k_ref, v_ref, g_ref, la_ref, gn_ref, o_ref, state_ref):
    c = pl.program_id(0)
    C = GLA_CHUNK
    levels = GLA_LEVELS
    nsub = q_ref.shape[0] // C

    @pl.when(c == 0)
    def _():
        state_ref[...] = jnp.zeros_like(state_ref)

    m = m_ref[...]
    ri = lax.broadcasted_iota(jnp.int32, (C, C), 0)
    ci = lax.broadcasted_iota(jnp.int32, (C, C), 1)
    xor = jnp.where(ri > ci, ri ^ ci, 0)
    row = lax.broadcasted_iota(jnp.int32, (C, 1), 0)

    heads = range(GLA_HEADS)
    ks = [slice(h * GLA_DK, (h + 1) * GLA_DK) for h in heads]
    vs = [slice(h * GLA_DV, (h + 1) * GLA_DV) for h in heads]
    rs = [slice(u * C, (u + 1) * C) for u in range(nsub)]
    items = [(u, h) for u in range(nsub) for h in heads]

    expo, q, k, v, scores = {}, {}, {}, {}, {}
    for u, h in items:
        la = la_ref[rs[u], ks[h]]
        la_hi = la.astype(BF16)
        la_lo = (la - la_hi.astype(F32)).astype(BF16)
        expo[u, h] = jnp.exp(_dot(m, la_hi) + _dot(m, la_lo))
    for u, h in items:
        q[u, h] = q_ref[rs[u], ks[h]].astype(F32) * (GLA_DK ** -0.5)
        k[u, h] = k_ref[rs[u], ks[h]].astype(F32)
        v[u, h] = v_ref[rs[u], vs[h]].astype(BF16)
        scores[u, h] = jnp.where(ri == ci, _dot_nt(q[u, h].astype(BF16), k[u, h].astype(BF16)), 0.0)
    for l in range(levels):
        half = C >> (l + 1)
        shift = int(math.log2(half))
        qside = (row & half) != 0
        keep = (xor >> shift) == 1
        for it in items:
            x = (jnp.where(qside, q[it], k[it]) * expo[it][l * C:(l + 1) * C]).astype(BF16)
            scores[it] = scores[it] + jnp.where(keep, _dot_nt(x, x), 0.0)

    state = [state_ref[h] for h in heads]
    outs = {}
    for u, h in items:
        e_b = expo[u, h][levels * C:(levels + 1) * C]
        e_rev = expo[u, h][(levels + 1) * C:(levels + 2) * C]
        e_last = e_b[C - 1:C, :]
        outs[u, h] = (_dot(scores[u, h].astype(BF16), v[u, h])
                      + _dot_nt((q[u, h] * e_b).astype(BF16), state[h].astype(BF16)))
        state[h] = state[h] * e_last + _dot_tn(v[u, h], (k[u, h] * e_rev).astype(BF16))
    for h in heads:
        state_ref[h] = state[h]

    for u, h in items:
        o = outs[u, h]
        o = o * lax.rsqrt(jnp.mean(o * o, axis=-1, keepdims=True) + LN_EPS) * gn_ref[...]
        o_ref[rs[u], vs[h]] = (_silu(g_ref[rs[u], vs[h]].astype(F32)) * o).astype(o_ref.dtype)


def _gla(h_main, log_a, g_norm):
    s = h_main.shape[0]
    C = GLA_CHUNK
    m = jnp.asarray(_gla_decay_matrix(C), dtype=BF16)
    nrow = m.shape[0]
    rows = min(s, GLA_STEP_CHUNKS * C)
    wk = GLA_HEADS * GLA_DK
    wv = GLA_HEADS * GLA_DV
    return pl.pallas_call(
        _gla_kernel,
        out_shape=jax.ShapeDtypeStruct((s, wv), BF16),
        grid=(s // rows,),
        in_specs=[pl.BlockSpec((nrow, C), lambda c: (0, 0)),
                  pl.BlockSpec((rows, wk), lambda c: (c, GQ_OFF // wk)),
                  pl.BlockSpec((rows, wk), lambda c: (c, GK_OFF // wk)),
                  pl.BlockSpec((rows, wv), lambda c: (c, GV_OFF // wv)),
                  pl.BlockSpec((rows, wv), lambda c: (c, GG_OFF // wv)),
                  pl.BlockSpec((rows, wk), lambda c: (c, 0)),
                  pl.BlockSpec((1, GLA_DV), lambda c: (0, 0))],
        out_specs=pl.BlockSpec((rows, wv), lambda c: (c, 0)),
        scratch_shapes=[pltpu.VMEM((GLA_HEADS, GLA_DV, GLA_DK), F32)],
        compiler_params=_cparams(("arbitrary",)),
        name="gla",
    )(m, h_main, h_main, h_main, h_main, log_a, g_norm)


def _mixout_kernel(ret_ref, gla_ref, x_ref, w_ref, g_ref, b_ref, o_ref, wb_ref):
    nr = ret_ref.shape[1]

    @pl.when(pl.program_id(0) == 0)
    def _():
        wb_ref[...] = w_ref[...].astype(BF16)

    tm = x_ref.shape[0]
    parts = 2 if tm % 512 == 0 else 1
    rows = [slice(p * (tm // parts), (p + 1) * (tm // parts)) for p in range(parts)]
    mix = [_dot(ret_ref[r, :], wb_ref[:nr, :]) + _dot(gla_ref[r, :], wb_ref[nr:, :]) for r in rows]
    for r, m in zip(rows, mix):
        o_ref[r, :] = _layer_norm(DEEPNORM_ALPHA * x_ref[r, :] + m, g_ref[...], b_ref[...])


def _mixout_ln(ret, gla, x2d, w_b, g, b):
    s = x2d.shape[0]
    tm = min(s, 512)
    nr, ng = ret.shape[1], gla.shape[1]
    return pl.pallas_call(
        _mixout_kernel,
        out_shape=jax.ShapeDtypeStruct((s, D_MODEL), F32),
        grid=(s // tm,),
        in_specs=[pl.BlockSpec((tm, nr), lambda i: (i, 0)),
                  pl.BlockSpec((tm, ng), lambda i: (i, 0)),
                  pl.BlockSpec((tm, D_MODEL), lambda i: (i, 0)),
                  pl.BlockSpec((nr + ng, D_MODEL), lambda i: (0, 0), pipeline_mode=pl.Buffered(1)),
                  pl.BlockSpec((1, D_MODEL), lambda i: (0, 0)),
                  pl.BlockSpec((1, D_MODEL), lambda i: (0, 0))],
        out_specs=pl.BlockSpec((tm, D_MODEL), lambda i: (i, 0)),
        scratch_shapes=[pltpu.VMEM((nr + ng, D_MODEL), BF16)],
        compiler_params=_cparams(("arbitrary",)),
        name="mixout_ln1",
    )(ret, gla, x2d, w_b, g, b)


def _kv_kernel(mem_ref, wk_ref, wv_ref, k_ref, v_ref):
    m = mem_ref[...].astype(BF16)
    k_ref[...] = _dot(m, wk_ref[...].astype(BF16)).astype(k_ref.dtype)
    v_ref[...] = _dot(m, wv_ref[...].astype(BF16)).astype(v_ref.dtype)


def _mem_kv(mem_b, wk_b, wv_b):
    tn = 512
    return pl.pallas_call(
        _kv_kernel,
        out_shape=(jax.ShapeDtypeStruct((MEM_LEN, D_MODEL), BF16),
                   jax.ShapeDtypeStruct((MEM_LEN, D_MODEL), BF16)),
        grid=(D_MODEL // tn,),
        in_specs=[pl.BlockSpec((MEM_LEN, D_MODEL), lambda j: (0, 0)),
                  pl.BlockSpec((D_MODEL, tn), lambda j: (0, j)),
                  pl.BlockSpec((D_MODEL, tn), lambda j: (0, j))],
        out_specs=(pl.BlockSpec((MEM_LEN, tn), lambda j: (0, j)),
                   pl.BlockSpec((MEM_LEN, tn), lambda j: (0, j))),
        compiler_params=_cparams(("arbitrary",)),
        name="mem_kv",
    )(mem_b, wk_b, wv_b)


def _cross_kernel(h_ref, wq_ref, k_ref, v_ref, wo_ref, g_ref, b_ref, wr_ref, br_ref,
                  h2_ref, lg_ref):
    tm = h_ref.shape[0]
    parts = 2 if tm % 512 == 0 else 1
    pr = tm // parts
    rows = [slice(p * pr, (p + 1) * pr) for p in range(parts)]
    sls = [slice(hd * MEM_HEAD_DIM, (hd + 1) * MEM_HEAD_DIM) for hd in range(MEM_HEADS)]
    h1 = [h_ref[r, :] for r in rows]
    q = [_dot(h.astype(BF16), wq_ref[...]).astype(BF16) for h in h1]
    o = []
    for p in range(parts):
        outs = []
        for sl in sls:
            s = _dot_nt(q[p][:, sl], k_ref[:, sl]) * (MEM_HEAD_DIM ** -0.5)
            s = s - jnp.max(s, axis=-1, keepdims=True)
            e = jnp.exp(s)
            e = e / jnp.sum(e, axis=-1, keepdims=True)
            outs.append(_dot(e.astype(BF16), v_ref[:, sl]))
        o.append(jnp.concatenate(outs, axis=-1).astype(BF16))
    cross = [_dot(x, wo_ref[...]) for x in o]
    for p in range(parts):
        h2 = _layer_norm(DEEPNORM_ALPHA * h1[p] + cross[p], g_ref[...], b_ref[...])
        h2_ref[rows[p], :] = h2
        lg_ref[rows[p], :] = _dot(h2.astype(BF16), wr_ref[...]) + br_ref[...]


def _cross_attention(h1, wq_b, k, v, wo_b, g, b, w_route, b_route):
    s = h1.shape[0]
    tm = min(s, 512)
    const = lambda shape: pl.BlockSpec(shape, lambda i: (0, 0), pipeline_mode=pl.Buffered(1))
    return pl.pallas_call(
        _cross_kernel,
        out_shape=(jax.ShapeDtypeStruct((s, D_MODEL), F32),
                   jax.ShapeDtypeStruct((s, LANES), F32)),
        grid=(s // tm,),
        in_specs=[pl.BlockSpec((tm, D_MODEL), lambda i: (i, 0)),
                  const((D_MODEL, D_MODEL)),
                  const((MEM_LEN, D_MODEL)),
                  const((MEM_LEN, D_MODEL)),
                  const((D_MODEL, D_MODEL)),
                  const((1, D_MODEL)),
                  const((1, D_MODEL)),
                  const((D_MODEL, LANES)),
                  const((1, LANES))],
        out_specs=(pl.BlockSpec((tm, D_MODEL), lambda i: (i, 0)),
                   pl.BlockSpec((tm, LANES), lambda i: (i, 0))),
        compiler_params=_cparams(("arbitrary",)),
        name="cross_attn_ln2",
    )(h1, wq_b, k, v, wo_b, g, b, w_route, b_route)


def _route_kernel(lg_ref, slot_ref, slot_t_ref, gate_ref, runs_ref, plan_ref, tot_row, tot_col, gstart, rinfo):
    phase = pl.program_id(0)
    i = pl.program_id(1)
    tm = lg_ref.shape[0]
    lane = lax.broadcasted_iota(jnp.int32, (tm, LANES), 1)
    rows = pl.ds(pl.multiple_of(i * tm, tm), tm)

    @pl.when(phase == 0)
    def _():
        neg = -jnp.inf
        logits = lg_ref[...]
        gmask = lane < N_GROUPS
        gl = jnp.where(gmask, logits, neg)
        ge = jnp.exp(gl - jnp.max(gl, axis=-1, keepdims=True))
        pg = ge / jnp.sum(ge, axis=-1, keepdims=True)
        pg_sel = jnp.max(pg, axis=-1, keepdims=True)
        grp = jnp.min(jnp.where((pg == pg_sel) & gmask, lane, LANES), axis=-1, keepdims=True)

        fl_lane = lane - N_GROUPS
        fmask = (fl_lane >= 0) & (fl_lane < N_EXPERTS) & ((fl_lane >> 3) == grp)
        fl = jnp.where(fmask, logits, neg)
        fe = jnp.exp(fl - jnp.max(fl, axis=-1, keepdims=True))
        fp = fe / jnp.sum(fe, axis=-1, keepdims=True)
        p1 = jnp.max(fp, axis=-1, keepdims=True)
        i1 = jnp.min(jnp.where((fp == p1) & fmask, lane, LANES), axis=-1, keepdims=True)
        rest = fmask & (lane != i1)
        fp2 = jnp.where(rest, fp, -1.0)
        p2 = jnp.max(fp2, axis=-1, keepdims=True)
        i2 = jnp.min(jnp.where((fp2 == p2) & rest, lane, LANES), axis=-1, keepdims=True)
        psum = p1 + p2
        picks = ((i1 - N_GROUPS).astype(F32), (i2 - N_GROUPS).astype(F32),
                 pg_sel * p1 / psum, pg_sel * p2 / psum)
        info = jnp.zeros((tm, LANES), F32)
        for j, val in enumerate(picks):
            info = jnp.where(lane == j, val, info)
        rinfo[rows, :] = info

    info = rinfo[rows, :]
    oh1 = lane == info[:, 0:1].astype(jnp.int32)
    oh2 = lane == info[:, 1:2].astype(jnp.int32)
    gate1 = info[:, 2:3]
    gate2 = info[:, 3:4]
    oh = (jnp.where(oh1, 1.0, 0.0) + jnp.where(oh2, 1.0, 0.0)).astype(BF16)
    ones = jnp.ones((tm, LANES), BF16)

    def align_up(v, a):
        return jnp.floor((v + (a - 1.0)) * (1.0 / a)) * a

    run_row = align_up(_dot_tn(ones, oh)[0:8, :], RUN_ALIGN)
    rr = lax.broadcasted_iota(jnp.int32, (LANES, LANES), 0)
    cc = lax.broadcasted_iota(jnp.int32, (LANES, LANES), 1)
    strict_upper = jnp.where(rr < cc, 1.0, 0.0).astype(BF16)

    @pl.when((phase == 0) & (i == 0))
    def _():
        tot_row[...] = jnp.zeros_like(tot_row)
        tot_col[...] = jnp.zeros_like(tot_col)

    @pl.when(phase == 0)
    def _():
        tot_row[...] += run_row
        tot_col[...] += align_up(_dot_tn(oh, ones), RUN_ALIGN)

    @pl.when((phase == 1) & (i == 0))
    def _():
        nblk_row = align_up(tot_row[...], MOE_BLK) * (1.0 / MOE_BLK)
        nblk_col = align_up(tot_col[...], MOE_BLK) * (1.0 / MOE_BLK)
        lower_incl = jnp.where(cc <= rr, 1.0, 0.0).astype(BF16)
        base = _dot(nblk_row.astype(BF16), strict_upper) * float(MOE_BLK)
        ends = _dot(lower_incl, nblk_col.astype(BF16))
        expert_rows = rr < N_EXPERTS
        be = jnp.sum(jnp.where(expert_rows & (ends <= cc.astype(F32)), 1.0, 0.0), axis=0, keepdims=True)
        be = jnp.minimum(be, N_EXPERTS - 1.0)
        total = jnp.sum(jnp.where(expert_rows, nblk_col, 0.0), axis=0, keepdims=True)
        sub = lax.broadcasted_iota(jnp.int32, plan_ref.shape, 0)
        plan = jnp.where(sub == 0, jnp.broadcast_to(be, plan_ref.shape),
                         jnp.where(sub == 1, jnp.broadcast_to(total, plan_ref.shape),
                                   jnp.where(sub == 2, base + tot_row[...],
                                             jnp.where(sub == 3, nblk_row,
                                                       nblk_row * float(MOE_BLK) - tot_row[...]))))
        plan_ref[...] = plan.astype(jnp.int32)
        gstart[...] = base

    @pl.when(phase == 1)
    def _():
        off = _dot((run_row * (1.0 / RUN_ALIGN)).astype(BF16), strict_upper) * float(RUN_ALIGN)
        tr = lax.broadcasted_iota(jnp.int32, (tm, tm), 0)
        tc = lax.broadcasted_iota(jnp.int32, (tm, tm), 1)
        strict_lower = jnp.where(tc < tr, 1.0, 0.0).astype(BF16)
        slot = _dot(strict_lower, oh) + off[0:1, :]
        s1 = jnp.sum(jnp.where(oh1, slot, 0.0), axis=-1, keepdims=True)
        s2 = jnp.sum(jnp.where(oh2, slot, 0.0), axis=-1, keepdims=True)
        slots = jnp.where(lane == 0, s1, jnp.where(lane == 1, s2, 0.0))
        slot_ref[...] = slots.astype(jnp.int32)
        slot_t_ref[...] = jnp.transpose(slots)[0:8, :].astype(jnp.int32)
        gate_ref[...] = jnp.where(lane == 0, gate1, jnp.where(lane == 1, gate2, 0.0))
        sub = lax.broadcasted_iota(jnp.int32, runs_ref.shape, 0)
        runs = jnp.where(sub == 0, gstart[...], jnp.where(sub == 1, off, run_row))
        runs_ref[...] = runs.astype(jnp.int32)
        gstart[...] += run_row


def _route(logits):
    t = logits.shape[0]
    tm = min(t, TILE_TOK)
    nt = t // tm
    step = lambda p, i: (i * p, 0)
    return pl.pallas_call(
        _route_kernel,
        out_shape=(jax.ShapeDtypeStruct((t, LANES), jnp.int32),
                   jax.ShapeDtypeStruct((nt * 8, tm), jnp.int32),
                   jax.ShapeDtypeStruct((t, LANES), F32),
                   jax.ShapeDtypeStruct((nt * 8, LANES), jnp.int32),
                   jax.ShapeDtypeStruct((8, LANES), jnp.int32)),
        grid=(2, nt),
        in_specs=[pl.BlockSpec((tm, LANES), lambda p, i: (i, 0))],
        out_specs=(pl.BlockSpec((tm, LANES), step),
                   pl.BlockSpec((8, tm), step),
                   pl.BlockSpec((tm, LANES), step),
                   pl.BlockSpec((8, LANES), step),
                   pl.BlockSpec((8, LANES), lambda p, i: (0, 0))),
        scratch_shapes=[pltpu.VMEM((8, LANES), F32),
                        pltpu.VMEM((LANES, LANES), F32),
                        pltpu.VMEM((8, LANES), F32),
                        pltpu.VMEM((t, LANES), F32)],
        compiler_params=_cparams(("arbitrary", "arbitrary")),
        name="moe_route",
    )(logits)


def _run_copies(rg_ref, ro_ref, rn_ref, tile, make_copy):
    def each(action):
        def body(e, carry):
            j = tile * N_EXPERTS + e
            n = rn_ref[j]

            @pl.when(n > 0)
            def _():
                action(make_copy(pl.multiple_of(rg_ref[j], RUN_ALIGN), pl.multiple_of(ro_ref[j], RUN_ALIGN),
                                 pl.multiple_of(n, RUN_ALIGN)))
            return carry
        lax.fori_loop(0, N_EXPERTS, body, 0)
    return each


def _dispatch_kernel(rg_ref, ro_ref, rn_ref, zrow_ref, zlen_ref, nu_ref, x_ref, slot_t_ref, xb_ref,
                     sorted_ref, zero_ref, sems, zsem):
    i = pl.program_id(0)
    n = pl.num_programs(0)
    tm = x_ref.shape[0]
    nb = xb_ref.shape[0] // MOE_BLK
    buf = i % 2

    def zero_copy(row, size):
        return pltpu.make_async_copy(
            zero_ref.at[pl.ds(0, size), :], xb_ref.at[pl.ds(pl.multiple_of(row, RUN_ALIGN), size), :], zsem)

    def pad_copy(e):
        return zero_copy(zrow_ref[e], pl.multiple_of(zlen_ref[e], RUN_ALIGN))

    def runs_of(tile, slot):
        def make_copy(g, o, size):
            return pltpu.make_async_copy(sorted_ref.at[slot, pl.ds(o, size), :],
                                         xb_ref.at[pl.ds(g, size), :], sems.at[slot])
        return _run_copies(rg_ref, ro_ref, rn_ref, tile, make_copy)

    @pl.when(i == 0)
    def _():
        zero_ref[...] = jnp.zeros_like(zero_ref)

        def issue_zero(e, carry):
            @pl.when(zlen_ref[e] > 0)
            def _():
                pad_copy(e).start()
            return carry
        lax.fori_loop(0, N_EXPERTS, issue_zero, 0)
        lax.fori_loop(nu_ref[0], nb, lambda b, c: (zero_copy(b * MOE_BLK, MOE_BLK).start(), c)[1], 0)

    @pl.when(i >= 2)
    def _():
        runs_of(i - 2, buf)(lambda c: c.wait())

    x = x_ref[...].astype(BF16)
    s1 = slot_t_ref[0:1, :]
    s2 = slot_t_ref[1:2, :]
    for rc in range(TILE_SLOTS // SORT_ROWS):
        r = lax.broadcasted_iota(jnp.int32, (SORT_ROWS, tm), 0) + rc * SORT_ROWS
        perm = jnp.where((r == s1) | (r == s2), 1.0, 0.0).astype(BF16)
        sorted_ref[buf, rc * SORT_ROWS:(rc + 1) * SORT_ROWS, :] = _dot(perm, x).astype(BF16)

    runs_of(i, buf)(lambda c: c.start())

    @pl.when(i == n - 1)
    def _():
        @pl.when(i >= 1)
        def _():
            runs_of(i - 1, 1 - buf)(lambda c: c.wait())
        runs_of(i, buf)(lambda c: c.wait())

        def wait_zero(e, carry):
            @pl.when(zlen_ref[e] > 0)
            def _():
                pad_copy(e).wait()
            return carry
        lax.fori_loop(0, N_EXPERTS, wait_zero, 0)
        lax.fori_loop(nu_ref[0], nb, lambda b, c: (zero_copy(0, MOE_BLK).wait(), c)[1], 0)


def _dispatch(rg, ro, rn, zrow, zlen, n_used, h2, slot_t, n_rows):
    t = h2.shape[0]
    tm = min(t, TILE_TOK)
    grid_spec = pltpu.PrefetchScalarGridSpec(
        num_scalar_prefetch=6,
        grid=(t // tm,),
        in_specs=[pl.BlockSpec((tm, D_MODEL), lambda i, *_: (i, 0)),
                  pl.BlockSpec((8, tm), lambda i, *_: (i, 0))],
        out_specs=pl.BlockSpec(memory_space=pl.ANY),
        scratch_shapes=[pltpu.VMEM((2, TILE_SLOTS, D_MODEL), BF16),
                        pltpu.VMEM((MOE_BLK, D_MODEL), BF16),
                        pltpu.SemaphoreType.DMA((2,)),
                        pltpu.SemaphoreType.DMA(())],
    )
    return pl.pallas_call(
        _dispatch_kernel,
        out_shape=jax.ShapeDtypeStruct((n_rows, D_MODEL), BF16),
        grid_spec=grid_spec,
        compiler_params=_cparams(("arbitrary",)),
        name="moe_dispatch",
    )(rg, ro, rn, zrow, zlen, n_used, h2, slot_t)


def _expert_kernel(be_ref, nu_ref, nblk_ref, x_ref, wg_ref, wu_ref, wd_ref, y_ref,
                   wg_f, wu_f, wd_f, wg_b, wu_b, wd_b, sems, cur_ref):
    step = pl.program_id(0)
    n_used = nu_ref[0]

    def weight_copies(expert, slot):
        half = EXPERT_FF // 2
        return [pltpu.make_async_copy(wg_ref.at[expert], wg_f.at[slot], sems.at[slot]),
                pltpu.make_async_copy(wu_ref.at[expert], wu_f.at[slot], sems.at[slot]),
                pltpu.make_async_copy(wd_ref.at[expert, :half], wd_f.at[slot, :half], sems.at[slot]),
                pltpu.make_async_copy(wd_ref.at[expert, half:], wd_f.at[slot, half:], sems.at[slot])]

    def start_weights(expert, slot):
        for c in weight_copies(expert, slot):
            c.start(priority=1)

    def after(blk):
        return blk + nblk_ref[be_ref[blk]]

    @pl.when(step == 0)
    def _():
        cur_ref[0] = 0
        start_weights(be_ref[0], 0)

        @pl.when(after(0) < n_used)
        def _():
            start_weights(be_ref[after(0)], 1)

    for sub in range(EXPERT_STEP_BLKS):
        b = step * EXPERT_STEP_BLKS + sub
        used = b < n_used
        e = be_ref[b]
        prev = be_ref[jnp.maximum(b - 1, 0)]
        fresh = used & ((b == 0) | (e != prev))
        rows = slice(sub * MOE_BLK, (sub + 1) * MOE_BLK)

        @pl.when(fresh)
        def _():
            slot = cur_ref[0]
            for c in weight_copies(e, slot):
                c.wait()
            wg_b[...] = wg_f[slot].astype(BF16)
            wu_b[...] = wu_f[slot].astype(BF16)
            wd_b[...] = wd_f[slot].astype(BF16)
            nxt = after(b)

            @pl.when(nxt < n_used)
            def _():
                nxt2 = after(nxt)

                @pl.when(nxt2 < n_used)
                def _():
                    start_weights(be_ref[nxt2], slot)
            cur_ref[0] = 1 - slot

        @pl.when(used)
        def _():
            x = x_ref[rows, :]
            hid = _silu(_dot(x, wg_b[...])) * _dot(x, wu_b[...])
            y_ref[rows, :] = _dot(hid.astype(BF16), wd_b[...]).astype(y_ref.dtype)

        @pl.when(jnp.logical_not(used))
        def _():
            y_ref[rows, :] = jnp.zeros((MOE_BLK, D_MODEL), y_ref.dtype)


def _experts(block_e, n_used, nblk, xb, w_gate, w_up, w_down):
    step_rows = EXPERT_STEP_BLKS * MOE_BLK
    any_space = pl.BlockSpec(memory_space=pl.ANY)
    last_step = lambda nu: (nu[0] - 1) // EXPERT_STEP_BLKS
    grid_spec = pltpu.PrefetchScalarGridSpec(
        num_scalar_prefetch=3,
        grid=(xb.shape[0] // step_rows,),
        in_specs=[pl.BlockSpec((step_rows, D_MODEL), lambda s, be, nu, nk: (jnp.minimum(s, last_step(nu)), 0)),
                  any_space, any_space, any_space],
        out_specs=pl.BlockSpec((step_rows, D_MODEL), lambda s, be, nu, nk: (s, 0)),
        scratch_shapes=[pltpu.VMEM((2, D_MODEL, EXPERT_FF), F32),
                        pltpu.VMEM((2, D_MODEL, EXPERT_FF), F32),
                        pltpu.VMEM((2, EXPERT_FF, D_MODEL), F32),
                        pltpu.VMEM((D_MODEL, EXPERT_FF), BF16),
                        pltpu.VMEM((D_MODEL, EXPERT_FF), BF16),
                        pltpu.VMEM((EXPERT_FF, D_MODEL), BF16),
                        pltpu.SemaphoreType.DMA((2,)),
                        pltpu.SMEM((1,), jnp.int32)],
    )
    return pl.pallas_call(
        _expert_kernel,
        out_shape=jax.ShapeDtypeStruct(xb.shape, BF16),
        grid_spec=grid_spec,
        compiler_params=_cparams(("arbitrary",)),
        name="moe_experts",
    )(block_e, n_used, nblk, xb, w_gate, w_up, w_down)


def _combine_kernel(rg_ref, ro_ref, rn_ref, yb_ref, h2_ref, slot_ref, gate_ref, g_ref, b_ref, o_ref,
                    ybuf, sems):
    i = pl.program_id(0)
    n = pl.num_programs(0)
    tm = h2_ref.shape[0]
    buf = i % 2

    def runs_of(tile, slot):
        def make_copy(g, o, size):
            return pltpu.make_async_copy(yb_ref.at[pl.ds(g, size), :],
                                         ybuf.at[slot, pl.ds(o, size), :], sems.at[slot])
        return _run_copies(rg_ref, ro_ref, rn_ref, tile, make_copy)

    @pl.when(i == 0)
    def _():
        ybuf[...] = jnp.zeros_like(ybuf)
        runs_of(0, 0)(lambda c: c.start())

    @pl.when(i + 1 < n)
    def _():
        runs_of(i + 1, 1 - buf)(lambda c: c.start())

    runs_of(i, buf)(lambda c: c.wait())

    lane = lax.broadcasted_iota(jnp.int32, (tm, TILE_SLOTS), 1)
    slots = slot_ref[...]
    gate = gate_ref[...]
    y = ybuf[buf]
    sel = jnp.where(lane == slots[:, 0:1], gate[:, 0:1],
                    jnp.where(lane == slots[:, 1:2], gate[:, 1:2], 0.0)).astype(BF16)
    parts = 2 if tm % 512 == 0 else 1
    rows = [slice(p * (tm // parts), (p + 1) * (tm // parts)) for p in range(parts)]
    ffn = [_dot(sel[r, :], y) for r in rows]
    for r, f in zip(rows, ffn):
        o_ref[r, :] = _layer_norm(DEEPNORM_ALPHA * h2_ref[r, :] + f, g_ref[...], b_ref[...])


def _combine_ln(rg, ro, rn, yb, h2, slots, gate, g, b):
    t = h2.shape[0]
    tm = min(t, TILE_TOK)
    grid_spec = pltpu.PrefetchScalarGridSpec(
        num_scalar_prefetch=3,
        grid=(t // tm,),
        in_specs=[pl.BlockSpec(memory_space=pl.ANY),
                  pl.BlockSpec((tm, D_MODEL), lambda i, *_: (i, 0)),
                  pl.BlockSpec((tm, LANES), lambda i, *_: (i, 0)),
                  pl.BlockSpec((tm, LANES), lambda i, *_: (i, 0)),
                  pl.BlockSpec((1, D_MODEL), lambda i, *_: (0, 0)),
                  pl.BlockSpec((1, D_MODEL), lambda i, *_: (0, 0))],
        out_specs=pl.BlockSpec((tm, D_MODEL), lambda i, *_: (i, 0)),
        scratch_shapes=[pltpu.VMEM((2, TILE_SLOTS, D_MODEL), BF16),
                        pltpu.SemaphoreType.DMA((2,))],
    )
    return pl.pallas_call(
        _combine_kernel,
        out_shape=jax.ShapeDtypeStruct((t, D_MODEL), F32),
        grid_spec=grid_spec,
        compiler_params=_cparams(("arbitrary",)),
        name="moe_combine_ln3",
    )(rg, ro, rn, yb, h2, slots, gate, g, b)


def _mixer(x2d, positions, w_in, w_gla_a2, b_gla_a, g_gla_norm):
    s = x2d.shape[0]
    half = RET_DK // 2
    inv_freq = (ROPE_BASE ** (-jnp.arange(half, dtype=F32) / half)).reshape(1, half)
    cos, sin = _rope_table(positions.reshape(s, 1), inv_freq)
    w_in_t = jnp.swapaxes(w_in, 0, 1)
    w_lr_t = jnp.pad(w_in_t[GLR_OFF:].astype(BF16), ((0, LANES - GLA_LOWRANK), (0, 0)))
    w_a2 = jnp.pad(w_gla_a2.astype(BF16), ((0, LANES - GLA_LOWRANK), (0, 0)))
    h_main, log_a = _proj_in(x2d, w_in_t, w_lr_t, w_a2, b_gla_a.reshape(1, -1))
    log_gamma = jnp.log1p(-jnp.exp2(-5.0 - jnp.arange(RET_HEADS, dtype=F32)))
    ret = _retention(h_main, cos, sin, log_gamma)
    gla = _gla(h_main, log_a, g_gla_norm.reshape(1, -1))
    return ret, gla


def _moe(h2, logits, w_gate, w_up, w_down, g, b):
    t = h2.shape[0]
    nt = t // min(t, TILE_TOK)
    slots, slot_t, gate, runs, plan = _route(logits)
    runs = runs.reshape(nt, 8, LANES)[:, :3, :N_EXPERTS]
    rg, ro, rn = (runs[:, j, :].reshape(-1) for j in range(3))
    max_rows = 2 * t + nt * N_EXPERTS * (RUN_ALIGN - 1) + N_EXPERTS * (MOE_BLK - 1)
    nb = -(-max_rows // (MOE_BLK * EXPERT_STEP_BLKS)) * EXPERT_STEP_BLKS
    block_e, n_used = plan[0, :nb], plan[1, :1]
    pad_row, nblk, pad_len = (plan[j, :N_EXPERTS] for j in (2, 3, 4))
    xb = _dispatch(rg, ro, rn, pad_row, pad_len, n_used, h2, slot_t, nb * MOE_BLK)
    yb = _experts(block_e, n_used, nblk, xb, w_gate, w_up, w_down)
    return _combine_ln(rg, ro, rn, yb, h2, slots, gate, g, b)


def kernel(x, mem, positions, w_in, w_gla_a2, b_gla_a, g_gla_norm, w_mix_out, ln1_g, ln1_b, w_mq, w_mk, w_mv, w_mo, ln2_g, ln2_b, w_route_group, b_route_group, w_route_expert, b_route_expert, w_exp_gate, w_exp_up, w_exp_down, ln3_g, ln3_b):
    bsz, s, d = x.shape
    assert bsz == 1 and d == D_MODEL
    x2d = x.reshape(s, d)
    row = lambda v: v.reshape(1, -1)

    ret, gla = _mixer(x2d, positions, w_in[0], w_gla_a2[0], b_gla_a[0], g_gla_norm[0])
    h1 = _mixout_ln(ret, gla, x2d, w_mix_out[0], row(ln1_g[0]), row(ln1_b[0]))

    k, v = _mem_kv(mem[0], w_mk[0], w_mv[0])
    n_route = N_GROUPS + N_EXPERTS
    w_route = jnp.pad(jnp.concatenate([w_route_group[0], w_route_expert[0]], axis=1).astype(BF16),
                      ((0, 0), (0, LANES - n_route)))
    b_route = jnp.pad(jnp.concatenate([b_route_group[0], b_route_expert[0].reshape(-1)]),
                      (0, LANES - n_route)).reshape(1, LANES)
    h2, logits = _cross_attention(h1, w_mq[0].astype(BF16), k, v, w_mo[0].astype(BF16),
                                  row(ln2_g[0]), row(ln2_b[0]), w_route, b_route)

    out = _moe(h2, logits, w_exp_gate[0], w_exp_up[0], w_exp_down[0],
               row(ln3_g[0]), row(ln3_b[0]))
    return out.reshape(bsz, s, d)
```

```python
import math

import jax
import jax.numpy as jnp
from jax import lax
from jax.experimental import pallas as pl
from jax.experimental.pallas import tpu as pltpu

F32 = jnp.float32
BF16 = jnp.bfloat16

D_MODEL = 2048
MEM_LEN = 256
RET_HEADS = 4
RET_DK = 256
RET_DV = 256
GLA_HEADS = 4
GLA_DK = 128
GLA_DV = 256
GLA_LOWRANK = 16
GLA_TAU = 16.0
ROPE_BASE = 10000.0
MEM_HEADS = 4
MEM_HEAD_DIM = D_MODEL // MEM_HEADS
N_GROUPS = 4
EXPERTS_PER_GROUP = 8
N_EXPERTS = N_GROUPS * EXPERTS_PER_GROUP
EXPERT_FF = 512
LN_EPS = 1e-5
DEPTH = 1
DEEPNORM_ALPHA = (2 * DEPTH) ** 0.25

RQ_OFF, RK_OFF, RV_OFF, RG_OFF = 0, 1024, 2048, 3072
GQ_OFF, GK_OFF, GV_OFF, GG_OFF, GLR_OFF = 4096, 4608, 5120, 6144, 7168
IN_MAIN = 7168

LANES = 128
RET_CHUNK = 256
GLA_CHUNK = 128
GLA_LEVELS = 7
GLA_STEP_CHUNKS = 4
MOE_BLK = 256
EXPERT_STEP_BLKS = 4
TILE_TOK = 512
RUN_ALIGN = 16
TILE_SLOTS = 2 * TILE_TOK + N_EXPERTS * RUN_ALIGN
SORT_ROWS = 512
PROJ_K_CHUNK = 512
VMEM_LIMIT = 56 * 1024 * 1024


def _cparams(sem):
    return pltpu.CompilerParams(dimension_semantics=sem, vmem_limit_bytes=VMEM_LIMIT)


def _layer_norm(y, g, b):
    mu = jnp.mean(y, axis=-1, keepdims=True)
    d = y - mu
    var = jnp.mean(d * d, axis=-1, keepdims=True)
    return d * lax.rsqrt(var + LN_EPS) * g + b


def _silu(x):
    return x / (1.0 + jnp.exp(-x))


def _dot(a, b):
    return jnp.dot(a, b, preferred_element_type=F32)


def _dot_nt(a, b):
    return lax.dot_general(a, b, (((1,), (1,)), ((), ())), preferred_element_type=F32)


def _dot_tn(a, b):
    return lax.dot_general(a, b, (((0,), (0,)), ((), ())), preferred_element_type=F32)


def _rope_kernel(pos_ref, invf_ref, cos_ref, sin_ref):
    ang = pos_ref[...].astype(F32) * invf_ref[...]
    cos_ref[...] = jnp.cos(ang)
    sin_ref[...] = jnp.sin(ang)


def _rope_table(pos_col, inv_freq):
    s = pos_col.shape[0]
    tm = min(s, 1024)
    half = inv_freq.shape[1]
    return pl.pallas_call(
        _rope_kernel,
        out_shape=(jax.ShapeDtypeStruct((s, half), F32), jax.ShapeDtypeStruct((s, half), F32)),
        grid=(s // tm,),
        in_specs=[pl.BlockSpec((tm, 1), lambda i: (i, 0)),
                  pl.BlockSpec((1, half), lambda i: (0, 0))],
        out_specs=(pl.BlockSpec((tm, half), lambda i: (i, 0)),
                   pl.BlockSpec((tm, half), lambda i: (i, 0))),
        compiler_params=_cparams(("arbitrary",)),
        name="rope_table",
    )(pos_col, inv_freq)


def _proj_in_kernel(x_ref, wt_ref, wlr_ref, wa2_ref, ba_ref, o_ref, la_ref, xb_ref):
    @pl.when(pl.program_id(1) == 0)
    def _():
        xb_ref[...] = x_ref[...].astype(BF16)
        glr = _dot_nt(xb_ref[...], wlr_ref[...])
        z = _dot(glr.astype(BF16), wa2_ref[...]) + ba_ref[...]
        la_ref[...] = (jnp.minimum(z, 0.0) - jnp.log(1.0 + jnp.exp(-jnp.abs(z)))) / GLA_TAU

    acc = None
    for c in range(D_MODEL // PROJ_K_CHUNK):
        ck = slice(c * PROJ_K_CHUNK, (c + 1) * PROJ_K_CHUNK)
        part = _dot_nt(xb_ref[:, ck], wt_ref[:, ck].astype(BF16))
        acc = part if acc is None else acc + part
    o_ref[...] = acc.astype(o_ref.dtype)


def _proj_in(x2d, w_in_t, w_lr_t, w_a2, b_a):
    s = x2d.shape[0]
    tm = min(s, 1024)
    tn = 1024
    n = GLA_HEADS * GLA_DK
    return pl.pallas_call(
        _proj_in_kernel,
        out_shape=(jax.ShapeDtypeStruct((s, IN_MAIN), BF16), jax.ShapeDtypeStruct((s, n), F32)),
        grid=(s // tm, IN_MAIN // tn),
        in_specs=[pl.BlockSpec((tm, D_MODEL), lambda i, j: (i, 0)),
                  pl.BlockSpec((tn, D_MODEL), lambda i, j: (j, 0)),
                  pl.BlockSpec((LANES, D_MODEL), lambda i, j: (0, 0)),
                  pl.BlockSpec((LANES, n), lambda i, j: (0, 0)),
                  pl.BlockSpec((1, n), lambda i, j: (0, 0))],
        out_specs=(pl.BlockSpec((tm, tn), lambda i, j: (i, j)),
                   pl.BlockSpec((tm, n), lambda i, j: (i, 0))),
        scratch_shapes=[pltpu.VMEM((tm, D_MODEL), BF16)],
        compiler_params=_cparams(("arbitrary", "arbitrary")),
        name="proj_in",
    )(x2d, w_in_t, w_lr_t, w_a2, b_a)


def _rotary(t, cos, sin):
    half = t.shape[-1] // 2
    t1, t2 = t[:, :half], t[:, half:]
    return jnp.concatenate([t1 * cos - t2 * sin, t1 * sin + t2 * cos], axis=-1)


def _retention_kernel(lg_ref, q_ref, k_ref, v_ref, g_ref, cos_ref, sin_ref, o_ref, state_ref, intra_ref, dq_ref, dk_ref):
    c = pl.program_id(0)
    C = q_ref.shape[0]

    @pl.when(c == 0)
    def _():
        state_ref[...] = jnp.zeros_like(state_ref)
        ri = lax.broadcasted_iota(jnp.int32, (C, C), 0)
        ci = lax.broadcasted_iota(jnp.int32, (C, C), 1)
        rel = jnp.maximum(ri - ci, 0).astype(F32)
        n = lax.broadcasted_iota(jnp.int32, (C, RET_DK), 0).astype(F32)
        for h in range(RET_HEADS):
            lg = lg_ref[h]
            intra_ref[h] = jnp.where(ri >= ci, jnp.exp(lg * rel), 0.0)
            dq_ref[h] = jnp.exp(lg * (n + 1.0))
            dk_ref[h] = jnp.exp(lg * (C - 1.0 - n))

    cos = cos_ref[...]
    sin = sin_ref[...]

    for h in range(RET_HEADS):
        sl = slice(h * RET_DK, (h + 1) * RET_DK)
        q = _rotary(q_ref[:, sl].astype(F32), cos, sin)
        k = _rotary(k_ref[:, sl].astype(F32), cos, sin) * (RET_DK ** -0.5)
        v = v_ref[:, sl].astype(BF16)
        decay_q = dq_ref[h]
        decay_k = dk_ref[h]
        decay_chunk = jnp.exp(lg_ref[h] * C)

        state = state_ref[h]
        scores = _dot_nt(q.astype(BF16), k.astype(BF16)) * intra_ref[h]
        o = _dot(scores.astype(BF16), v) + _dot((q * decay_q).astype(BF16), state.astype(BF16))
        state_ref[h] = decay_chunk * state + _dot_tn((k * decay_k).astype(BF16), v)

        mu = jnp.mean(o, axis=-1, keepdims=True)
        d = o - mu
        var = jnp.mean(d * d, axis=-1, keepdims=True)
        o = d * lax.rsqrt(var + LN_EPS)
        o_ref[:, sl] = (_silu(g_ref[:, sl].astype(F32)) * o).astype(o_ref.dtype)


def _retention(h_main, cos, sin, log_gamma):
    s = h_main.shape[0]
    C = min(RET_CHUNK, s)
    w = RET_HEADS * RET_DK
    col = lambda off: (lambda c, lg: (c, off // w))
    grid_spec = pltpu.PrefetchScalarGridSpec(
        num_scalar_prefetch=1,
        grid=(s // C,),
        in_specs=[pl.BlockSpec((C, w), col(RQ_OFF)),
                  pl.BlockSpec((C, w), col(RK_OFF)),
                  pl.BlockSpec((C, w), col(RV_OFF)),
                  pl.BlockSpec((C, w), col(RG_OFF)),
                  pl.BlockSpec((C, RET_DK // 2), lambda c, lg: (c, 0)),
                  pl.BlockSpec((C, RET_DK // 2), lambda c, lg: (c, 0))],
        out_specs=pl.BlockSpec((C, RET_HEADS * RET_DV), lambda c, lg: (c, 0)),
        scratch_shapes=[pltpu.VMEM((RET_HEADS, RET_DK, RET_DV), F32),
                        pltpu.VMEM((RET_HEADS, C, C), F32),
                        pltpu.VMEM((RET_HEADS, C, RET_DK), F32),
                        pltpu.VMEM((RET_HEADS, C, RET_DK), F32)],
    )
    return pl.pallas_call(
        _retention_kernel,
        out_shape=jax.ShapeDtypeStruct((s, RET_HEADS * RET_DV), BF16),
        grid_spec=grid_spec,
        compiler_params=_cparams(("arbitrary",)),
        name="retention",
    )(log_gamma, h_main, h_main, h_main, h_main, cos, sin)


def _gla_decay_matrix(C):
    import numpy as np
    levels = int(math.log2(C))
    r = np.arange(C)[:, None]
    t = np.arange(C)[None, :]
    mats = []
    for l in range(levels):
        blk = C >> l
        half = blk // 2
        m = (r // blk) * blk + half - 1
        qside = (r % blk) >= half
        mats.append(np.where(qside, (t > m) & (t <= r), (t > r) & (t <= m)))
    mats.append(t <= r)
    mats.append(t > r)
    return np.concatenate(mats, axis=0).astype(np.float32)


def _gla_kernel(m_ref, q_ref, k_ref, v_ref, g_ref, la_ref, gn_ref, o_ref, state_ref):
    c = pl.program_id(0)
    C = GLA_CHUNK
    levels = GLA_LEVELS
    nsub = q_ref.shape[0] // C

    @pl.when(c == 0)
    def _():
        state_ref[...] = jnp.zeros_like(state_ref)

    m = m_ref[...]
    ri = lax.broadcasted_iota(jnp.int32, (C, C), 0)
    ci = lax.broadcasted_iota(jnp.int32, (C, C), 1)
    xor = jnp.where(ri > ci, ri ^ ci, 0)
    row = lax.broadcasted_iota(jnp.int32, (C, 1), 0)

    heads = range(GLA_HEADS)
    ks = [slice(h * GLA_DK, (h + 1) * GLA_DK) for h in heads]
    vs = [slice(h * GLA_DV, (h + 1) * GLA_DV) for h in heads]
    rs = [slice(u * C, (u + 1) * C) for u in range(nsub)]
    items = [(u, h) for u in range(nsub) for h in heads]

    expo, q, k, v, scores = {}, {}, {}, {}, {}
    for u, h in items:
        la = la_ref[rs[u], ks[h]]
        la_hi = la.astype(BF16)
        la_lo = (la - la_hi.astype(F32)).astype(BF16)
        expo[u, h] = jnp.exp(_dot(m, la_hi) + _dot(m, la_lo))
    for u, h in items:
        q[u, h] = q_ref[rs[u], ks[h]].astype(F32) * (GLA_DK ** -0.5)
        k[u, h] = k_ref[rs[u], ks[h]].astype(F32)
        v[u, h] = v_ref[rs[u], vs[h]].astype(BF16)
        scores[u, h] = jnp.where(ri == ci, _dot_nt(q[u, h].astype(BF16), k[u, h].astype(BF16)), 0.0)
    for l in range(levels):
        half = C >> (l + 1)
        shift = int(math.log2(half))
        qside = (row & half) != 0
        keep = (xor >> shift) == 1
        for it in items:
            x = (jnp.where(qside, q[it], k[it]) * expo[it][l * C:(l + 1) * C]).astype(BF16)
            scores[it] = scores[it] + jnp.where(keep, _dot_nt(x, x), 0.0)

    state = [state_ref[h] for h in heads]
    outs = {}
    for u, h in items:
        e_b = expo[u, h][levels * C:(levels + 1) * C]
        e_rev = expo[u, h][(levels + 1) * C:(levels + 2) * C]
        e_last = e_b[C - 1:C, :]
        outs[u, h] = (_dot(scores[u, h].astype(BF16), v[u, h])
                      + _dot_nt((q[u, h] * e_b).astype(BF16), state[h].astype(BF16)))
        state[h] = state[h] * e_last + _dot_tn(v[u, h], (k[u, h] * e_rev).astype(BF16))
    for h in heads:
        state_ref[h] = state[h]

    for u, h in items:
        o = outs[u, h]
        o = o * lax.rsqrt(jnp.mean(o * o, axis=-1, keepdims=True) + LN_EPS) * gn_ref[...]
        o_ref[rs[u], vs[h]] = (_silu(g_ref[rs[u], vs[h]].astype(F32)) * o).astype(o_ref.dtype)


def _gla(h_main, log_a, g_norm):
    s = h_main.shape[0]
    C = GLA_CHUNK
    m = jnp.asarray(_gla_decay_matrix(C), dtype=BF16)
    nrow = m.shape[0]
    rows = min(s, GLA_STEP_CHUNKS * C)
    wk = GLA_HEADS * GLA_DK
    wv = GLA_HEADS * GLA_DV
    return pl.pallas_call(
        _gla_kernel,
        out_shape=jax.ShapeDtypeStruct((s, wv), BF16),
        grid=(s // rows,),
        in_specs=[pl.BlockSpec((nrow, C), lambda c: (0, 0)),
                  pl.BlockSpec((rows, wk), lambda c: (c, GQ_OFF // wk)),
                  pl.BlockSpec((rows, wk), lambda c: (c, GK_OFF // wk)),
                  pl.BlockSpec((rows, wv), lambda c: (c, GV_OFF // wv)),
                  pl.BlockSpec((rows, wv), lambda c: (c, GG_OFF // wv)),
                  pl.BlockSpec((rows, wk), lambda c: (c, 0)),
                  pl.BlockSpec((1, GLA_DV), lambda c: (0, 0))],
        out_specs=pl.BlockSpec((rows, wv), lambda c: (c, 0)),
        scratch_shapes=[pltpu.VMEM((GLA_HEADS, GLA_DV, GLA_DK), F32)],
        compiler_params=_cparams(("arbitrary",)),
        name="gla",
    )(m, h_main, h_main, h_main, h_main, log_a, g_norm)


def _mixout_kernel(ret_ref, gla_ref, x_ref, w_ref, g_ref, b_ref, o_ref, wb_ref):
    nr = ret_ref.shape[1]

    @pl.when(pl.program_id(0) == 0)
    def _():
        wb_ref[...] = w_ref[...].astype(BF16)

    tm = x_ref.shape[0]
    parts = 2 if tm % 512 == 0 else 1
    rows = [slice(p * (tm // parts), (p + 1) * (tm // parts)) for p in range(parts)]
    mix = [_dot(ret_ref[r, :], wb_ref[:nr, :]) + _dot(gla_ref[r, :], wb_ref[nr:, :]) for r in rows]
    for r, m in zip(rows, mix):
        o_ref[r, :] = _layer_norm(DEEPNORM_ALPHA * x_ref[r, :] + m, g_ref[...], b_ref[...])


def _mixout_ln(ret, gla, x2d, w_b, g, b):
    s = x2d.shape[0]
    tm = min(s, 512)
    nr, ng = ret.shape[1], gla.shape[1]
    return pl.pallas_call(
        _mixout_kernel,
        out_shape=jax.ShapeDtypeStruct((s, D_MODEL), F32),
        grid=(s // tm,),
        in_specs=[pl.BlockSpec((tm, nr), lambda i: (i, 0)),
                  pl.BlockSpec((tm, ng), lambda i: (i, 0)),
                  pl.BlockSpec((tm, D_MODEL), lambda i: (i, 0)),
                  pl.BlockSpec((nr + ng, D_MODEL), lambda i: (0, 0), pipeline_mode=pl.Buffered(1)),
                  pl.BlockSpec((1, D_MODEL), lambda i: (0, 0)),
                  pl.BlockSpec((1, D_MODEL), lambda i: (0, 0))],
        out_specs=pl.BlockSpec((tm, D_MODEL), lambda i: (i, 0)),
        scratch_shapes=[pltpu.VMEM((nr + ng, D_MODEL), BF16)],
        compiler_params=_cparams(("arbitrary",)),
        name="mixout_ln1",
    )(ret, gla, x2d, w_b, g, b)


def _kv_kernel(mem_ref, wk_ref, wv_ref, k_ref, v_ref):
    m = mem_ref[...].astype(BF16)
    k_ref[...] = _dot(m, wk_ref[...].astype(BF16)).astype(k_ref.dtype)
    v_ref[...] = _dot(m, wv_ref[...].astype(BF16)).astype(v_ref.dtype)


def _mem_kv(mem_b, wk_b, wv_b):
    tn = 512
    return pl.pallas_call(
        _kv_kernel,
        out_shape=(jax.ShapeDtypeStruct((MEM_LEN, D_MODEL), BF16),
                   jax.ShapeDtypeStruct((MEM_LEN, D_MODEL), BF16)),
        grid=(D_MODEL // tn,),
        in_specs=[pl.BlockSpec((MEM_LEN, D_MODEL), lambda j: (0, 0)),
                  pl.BlockSpec((D_MODEL, tn), lambda j: (0, j)),
                  pl.BlockSpec((D_MODEL, tn), lambda j: (0, j))],
        out_specs=(pl.BlockSpec((MEM_LEN, tn), lambda j: (0, j)),
                   pl.BlockSpec((MEM_LEN, tn), lambda j: (0, j))),
        compiler_params=_cparams(("arbitrary",)),
        name="mem_kv",
    )(mem_b, wk_b, wv_b)


def _cross_kernel(h_ref, wq_ref, k_ref, v_ref, wo_ref, g_ref, b_ref, wr_ref, br_ref,
                  h2_ref, lg_ref):
    tm = h_ref.shape[0]
    parts = 2 if tm % 512 == 0 else 1
    pr = tm // parts
    rows = [slice(p * pr, (p + 1) * pr) for p in range(parts)]
    sls = [slice(hd * MEM_HEAD_DIM, (hd + 1) * MEM_HEAD_DIM) for hd in range(MEM_HEADS)]
    h1 = [h_ref[r, :] for r in rows]
    q = [_dot(h.astype(BF16), wq_ref[...]).astype(BF16) for h in h1]
    o = []
    for p in range(parts):
        outs = []
        for sl in sls:
            s = _dot_nt(q[p][:, sl], k_ref[:, sl]) * (MEM_HEAD_DIM ** -0.5)
            s = s - jnp.max(s, axis=-1, keepdims=True)
            e = jnp.exp(s)
            e = e / jnp.sum(e, axis=-1, keepdims=True)
            outs.append(_dot(e.astype(BF16), v_ref[:, sl]))
        o.append(jnp.concatenate(outs, axis=-1).astype(BF16))
    cross = [_dot(x, wo_ref[...]) for x in o]
    for p in range(parts):
        h2 = _layer_norm(DEEPNORM_ALPHA * h1[p] + cross[p], g_ref[...], b_ref[...])
        h2_ref[rows[p], :] = h2
        lg_ref[rows[p], :] = _dot(h2.astype(BF16), wr_ref[...]) + br_ref[...]


def _cross_attention(h1, wq_b, k, v, wo_b, g, b, w_route, b_route):
    s = h1.shape[0]
    tm = min(s, 512)
    const = lambda shape: pl.BlockSpec(shape, lambda i: (0, 0), pipeline_mode=pl.Buffered(1))
    return pl.pallas_call(
        _cross_kernel,
        out_shape=(jax.ShapeDtypeStruct((s, D_MODEL), F32),
                   jax.ShapeDtypeStruct((s, LANES), F32)),
        grid=(s // tm,),
        in_specs=[pl.BlockSpec((tm, D_MODEL), lambda i: (i, 0)),
                  const((D_MODEL, D_MODEL)),
                  const((MEM_LEN, D_MODEL)),
                  const((MEM_LEN, D_MODEL)),
                  const((D_MODEL, D_MODEL)),
                  const((1, D_MODEL)),
                  const((1, D_MODEL)),
                  const((D_MODEL, LANES)),
                  const((1, LANES))],
        out_specs=(pl.BlockSpec((tm, D_MODEL), lambda i: (i, 0)),
                   pl.BlockSpec((tm, LANES), lambda i: (i, 0))),
        compiler_params=_cparams(("arbitrary",)),
        name="cross_attn_ln2",
    )(h1, wq_b, k, v, wo_b, g, b, w_route, b_route)


def _route_kernel(lg_ref, slot_ref, slot_t_ref, gate_ref, runs_ref, plan_ref, tot_row, tot_col, gstart, rinfo):
    phase = pl.program_id(0)
    i = pl.program_id(1)
    tm = lg_ref.shape[0]
    lane = lax.broadcasted_iota(jnp.int32, (tm, LANES), 1)
    rows = pl.ds(pl.multiple_of(i * tm, tm), tm)

    @pl.when(phase == 0)
    def _():
        neg = -jnp.inf
        logits = lg_ref[...]
        gmask = lane < N_GROUPS
        gl = jnp.where(gmask, logits, neg)
        ge = jnp.exp(gl - jnp.max(gl, axis=-1, keepdims=True))
        pg = ge / jnp.sum(ge, axis=-1, keepdims=True)
        pg_sel = jnp.max(pg, axis=-1, keepdims=True)
        grp = jnp.min(jnp.where((pg == pg_sel) & gmask, lane, LANES), axis=-1, keepdims=True)

        fl_lane = lane - N_GROUPS
        fmask = (fl_lane >= 0) & (fl_lane < N_EXPERTS) & ((fl_lane >> 3) == grp)
        fl = jnp.where(fmask, logits, neg)
        fe = jnp.exp(fl - jnp.max(fl, axis=-1, keepdims=True))
        fp = fe / jnp.sum(fe, axis=-1, keepdims=True)
        p1 = jnp.max(fp, axis=-1, keepdims=True)
        i1 = jnp.min(jnp.where((fp == p1) & fmask, lane, LANES), axis=-1, keepdims=True)
        rest = fmask & (lane != i1)
        fp2 = jnp.where(rest, fp, -1.0)
        p2 = jnp.max(fp2, axis=-1, keepdims=True)
        i2 = jnp.min(jnp.where((fp2 == p2) & rest, lane, LANES), axis=-1, keepdims=True)
        psum = p1 + p2
        picks = ((i1 - N_GROUPS).astype(F32), (i2 - N_GROUPS).astype(F32),
                 pg_sel * p1 / psum, pg_sel * p2 / psum)
        info = jnp.zeros((tm, LANES), F32)
        for j, val in enumerate(picks):
            info = jnp.where(lane == j, val, info)
        rinfo[rows, :] = info

    info = rinfo[rows, :]
    oh1 = lane == info[:, 0:1].astype(jnp.int32)
    oh2 = lane == info[:, 1:2].astype(jnp.int32)
    gate1 = info[:, 2:3]
    gate2 = info[:, 3:4]
    oh = (jnp.where(oh1, 1.0, 0.0) + jnp.where(oh2, 1.0, 0.0)).astype(BF16)
    ones = jnp.ones((tm, LANES), BF16)

    def align_up(v, a):
        return jnp.floor((v + (a - 1.0)) * (1.0 / a)) * a

    run_row = align_up(_dot_tn(ones, oh)[0:8, :], RUN_ALIGN)
    rr = lax.broadcasted_iota(jnp.int32, (LANES, LANES), 0)
    cc = lax.broadcasted_iota(jnp.int32, (LANES, LANES), 1)
    strict_upper = jnp.where(rr < cc, 1.0, 0.0).astype(BF16)

    @pl.when((phase == 0) & (i == 0))
    def _():
        tot_row[...] = jnp.zeros_like(tot_row)
        tot_col[...] = jnp.zeros_like(tot_col)

    @pl.when(phase == 0)
    def _():
        tot_row[...] += run_row
        tot_col[...] += align_up(_dot_tn(oh, ones), RUN_ALIGN)

    @pl.when((phase == 1) & (i == 0))
    def _():
        nblk_row = align_up(tot_row[...], MOE_BLK) * (1.0 / MOE_BLK)
        nblk_col = align_up(tot_col[...], MOE_BLK) * (1.0 / MOE_BLK)
        lower_incl = jnp.where(cc <= rr, 1.0, 0.0).astype(BF16)
        base = _dot(nblk_row.astype(BF16), strict_upper) * float(MOE_BLK)
        ends = _dot(lower_incl, nblk_col.astype(BF16))
        expert_rows = rr < N_EXPERTS
        be = jnp.sum(jnp.where(expert_rows & (ends <= cc.astype(F32)), 1.0, 0.0), axis=0, keepdims=True)
        be = jnp.minimum(be, N_EXPERTS - 1.0)
        total = jnp.sum(jnp.where(expert_rows, nblk_col, 0.0), axis=0, keepdims=True)
        sub = lax.broadcasted_iota(jnp.int32, plan_ref.shape, 0)
        plan = jnp.where(sub == 0, jnp.broadcast_to(be, plan_ref.shape),
                         jnp.where(sub == 1, jnp.broadcast_to(total, plan_ref.shape),
                                   jnp.where(sub == 2, base + tot_row[...],
                                             jnp.where(sub == 3, nblk_row,
                                                       nblk_row * float(MOE_BLK) - tot_row[...]))))
        plan_ref[...] = plan.astype(jnp.int32)
        gstart[...] = base

    @pl.when(phase == 1)
    def _():
        off = _dot((run_row * (1.0 / RUN_ALIGN)).astype(BF16), strict_upper) * float(RUN_ALIGN)
        tr = lax.broadcasted_iota(jnp.int32, (tm, tm), 0)
        tc = lax.broadcasted_iota(jnp.int32, (tm, tm), 1)
        strict_lower = jnp.where(tc < tr, 1.0, 0.0).astype(BF16)
        slot = _dot(strict_lower, oh) + off[0:1, :]
        s1 = jnp.sum(jnp.where(oh1, slot, 0.0), axis=-1, keepdims=True)
        s2 = jnp.sum(jnp.where(oh2, slot, 0.0), axis=-1, keepdims=True)
        slots = jnp.where(lane == 0, s1, jnp.where(lane == 1, s2, 0.0))
        slot_ref[...] = slots.astype(jnp.int32)
        slot_t_ref[...] = jnp.transpose(slots)[0:8, :].astype(jnp.int32)
        gate_ref[...] = jnp.where(lane == 0, gate1, jnp.where(lane == 1, gate2, 0.0))
        sub = lax.broadcasted_iota(jnp.int32, runs_ref.shape, 0)
        runs = jnp.where(sub == 0, gstart[...], jnp.where(sub == 1, off, run_row))
        runs_ref[...] = runs.astype(jnp.int32)
        gstart[...] += run_row


def _route(logits):
    t = logits.shape[0]
    tm = min(t, TILE_TOK)
    nt = t // tm
    step = lambda p, i: (i * p, 0)
    return pl.pallas_call(
        _route_kernel,
        out_shape=(jax.ShapeDtypeStruct((t, LANES), jnp.int32),
                   jax.ShapeDtypeStruct((nt * 8, tm), jnp.int32),
                   jax.ShapeDtypeStruct((t, LANES), F32),
                   jax.ShapeDtypeStruct((nt * 8, LANES), jnp.int32),
                   jax.ShapeDtypeStruct((8, LANES), jnp.int32)),
        grid=(2, nt),
        in_specs=[pl.BlockSpec((tm, LANES), lambda p, i: (i, 0))],
        out_specs=(pl.BlockSpec((tm, LANES), step),
                   pl.BlockSpec((8, tm), step),
                   pl.BlockSpec((tm, LANES), step),
                   pl.BlockSpec((8, LANES), step),
                   pl.BlockSpec((8, LANES), lambda p, i: (0, 0))),
        scratch_shapes=[pltpu.VMEM((8, LANES), F32),
                        pltpu.VMEM((LANES, LANES), F32),
                        pltpu.VMEM((8, LANES), F32),
                        pltpu.VMEM((t, LANES), F32)],
        compiler_params=_cparams(("arbitrary", "arbitrary")),
        name="moe_route",
    )(logits)


def _run_copies(rg_ref, ro_ref, rn_ref, tile, make_copy):
    def each(action):
        def body(e, carry):
            j = tile * N_EXPERTS + e
            n = rn_ref[j]

            @pl.when(n > 0)
            def _():
                action(make_copy(pl.multiple_of(rg_ref[j], RUN_ALIGN), pl.multiple_of(ro_ref[j], RUN_ALIGN),
                                 pl.multiple_of(n, RUN_ALIGN)))
            return carry
        lax.fori_loop(0, N_EXPERTS, body, 0)
    return each


def _dispatch_kernel(rg_ref, ro_ref, rn_ref, zrow_ref, zlen_ref, nu_ref, x_ref, slot_t_ref, xb_ref,
                     sorted_ref, zero_ref, sems, zsem):
    i = pl.program_id(0)
    n = pl.num_programs(0)
    tm = x_ref.shape[0]
    nb = xb_ref.shape[0] // MOE_BLK
    buf = i % 2

    def zero_copy(row, size):
        return pltpu.make_async_copy(
            zero_ref.at[pl.ds(0, size), :], xb_ref.at[pl.ds(pl.multiple_of(row, RUN_ALIGN), size), :], zsem)

    def pad_copy(e):
        return zero_copy(zrow_ref[e], pl.multiple_of(zlen_ref[e], RUN_ALIGN))

    def runs_of(tile, slot):
        def make_copy(g, o, size):
            return pltpu.make_async_copy(sorted_ref.at[slot, pl.ds(o, size), :],
                                         xb_ref.at[pl.ds(g, size), :], sems.at[slot])
        return _run_copies(rg_ref, ro_ref, rn_ref, tile, make_copy)

    @pl.when(i == 0)
    def _():
        zero_ref[...] = jnp.zeros_like(zero_ref)

        def issue_zero(e, carry):
            @pl.when(zlen_ref[e] > 0)
            def _():
                pad_copy(e).start()
            return carry
        lax.fori_loop(0, N_EXPERTS, issue_zero, 0)
        lax.fori_loop(nu_ref[0], nb, lambda b, c: (zero_copy(b * MOE_BLK, MOE_BLK).start(), c)[1], 0)

    @pl.when(i >= 2)
    def _():
        runs_of(i - 2, buf)(lambda c: c.wait())

    x = x_ref[...].astype(BF16)
    s1 = slot_t_ref[0:1, :]
    s2 = slot_t_ref[1:2, :]
    for rc in range(TILE_SLOTS // SORT_ROWS):
        r = lax.broadcasted_iota(jnp.int32, (SORT_ROWS, tm), 0) + rc * SORT_ROWS
        perm = jnp.where((r == s1) | (r == s2), 1.0, 0.0).astype(BF16)
        sorted_ref[buf, rc * SORT_ROWS:(rc + 1) * SORT_ROWS, :] = _dot(perm, x).astype(BF16)

    runs_of(i, buf)(lambda c: c.start())

    @pl.when(i == n - 1)
    def _():
        @pl.when(i >= 1)
        def _():
            runs_of(i - 1, 1 - buf)(lambda c: c.wait())
        runs_of(i, buf)(lambda c: c.wait())

        def wait_zero(e, carry):
            @pl.when(zlen_ref[e] > 0)
            def _():
                pad_copy(e).wait()
            return carry
        lax.fori_loop(0, N_EXPERTS, wait_zero, 0)
        lax.fori_loop(nu_ref[0], nb, lambda b, c: (zero_copy(0, MOE_BLK).wait(), c)[1], 0)


def _dispatch(rg, ro, rn, zrow, zlen, n_used, h2, slot_t, n_rows):
    t = h2.shape[0]
    tm = min(t, TILE_TOK)
    grid_spec = pltpu.PrefetchScalarGridSpec(
        num_scalar_prefetch=6,
        grid=(t // tm,),
        in_specs=[pl.BlockSpec((tm, D_MODEL), lambda i, *_: (i, 0)),
                  pl.BlockSpec((8, tm), lambda i, *_: (i, 0))],
        out_specs=pl.BlockSpec(memory_space=pl.ANY),
        scratch_shapes=[pltpu.VMEM((2, TILE_SLOTS, D_MODEL), BF16),
                        pltpu.VMEM((MOE_BLK, D_MODEL), BF16),
                        pltpu.SemaphoreType.DMA((2,)),
                        pltpu.SemaphoreType.DMA(())],
    )
    return pl.pallas_call(
        _dispatch_kernel,
        out_shape=jax.ShapeDtypeStruct((n_rows, D_MODEL), BF16),
        grid_spec=grid_spec,
        compiler_params=_cparams(("arbitrary",)),
        name="moe_dispatch",
    )(rg, ro, rn, zrow, zlen, n_used, h2, slot_t)


def _expert_kernel(be_ref, nu_ref, nblk_ref, x_ref, wg_ref, wu_ref, wd_ref, y_ref,
                   wg_f, wu_f, wd_f, wg_b, wu_b, wd_b, sems, cur_ref):
    step = pl.program_id(0)
    n_used = nu_ref[0]

    def weight_copies(expert, slot):
        half = EXPERT_FF // 2
        return [pltpu.make_async_copy(wg_ref.at[expert], wg_f.at[slot], sems.at[slot]),
                pltpu.make_async_copy(wu_ref.at[expert], wu_f.at[slot], sems.at[slot]),
                pltpu.make_async_copy(wd_ref.at[expert, :half], wd_f.at[slot, :half], sems.at[slot]),
                pltpu.make_async_copy(wd_ref.at[expert, half:], wd_f.at[slot, half:], sems.at[slot])]

    def start_weights(expert, slot):
        for c in weight_copies(expert, slot):
            c.start(priority=1)

    def after(blk):
        return blk + nblk_ref[be_ref[blk]]

    @pl.when(step == 0)
    def _():
        cur_ref[0] = 0
        start_weights(be_ref[0], 0)

        @pl.when(after(0) < n_used)
        def _():
            start_weights(be_ref[after(0)], 1)

    for sub in range(EXPERT_STEP_BLKS):
        b = step * EXPERT_STEP_BLKS + sub
        used = b < n_used
        e = be_ref[b]
        prev = be_ref[jnp.maximum(b - 1, 0)]
        fresh = used & ((b == 0) | (e != prev))
        rows = slice(sub * MOE_BLK, (sub + 1) * MOE_BLK)

        @pl.when(fresh)
        def _():
            slot = cur_ref[0]
            for c in weight_copies(e, slot):
                c.wait()
            wg_b[...] = wg_f[slot].astype(BF16)
            wu_b[...] = wu_f[slot].astype(BF16)
            wd_b[...] = wd_f[slot].astype(BF16)
            nxt = after(b)

            @pl.when(nxt < n_used)
            def _():
                nxt2 = after(nxt)

                @pl.when(nxt2 < n_used)
                def _():
                    start_weights(be_ref[nxt2], slot)
            cur_ref[0] = 1 - slot

        @pl.when(used)
        def _():
            x = x_ref[rows, :]
            hid = _silu(_dot(x, wg_b[...])) * _dot(x, wu_b[...])
            y_ref[rows, :] = _dot(hid.astype(BF16), wd_b[...]).astype(y_ref.dtype)

        @pl.when(jnp.logical_not(used))
        def _():
            y_ref[rows, :] = jnp.zeros((MOE_BLK, D_MODEL), y_ref.dtype)


def _experts(block_e, n_used, nblk, xb, w_gate, w_up, w_down):
    step_rows = EXPERT_STEP_BLKS * MOE_BLK
    any_space = pl.BlockSpec(memory_space=pl.ANY)
    last_step = lambda nu: (nu[0] - 1) // EXPERT_STEP_BLKS
    grid_spec = pltpu.PrefetchScalarGridSpec(
        num_scalar_prefetch=3,
        grid=(xb.shape[0] // step_rows,),
        in_specs=[pl.BlockSpec((step_rows, D_MODEL), lambda s, be, nu, nk: (jnp.minimum(s, last_step(nu)), 0)),
                  any_space, any_space, any_space],
        out_specs=pl.BlockSpec((step_rows, D_MODEL), lambda s, be, nu, nk: (s, 0)),
        scratch_shapes=[pltpu.VMEM((2, D_MODEL, EXPERT_FF), F32),
                        pltpu.VMEM((2, D_MODEL, EXPERT_FF), F32),
                        pltpu.VMEM((2, EXPERT_FF, D_MODEL), F32),
                        pltpu.VMEM((D_MODEL, EXPERT_FF), BF16),
                        pltpu.VMEM((D_MODEL, EXPERT_FF), BF16),
                        pltpu.VMEM((EXPERT_FF, D_MODEL), BF16),
                        pltpu.SemaphoreType.DMA((2,)),
                        pltpu.SMEM((1,), jnp.int32)],
    )
    return pl.pallas_call(
        _expert_kernel,
        out_shape=jax.ShapeDtypeStruct(xb.shape, BF16),
        grid_spec=grid_spec,
        compiler_params=_cparams(("arbitrary",)),
        name="moe_experts",
    )(block_e, n_used, nblk, xb, w_gate, w_up, w_down)


def _combine_kernel(rg_ref, ro_ref, rn_ref, yb_ref, h2_ref, slot_ref, gate_ref, g_ref, b_ref, o_ref,
                    ybuf, sems):
    i = pl.program_id(0)
    n = pl.num_programs(0)
    tm = h2_ref.shape[0]
    buf = i % 2

    def runs_of(tile, slot):
        def make_copy(g, o, size):
            return pltpu.make_async_copy(yb_ref.at[pl.ds(g, size), :],
                                         ybuf.at[slot, pl.ds(o, size), :], sems.at[slot])
        return _run_copies(rg_ref, ro_ref, rn_ref, tile, make_copy)

    @pl.when(i == 0)
    def _():
        ybuf[...] = jnp.zeros_like(ybuf)
        runs_of(0, 0)(lambda c: c.start())

    @pl.when(i + 1 < n)
    def _():
        runs_of(i + 1, 1 - buf)(lambda c: c.start())

    runs_of(i, buf)(lambda c: c.wait())

    lane = lax.broadcasted_iota(jnp.int32, (tm, TILE_SLOTS), 1)
    slots = slot_ref[...]
    gate = gate_ref[...]
    y = ybuf[buf]
    sel = jnp.where(lane == slots[:, 0:1], gate[:, 0:1],
                    jnp.where(lane == slots[:, 1:2], gate[:, 1:2], 0.0)).astype(BF16)
    ffn = _dot(sel, y)
    o_ref[...] = _layer_norm(DEEPNORM_ALPHA * h2_ref[...] + ffn, g_ref[...], b_ref[...])


def _combine_ln(rg, ro, rn, yb, h2, slots, gate, g, b):
    t = h2.shape[0]
    tm = min(t, TILE_TOK)
    grid_spec = pltpu.PrefetchScalarGridSpec(
        num_scalar_prefetch=3,
        grid=(t // tm,),
        in_specs=[pl.BlockSpec(memory_space=pl.ANY),
                  pl.BlockSpec((tm, D_MODEL), lambda i, *_: (i, 0)),
                  pl.BlockSpec((tm, LANES), lambda i, *_: (i, 0)),
                  pl.BlockSpec((tm, LANES), lambda i, *_: (i, 0)),
                  pl.BlockSpec((1, D_MODEL), lambda i, *_: (0, 0)),
                  pl.BlockSpec((1, D_MODEL), lambda i, *_: (0, 0))],
        out_specs=pl.BlockSpec((tm, D_MODEL), lambda i, *_: (i, 0)),
        scratch_shapes=[pltpu.VMEM((2, TILE_SLOTS, D_MODEL), BF16),
                        pltpu.SemaphoreType.DMA((2,))],
    )
    return pl.pallas_call(
        _combine_kernel,
        out_shape=jax.ShapeDtypeStruct((t, D_MODEL), F32),
        grid_spec=grid_spec,
        compiler_params=_cparams(("arbitrary",)),
        name="moe_combine_ln3",
    )(rg, ro, rn, yb, h2, slots, gate, g, b)


def _mixer(x2d, positions, w_in, w_gla_a2, b_gla_a, g_gla_norm):
    s = x2d.shape[0]
    half = RET_DK // 2
    inv_freq = (ROPE_BASE ** (-jnp.arange(half, dtype=F32) / half)).reshape(1, half)
    cos, sin = _rope_table(positions.reshape(s, 1), inv_freq)
    w_in_t = jnp.swapaxes(w_in, 0, 1)
    w_lr_t = jnp.pad(w_in_t[GLR_OFF:].astype(BF16), ((0, LANES - GLA_LOWRANK), (0, 0)))
    w_a2 = jnp.pad(w_gla_a2.astype(BF16), ((0, LANES - GLA_LOWRANK), (0, 0)))
    h_main, log_a = _proj_in(x2d, w_in_t, w_lr_t, w_a2, b_gla_a.reshape(1, -1))
    log_gamma = jnp.log1p(-jnp.exp2(-5.0 - jnp.arange(RET_HEADS, dtype=F32)))
    ret = _retention(h_main, cos, sin, log_gamma)
    gla = _gla(h_main, log_a, g_gla_norm.reshape(1, -1))
    return ret, gla


def _moe(h2, logits, w_gate, w_up, w_down, g, b):
    t = h2.shape[0]
    nt = t // min(t, TILE_TOK)
    slots, slot_t, gate, runs, plan = _route(logits)
    runs = runs.reshape(nt, 8, LANES)[:, :3, :N_EXPERTS]
    rg, ro, rn = (runs[:, j, :].reshape(-1) for j in range(3))
    max_rows = 2 * t + nt * N_EXPERTS * (RUN_ALIGN - 1) + N_EXPERTS * (MOE_BLK - 1)
    nb = -(-max_rows // (MOE_BLK * EXPERT_STEP_BLKS)) * EXPERT_STEP_BLKS
    block_e, n_used = plan[0, :nb], plan[1, :1]
    pad_row, nblk, pad_len = (plan[j, :N_EXPERTS] for j in (2, 3, 4))
    xb = _dispatch(rg, ro, rn, pad_row, pad_len, n_used, h2, slot_t, nb * MOE_BLK)
    yb = _experts(block_e, n_used, nblk, xb, w_gate, w_up, w_down)
    return _combine_ln(rg, ro, rn, yb, h2, slots, gate, g, b)


def kernel(x, mem, positions, w_in, w_gla_a2, b_gla_a, g_gla_norm, w_mix_out, ln1_g, ln1_b, w_mq, w_mk, w_mv, w_mo, ln2_g, ln2_b, w_route_group, b_route_group, w_route_expert, b_route_expert, w_exp_gate, w_exp_up, w_exp_down, ln3_g, ln3_b):
    bsz, s, d = x.shape
    assert bsz == 1 and d == D_MODEL
    x2d = x.reshape(s, d)
    row = lambda v: v.reshape(1, -1)

    ret, gla = _mixer(x2d, positions, w_in[0], w_gla_a2[0], b_gla_a[0], g_gla_norm[0])
    h1 = _mixout_ln(ret, gla, x2d, w_mix_out[0], row(ln1_g[0]), row(ln1_b[0]))

    k, v = _mem_kv(mem[0], w_mk[0], w_mv[0])
    n_route = N_GROUPS + N_EXPERTS
    w_route = jnp.pad(jnp.concatenate([w_route_group[0], w_route_expert[0]], axis=1).astype(BF16),
                      ((0, 0), (0, LANES - n_route)))
    b_route = jnp.pad(jnp.concatenate([b_route_group[0], b_route_expert[0].reshape(-1)]),
                      (0, LANES - n_route)).reshape(1, LANES)
    h2, logits = _cross_attention(h1, w_mq[0].astype(BF16), k, v, w_mo[0].astype(BF16),
                                  row(ln2_g[0]), row(ln2_b[0]), w_route, b_route)

    out = _moe(h2, logits, w_exp_gate[0], w_exp_up[0], w_exp_down[0],
               row(ln3_g[0]), row(ln3_b[0]))
    return out.reshape(bsz, s, d)
```

```python
import math

import jax
import jax.numpy as jnp
from jax import lax
from jax.experimental import pallas as pl
from jax.experimental.pallas import tpu as pltpu

F32 = jnp.float32
BF16 = jnp.bfloat16

D_MODEL = 2048
MEM_LEN = 256
RET_HEADS = 4
RET_DK = 256
RET_DV = 256
GLA_HEADS = 4
GLA_DK = 128
GLA_DV = 256
GLA_LOWRANK = 16
GLA_TAU = 16.0
ROPE_BASE = 10000.0
MEM_HEADS = 4
MEM_HEAD_DIM = D_MODEL // MEM_HEADS
N_GROUPS = 4
EXPERTS_PER_GROUP = 8
N_EXPERTS = N_GROUPS * EXPERTS_PER_GROUP
EXPERT_FF = 512
LN_EPS = 1e-5
DEPTH = 1
DEEPNORM_ALPHA = (2 * DEPTH) ** 0.25

RQ_OFF, RK_OFF, RV_OFF, RG_OFF = 0, 1024, 2048, 3072
GQ_OFF, GK_OFF, GV_OFF, GG_OFF, GLR_OFF = 4096, 4608, 5120, 6144, 7168
IN_MAIN = 7168

LANES = 128
SUBLANES = 8
RET_CHUNK = 256
RET_STEP_CHUNKS = 2
GLA_CHUNK = 128
GLA_LEVELS = 7
GLA_STEP_CHUNKS = 4
MOE_BLK = 256
EXPERT_STEP_BLKS = 4
TILE_TOK = 512
RUN_ALIGN = 16
TILE_SLOTS = 2 * TILE_TOK + N_EXPERTS * RUN_ALIGN
SORT_ROWS = 256
PROJ_K_CHUNK = 512
VMEM_LIMIT = 56 * 1024 * 1024


def _cparams(sem):
    return pltpu.CompilerParams(dimension_semantics=sem, vmem_limit_bytes=VMEM_LIMIT)


def _layer_norm(y, g, b):
    mu = jnp.mean(y, axis=-1, keepdims=True)
    d = y - mu
    var = jnp.mean(d * d, axis=-1, keepdims=True)
    return d * lax.rsqrt(var + LN_EPS) * g + b


def _silu(x):
    return x / (1.0 + jnp.exp(-x))


def _dot(a, b):
    return jnp.dot(a, b, preferred_element_type=F32)


def _dot_nt(a, b):
    return lax.dot_general(a, b, (((1,), (1,)), ((), ())), preferred_element_type=F32)


def _dot_tn(a, b):
    return lax.dot_general(a, b, (((0,), (0,)), ((), ())), preferred_element_type=F32)


def _rope_kernel(pos_ref, invf_ref, cos_ref, sin_ref):
    ang = pos_ref[...].astype(F32) * invf_ref[...]
    cos_ref[...] = jnp.cos(ang)
    sin_ref[...] = jnp.sin(ang)


def _rope_table(pos_col, inv_freq):
    s = pos_col.shape[0]
    tm = min(s, 1024)
    half = inv_freq.shape[1]
    return pl.pallas_call(
        _rope_kernel,
        out_shape=(jax.ShapeDtypeStruct((s, half), F32), jax.ShapeDtypeStruct((s, half), F32)),
        grid=(s // tm,),
        in_specs=[pl.BlockSpec((tm, 1), lambda i: (i, 0)),
                  pl.BlockSpec((1, half), lambda i: (0, 0))],
        out_specs=(pl.BlockSpec((tm, half), lambda i: (i, 0)),
                   pl.BlockSpec((tm, half), lambda i: (i, 0))),
        compiler_params=_cparams(("arbitrary",)),
        name="rope_table",
    )(pos_col, inv_freq)


def _proj_in_kernel(x_ref, wt_ref, wlr_ref, wa2_ref, ba_ref, o_ref, la_ref, xb_ref):
    @pl.when(pl.program_id(1) == 0)
    def _():
        xb_ref[...] = x_ref[...].astype(BF16)
        glr = _dot_nt(xb_ref[...], wlr_ref[...])
        z = _dot(glr.astype(BF16), wa2_ref[...]) + ba_ref[...]
        la_ref[...] = (jnp.minimum(z, 0.0) - jnp.log(1.0 + jnp.exp(-jnp.abs(z)))) / GLA_TAU

    acc = None
    for c in range(D_MODEL // PROJ_K_CHUNK):
        ck = slice(c * PROJ_K_CHUNK, (c + 1) * PROJ_K_CHUNK)
        part = _dot_nt(xb_ref[:, ck], wt_ref[:, ck].astype(BF16))
        acc = part if acc is None else acc + part
    o_ref[...] = acc.astype(o_ref.dtype)


def _proj_in(x2d, w_in_t, w_lr_t, w_a2, b_a):
    s = x2d.shape[0]
    tm = min(s, 1024)
    tn = 1024
    n = GLA_HEADS * GLA_DK
    return pl.pallas_call(
        _proj_in_kernel,
        out_shape=(jax.ShapeDtypeStruct((s, IN_MAIN), BF16), jax.ShapeDtypeStruct((s, n), F32)),
        grid=(s // tm, IN_MAIN // tn),
        in_specs=[pl.BlockSpec((tm, D_MODEL), lambda i, j: (i, 0)),
                  pl.BlockSpec((tn, D_MODEL), lambda i, j: (j, 0)),
                  pl.BlockSpec((LANES, D_MODEL), lambda i, j: (0, 0)),
                  pl.BlockSpec((LANES, n), lambda i, j: (0, 0)),
                  pl.BlockSpec((1, n), lambda i, j: (0, 0))],
        out_specs=(pl.BlockSpec((tm, tn), lambda i, j: (i, j)),
                   pl.BlockSpec((tm, n), lambda i, j: (i, 0))),
        scratch_shapes=[pltpu.VMEM((tm, D_MODEL), BF16)],
        compiler_params=_cparams(("arbitrary", "arbitrary")),
        name="proj_in",
    )(x2d, w_in_t, w_lr_t, w_a2, b_a)


def _rotary(t, cos, sin):
    half = t.shape[-1] // 2
    t1, t2 = t[:, :half], t[:, half:]
    return jnp.concatenate([t1 * cos - t2 * sin, t1 * sin + t2 * cos], axis=-1)


def _retention_kernel(lg_ref, q_ref, k_ref, v_ref, g_ref, cos_ref, sin_ref, o_ref, state_ref, intra_ref, dq_ref, dk_ref):
    c = pl.program_id(0)
    C = intra_ref.shape[1]
    nsub = q_ref.shape[0] // C

    @pl.when(c == 0)
    def _():
        state_ref[...] = jnp.zeros_like(state_ref)
        ri = lax.broadcasted_iota(jnp.int32, (C, C), 0)
        ci = lax.broadcasted_iota(jnp.int32, (C, C), 1)
        rel = jnp.maximum(ri - ci, 0).astype(F32)
        n = lax.broadcasted_iota(jnp.int32, (C, RET_DK), 0).astype(F32)
        for h in range(RET_HEADS):
            lg = lg_ref[h]
            intra_ref[h] = jnp.where(ri >= ci, jnp.exp(lg * rel), 0.0)
            dq_ref[h] = jnp.exp(lg * (n + 1.0))
            dk_ref[h] = jnp.exp(lg * (C - 1.0 - n))

    state = [state_ref[h] for h in range(RET_HEADS)]
    for u in range(nsub):
        rows = slice(u * C, (u + 1) * C)
        cos = cos_ref[rows, :]
        sin = sin_ref[rows, :]
        for h in range(RET_HEADS):
            sl = slice(h * RET_DK, (h + 1) * RET_DK)
            q = _rotary(q_ref[rows, sl].astype(F32), cos, sin)
            k = _rotary(k_ref[rows, sl].astype(F32), cos, sin) * (RET_DK ** -0.5)
            v = v_ref[rows, sl].astype(BF16)
            decay_chunk = jnp.exp(lg_ref[h] * C)

            scores = _dot_nt(q.astype(BF16), k.astype(BF16)) * intra_ref[h]
            o = _dot(scores.astype(BF16), v) + _dot((q * dq_ref[h]).astype(BF16), state[h].astype(BF16))
            state[h] = decay_chunk * state[h] + _dot_tn((k * dk_ref[h]).astype(BF16), v)

            mu = jnp.mean(o, axis=-1, keepdims=True)
            d = o - mu
            var = jnp.mean(d * d, axis=-1, keepdims=True)
            o = d * lax.rsqrt(var + LN_EPS)
            o_ref[rows, sl] = (_silu(g_ref[rows, sl].astype(F32)) * o).astype(o_ref.dtype)
    for h in range(RET_HEADS):
        state_ref[h] = state[h]


def _retention(h_main, cos, sin, log_gamma):
    s = h_main.shape[0]
    C = min(RET_CHUNK, s)
    rows = min(s, RET_STEP_CHUNKS * C)
    w = RET_HEADS * RET_DK
    col = lambda off: (lambda c, lg: (c, off // w))
    grid_spec = pltpu.PrefetchScalarGridSpec(
        num_scalar_prefetch=1,
        grid=(s // rows,),
        in_specs=[pl.BlockSpec((rows, w), col(RQ_OFF)),
                  pl.BlockSpec((rows, w), col(RK_OFF)),
                  pl.BlockSpec((rows, w), col(RV_OFF)),
                  pl.BlockSpec((rows, w), col(RG_OFF)),
                  pl.BlockSpec((rows, RET_DK // 2), lambda c, lg: (c, 0)),
                  pl.BlockSpec((rows, RET_DK // 2), lambda c, lg: (c, 0))],
        out_specs=pl.BlockSpec((rows, RET_HEADS * RET_DV), lambda c, lg: (c, 0)),
        scratch_shapes=[pltpu.VMEM((RET_HEADS, RET_DK, RET_DV), F32),
                        pltpu.VMEM((RET_HEADS, C, C), F32),
                        pltpu.VMEM((RET_HEADS, C, RET_DK), F32),
                        pltpu.VMEM((RET_HEADS, C, RET_DK), F32)],
    )
    return pl.pallas_call(
        _retention_kernel,
        out_shape=jax.ShapeDtypeStruct((s, RET_HEADS * RET_DV), BF16),
        grid_spec=grid_spec,
        compiler_params=_cparams(("arbitrary",)),
        name="retention",
    )(log_gamma, h_main, h_main, h_main, h_main, cos, sin)


def _gla_decay_matrix(C):
    import numpy as np
    levels = int(math.log2(C))
    r = np.arange(C)[:, None]
    t = np.arange(C)[None, :]
    mats = []
    for l in range(levels):
        blk = C >> l
        half = blk // 2
        m = (r // blk) * blk + half - 1
        qside = (r % blk) >= half
        mats.append(np.where(qside, (t > m) & (t <= r), (t > r) & (t <= m)))
    mats.append(t <= r)
    mats.append(t > r)
    return np.concatenate(mats, axis=0).astype(np.float32)


def _gla_kernel(m_ref, q_ref, k_ref, v_ref, g_ref, la_ref, gn_ref, o_ref, state_ref):
    c = pl.program_id(0)
    C = GLA_CHUNK
    levels = GLA_LEVELS
    nsub = q_ref.shape[0] // C

    @pl.when(c == 0)
    def _():
        state_ref[...] = jnp.zeros_like(state_ref)

    m = m_ref[...]
    ri = lax.broadcasted_iota(jnp.int32, (C, C), 0)
    ci = lax.broadcasted_iota(jnp.int32, (C, C), 1)
    xor = jnp.where(ri > ci, ri ^ ci, 0)
    row = lax.broadcasted_iota(jnp.int32, (C, 1), 0)

    heads = range(GLA_HEADS)
    ks = [slice(h * GLA_DK, (h + 1) * GLA_DK) for h in heads]
    vs = [slice(h * GLA_DV, (h + 1) * GLA_DV) for h in heads]
    rs = [slice(u * C, (u + 1) * C) for u in range(nsub)]
    items = [(u, h) for u in range(nsub) for h in heads]

    expo, q, k, v, scores = {}, {}, {}, {}, {}
    for u, h in items:
        la = la_ref[rs[u], ks[h]]
        la_hi = la.astype(BF16)
        la_lo = (la - la_hi.astype(F32)).astype(BF16)
        expo[u, h] = jnp.exp(_dot(m, la_hi) + _dot(m, la_lo))
    for u, h in items:
        q[u, h] = q_ref[rs[u], ks[h]].astype(F32) * (GLA_DK ** -0.5)
        k[u, h] = k_ref[rs[u], ks[h]].astype(F32)
        v[u, h] = v_ref[rs[u], vs[h]].astype(BF16)
        scores[u, h] = jnp.where(ri == ci, _dot_nt(q[u, h].astype(BF16), k[u, h].astype(BF16)), 0.0)
    for l in range(levels):
        half = C >> (l + 1)
        shift = int(math.log2(half))
        qside = (row & half) != 0
        keep = (xor >> shift) == 1
        for it in items:
            x = (jnp.where(qside, q[it], k[it]) * expo[it][l * C:(l + 1) * C]).astype(BF16)
            scores[it] = scores[it] + jnp.where(keep, _dot_nt(x, x), 0.0)

    state = [state_ref[h] for h in heads]
    outs = {}
    for u, h in items:
        e_b = expo[u, h][levels * C:(levels + 1) * C]
        e_rev = expo[u, h][(levels + 1) * C:(levels + 2) * C]
        e_last = e_b[C - 1:C, :]
        outs[u, h] = (_dot(scores[u, h].astype(BF16), v[u, h])
                      + _dot_nt((q[u, h] * e_b).astype(BF16), state[h].astype(BF16)))
        state[h] = state[h] * e_last + _dot_tn(v[u, h], (k[u, h] * e_rev).astype(BF16))
    for h in heads:
        state_ref[h] = state[h]

    for u, h in items:
        o = outs[u, h]
        o = o * lax.rsqrt(jnp.mean(o * o, axis=-1, keepdims=True) + LN_EPS) * gn_ref[...]
        o_ref[rs[u], vs[h]] = (_silu(g_ref[rs[u], vs[h]].astype(F32)) * o).astype(o_ref.dtype)


def _gla(h_main, log_a, g_norm):
    s = h_main.shape[0]
    C = GLA_CHUNK
    m = jnp.asarray(_gla_decay_matrix(C), dtype=BF16)
    nrow = m.shape[0]
    rows = min(s, GLA_STEP_CHUNKS * C)
    wk = GLA_HEADS * GLA_DK
    wv = GLA_HEADS * GLA_DV
    return pl.pallas_call(
        _gla_kernel,
        out_shape=jax.ShapeDtypeStruct((s, wv), BF16),
        grid=(s // rows,),
        in_specs=[pl.BlockSpec((nrow, C), lambda c: (0, 0)),
                  pl.BlockSpec((rows, wk), lambda c: (c, GQ_OFF // wk)),
                  pl.BlockSpec((rows, wk), lambda c: (c, GK_OFF // wk)),
                  pl.BlockSpec((rows, wv), lambda c: (c, GV_OFF // wv)),
                  pl.BlockSpec((rows, wv), lambda c: (c, GG_OFF // wv)),
                  pl.BlockSpec((rows, wk), lambda c: (c, 0)),
                  pl.BlockSpec((1, GLA_DV), lambda c: (0, 0))],
        out_specs=pl.BlockSpec((rows, wv), lambda c: (c, 0)),
        scratch_shapes=[pltpu.VMEM((GLA_HEADS, GLA_DV, GLA_DK), F32)],
        compiler_params=_cparams(("arbitrary",)),
        name="gla",
    )(m, h_main, h_main, h_main, h_main, log_a, g_norm)


def _mixout_kernel(ret_ref, gla_ref, x_ref, w_ref, g_ref, b_ref, o_ref, wb_ref):
    nr = ret_ref.shape[1]

    @pl.when(pl.program_id(0) == 0)
    def _():
        wb_ref[...] = w_ref[...].astype(BF16)

    tm = x_ref.shape[0]
    parts = 2 if tm % 512 == 0 else 1
    rows = [slice(p * (tm // parts), (p + 1) * (tm // parts)) for p in range(parts)]
    mix = [_dot(ret_ref[r, :], wb_ref[:nr, :]) + _dot(gla_ref[r, :], wb_ref[nr:, :]) for r in rows]
    for r, m in zip(rows, mix):
        o_ref[r, :] = _layer_norm(DEEPNORM_ALPHA * x_ref[r, :] + m, g_ref[...], b_ref[...])


def _mixout_ln(ret, gla, x2d, w_b, g, b):
    s = x2d.shape[0]
    tm = min(s, 512)
    nr, ng = ret.shape[1], gla.shape[1]
    return pl.pallas_call(
        _mixout_kernel,
        out_shape=jax.ShapeDtypeStruct((s, D_MODEL), F32),
        grid=(s // tm,),
        in_specs=[pl.BlockSpec((tm, nr), lambda i: (i, 0)),
                  pl.BlockSpec((tm, ng), lambda i: (i, 0)),
                  pl.BlockSpec((tm, D_MODEL), lambda i: (i, 0)),
                  pl.BlockSpec((nr + ng, D_MODEL), lambda i: (0, 0), pipeline_mode=pl.Buffered(1)),
                  pl.BlockSpec((1, D_MODEL), lambda i: (0, 0)),
                  pl.BlockSpec((1, D_MODEL), lambda i: (0, 0))],
        out_specs=pl.BlockSpec((tm, D_MODEL), lambda i: (i, 0)),
        scratch_shapes=[pltpu.VMEM((nr + ng, D_MODEL), BF16)],
        compiler_params=_cparams(("arbitrary",)),
        name="mixout_ln1",
    )(ret, gla, x2d, w_b, g, b)


def _kv_kernel(mem_ref, wk_ref, wv_ref, k_ref, v_ref):
    m = mem_ref[...].astype(BF16)
    k_ref[...] = _dot(m, wk_ref[...].astype(BF16)).astype(k_ref.dtype)
    v_ref[...] = _dot(m, wv_ref[...].astype(BF16)).astype(v_ref.dtype)


def _mem_kv(mem_b, wk_b, wv_b):
    tn = 512
    return pl.pallas_call(
        _kv_kernel,
        out_shape=(jax.ShapeDtypeStruct((MEM_LEN, D_MODEL), BF16),
                   jax.ShapeDtypeStruct((MEM_LEN, D_MODEL), BF16)),
        grid=(D_MODEL // tn,),
        in_specs=[pl.BlockSpec((MEM_LEN, D_MODEL), lambda j: (0, 0)),
                  pl.BlockSpec((D_MODEL, tn), lambda j: (0, j)),
                  pl.BlockSpec((D_MODEL, tn), lambda j: (0, j))],
        out_specs=(pl.BlockSpec((MEM_LEN, tn), lambda j: (0, j)),
                   pl.BlockSpec((MEM_LEN, tn), lambda j: (0, j))),
        compiler_params=_cparams(("arbitrary",)),
        name="mem_kv",
    )(mem_b, wk_b, wv_b)


def _cross_kernel(h_ref, wq_ref, k_ref, v_ref, wo_ref, g_ref, b_ref, wr_ref, br_ref,
                  h2_ref, lg_ref):
    tm = h_ref.shape[0]
    parts = 2 if tm % 512 == 0 else 1
    pr = tm // parts
    rows = [slice(p * pr, (p + 1) * pr) for p in range(parts)]
    sls = [slice(hd * MEM_HEAD_DIM, (hd + 1) * MEM_HEAD_DIM) for hd in range(MEM_HEADS)]
    h1 = [h_ref[r, :] for r in rows]
    q = [_dot(h.astype(BF16), wq_ref[...]).astype(BF16) for h in h1]
    o = []
    for p in range(parts):
        outs = []
        for sl in sls:
            s = _dot_nt(q[p][:, sl], k_ref[:, sl]) * (MEM_HEAD_DIM ** -0.5)
            s = s - jnp.max(s, axis=-1, keepdims=True)
            e = jnp.exp(s)
            e = e / jnp.sum(e, axis=-1, keepdims=True)
            outs.append(_dot(e.astype(BF16), v_ref[:, sl]))
        o.append(jnp.concatenate(outs, axis=-1).astype(BF16))
    cross = [_dot(x, wo_ref[...]) for x in o]
    for p in range(parts):
        h2 = _layer_norm(DEEPNORM_ALPHA * h1[p] + cross[p], g_ref[...], b_ref[...])
        h2_ref[rows[p], :] = h2
        lg_ref[rows[p], :] = _dot(h2.astype(BF16), wr_ref[...]) + br_ref[...]


def _cross_attention(h1, wq_b, k, v, wo_b, g, b, w_route, b_route):
    s = h1.shape[0]
    tm = min(s, 512)
    const = lambda shape: pl.BlockSpec(shape, lambda i: (0, 0), pipeline_mode=pl.Buffered(1))
    return pl.pallas_call(
        _cross_kernel,
        out_shape=(jax.ShapeDtypeStruct((s, D_MODEL), F32),
                   jax.ShapeDtypeStruct((s, LANES), F32)),
        grid=(s // tm,),
        in_specs=[pl.BlockSpec((tm, D_MODEL), lambda i: (i, 0)),
                  const((D_MODEL, D_MODEL)),
                  const((MEM_LEN, D_MODEL)),
                  const((MEM_LEN, D_MODEL)),
                  const((D_MODEL, D_MODEL)),
                  const((1, D_MODEL)),
                  const((1, D_MODEL)),
                  const((D_MODEL, LANES)),
                  const((1, LANES))],
        out_specs=(pl.BlockSpec((tm, D_MODEL), lambda i: (i, 0)),
                   pl.BlockSpec((tm, LANES), lambda i: (i, 0))),
        compiler_params=_cparams(("arbitrary",)),
        name="cross_attn_ln2",
    )(h1, wq_b, k, v, wo_b, g, b, w_route, b_route)


def _route_kernel(lg_ref, slot_ref, slot_t_ref, gate_ref, runs_ref, plan_ref, tot_row, tot_col, gstart, rinfo):
    phase = pl.program_id(0)
    i = pl.program_id(1)
    tm = lg_ref.shape[0]
    lane = lax.broadcasted_iota(jnp.int32, (tm, LANES), 1)
    rows = pl.ds(pl.multiple_of(i * tm, tm), tm)

    @pl.when(phase == 0)
    def _():
        neg = -jnp.inf
        logits = lg_ref[...]
        gmask = lane < N_GROUPS
        gl = jnp.where(gmask, logits, neg)
        ge = jnp.exp(gl - jnp.max(gl, axis=-1, keepdims=True))
        pg = ge / jnp.sum(ge, axis=-1, keepdims=True)
        pg_sel = jnp.max(pg, axis=-1, keepdims=True)
        grp = jnp.min(jnp.where((pg == pg_sel) & gmask, lane, LANES), axis=-1, keepdims=True)

        fl_lane = lane - N_GROUPS
        fmask = (fl_lane >= 0) & (fl_lane < N_EXPERTS) & ((fl_lane >> 3) == grp)
        fl = jnp.where(fmask, logits, neg)
        fe = jnp.exp(fl - jnp.max(fl, axis=-1, keepdims=True))
        fp = fe / jnp.sum(fe, axis=-1, keepdims=True)
        p1 = jnp.max(fp, axis=-1, keepdims=True)
        i1 = jnp.min(jnp.where((fp == p1) & fmask, lane, LANES), axis=-1, keepdims=True)
        rest = fmask & (lane != i1)
        fp2 = jnp.where(rest, fp, -1.0)
        p2 = jnp.max(fp2, axis=-1, keepdims=True)
        i2 = jnp.min(jnp.where((fp2 == p2) & rest, lane, LANES), axis=-1, keepdims=True)
        psum = p1 + p2
        picks = ((i1 - N_GROUPS).astype(F32), (i2 - N_GROUPS).astype(F32),
                 pg_sel * p1 / psum, pg_sel * p2 / psum)
        info = jnp.zeros((tm, LANES), F32)
        for j, val in enumerate(picks):
            info = jnp.where(lane == j, val, info)
        rinfo[rows, :] = info

    info = rinfo[rows, :]
    oh1 = lane == info[:, 0:1].astype(jnp.int32)
    oh2 = lane == info[:, 1:2].astype(jnp.int32)
    gate1 = info[:, 2:3]
    gate2 = info[:, 3:4]
    oh = (jnp.where(oh1, 1.0, 0.0) + jnp.where(oh2, 1.0, 0.0)).astype(BF16)
    ones = jnp.ones((tm, LANES), BF16)

    def align_up(v, a):
        return jnp.floor((v + (a - 1.0)) * (1.0 / a)) * a

    run_row = align_up(_dot_tn(ones, oh)[0:SUBLANES, :], RUN_ALIGN)
    rr = lax.broadcasted_iota(jnp.int32, (LANES, LANES), 0)
    cc = lax.broadcasted_iota(jnp.int32, (LANES, LANES), 1)
    strict_upper = jnp.where(rr < cc, 1.0, 0.0).astype(BF16)

    @pl.when((phase == 0) & (i == 0))
    def _():
        tot_row[...] = jnp.zeros_like(tot_row)
        tot_col[...] = jnp.zeros_like(tot_col)

    @pl.when(phase == 0)
    def _():
        tot_row[...] += run_row
        tot_col[...] += align_up(_dot_tn(oh, ones), RUN_ALIGN)

    @pl.when((phase == 1) & (i == 0))
    def _():
        nblk_row = align_up(tot_row[...], MOE_BLK) * (1.0 / MOE_BLK)
        nblk_col = align_up(tot_col[...], MOE_BLK) * (1.0 / MOE_BLK)
        lower_incl = jnp.where(cc <= rr, 1.0, 0.0).astype(BF16)
        base = _dot(nblk_row.astype(BF16), strict_upper) * float(MOE_BLK)
        ends = _dot(lower_incl, nblk_col.astype(BF16))
        expert_rows = rr < N_EXPERTS
        be = jnp.sum(jnp.where(expert_rows & (ends <= cc.astype(F32)), 1.0, 0.0), axis=0, keepdims=True)
        be = jnp.minimum(be, N_EXPERTS - 1.0)
        total = jnp.sum(jnp.where(expert_rows, nblk_col, 0.0), axis=0, keepdims=True)
        sub = lax.broadcasted_iota(jnp.int32, plan_ref.shape, 0)
        plan = jnp.where(sub == 0, jnp.broadcast_to(be, plan_ref.shape),
                         jnp.where(sub == 1, jnp.broadcast_to(total, plan_ref.shape),
                                   jnp.where(sub == 2, base + tot_row[...],
                                             jnp.where(sub == 3, nblk_row,
                                                       nblk_row * float(MOE_BLK) - tot_row[...]))))
        plan_ref[...] = plan.astype(jnp.int32)
        gstart[...] = base

    @pl.when(phase == 1)
    def _():
        off = _dot((run_row * (1.0 / RUN_ALIGN)).astype(BF16), strict_upper) * float(RUN_ALIGN)
        tr = lax.broadcasted_iota(jnp.int32, (tm, tm), 0)
        tc = lax.broadcasted_iota(jnp.int32, (tm, tm), 1)
        strict_lower = jnp.where(tc < tr, 1.0, 0.0).astype(BF16)
        slot = _dot(strict_lower, oh) + off[0:1, :]
        s1 = jnp.sum(jnp.where(oh1, slot, 0.0), axis=-1, keepdims=True)
        s2 = jnp.sum(jnp.where(oh2, slot, 0.0), axis=-1, keepdims=True)
        slots = jnp.where(lane == 0, s1, jnp.where(lane == 1, s2, 0.0))
        slot_ref[...] = slots.astype(jnp.int32)
        slot_t_ref[...] = jnp.transpose(slots)[0:SUBLANES, :].astype(jnp.int32)
        gate_ref[...] = jnp.where(lane == 0, gate1, jnp.where(lane == 1, gate2, 0.0))
        sub = lax.broadcasted_iota(jnp.int32, runs_ref.shape, 0)
        runs = jnp.where(sub == 0, gstart[...], jnp.where(sub == 1, off, run_row))
        runs_ref[...] = runs.astype(jnp.int32)
        gstart[...] += run_row


def _route(logits):
    t = logits.shape[0]
    tm = min(t, TILE_TOK)
    nt = t // tm
    step = lambda p, i: (i * p, 0)
    return pl.pallas_call(
        _route_kernel,
        out_shape=(jax.ShapeDtypeStruct((t, LANES), jnp.int32),
                   jax.ShapeDtypeStruct((nt * SUBLANES, tm), jnp.int32),
                   jax.ShapeDtypeStruct((t, LANES), F32),
                   jax.ShapeDtypeStruct((nt * SUBLANES, LANES), jnp.int32),
                   jax.ShapeDtypeStruct((SUBLANES, LANES), jnp.int32)),
        grid=(2, nt),
        in_specs=[pl.BlockSpec((tm, LANES), lambda p, i: (i, 0))],
        out_specs=(pl.BlockSpec((tm, LANES), step),
                   pl.BlockSpec((SUBLANES, tm), step),
                   pl.BlockSpec((tm, LANES), step),
                   pl.BlockSpec((SUBLANES, LANES), step),
                   pl.BlockSpec((SUBLANES, LANES), lambda p, i: (0, 0))),
        scratch_shapes=[pltpu.VMEM((SUBLANES, LANES), F32),
                        pltpu.VMEM((LANES, LANES), F32),
                        pltpu.VMEM((SUBLANES, LANES), F32),
                        pltpu.VMEM((t, LANES), F32)],
        compiler_params=_cparams(("arbitrary", "arbitrary")),
        name="moe_route",
    )(logits)


def _run_copies(rg_ref, ro_ref, rn_ref, tile, make_copy):
    def each(action):
        def body(e, carry):
            j = tile * N_EXPERTS + e
            n = rn_ref[j]

            @pl.when(n > 0)
            def _():
                action(make_copy(pl.multiple_of(rg_ref[j], RUN_ALIGN), pl.multiple_of(ro_ref[j], RUN_ALIGN),
                                 pl.multiple_of(n, RUN_ALIGN)))
            return carry
        lax.fori_loop(0, N_EXPERTS, body, 0)
    return each


def _dispatch_kernel(rg_ref, ro_ref, rn_ref, zrow_ref, zlen_ref, nu_ref, x_ref, slot_t_ref, xb_ref,
                     sorted_ref, zero_ref, sems, zsem):
    i = pl.program_id(0)
    n = pl.num_programs(0)
    tm = x_ref.shape[0]
    nb = xb_ref.shape[0] // MOE_BLK
    buf = i % 2

    def zero_copy(row, size):
        return pltpu.make_async_copy(
            zero_ref.at[pl.ds(0, size), :], xb_ref.at[pl.ds(pl.multiple_of(row, RUN_ALIGN), size), :], zsem)

    def pad_copy(e):
        return zero_copy(zrow_ref[e], pl.multiple_of(zlen_ref[e], RUN_ALIGN))

    def runs_of(tile, slot):
        def make_copy(g, o, size):
            return pltpu.make_async_copy(sorted_ref.at[slot, pl.ds(o, size), :],
                                         xb_ref.at[pl.ds(g, size), :], sems.at[slot])
        return _run_copies(rg_ref, ro_ref, rn_ref, tile, make_copy)

    @pl.when(i == 0)
    def _():
        zero_ref[...] = jnp.zeros_like(zero_ref)

        def issue_zero(e, carry):
            @pl.when(zlen_ref[e] > 0)
            def _():
                pad_copy(e).start()
            return carry
        lax.fori_loop(0, N_EXPERTS, issue_zero, 0)
        lax.fori_loop(nu_ref[0], nb, lambda b, c: (zero_copy(b * MOE_BLK, MOE_BLK).start(), c)[1], 0)

    @pl.when(i >= 2)
    def _():
        runs_of(i - 2, buf)(lambda c: c.wait())

    x = x_ref[...].astype(BF16)
    s1 = slot_t_ref[0:1, :]
    s2 = slot_t_ref[1:2, :]
    for rc in range(TILE_SLOTS // SORT_ROWS):
        r = lax.broadcasted_iota(jnp.int32, (SORT_ROWS, tm), 0) + rc * SORT_ROWS
        perm = jnp.where((r == s1) | (r == s2), 1.0, 0.0).astype(BF16)
        sorted_ref[buf, rc * SORT_ROWS:(rc + 1) * SORT_ROWS, :] = _dot(perm, x).astype(BF16)

    runs_of(i, buf)(lambda c: c.start())

    @pl.when(i == n - 1)
    def _():
        @pl.when(i >= 1)
        def _():
            runs_of(i - 1, 1 - buf)(lambda c: c.wait())
        runs_of(i, buf)(lambda c: c.wait())

        def wait_zero(e, carry):
            @pl.when(zlen_ref[e] > 0)
            def _():
                pad_copy(e).wait()
            return carry
        lax.fori_loop(0, N_EXPERTS, wait_zero, 0)
        lax.fori_loop(nu_ref[0], nb, lambda b, c: (zero_copy(0, MOE_BLK).wait(), c)[1], 0)


def _dispatch(rg, ro, rn, zrow, zlen, n_used, h2, slot_t, n_rows):
    t = h2.shape[0]
    tm = min(t, TILE_TOK)
    grid_spec = pltpu.PrefetchScalarGridSpec(
        num_scalar_prefetch=6,
        grid=(t // tm,),
        in_specs=[pl.BlockSpec((tm, D_MODEL), lambda i, *_: (i, 0)),
                  pl.BlockSpec((SUBLANES, tm), lambda i, *_: (i, 0))],
        out_specs=pl.BlockSpec(memory_space=pl.ANY),
        scratch_shapes=[pltpu.VMEM((2, TILE_SLOTS, D_MODEL), BF16),
                        pltpu.VMEM((MOE_BLK, D_MODEL), BF16),
                        pltpu.SemaphoreType.DMA((2,)),
                        pltpu.SemaphoreType.DMA(())],
    )
    return pl.pallas_call(
        _dispatch_kernel,
        out_shape=jax.ShapeDtypeStruct((n_rows, D_MODEL), BF16),
        grid_spec=grid_spec,
        compiler_params=_cparams(("arbitrary",)),
        name="moe_dispatch",
    )(rg, ro, rn, zrow, zlen, n_used, h2, slot_t)


def _expert_kernel(be_ref, nu_ref, nblk_ref, x_ref, wg_ref, wu_ref, wd_ref, y_ref,
                   wg_f, wu_f, wd_f, wg_b, wu_b, wd_b, sems, cur_ref):
    step = pl.program_id(0)
    n_used = nu_ref[0]

    def weight_copies(expert, slot):
        half = EXPERT_FF // 2
        return [pltpu.make_async_copy(wg_ref.at[expert], wg_f.at[slot], sems.at[slot]),
                pltpu.make_async_copy(wu_ref.at[expert], wu_f.at[slot], sems.at[slot]),
                pltpu.make_async_copy(wd_ref.at[expert, :half], wd_f.at[slot, :half], sems.at[slot]),
                pltpu.make_async_copy(wd_ref.at[expert, half:], wd_f.at[slot, half:], sems.at[slot])]

    def start_weights(expert, slot):
        for c in weight_copies(expert, slot):
            c.start()

    def after(blk):
        return blk + nblk_ref[be_ref[blk]]

    @pl.when(step == 0)
    def _():
        cur_ref[0] = 0
        start_weights(be_ref[0], 0)

        @pl.when(after(0) < n_used)
        def _():
            start_weights(be_ref[after(0)], 1)

    for sub in range(EXPERT_STEP_BLKS):
        b = step * EXPERT_STEP_BLKS + sub
        used = b < n_used
        e = be_ref[b]
        prev = be_ref[jnp.maximum(b - 1, 0)]
        fresh = used & ((b == 0) | (e != prev))
        rows = slice(sub * MOE_BLK, (sub + 1) * MOE_BLK)

        @pl.when(fresh)
        def _():
            slot = cur_ref[0]
            for c in weight_copies(e, slot):
                c.wait()
            wg_b[...] = wg_f[slot].astype(BF16)
            wu_b[...] = wu_f[slot].astype(BF16)
            wd_b[...] = wd_f[slot].astype(BF16)
            nxt = after(b)

            @pl.when(nxt < n_used)
            def _():
                nxt2 = after(nxt)

                @pl.when(nxt2 < n_used)
                def _():
                    start_weights(be_ref[nxt2], slot)
            cur_ref[0] = 1 - slot

        @pl.when(used)
        def _():
            x = x_ref[rows, :]
            hid = _silu(_dot(x, wg_b[...])) * _dot(x, wu_b[...])
            y_ref[rows, :] = _dot(hid.astype(BF16), wd_b[...]).astype(y_ref.dtype)

        @pl.when(jnp.logical_not(used))
        def _():
            y_ref[rows, :] = jnp.zeros((MOE_BLK, D_MODEL), y_ref.dtype)


def _experts(block_e, n_used, nblk, xb, w_gate, w_up, w_down):
    step_rows = EXPERT_STEP_BLKS * MOE_BLK
    any_space = pl.BlockSpec(memory_space=pl.ANY)
    last_step = lambda nu: (nu[0] - 1) // EXPERT_STEP_BLKS
    grid_spec = pltpu.PrefetchScalarGridSpec(
        num_scalar_prefetch=3,
        grid=(xb.shape[0] // step_rows,),
        in_specs=[pl.BlockSpec((step_rows, D_MODEL), lambda s, be, nu, nk: (jnp.minimum(s, last_step(nu)), 0)),
                  any_space, any_space, any_space],
        out_specs=pl.BlockSpec((step_rows, D_MODEL), lambda s, be, nu, nk: (s, 0)),
        scratch_shapes=[pltpu.VMEM((2, D_MODEL, EXPERT_FF), F32),
                        pltpu.VMEM((2, D_MODEL, EXPERT_FF), F32),
                        pltpu.VMEM((2, EXPERT_FF, D_MODEL), F32),
                        pltpu.VMEM((D_MODEL, EXPERT_FF), BF16),
                        pltpu.VMEM((D_MODEL, EXPERT_FF), BF16),
                        pltpu.VMEM((EXPERT_FF, D_MODEL), BF16),
                        pltpu.SemaphoreType.DMA((2,)),
                        pltpu.SMEM((1,), jnp.int32)],
    )
    return pl.pallas_call(
        _expert_kernel,
        out_shape=jax.ShapeDtypeStruct(xb.shape, BF16),
        grid_spec=grid_spec,
        compiler_params=_cparams(("arbitrary",)),
        name="moe_experts",
    )(block_e, n_used, nblk, xb, w_gate, w_up, w_down)


def _combine_kernel(rg_ref, ro_ref, rn_ref, yb_ref, h2_ref, slot_ref, gate_ref, g_ref, b_ref, o_ref,
                    ybuf, sems):
    i = pl.program_id(0)
    n = pl.num_programs(0)
    tm = h2_ref.shape[0]
    buf = i % 2

    def runs_of(tile, slot):
        def make_copy(g, o, size):
            return pltpu.make_async_copy(yb_ref.at[pl.ds(g, size), :],
                                         ybuf.at[slot, pl.ds(o, size), :], sems.at[slot])
        return _run_copies(rg_ref, ro_ref, rn_ref, tile, make_copy)

    @pl.when(i == 0)
    def _():
        ybuf[...] = jnp.zeros_like(ybuf)
        runs_of(0, 0)(lambda c: c.start())

    @pl.when(i + 1 < n)
    def _():
        runs_of(i + 1, 1 - buf)(lambda c: c.start())

    runs_of(i, buf)(lambda c: c.wait())

    lane = lax.broadcasted_iota(jnp.int32, (tm, TILE_SLOTS), 1)
    slots = slot_ref[...]
    gate = gate_ref[...]
    y = ybuf[buf]
    sel = jnp.where(lane == slots[:, 0:1], gate[:, 0:1],
                    jnp.where(lane == slots[:, 1:2], gate[:, 1:2], 0.0)).astype(BF16)
    parts = 2 if tm % 512 == 0 else 1
    rows = [slice(p * (tm // parts), (p + 1) * (tm // parts)) for p in range(parts)]
    ffn = [_dot(sel[r, :], y) for r in rows]
    for r, f in zip(rows, ffn):
        o_ref[r, :] = _layer_norm(DEEPNORM_ALPHA * h2_ref[r, :] + f, g_ref[...], b_ref[...])


def _combine_ln(rg, ro, rn, yb, h2, slots, gate, g, b):
    t = h2.shape[0]
    tm = min(t, TILE_TOK)
    grid_spec = pltpu.PrefetchScalarGridSpec(
        num_scalar_prefetch=3,
        grid=(t // tm,),
        in_specs=[pl.BlockSpec(memory_space=pl.ANY),
                  pl.BlockSpec((tm, D_MODEL), lambda i, *_: (i, 0)),
                  pl.BlockSpec((tm, LANES), lambda i, *_: (i, 0)),
                  pl.BlockSpec((tm, LANES), lambda i, *_: (i, 0)),
                  pl.BlockSpec((1, D_MODEL), lambda i, *_: (0, 0)),
                  pl.BlockSpec((1, D_MODEL), lambda i, *_: (0, 0))],
        out_specs=pl.BlockSpec((tm, D_MODEL), lambda i, *_: (i, 0)),
        scratch_shapes=[pltpu.VMEM((2, TILE_SLOTS, D_MODEL), BF16),
                        pltpu.SemaphoreType.DMA((2,))],
    )
    return pl.pallas_call(
        _combine_kernel,
        out_shape=jax.ShapeDtypeStruct((t, D_MODEL), F32),
        grid_spec=grid_spec,
        compiler_params=_cparams(("arbitrary",)),
        name="moe_combine_ln3",
    )(rg, ro, rn, yb, h2, slots, gate, g, b)


def _mixer(x2d, positions, w_in, w_gla_a2, b_gla_a, g_gla_norm):
    s = x2d.shape[0]
    half = RET_DK // 2
    inv_freq = (ROPE_BASE ** (-jnp.arange(half, dtype=F32) / half)).reshape(1, half)
    cos, sin = _rope_table(positions.reshape(s, 1), inv_freq)
    w_in_t = jnp.swapaxes(w_in, 0, 1)
    w_lr_t = jnp.pad(w_in_t[GLR_OFF:].astype(BF16), ((0, LANES - GLA_LOWRANK), (0, 0)))
    w_a2 = jnp.pad(w_gla_a2.astype(BF16), ((0, LANES - GLA_LOWRANK), (0, 0)))
    h_main, log_a = _proj_in(x2d, w_in_t, w_lr_t, w_a2, b_gla_a.reshape(1, -1))
    log_gamma = jnp.log1p(-jnp.exp2(-5.0 - jnp.arange(RET_HEADS, dtype=F32)))
    ret = _retention(h_main, cos, sin, log_gamma)
    gla = _gla(h_main, log_a, g_gla_norm.reshape(1, -1))
    return ret, gla


def _moe(h2, logits, w_gate, w_up, w_down, g, b):
    t = h2.shape[0]
    nt = t // min(t, TILE_TOK)
    slots, slot_t, gate, runs, plan = _route(logits)
    runs = runs.reshape(nt, SUBLANES, LANES)[:, :3, :N_EXPERTS]
    rg, ro, rn = (runs[:, j, :].reshape(-1) for j in range(3))
    max_rows = 2 * t + nt * N_EXPERTS * (RUN_ALIGN - 1) + N_EXPERTS * (MOE_BLK - 1)
    nb = -(-max_rows // (MOE_BLK * EXPERT_STEP_BLKS)) * EXPERT_STEP_BLKS
    block_e, n_used = plan[0, :nb], plan[1, :1]
    pad_row, nblk, pad_len = (plan[j, :N_EXPERTS] for j in (2, 3, 4))
    xb = _dispatch(rg, ro, rn, pad_row, pad_len, n_used, h2, slot_t, nb * MOE_BLK)
    yb = _experts(block_e, n_used, nblk, xb, w_gate, w_up, w_down)
    return _combine_ln(rg, ro, rn, yb, h2, slots, gate, g, b)


def kernel(x, mem, positions, w_in, w_gla_a2, b_gla_a, g_gla_norm, w_mix_out, ln1_g, ln1_b, w_mq, w_mk, w_mv, w_mo, ln2_g, ln2_b, w_route_group, b_route_group, w_route_expert, b_route_expert, w_exp_gate, w_exp_up, w_exp_down, ln3_g, ln3_b):
    bsz, s, d = x.shape
    assert bsz == 1 and d == D_MODEL
    x2d = x.reshape(s, d)
    row = lambda v: v.reshape(1, -1)

    ret, gla = _mixer(x2d, positions, w_in[0], w_gla_a2[0], b_gla_a[0], g_gla_norm[0])
    h1 = _mixout_ln(ret, gla, x2d, w_mix_out[0], row(ln1_g[0]), row(ln1_b[0]))

    k, v = _mem_kv(mem[0], w_mk[0], w_mv[0])
    n_route = N_GROUPS + N_EXPERTS
    w_route = jnp.pad(jnp.concatenate([w_route_group[0], w_route_expert[0]], axis=1).astype(BF16),
                      ((0, 0), (0, LANES - n_route)))
    b_route = jnp.pad(jnp.concatenate([b_route_group[0], b_route_expert[0].reshape(-1)]),
                      (0, LANES - n_route)).reshape(1, LANES)
    h2, logits = _cross_attention(h1, w_mq[0].astype(BF16), k, v, w_mo[0].astype(BF16),
                                  row(ln2_g[0]), row(ln2_b[0]), w_route, b_route)

    out = _moe(h2, logits, w_exp_gate[0], w_exp_up[0], w_exp_down[0],
               row(ln3_g[0]), row(ln3_b[0]))
    return out.reshape(bsz, s, d)
```

```python
import math

import jax
import jax.numpy as jnp
from jax import lax
from jax.experimental import pallas as pl
from jax.experimental.pallas import tpu as pltpu

F32 = jnp.float32
BF16 = jnp.bfloat16

D_MODEL = 2048
MEM_LEN = 256
RET_HEADS = 4
RET_DK = 256
RET_DV = 256
GLA_HEADS = 4
GLA_DK = 128
GLA_DV = 256
GLA_LOWRANK = 16
GLA_TAU = 16.0
ROPE_BASE = 10000.0
MEM_HEADS = 4
MEM_HEAD_DIM = D_MODEL // MEM_HEADS
N_GROUPS = 4
EXPERTS_PER_GROUP = 8
N_EXPERTS = N_GROUPS * EXPERTS_PER_GROUP
EXPERT_FF = 512
LN_EPS = 1e-5
DEPTH = 1
DEEPNORM_ALPHA = (2 * DEPTH) ** 0.25

RQ_OFF, RK_OFF, RV_OFF, RG_OFF = 0, 1024, 2048, 3072
GQ_OFF, GK_OFF, GV_OFF, GG_OFF, GLR_OFF = 4096, 4608, 5120, 6144, 7168
IN_MAIN = 7168

LANES = 128
SUBLANES = 8
RET_CHUNK = 256
RET_STEP_CHUNKS = 4
GLA_CHUNK = 128
GLA_LEVELS = 7
GLA_STEP_CHUNKS = 8
MOE_BLK = 256
EXPERT_STEP_BLKS = 4
TILE_TOK = 512
RUN_ALIGN = 16
TILE_SLOTS = 2 * TILE_TOK + N_EXPERTS * RUN_ALIGN
SORT_ROWS = 256
PROJ_K_CHUNK = 512
VMEM_LIMIT = 56 * 1024 * 1024


def _cparams(sem):
    return pltpu.CompilerParams(dimension_semantics=sem, vmem_limit_bytes=VMEM_LIMIT)


def _layer_norm(y, g, b):
    mu = jnp.mean(y, axis=-1, keepdims=True)
    d = y - mu
    var = jnp.mean(d * d, axis=-1, keepdims=True)
    return d * lax.rsqrt(var + LN_EPS) * g + b


def _silu(x):
    return x / (1.0 + jnp.exp(-x))


def _dot(a, b):
    return jnp.dot(a, b, preferred_element_type=F32)


def _dot_nt(a, b):
    return lax.dot_general(a, b, (((1,), (1,)), ((), ())), preferred_element_type=F32)


def _dot_tn(a, b):
    return lax.dot_general(a, b, (((0,), (0,)), ((), ())), preferred_element_type=F32)


def _rope_kernel(pos_ref, invf_ref, cos_ref, sin_ref):
    ang = pos_ref[...].astype(F32) * invf_ref[...]
    cos_ref[...] = jnp.cos(ang)
    sin_ref[...] = jnp.sin(ang)


def _rope_table(pos_col, inv_freq):
    s = pos_col.shape[0]
    tm = min(s, 1024)
    half = inv_freq.shape[1]
    return pl.pallas_call(
        _rope_kernel,
        out_shape=(jax.ShapeDtypeStruct((s, half), F32), jax.ShapeDtypeStruct((s, half), F32)),
        grid=(s // tm,),
        in_specs=[pl.BlockSpec((tm, 1), lambda i: (i, 0)),
                  pl.BlockSpec((1, half), lambda i: (0, 0))],
        out_specs=(pl.BlockSpec((tm, half), lambda i: (i, 0)),
                   pl.BlockSpec((tm, half), lambda i: (i, 0))),
        compiler_params=_cparams(("arbitrary",)),
        name="rope_table",
    )(pos_col, inv_freq)


def _proj_in_kernel(x_ref, wt_ref, wlr_ref, wa2_ref, ba_ref, o_ref, la_ref, xb_ref):
    @pl.when(pl.program_id(1) == 0)
    def _():
        xb_ref[...] = x_ref[...].astype(BF16)
        glr = _dot_nt(xb_ref[...], wlr_ref[...])
        z = _dot(glr.astype(BF16), wa2_ref[...]) + ba_ref[...]
        la_ref[...] = (jnp.minimum(z, 0.0) - jnp.log(1.0 + jnp.exp(-jnp.abs(z)))) / GLA_TAU

    acc = None
    for c in range(D_MODEL // PROJ_K_CHUNK):
        ck = slice(c * PROJ_K_CHUNK, (c + 1) * PROJ_K_CHUNK)
        part = _dot_nt(xb_ref[:, ck], wt_ref[:, ck].astype(BF16))
        acc = part if acc is None else acc + part
    o_ref[...] = acc.astype(o_ref.dtype)


def _proj_in(x2d, w_in_t, w_lr_t, w_a2, b_a):
    s = x2d.shape[0]
    tm = min(s, 1024)
    tn = 1024
    n = GLA_HEADS * GLA_DK
    return pl.pallas_call(
        _proj_in_kernel,
        out_shape=(jax.ShapeDtypeStruct((s, IN_MAIN), BF16), jax.ShapeDtypeStruct((s, n), F32)),
        grid=(s // tm, IN_MAIN // tn),
        in_specs=[pl.BlockSpec((tm, D_MODEL), lambda i, j: (i, 0)),
                  pl.BlockSpec((tn, D_MODEL), lambda i, j: (j, 0)),
                  pl.BlockSpec((LANES, D_MODEL), lambda i, j: (0, 0)),
                  pl.BlockSpec((LANES, n), lambda i, j: (0, 0)),
                  pl.BlockSpec((1, n), lambda i, j: (0, 0))],
        out_specs=(pl.BlockSpec((tm, tn), lambda i, j: (i, j)),
                   pl.BlockSpec((tm, n), lambda i, j: (i, 0))),
        scratch_shapes=[pltpu.VMEM((tm, D_MODEL), BF16)],
        compiler_params=_cparams(("arbitrary", "arbitrary")),
        name="proj_in",
    )(x2d, w_in_t, w_lr_t, w_a2, b_a)


def _rotary(t, cos, sin):
    half = t.shape[-1] // 2
    t1, t2 = t[:, :half], t[:, half:]
    return jnp.concatenate([t1 * cos - t2 * sin, t1 * sin + t2 * cos], axis=-1)


def _retention_kernel(lg_ref, q_ref, k_ref, v_ref, g_ref, cos_ref, sin_ref, o_ref, state_ref, intra_ref, dq_ref, dk_ref):
    c = pl.program_id(0)
    C = intra_ref.shape[1]
    nsub = q_ref.shape[0] // C

    @pl.when(c == 0)
    def _():
        state_ref[...] = jnp.zeros_like(state_ref)
        ri = lax.broadcasted_iota(jnp.int32, (C, C), 0)
        ci = lax.broadcasted_iota(jnp.int32, (C, C), 1)
        rel = jnp.maximum(ri - ci, 0).astype(F32)
        n = lax.broadcasted_iota(jnp.int32, (C, RET_DK), 0).astype(F32)
        for h in range(RET_HEADS):
            lg = lg_ref[h]
            intra_ref[h] = jnp.where(ri >= ci, jnp.exp(lg * rel), 0.0)
            dq_ref[h] = jnp.exp(lg * (n + 1.0))
            dk_ref[h] = jnp.exp(lg * (C - 1.0 - n))

    state = [state_ref[h] for h in range(RET_HEADS)]
    for u in range(nsub):
        rows = slice(u * C, (u + 1) * C)
        cos = cos_ref[rows, :]
        sin = sin_ref[rows, :]
        for h in range(RET_HEADS):
            sl = slice(h * RET_DK, (h + 1) * RET_DK)
            q = _rotary(q_ref[rows, sl].astype(F32), cos, sin)
            k = _rotary(k_ref[rows, sl].astype(F32), cos, sin) * (RET_DK ** -0.5)
            v = v_ref[rows, sl].astype(BF16)
            decay_chunk = jnp.exp(lg_ref[h] * C)

            scores = _dot_nt(q.astype(BF16), k.astype(BF16)) * intra_ref[h]
            o = _dot(scores.astype(BF16), v) + _dot((q * dq_ref[h]).astype(BF16), state[h].astype(BF16))
            state[h] = decay_chunk * state[h] + _dot_tn((k * dk_ref[h]).astype(BF16), v)

            mu = jnp.mean(o, axis=-1, keepdims=True)
            d = o - mu
            var = jnp.mean(d * d, axis=-1, keepdims=True)
            o = d * lax.rsqrt(var + LN_EPS)
            o_ref[rows, sl] = (_silu(g_ref[rows, sl].astype(F32)) * o).astype(o_ref.dtype)
    for h in range(RET_HEADS):
        state_ref[h] = state[h]


def _retention(h_main, cos, sin, log_gamma):
    s = h_main.shape[0]
    C = min(RET_CHUNK, s)
    rows = min(s, RET_STEP_CHUNKS * C)
    w = RET_HEADS * RET_DK
    col = lambda off: (lambda c, lg: (c, off // w))
    grid_spec = pltpu.PrefetchScalarGridSpec(
        num_scalar_prefetch=1,
        grid=(s // rows,),
        in_specs=[pl.BlockSpec((rows, w), col(RQ_OFF)),
                  pl.BlockSpec((rows, w), col(RK_OFF)),
                  pl.BlockSpec((rows, w), col(RV_OFF)),
                  pl.BlockSpec((rows, w), col(RG_OFF)),
                  pl.BlockSpec((rows, RET_DK // 2), lambda c, lg: (c, 0)),
                  pl.BlockSpec((rows, RET_DK // 2), lambda c, lg: (c, 0))],
        out_specs=pl.BlockSpec((rows, RET_HEADS * RET_DV), lambda c, lg: (c, 0)),
        scratch_shapes=[pltpu.VMEM((RET_HEADS, RET_DK, RET_DV), F32),
                        pltpu.VMEM((RET_HEADS, C, C), F32),
                        pltpu.VMEM((RET_HEADS, C, RET_DK), F32),
                        pltpu.VMEM((RET_HEADS, C, RET_DK), F32)],
    )
    return pl.pallas_call(
        _retention_kernel,
        out_shape=jax.ShapeDtypeStruct((s, RET_HEADS * RET_DV), BF16),
        grid_spec=grid_spec,
        compiler_params=_cparams(("arbitrary",)),
        name="retention",
    )(log_gamma, h_main, h_main, h_main, h_main, cos, sin)


def _gla_decay_matrix(C):
    import numpy as np
    levels = int(math.log2(C))
    r = np.arange(C)[:, None]
    t = np.arange(C)[None, :]
    mats = []
    for l in range(levels):
        blk = C >> l
        half = blk // 2
        m = (r // blk) * blk + half - 1
        qside = (r % blk) >= half
        mats.append(np.where(qside, (t > m) & (t <= r), (t > r) & (t <= m)))
    mats.append(t <= r)
    mats.append(t > r)
    return np.concatenate(mats, axis=0).astype(np.float32)


def _gla_kernel(m_ref, q_ref, k_ref, v_ref, g_ref, la_ref, gn_ref, o_ref, state_ref):
    c = pl.program_id(0)
    C = GLA_CHUNK
    levels = GLA_LEVELS
    nsub = q_ref.shape[0] // C

    @pl.when(c == 0)
    def _():
        state_ref[...] = jnp.zeros_like(state_ref)

    m = m_ref[...]
    ri = lax.broadcasted_iota(jnp.int32, (C, C), 0)
    ci = lax.broadcasted_iota(jnp.int32, (C, C), 1)
    xor = jnp.where(ri > ci, ri ^ ci, 0)
    row = lax.broadcasted_iota(jnp.int32, (C, 1), 0)

    heads = range(GLA_HEADS)
    ks = [slice(h * GLA_DK, (h + 1) * GLA_DK) for h in heads]
    vs = [slice(h * GLA_DV, (h + 1) * GLA_DV) for h in heads]
    rs = [slice(u * C, (u + 1) * C) for u in range(nsub)]
    items = [(u, h) for u in range(nsub) for h in heads]

    expo, q, k, v, scores = {}, {}, {}, {}, {}
    for u, h in items:
        la = la_ref[rs[u], ks[h]]
        la_hi = la.astype(BF16)
        la_lo = (la - la_hi.astype(F32)).astype(BF16)
        expo[u, h] = jnp.exp(_dot(m, la_hi) + _dot(m, la_lo))
    for u, h in items:
        q[u, h] = q_ref[rs[u], ks[h]].astype(F32) * (GLA_DK ** -0.5)
        k[u, h] = k_ref[rs[u], ks[h]].astype(F32)
        v[u, h] = v_ref[rs[u], vs[h]].astype(BF16)
        scores[u, h] = jnp.where(ri == ci, _dot_nt(q[u, h].astype(BF16), k[u, h].astype(BF16)), 0.0)
    for l in range(levels):
        half = C >> (l + 1)
        shift = int(math.log2(half))
        qside = (row & half) != 0
        keep = (xor >> shift) == 1
        for it in items:
            x = (jnp.where(qside, q[it], k[it]) * expo[it][l * C:(l + 1) * C]).astype(BF16)
            scores[it] = scores[it] + jnp.where(keep, _dot_nt(x, x), 0.0)

    state = [state_ref[h] for h in heads]
    outs = {}
    for u, h in items:
        e_b = expo[u, h][levels * C:(levels + 1) * C]
        e_rev = expo[u, h][(levels + 1) * C:(levels + 2) * C]
        e_last = e_b[C - 1:C, :]
        outs[u, h] = (_dot(scores[u, h].astype(BF16), v[u, h])
                      + _dot_nt((q[u, h] * e_b).astype(BF16), state[h].astype(BF16)))
        state[h] = state[h] * e_last + _dot_tn(v[u, h], (k[u, h] * e_rev).astype(BF16))
    for h in heads:
        state_ref[h] = state[h]

    for u, h in items:
        o = outs[u, h]
        o = o * lax.rsqrt(jnp.mean(o * o, axis=-1, keepdims=True) + LN_EPS) * gn_ref[...]
        o_ref[rs[u], vs[h]] = (_silu(g_ref[rs[u], vs[h]].astype(F32)) * o).astype(o_ref.dtype)


def _gla(h_main, log_a, g_norm):
    s = h_main.shape[0]
    C = GLA_CHUNK
    m = jnp.asarray(_gla_decay_matrix(C), dtype=BF16)
    nrow = m.shape[0]
    rows = min(s, GLA_STEP_CHUNKS * C)
    wk = GLA_HEADS * GLA_DK
    wv = GLA_HEADS * GLA_DV
    return pl.pallas_call(
        _gla_kernel,
        out_shape=jax.ShapeDtypeStruct((s, wv), BF16),
        grid=(s // rows,),
        in_specs=[pl.BlockSpec((nrow, C), lambda c: (0, 0)),
                  pl.BlockSpec((rows, wk), lambda c: (c, GQ_OFF // wk)),
                  pl.BlockSpec((rows, wk), lambda c: (c, GK_OFF // wk)),
                  pl.BlockSpec((rows, wv), lambda c: (c, GV_OFF // wv)),
                  pl.BlockSpec((rows, wv), lambda c: (c, GG_OFF // wv)),
                  pl.BlockSpec((rows, wk), lambda c: (c, 0)),
                  pl.BlockSpec((1, GLA_DV), lambda c: (0, 0))],
        out_specs=pl.BlockSpec((rows, wv), lambda c: (c, 0)),
        scratch_shapes=[pltpu.VMEM((GLA_HEADS, GLA_DV, GLA_DK), F32)],
        compiler_params=_cparams(("arbitrary",)),
        name="gla",
    )(m, h_main, h_main, h_main, h_main, log_a, g_norm)


def _mixout_kernel(ret_ref, gla_ref, x_ref, w_ref, g_ref, b_ref, o_ref, wb_ref):
    nr = ret_ref.shape[1]

    @pl.when(pl.program_id(0) == 0)
    def _():
        wb_ref[...] = w_ref[...].astype(BF16)

    tm = x_ref.shape[0]
    parts = 2 if tm % 512 == 0 else 1
    rows = [slice(p * (tm // parts), (p + 1) * (tm // parts)) for p in range(parts)]
    mix = [_dot(ret_ref[r, :], wb_ref[:nr, :]) + _dot(gla_ref[r, :], wb_ref[nr:, :]) for r in rows]
    for r, m in zip(rows, mix):
        o_ref[r, :] = _layer_norm(DEEPNORM_ALPHA * x_ref[r, :] + m, g_ref[...], b_ref[...])


def _mixout_ln(ret, gla, x2d, w_b, g, b):
    s = x2d.shape[0]
    tm = min(s, 512)
    nr, ng = ret.shape[1], gla.shape[1]
    return pl.pallas_call(
        _mixout_kernel,
        out_shape=jax.ShapeDtypeStruct((s, D_MODEL), F32),
        grid=(s // tm,),
        in_specs=[pl.BlockSpec((tm, nr), lambda i: (i, 0)),
                  pl.BlockSpec((tm, ng), lambda i: (i, 0)),
                  pl.BlockSpec((tm, D_MODEL), lambda i: (i, 0)),
                  pl.BlockSpec((nr + ng, D_MODEL), lambda i: (0, 0), pipeline_mode=pl.Buffered(1)),
                  pl.BlockSpec((1, D_MODEL), lambda i: (0, 0)),
                  pl.BlockSpec((1, D_MODEL), lambda i: (0, 0))],
        out_specs=pl.BlockSpec((tm, D_MODEL), lambda i: (i, 0)),
        scratch_shapes=[pltpu.VMEM((nr + ng, D_MODEL), BF16)],
        compiler_params=_cparams(("arbitrary",)),
        name="mixout_ln1",
    )(ret, gla, x2d, w_b, g, b)


def _kv_kernel(mem_ref, wk_ref, wv_ref, k_ref, v_ref):
    m = mem_ref[...].astype(BF16)
    k_ref[...] = _dot(m, wk_ref[...].astype(BF16)).astype(k_ref.dtype)
    v_ref[...] = _dot(m, wv_ref[...].astype(BF16)).astype(v_ref.dtype)


def _mem_kv(mem_b, wk_b, wv_b):
    tn = 512
    return pl.pallas_call(
        _kv_kernel,
        out_shape=(jax.ShapeDtypeStruct((MEM_LEN, D_MODEL), BF16),
                   jax.ShapeDtypeStruct((MEM_LEN, D_MODEL), BF16)),
        grid=(D_MODEL // tn,),
        in_specs=[pl.BlockSpec((MEM_LEN, D_MODEL), lambda j: (0, 0)),
                  pl.BlockSpec((D_MODEL, tn), lambda j: (0, j)),
                  pl.BlockSpec((D_MODEL, tn), lambda j: (0, j))],
        out_specs=(pl.BlockSpec((MEM_LEN, tn), lambda j: (0, j)),
                   pl.BlockSpec((MEM_LEN, tn), lambda j: (0, j))),
        compiler_params=_cparams(("arbitrary",)),
        name="mem_kv",
    )(mem_b, wk_b, wv_b)


def _cross_kernel(h_ref, wq_ref, k_ref, v_ref, wo_ref, g_ref, b_ref, wr_ref, br_ref,
                  h2_ref, lg_ref):
    tm = h_ref.shape[0]
    parts = 2 if tm % 512 == 0 else 1
    pr = tm // parts
    rows = [slice(p * pr, (p + 1) * pr) for p in range(parts)]
    sls = [slice(hd * MEM_HEAD_DIM, (hd + 1) * MEM_HEAD_DIM) for hd in range(MEM_HEADS)]
    h1 = [h_ref[r, :] for r in rows]
    q = [_dot(h.astype(BF16), wq_ref[...]).astype(BF16) for h in h1]
    o = []
    for p in range(parts):
        outs = []
        for sl in sls:
            s = _dot_nt(q[p][:, sl], k_ref[:, sl]) * (MEM_HEAD_DIM ** -0.5)
            s = s - jnp.max(s, axis=-1, keepdims=True)
            e = jnp.exp(s)
            e = e / jnp.sum(e, axis=-1, keepdims=True)
            outs.append(_dot(e.astype(BF16), v_ref[:, sl]))
        o.append(jnp.concatenate(outs, axis=-1).astype(BF16))
    cross = [_dot(x, wo_ref[...]) for x in o]
    for p in range(parts):
        h2 = _layer_norm(DEEPNORM_ALPHA * h1[p] + cross[p], g_ref[...], b_ref[...])
        h2_ref[rows[p], :] = h2
        lg_ref[rows[p], :] = _dot(h2.astype(BF16), wr_ref[...]) + br_ref[...]


def _cross_attention(h1, wq_b, k, v, wo_b, g, b, w_route, b_route):
    s = h1.shape[0]
    tm = min(s, 512)
    const = lambda shape: pl.BlockSpec(shape, lambda i: (0, 0), pipeline_mode=pl.Buffered(1))
    return pl.pallas_call(
        _cross_kernel,
        out_shape=(jax.ShapeDtypeStruct((s, D_MODEL), F32),
                   jax.ShapeDtypeStruct((s, LANES), F32)),
        grid=(s // tm,),
        in_specs=[pl.BlockSpec((tm, D_MODEL), lambda i: (i, 0)),
                  const((D_MODEL, D_MODEL)),
                  const((MEM_LEN, D_MODEL)),
                  const((MEM_LEN, D_MODEL)),
                  const((D_MODEL, D_MODEL)),
                  const((1, D_MODEL)),
                  const((1, D_MODEL)),
                  const((D_MODEL, LANES)),
                  const((1, LANES))],
        out_specs=(pl.BlockSpec((tm, D_MODEL), lambda i: (i, 0)),
                   pl.BlockSpec((tm, LANES), lambda i: (i, 0))),
        compiler_params=_cparams(("arbitrary",)),
        name="cross_attn_ln2",
    )(h1, wq_b, k, v, wo_b, g, b, w_route, b_route)


def _route_kernel(lg_ref, slot_ref, slot_t_ref, gate_ref, runs_ref, plan_ref, tot_row, tot_col, gstart, rinfo):
    phase = pl.program_id(0)
    i = pl.program_id(1)
    tm = lg_ref.shape[0]
    lane = lax.broadcasted_iota(jnp.int32, (tm, LANES), 1)
    rows = pl.ds(pl.multiple_of(i * tm, tm), tm)

    @pl.when(phase == 0)
    def _():
        neg = -jnp.inf
        logits = lg_ref[...]
        gmask = lane < N_GROUPS
        gl = jnp.where(gmask, logits, neg)
        ge = jnp.exp(gl - jnp.max(gl, axis=-1, keepdims=True))
        pg = ge / jnp.sum(ge, axis=-1, keepdims=True)
        pg_sel = jnp.max(pg, axis=-1, keepdims=True)
        grp = jnp.min(jnp.where((pg == pg_sel) & gmask, lane, LANES), axis=-1, keepdims=True)

        fl_lane = lane - N_GROUPS
        fmask = (fl_lane >= 0) & (fl_lane < N_EXPERTS) & ((fl_lane >> 3) == grp)
        fl = jnp.where(fmask, logits, neg)
        fe = jnp.exp(fl - jnp.max(fl, axis=-1, keepdims=True))
        fp = fe / jnp.sum(fe, axis=-1, keepdims=True)
        p1 = jnp.max(fp, axis=-1, keepdims=True)
        i1 = jnp.min(jnp.where((fp == p1) & fmask, lane, LANES), axis=-1, keepdims=True)
        rest = fmask & (lane != i1)
        fp2 = jnp.where(rest, fp, -1.0)
        p2 = jnp.max(fp2, axis=-1, keepdims=True)
        i2 = jnp.min(jnp.where((fp2 == p2) & rest, lane, LANES), axis=-1, keepdims=True)
        psum = p1 + p2
        picks = ((i1 - N_GROUPS).astype(F32), (i2 - N_GROUPS).astype(F32),
                 pg_sel * p1 / psum, pg_sel * p2 / psum)
        info = jnp.zeros((tm, LANES), F32)
        for j, val in enumerate(picks):
            info = jnp.where(lane == j, val, info)
        rinfo[rows, :] = info

    info = rinfo[rows, :]
    oh1 = lane == info[:, 0:1].astype(jnp.int32)
    oh2 = lane == info[:, 1:2].astype(jnp.int32)
    gate1 = info[:, 2:3]
    gate2 = info[:, 3:4]
    oh = (jnp.where(oh1, 1.0, 0.0) + jnp.where(oh2, 1.0, 0.0)).astype(BF16)
    ones = jnp.ones((tm, LANES), BF16)

    def align_up(v, a):
        return jnp.floor((v + (a - 1.0)) * (1.0 / a)) * a

    run_row = align_up(_dot_tn(ones, oh)[0:SUBLANES, :], RUN_ALIGN)
    rr = lax.broadcasted_iota(jnp.int32, (LANES, LANES), 0)
    cc = lax.broadcasted_iota(jnp.int32, (LANES, LANES), 1)
    strict_upper = jnp.where(rr < cc, 1.0, 0.0).astype(BF16)

    @pl.when((phase == 0) & (i == 0))
    def _():
        tot_row[...] = jnp.zeros_like(tot_row)
        tot_col[...] = jnp.zeros_like(tot_col)

    @pl.when(phase == 0)
    def _():
        tot_row[...] += run_row
        tot_col[...] += align_up(_dot_tn(oh, ones), RUN_ALIGN)

    @pl.when((phase == 1) & (i == 0))
    def _():
        nblk_row = align_up(tot_row[...], MOE_BLK) * (1.0 / MOE_BLK)
        nblk_col = align_up(tot_col[...], MOE_BLK) * (1.0 / MOE_BLK)
        lower_incl = jnp.where(cc <= rr, 1.0, 0.0).astype(BF16)
        base = _dot(nblk_row.astype(BF16), strict_upper) * float(MOE_BLK)
        ends = _dot(lower_incl, nblk_col.astype(BF16))
        expert_rows = rr < N_EXPERTS
        be = jnp.sum(jnp.where(expert_rows & (ends <= cc.astype(F32)), 1.0, 0.0), axis=0, keepdims=True)
        be = jnp.minimum(be, N_EXPERTS - 1.0)
        total = jnp.sum(jnp.where(expert_rows, nblk_col, 0.0), axis=0, keepdims=True)
        sub = lax.broadcasted_iota(jnp.int32, plan_ref.shape, 0)
        plan = jnp.where(sub == 0, jnp.broadcast_to(be, plan_ref.shape),
                         jnp.where(sub == 1, jnp.broadcast_to(total, plan_ref.shape),
                                   jnp.where(sub == 2, base + tot_row[...],
                                             jnp.where(sub == 3, nblk_row,
                                                       nblk_row * float(MOE_BLK) - tot_row[...]))))
        plan_ref[...] = plan.astype(jnp.int32)
        gstart[...] = base

    @pl.when(phase == 1)
    def _():
        off = _dot((run_row * (1.0 / RUN_ALIGN)).astype(BF16), strict_upper) * float(RUN_ALIGN)
        tr = lax.broadcasted_iota(jnp.int32, (tm, tm), 0)
        tc = lax.broadcasted_iota(jnp.int32, (tm, tm), 1)
        strict_lower = jnp.where(tc < tr, 1.0, 0.0).astype(BF16)
        slot = _dot(strict_lower, oh) + off[0:1, :]
        s1 = jnp.sum(jnp.where(oh1, slot, 0.0), axis=-1, keepdims=True)
        s2 = jnp.sum(jnp.where(oh2, slot, 0.0), axis=-1, keepdims=True)
        slots = jnp.where(lane == 0, s1, jnp.where(lane == 1, s2, 0.0))
        slot_ref[...] = slots.astype(jnp.int32)
        slot_t_ref[...] = jnp.transpose(slots)[0:SUBLANES, :].astype(jnp.int32)
        gate_ref[...] = jnp.where(lane == 0, gate1, jnp.where(lane == 1, gate2, 0.0))
        sub = lax.broadcasted_iota(jnp.int32, runs_ref.shape, 0)
        runs = jnp.where(sub == 0, gstart[...], jnp.where(sub == 1, off, run_row))
        runs_ref[...] = runs.astype(jnp.int32)
        gstart[...] += run_row


def _route(logits):
    t = logits.shape[0]
    tm = min(t, TILE_TOK)
    nt = t // tm
    step = lambda p, i: (i * p, 0)
    return pl.pallas_call(
        _route_kernel,
        out_shape=(jax.ShapeDtypeStruct((t, LANES), jnp.int32),
                   jax.ShapeDtypeStruct((nt * SUBLANES, tm), jnp.int32),
                   jax.ShapeDtypeStruct((t, LANES), F32),
                   jax.ShapeDtypeStruct((nt * SUBLANES, LANES), jnp.int32),
                   jax.ShapeDtypeStruct((SUBLANES, LANES), jnp.int32)),
        grid=(2, nt),
        in_specs=[pl.BlockSpec((tm, LANES), lambda p, i: (i, 0))],
        out_specs=(pl.BlockSpec((tm, LANES), step),
                   pl.BlockSpec((SUBLANES, tm), step),
                   pl.BlockSpec((tm, LANES), step),
                   pl.BlockSpec((SUBLANES, LANES), step),
                   pl.BlockSpec((SUBLANES, LANES), lambda p, i: (0, 0))),
        scratch_shapes=[pltpu.VMEM((SUBLANES, LANES), F32),
                        pltpu.VMEM((LANES, LANES), F32),
                        pltpu.VMEM((SUBLANES, LANES), F32),
                        pltpu.VMEM((t, LANES), F32)],
        compiler_params=_cparams(("arbitrary", "arbitrary")),
        name="moe_route",
    )(logits)


def _run_copies(rg_ref, ro_ref, rn_ref, tile, make_copy):
    def each(action):
        def body(e, carry):
            j = tile * N_EXPERTS + e
            n = rn_ref[j]

            @pl.when(n > 0)
            def _():
                action(make_copy(pl.multiple_of(rg_ref[j], RUN_ALIGN), pl.multiple_of(ro_ref[j], RUN_ALIGN),
                                 pl.multiple_of(n, RUN_ALIGN)))
            return carry
        lax.fori_loop(0, N_EXPERTS, body, 0)
    return each


def _dispatch_kernel(rg_ref, ro_ref, rn_ref, zrow_ref, zlen_ref, nu_ref, x_ref, slot_t_ref, xb_ref,
                     sorted_ref, zero_ref, sems, zsem):
    i = pl.program_id(0)
    n = pl.num_programs(0)
    tm = x_ref.shape[0]
    nb = xb_ref.shape[0] // MOE_BLK
    buf = i % 2

    def zero_copy(row, size):
        return pltpu.make_async_copy(
            zero_ref.at[pl.ds(0, size), :], xb_ref.at[pl.ds(pl.multiple_of(row, RUN_ALIGN), size), :], zsem)

    def pad_copy(e):
        return zero_copy(zrow_ref[e], pl.multiple_of(zlen_ref[e], RUN_ALIGN))

    def runs_of(tile, slot):
        def make_copy(g, o, size):
            return pltpu.make_async_copy(sorted_ref.at[slot, pl.ds(o, size), :],
                                         xb_ref.at[pl.ds(g, size), :], sems.at[slot])
        return _run_copies(rg_ref, ro_ref, rn_ref, tile, make_copy)

    @pl.when(i == 0)
    def _():
        zero_ref[...] = jnp.zeros_like(zero_ref)

        def issue_zero(e, carry):
            @pl.when(zlen_ref[e] > 0)
            def _():
                pad_copy(e).start()
            return carry
        lax.fori_loop(0, N_EXPERTS, issue_zero, 0)
        lax.fori_loop(nu_ref[0], nb, lambda b, c: (zero_copy(b * MOE_BLK, MOE_BLK).start(), c)[1], 0)

    @pl.when(i >= 2)
    def _():
        runs_of(i - 2, buf)(lambda c: c.wait())

    x = x_ref[...].astype(BF16)
    s1 = slot_t_ref[0:1, :]
    s2 = slot_t_ref[1:2, :]
    for rc in range(TILE_SLOTS // SORT_ROWS):
        r = lax.broadcasted_iota(jnp.int32, (SORT_ROWS, tm), 0) + rc * SORT_ROWS
        perm = jnp.where((r == s1) | (r == s2), 1.0, 0.0).astype(BF16)
        sorted_ref[buf, rc * SORT_ROWS:(rc + 1) * SORT_ROWS, :] = _dot(perm, x).astype(BF16)

    runs_of(i, buf)(lambda c: c.start())

    @pl.when(i == n - 1)
    def _():
        @pl.when(i >= 1)
        def _():
            runs_of(i - 1, 1 - buf)(lambda c: c.wait())
        runs_of(i, buf)(lambda c: c.wait())

        def wait_zero(e, carry):
            @pl.when(zlen_ref[e] > 0)
            def _():
                pad_copy(e).wait()
            return carry
        lax.fori_loop(0, N_EXPERTS, wait_zero, 0)
        lax.fori_loop(nu_ref[0], nb, lambda b, c: (zero_copy(0, MOE_BLK).wait(), c)[1], 0)


def _dispatch(rg, ro, rn, zrow, zlen, n_used, h2, slot_t, n_rows):
    t = h2.shape[0]
    tm = min(t, TILE_TOK)
    grid_spec = pltpu.PrefetchScalarGridSpec(
        num_scalar_prefetch=6,
        grid=(t // tm,),
        in_specs=[pl.BlockSpec((tm, D_MODEL), lambda i, *_: (i, 0)),
                  pl.BlockSpec((SUBLANES, tm), lambda i, *_: (i, 0))],
        out_specs=pl.BlockSpec(memory_space=pl.ANY),
        scratch_shapes=[pltpu.VMEM((2, TILE_SLOTS, D_MODEL), BF16),
                        pltpu.VMEM((MOE_BLK, D_MODEL), BF16),
                        pltpu.SemaphoreType.DMA((2,)),
                        pltpu.SemaphoreType.DMA(())],
    )
    return pl.pallas_call(
        _dispatch_kernel,
        out_shape=jax.ShapeDtypeStruct((n_rows, D_MODEL), BF16),
        grid_spec=grid_spec,
        compiler_params=_cparams(("arbitrary",)),
        name="moe_dispatch",
    )(rg, ro, rn, zrow, zlen, n_used, h2, slot_t)


def _expert_kernel(be_ref, nu_ref, nblk_ref, x_ref, wg_ref, wu_ref, wd_ref, y_ref,
                   wg_f, wu_f, wd_f, wg_b, wu_b, wd_b, sems, cur_ref):
    step = pl.program_id(0)
    n_used = nu_ref[0]

    def weight_copies(expert, slot):
        half = EXPERT_FF // 2
        return [pltpu.make_async_copy(wg_ref.at[expert], wg_f.at[slot], sems.at[slot]),
                pltpu.make_async_copy(wu_ref.at[expert], wu_f.at[slot], sems.at[slot]),
                pltpu.make_async_copy(wd_ref.at[expert, :half], wd_f.at[slot, :half], sems.at[slot]),
                pltpu.make_async_copy(wd_ref.at[expert, half:], wd_f.at[slot, half:], sems.at[slot])]

    def start_weights(expert, slot):
        for c in weight_copies(expert, slot):
            c.start()

    def after(blk):
        return blk + nblk_ref[be_ref[blk]]

    @pl.when(step == 0)
    def _():
        cur_ref[0] = 0
        start_weights(be_ref[0], 0)

        @pl.when(after(0) < n_used)
        def _():
            start_weights(be_ref[after(0)], 1)

    for sub in range(EXPERT_STEP_BLKS):
        b = step * EXPERT_STEP_BLKS + sub
        used = b < n_used
        e = be_ref[b]
        prev = be_ref[jnp.maximum(b - 1, 0)]
        fresh = used & ((b == 0) | (e != prev))
        rows = slice(sub * MOE_BLK, (sub + 1) * MOE_BLK)

        @pl.when(fresh)
        def _():
            slot = cur_ref[0]
            for c in weight_copies(e, slot):
                c.wait()
            wg_b[...] = wg_f[slot].astype(BF16)
            wu_b[...] = wu_f[slot].astype(BF16)
            wd_b[...] = wd_f[slot].astype(BF16)
            nxt = after(b)

            @pl.when(nxt < n_used)
            def _():
                nxt2 = after(nxt)

                @pl.when(nxt2 < n_used)
                def _():
                    start_weights(be_ref[nxt2], slot)
            cur_ref[0] = 1 - slot

        @pl.when(used)
        def _():
            x = x_ref[rows, :]
            hid = _silu(_dot(x, wg_b[...])) * _dot(x, wu_b[...])
            y_ref[rows, :] = _dot(hid.astype(BF16), wd_b[...]).astype(y_ref.dtype)

        @pl.when(jnp.logical_not(used))
        def _():
            y_ref[rows, :] = jnp.zeros((MOE_BLK, D_MODEL), y_ref.dtype)


def _experts(block_e, n_used, nblk, xb, w_gate, w_up, w_down):
    step_rows = EXPERT_STEP_BLKS * MOE_BLK
    any_space = pl.BlockSpec(memory_space=pl.ANY)
    last_step = lambda nu: (nu[0] - 1) // EXPERT_STEP_BLKS
    grid_spec = pltpu.PrefetchScalarGridSpec(
        num_scalar_prefetch=3,
        grid=(xb.shape[0] // step_rows,),
        in_specs=[pl.BlockSpec((step_rows, D_MODEL), lambda s, be, nu, nk: (jnp.minimum(s, last_step(nu)), 0)),
                  any_space, any_space, any_space],
        out_specs=pl.BlockSpec((step_rows, D_MODEL), lambda s, be, nu, nk: (s, 0)),
        scratch_shapes=[pltpu.VMEM((2, D_MODEL, EXPERT_FF), F32),
                        pltpu.VMEM((2, D_MODEL, EXPERT_FF), F32),
                        pltpu.VMEM((2, EXPERT_FF, D_MODEL), F32),
                        pltpu.VMEM((D_MODEL, EXPERT_FF), BF16),
                        pltpu.VMEM((D_MODEL, EXPERT_FF), BF16),
                        pltpu.VMEM((EXPERT_FF, D_MODEL), BF16),
                        pltpu.SemaphoreType.DMA((2,)),
                        pltpu.SMEM((1,), jnp.int32)],
    )
    return pl.pallas_call(
        _expert_kernel,
        out_shape=jax.ShapeDtypeStruct(xb.shape, BF16),
        grid_spec=grid_spec,
        compiler_params=_cparams(("arbitrary",)),
        name="moe_experts",
    )(block_e, n_used, nblk, xb, w_gate, w_up, w_down)


def _combine_kernel(rg_ref, ro_ref, rn_ref, yb_ref, h2_ref, slot_ref, gate_ref, g_ref, b_ref, o_ref,
                    ybuf, sems):
    i = pl.program_id(0)
    n = pl.num_programs(0)
    tm = h2_ref.shape[0]
    buf = i % 2

    def runs_of(tile, slot):
        def make_copy(g, o, size):
            return pltpu.make_async_copy(yb_ref.at[pl.ds(g, size), :],
                                         ybuf.at[slot, pl.ds(o, size), :], sems.at[slot])
        return _run_copies(rg_ref, ro_ref, rn_ref, tile, make_copy)

    @pl.when(i == 0)
    def _():
        ybuf[...] = jnp.zeros_like(ybuf)
        runs_of(0, 0)(lambda c: c.start())

    @pl.when(i + 1 < n)
    def _():
        runs_of(i + 1, 1 - buf)(lambda c: c.start())

    runs_of(i, buf)(lambda c: c.wait())

    lane = lax.broadcasted_iota(jnp.int32, (tm, TILE_SLOTS), 1)
    slots = slot_ref[...]
    gate = gate_ref[...]
    y = ybuf[buf]
    sel = jnp.where(lane == slots[:, 0:1], gate[:, 0:1],
                    jnp.where(lane == slots[:, 1:2], gate[:, 1:2], 0.0)).astype(BF16)
    parts = 2 if tm % 512 == 0 else 1
    rows = [slice(p * (tm // parts), (p + 1) * (tm // parts)) for p in range(parts)]
    ffn = [_dot(sel[r, :], y) for r in rows]
    for r, f in zip(rows, ffn):
        o_ref[r, :] = _layer_norm(DEEPNORM_ALPHA * h2_ref[r, :] + f, g_ref[...], b_ref[...])


def _combine_ln(rg, ro, rn, yb, h2, slots, gate, g, b):
    t = h2.shape[0]
    tm = min(t, TILE_TOK)
    grid_spec = pltpu.PrefetchScalarGridSpec(
        num_scalar_prefetch=3,
        grid=(t // tm,),
        in_specs=[pl.BlockSpec(memory_space=pl.ANY),
                  pl.BlockSpec((tm, D_MODEL), lambda i, *_: (i, 0)),
                  pl.BlockSpec((tm, LANES), lambda i, *_: (i, 0)),
                  pl.BlockSpec((tm, LANES), lambda i, *_: (i, 0)),
                  pl.BlockSpec((1, D_MODEL), lambda i, *_: (0, 0)),
                  pl.BlockSpec((1, D_MODEL), lambda i, *_: (0, 0))],
        out_specs=pl.BlockSpec((tm, D_MODEL), lambda i, *_: (i, 0)),
        scratch_shapes=[pltpu.VMEM((2, TILE_SLOTS, D_MODEL), BF16),
                        pltpu.SemaphoreType.DMA((2,))],
    )
    return pl.pallas_call(
        _combine_kernel,
        out_shape=jax.ShapeDtypeStruct((t, D_MODEL), F32),
        grid_spec=grid_spec,
        compiler_params=_cparams(("arbitrary",)),
        name="moe_combine_ln3",
    )(rg, ro, rn, yb, h2, slots, gate, g, b)


def _mixer(x2d, positions, w_in, w_gla_a2, b_gla_a, g_gla_norm):
    s = x2d.shape[0]
    half = RET_DK // 2
    inv_freq = (ROPE_BASE ** (-jnp.arange(half, dtype=F32) / half)).reshape(1, half)
    cos, sin = _rope_table(positions.reshape(s, 1), inv_freq)
    w_in_t = jnp.swapaxes(w_in, 0, 1)
    w_lr_t = jnp.pad(w_in_t[GLR_OFF:].astype(BF16), ((0, LANES - GLA_LOWRANK), (0, 0)))
    w_a2 = jnp.pad(w_gla_a2.astype(BF16), ((0, LANES - GLA_LOWRANK), (0, 0)))
    h_main, log_a = _proj_in(x2d, w_in_t, w_lr_t, w_a2, b_gla_a.reshape(1, -1))
    log_gamma = jnp.log1p(-jnp.exp2(-5.0 - jnp.arange(RET_HEADS, dtype=F32)))
    ret = _retention(h_main, cos, sin, log_gamma)
    gla = _gla(h_main, log_a, g_gla_norm.reshape(1, -1))
    return ret, gla


def _moe(h2, logits, w_gate, w_up, w_down, g, b):
    t = h2.shape[0]
    nt = t // min(t, TILE_TOK)
    slots, slot_t, gate, runs, plan = _route(logits)
    runs = runs.reshape(nt, SUBLANES, LANES)[:, :3, :N_EXPERTS]
    rg, ro, rn = (runs[:, j, :].reshape(-1) for j in range(3))
    max_rows = 2 * t + nt * N_EXPERTS * (RUN_ALIGN - 1) + N_EXPERTS * (MOE_BLK - 1)
    nb = -(-max_rows // (MOE_BLK * EXPERT_STEP_BLKS)) * EXPERT_STEP_BLKS
    block_e, n_used = plan[0, :nb], plan[1, :1]
    pad_row, nblk, pad_len = (plan[j, :N_EXPERTS] for j in (2, 3, 4))
    xb = _dispatch(rg, ro, rn, pad_row, pad_len, n_used, h2, slot_t, nb * MOE_BLK)
    yb = _experts(block_e, n_used, nblk, xb, w_gate, w_up, w_down)
    return _combine_ln(rg, ro, rn, yb, h2, slots, gate, g, b)


def kernel(x, mem, positions, w_in, w_gla_a2, b_gla_a, g_gla_norm, w_mix_out, ln1_g, ln1_b, w_mq, w_mk, w_mv, w_mo, ln2_g, ln2_b, w_route_group, b_route_group, w_route_expert, b_route_expert, w_exp_gate, w_exp_up, w_exp_down, ln3_g, ln3_b):
    bsz, s, d = x.shape
    assert bsz == 1 and d == D_MODEL
    x2d = x.reshape(s, d)
    row = lambda v: v.reshape(1, -1)

    ret, gla = _mixer(x2d, positions, w_in[0], w_gla_a2[0], b_gla_a[0], g_gla_norm[0])
    h1 = _mixout_ln(ret, gla, x2d, w_mix_out[0], row(ln1_g[0]), row(ln1_b[0]))

    k, v = _mem_kv(mem[0], w_mk[0], w_mv[0])
    n_route = N_GROUPS + N_EXPERTS
    w_route = jnp.pad(jnp.concatenate([w_route_group[0], w_route_expert[0]], axis=1).astype(BF16),
                      ((0, 0), (0, LANES - n_route)))
    b_route = jnp.pad(jnp.concatenate([b_route_group[0], b_route_expert[0].reshape(-1)]),
                      (0, LANES - n_route)).reshape(1, LANES)
    h2, logits = _cross_attention(h1, w_mq[0].astype(BF16), k, v, w_mo[0].astype(BF16),
                                  row(ln2_g[0]), row(ln2_b[0]), w_route, b_route)

    out = _moe(h2, logits, w_exp_gate[0], w_exp_up[0], w_exp_down[0],
               row(ln3_g[0]), row(ln3_b[0]))
    return out.reshape(bsz, s, d)
```

```python
import math

import jax
import jax.numpy as jnp
from jax import lax
from jax.experimental import pallas as pl
from jax.experimental.pallas import tpu as pltpu

F32 = jnp.float32
BF16 = jnp.bfloat16

D_MODEL = 2048
MEM_LEN = 256
RET_HEADS = 4
RET_DK = 256
RET_DV = 256
GLA_HEADS = 4
GLA_DK = 128
GLA_DV = 256
GLA_LOWRANK = 16
GLA_TAU = 16.0
ROPE_BASE = 10000.0
MEM_HEADS = 4
MEM_HEAD_DIM = D_MODEL // MEM_HEADS
N_GROUPS = 4
EXPERTS_PER_GROUP = 8
N_EXPERTS = N_GROUPS * EXPERTS_PER_GROUP
EXPERT_FF = 512
LN_EPS = 1e-5
DEPTH = 1
DEEPNORM_ALPHA = (2 * DEPTH) ** 0.25

RQ_OFF, RK_OFF, RV_OFF, RG_OFF = 0, 1024, 2048, 3072
GQ_OFF, GK_OFF, GV_OFF, GG_OFF, GLR_OFF = 4096, 4608, 5120, 6144, 7168
IN_MAIN = 7168

LANES = 128
SUBLANES = 8
RET_CHUNK = 256
RET_STEP_CHUNKS = 2
GLA_CHUNK = 128
GLA_LEVELS = 7
GLA_STEP_CHUNKS = 4
MOE_BLK = 256
EXPERT_STEP_BLKS = 4
TILE_TOK = 512
RUN_ALIGN = 16
TILE_SLOTS = 2 * TILE_TOK + N_EXPERTS * RUN_ALIGN
SORT_ROWS = 256
PROJ_K_CHUNK = 512
VMEM_LIMIT = 56 * 1024 * 1024


def _cparams(sem):
    return pltpu.CompilerParams(dimension_semantics=sem, vmem_limit_bytes=VMEM_LIMIT)


def _layer_norm(y, g, b):
    mu = jnp.mean(y, axis=-1, keepdims=True)
    d = y - mu
    var = jnp.mean(d * d, axis=-1, keepdims=True)
    return d * lax.rsqrt(var + LN_EPS) * g + b


def _silu(x):
    return x / (1.0 + jnp.exp(-x))


def _dot(a, b):
    return jnp.dot(a, b, preferred_element_type=F32)


def _dot_nt(a, b):
    return lax.dot_general(a, b, (((1,), (1,)), ((), ())), preferred_element_type=F32)


def _dot_tn(a, b):
    return lax.dot_general(a, b, (((0,), (0,)), ((), ())), preferred_element_type=F32)


def _rope_kernel(pos_ref, invf_ref, cos_ref, sin_ref):
    ang = pos_ref[...].astype(F32) * invf_ref[...]
    cos_ref[...] = jnp.cos(ang)
    sin_ref[...] = jnp.sin(ang)


def _rope_table(pos_col, inv_freq):
    s = pos_col.shape[0]
    tm = min(s, 1024)
    half = inv_freq.shape[1]
    return pl.pallas_call(
        _rope_kernel,
        out_shape=(jax.ShapeDtypeStruct((s, half), F32), jax.ShapeDtypeStruct((s, half), F32)),
        grid=(s // tm,),
        in_specs=[pl.BlockSpec((tm, 1), lambda i: (i, 0)),
                  pl.BlockSpec((1, half), lambda i: (0, 0))],
        out_specs=(pl.BlockSpec((tm, half), lambda i: (i, 0)),
                   pl.BlockSpec((tm, half), lambda i: (i, 0))),
        compiler_params=_cparams(("arbitrary",)),
        name="rope_table",
    )(pos_col, inv_freq)


def _proj_in_kernel(x_ref, wt_ref, wlr_ref, wa2_ref, ba_ref, o_ref, la_ref, xb_ref):
    @pl.when(pl.program_id(1) == 0)
    def _():
        xb_ref[...] = x_ref[...].astype(BF16)
        glr = _dot_nt(xb_ref[...], wlr_ref[...])
        z = _dot(glr.astype(BF16), wa2_ref[...]) + ba_ref[...]
        la_ref[...] = (jnp.minimum(z, 0.0) - jnp.log(1.0 + jnp.exp(-jnp.abs(z)))) / GLA_TAU

    acc = None
    for c in range(D_MODEL // PROJ_K_CHUNK):
        ck = slice(c * PROJ_K_CHUNK, (c + 1) * PROJ_K_CHUNK)
        part = _dot_nt(xb_ref[:, ck], wt_ref[:, ck].astype(BF16))
        acc = part if acc is None else acc + part
    o_ref[...] = acc.astype(o_ref.dtype)


def _proj_in(x2d, w_in_t, w_lr_t, w_a2, b_a):
    s = x2d.shape[0]
    tm = min(s, 1024)
    tn = 1024
    n = GLA_HEADS * GLA_DK
    return pl.pallas_call(
        _proj_in_kernel,
        out_shape=(jax.ShapeDtypeStruct((s, IN_MAIN), BF16), jax.ShapeDtypeStruct((s, n), F32)),
        grid=(s // tm, IN_MAIN // tn),
        in_specs=[pl.BlockSpec((tm, D_MODEL), lambda i, j: (i, 0)),
                  pl.BlockSpec((tn, D_MODEL), lambda i, j: (j, 0)),
                  pl.BlockSpec((LANES, D_MODEL), lambda i, j: (0, 0)),
                  pl.BlockSpec((LANES, n), lambda i, j: (0, 0)),
                  pl.BlockSpec((1, n), lambda i, j: (0, 0))],
        out_specs=(pl.BlockSpec((tm, tn), lambda i, j: (i, j)),
                   pl.BlockSpec((tm, n), lambda i, j: (i, 0))),
        scratch_shapes=[pltpu.VMEM((tm, D_MODEL), BF16)],
        compiler_params=_cparams(("arbitrary", "arbitrary")),
        name="proj_in",
    )(x2d, w_in_t, w_lr_t, w_a2, b_a)


def _rotary(t, cos, sin):
    half = t.shape[-1] // 2
    t1, t2 = t[:, :half], t[:, half:]
    return jnp.concatenate([t1 * cos - t2 * sin, t1 * sin + t2 * cos], axis=-1)


def _retention_kernel(lg_ref, q_ref, k_ref, v_ref, g_ref, cos_ref, sin_ref, o_ref, state_ref, intra_ref, dq_ref, dk_ref):
    c = pl.program_id(0)
    C = intra_ref.shape[1]
    nsub = q_ref.shape[0] // C

    @pl.when(c == 0)
    def _():
        state_ref[...] = jnp.zeros_like(state_ref)
        ri = lax.broadcasted_iota(jnp.int32, (C, C), 0)
        ci = lax.broadcasted_iota(jnp.int32, (C, C), 1)
        rel = jnp.maximum(ri - ci, 0).astype(F32)
        n = lax.broadcasted_iota(jnp.int32, (C, RET_DK), 0).astype(F32)
        for h in range(RET_HEADS):
            lg = lg_ref[h]
            intra_ref[h] = jnp.where(ri >= ci, jnp.exp(lg * rel), 0.0)
            dq_ref[h] = jnp.exp(lg * (n + 1.0))
            dk_ref[h] = jnp.exp(lg * (C - 1.0 - n))

    state = [state_ref[h] for h in range(RET_HEADS)]
    for u in range(nsub):
        rows = slice(u * C, (u + 1) * C)
        cos = cos_ref[rows, :]
        sin = sin_ref[rows, :]
        for h in range(RET_HEADS):
            sl = slice(h * RET_DK, (h + 1) * RET_DK)
            q = _rotary(q_ref[rows, sl].astype(F32), cos, sin)
            k = _rotary(k_ref[rows, sl].astype(F32), cos, sin) * (RET_DK ** -0.5)
            v = v_ref[rows, sl].astype(BF16)
            decay_chunk = jnp.exp(lg_ref[h] * C)

            scores = _dot_nt(q.astype(BF16), k.astype(BF16)) * intra_ref[h]
            o = _dot(scores.astype(BF16), v) + _dot((q * dq_ref[h]).astype(BF16), state[h].astype(BF16))
            state[h] = decay_chunk * state[h] + _dot_tn((k * dk_ref[h]).astype(BF16), v)

            mu = jnp.mean(o, axis=-1, keepdims=True)
            d = o - mu
            var = jnp.mean(d * d, axis=-1, keepdims=True)
            o = d * lax.rsqrt(var + LN_EPS)
            o_ref[rows, sl] = (_silu(g_ref[rows, sl].astype(F32)) * o).astype(o_ref.dtype)
    for h in range(RET_HEADS):
        state_ref[h] = state[h]


def _retention(h_main, cos, sin, log_gamma):
    s = h_main.shape[0]
    C = min(RET_CHUNK, s)
    rows = min(s, RET_STEP_CHUNKS * C)
    w = RET_HEADS * RET_DK
    col = lambda off: (lambda c, lg: (c, off // w))
    grid_spec = pltpu.PrefetchScalarGridSpec(
        num_scalar_prefetch=1,
        grid=(s // rows,),
        in_specs=[pl.BlockSpec((rows, w), col(RQ_OFF)),
                  pl.BlockSpec((rows, w), col(RK_OFF)),
                  pl.BlockSpec((rows, w), col(RV_OFF)),
                  pl.BlockSpec((rows, w), col(RG_OFF)),
                  pl.BlockSpec((rows, RET_DK // 2), lambda c, lg: (c, 0)),
                  pl.BlockSpec((rows, RET_DK // 2), lambda c, lg: (c, 0))],
        out_specs=pl.BlockSpec((rows, RET_HEADS * RET_DV), lambda c, lg: (c, 0)),
        scratch_shapes=[pltpu.VMEM((RET_HEADS, RET_DK, RET_DV), F32),
                        pltpu.VMEM((RET_HEADS, C, C), F32),
                        pltpu.VMEM((RET_HEADS, C, RET_DK), F32),
                        pltpu.VMEM((RET_HEADS, C, RET_DK), F32)],
    )
    return pl.pallas_call(
        _retention_kernel,
        out_shape=jax.ShapeDtypeStruct((s, RET_HEADS * RET_DV), BF16),
        grid_spec=grid_spec,
        compiler_params=_cparams(("arbitrary",)),
        name="retention",
    )(log_gamma, h_main, h_main, h_main, h_main, cos, sin)


def _gla_decay_matrix(C):
    import numpy as np
    levels = int(math.log2(C))
    r = np.arange(C)[:, None]
    t = np.arange(C)[None, :]
    mats = []
    for l in range(levels):
        blk = C >> l
        half = blk // 2
        m = (r // blk) * blk + half - 1
        qside = (r % blk) >= half
        mats.append(np.where(qside, (t > m) & (t <= r), (t > r) & (t <= m)))
    mats.append(t <= r)
    mats.append(t > r)
    return np.concatenate(mats, axis=0).astype(np.float32)


def _gla_kernel(m_ref, q_ref, k_ref, v_ref, g_ref, la_ref, gn_ref, o_ref, state_ref):
    c = pl.program_id(0)
    C = GLA_CHUNK
    levels = GLA_LEVELS
    nsub = q_ref.shape[0] // C

    @pl.when(c == 0)
    def _():
        state_ref[...] = jnp.zeros_like(state_ref)

    m = m_ref[...]
    ri = lax.broadcasted_iota(jnp.int32, (C, C), 0)
    ci = lax.broadcasted_iota(jnp.int32, (C, C), 1)
    xor = jnp.where(ri > ci, ri ^ ci, 0)
    row = lax.broadcasted_iota(jnp.int32, (C, 1), 0)

    heads = range(GLA_HEADS)
    ks = [slice(h * GLA_DK, (h + 1) * GLA_DK) for h in heads]
    vs = [slice(h * GLA_DV, (h + 1) * GLA_DV) for h in heads]
    rs = [slice(u * C, (u + 1) * C) for u in range(nsub)]
    items = [(u, h) for u in range(nsub) for h in heads]

    expo, q, k, v, scores = {}, {}, {}, {}, {}
    for u, h in items:
        la = la_ref[rs[u], ks[h]]
        la_hi = la.astype(BF16)
        la_lo = (la - la_hi.astype(F32)).astype(BF16)
        expo[u, h] = jnp.exp(_dot(m, la_hi) + _dot(m, la_lo))
    for u, h in items:
        q[u, h] = q_ref[rs[u], ks[h]].astype(F32) * (GLA_DK ** -0.5)
        k[u, h] = k_ref[rs[u], ks[h]].astype(F32)
        v[u, h] = v_ref[rs[u], vs[h]].astype(BF16)
        scores[u, h] = jnp.where(ri == ci, _dot_nt(q[u, h].astype(BF16), k[u, h].astype(BF16)), 0.0)
    for l in range(levels):
        half = C >> (l + 1)
        shift = int(math.log2(half))
        qside = (row & half) != 0
        keep = (xor >> shift) == 1
        for it in items:
            x = (jnp.where(qside, q[it], k[it]) * expo[it][l * C:(l + 1) * C]).astype(BF16)
            scores[it] = scores[it] + jnp.where(keep, _dot_nt(x, x), 0.0)

    state = [state_ref[h] for h in heads]
    outs = {}
    for u, h in items:
        e_b = expo[u, h][levels * C:(levels + 1) * C]
        e_rev = expo[u, h][(levels + 1) * C:(levels + 2) * C]
        e_last = e_b[C - 1:C, :]
        outs[u, h] = (_dot(scores[u, h].astype(BF16), v[u, h])
                      + _dot_nt((q[u, h] * e_b).astype(BF16), state[h].astype(BF16)))
        state[h] = state[h] * e_last + _dot_tn(v[u, h], (k[u, h] * e_rev).astype(BF16))
    for h in heads:
        state_ref[h] = state[h]

    for u, h in items:
        o = outs[u, h]
        o = o * lax.rsqrt(jnp.mean(o * o, axis=-1, keepdims=True) + LN_EPS) * gn_ref[...]
        o_ref[rs[u], vs[h]] = (_silu(g_ref[rs[u], vs[h]].astype(F32)) * o).astype(o_ref.dtype)


def _gla(h_main, log_a, g_norm):
    s = h_main.shape[0]
    C = GLA_CHUNK
    m = jnp.asarray(_gla_decay_matrix(C), dtype=BF16)
    nrow = m.shape[0]
    rows = min(s, GLA_STEP_CHUNKS * C)
    wk = GLA_HEADS * GLA_DK
    wv = GLA_HEADS * GLA_DV
    return pl.pallas_call(
        _gla_kernel,
        out_shape=jax.ShapeDtypeStruct((s, wv), BF16),
        grid=(s // rows,),
        in_specs=[pl.BlockSpec((nrow, C), lambda c: (0, 0)),
                  pl.BlockSpec((rows, wk), lambda c: (c, GQ_OFF // wk)),
                  pl.BlockSpec((rows, wk), lambda c: (c, GK_OFF // wk)),
                  pl.BlockSpec((rows, wv), lambda c: (c, GV_OFF // wv)),
                  pl.BlockSpec((rows, wv), lambda c: (c, GG_OFF // wv)),
                  pl.BlockSpec((rows, wk), lambda c: (c, 0)),
                  pl.BlockSpec((1, GLA_DV), lambda c: (0, 0))],
        out_specs=pl.BlockSpec((rows, wv), lambda c: (c, 0)),
        scratch_shapes=[pltpu.VMEM((GLA_HEADS, GLA_DV, GLA_DK), F32)],
        compiler_params=_cparams(("arbitrary",)),
        name="gla",
    )(m, h_main, h_main, h_main, h_main, log_a, g_norm)


def _mixout_kernel(ret_ref, gla_ref, x_ref, w_ref, g_ref, b_ref, o_ref, wb_ref):
    nr = ret_ref.shape[1]

    @pl.when(pl.program_id(0) == 0)
    def _():
        wb_ref[...] = w_ref[...].astype(BF16)

    tm = x_ref.shape[0]
    parts = 2 if tm % 512 == 0 else 1
    rows = [slice(p * (tm // parts), (p + 1) * (tm // parts)) for p in range(parts)]
    mix = [_dot(ret_ref[r, :], wb_ref[:nr, :]) + _dot(gla_ref[r, :], wb_ref[nr:, :]) for r in rows]
    for r, m in zip(rows, mix):
        o_ref[r, :] = _layer_norm(DEEPNORM_ALPHA * x_ref[r, :] + m, g_ref[...], b_ref[...])


def _mixout_ln(ret, gla, x2d, w_b, g, b):
    s = x2d.shape[0]
    tm = min(s, 512)
    nr, ng = ret.shape[1], gla.shape[1]
    return pl.pallas_call(
        _mixout_kernel,
        out_shape=jax.ShapeDtypeStruct((s, D_MODEL), F32),
        grid=(s // tm,),
        in_specs=[pl.BlockSpec((tm, nr), lambda i: (i, 0)),
                  pl.BlockSpec((tm, ng), lambda i: (i, 0)),
                  pl.BlockSpec((tm, D_MODEL), lambda i: (i, 0)),
                  pl.BlockSpec((nr + ng, D_MODEL), lambda i: (0, 0), pipeline_mode=pl.Buffered(1)),
                  pl.BlockSpec((1, D_MODEL), lambda i: (0, 0)),
                  pl.BlockSpec((1, D_MODEL), lambda i: (0, 0))],
        out_specs=pl.BlockSpec((tm, D_MODEL), lambda i: (i, 0)),
        scratch_shapes=[pltpu.VMEM((nr + ng, D_MODEL), BF16)],
        compiler_params=_cparams(("arbitrary",)),
        name="mixout_ln1",
    )(ret, gla, x2d, w_b, g, b)


def _kv_kernel(mem_ref, wk_ref, wv_ref, k_ref, v_ref):
    m = mem_ref[...].astype(BF16)
    k_ref[...] = _dot(m, wk_ref[...].astype(BF16)).astype(k_ref.dtype)
    v_ref[...] = _dot(m, wv_ref[...].astype(BF16)).astype(v_ref.dtype)


def _mem_kv(mem_b, wk_b, wv_b):
    tn = 512
    return pl.pallas_call(
        _kv_kernel,
        out_shape=(jax.ShapeDtypeStruct((MEM_LEN, D_MODEL), BF16),
                   jax.ShapeDtypeStruct((MEM_LEN, D_MODEL), BF16)),
        grid=(D_MODEL // tn,),
        in_specs=[pl.BlockSpec((MEM_LEN, D_MODEL), lambda j: (0, 0)),
                  pl.BlockSpec((D_MODEL, tn), lambda j: (0, j)),
                  pl.BlockSpec((D_MODEL, tn), lambda j: (0, j))],
        out_specs=(pl.BlockSpec((MEM_LEN, tn), lambda j: (0, j)),
                   pl.BlockSpec((MEM_LEN, tn), lambda j: (0, j))),
        compiler_params=_cparams(("arbitrary",)),
        name="mem_kv",
    )(mem_b, wk_b, wv_b)


def _cross_kernel(h_ref, wq_ref, k_ref, v_ref, wo_ref, g_ref, b_ref, wr_ref, br_ref,
                  h2_ref, lg_ref):
    tm = h_ref.shape[0]
    parts = 2 if tm % 512 == 0 else 1
    pr = tm // parts
    rows = [slice(p * pr, (p + 1) * pr) for p in range(parts)]
    sls = [slice(hd * MEM_HEAD_DIM, (hd + 1) * MEM_HEAD_DIM) for hd in range(MEM_HEADS)]
    h1 = [h_ref[r, :] for r in rows]
    q = [_dot(h.astype(BF16), wq_ref[...]).astype(BF16) for h in h1]
    o = []
    for p in range(parts):
        outs = []
        for sl in sls:
            s = _dot_nt(q[p][:, sl], k_ref[:, sl]) * (MEM_HEAD_DIM ** -0.5)
            s = s - jnp.max(s, axis=-1, keepdims=True)
            e = jnp.exp(s)
            e = e / jnp.sum(e, axis=-1, keepdims=True)
            outs.append(_dot(e.astype(BF16), v_ref[:, sl]))
        o.append(jnp.concatenate(outs, axis=-1).astype(BF16))
    cross = [_dot(x, wo_ref[...]) for x in o]
    for p in range(parts):
        h2 = _layer_norm(DEEPNORM_ALPHA * h1[p] + cross[p], g_ref[...], b_ref[...])
        h2_ref[rows[p], :] = h2
        lg_ref[rows[p], :] = _dot(h2.astype(BF16), wr_ref[...]) + br_ref[...]


def _cross_attention(h1, wq_b, k, v, wo_b, g, b, w_route, b_route):
    s = h1.shape[0]
    tm = min(s, 512)
    const = lambda shape: pl.BlockSpec(shape, lambda i: (0, 0), pipeline_mode=pl.Buffered(1))
    return pl.pallas_call(
        _cross_kernel,
        out_shape=(jax.ShapeDtypeStruct((s, D_MODEL), F32),
                   jax.ShapeDtypeStruct((s, LANES), F32)),
        grid=(s // tm,),
        in_specs=[pl.BlockSpec((tm, D_MODEL), lambda i: (i, 0)),
                  const((D_MODEL, D_MODEL)),
                  const((MEM_LEN, D_MODEL)),
                  const((MEM_LEN, D_MODEL)),
                  const((D_MODEL, D_MODEL)),
                  const((1, D_MODEL)),
                  const((1, D_MODEL)),
                  const((D_MODEL, LANES)),
                  const((1, LANES))],
        out_specs=(pl.BlockSpec((tm, D_MODEL), lambda i: (i, 0)),
                   pl.BlockSpec((tm, LANES), lambda i: (i, 0))),
        compiler_params=_cparams(("arbitrary",)),
        name="cross_attn_ln2",
    )(h1, wq_b, k, v, wo_b, g, b, w_route, b_route)


def _route_kernel(lg_ref, slot_ref, slot_t_ref, gate_ref, runs_ref, plan_ref, tot_row, tot_col, gstart, rinfo):
    phase = pl.program_id(0)
    i = pl.program_id(1)
    tm = lg_ref.shape[0]
    lane = lax.broadcasted_iota(jnp.int32, (tm, LANES), 1)
    rows = pl.ds(pl.multiple_of(i * tm, tm), tm)

    @pl.when(phase == 0)
    def _():
        neg = -jnp.inf
        logits = lg_ref[...]
        gmask = lane < N_GROUPS
        gl = jnp.where(gmask, logits, neg)
        ge = jnp.exp(gl - jnp.max(gl, axis=-1, keepdims=True))
        pg = ge / jnp.sum(ge, axis=-1, keepdims=True)
        pg_sel = jnp.max(pg, axis=-1, keepdims=True)
        grp = jnp.min(jnp.where((pg == pg_sel) & gmask, lane, LANES), axis=-1, keepdims=True)

        fl_lane = lane - N_GROUPS
        fmask = (fl_lane >= 0) & (fl_lane < N_EXPERTS) & ((fl_lane >> 3) == grp)
        fl = jnp.where(fmask, logits, neg)
        fe = jnp.exp(fl - jnp.max(fl, axis=-1, keepdims=True))
        fp = fe / jnp.sum(fe, axis=-1, keepdims=True)
        p1 = jnp.max(fp, axis=-1, keepdims=True)
        i1 = jnp.min(jnp.where((fp == p1) & fmask, lane, LANES), axis=-1, keepdims=True)
        rest = fmask & (lane != i1)
        fp2 = jnp.where(rest, fp, -1.0)
        p2 = jnp.max(fp2, axis=-1, keepdims=True)
        i2 = jnp.min(jnp.where((fp2 == p2) & rest, lane, LANES), axis=-1, keepdims=True)
        psum = p1 + p2
        picks = ((i1 - N_GROUPS).astype(F32), (i2 - N_GROUPS).astype(F32),
                 pg_sel * p1 / psum, pg_sel * p2 / psum)
        info = jnp.zeros((tm, LANES), F32)
        for j, val in enumerate(picks):
            info = jnp.where(lane == j, val, info)
        rinfo[rows, :] = info

    info = rinfo[rows, :]
    oh1 = lane == info[:, 0:1].astype(jnp.int32)
    oh2 = lane == info[:, 1:2].astype(jnp.int32)
    gate1 = info[:, 2:3]
    gate2 = info[:, 3:4]
    oh = (jnp.where(oh1, 1.0, 0.0) + jnp.where(oh2, 1.0, 0.0)).astype(BF16)
    ones = jnp.ones((tm, LANES), BF16)

    def align_up(v, a):
        return jnp.floor((v + (a - 1.0)) * (1.0 / a)) * a

    run_row = align_up(_dot_tn(ones, oh)[0:SUBLANES, :], RUN_ALIGN)
    rr = lax.broadcasted_iota(jnp.int32, (LANES, LANES), 0)
    cc = lax.broadcasted_iota(jnp.int32, (LANES, LANES), 1)
    strict_upper = jnp.where(rr < cc, 1.0, 0.0).astype(BF16)

    @pl.when((phase == 0) & (i == 0))
    def _():
        tot_row[...] = jnp.zeros_like(tot_row)
        tot_col[...] = jnp.zeros_like(tot_col)

    @pl.when(phase == 0)
    def _():
        tot_row[...] += run_row
        tot_col[...] += align_up(_dot_tn(oh, ones), RUN_ALIGN)

    @pl.when((phase == 1) & (i == 0))
    def _():
        nblk_row = align_up(tot_row[...], MOE_BLK) * (1.0 / MOE_BLK)
        nblk_col = align_up(tot_col[...], MOE_BLK) * (1.0 / MOE_BLK)
        lower_incl = jnp.where(cc <= rr, 1.0, 0.0).astype(BF16)
        base = _dot(nblk_row.astype(BF16), strict_upper) * float(MOE_BLK)
        ends = _dot(lower_incl, nblk_col.astype(BF16))
        expert_rows = rr < N_EXPERTS
        be = jnp.sum(jnp.where(expert_rows & (ends <= cc.astype(F32)), 1.0, 0.0), axis=0, keepdims=True)
        be = jnp.minimum(be, N_EXPERTS - 1.0)
        total = jnp.sum(jnp.where(expert_rows, nblk_col, 0.0), axis=0, keepdims=True)
        sub = lax.broadcasted_iota(jnp.int32, plan_ref.shape, 0)
        plan = jnp.where(sub == 0, jnp.broadcast_to(be, plan_ref.shape),
                         jnp.where(sub == 1, jnp.broadcast_to(total, plan_ref.shape),
                                   jnp.where(sub == 2, base + tot_row[...],
                                             jnp.where(sub == 3, nblk_row,
                                                       nblk_row * float(MOE_BLK) - tot_row[...]))))
        plan_ref[...] = plan.astype(jnp.int32)
        gstart[...] = base

    @pl.when(phase == 1)
    def _():
        off = _dot((run_row * (1.0 / RUN_ALIGN)).astype(BF16), strict_upper) * float(RUN_ALIGN)
        tr = lax.broadcasted_iota(jnp.int32, (tm, tm), 0)
        tc = lax.broadcasted_iota(jnp.int32, (tm, tm), 1)
        strict_lower = jnp.where(tc < tr, 1.0, 0.0).astype(BF16)
        slot = _dot(strict_lower, oh) + off[0:1, :]
        s1 = jnp.sum(jnp.where(oh1, slot, 0.0), axis=-1, keepdims=True)
        s2 = jnp.sum(jnp.where(oh2, slot, 0.0), axis=-1, keepdims=True)
        slots = jnp.where(lane == 0, s1, jnp.where(lane == 1, s2, 0.0))
        slot_ref[...] = slots.astype(jnp.int32)
        slot_t_ref[...] = jnp.transpose(slots)[0:SUBLANES, :].astype(jnp.int32)
        gate_ref[...] = jnp.where(lane == 0, gate1, jnp.where(lane == 1, gate2, 0.0))
        sub = lax.broadcasted_iota(jnp.int32, runs_ref.shape, 0)
        runs = jnp.where(sub == 0, gstart[...], jnp.where(sub == 1, off, run_row))
        runs_ref[...] = runs.astype(jnp.int32)
        gstart[...] += run_row


def _route(logits):
    t = logits.shape[0]
    tm = min(t, TILE_TOK)
    nt = t // tm
    step = lambda p, i: (i * p, 0)
    return pl.pallas_call(
        _route_kernel,
        out_shape=(jax.ShapeDtypeStruct((t, LANES), jnp.int32),
                   jax.ShapeDtypeStruct((nt * SUBLANES, tm), jnp.int32),
                   jax.ShapeDtypeStruct((t, LANES), F32),
                   jax.ShapeDtypeStruct((nt * SUBLANES, LANES), jnp.int32),
                   jax.ShapeDtypeStruct((SUBLANES, LANES), jnp.int32)),
        grid=(2, nt),
        in_specs=[pl.BlockSpec((tm, LANES), lambda p, i: (i, 0))],
        out_specs=(pl.BlockSpec((tm, LANES), step),
                   pl.BlockSpec((SUBLANES, tm), step),
                   pl.BlockSpec((tm, LANES), step),
                   pl.BlockSpec((SUBLANES, LANES), step),
                   pl.BlockSpec((SUBLANES, LANES), lambda p, i: (0, 0))),
        scratch_shapes=[pltpu.VMEM((SUBLANES, LANES), F32),
                        pltpu.VMEM((LANES, LANES), F32),
                        pltpu.VMEM((SUBLANES, LANES), F32),
                        pltpu.VMEM((t, LANES), F32)],
        compiler_params=_cparams(("arbitrary", "arbitrary")),
        name="moe_route",
    )(logits)


def _run_copies(rg_ref, ro_ref, rn_ref, tile, make_copy):
    def each(action):
        for e in range(N_EXPERTS):
            j = tile * N_EXPERTS + e
            n = rn_ref[j]

            @pl.when(n > 0)
            def _():
                action(make_copy(pl.multiple_of(rg_ref[j], RUN_ALIGN), pl.multiple_of(ro_ref[j], RUN_ALIGN),
                                 pl.multiple_of(n, RUN_ALIGN)))
    return each


def _dispatch_kernel(rg_ref, ro_ref, rn_ref, zrow_ref, zlen_ref, nu_ref, x_ref, slot_t_ref, xb_ref,
                     sorted_ref, zero_ref, sems, zsem):
    i = pl.program_id(0)
    n = pl.num_programs(0)
    tm = x_ref.shape[0]
    nb = xb_ref.shape[0] // MOE_BLK
    buf = i % 2

    def zero_copy(row, size):
        return pltpu.make_async_copy(
            zero_ref.at[pl.ds(0, size), :], xb_ref.at[pl.ds(pl.multiple_of(row, RUN_ALIGN), size), :], zsem)

    def pad_copy(e):
        return zero_copy(zrow_ref[e], pl.multiple_of(zlen_ref[e], RUN_ALIGN))

    def runs_of(tile, slot):
        def make_copy(g, o, size):
            return pltpu.make_async_copy(sorted_ref.at[slot, pl.ds(o, size), :],
                                         xb_ref.at[pl.ds(g, size), :], sems.at[slot])
        return _run_copies(rg_ref, ro_ref, rn_ref, tile, make_copy)

    @pl.when(i == 0)
    def _():
        zero_ref[...] = jnp.zeros_like(zero_ref)

        def issue_zero(e, carry):
            @pl.when(zlen_ref[e] > 0)
            def _():
                pad_copy(e).start()
            return carry
        lax.fori_loop(0, N_EXPERTS, issue_zero, 0)
        lax.fori_loop(nu_ref[0], nb, lambda b, c: (zero_copy(b * MOE_BLK, MOE_BLK).start(), c)[1], 0)

    @pl.when(i >= 2)
    def _():
        runs_of(i - 2, buf)(lambda c: c.wait())

    x = x_ref[...].astype(BF16)
    s1 = slot_t_ref[0:1, :]
    s2 = slot_t_ref[1:2, :]
    for rc in range(TILE_SLOTS // SORT_ROWS):
        r = lax.broadcasted_iota(jnp.int32, (SORT_ROWS, tm), 0) + rc * SORT_ROWS
        perm = jnp.where((r == s1) | (r == s2), 1.0, 0.0).astype(BF16)
        sorted_ref[buf, rc * SORT_ROWS:(rc + 1) * SORT_ROWS, :] = _dot(perm, x).astype(BF16)

    runs_of(i, buf)(lambda c: c.start())

    @pl.when(i == n - 1)
    def _():
        @pl.when(i >= 1)
        def _():
            runs_of(i - 1, 1 - buf)(lambda c: c.wait())
        runs_of(i, buf)(lambda c: c.wait())

        def wait_zero(e, carry):
            @pl.when(zlen_ref[e] > 0)
            def _():
                pad_copy(e).wait()
            return carry
        lax.fori_loop(0, N_EXPERTS, wait_zero, 0)
        lax.fori_loop(nu_ref[0], nb, lambda b, c: (zero_copy(0, MOE_BLK).wait(), c)[1], 0)


def _dispatch(rg, ro, rn, zrow, zlen, n_used, h2, slot_t, n_rows):
    t = h2.shape[0]
    tm = min(t, TILE_TOK)
    grid_spec = pltpu.PrefetchScalarGridSpec(
        num_scalar_prefetch=6,
        grid=(t // tm,),
        in_specs=[pl.BlockSpec((tm, D_MODEL), lambda i, *_: (i, 0)),
                  pl.BlockSpec((SUBLANES, tm), lambda i, *_: (i, 0))],
        out_specs=pl.BlockSpec(memory_space=pl.ANY),
        scratch_shapes=[pltpu.VMEM((2, TILE_SLOTS, D_MODEL), BF16),
                        pltpu.VMEM((MOE_BLK, D_MODEL), BF16),
                        pltpu.SemaphoreType.DMA((2,)),
                        pltpu.SemaphoreType.DMA(())],
    )
    return pl.pallas_call(
        _dispatch_kernel,
        out_shape=jax.ShapeDtypeStruct((n_rows, D_MODEL), BF16),
        grid_spec=grid_spec,
        compiler_params=_cparams(("arbitrary",)),
        name="moe_dispatch",
    )(rg, ro, rn, zrow, zlen, n_used, h2, slot_t)


def _expert_kernel(be_ref, nu_ref, nblk_ref, x_ref, wg_ref, wu_ref, wd_ref, y_ref,
                   wg_f, wu_f, wd_f, wg_b, wu_b, wd_b, sems, cur_ref):
    step = pl.program_id(0)
    n_used = nu_ref[0]

    def weight_copies(expert, slot):
        half = EXPERT_FF // 2
        return [pltpu.make_async_copy(wg_ref.at[expert], wg_f.at[slot], sems.at[slot]),
                pltpu.make_async_copy(wu_ref.at[expert], wu_f.at[slot], sems.at[slot]),
                pltpu.make_async_copy(wd_ref.at[expert, :half], wd_f.at[slot, :half], sems.at[slot]),
                pltpu.make_async_copy(wd_ref.at[expert, half:], wd_f.at[slot, half:], sems.at[slot])]

    def start_weights(expert, slot):
        for c in weight_copies(expert, slot):
            c.start()

    def after(blk):
        return blk + nblk_ref[be_ref[blk]]

    @pl.when(step == 0)
    def _():
        cur_ref[0] = 0
        start_weights(be_ref[0], 0)

        @pl.when(after(0) < n_used)
        def _():
            start_weights(be_ref[after(0)], 1)

    for sub in range(EXPERT_STEP_BLKS):
        b = step * EXPERT_STEP_BLKS + sub
        used = b < n_used
        e = be_ref[b]
        prev = be_ref[jnp.maximum(b - 1, 0)]
        fresh = used & ((b == 0) | (e != prev))
        rows = slice(sub * MOE_BLK, (sub + 1) * MOE_BLK)

        @pl.when(fresh)
        def _():
            slot = cur_ref[0]
            for c in weight_copies(e, slot):
                c.wait()
            wg_b[...] = wg_f[slot].astype(BF16)
            wu_b[...] = wu_f[slot].astype(BF16)
            wd_b[...] = wd_f[slot].astype(BF16)
            nxt = after(b)

            @pl.when(nxt < n_used)
            def _():
                nxt2 = after(nxt)

                @pl.when(nxt2 < n_used)
                def _():
                    start_weights(be_ref[nxt2], slot)
            cur_ref[0] = 1 - slot

        @pl.when(used)
        def _():
            x = x_ref[rows, :]
            hid = _silu(_dot(x, wg_b[...])) * _dot(x, wu_b[...])
            y_ref[rows, :] = _dot(hid.astype(BF16), wd_b[...]).astype(y_ref.dtype)

        @pl.when(jnp.logical_not(used))
        def _():
            y_ref[rows, :] = jnp.zeros((MOE_BLK, D_MODEL), y_ref.dtype)


def _experts(block_e, n_used, nblk, xb, w_gate, w_up, w_down):
    step_rows = EXPERT_STEP_BLKS * MOE_BLK
    any_space = pl.BlockSpec(memory_space=pl.ANY)
    last_step = lambda nu: (nu[0] - 1) // EXPERT_STEP_BLKS
    grid_spec = pltpu.PrefetchScalarGridSpec(
        num_scalar_prefetch=3,
        grid=(xb.shape[0] // step_rows,),
        in_specs=[pl.BlockSpec((step_rows, D_MODEL), lambda s, be, nu, nk: (jnp.minimum(s, last_step(nu)), 0)),
                  any_space, any_space, any_space],
        out_specs=pl.BlockSpec((step_rows, D_MODEL), lambda s, be, nu, nk: (s, 0)),
        scratch_shapes=[pltpu.VMEM((2, D_MODEL, EXPERT_FF), F32),
                        pltpu.VMEM((2, D_MODEL, EXPERT_FF), F32),
                        pltpu.VMEM((2, EXPERT_FF, D_MODEL), F32),
                        pltpu.VMEM((D_MODEL, EXPERT_FF), BF16),
                        pltpu.VMEM((D_MODEL, EXPERT_FF), BF16),
                        pltpu.VMEM((EXPERT_FF, D_MODEL), BF16),
                        pltpu.SemaphoreType.DMA((2,)),
                        pltpu.SMEM((1,), jnp.int32)],
    )
    return pl.pallas_call(
        _expert_kernel,
        out_shape=jax.ShapeDtypeStruct(xb.shape, BF16),
        grid_spec=grid_spec,
        compiler_params=_cparams(("arbitrary",)),
        name="moe_experts",
    )(block_e, n_used, nblk, xb, w_gate, w_up, w_down)


def _combine_kernel(rg_ref, ro_ref, rn_ref, yb_ref, h2_ref, slot_ref, gate_ref, g_ref, b_ref, o_ref,
                    ybuf, sems):
    i = pl.program_id(0)
    n = pl.num_programs(0)
    tm = h2_ref.shape[0]
    buf = i % 2

    def runs_of(tile, slot):
        def make_copy(g, o, size):
            return pltpu.make_async_copy(yb_ref.at[pl.ds(g, size), :],
                                         ybuf.at[slot, pl.ds(o, size), :], sems.at[slot])
        return _run_copies(rg_ref, ro_ref, rn_ref, tile, make_copy)

    @pl.when(i == 0)
    def _():
        ybuf[...] = jnp.zeros_like(ybuf)
        runs_of(0, 0)(lambda c: c.start())

    @pl.when(i + 1 < n)
    def _():
        runs_of(i + 1, 1 - buf)(lambda c: c.start())

    runs_of(i, buf)(lambda c: c.wait())

    lane = lax.broadcasted_iota(jnp.int32, (tm, TILE_SLOTS), 1)
    slots = slot_ref[...]
    gate = gate_ref[...]
    y = ybuf[buf]
    sel = jnp.where(lane == slots[:, 0:1], gate[:, 0:1],
                    jnp.where(lane == slots[:, 1:2], gate[:, 1:2], 0.0)).astype(BF16)
    parts = 2 if tm % 512 == 0 else 1
    rows = [slice(p * (tm // parts), (p + 1) * (tm // parts)) for p in range(parts)]
    ffn = [_dot(sel[r, :], y) for r in rows]
    for r, f in zip(rows, ffn):
        o_ref[r, :] = _layer_norm(DEEPNORM_ALPHA * h2_ref[r, :] + f, g_ref[...], b_ref[...])


def _combine_ln(rg, ro, rn, yb, h2, slots, gate, g, b):
    t = h2.shape[0]
    tm = min(t, TILE_TOK)
    grid_spec = pltpu.PrefetchScalarGridSpec(
        num_scalar_prefetch=3,
        grid=(t // tm,),
        in_specs=[pl.BlockSpec(memory_space=pl.ANY),
                  pl.BlockSpec((tm, D_MODEL), lambda i, *_: (i, 0)),
                  pl.BlockSpec((tm, LANES), lambda i, *_: (i, 0)),
                  pl.BlockSpec((tm, LANES), lambda i, *_: (i, 0)),
                  pl.BlockSpec((1, D_MODEL), lambda i, *_: (0, 0)),
                  pl.BlockSpec((1, D_MODEL), lambda i, *_: (0, 0))],
        out_specs=pl.BlockSpec((tm, D_MODEL), lambda i, *_: (i, 0)),
        scratch_shapes=[pltpu.VMEM((2, TILE_SLOTS, D_MODEL), BF16),
                        pltpu.SemaphoreType.DMA((2,))],
    )
    return pl.pallas_call(
        _combine_kernel,
        out_shape=jax.ShapeDtypeStruct((t, D_MODEL), F32),
        grid_spec=grid_spec,
        compiler_params=_cparams(("arbitrary",)),
        name="moe_combine_ln3",
    )(rg, ro, rn, yb, h2, slots, gate, g, b)


def _mixer(x2d, positions, w_in, w_gla_a2, b_gla_a, g_gla_norm):
    s = x2d.shape[0]
    half = RET_DK // 2
    inv_freq = (ROPE_BASE ** (-jnp.arange(half, dtype=F32) / half)).reshape(1, half)
    cos, sin = _rope_table(positions.reshape(s, 1), inv_freq)
    w_in_t = jnp.swapaxes(w_in, 0, 1)
    w_lr_t = jnp.pad(w_in_t[GLR_OFF:].astype(BF16), ((0, LANES - GLA_LOWRANK), (0, 0)))
    w_a2 = jnp.pad(w_gla_a2.astype(BF16), ((0, LANES - GLA_LOWRANK), (0, 0)))
    h_main, log_a = _proj_in(x2d, w_in_t, w_lr_t, w_a2, b_gla_a.reshape(1, -1))
    log_gamma = jnp.log1p(-jnp.exp2(-5.0 - jnp.arange(RET_HEADS, dtype=F32)))
    ret = _retention(h_main, cos, sin, log_gamma)
    gla = _gla(h_main, log_a, g_gla_norm.reshape(1, -1))
    return ret, gla


def _moe(h2, logits, w_gate, w_up, w_down, g, b):
    t = h2.shape[0]
    nt = t // min(t, TILE_TOK)
    slots, slot_t, gate, runs, plan = _route(logits)
    runs = runs.reshape(nt, SUBLANES, LANES)[:, :3, :N_EXPERTS]
    rg, ro, rn = (runs[:, j, :].reshape(-1) for j in range(3))
    max_rows = 2 * t + nt * N_EXPERTS * (RUN_ALIGN - 1) + N_EXPERTS * (MOE_BLK - 1)
    nb = -(-max_rows // (MOE_BLK * EXPERT_STEP_BLKS)) * EXPERT_STEP_BLKS
    block_e, n_used = plan[0, :nb], plan[1, :1]
    pad_row, nblk, pad_len = (plan[j, :N_EXPERTS] for j in (2, 3, 4))
    xb = _dispatch(rg, ro, rn, pad_row, pad_len, n_used, h2, slot_t, nb * MOE_BLK)
    yb = _experts(block_e, n_used, nblk, xb, w_gate, w_up, w_down)
    return _combine_ln(rg, ro, rn, yb, h2, slots, gate, g, b)


def kernel(x, mem, positions, w_in, w_gla_a2, b_gla_a, g_gla_norm, w_mix_out, ln1_g, ln1_b, w_mq, w_mk, w_mv, w_mo, ln2_g, ln2_b, w_route_group, b_route_group, w_route_expert, b_route_expert, w_exp_gate, w_exp_up, w_exp_down, ln3_g, ln3_b):
    bsz, s, d = x.shape
    assert bsz == 1 and d == D_MODEL
    x2d = x.reshape(s, d)
    row = lambda v: v.reshape(1, -1)

    ret, gla = _mixer(x2d, positions, w_in[0], w_gla_a2[0], b_gla_a[0], g_gla_norm[0])
    h1 = _mixout_ln(ret, gla, x2d, w_mix_out[0], row(ln1_g[0]), row(ln1_b[0]))

    k, v = _mem_kv(mem[0], w_mk[0], w_mv[0])
    n_route = N_GROUPS + N_EXPERTS
    w_route = jnp.pad(jnp.concatenate([w_route_group[0], w_route_expert[0]], axis=1).astype(BF16),
                      ((0, 0), (0, LANES - n_route)))
    b_route = jnp.pad(jnp.concatenate([b_route_group[0], b_route_expert[0].reshape(-1)]),
                      (0, LANES - n_route)).reshape(1, LANES)
    h2, logits = _cross_attention(h1, w_mq[0].astype(BF16), k, v, w_mo[0].astype(BF16),
                                  row(ln2_g[0]), row(ln2_b[0]), w_route, b_route)

    out = _moe(h2, logits, w_exp_gate[0], w_exp_up[0], w_exp_down[0],
               row(ln3_g[0]), row(ln3_b[0]))
    return out.reshape(bsz, s, d)
```

```python
import math

import jax
import jax.numpy as jnp
from jax import lax
from jax.experimental import pallas as pl
from jax.experimental.pallas import tpu as pltpu

F32 = jnp.float32
BF16 = jnp.bfloat16

D_MODEL = 2048
MEM_LEN = 256
RET_HEADS = 4
RET_DK = 256
RET_DV = 256
GLA_HEADS = 4
GLA_DK = 128
GLA_DV = 256
GLA_LOWRANK = 16
GLA_TAU = 16.0
ROPE_BASE = 10000.0
MEM_HEADS = 4
MEM_HEAD_DIM = D_MODEL // MEM_HEADS
N_GROUPS = 4
EXPERTS_PER_GROUP = 8
N_EXPERTS = N_GROUPS * EXPERTS_PER_GROUP
EXPERT_FF = 512
LN_EPS = 1e-5
DEPTH = 1
DEEPNORM_ALPHA = (2 * DEPTH) ** 0.25

RQ_OFF, RK_OFF, RV_OFF, RG_OFF = 0, 1024, 2048, 3072
GQ_OFF, GK_OFF, GV_OFF, GG_OFF, GLR_OFF = 4096, 4608, 5120, 6144, 7168
IN_MAIN = 7168

LANES = 128
SUBLANES = 8
ROUTE_ROWS = 40
RET_CHUNK = 256
RET_STEP_CHUNKS = 2
GLA_CHUNK = 128
GLA_LEVELS = 7
GLA_STEP_CHUNKS = 4
MOE_BLK = 256
EXPERT_STEP_BLKS = 4
TILE_TOK = 512
RUN_ALIGN = 16
TILE_SLOTS = 2 * TILE_TOK + N_EXPERTS * RUN_ALIGN
SORT_ROWS = 256
PROJ_K_CHUNK = 512
VMEM_LIMIT = 56 * 1024 * 1024


def _cparams(sem):
    return pltpu.CompilerParams(dimension_semantics=sem, vmem_limit_bytes=VMEM_LIMIT)


def _layer_norm(y, g, b):
    mu = jnp.mean(y, axis=-1, keepdims=True)
    d = y - mu
    var = jnp.mean(d * d, axis=-1, keepdims=True)
    return d * lax.rsqrt(var + LN_EPS) * g + b


def _silu(x):
    return x / (1.0 + jnp.exp(-x))


def _dot(a, b):
    return jnp.dot(a, b, preferred_element_type=F32)


def _dot_nt(a, b):
    return lax.dot_general(a, b, (((1,), (1,)), ((), ())), preferred_element_type=F32)


def _dot_tn(a, b):
    return lax.dot_general(a, b, (((0,), (0,)), ((), ())), preferred_element_type=F32)


def _rope_kernel(pos_ref, invf_ref, cos_ref, sin_ref):
    ang = pos_ref[...].astype(F32) * invf_ref[...]
    cos_ref[...] = jnp.cos(ang)
    sin_ref[...] = jnp.sin(ang)


def _rope_table(pos_col, inv_freq):
    s = pos_col.shape[0]
    tm = min(s, 1024)
    half = inv_freq.shape[1]
    return pl.pallas_call(
        _rope_kernel,
        out_shape=(jax.ShapeDtypeStruct((s, half), F32), jax.ShapeDtypeStruct((s, half), F32)),
        grid=(s // tm,),
        in_specs=[pl.BlockSpec((tm, 1), lambda i: (i, 0)),
                  pl.BlockSpec((1, half), lambda i: (0, 0))],
        out_specs=(pl.BlockSpec((tm, half), lambda i: (i, 0)),
                   pl.BlockSpec((tm, half), lambda i: (i, 0))),
        compiler_params=_cparams(("arbitrary",)),
        name="rope_table",
    )(pos_col, inv_freq)


def _proj_in_kernel(x_ref, wt_ref, wlr_ref, wa2_ref, ba_ref, o_ref, la_ref, xb_ref):
    @pl.when(pl.program_id(1) == 0)
    def _():
        xb_ref[...] = x_ref[...].astype(BF16)
        glr = _dot_nt(xb_ref[...], wlr_ref[...])
        z = _dot(glr.astype(BF16), wa2_ref[...]) + ba_ref[...]
        la_ref[...] = (jnp.minimum(z, 0.0) - jnp.log(1.0 + jnp.exp(-jnp.abs(z)))) / GLA_TAU

    acc = None
    for c in range(D_MODEL // PROJ_K_CHUNK):
        ck = slice(c * PROJ_K_CHUNK, (c + 1) * PROJ_K_CHUNK)
        part = _dot_nt(xb_ref[:, ck], wt_ref[:, ck].astype(BF16))
        acc = part if acc is None else acc + part
    o_ref[...] = acc.astype(o_ref.dtype)


def _proj_in(x2d, w_in_t, w_lr_t, w_a2, b_a):
    s = x2d.shape[0]
    tm = min(s, 1024)
    tn = 1024
    n = GLA_HEADS * GLA_DK
    return pl.pallas_call(
        _proj_in_kernel,
        out_shape=(jax.ShapeDtypeStruct((s, IN_MAIN), BF16), jax.ShapeDtypeStruct((s, n), F32)),
        grid=(s // tm, IN_MAIN // tn),
        in_specs=[pl.BlockSpec((tm, D_MODEL), lambda i, j: (i, 0)),
                  pl.BlockSpec((tn, D_MODEL), lambda i, j: (j, 0)),
                  pl.BlockSpec((LANES, D_MODEL), lambda i, j: (0, 0)),
                  pl.BlockSpec((LANES, n), lambda i, j: (0, 0)),
                  pl.BlockSpec((1, n), lambda i, j: (0, 0))],
        out_specs=(pl.BlockSpec((tm, tn), lambda i, j: (i, j)),
                   pl.BlockSpec((tm, n), lambda i, j: (i, 0))),
        scratch_shapes=[pltpu.VMEM((tm, D_MODEL), BF16)],
        compiler_params=_cparams(("arbitrary", "arbitrary")),
        name="proj_in",
    )(x2d, w_in_t, w_lr_t, w_a2, b_a)


def _rotary(t, cos, sin):
    half = t.shape[-1] // 2
    t1, t2 = t[:, :half], t[:, half:]
    return jnp.concatenate([t1 * cos - t2 * sin, t1 * sin + t2 * cos], axis=-1)


def _retention_kernel(lg_ref, q_ref, k_ref, v_ref, g_ref, cos_ref, sin_ref, o_ref, state_ref, intra_ref, dq_ref, dk_ref):
    c = pl.program_id(0)
    C = intra_ref.shape[1]
    nsub = q_ref.shape[0] // C

    @pl.when(c == 0)
    def _():
        state_ref[...] = jnp.zeros_like(state_ref)
        ri = lax.broadcasted_iota(jnp.int32, (C, C), 0)
        ci = lax.broadcasted_iota(jnp.int32, (C, C), 1)
        rel = jnp.maximum(ri - ci, 0).astype(F32)
        n = lax.broadcasted_iota(jnp.int32, (C, RET_DK), 0).astype(F32)
        for h in range(RET_HEADS):
            lg = lg_ref[h]
            intra_ref[h] = jnp.where(ri >= ci, jnp.exp(lg * rel), 0.0)
            dq_ref[h] = jnp.exp(lg * (n + 1.0))
            dk_ref[h] = jnp.exp(lg * (C - 1.0 - n))

    state = [state_ref[h] for h in range(RET_HEADS)]
    for u in range(nsub):
        rows = slice(u * C, (u + 1) * C)
        cos = cos_ref[rows, :]
        sin = sin_ref[rows, :]
        for h in range(RET_HEADS):
            sl = slice(h * RET_DK, (h + 1) * RET_DK)
            q = _rotary(q_ref[rows, sl].astype(F32), cos, sin)
            k = _rotary(k_ref[rows, sl].astype(F32), cos, sin) * (RET_DK ** -0.5)
            v = v_ref[rows, sl].astype(BF16)
            decay_chunk = jnp.exp(lg_ref[h] * C)

            scores = _dot_nt(q.astype(BF16), k.astype(BF16)) * intra_ref[h]
            o = _dot(scores.astype(BF16), v) + _dot((q * dq_ref[h]).astype(BF16), state[h].astype(BF16))
            state[h] = decay_chunk * state[h] + _dot_tn((k * dk_ref[h]).astype(BF16), v)

            mu = jnp.mean(o, axis=-1, keepdims=True)
            d = o - mu
            var = jnp.mean(d * d, axis=-1, keepdims=True)
            o = d * lax.rsqrt(var + LN_EPS)
            o_ref[rows, sl] = (_silu(g_ref[rows, sl].astype(F32)) * o).astype(o_ref.dtype)
    for h in range(RET_HEADS):
        state_ref[h] = state[h]


def _retention(h_main, cos, sin, log_gamma):
    s = h_main.shape[0]
    C = min(RET_CHUNK, s)
    rows = min(s, RET_STEP_CHUNKS * C)
    w = RET_HEADS * RET_DK
    col = lambda off: (lambda c, lg: (c, off // w))
    grid_spec = pltpu.PrefetchScalarGridSpec(
        num_scalar_prefetch=1,
        grid=(s // rows,),
        in_specs=[pl.BlockSpec((rows, w), col(RQ_OFF)),
                  pl.BlockSpec((rows, w), col(RK_OFF)),
                  pl.BlockSpec((rows, w), col(RV_OFF)),
                  pl.BlockSpec((rows, w), col(RG_OFF)),
                  pl.BlockSpec((rows, RET_DK // 2), lambda c, lg: (c, 0)),
                  pl.BlockSpec((rows, RET_DK // 2), lambda c, lg: (c, 0))],
        out_specs=pl.BlockSpec((rows, RET_HEADS * RET_DV), lambda c, lg: (c, 0)),
        scratch_shapes=[pltpu.VMEM((RET_HEADS, RET_DK, RET_DV), F32),
                        pltpu.VMEM((RET_HEADS, C, C), F32),
                        pltpu.VMEM((RET_HEADS, C, RET_DK), F32),
                        pltpu.VMEM((RET_HEADS, C, RET_DK), F32)],
    )
    return pl.pallas_call(
        _retention_kernel,
        out_shape=jax.ShapeDtypeStruct((s, RET_HEADS * RET_DV), BF16),
        grid_spec=grid_spec,
        compiler_params=_cparams(("arbitrary",)),
        name="retention",
    )(log_gamma, h_main, h_main, h_main, h_main, cos, sin)


def _gla_decay_matrix(C):
    import numpy as np
    levels = int(math.log2(C))
    r = np.arange(C)[:, None]
    t = np.arange(C)[None, :]
    mats = []
    for l in range(levels):
        blk = C >> l
        half = blk // 2
        m = (r // blk) * blk + half - 1
        qside = (r % blk) >= half
        mats.append(np.where(qside, (t > m) & (t <= r), (t > r) & (t <= m)))
    mats.append(t <= r)
    mats.append(t > r)
    return np.concatenate(mats, axis=0).astype(np.float32)


def _gla_kernel(m_ref, q_ref, k_ref, v_ref, g_ref, la_ref, gn_ref, o_ref, state_ref):
    c = pl.program_id(0)
    C = GLA_CHUNK
    levels = GLA_LEVELS
    nsub = q_ref.shape[0] // C

    @pl.when(c == 0)
    def _():
        state_ref[...] = jnp.zeros_like(state_ref)

    m = m_ref[...]
    ri = lax.broadcasted_iota(jnp.int32, (C, C), 0)
    ci = lax.broadcasted_iota(jnp.int32, (C, C), 1)
    xor = jnp.where(ri > ci, ri ^ ci, 0)
    row = lax.broadcasted_iota(jnp.int32, (C, 1), 0)

    heads = range(GLA_HEADS)
    ks = [slice(h * GLA_DK, (h + 1) * GLA_DK) for h in heads]
    vs = [slice(h * GLA_DV, (h + 1) * GLA_DV) for h in heads]
    rs = [slice(u * C, (u + 1) * C) for u in range(nsub)]
    items = [(u, h) for u in range(nsub) for h in heads]

    expo, q, k, v, scores = {}, {}, {}, {}, {}
    for u, h in items:
        la = la_ref[rs[u], ks[h]]
        la_hi = la.astype(BF16)
        la_lo = (la - la_hi.astype(F32)).astype(BF16)
        expo[u, h] = jnp.exp(_dot(m, la_hi) + _dot(m, la_lo))
    for u, h in items:
        q[u, h] = q_ref[rs[u], ks[h]].astype(F32) * (GLA_DK ** -0.5)
        k[u, h] = k_ref[rs[u], ks[h]].astype(F32)
        v[u, h] = v_ref[rs[u], vs[h]].astype(BF16)
        scores[u, h] = jnp.where(ri == ci, _dot_nt(q[u, h].astype(BF16), k[u, h].astype(BF16)), 0.0)
    for l in range(levels):
        half = C >> (l + 1)
        shift = int(math.log2(half))
        qside = (row & half) != 0
        keep = (xor >> shift) == 1
        for it in items:
            x = (jnp.where(qside, q[it], k[it]) * expo[it][l * C:(l + 1) * C]).astype(BF16)
            scores[it] = scores[it] + jnp.where(keep, _dot_nt(x, x), 0.0)

    state = [state_ref[h] for h in heads]
    outs = {}
    for u, h in items:
        e_b = expo[u, h][levels * C:(levels + 1) * C]
        e_rev = expo[u, h][(levels + 1) * C:(levels + 2) * C]
        e_last = e_b[C - 1:C, :]
        outs[u, h] = (_dot(scores[u, h].astype(BF16), v[u, h])
                      + _dot_nt((q[u, h] * e_b).astype(BF16), state[h].astype(BF16)))
        state[h] = state[h] * e_last + _dot_tn(v[u, h], (k[u, h] * e_rev).astype(BF16))
    for h in heads:
        state_ref[h] = state[h]

    for u, h in items:
        o = outs[u, h]
        o = o * lax.rsqrt(jnp.mean(o * o, axis=-1, keepdims=True) + LN_EPS) * gn_ref[...]
        o_ref[rs[u], vs[h]] = (_silu(g_ref[rs[u], vs[h]].astype(F32)) * o).astype(o_ref.dtype)


def _gla(h_main, log_a, g_norm):
    s = h_main.shape[0]
    C = GLA_CHUNK
    m = jnp.asarray(_gla_decay_matrix(C), dtype=BF16)
    nrow = m.shape[0]
    rows = min(s, GLA_STEP_CHUNKS * C)
    wk = GLA_HEADS * GLA_DK
    wv = GLA_HEADS * GLA_DV
    return pl.pallas_call(
        _gla_kernel,
        out_shape=jax.ShapeDtypeStruct((s, wv), BF16),
        grid=(s // rows,),
        in_specs=[pl.BlockSpec((nrow, C), lambda c: (0, 0)),
                  pl.BlockSpec((rows, wk), lambda c: (c, GQ_OFF // wk)),
                  pl.BlockSpec((rows, wk), lambda c: (c, GK_OFF // wk)),
                  pl.BlockSpec((rows, wv), lambda c: (c, GV_OFF // wv)),
                  pl.BlockSpec((rows, wv), lambda c: (c, GG_OFF // wv)),
                  pl.BlockSpec((rows, wk), lambda c: (c, 0)),
                  pl.BlockSpec((1, GLA_DV), lambda c: (0, 0))],
        out_specs=pl.BlockSpec((rows, wv), lambda c: (c, 0)),
        scratch_shapes=[pltpu.VMEM((GLA_HEADS, GLA_DV, GLA_DK), F32)],
        compiler_params=_cparams(("arbitrary",)),
        name="gla",
    )(m, h_main, h_main, h_main, h_main, log_a, g_norm)


def _mixout_kernel(ret_ref, gla_ref, x_ref, w_ref, g_ref, b_ref, o_ref, wb_ref):
    nr = ret_ref.shape[1]

    @pl.when(pl.program_id(0) == 0)
    def _():
        wb_ref[...] = w_ref[...].astype(BF16)

    tm = x_ref.shape[0]
    parts = 2 if tm % 512 == 0 else 1
    rows = [slice(p * (tm // parts), (p + 1) * (tm // parts)) for p in range(parts)]
    mix = [_dot(ret_ref[r, :], wb_ref[:nr, :]) + _dot(gla_ref[r, :], wb_ref[nr:, :]) for r in rows]
    for r, m in zip(rows, mix):
        o_ref[r, :] = _layer_norm(DEEPNORM_ALPHA * x_ref[r, :] + m, g_ref[...], b_ref[...])


def _mixout_ln(ret, gla, x2d, w_b, g, b):
    s = x2d.shape[0]
    tm = min(s, 512)
    nr, ng = ret.shape[1], gla.shape[1]
    return pl.pallas_call(
        _mixout_kernel,
        out_shape=jax.ShapeDtypeStruct((s, D_MODEL), F32),
        grid=(s // tm,),
        in_specs=[pl.BlockSpec((tm, nr), lambda i: (i, 0)),
                  pl.BlockSpec((tm, ng), lambda i: (i, 0)),
                  pl.BlockSpec((tm, D_MODEL), lambda i: (i, 0)),
                  pl.BlockSpec((nr + ng, D_MODEL), lambda i: (0, 0), pipeline_mode=pl.Buffered(1)),
                  pl.BlockSpec((1, D_MODEL), lambda i: (0, 0)),
                  pl.BlockSpec((1, D_MODEL), lambda i: (0, 0))],
        out_specs=pl.BlockSpec((tm, D_MODEL), lambda i: (i, 0)),
        scratch_shapes=[pltpu.VMEM((nr + ng, D_MODEL), BF16)],
        compiler_params=_cparams(("arbitrary",)),
        name="mixout_ln1",
    )(ret, gla, x2d, w_b, g, b)


def _kv_kernel(mem_ref, wk_ref, wv_ref, k_ref, v_ref):
    m = mem_ref[...].astype(BF16)
    k_ref[...] = _dot(m, wk_ref[...].astype(BF16)).astype(k_ref.dtype)
    v_ref[...] = _dot(m, wv_ref[...].astype(BF16)).astype(v_ref.dtype)


def _mem_kv(mem_b, wk_b, wv_b):
    tn = 512
    return pl.pallas_call(
        _kv_kernel,
        out_shape=(jax.ShapeDtypeStruct((MEM_LEN, D_MODEL), BF16),
                   jax.ShapeDtypeStruct((MEM_LEN, D_MODEL), BF16)),
        grid=(D_MODEL // tn,),
        in_specs=[pl.BlockSpec((MEM_LEN, D_MODEL), lambda j: (0, 0)),
                  pl.BlockSpec((D_MODEL, tn), lambda j: (0, j)),
                  pl.BlockSpec((D_MODEL, tn), lambda j: (0, j))],
        out_specs=(pl.BlockSpec((MEM_LEN, tn), lambda j: (0, j)),
                   pl.BlockSpec((MEM_LEN, tn), lambda j: (0, j))),
        compiler_params=_cparams(("arbitrary",)),
        name="mem_kv",
    )(mem_b, wk_b, wv_b)


def _cross_kernel(h_ref, wq_ref, k_ref, v_ref, wo_ref, g_ref, b_ref, wr_ref, br_ref,
                  h2_ref, lg_ref):
    tm = h_ref.shape[0]
    parts = 2 if tm % 512 == 0 else 1
    pr = tm // parts
    rows = [slice(p * pr, (p + 1) * pr) for p in range(parts)]
    sls = [slice(hd * MEM_HEAD_DIM, (hd + 1) * MEM_HEAD_DIM) for hd in range(MEM_HEADS)]
    h1 = [h_ref[r, :] for r in rows]
    q = [_dot(h.astype(BF16), wq_ref[...]).astype(BF16) for h in h1]
    o = []
    for p in range(parts):
        outs = []
        for sl in sls:
            s = _dot_nt(q[p][:, sl], k_ref[:, sl]) * (MEM_HEAD_DIM ** -0.5)
            s = s - jnp.max(s, axis=-1, keepdims=True)
            e = jnp.exp(s)
            e = e / jnp.sum(e, axis=-1, keepdims=True)
            outs.append(_dot(e.astype(BF16), v_ref[:, sl]))
        o.append(jnp.concatenate(outs, axis=-1).astype(BF16))
    cross = [_dot(x, wo_ref[...]) for x in o]
    for p in range(parts):
        h2 = _layer_norm(DEEPNORM_ALPHA * h1[p] + cross[p], g_ref[...], b_ref[...])
        h2_ref[rows[p], :] = h2
        lg_ref[rows[p], :] = _dot(h2.astype(BF16), wr_ref[...]) + br_ref[...]


def _cross_attention(h1, wq_b, k, v, wo_b, g, b, w_route, b_route):
    s = h1.shape[0]
    tm = min(s, 512)
    const = lambda shape: pl.BlockSpec(shape, lambda i: (0, 0), pipeline_mode=pl.Buffered(1))
    return pl.pallas_call(
        _cross_kernel,
        out_shape=(jax.ShapeDtypeStruct((s, D_MODEL), F32),
                   jax.ShapeDtypeStruct((s, LANES), F32)),
        grid=(s // tm,),
        in_specs=[pl.BlockSpec((tm, D_MODEL), lambda i: (i, 0)),
                  const((D_MODEL, D_MODEL)),
                  const((MEM_LEN, D_MODEL)),
                  const((MEM_LEN, D_MODEL)),
                  const((D_MODEL, D_MODEL)),
                  const((1, D_MODEL)),
                  const((1, D_MODEL)),
                  const((D_MODEL, LANES)),
                  const((1, LANES))],
        out_specs=(pl.BlockSpec((tm, D_MODEL), lambda i: (i, 0)),
                   pl.BlockSpec((tm, LANES), lambda i: (i, 0))),
        compiler_params=_cparams(("arbitrary",)),
        name="cross_attn_ln2",
    )(h1, wq_b, k, v, wo_b, g, b, w_route, b_route)


def _route_kernel(lg_ref, slot_ref, slot_t_ref, gate_ref, runs_ref, plan_ref, tot_row, tot_col, gstart, rinfo):
    phase = pl.program_id(0)
    i = pl.program_id(1)
    tm = lg_ref.shape[0]
    lane = lax.broadcasted_iota(jnp.int32, (tm, LANES), 1)
    rows = pl.ds(pl.multiple_of(i * tm, tm), tm)

    @pl.when(phase == 0)
    def _():
        neg = -jnp.inf
        lt = jnp.transpose(lg_ref[...])[0:ROUTE_ROWS, :]
        r = lax.broadcasted_iota(jnp.int32, (ROUTE_ROWS, tm), 0)
        gmask = r < N_GROUPS
        gl = jnp.where(gmask, lt, neg)
        ge = jnp.exp(gl - jnp.max(gl, axis=0, keepdims=True))
        pg = ge / jnp.sum(ge, axis=0, keepdims=True)
        pg_sel = jnp.max(pg, axis=0, keepdims=True)
        grp = jnp.min(jnp.where((pg == pg_sel) & gmask, r, ROUTE_ROWS), axis=0, keepdims=True)

        fr = r - N_GROUPS
        fmask = (fr >= 0) & (fr < N_EXPERTS) & ((fr >> 3) == grp)
        fl = jnp.where(fmask, lt, neg)
        fe = jnp.exp(fl - jnp.max(fl, axis=0, keepdims=True))
        fp = fe / jnp.sum(fe, axis=0, keepdims=True)
        p1 = jnp.max(fp, axis=0, keepdims=True)
        i1 = jnp.min(jnp.where((fp == p1) & fmask, r, ROUTE_ROWS), axis=0, keepdims=True)
        rest = fmask & (r != i1)
        fp2 = jnp.where(rest, fp, -1.0)
        p2 = jnp.max(fp2, axis=0, keepdims=True)
        i2 = jnp.min(jnp.where((fp2 == p2) & rest, r, ROUTE_ROWS), axis=0, keepdims=True)
        psum = p1 + p2
        picks = ((i1 - N_GROUPS).astype(F32), (i2 - N_GROUPS).astype(F32),
                 pg_sel * p1 / psum, pg_sel * p2 / psum)
        rr8 = lax.broadcasted_iota(jnp.int32, (LANES, tm), 0)
        info_t = jnp.zeros((LANES, tm), F32)
        for j, val in enumerate(picks):
            info_t = jnp.where(rr8 == j, val, info_t)
        rinfo[rows, :] = jnp.transpose(info_t)

    info = rinfo[rows, :]
    oh1 = lane == info[:, 0:1].astype(jnp.int32)
    oh2 = lane == info[:, 1:2].astype(jnp.int32)
    gate1 = info[:, 2:3]
    gate2 = info[:, 3:4]
    oh = (jnp.where(oh1, 1.0, 0.0) + jnp.where(oh2, 1.0, 0.0)).astype(BF16)
    ones = jnp.ones((tm, LANES), BF16)

    def align_up(v, a):
        return jnp.floor((v + (a - 1.0)) * (1.0 / a)) * a

    run_row = align_up(_dot_tn(ones, oh)[0:SUBLANES, :], RUN_ALIGN)
    rr = lax.broadcasted_iota(jnp.int32, (LANES, LANES), 0)
    cc = lax.broadcasted_iota(jnp.int32, (LANES, LANES), 1)
    strict_upper = jnp.where(rr < cc, 1.0, 0.0).astype(BF16)

    @pl.when((phase == 0) & (i == 0))
    def _():
        tot_row[...] = jnp.zeros_like(tot_row)
        tot_col[...] = jnp.zeros_like(tot_col)

    @pl.when(phase == 0)
    def _():
        tot_row[...] += run_row
        tot_col[...] += align_up(_dot_tn(oh, ones), RUN_ALIGN)

    @pl.when((phase == 1) & (i == 0))
    def _():
        nblk_row = align_up(tot_row[...], MOE_BLK) * (1.0 / MOE_BLK)
        nblk_col = align_up(tot_col[...], MOE_BLK) * (1.0 / MOE_BLK)
        lower_incl = jnp.where(cc <= rr, 1.0, 0.0).astype(BF16)
        base = _dot(nblk_row.astype(BF16), strict_upper) * float(MOE_BLK)
        ends = _dot(lower_incl, nblk_col.astype(BF16))
        expert_rows = rr < N_EXPERTS
        be = jnp.sum(jnp.where(expert_rows & (ends <= cc.astype(F32)), 1.0, 0.0), axis=0, keepdims=True)
        be = jnp.minimum(be, N_EXPERTS - 1.0)
        total = jnp.sum(jnp.where(expert_rows, nblk_col, 0.0), axis=0, keepdims=True)
        sub = lax.broadcasted_iota(jnp.int32, plan_ref.shape, 0)
        plan = jnp.where(sub == 0, jnp.broadcast_to(be, plan_ref.shape),
                         jnp.where(sub == 1, jnp.broadcast_to(total, plan_ref.shape),
                                   jnp.where(sub == 2, base + tot_row[...],
                                             jnp.where(sub == 3, nblk_row,
                                                       nblk_row * float(MOE_BLK) - tot_row[...]))))
        plan_ref[...] = plan.astype(jnp.int32)
        gstart[...] = base

    @pl.when(phase == 1)
    def _():
        off = _dot((run_row * (1.0 / RUN_ALIGN)).astype(BF16), strict_upper) * float(RUN_ALIGN)
        tr = lax.broadcasted_iota(jnp.int32, (tm, tm), 0)
        tc = lax.broadcasted_iota(jnp.int32, (tm, tm), 1)
        strict_lower = jnp.where(tc < tr, 1.0, 0.0).astype(BF16)
        slot = _dot(strict_lower, oh) + off[0:1, :]
        s1 = jnp.sum(jnp.where(oh1, slot, 0.0), axis=-1, keepdims=True)
        s2 = jnp.sum(jnp.where(oh2, slot, 0.0), axis=-1, keepdims=True)
        slots = jnp.where(lane == 0, s1, jnp.where(lane == 1, s2, 0.0))
        slot_ref[...] = slots.astype(jnp.int32)
        slot_t_ref[...] = jnp.transpose(slots)[0:SUBLANES, :].astype(jnp.int32)
        gate_ref[...] = jnp.where(lane == 0, gate1, jnp.where(lane == 1, gate2, 0.0))
        sub = lax.broadcasted_iota(jnp.int32, runs_ref.shape, 0)
        runs = jnp.where(sub == 0, gstart[...], jnp.where(sub == 1, off, run_row))
        runs_ref[...] = runs.astype(jnp.int32)
        gstart[...] += run_row


def _route(logits):
    t = logits.shape[0]
    tm = min(t, TILE_TOK)
    nt = t // tm
    step = lambda p, i: (i * p, 0)
    return pl.pallas_call(
        _route_kernel,
        out_shape=(jax.ShapeDtypeStruct((t, LANES), jnp.int32),
                   jax.ShapeDtypeStruct((nt * SUBLANES, tm), jnp.int32),
                   jax.ShapeDtypeStruct((t, LANES), F32),
                   jax.ShapeDtypeStruct((nt * SUBLANES, LANES), jnp.int32),
                   jax.ShapeDtypeStruct((SUBLANES, LANES), jnp.int32)),
        grid=(2, nt),
        in_specs=[pl.BlockSpec((tm, LANES), lambda p, i: (i, 0))],
        out_specs=(pl.BlockSpec((tm, LANES), step),
                   pl.BlockSpec((SUBLANES, tm), step),
                   pl.BlockSpec((tm, LANES), step),
                   pl.BlockSpec((SUBLANES, LANES), step),
                   pl.BlockSpec((SUBLANES, LANES), lambda p, i: (0, 0))),
        scratch_shapes=[pltpu.VMEM((SUBLANES, LANES), F32),
                        pltpu.VMEM((LANES, LANES), F32),
                        pltpu.VMEM((SUBLANES, LANES), F32),
                        pltpu.VMEM((t, LANES), F32)],
        compiler_params=_cparams(("arbitrary", "arbitrary")),
        name="moe_route",
    )(logits)


def _run_copies(rg_ref, ro_ref, rn_ref, tile, make_copy):
    def each(action):
        for e in range(N_EXPERTS):
            j = tile * N_EXPERTS + e
            n = rn_ref[j]

            @pl.when(n > 0)
            def _():
                action(make_copy(pl.multiple_of(rg_ref[j], RUN_ALIGN), pl.multiple_of(ro_ref[j], RUN_ALIGN),
                                 pl.multiple_of(n, RUN_ALIGN)))
    return each


def _dispatch_kernel(rg_ref, ro_ref, rn_ref, zrow_ref, zlen_ref, nu_ref, x_ref, slot_t_ref, xb_ref,
                     sorted_ref, zero_ref, sems, zsem):
    i = pl.program_id(0)
    n = pl.num_programs(0)
    tm = x_ref.shape[0]
    nb = xb_ref.shape[0] // MOE_BLK
    buf = i % 2

    def zero_copy(row, size):
        return pltpu.make_async_copy(
            zero_ref.at[pl.ds(0, size), :], xb_ref.at[pl.ds(pl.multiple_of(row, RUN_ALIGN), size), :], zsem)

    def pad_copy(e):
        return zero_copy(zrow_ref[e], pl.multiple_of(zlen_ref[e], RUN_ALIGN))

    def runs_of(tile, slot):
        def make_copy(g, o, size):
            return pltpu.make_async_copy(sorted_ref.at[slot, pl.ds(o, size), :],
                                         xb_ref.at[pl.ds(g, size), :], sems.at[slot])
        return _run_copies(rg_ref, ro_ref, rn_ref, tile, make_copy)

    @pl.when(i == 0)
    def _():
        zero_ref[...] = jnp.zeros_like(zero_ref)

        def issue_zero(e, carry):
            @pl.when(zlen_ref[e] > 0)
            def _():
                pad_copy(e).start()
            return carry
        lax.fori_loop(0, N_EXPERTS, issue_zero, 0)
        lax.fori_loop(nu_ref[0], nb, lambda b, c: (zero_copy(b * MOE_BLK, MOE_BLK).start(), c)[1], 0)

    @pl.when(i >= 2)
    def _():
        runs_of(i - 2, buf)(lambda c: c.wait())

    x = x_ref[...].astype(BF16)
    s1 = slot_t_ref[0:1, :]
    s2 = slot_t_ref[1:2, :]
    for rc in range(TILE_SLOTS // SORT_ROWS):
        r = lax.broadcasted_iota(jnp.int32, (SORT_ROWS, tm), 0) + rc * SORT_ROWS
        perm = jnp.where((r == s1) | (r == s2), 1.0, 0.0).astype(BF16)
        sorted_ref[buf, rc * SORT_ROWS:(rc + 1) * SORT_ROWS, :] = _dot(perm, x).astype(BF16)

    runs_of(i, buf)(lambda c: c.start())

    @pl.when(i == n - 1)
    def _():
        @pl.when(i >= 1)
        def _():
            runs_of(i - 1, 1 - buf)(lambda c: c.wait())
        runs_of(i, buf)(lambda c: c.wait())

        def wait_zero(e, carry):
            @pl.when(zlen_ref[e] > 0)
            def _():
                pad_copy(e).wait()
            return carry
        lax.fori_loop(0, N_EXPERTS, wait_zero, 0)
        lax.fori_loop(nu_ref[0], nb, lambda b, c: (zero_copy(0, MOE_BLK).wait(), c)[1], 0)


def _dispatch(rg, ro, rn, zrow, zlen, n_used, h2, slot_t, n_rows):
    t = h2.shape[0]
    tm = min(t, TILE_TOK)
    grid_spec = pltpu.PrefetchScalarGridSpec(
        num_scalar_prefetch=6,
        grid=(t // tm,),
        in_specs=[pl.BlockSpec((tm, D_MODEL), lambda i, *_: (i, 0)),
                  pl.BlockSpec((SUBLANES, tm), lambda i, *_: (i, 0))],
        out_specs=pl.BlockSpec(memory_space=pl.ANY),
        scratch_shapes=[pltpu.VMEM((2, TILE_SLOTS, D_MODEL), BF16),
                        pltpu.VMEM((MOE_BLK, D_MODEL), BF16),
                        pltpu.SemaphoreType.DMA((2,)),
                        pltpu.SemaphoreType.DMA(())],
    )
    return pl.pallas_call(
        _dispatch_kernel,
        out_shape=jax.ShapeDtypeStruct((n_rows, D_MODEL), BF16),
        grid_spec=grid_spec,
        compiler_params=_cparams(("arbitrary",)),
        name="moe_dispatch",
    )(rg, ro, rn, zrow, zlen, n_used, h2, slot_t)


def _expert_kernel(be_ref, nu_ref, nblk_ref, x_ref, wg_ref, wu_ref, wd_ref, y_ref,
                   wg_f, wu_f, wd_f, wg_b, wu_b, wd_b, sems, cur_ref):
    step = pl.program_id(0)
    n_used = nu_ref[0]

    def weight_copies(expert, slot):
        half = EXPERT_FF // 2
        return [pltpu.make_async_copy(wg_ref.at[expert], wg_f.at[slot], sems.at[slot]),
                pltpu.make_async_copy(wu_ref.at[expert], wu_f.at[slot], sems.at[slot]),
                pltpu.make_async_copy(wd_ref.at[expert, :half], wd_f.at[slot, :half], sems.at[slot]),
                pltpu.make_async_copy(wd_ref.at[expert, half:], wd_f.at[slot, half:], sems.at[slot])]

    def start_weights(expert, slot):
        for c in weight_copies(expert, slot):
            c.start()

    def after(blk):
        return blk + nblk_ref[be_ref[blk]]

    @pl.when(step == 0)
    def _():
        cur_ref[0] = 0
        start_weights(be_ref[0], 0)

        @pl.when(after(0) < n_used)
        def _():
            start_weights(be_ref[after(0)], 1)

    for sub in range(EXPERT_STEP_BLKS):
        b = step * EXPERT_STEP_BLKS + sub
        used = b < n_used
        e = be_ref[b]
        prev = be_ref[jnp.maximum(b - 1, 0)]
        fresh = used & ((b == 0) | (e != prev))
        rows = slice(sub * MOE_BLK, (sub + 1) * MOE_BLK)

        @pl.when(fresh)
        def _():
            slot = cur_ref[0]
            for c in weight_copies(e, slot):
                c.wait()
            wg_b[...] = wg_f[slot].astype(BF16)
            wu_b[...] = wu_f[slot].astype(BF16)
            wd_b[...] = wd_f[slot].astype(BF16)
            nxt = after(b)

            @pl.when(nxt < n_used)
            def _():
                nxt2 = after(nxt)

                @pl.when(nxt2 < n_used)
                def _():
                    start_weights(be_ref[nxt2], slot)
            cur_ref[0] = 1 - slot

        @pl.when(used)
        def _():
            x = x_ref[rows, :]
            hid = _silu(_dot(x, wg_b[...])) * _dot(x, wu_b[...])
            y_ref[rows, :] = _dot(hid.astype(BF16), wd_b[...]).astype(y_ref.dtype)

        @pl.when(jnp.logical_not(used))
        def _():
            y_ref[rows, :] = jnp.zeros((MOE_BLK, D_MODEL), y_ref.dtype)


def _experts(block_e, n_used, nblk, xb, w_gate, w_up, w_down):
    step_rows = EXPERT_STEP_BLKS * MOE_BLK
    any_space = pl.BlockSpec(memory_space=pl.ANY)
    last_step = lambda nu: (nu[0] - 1) // EXPERT_STEP_BLKS
    grid_spec = pltpu.PrefetchScalarGridSpec(
        num_scalar_prefetch=3,
        grid=(xb.shape[0] // step_rows,),
        in_specs=[pl.BlockSpec((step_rows, D_MODEL), lambda s, be, nu, nk: (jnp.minimum(s, last_step(nu)), 0)),
                  any_space, any_space, any_space],
        out_specs=pl.BlockSpec((step_rows, D_MODEL), lambda s, be, nu, nk: (s, 0)),
        scratch_shapes=[pltpu.VMEM((2, D_MODEL, EXPERT_FF), F32),
                        pltpu.VMEM((2, D_MODEL, EXPERT_FF), F32),
                        pltpu.VMEM((2, EXPERT_FF, D_MODEL), F32),
                        pltpu.VMEM((D_MODEL, EXPERT_FF), BF16),
                        pltpu.VMEM((D_MODEL, EXPERT_FF), BF16),
                        pltpu.VMEM((EXPERT_FF, D_MODEL), BF16),
                        pltpu.SemaphoreType.DMA((2,)),
                        pltpu.SMEM((1,), jnp.int32)],
    )
    return pl.pallas_call(
        _expert_kernel,
        out_shape=jax.ShapeDtypeStruct(xb.shape, BF16),
        grid_spec=grid_spec,
        compiler_params=_cparams(("arbitrary",)),
        name="moe_experts",
    )(block_e, n_used, nblk, xb, w_gate, w_up, w_down)


def _combine_kernel(rg_ref, ro_ref, rn_ref, yb_ref, h2_ref, slot_ref, gate_ref, g_ref, b_ref, o_ref,
                    ybuf, sems):
    i = pl.program_id(0)
    n = pl.num_programs(0)
    tm = h2_ref.shape[0]
    buf = i % 2

    def runs_of(tile, slot):
        def make_copy(g, o, size):
            return pltpu.make_async_copy(yb_ref.at[pl.ds(g, size), :],
                                         ybuf.at[slot, pl.ds(o, size), :], sems.at[slot])
        return _run_copies(rg_ref, ro_ref, rn_ref, tile, make_copy)

    @pl.when(i == 0)
    def _():
        ybuf[...] = jnp.zeros_like(ybuf)
        runs_of(0, 0)(lambda c: c.start())

    @pl.when(i + 1 < n)
    def _():
        runs_of(i + 1, 1 - buf)(lambda c: c.start())

    runs_of(i, buf)(lambda c: c.wait())

    lane = lax.broadcasted_iota(jnp.int32, (tm, TILE_SLOTS), 1)
    slots = slot_ref[...]
    gate = gate_ref[...]
    y = ybuf[buf]
    sel = jnp.where(lane == slots[:, 0:1], gate[:, 0:1],
                    jnp.where(lane == slots[:, 1:2], gate[:, 1:2], 0.0)).astype(BF16)
    parts = 2 if tm % 512 == 0 else 1
    rows = [slice(p * (tm // parts), (p + 1) * (tm // parts)) for p in range(parts)]
    ffn = [_dot(sel[r, :], y) for r in rows]
    for r, f in zip(rows, ffn):
        o_ref[r, :] = _layer_norm(DEEPNORM_ALPHA * h2_ref[r, :] + f, g_ref[...], b_ref[...])


def _combine_ln(rg, ro, rn, yb, h2, slots, gate, g, b):
    t = h2.shape[0]
    tm = min(t, TILE_TOK)
    grid_spec = pltpu.PrefetchScalarGridSpec(
        num_scalar_prefetch=3,
        grid=(t // tm,),
        in_specs=[pl.BlockSpec(memory_space=pl.ANY),
                  pl.BlockSpec((tm, D_MODEL), lambda i, *_: (i, 0)),
                  pl.BlockSpec((tm, LANES), lambda i, *_: (i, 0)),
                  pl.BlockSpec((tm, LANES), lambda i, *_: (i, 0)),
                  pl.BlockSpec((1, D_MODEL), lambda i, *_: (0, 0)),
                  pl.BlockSpec((1, D_MODEL), lambda i, *_: (0, 0))],
        out_specs=pl.BlockSpec((tm, D_MODEL), lambda i, *_: (i, 0)),
        scratch_shapes=[pltpu.VMEM((2, TILE_SLOTS, D_MODEL), BF16),
                        pltpu.SemaphoreType.DMA((2,))],
    )
    return pl.pallas_call(
        _combine_kernel,
        out_shape=jax.ShapeDtypeStruct((t, D_MODEL), F32),
        grid_spec=grid_spec,
        compiler_params=_cparams(("arbitrary",)),
        name="moe_combine_ln3",
    )(rg, ro, rn, yb, h2, slots, gate, g, b)


def _mixer(x2d, positions, w_in, w_gla_a2, b_gla_a, g_gla_norm):
    s = x2d.shape[0]
    half = RET_DK // 2
    inv_freq = (ROPE_BASE ** (-jnp.arange(half, dtype=F32) / half)).reshape(1, half)
    cos, sin = _rope_table(positions.reshape(s, 1), inv_freq)
    w_in_t = jnp.swapaxes(w_in, 0, 1)
    w_lr_t = jnp.pad(w_in_t[GLR_OFF:].astype(BF16), ((0, LANES - GLA_LOWRANK), (0, 0)))
    w_a2 = jnp.pad(w_gla_a2.astype(BF16), ((0, LANES - GLA_LOWRANK), (0, 0)))
    h_main, log_a = _proj_in(x2d, w_in_t, w_lr_t, w_a2, b_gla_a.reshape(1, -1))
    log_gamma = jnp.log1p(-jnp.exp2(-5.0 - jnp.arange(RET_HEADS, dtype=F32)))
    ret = _retention(h_main, cos, sin, log_gamma)
    gla = _gla(h_main, log_a, g_gla_norm.reshape(1, -1))
    return ret, gla


def _moe(h2, logits, w_gate, w_up, w_down, g, b):
    t = h2.shape[0]
    nt = t // min(t, TILE_TOK)
    slots, slot_t, gate, runs, plan = _route(logits)
    runs = runs.reshape(nt, SUBLANES, LANES)[:, :3, :N_EXPERTS]
    rg, ro, rn = (runs[:, j, :].reshape(-1) for j in range(3))
    max_rows = 2 * t + nt * N_EXPERTS * (RUN_ALIGN - 1) + N_EXPERTS * (MOE_BLK - 1)
    nb = -(-max_rows // (MOE_BLK * EXPERT_STEP_BLKS)) * EXPERT_STEP_BLKS
    block_e, n_used = plan[0, :nb], plan[1, :1]
    pad_row, nblk, pad_len = (plan[j, :N_EXPERTS] for j in (2, 3, 4))
    xb = _dispatch(rg, ro, rn, pad_row, pad_len, n_used, h2, slot_t, nb * MOE_BLK)
    yb = _experts(block_e, n_used, nblk, xb, w_gate, w_up, w_down)
    return _combine_ln(rg, ro, rn, yb, h2, slots, gate, g, b)


def kernel(x, mem, positions, w_in, w_gla_a2, b_gla_a, g_gla_norm, w_mix_out, ln1_g, ln1_b, w_mq, w_mk, w_mv, w_mo, ln2_g, ln2_b, w_route_group, b_route_group, w_route_expert, b_route_expert, w_exp_gate, w_exp_up, w_exp_down, ln3_g, ln3_b):
    bsz, s, d = x.shape
    assert bsz == 1 and d == D_MODEL
    x2d = x.reshape(s, d)
    row = lambda v: v.reshape(1, -1)

    ret, gla = _mixer(x2d, positions, w_in[0], w_gla_a2[0], b_gla_a[0], g_gla_norm[0])
    h1 = _mixout_ln(ret, gla, x2d, w_mix_out[0], row(ln1_g[0]), row(ln1_b[0]))

    k, v = _mem_kv(mem[0], w_mk[0], w_mv[0])
    n_route = N_GROUPS + N_EXPERTS
    w_route = jnp.pad(jnp.concatenate([w_route_group[0], w_route_expert[0]], axis=1).astype(BF16),
                      ((0, 0), (0, LANES - n_route)))
    b_route = jnp.pad(jnp.concatenate([b_route_group[0], b_route_expert[0].reshape(-1)]),
                      (0, LANES - n_route)).reshape(1, LANES)
    h2, logits = _cross_attention(h1, w_mq[0].astype(BF16), k, v, w_mo[0].astype(BF16),
                                  row(ln2_g[0]), row(ln2_b[0]), w_route, b_route)

    out = _moe(h2, logits, w_exp_gate[0], w_exp_up[0], w_exp_down[0],
               row(ln3_g[0]), row(ln3_b[0]))
    return out.reshape(bsz, s, d)
```
